```python
import jax, jax.numpy as jnp
from jax import lax
import numpy as np

D_MODEL = 1024
BATCH = 8
SEQ = 4096
DEPTH = 4

MEM_LEN = 256
HEAD_DIM = 64
N_SB_HEADS = 8
N_FOX_HEADS = 8
N_MEM_HEADS = 4
MEM_HEAD_DIM = 128
SB_W = N_SB_HEADS * HEAD_DIM
FOX_W = N_FOX_HEADS * HEAD_DIM
MEM_W = N_MEM_HEADS * MEM_HEAD_DIM
N_BRANCH = 3
IN_W = 3 * SB_W + 3 * FOX_W + N_FOX_HEADS + MEM_W
D_FF = ((8 * D_MODEL // 3 + 127) // 128) * 128
Q_BLOCK = 128
RMS_EPS = 1e-6

kernel_name = 'hybrid_sb_fox_mem_macaron'


def _rmsnorm(t, g):
    t32 = t.astype(jnp.float32)
    t32 = t32 * lax.rsqrt(jnp.mean(t32 * t32, axis=-1, keepdims=True) + RMS_EPS)
    return t32.astype(t.dtype) * g


def _swiglu(t, w_gate, w_up, w_down):
    return (jax.nn.silu(t @ w_gate) * (t @ w_up)) @ w_down


def _split_heads(t, n_heads):
    b, s, _ = t.shape
    return t.reshape(b, s, n_heads, -1).transpose(0, 2, 1, 3)


def _merge_heads(t):
    b, h, s, d = t.shape
    return t.transpose(0, 2, 1, 3).reshape(b, s, h * d)


def _query_blocks(t):
    b, h, s = t.shape[:3]
    t = t.reshape((b, h, s // Q_BLOCK, Q_BLOCK) + t.shape[3:])
    return jnp.moveaxis(t, 2, 0)


def _unblock(o):
    nb, b, h, blk, d = o.shape
    return jnp.moveaxis(o, 0, 2).reshape(b, h, nb * blk, d)


def _stick_breaking_attention(q, k, v):
    b, h, s_len, d = q.shape
    scale = d ** -0.5
    key_pos = jnp.arange(s_len)

    def block(args):
        qb, i = args
        z = jnp.einsum('bhqd,bhkd->bhqk', qb, k).astype(jnp.float32) * scale
        q_pos = i * Q_BLOCK + jnp.arange(Q_BLOCK)
        mask = key_pos[None, :] < q_pos[:, None]
        log_beta = jax.nn.log_sigmoid(z)
        log_not = jnp.where(mask, log_beta - z, 0.0)
        log_between = lax.cumsum(log_not, axis=3, reverse=True) - log_not
        w = jnp.where(mask, jnp.exp(log_beta + log_between), 0.0)
        return jnp.einsum('bhqk,bhkd->bhqd', w.astype(v.dtype), v)

    out = lax.map(block, (_query_blocks(q), jnp.arange(s_len // Q_BLOCK)))
    return _unblock(out)


def _forgetting_attention(q, k, v, log_f):
    b, h, s_len, d = q.shape
    scale = d ** -0.5
    key_pos = jnp.arange(s_len)
    c = lax.cumsum(log_f.astype(jnp.float32), axis=2)
    neg = jnp.finfo(jnp.float32).min

    def block(args):
        qb, cb, i = args
        z = jnp.einsum('bhqd,bhkd->bhqk', qb, k).astype(jnp.float32) * scale
        z = z + cb[..., :, None] - c[..., None, :]
        q_pos = i * Q_BLOCK + jnp.arange(Q_BLOCK)
        mask = key_pos[None, :] <= q_pos[:, None]
        p = jax.nn.softmax(jnp.where(mask, z, neg), axis=-1)
        return jnp.einsum('bhqk,bhkd->bhqd', p.astype(v.dtype), v)

    out = lax.map(block, (_query_blocks(q), _query_blocks(c), jnp.arange(s_len // Q_BLOCK)))
    return _unblock(out)


def _memory_attention(q, k, v):
    z = jnp.einsum('bhqd,bhkd->bhqk', q, k).astype(jnp.float32) * (q.shape[-1] ** -0.5)
    p = jax.nn.softmax(z, axis=-1)
    return jnp.einsum('bhqk,bhkd->bhqd', p.astype(v.dtype), v)


def _fwd_setup_inputs(seed: int = 0) -> dict:
    key = jax.random.key(seed)
    ks = jax.random.split(key, 32)
    L = DEPTH

    def w(k, shape, fan_in):
        return jax.random.normal(k, shape, jnp.float32) * (fan_in ** -0.5)

    def gain(k, shape):
        return 1.0 + 0.05 * jax.random.normal(k, shape, jnp.float32)

    return {
        'x': jax.random.normal(ks[0], (BATCH, SEQ, D_MODEL), jnp.float32),
        'mem': jax.random.normal(ks[1], (BATCH, MEM_LEN, D_MODEL), jnp.float32),
        'ffn1_pre_g': gain(ks[2], (L, D_MODEL)),
        'ffn1_post_g': gain(ks[3], (L, D_MODEL)),
        'ffn1_w_gate': w(ks[4], (L, D_MODEL, D_FF), D_MODEL),
        'ffn1_w_up': w(ks[5], (L, D_MODEL, D_FF), D_MODEL),
        'ffn1_w_down': w(ks[6], (L, D_FF, D_MODEL), D_FF),
        'mix_pre_g': gain(ks[7], (L, D_MODEL)),
        'mix_post_g': gain(ks[8], (L, D_MODEL)),
        'w_in': w(ks[9], (L, D_MODEL, IN_W), D_MODEL),
        'b_forget': 2.0 + 0.5 * jax.random.normal(ks[10], (L, N_FOX_HEADS), jnp.float32),
        'mem_norm_g': gain(ks[11], (D_MODEL,)),
        'w_mem_kv': w(ks[12], (L, D_MODEL, 2 * MEM_W), D_MODEL),
        'w_gate': w(ks[13], (L, D_MODEL, N_BRANCH * D_MODEL), D_MODEL),
        'b_gate': 0.02 * jax.random.normal(ks[14], (L, N_BRANCH * D_MODEL), jnp.float32),
        'w_br_sb': w(ks[15], (L, SB_W, D_MODEL), SB_W),
        'w_br_fox': w(ks[16], (L, FOX_W, D_MODEL), FOX_W),
        'w_br_mem': w(ks[17], (L, MEM_W, D_MODEL), MEM_W),
        'w_out': w(ks[18], (L, D_MODEL, D_MODEL), D_MODEL),
        'ffn2_pre_g': gain(ks[19], (L, D_MODEL)),
        'ffn2_post_g': gain(ks[20], (L, D_MODEL)),
        'ffn2_w_gate': w(ks[21], (L, D_MODEL, D_FF), D_MODEL),
        'ffn2_w_up': w(ks[22], (L, D_MODEL, D_FF), D_MODEL),
        'ffn2_w_down': w(ks[23], (L, D_FF, D_MODEL), D_FF),
    }


def _fwd_reference(x, mem, ffn1_pre_g, ffn1_post_g, ffn1_w_gate, ffn1_w_up, ffn1_w_down,
              mix_pre_g, mix_post_g, w_in, b_forget, mem_norm_g, w_mem_kv, w_gate, b_gate,
              w_br_sb, w_br_fox, w_br_mem, w_out,
              ffn2_pre_g, ffn2_post_g, ffn2_w_gate, ffn2_w_up, ffn2_w_down):
    mem_n = _rmsnorm(mem, mem_norm_g)
    split_at = np.cumsum([SB_W, SB_W, SB_W, FOX_W, FOX_W, FOX_W, N_FOX_HEADS])
    h = x
    for l in range(DEPTH):
        f = _swiglu(_rmsnorm(h, ffn1_pre_g[l]), ffn1_w_gate[l], ffn1_w_up[l], ffn1_w_down[l])
        h = h + 0.5 * _rmsnorm(f, ffn1_post_g[l])

        u = _rmsnorm(h, mix_pre_g[l])
        proj = u @ w_in[l]
        q_sb, k_sb, v_sb, q_fx, k_fx, v_fx, f_logit, q_mem = jnp.split(proj, split_at, axis=-1)

        o_sb = _stick_breaking_attention(_split_heads(q_sb, N_SB_HEADS), _split_heads(k_sb, N_SB_HEADS),
                                         _split_heads(v_sb, N_SB_HEADS))
        log_f = jax.nn.log_sigmoid((f_logit + b_forget[l]).astype(jnp.float32)).transpose(0, 2, 1)
        o_fx = _forgetting_attention(_split_heads(q_fx, N_FOX_HEADS), _split_heads(k_fx, N_FOX_HEADS),
                                     _split_heads(v_fx, N_FOX_HEADS), log_f)
        k_mem, v_mem = jnp.split(mem_n @ w_mem_kv[l], 2, axis=-1)
        o_mem = _memory_attention(_split_heads(q_mem, N_MEM_HEADS), _split_heads(k_mem, N_MEM_HEADS),
                                  _split_heads(v_mem, N_MEM_HEADS))

        g_sb, g_fx, g_mem = jnp.split(jax.nn.sigmoid(u @ w_gate[l] + b_gate[l]), N_BRANCH, axis=-1)
        merged = (g_sb * (_merge_heads(o_sb) @ w_br_sb[l])
                  + g_fx * (_merge_heads(o_fx) @ w_br_fox[l])
                  + g_mem * (_merge_heads(o_mem) @ w_br_mem[l]))
        h = h + _rmsnorm(merged @ w_out[l], mix_post_g[l])

        f = _swiglu(_rmsnorm(h, ffn2_pre_g[l]), ffn2_w_gate[l], ffn2_w_up[l], ffn2_w_down[l])
        h = h + 0.5 * _rmsnorm(f, ffn2_post_g[l])
    return h


import jax as _jax
import jax.numpy as _jnp

TWIN_FORMAT = 'train_step'
FWD_PARAMS = ['x', 'mem', 'ffn1_pre_g', 'ffn1_post_g', 'ffn1_w_gate', 'ffn1_w_up', 'ffn1_w_down', 'mix_pre_g', 'mix_post_g', 'w_in', 'b_forget', 'mem_norm_g', 'w_mem_kv', 'w_gate', 'b_gate', 'w_br_sb', 'w_br_fox', 'w_br_mem', 'w_out', 'ffn2_pre_g', 'ffn2_post_g', 'ffn2_w_gate', 'ffn2_w_up', 'ffn2_w_down']
TWIN_WEIGHTS = ['ffn1_pre_g', 'ffn1_post_g', 'ffn1_w_gate', 'ffn1_w_up', 'ffn1_w_down', 'mix_pre_g', 'mix_post_g', 'w_in', 'b_forget', 'mem_norm_g', 'w_mem_kv', 'w_gate', 'b_gate', 'w_br_sb', 'w_br_fox', 'w_br_mem', 'w_out', 'ffn2_pre_g', 'ffn2_post_g', 'ffn2_w_gate', 'ffn2_w_up', 'ffn2_w_down']
TWIN_DIFF_INPUT = 'x'
TWIN_INPUTS = ['x', 'mem', 'ffn1_pre_g', 'ffn1_post_g', 'ffn1_w_gate', 'ffn1_w_up', 'ffn1_w_down', 'mix_pre_g', 'mix_post_g', 'w_in', 'b_forget', 'mem_norm_g', 'w_mem_kv', 'w_gate', 'b_gate', 'w_br_sb', 'w_br_fox', 'w_br_mem', 'w_out', 'ffn2_pre_g', 'ffn2_post_g', 'ffn2_w_gate', 'ffn2_w_up', 'ffn2_w_down', 'loss_target', 'm_ffn1_pre_g', 'm_ffn1_post_g', 'm_ffn1_w_gate', 'm_ffn1_w_up', 'm_ffn1_w_down', 'm_mix_pre_g', 'm_mix_post_g', 'm_w_in', 'm_b_forget', 'm_mem_norm_g', 'm_w_mem_kv', 'm_w_gate', 'm_b_gate', 'm_w_br_sb', 'm_w_br_fox', 'm_w_br_mem', 'm_w_out', 'm_ffn2_pre_g', 'm_ffn2_post_g', 'm_ffn2_w_gate', 'm_ffn2_w_up', 'm_ffn2_w_down', 'v_ffn1_pre_g', 'v_ffn1_post_g', 'v_ffn1_w_gate', 'v_ffn1_w_up', 'v_ffn1_w_down', 'v_mix_pre_g', 'v_mix_post_g', 'v_w_in', 'v_b_forget', 'v_mem_norm_g', 'v_w_mem_kv', 'v_w_gate', 'v_b_gate', 'v_w_br_sb', 'v_w_br_fox', 'v_w_br_mem', 'v_w_out', 'v_ffn2_pre_g', 'v_ffn2_post_g', 'v_ffn2_w_gate', 'v_ffn2_w_up', 'v_ffn2_w_down']
TWIN_OUTPUTS = ['loss', 'grad_x', 'grad_ffn1_pre_g', 'grad_ffn1_post_g', 'grad_ffn1_w_gate', 'grad_ffn1_w_up', 'grad_ffn1_w_down', 'grad_mix_pre_g', 'grad_mix_post_g', 'grad_w_in', 'grad_b_forget', 'grad_mem_norm_g', 'grad_w_mem_kv', 'grad_w_gate', 'grad_b_gate', 'grad_w_br_sb', 'grad_w_br_fox', 'grad_w_br_mem', 'grad_w_out', 'grad_ffn2_pre_g', 'grad_ffn2_post_g', 'grad_ffn2_w_gate', 'grad_ffn2_w_up', 'grad_ffn2_w_down', 'delta_ffn1_pre_g', 'delta_ffn1_post_g', 'delta_ffn1_w_gate', 'delta_ffn1_w_up', 'delta_ffn1_w_down', 'delta_mix_pre_g', 'delta_mix_post_g', 'delta_w_in', 'delta_b_forget', 'delta_mem_norm_g', 'delta_w_mem_kv', 'delta_w_gate', 'delta_b_gate', 'delta_w_br_sb', 'delta_w_br_fox', 'delta_w_br_mem', 'delta_w_out', 'delta_ffn2_pre_g', 'delta_ffn2_post_g', 'delta_ffn2_w_gate', 'delta_ffn2_w_up', 'delta_ffn2_w_down', 'new_m_ffn1_pre_g', 'new_m_ffn1_post_g', 'new_m_ffn1_w_gate', 'new_m_ffn1_w_up', 'new_m_ffn1_w_down', 'new_m_mix_pre_g', 'new_m_mix_post_g', 'new_m_w_in', 'new_m_b_forget', 'new_m_mem_norm_g', 'new_m_w_mem_kv', 'new_m_w_gate', 'new_m_b_gate', 'new_m_w_br_sb', 'new_m_w_br_fox', 'new_m_w_br_mem', 'new_m_w_out', 'new_m_ffn2_pre_g', 'new_m_ffn2_post_g', 'new_m_ffn2_w_gate', 'new_m_ffn2_w_up', 'new_m_ffn2_w_down', 'new_v_ffn1_pre_g', 'new_v_ffn1_post_g', 'new_v_ffn1_w_gate', 'new_v_ffn1_w_up', 'new_v_ffn1_w_down', 'new_v_mix_pre_g', 'new_v_mix_post_g', 'new_v_w_in', 'new_v_b_forget', 'new_v_mem_norm_g', 'new_v_w_mem_kv', 'new_v_w_gate', 'new_v_b_gate', 'new_v_w_br_sb', 'new_v_w_br_fox', 'new_v_w_br_mem', 'new_v_w_out', 'new_v_ffn2_pre_g', 'new_v_ffn2_post_g', 'new_v_ffn2_w_gate', 'new_v_ffn2_w_up', 'new_v_ffn2_w_down']
TWIN_LEAF_KINDS = {'loss': 'loss', 'grad_x': 'grad_x', 'grad_ffn1_pre_g': 'grad_w', 'grad_ffn1_post_g': 'grad_w', 'grad_ffn1_w_gate': 'grad_w', 'grad_ffn1_w_up': 'grad_w', 'grad_ffn1_w_down': 'grad_w', 'grad_mix_pre_g': 'grad_w', 'grad_mix_post_g': 'grad_w', 'grad_w_in': 'grad_w', 'grad_b_forget': 'grad_w', 'grad_mem_norm_g': 'grad_w', 'grad_w_mem_kv': 'grad_w', 'grad_w_gate': 'grad_w', 'grad_b_gate': 'grad_w', 'grad_w_br_sb': 'grad_w', 'grad_w_br_fox': 'grad_w', 'grad_w_br_mem': 'grad_w', 'grad_w_out': 'grad_w', 'grad_ffn2_pre_g': 'grad_w', 'grad_ffn2_post_g': 'grad_w', 'grad_ffn2_w_gate': 'grad_w', 'grad_ffn2_w_up': 'grad_w', 'grad_ffn2_w_down': 'grad_w', 'delta_ffn1_pre_g': 'delta_w', 'delta_ffn1_post_g': 'delta_w', 'delta_ffn1_w_gate': 'delta_w', 'delta_ffn1_w_up': 'delta_w', 'delta_ffn1_w_down': 'delta_w', 'delta_mix_pre_g': 'delta_w', 'delta_mix_post_g': 'delta_w', 'delta_w_in': 'delta_w', 'delta_b_forget': 'delta_w', 'delta_mem_norm_g': 'delta_w', 'delta_w_mem_kv': 'delta_w', 'delta_w_gate': 'delta_w', 'delta_b_gate': 'delta_w', 'delta_w_br_sb': 'delta_w', 'delta_w_br_fox': 'delta_w', 'delta_w_br_mem': 'delta_w', 'delta_w_out': 'delta_w', 'delta_ffn2_pre_g': 'delta_w', 'delta_ffn2_post_g': 'delta_w', 'delta_ffn2_w_gate': 'delta_w', 'delta_ffn2_w_up': 'delta_w', 'delta_ffn2_w_down': 'delta_w', 'new_m_ffn1_pre_g': 'new_m', 'new_m_ffn1_post_g': 'new_m', 'new_m_ffn1_w_gate': 'new_m', 'new_m_ffn1_w_up': 'new_m', 'new_m_ffn1_w_down': 'new_m', 'new_m_mix_pre_g': 'new_m', 'new_m_mix_post_g': 'new_m', 'new_m_w_in': 'new_m', 'new_m_b_forget': 'new_m', 'new_m_mem_norm_g': 'new_m', 'new_m_w_mem_kv': 'new_m', 'new_m_w_gate': 'new_m', 'new_m_b_gate': 'new_m', 'new_m_w_br_sb': 'new_m', 'new_m_w_br_fox': 'new_m', 'new_m_w_br_mem': 'new_m', 'new_m_w_out': 'new_m', 'new_m_ffn2_pre_g': 'new_m', 'new_m_ffn2_post_g': 'new_m', 'new_m_ffn2_w_gate': 'new_m', 'new_m_ffn2_w_up': 'new_m', 'new_m_ffn2_w_down': 'new_m', 'new_v_ffn1_pre_g': 'new_v', 'new_v_ffn1_post_g': 'new_v', 'new_v_ffn1_w_gate': 'new_v', 'new_v_ffn1_w_up': 'new_v', 'new_v_ffn1_w_down': 'new_v', 'new_v_mix_pre_g': 'new_v', 'new_v_mix_post_g': 'new_v', 'new_v_w_in': 'new_v', 'new_v_b_forget': 'new_v', 'new_v_mem_norm_g': 'new_v', 'new_v_w_mem_kv': 'new_v', 'new_v_w_gate': 'new_v', 'new_v_b_gate': 'new_v', 'new_v_w_br_sb': 'new_v', 'new_v_w_br_fox': 'new_v', 'new_v_w_br_mem': 'new_v', 'new_v_w_out': 'new_v', 'new_v_ffn2_pre_g': 'new_v', 'new_v_ffn2_post_g': 'new_v', 'new_v_ffn2_w_gate': 'new_v', 'new_v_ffn2_w_up': 'new_v', 'new_v_ffn2_w_down': 'new_v'}


def _forward(args):
    return _fwd_reference(*[args[k] for k in FWD_PARAMS])


def _output_shape():
    def fwd():
        inp = _fwd_setup_inputs(0)
        return _fwd_reference(*[inp[k] for k in FWD_PARAMS])
    out = _jax.eval_shape(fwd)
    return out.shape, out.dtype

N_MICROBATCH = 1
ADAM_LR = 0.001
ADAM_B1 = 0.9
ADAM_B2 = 0.999
ADAM_EPS = 1e-08
ADAM_WD = 0.01
ADAM_STEP = 10
PER_EXAMPLE_BATCH_AXIS = {'x': 0, 'mem': 0, 'loss_target': 0}
SHARED_INPUTS = []
_WEIGHT_DTYPES = {'ffn1_pre_g': _jnp.float32, 'ffn1_post_g': _jnp.float32, 'ffn1_w_gate': _jnp.float32, 'ffn1_w_up': _jnp.float32, 'ffn1_w_down': _jnp.float32, 'mix_pre_g': _jnp.float32, 'mix_post_g': _jnp.float32, 'w_in': _jnp.float32, 'b_forget': _jnp.float32, 'mem_norm_g': _jnp.float32, 'w_mem_kv': _jnp.float32, 'w_gate': _jnp.float32, 'b_gate': _jnp.float32, 'w_br_sb': _jnp.float32, 'w_br_fox': _jnp.float32, 'w_br_mem': _jnp.float32, 'w_out': _jnp.float32, 'ffn2_pre_g': _jnp.float32, 'ffn2_post_g': _jnp.float32, 'ffn2_w_gate': _jnp.float32, 'ffn2_w_up': _jnp.float32, 'ffn2_w_down': _jnp.float32}
MOMENT_SCALE = {'ffn1_pre_g': 9.789343e-01, 'ffn1_post_g': 7.825879e+00, 'ffn1_w_gate': 4.008728e-01, 'ffn1_w_up': 4.268358e-01, 'ffn1_w_down': 7.104744e-01, 'mix_pre_g': 1.654769e+00, 'mix_post_g': 3.203295e+01, 'w_in': 8.308592e-01, 'b_forget': 3.941375e+00, 'mem_norm_g': 5.970931e-01, 'w_mem_kv': 2.702442e-01, 'w_gate': 1.985371e-01, 'b_gate': 3.812646e-01, 'w_br_sb': 1.197016e+00, 'w_br_fox': 8.818056e-01, 'w_br_mem': 2.416494e-01, 'w_out': 1.575826e+00, 'ffn2_pre_g': 6.668255e-01, 'ffn2_post_g': 7.991819e+00, 'ffn2_w_gate': 2.589493e-01, 'ffn2_w_up': 3.267641e-01, 'ffn2_w_down': 5.427795e-01}


def _to_microbatches(a, axis):
    t = _jnp.moveaxis(a, axis, 0)
    t = t.reshape((N_MICROBATCH, t.shape[0] // N_MICROBATCH) + t.shape[1:])
    return _jnp.moveaxis(t, 1, axis + 1)


def setup_inputs(seed: int = 0) -> dict:
    inp = _fwd_setup_inputs(seed)
    key = _jax.random.fold_in(_jax.random.key(seed), 7919)
    shape, _ = _output_shape()
    out = dict(inp)
    out["loss_target"] = _jax.random.normal(_jax.random.fold_in(key, 0), shape, _jnp.float32)
    for i, name in enumerate(TWIN_WEIGHTS):
        w = inp[name].astype(_jnp.float32)
        if MOMENT_SCALE is None:
            s = _jnp.sqrt(_jnp.mean(_jnp.square(w)) + 1e-30)
        else:
            s = MOMENT_SCALE[name]
        km, kv = _jax.random.split(_jax.random.fold_in(key, i + 1))
        out[name] = w
        out["m_" + name] = s * _jax.random.normal(km, w.shape, _jnp.float32)
        out["v_" + name] = (s * s) * _jax.random.uniform(kv, w.shape, _jnp.float32, 0.5, 1.5)
    if N_MICROBATCH > 1:
        for name, axis in PER_EXAMPLE_BATCH_AXIS.items():
            out[name] = _to_microbatches(out[name], axis)
    return {'x': out['x'], 'mem': out['mem'], 'ffn1_pre_g': out['ffn1_pre_g'], 'ffn1_post_g': out['ffn1_post_g'], 'ffn1_w_gate': out['ffn1_w_gate'], 'ffn1_w_up': out['ffn1_w_up'], 'ffn1_w_down': out['ffn1_w_down'], 'mix_pre_g': out['mix_pre_g'], 'mix_post_g': out['mix_post_g'], 'w_in': out['w_in'], 'b_forget': out['b_forget'], 'mem_norm_g': out['mem_norm_g'], 'w_mem_kv': out['w_mem_kv'], 'w_gate': out['w_gate'], 'b_gate': out['b_gate'], 'w_br_sb': out['w_br_sb'], 'w_br_fox': out['w_br_fox'], 'w_br_mem': out['w_br_mem'], 'w_out': out['w_out'], 'ffn2_pre_g': out['ffn2_pre_g'], 'ffn2_post_g': out['ffn2_post_g'], 'ffn2_w_gate': out['ffn2_w_gate'], 'ffn2_w_up': out['ffn2_w_up'], 'ffn2_w_down': out['ffn2_w_down'], 'loss_target': out['loss_target'], 'm_ffn1_pre_g': out['m_ffn1_pre_g'], 'm_ffn1_post_g': out['m_ffn1_post_g'], 'm_ffn1_w_gate': out['m_ffn1_w_gate'], 'm_ffn1_w_up': out['m_ffn1_w_up'], 'm_ffn1_w_down': out['m_ffn1_w_down'], 'm_mix_pre_g': out['m_mix_pre_g'], 'm_mix_post_g': out['m_mix_post_g'], 'm_w_in': out['m_w_in'], 'm_b_forget': out['m_b_forget'], 'm_mem_norm_g': out['m_mem_norm_g'], 'm_w_mem_kv': out['m_w_mem_kv'], 'm_w_gate': out['m_w_gate'], 'm_b_gate': out['m_b_gate'], 'm_w_br_sb': out['m_w_br_sb'], 'm_w_br_fox': out['m_w_br_fox'], 'm_w_br_mem': out['m_w_br_mem'], 'm_w_out': out['m_w_out'], 'm_ffn2_pre_g': out['m_ffn2_pre_g'], 'm_ffn2_post_g': out['m_ffn2_post_g'], 'm_ffn2_w_gate': out['m_ffn2_w_gate'], 'm_ffn2_w_up': out['m_ffn2_w_up'], 'm_ffn2_w_down': out['m_ffn2_w_down'], 'v_ffn1_pre_g': out['v_ffn1_pre_g'], 'v_ffn1_post_g': out['v_ffn1_post_g'], 'v_ffn1_w_gate': out['v_ffn1_w_gate'], 'v_ffn1_w_up': out['v_ffn1_w_up'], 'v_ffn1_w_down': out['v_ffn1_w_down'], 'v_mix_pre_g': out['v_mix_pre_g'], 'v_mix_post_g': out['v_mix_post_g'], 'v_w_in': out['v_w_in'], 'v_b_forget': out['v_b_forget'], 'v_mem_norm_g': out['v_mem_norm_g'], 'v_w_mem_kv': out['v_w_mem_kv'], 'v_w_gate': out['v_w_gate'], 'v_b_gate': out['v_b_gate'], 'v_w_br_sb': out['v_w_br_sb'], 'v_w_br_fox': out['v_w_br_fox'], 'v_w_br_mem': out['v_w_br_mem'], 'v_w_out': out['v_w_out'], 'v_ffn2_pre_g': out['v_ffn2_pre_g'], 'v_ffn2_post_g': out['v_ffn2_post_g'], 'v_ffn2_w_gate': out['v_ffn2_w_gate'], 'v_ffn2_w_up': out['v_ffn2_w_up'], 'v_ffn2_w_down': out['v_ffn2_w_down']}


def _loss(weights, diff, rest, loss_target):
    with _jax.named_scope("forward"):
        args = {**rest, TWIN_DIFF_INPUT: diff, **{k: w.astype(_WEIGHT_DTYPES[k]) for k, w in weights.items()}}
        y = _forward(args)
    with _jax.named_scope("loss_head"):
        err = _jnp.square(y.astype(_jnp.float32) - loss_target)
        return 0.5 * _jnp.sum(_jnp.mean(err, axis=-1)) if err.ndim else 0.5 * err


def _adamw(w, g, m, v):
    m = ADAM_B1 * m + (1.0 - ADAM_B1) * g
    v = ADAM_B2 * v + (1.0 - ADAM_B2) * _jnp.square(g)
    m_hat = m / (1.0 - ADAM_B1 ** ADAM_STEP)
    v_hat = v / (1.0 - ADAM_B2 ** ADAM_STEP)
    delta = -ADAM_LR * (m_hat / (_jnp.sqrt(v_hat) + ADAM_EPS) + ADAM_WD * w)
    return delta, m, v


def reference(x, mem, ffn1_pre_g, ffn1_post_g, ffn1_w_gate, ffn1_w_up, ffn1_w_down, mix_pre_g, mix_post_g, w_in, b_forget, mem_norm_g, w_mem_kv, w_gate, b_gate, w_br_sb, w_br_fox, w_br_mem, w_out, ffn2_pre_g, ffn2_post_g, ffn2_w_gate, ffn2_w_up, ffn2_w_down, loss_target, m_ffn1_pre_g, m_ffn1_post_g, m_ffn1_w_gate, m_ffn1_w_up, m_ffn1_w_down, m_mix_pre_g, m_mix_post_g, m_w_in, m_b_forget, m_mem_norm_g, m_w_mem_kv, m_w_gate, m_b_gate, m_w_br_sb, m_w_br_fox, m_w_br_mem, m_w_out, m_ffn2_pre_g, m_ffn2_post_g, m_ffn2_w_gate, m_ffn2_w_up, m_ffn2_w_down, v_ffn1_pre_g, v_ffn1_post_g, v_ffn1_w_gate, v_ffn1_w_up, v_ffn1_w_down, v_mix_pre_g, v_mix_post_g, v_w_in, v_b_forget, v_mem_norm_g, v_w_mem_kv, v_w_gate, v_b_gate, v_w_br_sb, v_w_br_fox, v_w_br_mem, v_w_out, v_ffn2_pre_g, v_ffn2_post_g, v_ffn2_w_gate, v_ffn2_w_up, v_ffn2_w_down):
    given = dict(x=x, mem=mem, ffn1_pre_g=ffn1_pre_g, ffn1_post_g=ffn1_post_g, ffn1_w_gate=ffn1_w_gate, ffn1_w_up=ffn1_w_up, ffn1_w_down=ffn1_w_down, mix_pre_g=mix_pre_g, mix_post_g=mix_post_g, w_in=w_in, b_forget=b_forget, mem_norm_g=mem_norm_g, w_mem_kv=w_mem_kv, w_gate=w_gate, b_gate=b_gate, w_br_sb=w_br_sb, w_br_fox=w_br_fox, w_br_mem=w_br_mem, w_out=w_out, ffn2_pre_g=ffn2_pre_g, ffn2_post_g=ffn2_post_g, ffn2_w_gate=ffn2_w_gate, ffn2_w_up=ffn2_w_up, ffn2_w_down=ffn2_w_down, loss_target=loss_target, m_ffn1_pre_g=m_ffn1_pre_g, m_ffn1_post_g=m_ffn1_post_g, m_ffn1_w_gate=m_ffn1_w_gate, m_ffn1_w_up=m_ffn1_w_up, m_ffn1_w_down=m_ffn1_w_down, m_mix_pre_g=m_mix_pre_g, m_mix_post_g=m_mix_post_g, m_w_in=m_w_in, m_b_forget=m_b_forget, m_mem_norm_g=m_mem_norm_g, m_w_mem_kv=m_w_mem_kv, m_w_gate=m_w_gate, m_b_gate=m_b_gate, m_w_br_sb=m_w_br_sb, m_w_br_fox=m_w_br_fox, m_w_br_mem=m_w_br_mem, m_w_out=m_w_out, m_ffn2_pre_g=m_ffn2_pre_g, m_ffn2_post_g=m_ffn2_post_g, m_ffn2_w_gate=m_ffn2_w_gate, m_ffn2_w_up=m_ffn2_w_up, m_ffn2_w_down=m_ffn2_w_down, v_ffn1_pre_g=v_ffn1_pre_g, v_ffn1_post_g=v_ffn1_post_g, v_ffn1_w_gate=v_ffn1_w_gate, v_ffn1_w_up=v_ffn1_w_up, v_ffn1_w_down=v_ffn1_w_down, v_mix_pre_g=v_mix_pre_g, v_mix_post_g=v_mix_post_g, v_w_in=v_w_in, v_b_forget=v_b_forget, v_mem_norm_g=v_mem_norm_g, v_w_mem_kv=v_w_mem_kv, v_w_gate=v_w_gate, v_b_gate=v_b_gate, v_w_br_sb=v_w_br_sb, v_w_br_fox=v_w_br_fox, v_w_br_mem=v_w_br_mem, v_w_out=v_w_out, v_ffn2_pre_g=v_ffn2_pre_g, v_ffn2_post_g=v_ffn2_post_g, v_ffn2_w_gate=v_ffn2_w_gate, v_ffn2_w_up=v_ffn2_w_up, v_ffn2_w_down=v_ffn2_w_down)
    weights = {n: given[n] for n in TWIN_WEIGHTS}
    shared = {n: given[n] for n in SHARED_INPUTS}
    per_example = {n: given[n] for n in ['x', 'mem']}
    grad_fn = _jax.value_and_grad(_loss, argnums=(0, 1))

    def one_microbatch(ex, loss_target):
        ex = dict(ex)
        diff = ex.pop(TWIN_DIFF_INPUT)
        return grad_fn(weights, diff, {**shared, **ex}, loss_target)

    if N_MICROBATCH == 1:
        loss, (grad_w, grad_x) = one_microbatch(per_example, given["loss_target"])
    else:
        def body(carry, xs):
            loss_sum, grad_sum = carry
            l_k, (gw_k, gx_k) = one_microbatch(xs[0], xs[1])
            with _jax.named_scope("update"):
                return (loss_sum + l_k, _jax.tree.map(_jnp.add, grad_sum, gw_k)), gx_k

        init = (_jnp.zeros((), _jnp.float32), _jax.tree.map(_jnp.zeros_like, weights))
        (loss, grad_w), grad_x = _jax.lax.scan(body, init, (per_example, given["loss_target"]))
    with _jax.named_scope("update"):
        delta_w, new_m, new_v = {}, {}, {}
        for n in TWIN_WEIGHTS:
            delta_w[n], new_m[n], new_v[n] = _adamw(weights[n], grad_w[n], given["m_" + n], given["v_" + n])
    return (loss, grad_x, *[grad_w[n] for n in TWIN_WEIGHTS], *[delta_w[n] for n in TWIN_WEIGHTS],
            *[new_m[n] for n in TWIN_WEIGHTS], *[new_v[n] for n in TWIN_WEIGHTS])
```

```python
import functools

import jax
import jax.numpy as jnp
from jax import lax
from jax.experimental import pallas as pl
from jax.experimental.pallas import tpu as pltpu

F32 = jnp.float32
BF16 = jnp.bfloat16
RMS_EPS = 1e-6
HEAD_DIM = 64
MEM_HEAD_DIM = 128
LANE = 128
V7X_VMEM_LIMIT_BYTES = 56 * 1024 * 1024
FLAT_COLS = 512
FLAT_UNIT = 16 * FLAT_COLS
FLAT_ROW_BLOCK = 512
N_CHIPS = 4
N_DEV = 8
NEG = float(jnp.finfo(jnp.float32).min)

ADAM_LR = 0.001
ADAM_B1 = 0.9
ADAM_B2 = 0.999
ADAM_EPS = 1e-08
ADAM_WD = 0.01
ADAM_STEP = 10

BIG = (("ffn1_w_gate", 2), ("ffn1_w_up", 2), ("ffn1_w_down", 1), ("w_in", 2), ("w_mem_kv", 1), ("w_gate", 2),
       ("w_br_sb", 2), ("w_br_fox", 2), ("w_br_mem", 2), ("w_out", 1),
       ("ffn2_w_gate", 2), ("ffn2_w_up", 2), ("ffn2_w_down", 1))
SMALL = ("ffn1_pre_g", "ffn1_post_g", "mix_pre_g", "mix_post_g", "b_forget", "mem_norm_g", "b_gate",
         "ffn2_pre_g", "ffn2_post_g")
WEIGHTS = ("ffn1_pre_g", "ffn1_post_g", "ffn1_w_gate", "ffn1_w_up", "ffn1_w_down", "mix_pre_g", "mix_post_g", "w_in",
           "b_forget", "mem_norm_g", "w_mem_kv", "w_gate", "b_gate", "w_br_sb", "w_br_fox", "w_br_mem", "w_out",
           "ffn2_pre_g", "ffn2_post_g", "ffn2_w_gate", "ffn2_w_up", "ffn2_w_down")


def _params(**kw):
    return pltpu.CompilerParams(vmem_limit_bytes=V7X_VMEM_LIMIT_BYTES, **kw)


def _dot(a, b):
    return jnp.dot(a, b, preferred_element_type=F32)


def _dot_nt(a, b):
    return lax.dot_general(a, b, (((1,), (1,)), ((), ())), preferred_element_type=F32)


def _dot_tn(a, b):
    return lax.dot_general(a, b, (((0,), (0,)), ((), ())), preferred_element_type=F32)


def _rms(t, g):
    return t * lax.rsqrt(jnp.mean(t * t, axis=-1, keepdims=True) + RMS_EPS) * g


def _pick(dim, pref):
    if dim <= pref:
        return dim
    for cand in range(pref - pref % LANE, 0, -LANE):
        if dim % cand == 0:
            return cand
    return dim


def _rows(bm, cols):
    return pl.BlockSpec((bm, cols), lambda i: (i, 0))


def _whole(shape):
    nd = len(shape)
    return pl.BlockSpec(shape, lambda i: (0,) * nd)


def _split3(x):
    hi = x.astype(BF16)
    r1 = x - hi.astype(F32)
    mid = r1.astype(BF16)
    lo = (r1 - mid.astype(F32)).astype(BF16)
    return hi, mid, lo


def _cumdot(x, tri):
    hi = x.astype(BF16)
    lo = (x - hi.astype(F32)).astype(BF16)
    return _dot(hi, tri) + _dot(lo, tri)


def _ffn_fwd_up(h, g_pre, wg, wu):
    S, D = h.shape
    F = wg.shape[1]
    bm = _pick(S, 256)

    def body(h_ref, g_ref, wg_ref, wu_ref, n_ref, gate_ref, up_ref, a_ref):
        n = _rms(h_ref[...], g_ref[...]).astype(BF16)
        n_ref[...] = n
        gate = _dot(n, wg_ref[...])
        up = _dot(n, wu_ref[...])
        gate_ref[...] = gate.astype(BF16)
        up_ref[...] = up.astype(BF16)
        a_ref[...] = (gate * jax.nn.sigmoid(gate) * up).astype(BF16)

    return pl.pallas_call(
        body, name="ffn_fwd_up", grid=(S // bm,),
        in_specs=[_rows(bm, D), _whole((1, D)), _whole((D, F)), _whole((D, F))],
        out_specs=[_rows(bm, D), _rows(bm, F), _rows(bm, F), _rows(bm, F)],
        out_shape=[jax.ShapeDtypeStruct((S, D), BF16)] + [jax.ShapeDtypeStruct((S, F), BF16)] * 3,
        compiler_params=_params(),
    )(h, g_pre, wg, wu)


def _ffn_fwd_down(a, wd, h, g_post):
    S, F = a.shape
    D = wd.shape[1]
    bm = _pick(S, 256)

    def body(a_ref, wd_ref, h_ref, g_ref, hout_ref, f_ref):
        f = _dot(a_ref[...], wd_ref[...])
        f_ref[...] = f
        hout_ref[...] = h_ref[...] + 0.5 * _rms(f, g_ref[...])

    return pl.pallas_call(
        body, name="ffn_fwd_down", grid=(S // bm,),
        in_specs=[_rows(bm, F), _whole((F, D)), _rows(bm, D), _whole((1, D))],
        out_specs=[_rows(bm, D), _rows(bm, D)],
        out_shape=[jax.ShapeDtypeStruct((S, D), F32)] * 2,
        compiler_params=_params(),
    )(a, wd, h, g_post)


def _ffn_bwd_down(dh, f, g_post, wd, gate, up):
    S, D = dh.shape
    F = wd.shape[0]
    bm = _pick(S, 256)

    def body(dh_ref, f_ref, g_ref, wd_ref, gate_ref, up_ref, df_ref, dgate_ref, dup_ref, dg_ref):
        _, vjp = jax.vjp(lambda t, g: 0.5 * _rms(t, g), f_ref[...], g_ref[...])
        df, dg = vjp(dh_ref[...])

        @pl.when(pl.program_id(0) == 0)
        def _():
            dg_ref[...] = jnp.zeros_like(dg_ref)

        dg_ref[...] += dg
        dfb = df.astype(BF16)
        df_ref[...] = dfb
        da = _dot_nt(dfb, wd_ref[...])
        gt = gate_ref[...].astype(F32)
        sig = jax.nn.sigmoid(gt)
        silu = gt * sig
        dup_ref[...] = (da * silu).astype(BF16)
        dgate_ref[...] = (da * up_ref[...].astype(F32) * (sig + silu * (1.0 - sig))).astype(BF16)

    return pl.pallas_call(
        body, name="ffn_bwd_down", grid=(S // bm,),
        in_specs=[_rows(bm, D), _rows(bm, D), _whole((1, D)), _whole((F, D)), _rows(bm, F), _rows(bm, F)],
        out_specs=[_rows(bm, D), _rows(bm, F), _rows(bm, F), _whole((1, D))],
        out_shape=[jax.ShapeDtypeStruct((S, D), BF16), jax.ShapeDtypeStruct((S, F), BF16),
                   jax.ShapeDtypeStruct((S, F), BF16), jax.ShapeDtypeStruct((1, D), F32)],
        compiler_params=_params(),
    )(dh, f, g_post, wd, gate, up)


def _ffn_bwd_up(dgate, dup, wg, wu, h_in, g_pre, dh):
    S, F = dgate.shape
    D = wg.shape[0]
    bm = _pick(S, 256)

    def body(dgate_ref, dup_ref, wg_ref, wu_ref, h_ref, g_ref, dh_ref, dhin_ref, dg_ref):
        dn = _dot_nt(dgate_ref[...], wg_ref[...]) + _dot_nt(dup_ref[...], wu_ref[...])
        _, vjp = jax.vjp(_rms, h_ref[...], g_ref[...])
        dhx, dg = vjp(dn)

        @pl.when(pl.program_id(0) == 0)
        def _():
            dg_ref[...] = jnp.zeros_like(dg_ref)

        dg_ref[...] += dg
        dhin_ref[...] = dh_ref[...] + dhx

    return pl.pallas_call(
        body, name="ffn_bwd_up", grid=(S // bm,),
        in_specs=[_rows(bm, F), _rows(bm, F), _whole((D, F)), _whole((D, F)), _rows(bm, D), _whole((1, D)),
                  _rows(bm, D)],
        out_specs=[_rows(bm, D), _whole((1, D))],
        out_shape=[jax.ShapeDtypeStruct((S, D), F32), jax.ShapeDtypeStruct((1, D), F32)],
        compiler_params=_params(),
    )(dgate, dup, wg, wu, h_in, g_pre, dh)


def _matmul(a, b, *, ta=False, tb=False, out_dtype=BF16, name):
    M, K = (a.shape[1], a.shape[0]) if ta else a.shape
    N = b.shape[0] if tb else b.shape[1]
    bm, bk = _pick(M, 512), _pick(K, 512)
    bn = N if N * bm * 4 <= 8 * 1024 * 1024 else _pick(N, 1536)
    nk = K // bk

    def body(a_ref, b_ref, o_ref, acc_ref):
        kk = pl.program_id(2)

        @pl.when(kk == 0)
        def _():
            acc_ref[...] = jnp.zeros_like(acc_ref)

        av, bv = a_ref[...], b_ref[...]
        dims = (((0 if ta else 1,), (1 if tb else 0,)), ((), ()))
        acc_ref[...] += lax.dot_general(av, bv, dims, preferred_element_type=F32)

        @pl.when(kk == nk - 1)
        def _():
            o_ref[...] = acc_ref[...].astype(o_ref.dtype)

    a_spec = pl.BlockSpec((bk, bm), lambda i, j, k: (k, i)) if ta else pl.BlockSpec((bm, bk), lambda i, j, k: (i, k))
    b_spec = pl.BlockSpec((bn, bk), lambda i, j, k: (j, k)) if tb else pl.BlockSpec((bk, bn), lambda i, j, k: (k, j))
    return pl.pallas_call(
        body, name=name, grid=(M // bm, N // bn, nk),
        in_specs=[a_spec, b_spec],
        out_specs=pl.BlockSpec((bm, bn), lambda i, j, k: (i, j)),
        out_shape=jax.ShapeDtypeStruct((M, N), out_dtype),
        scratch_shapes=[pltpu.VMEM((bm, bn), F32)],
        compiler_params=_params(),
    )(a, b)


def _mix_fwd_in(h, g_pre, win, wgate, b_gate, b_forget):
    S, D = h.shape
    PW = win.shape[1] - LANE
    G = wgate.shape[1]
    bm = _pick(S, 256)

    def body(h_ref, g_ref, win_ref, wgate_ref, bg_ref, bf_ref, u_ref, proj_ref, fl_ref, sg_ref):
        u = _rms(h_ref[...], g_ref[...]).astype(BF16)
        u_ref[...] = u
        proj = _dot(u, win_ref[...])
        proj_ref[...] = proj[:, :PW].astype(BF16)
        fl_ref[...] = proj[:, PW:] + bf_ref[...]
        sg_ref[...] = jax.nn.sigmoid(_dot(u, wgate_ref[...]) + bg_ref[...]).astype(BF16)

    return pl.pallas_call(
        body, name="mix_fwd_in", grid=(S // bm,),
        in_specs=[_rows(bm, D), _whole((1, D)), _whole((D, PW + LANE)), _whole((D, G)), _whole((1, G)),
                  _whole((1, LANE))],
        out_specs=[_rows(bm, D), _rows(bm, PW), _rows(bm, LANE), _rows(bm, G)],
        out_shape=[jax.ShapeDtypeStruct((S, D), BF16), jax.ShapeDtypeStruct((S, PW), BF16),
                   jax.ShapeDtypeStruct((S, LANE), F32), jax.ShapeDtypeStruct((S, G), BF16)],
        compiler_params=_params(),
    )(h, g_pre, win, wgate, b_gate, b_forget)


def _mix_fwd_out(o_sb, o_fx, o_mem, sg, w_sb, w_fx, w_mem, w_out, h, g_post):
    S, D = h.shape
    bm = _pick(S, 256)
    widths = (o_sb.shape[1], o_fx.shape[1], o_mem.shape[1])

    def body(osb_ref, ofx_ref, omem_ref, sg_ref, wsb_ref, wfx_ref, wmem_ref, wout_ref, h_ref, g_ref,
             hout_ref, z_ref, merged_ref):
        s = sg_ref[...].astype(F32)
        merged = (s[:, :D] * _dot(osb_ref[...], wsb_ref[...]) + s[:, D:2 * D] * _dot(ofx_ref[...], wfx_ref[...])
                  + s[:, 2 * D:] * _dot(omem_ref[...], wmem_ref[...]))
        mb = merged.astype(BF16)
        merged_ref[...] = mb
        z = _dot(mb, wout_ref[...])
        z_ref[...] = z
        hout_ref[...] = h_ref[...] + _rms(z, g_ref[...])

    return pl.pallas_call(
        body, name="mix_fwd_out", grid=(S // bm,),
        in_specs=[_rows(bm, widths[0]), _rows(bm, widths[1]), _rows(bm, widths[2]), _rows(bm, 3 * D),
                  _whole((widths[0], D)), _whole((widths[1], D)), _whole((widths[2], D)), _whole((D, D)),
                  _rows(bm, D), _whole((1, D))],
        out_specs=[_rows(bm, D), _rows(bm, D), _rows(bm, D)],
        out_shape=[jax.ShapeDtypeStruct((S, D), F32), jax.ShapeDtypeStruct((S, D), F32),
                   jax.ShapeDtypeStruct((S, D), BF16)],
        compiler_params=_params(),
    )(o_sb, o_fx, o_mem, sg, w_sb, w_fx, w_mem, w_out, h, g_post)


def _mix_bwd_out(dh, z, g_post, w_out, o_sb, o_fx, o_mem, w_sb, w_fx, w_mem, sg):
    S, D = dh.shape
    bm = _pick(S, 256)
    widths = (o_sb.shape[1], o_fx.shape[1], o_mem.shape[1])

    def body(dh_ref, z_ref, g_ref, wout_ref, osb_ref, ofx_ref, omem_ref, wsb_ref, wfx_ref, wmem_ref, sg_ref,
             dz_ref, dbsb_ref, dbfx_ref, dbmem_ref, dosb_ref, dofx_ref, domem_ref, dgp_ref, dbg_ref, dg_ref):
        _, vjp = jax.vjp(_rms, z_ref[...], g_ref[...])
        dz, dg = vjp(dh_ref[...])

        @pl.when(pl.program_id(0) == 0)
        def _():
            dg_ref[...] = jnp.zeros_like(dg_ref)
            dbg_ref[...] = jnp.zeros_like(dbg_ref)

        dg_ref[...] += dg
        dzb = dz.astype(BF16)
        dz_ref[...] = dzb
        dmerged = _dot_nt(dzb, wout_ref[...])
        s = sg_ref[...].astype(F32)
        branches = ((osb_ref, wsb_ref, dbsb_ref, dosb_ref), (ofx_ref, wfx_ref, dbfx_ref, dofx_ref),
                    (omem_ref, wmem_ref, dbmem_ref, domem_ref))
        for k, (o_ref, w_ref, db_ref, do_ref) in enumerate(branches):
            gs = s[:, k * D:(k + 1) * D]
            dbb = (dmerged * gs).astype(BF16)
            db_ref[...] = dbb
            do_ref[...] = _dot_nt(dbb, w_ref[...]).astype(BF16)
            dgp = dmerged * _dot(o_ref[...], w_ref[...]) * gs * (1.0 - gs)
            dgp_ref[:, k * D:(k + 1) * D] = dgp.astype(BF16)
            dbg_ref[:, k * D:(k + 1) * D] += jnp.sum(dgp, axis=0, keepdims=True)

    return pl.pallas_call(
        body, name="mix_bwd_out", grid=(S // bm,),
        in_specs=[_rows(bm, D), _rows(bm, D), _whole((1, D)), _whole((D, D)),
                  _rows(bm, widths[0]), _rows(bm, widths[1]), _rows(bm, widths[2]),
                  _whole((widths[0], D)), _whole((widths[1], D)), _whole((widths[2], D)), _rows(bm, 3 * D)],
        out_specs=[_rows(bm, D)] * 4 + [_rows(bm, widths[0]), _rows(bm, widths[1]), _rows(bm, widths[2]),
                                        _rows(bm, 3 * D), _whole((1, 3 * D)), _whole((1, D))],
        out_shape=[jax.ShapeDtypeStruct((S, D), BF16)] * 4
        + [jax.ShapeDtypeStruct((S, w), BF16) for w in widths]
        + [jax.ShapeDtypeStruct((S, 3 * D), BF16), jax.ShapeDtypeStruct((1, 3 * D), F32),
           jax.ShapeDtypeStruct((1, D), F32)],
        compiler_params=_params(),
    )(dh, z, g_post, w_out, o_sb, o_fx, o_mem, w_sb, w_fx, w_mem, sg)


def _mix_bwd_in(dproj, dgp, win, wgate, h_in, g_pre, dh):
    S, PWL = dproj.shape
    G = dgp.shape[1]
    D = h_in.shape[1]
    bm = _pick(S, 256)

    def body(dproj_ref, dgp_ref, win_ref, wgate_ref, h_ref, g_ref, dh_ref, dhin_ref, dg_ref):
        du = _dot_nt(dproj_ref[...], win_ref[...]) + _dot_nt(dgp_ref[...], wgate_ref[...])
        _, vjp = jax.vjp(_rms, h_ref[...], g_ref[...])
        dhx, dg = vjp(du)

        @pl.when(pl.program_id(0) == 0)
        def _():
            dg_ref[...] = jnp.zeros_like(dg_ref)

        dg_ref[...] += dg
        dhin_ref[...] = dh_ref[...] + dhx

    return pl.pallas_call(
        body, name="mix_bwd_in", grid=(S // bm,),
        in_specs=[_rows(bm, PWL), _rows(bm, G), _whole((D, PWL)), _whole((D, G)), _rows(bm, D), _whole((1, D)),
                  _rows(bm, D)],
        out_specs=[_rows(bm, D), _whole((1, D))],
        out_shape=[jax.ShapeDtypeStruct((S, D), F32), jax.ShapeDtypeStruct((1, D), F32)],
        compiler_params=_params(),
    )(dproj, dgp, win, wgate, h_in, g_pre, dh)


def _log_sigmoid(x):
    return jnp.minimum(x, 0.0) - jnp.log(1.0 + jnp.exp(-jnp.abs(x)))


def _fox_cumsum(fl):
    S = fl.shape[0]
    rb = _pick(S, LANE)

    def body(fl_ref, c_ref, carry_ref):
        @pl.when(pl.program_id(0) == 0)
        def _():
            carry_ref[...] = jnp.zeros_like(carry_ref)

        r = lax.broadcasted_iota(jnp.int32, (rb, rb), 0)
        cidx = lax.broadcasted_iota(jnp.int32, (rb, rb), 1)
        tri = (cidx <= r).astype(BF16)
        hi, mid, lo = _split3(_log_sigmoid(fl_ref[...]))
        c = _dot(tri, hi) + _dot(tri, mid) + _dot(tri, lo) + carry_ref[...]
        c_ref[...] = c
        carry_ref[...] = c[rb - 1:rb, :]

    return pl.pallas_call(
        body, name="fox_cumsum", grid=(S // rb,),
        in_specs=[_rows(rb, LANE)], out_specs=_rows(rb, LANE),
        out_shape=jax.ShapeDtypeStruct((S, LANE), F32),
        scratch_shapes=[pltpu.VMEM((1, LANE), F32)],
        compiler_params=_params(),
    )(fl)


def _fox_dlogit(dc, fl):
    S = fl.shape[0]
    rb = _pick(S, LANE)
    nb = S // rb

    def body(dc_ref, fl_ref, dfl_ref, dbf_ref, carry_ref):
        @pl.when(pl.program_id(0) == 0)
        def _():
            carry_ref[...] = jnp.zeros_like(carry_ref)
            dbf_ref[...] = jnp.zeros_like(dbf_ref)

        r = lax.broadcasted_iota(jnp.int32, (rb, rb), 0)
        cidx = lax.broadcasted_iota(jnp.int32, (rb, rb), 1)
        tri = (cidx >= r).astype(BF16)
        hi, mid, lo = _split3(dc_ref[...])
        rc = _dot(tri, hi) + _dot(tri, mid) + _dot(tri, lo) + carry_ref[...]
        carry_ref[...] = rc[0:1, :]
        dfl = rc * jax.nn.sigmoid(-fl_ref[...])
        dfl_ref[...] = dfl.astype(BF16)
        dbf_ref[...] += jnp.sum(dfl, axis=0, keepdims=True)

    rev = pl.BlockSpec((rb, LANE), lambda i: (nb - 1 - i, 0))
    return pl.pallas_call(
        body, name="fox_dlogit", grid=(nb,),
        in_specs=[rev, rev], out_specs=[rev, _whole((1, LANE))],
        out_shape=[jax.ShapeDtypeStruct((S, LANE), BF16), jax.ShapeDtypeStruct((1, LANE), F32)],
        scratch_shapes=[pltpu.VMEM((1, LANE), F32)],
        compiler_params=_params(),
    )(dc, fl)


def _attn_blocks(kind, S, Sk):
    tq = _pick(S, 256)
    tc = LANE if kind == "sb" else _pick(Sk, 256)
    return tq, tc


def _sb_logs(z, mask):
    tt = jnp.log(1.0 + jnp.exp(-jnp.abs(z)))
    ln = -jnp.maximum(z, 0.0) - tt
    lb = jnp.minimum(z, 0.0) - tt
    return lb, jnp.where(mask, ln, 0.0)


def _attn_fwd(kind, q, k, v, ccol=None, crow=None):
    H, S, dh = q.shape
    Sk = k.shape[1]
    tq, tc = _attn_blocks(kind, S, Sk)
    scale = dh ** -0.5
    causal = kind != "mem"

    def body(*refs):
        if kind == "fox":
            q_ref, k_ref, v_ref, cc_ref, cr_ref, o_ref, lse_ref = refs
        else:
            q_ref, k_ref, v_ref, o_ref, lse_ref = refs
        i = pl.program_id(1)
        qb = q_ref[0]
        nk = ((i + 1) * tq) // tc if causal else Sk // tc
        qpos = i * tq + lax.broadcasted_iota(jnp.int32, (tq, tc), 0)
        kio = lax.broadcasted_iota(jnp.int32, (tq, tc), 1)

        if kind == "sb":
            tri = (lax.broadcasted_iota(jnp.int32, (tc, tc), 0) > lax.broadcasted_iota(jnp.int32, (tc, tc), 1)
                   ).astype(BF16)

            def step(t, carry):
                run, acc = carry
                off = pl.multiple_of((nk - 1 - t) * tc, tc)
                ks = k_ref[0, pl.ds(off, tc), :]
                vs = v_ref[0, pl.ds(off, tc), :]
                z = _dot_nt(qb, ks) * scale
                mask = (off + kio) < qpos
                lb, lnm = _sb_logs(z, mask)
                w = jnp.where(mask, jnp.exp(lb + _cumdot(lnm, tri) + run), 0.0)
                acc = acc + _dot(w.astype(BF16), vs)
                return run + jnp.sum(lnm, axis=1, keepdims=True), acc

            run, acc = lax.fori_loop(0, nk, step, (jnp.zeros((tq, 1), F32), jnp.zeros((tq, dh), F32)))
            o_ref[0] = acc.astype(o_ref.dtype)
            lse_ref[0] = run
        else:
            def step(jc, carry):
                m, l, acc = carry
                off = pl.multiple_of(jc * tc, tc)
                ks = k_ref[0, pl.ds(off, tc), :]
                vs = v_ref[0, pl.ds(off, tc), :]
                z = _dot_nt(qb, ks) * scale
                if kind == "fox":
                    z = z + cc_ref[0] - cr_ref[0, pl.ds(jc, 1), :]
                    z = jnp.where((off + kio) <= qpos, z, NEG)
                m_new = jnp.maximum(m, jnp.max(z, axis=1, keepdims=True))
                alpha = jnp.exp(m - m_new)
                p = jnp.exp(z - m_new)
                l = alpha * l + jnp.sum(p, axis=1, keepdims=True)
                acc = alpha * acc + _dot(p.astype(BF16), vs)
                return m_new, l, acc

            m, l, acc = lax.fori_loop(0, nk, step, (jnp.full((tq, 1), NEG, F32), jnp.zeros((tq, 1), F32),
                                                    jnp.zeros((tq, dh), F32)))
            o_ref[0] = (acc / l).astype(o_ref.dtype)
            lse_ref[0] = m + jnp.log(l)

    qspec = pl.BlockSpec((1, tq, dh), lambda h, i: (h, i, 0))
    kspec = pl.BlockSpec((1, Sk, dh), lambda h, i: (h, 0, 0))
    colspec = pl.BlockSpec((1, tq, 1), lambda h, i: (h, i, 0))
    in_specs, args = [qspec, kspec, kspec], [q, k, v]
    if kind == "fox":
        in_specs += [colspec, pl.BlockSpec((1, Sk // tc, tc), lambda h, i: (h, 0, 0))]
        args += [ccol, crow]
    out_specs = [qspec, colspec]
    out_shape = [jax.ShapeDtypeStruct((H, S, dh), BF16), jax.ShapeDtypeStruct((H, S, 1), F32)]
    return pl.pallas_call(
        body, name="attn_fwd_" + kind, grid=(H, S // tq),
        in_specs=in_specs, out_specs=out_specs, out_shape=out_shape,
        compiler_params=_params(),
    )(*args)


def _attn_bwd(kind, q, k, v, o, do, ccol=None, crow=None, lse=None):
    H, S, dh = q.shape
    Sk = k.shape[1]
    tq, tc = _attn_blocks(kind, S, Sk)
    scale = dh ** -0.5
    causal = kind != "mem"
    nq = S // tq

    def body(*refs):
        if kind == "fox":
            (q_ref, k_ref, v_ref, o_ref, do_ref, cc_ref, cr_ref, lse_ref,
             dq_ref, dk_ref, dv_ref, dc_ref, dcc_ref, dk_acc, dv_acc, dc_acc) = refs
        else:
            q_ref, k_ref, v_ref, o_ref, do_ref, lse_ref, dq_ref, dk_ref, dv_ref, dk_acc, dv_acc = refs
        i = pl.program_id(1)

        @pl.when(i == 0)
        def _():
            dk_acc[...] = jnp.zeros_like(dk_acc)
            dv_acc[...] = jnp.zeros_like(dv_acc)
            if kind == "fox":
                dc_acc[...] = jnp.zeros_like(dc_acc)

        qb = q_ref[0]
        dob = do_ref[0]
        nk = ((i + 1) * tq) // tc if causal else Sk // tc
        qpos = i * tq + lax.broadcasted_iota(jnp.int32, (tq, tc), 0)
        kio = lax.broadcasted_iota(jnp.int32, (tq, tc), 1)

        if kind == "sb":
            r = lax.broadcasted_iota(jnp.int32, (tc, tc), 0)
            cidx = lax.broadcasted_iota(jnp.int32, (tc, tc), 1)
            tri_inc = (r <= cidx).astype(BF16)
            tri_exc = (r < cidx).astype(BF16)
            total = lse_ref[0]

            def step(jc, carry):
                pre, pre_e, dq = carry
                off = pl.multiple_of(jc * tc, tc)
                ks = k_ref[0, pl.ds(off, tc), :]
                vs = v_ref[0, pl.ds(off, tc), :]
                z = _dot_nt(qb, ks) * scale
                mask = (off + kio) < qpos
                lb, lnm = _sb_logs(z, mask)
                w = jnp.where(mask, jnp.exp(lb + (total - pre - _cumdot(lnm, tri_inc))), 0.0)
                e = w * _dot_nt(dob, vs)
                before = pre_e + _cumdot(e, tri_exc)
                beta = jnp.exp(lb)
                dzb = jnp.where(mask, e * (1.0 - beta) - beta * before, 0.0).astype(BF16)
                dk_acc[pl.ds(off, tc), :] += _dot_tn(dzb, qb)
                dv_acc[pl.ds(off, tc), :] += _dot_tn(w.astype(BF16), dob)
                return (pre + jnp.sum(lnm, axis=1, keepdims=True), pre_e + jnp.sum(e, axis=1, keepdims=True),
                        dq + _dot(dzb, ks))

            _, _, dq = lax.fori_loop(0, nk, step, (jnp.zeros((tq, 1), F32), jnp.zeros((tq, 1), F32),
                                                   jnp.zeros((tq, dh), F32)))
        else:
            dsum = jnp.sum(o_ref[0].astype(F32) * dob.astype(F32), axis=1, keepdims=True)

            def step(jc, carry):
                dq, rowsum = carry
                off = pl.multiple_of(jc * tc, tc)
                ks = k_ref[0, pl.ds(off, tc), :]
                vs = v_ref[0, pl.ds(off, tc), :]
                z = _dot_nt(qb, ks) * scale
                if kind == "fox":
                    z = z + cc_ref[0] - cr_ref[0, pl.ds(jc, 1), :]
                    z = jnp.where((off + kio) <= qpos, z, NEG)
                p = jnp.exp(z - lse_ref[0])
                ds = p * (_dot_nt(dob, vs) - dsum)
                dsb = ds.astype(BF16)
                dk_acc[pl.ds(off, tc), :] += _dot_tn(dsb, qb)
                dv_acc[pl.ds(off, tc), :] += _dot_tn(p.astype(BF16), dob)
                if kind == "fox":
                    dc_acc[pl.ds(jc, 1), :] -= jnp.sum(ds, axis=0, keepdims=True)
                    rowsum = rowsum + jnp.sum(ds, axis=1, keepdims=True)
                return dq + _dot(dsb, ks), rowsum

            dq, rowsum = lax.fori_loop(0, nk, step, (jnp.zeros((tq, dh), F32), jnp.zeros((tq, 1), F32)))
            if kind == "fox":
                dcc_ref[0] = rowsum
        dq_ref[0] = (dq * scale).astype(dq_ref.dtype)

        @pl.when(i == nq - 1)
        def _():
            dk_ref[0] = (dk_acc[...] * scale).astype(dk_ref.dtype)
            dv_ref[0] = dv_acc[...].astype(dv_ref.dtype)
            if kind == "fox":
                dc_ref[0] = dc_acc[...]

    qspec = pl.BlockSpec((1, tq, dh), lambda h, i: (h, i, 0))
    kspec = pl.BlockSpec((1, Sk, dh), lambda h, i: (h, 0, 0))
    colspec = pl.BlockSpec((1, tq, 1), lambda h, i: (h, i, 0))
    rowspec = pl.BlockSpec((1, Sk // tc, tc), lambda h, i: (h, 0, 0))
    in_specs, args = [qspec, kspec, kspec, qspec, qspec], [q, k, v, o, do]
    if kind == "fox":
        in_specs += [colspec, rowspec]
        args += [ccol, crow]
    in_specs += [colspec]
    args += [lse]
    out_specs = [qspec, kspec, kspec]
    out_shape = [jax.ShapeDtypeStruct((H, S, dh), BF16), jax.ShapeDtypeStruct((H, Sk, dh), BF16),
                 jax.ShapeDtypeStruct((H, Sk, dh), BF16)]
    scratch = [pltpu.VMEM((Sk, dh), F32), pltpu.VMEM((Sk, dh), F32)]
    if kind == "fox":
        out_specs += [rowspec, colspec]
        out_shape += [jax.ShapeDtypeStruct((H, Sk // tc, tc), F32), jax.ShapeDtypeStruct((H, S, 1), F32)]
        scratch.append(pltpu.VMEM((Sk // tc, tc), F32))
    return pl.pallas_call(
        body, name="attn_bwd_" + kind, grid=(H, nq),
        in_specs=in_specs, out_specs=out_specs, out_shape=out_shape, scratch_shapes=scratch,
        compiler_params=_params(),
    )(*args)


def _to_heads(t, n_heads):
    s = t.shape[0]
    return t.reshape(s, n_heads, -1).transpose(1, 0, 2)


def _from_heads(t):
    h, s, d = t.shape
    return t.transpose(1, 0, 2).reshape(s, h * d)


def _mem_norm(mem, g):
    M, D = mem.shape

    def body(mem_ref, g_ref, out_ref):
        out_ref[...] = _rms(mem_ref[...], g_ref[...]).astype(BF16)

    return pl.pallas_call(
        body, name="mem_norm", grid=(1,),
        in_specs=[_whole((M, D)), _whole((1, D))], out_specs=_whole((M, D)),
        out_shape=jax.ShapeDtypeStruct((M, D), BF16), compiler_params=_params(),
    )(mem, g)


def _mem_norm_bwd(mem, g, dmem_n):
    M, D = mem.shape
    L = dmem_n.shape[0]

    def body(mem_ref, g_ref, d_ref, dg_ref):
        d = d_ref[0]
        for l in range(1, L):
            d = d + d_ref[l]
        _, vjp = jax.vjp(_rms, mem_ref[...], g_ref[...])
        dg_ref[...] = vjp(d)[1]

    return pl.pallas_call(
        body, name="mem_norm_bwd", grid=(1,),
        in_specs=[_whole((M, D)), _whole((1, D)), _whole((L, M, D))], out_specs=_whole((1, D)),
        out_shape=jax.ShapeDtypeStruct((1, D), F32), compiler_params=_params(),
    )(mem, g, dmem_n)


def _loss_head(h, target):
    S, D = h.shape
    bm = _pick(S, 512)

    def body(h_ref, t_ref, dh_ref, loss_ref):
        err = h_ref[...] - t_ref[...]
        dh_ref[...] = err * (1.0 / D)

        @pl.when(pl.program_id(0) == 0)
        def _():
            loss_ref[...] = jnp.zeros_like(loss_ref)

        loss_ref[...] += 0.5 * jnp.sum(jnp.mean(err * err, axis=-1, keepdims=True), axis=0, keepdims=True)

    return pl.pallas_call(
        body, name="loss_head", grid=(S // bm,),
        in_specs=[_rows(bm, D), _rows(bm, D)], out_specs=[_rows(bm, D), _whole((8, LANE))],
        out_shape=[jax.ShapeDtypeStruct((S, D), F32), jax.ShapeDtypeStruct((8, LANE), F32)],
        compiler_params=_params(),
    )(h, target)


def _adamw(w, g, m, v, name):
    R, C = w.shape
    rb = R if R * C * 4 <= (1 << 20) else _pick(R, 256)
    if R % rb:
        rb = R
    c1 = 1.0 - ADAM_B1 ** ADAM_STEP
    c2 = 1.0 - ADAM_B2 ** ADAM_STEP

    def body(w_ref, g_ref, m_ref, v_ref, d_ref, mo_ref, vo_ref):
        gv = g_ref[...]
        mn = ADAM_B1 * m_ref[...] + (1.0 - ADAM_B1) * gv
        vn = ADAM_B2 * v_ref[...] + (1.0 - ADAM_B2) * (gv * gv)
        mo_ref[...] = mn
        vo_ref[...] = vn
        d_ref[...] = -ADAM_LR * ((mn / c1) / (jnp.sqrt(vn / c2) + ADAM_EPS) + ADAM_WD * w_ref[...])

    return pl.pallas_call(
        body, name=name, grid=(R // rb,),
        in_specs=[_rows(rb, C)] * 4, out_specs=[_rows(rb, C)] * 3,
        out_shape=[jax.ShapeDtypeStruct((R, C), F32)] * 3, compiler_params=_params(),
    )(w, g, m, v)


ANY = pl.BlockSpec(memory_space=pl.ANY)
MESH = pl.DeviceIdType.MESH


def _place():
    x, y, c = lax.axis_index("x"), lax.axis_index("y"), lax.axis_index("c")
    others = [(1 - x, y), (x, 1 - y), (1 - x, 1 - y)]
    return x, y, c, others


def _gather_weights(loc):
    _, R, C = loc.shape

    def body(loc_ref, out_ref, send_sems, recv_sems, local_sem):
        x, y, c, others = _place()
        me = 2 * x + y
        sibling = (x, y, 1 - c)

        def copy(k, src, dst, to):
            return pltpu.make_async_remote_copy(src_ref=src, dst_ref=dst, send_sem=send_sems.at[k],
                                                recv_sem=recv_sems.at[k], device_id=to, device_id_type=MESH)

        mine = pltpu.make_async_copy(loc_ref, out_ref.at[me], local_sem)
        mine.start()
        first = [copy(j, loc_ref.at[c], out_ref.at[me, c], (ox, oy, c)) for j, (ox, oy) in enumerate(others)]
        for cp in first:
            cp.start()
        passed = []
        for j, (ox, oy) in enumerate(others):
            landed = out_ref.at[2 * ox + oy, c]
            copy(j, loc_ref.at[c], landed, sibling).wait_recv()
            cp = copy(3 + j, landed, landed, sibling)
            cp.start()
            passed.append(cp)
        for j, (ox, oy) in enumerate(others):
            copy(3 + j, loc_ref.at[c], out_ref.at[2 * ox + oy, 1 - c], sibling).wait_recv()
        for cp in first + passed:
            cp.wait_send()
        mine.wait()

    return pl.pallas_call(
        body, name="gather_weights", in_specs=[ANY], out_specs=ANY,
        out_shape=jax.ShapeDtypeStruct((N_CHIPS, 2, R, C), loc.dtype),
        scratch_shapes=[pltpu.SemaphoreType.DMA((6,)), pltpu.SemaphoreType.DMA((6,)), pltpu.SemaphoreType.DMA],
    )(loc)


def _pair_exchange(g):
    _, _, R, C = g.shape

    def body(g_ref, out_ref, send_sem, recv_sem):
        x, y, c, _ = _place()
        cp = pltpu.make_async_remote_copy(src_ref=g_ref.at[1 - c], dst_ref=out_ref, send_sem=send_sem,
                                          recv_sem=recv_sem, device_id=(x, y, 1 - c), device_id_type=MESH)
        cp.start()
        cp.wait()

    return pl.pallas_call(
        body, name="pair_exchange", in_specs=[ANY], out_specs=ANY,
        out_shape=jax.ShapeDtypeStruct((N_CHIPS, R, C), g.dtype),
        scratch_shapes=[pltpu.SemaphoreType.DMA, pltpu.SemaphoreType.DMA],
    )(g)


def _pair_sum(g, sib, c_idx):
    _, _, R, C = g.shape
    rb = _pick(R, 512)

    def body(c_ref, g_ref, s_ref, o_ref):
        o_ref[...] = (g_ref[...].astype(F32) + s_ref[...].astype(F32)).astype(o_ref.dtype)

    return pl.pallas_call(
        body, name="pair_sum",
        grid_spec=pltpu.PrefetchScalarGridSpec(
            num_scalar_prefetch=1, grid=(N_CHIPS, R // rb),
            in_specs=[pl.BlockSpec((None, None, rb, C), lambda j, i, c_ref: (c_ref[0], j, i, 0)),
                      pl.BlockSpec((None, rb, C), lambda j, i, c_ref: (j, i, 0))],
            out_specs=pl.BlockSpec((None, rb, C), lambda j, i, c_ref: (j, i, 0))),
        out_shape=jax.ShapeDtypeStruct((N_CHIPS, R, C), g.dtype), compiler_params=_params(),
    )(c_idx, g, sib)


def _chip_exchange(p):
    _, R, C = p.shape

    def body(p_ref, out_ref, send_sems, recv_sems, local_sem):
        x, y, c, others = _place()
        me = 2 * x + y
        mine = pltpu.make_async_copy(p_ref.at[me], out_ref.at[me], local_sem)
        mine.start()
        sends = []
        for j, (ox, oy) in enumerate(others):
            cp = pltpu.make_async_remote_copy(src_ref=p_ref.at[2 * ox + oy], dst_ref=out_ref.at[me],
                                              send_sem=send_sems.at[j], recv_sem=recv_sems.at[j],
                                              device_id=(ox, oy, c), device_id_type=MESH)
            cp.start()
            sends.append(cp)
        for j, (ox, oy) in enumerate(others):
            pltpu.make_async_remote_copy(src_ref=p_ref.at[me], dst_ref=out_ref.at[2 * ox + oy],
                                         send_sem=send_sems.at[j], recv_sem=recv_sems.at[j],
                                         device_id=(ox, oy, c), device_id_type=MESH).wait_recv()
        for cp in sends:
            cp.wait_send()
        mine.wait()

    return pl.pallas_call(
        body, name="chip_exchange", in_specs=[ANY], out_specs=ANY,
        out_shape=jax.ShapeDtypeStruct((N_CHIPS, R, C), p.dtype),
        scratch_shapes=[pltpu.SemaphoreType.DMA((3,)), pltpu.SemaphoreType.DMA((3,)), pltpu.SemaphoreType.DMA],
    )(p)


def _chip_sum(r):
    _, R, C = r.shape
    rb = _pick(R, 512)

    def body(r_ref, o_ref):
        acc = r_ref[0].astype(F32)
        for j in range(1, N_CHIPS):
            acc = acc + r_ref[j].astype(F32)
        o_ref[...] = acc

    return pl.pallas_call(
        body, name="chip_sum", grid=(R // rb,),
        in_specs=[pl.BlockSpec((N_CHIPS, rb, C), lambda i: (0, i, 0))], out_specs=_rows(rb, C),
        out_shape=jax.ShapeDtypeStruct((R, C), F32), compiler_params=_params(),
    )(r)


def _pair_swap(rh):
    R, C = rh.shape

    def body(rh_ref, out_ref, send_sem, recv_sem, local_sem):
        x, y, c, _ = _place()
        mine = pltpu.make_async_copy(rh_ref, out_ref.at[c], local_sem)
        mine.start()
        cp = pltpu.make_async_remote_copy(src_ref=rh_ref, dst_ref=out_ref.at[c], send_sem=send_sem,
                                          recv_sem=recv_sem, device_id=(x, y, 1 - c), device_id_type=MESH)
        cp.start()
        pltpu.make_async_remote_copy(src_ref=rh_ref, dst_ref=out_ref.at[1 - c], send_sem=send_sem,
                                     recv_sem=recv_sem, device_id=(x, y, 1 - c), device_id_type=MESH).wait_recv()
        cp.wait_send()
        mine.wait()

    return pl.pallas_call(
        body, name="pair_swap", in_specs=[ANY], out_specs=ANY,
        out_shape=jax.ShapeDtypeStruct((2, R, C), rh.dtype),
        scratch_shapes=[pltpu.SemaphoreType.DMA, pltpu.SemaphoreType.DMA, pltpu.SemaphoreType.DMA],
    )(rh)


def _all_reduce_small(s):
    R, C = s.shape

    def body(s_ref, o_ref, buf, send_sems, recv_sems):
        x, y, c, _ = _place()
        me = 4 * x + 2 * y + c
        sends = []
        for k in range(1, N_DEV):
            fx, fy, fc = (k >> 2) & 1, (k >> 1) & 1, k & 1
            to = (x ^ fx, y ^ fy, c ^ fc)
            cp = pltpu.make_async_remote_copy(src_ref=s_ref, dst_ref=buf.at[me], send_sem=send_sems.at[k - 1],
                                              recv_sem=recv_sems.at[k - 1], device_id=to, device_id_type=MESH)
            cp.start()
            sends.append(cp)
        buf[me] = s_ref[...]
        for k in range(1, N_DEV):
            fx, fy, fc = (k >> 2) & 1, (k >> 1) & 1, k & 1
            frm = 4 * (x ^ fx) + 2 * (y ^ fy) + (c ^ fc)
            pltpu.make_async_remote_copy(src_ref=s_ref, dst_ref=buf.at[frm], send_sem=send_sems.at[k - 1],
                                         recv_sem=recv_sems.at[k - 1], device_id=(x, y, c),
                                         device_id_type=MESH).wait_recv()
        acc = buf[0]
        for d in range(1, N_DEV):
            acc = acc + buf[d]
        o_ref[...] = acc
        for cp in sends:
            cp.wait_send()

    vm = pl.BlockSpec(memory_space=pltpu.VMEM)
    return pl.pallas_call(
        body, name="all_reduce_small", in_specs=[vm], out_specs=vm,
        out_shape=jax.ShapeDtypeStruct((R, C), F32),
        scratch_shapes=[pltpu.VMEM((N_DEV, R, C), F32), pltpu.SemaphoreType.DMA((N_DEV - 1,)),
                        pltpu.SemaphoreType.DMA((N_DEV - 1,))],
    )(s)


def _padded(n):
    return -(-n // FLAT_UNIT) * FLAT_UNIT


def _pack_flat(pieces, dtype, row_block=FLAT_ROW_BLOCK):
    flat = []
    for p in pieces:
        p = p.reshape(-1).astype(dtype)
        flat.append(jnp.pad(p, (0, _padded(p.size) - p.size)))
    total = sum(p.size for p in flat)
    flat.append(jnp.zeros((-total) % (row_block * FLAT_COLS), dtype))
    return jnp.concatenate(flat).reshape(-1, FLAT_COLS)


def _unpack_flat(flat, shapes):
    lead = flat.shape[:-2]
    flat = flat.reshape(lead + (-1,))
    out, off = [], 0
    for shp in shapes:
        n = 1
        for d in shp:
            n *= d
        out.append(flat[..., off:off + n].reshape(lead + tuple(shp)))
        off += _padded(n)
    return out


def _slab(t, axis, j):
    n = t.shape[axis] // N_CHIPS
    return lax.slice_in_dim(t, j * n, (j + 1) * n, axis=axis)


def _layer_fwd(h0, mem_n, wl, dims):
    n_sb, n_fx, n_mem, sbw, fxw, memw = dims
    n1, gate1, up1, a1 = _ffn_fwd_up(h0, wl["ffn1_pre_g"], wl["ffn1_w_gate"], wl["ffn1_w_up"])
    h1, f1 = _ffn_fwd_down(a1, wl["ffn1_w_down"], h0, wl["ffn1_post_g"])

    u, proj, fl, sg = _mix_fwd_in(h1, wl["mix_pre_g"], wl["w_in"], wl["w_gate"], wl["b_gate"], wl["b_forget"])
    c = _fox_cumsum(fl)
    S = h0.shape[0]
    q_sb, k_sb, v_sb = (_to_heads(proj[:, k * sbw:(k + 1) * sbw], n_sb) for k in range(3))
    q_fx, k_fx, v_fx = (_to_heads(proj[:, 3 * sbw + k * fxw:3 * sbw + (k + 1) * fxw], n_fx) for k in range(3))
    q_mem = _to_heads(proj[:, 3 * sbw + 3 * fxw:], n_mem)
    tc = _attn_blocks("fox", S, S)[1]
    ct = c[:, :n_fx].T
    ccol, crow = ct.reshape(n_fx, S, 1), ct.reshape(n_fx, S // tc, tc)
    o_sb, tot_sb = _attn_fwd("sb", q_sb, k_sb, v_sb)
    o_fx, lse_fx = _attn_fwd("fox", q_fx, k_fx, v_fx, ccol, crow)
    kv = _matmul(mem_n, wl["w_mem_kv"], out_dtype=BF16, name="mem_kv")
    k_mem, v_mem = _to_heads(kv[:, :memw], n_mem), _to_heads(kv[:, memw:], n_mem)
    o_mem, lse_mem = _attn_fwd("mem", q_mem, k_mem, v_mem)
    o_sb_m, o_fx_m, o_mem_m = _from_heads(o_sb), _from_heads(o_fx), _from_heads(o_mem)
    h2, zmix, merged = _mix_fwd_out(o_sb_m, o_fx_m, o_mem_m, sg, wl["w_br_sb"], wl["w_br_fox"], wl["w_br_mem"],
                                    wl["w_out"], h1, wl["mix_post_g"])

    n2, gate2, up2, a2 = _ffn_fwd_up(h2, wl["ffn2_pre_g"], wl["ffn2_w_gate"], wl["ffn2_w_up"])
    h3, f2 = _ffn_fwd_down(a2, wl["ffn2_w_down"], h2, wl["ffn2_post_g"])
    saved = dict(h0=h0, n1=n1, gate1=gate1, up1=up1, a1=a1, f1=f1, h1=h1, u=u, fl=fl, sg=sg,
                 q_sb=q_sb, k_sb=k_sb, v_sb=v_sb, q_fx=q_fx, k_fx=k_fx, v_fx=v_fx, q_mem=q_mem,
                 k_mem=k_mem, v_mem=v_mem, ccol=ccol, crow=crow, o_sb=o_sb, o_fx=o_fx, o_mem=o_mem,
                 tot_sb=tot_sb, lse_fx=lse_fx, lse_mem=lse_mem, o_sb_m=o_sb_m, o_fx_m=o_fx_m, o_mem_m=o_mem_m,
                 zmix=zmix, merged=merged, h2=h2, n2=n2, gate2=gate2, up2=up2, a2=a2, f2=f2)
    return h3, saved


def _ffn_bwd(dh, sv, wl, tag, h_in):
    n, gate, up, a, f = (sv[k + tag] for k in ("n", "gate", "up", "a", "f"))
    pre = "ffn" + tag
    df, dgate, dup, dg_post = _ffn_bwd_down(dh, f, wl[pre + "_post_g"], wl[pre + "_w_down"], gate, up)
    dh_in, dg_pre = _ffn_bwd_up(dgate, dup, wl[pre + "_w_gate"], wl[pre + "_w_up"], h_in, wl[pre + "_pre_g"], dh)
    grads = {pre + "_post_g": dg_post, pre + "_pre_g": dg_pre,
             pre + "_w_down": _matmul(a, df, ta=True, name="dw_down"),
             pre + "_w_gate": _matmul(n, dgate, ta=True, name="dw_gate"),
             pre + "_w_up": _matmul(n, dup, ta=True, name="dw_up")}
    return dh_in, grads


def _layer_bwd(dh3, mem_n, wl, sv, dims):
    n_sb, n_fx, n_mem, sbw, fxw, memw = dims
    S = dh3.shape[0]
    dh2, grads = _ffn_bwd(dh3, sv, wl, "2", sv["h2"])

    (dz, db_sb, db_fx, db_mem, do_sb, do_fx, do_mem, dgp, db_gate, dg_post) = _mix_bwd_out(
        dh2, sv["zmix"], wl["mix_post_g"], wl["w_out"], sv["o_sb_m"], sv["o_fx_m"], sv["o_mem_m"],
        wl["w_br_sb"], wl["w_br_fox"], wl["w_br_mem"], sv["sg"])
    grads["mix_post_g"] = dg_post
    grads["b_gate"] = db_gate
    grads["w_out"] = _matmul(sv["merged"], dz, ta=True, name="dw_out")
    grads["w_br_sb"] = _matmul(sv["o_sb_m"], db_sb, ta=True, name="dw_br_sb")
    grads["w_br_fox"] = _matmul(sv["o_fx_m"], db_fx, ta=True, name="dw_br_fox")
    grads["w_br_mem"] = _matmul(sv["o_mem_m"], db_mem, ta=True, name="dw_br_mem")

    dq_sb, dk_sb, dv_sb = _attn_bwd("sb", sv["q_sb"], sv["k_sb"], sv["v_sb"], sv["o_sb"], _to_heads(do_sb, n_sb),
                                    lse=sv["tot_sb"])
    dq_fx, dk_fx, dv_fx, dcrow, dccol = _attn_bwd("fox", sv["q_fx"], sv["k_fx"], sv["v_fx"], sv["o_fx"],
                                                  _to_heads(do_fx, n_fx), sv["ccol"], sv["crow"], sv["lse_fx"])
    dq_mem, dk_mem, dv_mem = _attn_bwd("mem", sv["q_mem"], sv["k_mem"], sv["v_mem"], sv["o_mem"],
                                       _to_heads(do_mem, n_mem), lse=sv["lse_mem"])
    dkv = jnp.concatenate([_from_heads(dk_mem), _from_heads(dv_mem)], axis=1)
    grads["w_mem_kv"] = _matmul(mem_n, dkv, ta=True, name="dw_mem_kv")
    dmem_n = _matmul(dkv, wl["w_mem_kv"], tb=True, out_dtype=F32, name="dmem_n")

    dc = jnp.pad((dcrow.reshape(n_fx, S) + dccol.reshape(n_fx, S)).T, ((0, 0), (0, LANE - n_fx)))
    dfl, db_forget = _fox_dlogit(dc, sv["fl"])
    grads["b_forget"] = db_forget
    dproj = jnp.concatenate([_from_heads(t) for t in (dq_sb, dk_sb, dv_sb, dq_fx, dk_fx, dv_fx, dq_mem)] + [dfl],
                            axis=1)
    dh1, dg_pre = _mix_bwd_in(dproj, dgp, wl["w_in"], wl["w_gate"], sv["h1"], wl["mix_pre_g"], dh2)
    grads["mix_pre_g"] = dg_pre
    grads["w_in"] = _matmul(sv["u"], dproj, ta=True, name="dw_in")
    grads["w_gate"] = _matmul(sv["u"], dgp, ta=True, name="dw_gate_mix")

    dh0, g1 = _ffn_bwd(dh1, sv, wl, "1", sv["h0"])
    grads.update(g1)
    return dh0, grads, dmem_n


def kernel(x, mem, ffn1_pre_g, ffn1_post_g, ffn1_w_gate, ffn1_w_up, ffn1_w_down, mix_pre_g, mix_post_g, w_in, b_forget, mem_norm_g, w_mem_kv, w_gate, b_gate, w_br_sb, w_br_fox, w_br_mem, w_out, ffn2_pre_g, ffn2_post_g, ffn2_w_gate, ffn2_w_up, ffn2_w_down, loss_target, m_ffn1_pre_g, m_ffn1_post_g, m_ffn1_w_gate, m_ffn1_w_up, m_ffn1_w_down, m_mix_pre_g, m_mix_post_g, m_w_in, m_b_forget, m_mem_norm_g, m_w_mem_kv, m_w_gate, m_b_gate, m_w_br_sb, m_w_br_fox, m_w_br_mem, m_w_out, m_ffn2_pre_g, m_ffn2_post_g, m_ffn2_w_gate, m_ffn2_w_up, m_ffn2_w_down, v_ffn1_pre_g, v_ffn1_post_g, v_ffn1_w_gate, v_ffn1_w_up, v_ffn1_w_down, v_mix_pre_g, v_mix_post_g, v_w_in, v_b_forget, v_mem_norm_g, v_w_mem_kv, v_w_gate, v_b_gate, v_w_br_sb, v_w_br_fox, v_w_br_mem, v_w_out, v_ffn2_pre_g, v_ffn2_post_g, v_ffn2_w_gate, v_ffn2_w_up, v_ffn2_w_down):
    args = dict(locals())
    w = {n: args[n] for n in WEIGHTS}
    m = {n: args["m_" + n] for n in WEIGHTS}
    v = {n: args["v_" + n] for n in WEIGHTS}
    big_axis = dict(BIG)
    L = w["ffn1_pre_g"].shape[0]
    Lh = L // 2
    S, D = x.shape[1], x.shape[2]
    sbw, fxw, memw = w["w_br_sb"].shape[1], w["w_br_fox"].shape[1], w["w_br_mem"].shape[1]
    n_sb, n_fx, n_mem = sbw // HEAD_DIM, fxw // HEAD_DIM, memw // MEM_HEAD_DIM
    dims = (n_sb, n_fx, n_mem, sbw, fxw, memw)
    qkv_w = 3 * sbw + 3 * fxw
    c_idx = lax.axis_index("c")
    chip = 2 * lax.axis_index("x") + lax.axis_index("y")

    local_shapes = [(Lh,) + w[n].shape[1:] for n, _ in BIG]
    loc = jnp.stack([_pack_flat([w[n][hf * Lh:(hf + 1) * Lh] for n, _ in BIG], BF16) for hf in range(2)])
    gathered = _unpack_flat(_gather_weights(loc), local_shapes)
    full = {}
    for (n, axis), t in zip(BIG, gathered):
        t = t.reshape((N_CHIPS, L) + t.shape[3:])
        if axis == 2:
            full[n] = t.transpose(1, 2, 0, 3).reshape(L, t.shape[2], N_CHIPS * t.shape[3])
        else:
            full[n] = t.transpose(1, 0, 2, 3).reshape(L, N_CHIPS * t.shape[2], t.shape[3])
    wi = full["w_in"]
    full["w_in"] = jnp.concatenate(
        [wi[:, :, :qkv_w], wi[:, :, qkv_w + n_fx:], wi[:, :, qkv_w:qkv_w + n_fx],
         jnp.zeros((L, D, LANE - n_fx), BF16)], axis=2)
    wstack = dict(full)
    for n in SMALL:
        if n == "mem_norm_g":
            continue
        t = w[n]
        if n == "b_forget":
            t = jnp.pad(t, ((0, 0), (0, LANE - n_fx)))
        wstack[n] = t[:, None, :]
    g_mem = w["mem_norm_g"][None, :]

    mem_n = _mem_norm(mem[0], g_mem)

    def fwd_body(h, wl):
        return _layer_fwd(h, mem_n, wl, dims)

    h_out, saved = lax.scan(fwd_body, x[0], wstack)
    dh, loss_tile = _loss_head(h_out, loss_target[0])
    loss = lax.psum(loss_tile[0, 0], ("x", "y", "c"))

    def bwd_body(dh_c, xs):
        wl, sv = xs
        dh_n, grads, dmem_n = _layer_bwd(dh_c, mem_n, wl, sv, dims)
        return dh_n, (grads, dmem_n)

    grad_x, (gl, dmem_n) = lax.scan(bwd_body, dh, (wstack, saved), reverse=True)
    g_mem_norm = _mem_norm_bwd(mem[0], g_mem, dmem_n)

    gi = gl["w_in"]
    gl["w_in"] = jnp.concatenate([gi[:, :, :qkv_w], gi[:, :, qkv_w + memw:qkv_w + memw + n_fx],
                                  gi[:, :, qkv_w:qkv_w + memw]], axis=2)
    partial = jnp.stack([
        jnp.stack([_pack_flat([_slab(gl[n][hf * Lh:(hf + 1) * Lh], axis, j) for n, axis in BIG], BF16)
                   for j in range(N_CHIPS)]) for hf in range(2)])
    sib = _pair_exchange(partial)
    pair = _pair_sum(partial, sib, c_idx.reshape(1).astype(jnp.int32))
    reduced_half = _chip_sum(_chip_exchange(pair))
    reduced = _pair_swap(reduced_half)
    big_grads = {n: t.reshape((L,) + t.shape[2:]) for (n, _), t in zip(BIG, _unpack_flat(reduced, local_shapes))}

    small_local = {n: (g_mem_norm if n == "mem_norm_g" else
                       gl[n][:, 0, :n_fx] if n == "b_forget" else gl[n][:, 0, :]) for n in SMALL}
    small_shapes = [small_local[n].shape for n in SMALL]
    small_sum = _unpack_flat(_all_reduce_small(_pack_flat([small_local[n] for n in SMALL], F32, row_block=16)), small_shapes)
    grad = dict(big_grads)
    for n, t in zip(SMALL, small_sum):
        grad[n] = t.reshape(w[n].shape)

    delta, new_m, new_v = {}, {}, {}
    for n in WEIGHTS:
        shp = w[n].shape
        two_d = (1, shp[0]) if len(shp) == 1 else (-1, shp[-1])
        d_, m_, v_ = _adamw(w[n].reshape(two_d), grad[n].reshape(two_d), m[n].reshape(two_d), v[n].reshape(two_d),
                            name="adamw_" + n)
        delta[n], new_m[n], new_v[n] = d_.reshape(shp), m_.reshape(shp), v_.reshape(shp)

    return (loss, grad_x[None], *[grad[n] for n in WEIGHTS], *[delta[n] for n in WEIGHTS],
            *[new_m[n] for n in WEIGHTS], *[new_v[n] for n in WEIGHTS])
```

```python
import math

import jax
import jax.numpy as jnp
from jax import lax
from jax.experimental import pallas as pl
from jax.experimental.pallas import tpu as pltpu

F32 = jnp.float32
BF16 = jnp.bfloat16
RMS_EPS = 1e-6
HEAD_DIM = 64
MEM_HEAD_DIM = 128
LANE = 128
V7X_VMEM_LIMIT_BYTES = 56 * 1024 * 1024
FLAT_COLS = 512
FLAT_UNIT = 16 * FLAT_COLS
FLAT_ROW_BLOCK = 512
N_CHIPS = 4
N_DEV = 8
NEG = float(jnp.finfo(jnp.float32).min)

ADAM_LR = 0.001
ADAM_B1 = 0.9
ADAM_B2 = 0.999
ADAM_EPS = 1e-08
ADAM_WD = 0.01
ADAM_STEP = 10

BIG = (("ffn1_w_gate", 2), ("ffn1_w_up", 2), ("ffn1_w_down", 1), ("w_in", 2), ("w_mem_kv", 1), ("w_gate", 2),
       ("w_br_sb", 2), ("w_br_fox", 2), ("w_br_mem", 2), ("w_out", 1),
       ("ffn2_w_gate", 2), ("ffn2_w_up", 2), ("ffn2_w_down", 1))
SMALL = ("ffn1_pre_g", "ffn1_post_g", "mix_pre_g", "mix_post_g", "b_forget", "mem_norm_g", "b_gate",
         "ffn2_pre_g", "ffn2_post_g")
WEIGHTS = ("ffn1_pre_g", "ffn1_post_g", "ffn1_w_gate", "ffn1_w_up", "ffn1_w_down", "mix_pre_g", "mix_post_g", "w_in",
           "b_forget", "mem_norm_g", "w_mem_kv", "w_gate", "b_gate", "w_br_sb", "w_br_fox", "w_br_mem", "w_out",
           "ffn2_pre_g", "ffn2_post_g", "ffn2_w_gate", "ffn2_w_up", "ffn2_w_down")


def _params(**kw):
    return pltpu.CompilerParams(vmem_limit_bytes=V7X_VMEM_LIMIT_BYTES, **kw)


def _dot(a, b):
    return jnp.dot(a, b, preferred_element_type=F32)


def _dot_nt(a, b):
    return lax.dot_general(a, b, (((1,), (1,)), ((), ())), preferred_element_type=F32)


def _dot_tn(a, b):
    return lax.dot_general(a, b, (((0,), (0,)), ((), ())), preferred_element_type=F32)


def _rms(t, g):
    return t * lax.rsqrt(jnp.mean(t * t, axis=-1, keepdims=True) + RMS_EPS) * g


def _pick(dim, pref):
    if dim <= pref:
        return dim
    for cand in range(pref - pref % LANE, 0, -LANE):
        if dim % cand == 0:
            return cand
    return dim


def _rows(bm, cols):
    return pl.BlockSpec((bm, cols), lambda i: (i, 0))


def _whole(shape):
    nd = len(shape)
    return pl.BlockSpec(shape, lambda i: (0,) * nd)


def _split3(x):
    hi = x.astype(BF16)
    r1 = x - hi.astype(F32)
    mid = r1.astype(BF16)
    lo = (r1 - mid.astype(F32)).astype(BF16)
    return hi, mid, lo


def _cumdot(x, tri):
    hi = x.astype(BF16)
    lo = (x - hi.astype(F32)).astype(BF16)
    return _dot(hi, tri) + _dot(lo, tri)


def _ffn_fwd_up(h, g_pre, wg, wu):
    S, D = h.shape
    F = wg.shape[1]
    bm = _pick(S, 256)

    def body(h_ref, g_ref, wg_ref, wu_ref, n_ref, gate_ref, up_ref, a_ref):
        n = _rms(h_ref[...], g_ref[...]).astype(BF16)
        n_ref[...] = n
        gate = _dot(n, wg_ref[...])
        up = _dot(n, wu_ref[...])
        gate_ref[...] = gate.astype(BF16)
        up_ref[...] = up.astype(BF16)
        a_ref[...] = (gate * jax.nn.sigmoid(gate) * up).astype(BF16)

    return pl.pallas_call(
        body, name="ffn_fwd_up", grid=(S // bm,),
        in_specs=[_rows(bm, D), _whole((1, D)), _whole((D, F)), _whole((D, F))],
        out_specs=[_rows(bm, D), _rows(bm, F), _rows(bm, F), _rows(bm, F)],
        out_shape=[jax.ShapeDtypeStruct((S, D), BF16)] + [jax.ShapeDtypeStruct((S, F), BF16)] * 3,
        compiler_params=_params(),
    )(h, g_pre, wg, wu)


def _ffn_fwd_down(a, wd, h, g_post):
    S, F = a.shape
    D = wd.shape[1]
    bm = _pick(S, 256)

    def body(a_ref, wd_ref, h_ref, g_ref, hout_ref, f_ref):
        f = _dot(a_ref[...], wd_ref[...])
        f_ref[...] = f
        hout_ref[...] = h_ref[...] + 0.5 * _rms(f, g_ref[...])

    return pl.pallas_call(
        body, name="ffn_fwd_down", grid=(S // bm,),
        in_specs=[_rows(bm, F), _whole((F, D)), _rows(bm, D), _whole((1, D))],
        out_specs=[_rows(bm, D), _rows(bm, D)],
        out_shape=[jax.ShapeDtypeStruct((S, D), F32)] * 2,
        compiler_params=_params(),
    )(a, wd, h, g_post)


def _ffn_bwd_down(dh, f, g_post, wd, gate, up):
    S, D = dh.shape
    F = wd.shape[0]
    bm = _pick(S, 256)

    def body(dh_ref, f_ref, g_ref, wd_ref, gate_ref, up_ref, df_ref, dgate_ref, dup_ref, dg_ref):
        _, vjp = jax.vjp(lambda t, g: 0.5 * _rms(t, g), f_ref[...], g_ref[...])
        df, dg = vjp(dh_ref[...])

        @pl.when(pl.program_id(0) == 0)
        def _():
            dg_ref[...] = jnp.zeros_like(dg_ref)

        dg_ref[...] += dg
        dfb = df.astype(BF16)
        df_ref[...] = dfb
        da = _dot_nt(dfb, wd_ref[...])
        gt = gate_ref[...].astype(F32)
        sig = jax.nn.sigmoid(gt)
        silu = gt * sig
        dup_ref[...] = (da * silu).astype(BF16)
        dgate_ref[...] = (da * up_ref[...].astype(F32) * (sig + silu * (1.0 - sig))).astype(BF16)

    return pl.pallas_call(
        body, name="ffn_bwd_down", grid=(S // bm,),
        in_specs=[_rows(bm, D), _rows(bm, D), _whole((1, D)), _whole((F, D)), _rows(bm, F), _rows(bm, F)],
        out_specs=[_rows(bm, D), _rows(bm, F), _rows(bm, F), _whole((1, D))],
        out_shape=[jax.ShapeDtypeStruct((S, D), BF16), jax.ShapeDtypeStruct((S, F), BF16),
                   jax.ShapeDtypeStruct((S, F), BF16), jax.ShapeDtypeStruct((1, D), F32)],
        compiler_params=_params(),
    )(dh, f, g_post, wd, gate, up)


def _ffn_bwd_up(dgate, dup, wg, wu, h_in, g_pre, dh):
    S, F = dgate.shape
    D = wg.shape[0]
    bm = _pick(S, 256)

    def body(dgate_ref, dup_ref, wg_ref, wu_ref, h_ref, g_ref, dh_ref, dhin_ref, dg_ref):
        dn = _dot_nt(dgate_ref[...], wg_ref[...]) + _dot_nt(dup_ref[...], wu_ref[...])
        _, vjp = jax.vjp(_rms, h_ref[...], g_ref[...])
        dhx, dg = vjp(dn)

        @pl.when(pl.program_id(0) == 0)
        def _():
            dg_ref[...] = jnp.zeros_like(dg_ref)

        dg_ref[...] += dg
        dhin_ref[...] = dh_ref[...] + dhx

    return pl.pallas_call(
        body, name="ffn_bwd_up", grid=(S // bm,),
        in_specs=[_rows(bm, F), _rows(bm, F), _whole((D, F)), _whole((D, F)), _rows(bm, D), _whole((1, D)),
                  _rows(bm, D)],
        out_specs=[_rows(bm, D), _whole((1, D))],
        out_shape=[jax.ShapeDtypeStruct((S, D), F32), jax.ShapeDtypeStruct((1, D), F32)],
        compiler_params=_params(),
    )(dgate, dup, wg, wu, h_in, g_pre, dh)


def _matmul(a, b, *, ta=False, tb=False, out_dtype=BF16, name):
    M, K = (a.shape[1], a.shape[0]) if ta else a.shape
    N = b.shape[0] if tb else b.shape[1]
    bm, bk = _pick(M, 512), _pick(K, 512)
    bn = N if N * bm * 4 <= 8 * 1024 * 1024 else _pick(N, 1536)
    nk = K // bk

    def body(a_ref, b_ref, o_ref, acc_ref):
        kk = pl.program_id(2)

        @pl.when(kk == 0)
        def _():
            acc_ref[...] = jnp.zeros_like(acc_ref)

        av, bv = a_ref[...], b_ref[...]
        dims = (((0 if ta else 1,), (1 if tb else 0,)), ((), ()))
        acc_ref[...] += lax.dot_general(av, bv, dims, preferred_element_type=F32)

        @pl.when(kk == nk - 1)
        def _():
            o_ref[...] = acc_ref[...].astype(o_ref.dtype)

    a_spec = pl.BlockSpec((bk, bm), lambda i, j, k: (k, i)) if ta else pl.BlockSpec((bm, bk), lambda i, j, k: (i, k))
    b_spec = pl.BlockSpec((bn, bk), lambda i, j, k: (j, k)) if tb else pl.BlockSpec((bk, bn), lambda i, j, k: (k, j))
    return pl.pallas_call(
        body, name=name, grid=(M // bm, N // bn, nk),
        in_specs=[a_spec, b_spec],
        out_specs=pl.BlockSpec((bm, bn), lambda i, j, k: (i, j)),
        out_shape=jax.ShapeDtypeStruct((M, N), out_dtype),
        scratch_shapes=[pltpu.VMEM((bm, bn), F32)],
        compiler_params=_params(),
    )(a, b)


def _mix_fwd_in(h, g_pre, win, wgate, b_gate, b_forget):
    S, D = h.shape
    PW = win.shape[1] - LANE
    G = wgate.shape[1]
    bm = _pick(S, 256)

    def body(h_ref, g_ref, win_ref, wgate_ref, bg_ref, bf_ref, u_ref, proj_ref, fl_ref, sg_ref):
        u = _rms(h_ref[...], g_ref[...]).astype(BF16)
        u_ref[...] = u
        proj = _dot(u, win_ref[...])
        proj_ref[...] = proj[:, :PW].astype(BF16)
        fl_ref[...] = proj[:, PW:] + bf_ref[...]
        sg_ref[...] = jax.nn.sigmoid(_dot(u, wgate_ref[...]) + bg_ref[...]).astype(BF16)

    return pl.pallas_call(
        body, name="mix_fwd_in", grid=(S // bm,),
        in_specs=[_rows(bm, D), _whole((1, D)), _whole((D, PW + LANE)), _whole((D, G)), _whole((1, G)),
                  _whole((1, LANE))],
        out_specs=[_rows(bm, D), _rows(bm, PW), _rows(bm, LANE), _rows(bm, G)],
        out_shape=[jax.ShapeDtypeStruct((S, D), BF16), jax.ShapeDtypeStruct((S, PW), BF16),
                   jax.ShapeDtypeStruct((S, LANE), F32), jax.ShapeDtypeStruct((S, G), BF16)],
        compiler_params=_params(),
    )(h, g_pre, win, wgate, b_gate, b_forget)


def _mix_fwd_out(o_sb, o_fx, o_mem, sg, w_sb, w_fx, w_mem, w_out, h, g_post):
    S, D = h.shape
    bm = _pick(S, 256)
    widths = (o_sb.shape[1], o_fx.shape[1], o_mem.shape[1])

    def body(osb_ref, ofx_ref, omem_ref, sg_ref, wsb_ref, wfx_ref, wmem_ref, wout_ref, h_ref, g_ref,
             hout_ref, z_ref, merged_ref):
        s = sg_ref[...].astype(F32)
        merged = (s[:, :D] * _dot(osb_ref[...], wsb_ref[...]) + s[:, D:2 * D] * _dot(ofx_ref[...], wfx_ref[...])
                  + s[:, 2 * D:] * _dot(omem_ref[...], wmem_ref[...]))
        mb = merged.astype(BF16)
        merged_ref[...] = mb
        z = _dot(mb, wout_ref[...])
        z_ref[...] = z
        hout_ref[...] = h_ref[...] + _rms(z, g_ref[...])

    return pl.pallas_call(
        body, name="mix_fwd_out", grid=(S // bm,),
        in_specs=[_rows(bm, widths[0]), _rows(bm, widths[1]), _rows(bm, widths[2]), _rows(bm, 3 * D),
                  _whole((widths[0], D)), _whole((widths[1], D)), _whole((widths[2], D)), _whole((D, D)),
                  _rows(bm, D), _whole((1, D))],
        out_specs=[_rows(bm, D), _rows(bm, D), _rows(bm, D)],
        out_shape=[jax.ShapeDtypeStruct((S, D), F32), jax.ShapeDtypeStruct((S, D), F32),
                   jax.ShapeDtypeStruct((S, D), BF16)],
        compiler_params=_params(),
    )(o_sb, o_fx, o_mem, sg, w_sb, w_fx, w_mem, w_out, h, g_post)


def _mix_bwd_out(dh, z, g_post, w_out, o_sb, o_fx, o_mem, w_sb, w_fx, w_mem, sg):
    S, D = dh.shape
    bm = _pick(S, 256)
    widths = (o_sb.shape[1], o_fx.shape[1], o_mem.shape[1])

    def body(dh_ref, z_ref, g_ref, wout_ref, osb_ref, ofx_ref, omem_ref, wsb_ref, wfx_ref, wmem_ref, sg_ref,
             dz_ref, dbsb_ref, dbfx_ref, dbmem_ref, dosb_ref, dofx_ref, domem_ref, dgp_ref, dbg_ref, dg_ref):
        _, vjp = jax.vjp(_rms, z_ref[...], g_ref[...])
        dz, dg = vjp(dh_ref[...])

        @pl.when(pl.program_id(0) == 0)
        def _():
            dg_ref[...] = jnp.zeros_like(dg_ref)
            dbg_ref[...] = jnp.zeros_like(dbg_ref)

        dg_ref[...] += dg
        dzb = dz.astype(BF16)
        dz_ref[...] = dzb
        dmerged = _dot_nt(dzb, wout_ref[...])
        s = sg_ref[...].astype(F32)
        branches = ((osb_ref, wsb_ref, dbsb_ref, dosb_ref), (ofx_ref, wfx_ref, dbfx_ref, dofx_ref),
                    (omem_ref, wmem_ref, dbmem_ref, domem_ref))
        for k, (o_ref, w_ref, db_ref, do_ref) in enumerate(branches):
            gs = s[:, k * D:(k + 1) * D]
            dbb = (dmerged * gs).astype(BF16)
            db_ref[...] = dbb
            do_ref[...] = _dot_nt(dbb, w_ref[...]).astype(BF16)
            dgp = dmerged * _dot(o_ref[...], w_ref[...]) * gs * (1.0 - gs)
            dgp_ref[:, k * D:(k + 1) * D] = dgp.astype(BF16)
            dbg_ref[:, k * D:(k + 1) * D] += jnp.sum(dgp, axis=0, keepdims=True)

    return pl.pallas_call(
        body, name="mix_bwd_out", grid=(S // bm,),
        in_specs=[_rows(bm, D), _rows(bm, D), _whole((1, D)), _whole((D, D)),
                  _rows(bm, widths[0]), _rows(bm, widths[1]), _rows(bm, widths[2]),
                  _whole((widths[0], D)), _whole((widths[1], D)), _whole((widths[2], D)), _rows(bm, 3 * D)],
        out_specs=[_rows(bm, D)] * 4 + [_rows(bm, widths[0]), _rows(bm, widths[1]), _rows(bm, widths[2]),
                                        _rows(bm, 3 * D), _whole((1, 3 * D)), _whole((1, D))],
        out_shape=[jax.ShapeDtypeStruct((S, D), BF16)] * 4
        + [jax.ShapeDtypeStruct((S, w), BF16) for w in widths]
        + [jax.ShapeDtypeStruct((S, 3 * D), BF16), jax.ShapeDtypeStruct((1, 3 * D), F32),
           jax.ShapeDtypeStruct((1, D), F32)],
        compiler_params=_params(),
    )(dh, z, g_post, w_out, o_sb, o_fx, o_mem, w_sb, w_fx, w_mem, sg)


def _mix_bwd_in(dproj, dgp, win, wgate, h_in, g_pre, dh):
    S, PWL = dproj.shape
    G = dgp.shape[1]
    D = h_in.shape[1]
    bm = _pick(S, 256)

    def body(dproj_ref, dgp_ref, win_ref, wgate_ref, h_ref, g_ref, dh_ref, dhin_ref, dg_ref):
        du = _dot_nt(dproj_ref[...], win_ref[...]) + _dot_nt(dgp_ref[...], wgate_ref[...])
        _, vjp = jax.vjp(_rms, h_ref[...], g_ref[...])
        dhx, dg = vjp(du)

        @pl.when(pl.program_id(0) == 0)
        def _():
            dg_ref[...] = jnp.zeros_like(dg_ref)

        dg_ref[...] += dg
        dhin_ref[...] = dh_ref[...] + dhx

    return pl.pallas_call(
        body, name="mix_bwd_in", grid=(S // bm,),
        in_specs=[_rows(bm, PWL), _rows(bm, G), _whole((D, PWL)), _whole((D, G)), _rows(bm, D), _whole((1, D)),
                  _rows(bm, D)],
        out_specs=[_rows(bm, D), _whole((1, D))],
        out_shape=[jax.ShapeDtypeStruct((S, D), F32), jax.ShapeDtypeStruct((1, D), F32)],
        compiler_params=_params(),
    )(dproj, dgp, win, wgate, h_in, g_pre, dh)


def _log_sigmoid(x):
    return jnp.minimum(x, 0.0) - jnp.log(1.0 + jnp.exp(-jnp.abs(x)))


def _fox_cumsum(fl):
    S = fl.shape[0]
    rb = _pick(S, LANE)

    def body(fl_ref, c_ref, carry_ref):
        @pl.when(pl.program_id(0) == 0)
        def _():
            carry_ref[...] = jnp.zeros_like(carry_ref)

        r = lax.broadcasted_iota(jnp.int32, (rb, rb), 0)
        cidx = lax.broadcasted_iota(jnp.int32, (rb, rb), 1)
        tri = (cidx <= r).astype(BF16)
        hi, mid, lo = _split3(_log_sigmoid(fl_ref[...]))
        c = _dot(tri, hi) + _dot(tri, mid) + _dot(tri, lo) + carry_ref[...]
        c_ref[...] = c
        carry_ref[...] = c[rb - 1:rb, :]

    return pl.pallas_call(
        body, name="fox_cumsum", grid=(S // rb,),
        in_specs=[_rows(rb, LANE)], out_specs=_rows(rb, LANE),
        out_shape=jax.ShapeDtypeStruct((S, LANE), F32),
        scratch_shapes=[pltpu.VMEM((1, LANE), F32)],
        compiler_params=_params(),
    )(fl)


def _fox_dlogit(dc, fl):
    S = fl.shape[0]
    rb = _pick(S, LANE)
    nb = S // rb

    def body(dc_ref, fl_ref, dfl_ref, dbf_ref, carry_ref):
        @pl.when(pl.program_id(0) == 0)
        def _():
            carry_ref[...] = jnp.zeros_like(carry_ref)
            dbf_ref[...] = jnp.zeros_like(dbf_ref)

        r = lax.broadcasted_iota(jnp.int32, (rb, rb), 0)
        cidx = lax.broadcasted_iota(jnp.int32, (rb, rb), 1)
        tri = (cidx >= r).astype(BF16)
        hi, mid, lo = _split3(dc_ref[...])
        rc = _dot(tri, hi) + _dot(tri, mid) + _dot(tri, lo) + carry_ref[...]
        carry_ref[...] = rc[0:1, :]
        dfl = rc * jax.nn.sigmoid(-fl_ref[...])
        dfl_ref[...] = dfl.astype(BF16)
        dbf_ref[...] += jnp.sum(dfl, axis=0, keepdims=True)

    rev = pl.BlockSpec((rb, LANE), lambda i: (nb - 1 - i, 0))
    return pl.pallas_call(
        body, name="fox_dlogit", grid=(nb,),
        in_specs=[rev, rev], out_specs=[rev, _whole((1, LANE))],
        out_shape=[jax.ShapeDtypeStruct((S, LANE), BF16), jax.ShapeDtypeStruct((1, LANE), F32)],
        scratch_shapes=[pltpu.VMEM((1, LANE), F32)],
        compiler_params=_params(),
    )(dc, fl)


def _attn_blocks(kind, S, Sk):
    tq = _pick(S, 256)
    tc = LANE if kind == "sb" else _pick(Sk, 256)
    return tq, tc


def _heads_per_step(n_heads):
    return 2 if n_heads % 2 == 0 else 1


def _is_power_of_two(x):
    return math.frexp(x)[0] == 0.5


def _sb_logs(z):
    ln = -jnp.maximum(z, 0.0) - jnp.log(1.0 + jnp.exp(-jnp.abs(z)))
    return ln + z, ln


def _attn_fwd(kind, q, k, v, ccol=None, crow=None):
    H, S, dh = q.shape
    Sk = k.shape[1]
    tq, tc = _attn_blocks(kind, S, Sk)
    hb = _heads_per_step(H)
    scale = dh ** -0.5
    fold = _is_power_of_two(scale)
    causal = kind != "mem"
    n_diag = tq // tc if causal else 0

    def body(*refs):
        if kind == "fox":
            q_ref, k_ref, v_ref, cc_ref, cr_ref, o_ref, lse_ref = refs
        else:
            q_ref, k_ref, v_ref, o_ref, lse_ref = refs
        i = pl.program_id(1)
        n_full = (i * tq) // tc if causal else Sk // tc
        qpos = i * tq + lax.broadcasted_iota(jnp.int32, (tq, tc), 0)
        kio = lax.broadcasted_iota(jnp.int32, (tq, tc), 1)
        qs = [q_ref[hh] * scale if fold else q_ref[hh] for hh in range(hb)]
        heads = range(hb)

        if kind == "sb":
            tri = (lax.broadcasted_iota(jnp.int32, (tc, tc), 0) > lax.broadcasted_iota(jnp.int32, (tc, tc), 1)
                   ).astype(BF16)

            def chunk(hh, jc, masked, run, acc):
                off = pl.multiple_of(jc * tc, tc)
                ks = k_ref[hh, pl.ds(off, tc), :]
                vs = v_ref[hh, pl.ds(off, tc), :]
                lb, ln = _sb_logs(_dot_nt(qs[hh], ks))
                if masked:
                    mask = (off + kio) < qpos
                    ln = jnp.where(mask, ln, 0.0)
                w = jnp.exp(lb + _cumdot(ln, tri) + run)
                if masked:
                    w = jnp.where(mask, w, 0.0)
                return run + jnp.sum(ln, axis=1, keepdims=True), acc + _dot(w.astype(BF16), vs)

            state = tuple((jnp.zeros((tq, 1), F32), jnp.zeros((tq, dh), F32)) for _ in heads)
            for d in range(n_diag - 1, -1, -1):
                state = tuple(chunk(hh, n_full + d, True, *state[hh]) for hh in heads)
            state = lax.fori_loop(
                0, n_full, lambda t, st: tuple(chunk(hh, n_full - 1 - t, False, *st[hh]) for hh in heads), state)
            for hh in heads:
                o_ref[hh] = state[hh][1].astype(o_ref.dtype)
                lse_ref[hh] = state[hh][0]
        else:
            def chunk(hh, jc, masked, m, l, acc):
                off = pl.multiple_of(jc * tc, tc)
                ks = k_ref[hh, pl.ds(off, tc), :]
                vs = v_ref[hh, pl.ds(off, tc), :]
                z = _dot_nt(qs[hh], ks)
                if not fold:
                    z = z * scale
                if kind == "fox":
                    z = z + cc_ref[hh] - cr_ref[hh, pl.ds(jc, 1), :]
                if masked:
                    z = jnp.where((off + kio) <= qpos, z, NEG)
                m_new = jnp.maximum(m, jnp.max(z, axis=1, keepdims=True))
                alpha = jnp.exp(m - m_new)
                p = jnp.exp(z - m_new)
                return (m_new, alpha * l + jnp.sum(p, axis=1, keepdims=True),
                        alpha * acc + _dot(p.astype(BF16), vs))

            state = tuple((jnp.full((tq, 1), NEG, F32), jnp.zeros((tq, 1), F32), jnp.zeros((tq, dh), F32))
                          for _ in heads)
            state = lax.fori_loop(
                0, n_full, lambda jc, st: tuple(chunk(hh, jc, False, *st[hh]) for hh in heads), state)
            for d in range(n_diag):
                state = tuple(chunk(hh, n_full + d, True, *state[hh]) for hh in heads)
            for hh in heads:
                m, l, acc = state[hh]
                o_ref[hh] = (acc / l).astype(o_ref.dtype)
                lse_ref[hh] = m + jnp.log(l)

    qspec = pl.BlockSpec((hb, tq, dh), lambda h, i: (h, i, 0))
    kspec = pl.BlockSpec((hb, Sk, dh), lambda h, i: (h, 0, 0))
    colspec = pl.BlockSpec((hb, tq, 1), lambda h, i: (h, i, 0))
    in_specs, args = [qspec, kspec, kspec], [q, k, v]
    if kind == "fox":
        in_specs += [colspec, pl.BlockSpec((hb, Sk // tc, tc), lambda h, i: (h, 0, 0))]
        args += [ccol, crow]
    out_specs = [qspec, colspec]
    out_shape = [jax.ShapeDtypeStruct((H, S, dh), BF16), jax.ShapeDtypeStruct((H, S, 1), F32)]
    return pl.pallas_call(
        body, name="attn_fwd_" + kind, grid=(H // hb, S // tq),
        in_specs=in_specs, out_specs=out_specs, out_shape=out_shape,
        compiler_params=_params(),
    )(*args)


def _attn_bwd(kind, q, k, v, o, do, ccol=None, crow=None, lse=None):
    H, S, dh = q.shape
    Sk = k.shape[1]
    tq, tc = _attn_blocks(kind, S, Sk)
    hb = _heads_per_step(H)
    scale = dh ** -0.5
    fold = _is_power_of_two(scale)
    causal = kind != "mem"
    n_diag = tq // tc if causal else 0
    nq = S // tq

    def body(*refs):
        if kind == "fox":
            (q_ref, k_ref, v_ref, o_ref, do_ref, cc_ref, cr_ref, lse_ref,
             dq_ref, dk_ref, dv_ref, dc_ref, dcc_ref, dk_acc, dv_acc, dc_acc) = refs
        else:
            q_ref, k_ref, v_ref, o_ref, do_ref, lse_ref, dq_ref, dk_ref, dv_ref, dk_acc, dv_acc = refs
        i = pl.program_id(1)

        @pl.when(i == 0)
        def _():
            dk_acc[...] = jnp.zeros_like(dk_acc)
            dv_acc[...] = jnp.zeros_like(dv_acc)
            if kind == "fox":
                dc_acc[...] = jnp.zeros_like(dc_acc)

        n_full = (i * tq) // tc if causal else Sk // tc
        qpos = i * tq + lax.broadcasted_iota(jnp.int32, (tq, tc), 0)
        kio = lax.broadcasted_iota(jnp.int32, (tq, tc), 1)
        heads = range(hb)
        qs = [q_ref[hh] * scale if fold else q_ref[hh] for hh in heads]
        dos = [do_ref[hh] for hh in heads]

        if kind == "sb":
            r = lax.broadcasted_iota(jnp.int32, (tc, tc), 0)
            cidx = lax.broadcasted_iota(jnp.int32, (tc, tc), 1)
            tri_inc = (r <= cidx).astype(BF16)
            tri_exc = (r < cidx).astype(BF16)

            def chunk(hh, jc, masked, pre, pre_e, dq):
                off = pl.multiple_of(jc * tc, tc)
                ks = k_ref[hh, pl.ds(off, tc), :]
                vs = v_ref[hh, pl.ds(off, tc), :]
                lb, ln = _sb_logs(_dot_nt(qs[hh], ks))
                if masked:
                    mask = (off + kio) < qpos
                    ln = jnp.where(mask, ln, 0.0)
                w = jnp.exp(lb + (lse_ref[hh] - pre - _cumdot(ln, tri_inc)))
                if masked:
                    w = jnp.where(mask, w, 0.0)
                e = w * _dot_nt(dos[hh], vs)
                beta = jnp.exp(lb)
                dz = e * (1.0 - beta) - beta * (pre_e + _cumdot(e, tri_exc))
                if masked:
                    dz = jnp.where(mask, dz, 0.0)
                dzb = dz.astype(BF16)
                dk_acc[hh, pl.ds(off, tc), :] += _dot_tn(dzb, qs[hh])
                dv_acc[hh, pl.ds(off, tc), :] += _dot_tn(w.astype(BF16), dos[hh])
                return (pre + jnp.sum(ln, axis=1, keepdims=True), pre_e + jnp.sum(e, axis=1, keepdims=True),
                        dq + _dot(dzb, ks))

            state = tuple((jnp.zeros((tq, 1), F32), jnp.zeros((tq, 1), F32), jnp.zeros((tq, dh), F32)) for _ in heads)
            state = lax.fori_loop(
                0, n_full, lambda jc, st: tuple(chunk(hh, jc, False, *st[hh]) for hh in heads), state)
            for d in range(n_diag):
                state = tuple(chunk(hh, n_full + d, True, *state[hh]) for hh in heads)
            dqs = [state[hh][2] for hh in heads]
        else:
            dsum = [jnp.sum(o_ref[hh].astype(F32) * dos[hh].astype(F32), axis=1, keepdims=True) for hh in heads]

            def chunk(hh, jc, masked, dq, rowsum):
                off = pl.multiple_of(jc * tc, tc)
                ks = k_ref[hh, pl.ds(off, tc), :]
                vs = v_ref[hh, pl.ds(off, tc), :]
                z = _dot_nt(qs[hh], ks)
                if not fold:
                    z = z * scale
                if kind == "fox":
                    z = z + cc_ref[hh] - cr_ref[hh, pl.ds(jc, 1), :]
                if masked:
                    z = jnp.where((off + kio) <= qpos, z, NEG)
                p = jnp.exp(z - lse_ref[hh])
                ds = p * (_dot_nt(dos[hh], vs) - dsum[hh])
                dsb = ds.astype(BF16)
                dk_acc[hh, pl.ds(off, tc), :] += _dot_tn(dsb, qs[hh])
                dv_acc[hh, pl.ds(off, tc), :] += _dot_tn(p.astype(BF16), dos[hh])
                if kind == "fox":
                    dc_acc[hh, pl.ds(jc, 1), :] -= jnp.sum(ds, axis=0, keepdims=True)
                    rowsum = rowsum + jnp.sum(ds, axis=1, keepdims=True)
                return dq + _dot(dsb, ks), rowsum

            state = tuple((jnp.zeros((tq, dh), F32), jnp.zeros((tq, 1), F32)) for _ in heads)
            state = lax.fori_loop(
                0, n_full, lambda jc, st: tuple(chunk(hh, jc, False, *st[hh]) for hh in heads), state)
            for d in range(n_diag):
                state = tuple(chunk(hh, n_full + d, True, *state[hh]) for hh in heads)
            dqs = [state[hh][0] for hh in heads]
            if kind == "fox":
                for hh in heads:
                    dcc_ref[hh] = state[hh][1]
        for hh in heads:
            dq_ref[hh] = (dqs[hh] * scale).astype(dq_ref.dtype)

        @pl.when(i == nq - 1)
        def _():
            dk = dk_acc[...] if fold else dk_acc[...] * scale
            dk_ref[...] = dk.astype(dk_ref.dtype)
            dv_ref[...] = dv_acc[...].astype(dv_ref.dtype)
            if kind == "fox":
                dc_ref[...] = dc_acc[...]

    qspec = pl.BlockSpec((hb, tq, dh), lambda h, i: (h, i, 0))
    kspec = pl.BlockSpec((hb, Sk, dh), lambda h, i: (h, 0, 0))
    colspec = pl.BlockSpec((hb, tq, 1), lambda h, i: (h, i, 0))
    rowspec = pl.BlockSpec((hb, Sk // tc, tc), lambda h, i: (h, 0, 0))
    in_specs, args = [qspec, kspec, kspec, qspec, qspec], [q, k, v, o, do]
    if kind == "fox":
        in_specs += [colspec, rowspec]
        args += [ccol, crow]
    in_specs += [colspec]
    args += [lse]
    out_specs = [qspec, kspec, kspec]
    out_shape = [jax.ShapeDtypeStruct((H, S, dh), BF16), jax.ShapeDtypeStruct((H, Sk, dh), BF16),
                 jax.ShapeDtypeStruct((H, Sk, dh), BF16)]
    scratch = [pltpu.VMEM((hb, Sk, dh), F32), pltpu.VMEM((hb, Sk, dh), F32)]
    if kind == "fox":
        out_specs += [rowspec, colspec]
        out_shape += [jax.ShapeDtypeStruct((H, Sk // tc, tc), F32), jax.ShapeDtypeStruct((H, S, 1), F32)]
        scratch.append(pltpu.VMEM((hb, Sk // tc, tc), F32))
    return pl.pallas_call(
        body, name="attn_bwd_" + kind, grid=(H // hb, nq),
        in_specs=in_specs, out_specs=out_specs, out_shape=out_shape, scratch_shapes=scratch,
        compiler_params=_params(),
    )(*args)


def _to_heads(t, n_heads):
    s = t.shape[0]
    return t.reshape(s, n_heads, -1).transpose(1, 0, 2)


def _from_heads(t):
    h, s, d = t.shape
    return t.transpose(1, 0, 2).reshape(s, h * d)


def _mem_norm(mem, g):
    M, D = mem.shape

    def body(mem_ref, g_ref, out_ref):
        out_ref[...] = _rms(mem_ref[...], g_ref[...]).astype(BF16)

    return pl.pallas_call(
        body, name="mem_norm", grid=(1,),
        in_specs=[_whole((M, D)), _whole((1, D))], out_specs=_whole((M, D)),
        out_shape=jax.ShapeDtypeStruct((M, D), BF16), compiler_params=_params(),
    )(mem, g)


def _mem_norm_bwd(mem, g, dmem_n):
    M, D = mem.shape
    L = dmem_n.shape[0]

    def body(mem_ref, g_ref, d_ref, dg_ref):
        d = d_ref[0]
        for l in range(1, L):
            d = d + d_ref[l]
        _, vjp = jax.vjp(_rms, mem_ref[...], g_ref[...])
        dg_ref[...] = vjp(d)[1]

    return pl.pallas_call(
        body, name="mem_norm_bwd", grid=(1,),
        in_specs=[_whole((M, D)), _whole((1, D)), _whole((L, M, D))], out_specs=_whole((1, D)),
        out_shape=jax.ShapeDtypeStruct((1, D), F32), compiler_params=_params(),
    )(mem, g, dmem_n)


def _loss_head(h, target):
    S, D = h.shape
    bm = _pick(S, 512)

    def body(h_ref, t_ref, dh_ref, loss_ref):
        err = h_ref[...] - t_ref[...]
        dh_ref[...] = err * (1.0 / D)

        @pl.when(pl.program_id(0) == 0)
        def _():
            loss_ref[...] = jnp.zeros_like(loss_ref)

        loss_ref[...] += 0.5 * jnp.sum(jnp.mean(err * err, axis=-1, keepdims=True), axis=0, keepdims=True)

    return pl.pallas_call(
        body, name="loss_head", grid=(S // bm,),
        in_specs=[_rows(bm, D), _rows(bm, D)], out_specs=[_rows(bm, D), _whole((8, LANE))],
        out_shape=[jax.ShapeDtypeStruct((S, D), F32), jax.ShapeDtypeStruct((8, LANE), F32)],
        compiler_params=_params(),
    )(h, target)


def _adamw(w, g, m, v, name):
    R, C = w.shape
    rb = R if R * C * 4 <= (1 << 20) else _pick(R, 256)
    if R % rb:
        rb = R
    c1 = 1.0 - ADAM_B1 ** ADAM_STEP
    c2 = 1.0 - ADAM_B2 ** ADAM_STEP

    def body(w_ref, g_ref, m_ref, v_ref, d_ref, mo_ref, vo_ref):
        gv = g_ref[...]
        mn = ADAM_B1 * m_ref[...] + (1.0 - ADAM_B1) * gv
        vn = ADAM_B2 * v_ref[...] + (1.0 - ADAM_B2) * (gv * gv)
        mo_ref[...] = mn
        vo_ref[...] = vn
        d_ref[...] = -ADAM_LR * ((mn / c1) / (jnp.sqrt(vn / c2) + ADAM_EPS) + ADAM_WD * w_ref[...])

    return pl.pallas_call(
        body, name=name, grid=(R // rb,),
        in_specs=[_rows(rb, C)] * 4, out_specs=[_rows(rb, C)] * 3,
        out_shape=[jax.ShapeDtypeStruct((R, C), F32)] * 3, compiler_params=_params(),
    )(w, g, m, v)


ANY = pl.BlockSpec(memory_space=pl.ANY)
MESH = pl.DeviceIdType.MESH


def _place():
    x, y, c = lax.axis_index("x"), lax.axis_index("y"), lax.axis_index("c")
    others = [(1 - x, y), (x, 1 - y), (1 - x, 1 - y)]
    return x, y, c, others


def _place_own(loc, chip_idx):
    _, R, C = loc.shape
    rb = _pick(R, 2 * FLAT_ROW_BLOCK)

    def body(chip_ref, loc_ref, out_ref):
        out_ref[...] = loc_ref[...]

    return pl.pallas_call(
        body, name="place_own",
        grid_spec=pltpu.PrefetchScalarGridSpec(
            num_scalar_prefetch=1, grid=(2, R // rb),
            in_specs=[pl.BlockSpec((None, rb, C), lambda hf, i, chip_ref: (hf, i, 0))],
            out_specs=pl.BlockSpec((None, None, rb, C), lambda hf, i, chip_ref: (chip_ref[0], hf, i, 0))),
        out_shape=jax.ShapeDtypeStruct((N_CHIPS, 2, R, C), loc.dtype), compiler_params=_params(),
    )(chip_idx, loc)


def _gather_weights(loc, own):
    _, R, C = loc.shape

    def body(loc_ref, own_ref, out_ref, send_sems, recv_sems):
        del own_ref
        x, y, c, others = _place()
        me = 2 * x + y
        sibling = (x, y, 1 - c)

        def copy(k, src, dst, to):
            return pltpu.make_async_remote_copy(src_ref=src, dst_ref=dst, send_sem=send_sems.at[k],
                                                recv_sem=recv_sems.at[k], device_id=to, device_id_type=MESH)

        first = [copy(j, loc_ref.at[c], out_ref.at[me, c], (ox, oy, c)) for j, (ox, oy) in enumerate(others)]
        for cp in first:
            cp.start()
        passed = []
        for j, (ox, oy) in enumerate(others):
            landed = out_ref.at[2 * ox + oy, c]
            copy(j, loc_ref.at[c], landed, sibling).wait_recv()
            cp = copy(3 + j, landed, landed, sibling)
            cp.start()
            passed.append(cp)
        for j, (ox, oy) in enumerate(others):
            copy(3 + j, loc_ref.at[c], out_ref.at[2 * ox + oy, 1 - c], sibling).wait_recv()
        for cp in first + passed:
            cp.wait_send()

    return pl.pallas_call(
        body, name="gather_weights", in_specs=[ANY, ANY], out_specs=ANY,
        out_shape=jax.ShapeDtypeStruct((N_CHIPS, 2, R, C), loc.dtype), input_output_aliases={1: 0},
        scratch_shapes=[pltpu.SemaphoreType.DMA((6,)), pltpu.SemaphoreType.DMA((6,))],
    )(loc, own)


def _pair_exchange(g):
    _, _, R, C = g.shape

    def body(g_ref, out_ref, send_sem, recv_sem):
        x, y, c, _ = _place()
        cp = pltpu.make_async_remote_copy(src_ref=g_ref.at[1 - c], dst_ref=out_ref, send_sem=send_sem,
                                          recv_sem=recv_sem, device_id=(x, y, 1 - c), device_id_type=MESH)
        cp.start()
        cp.wait()

    return pl.pallas_call(
        body, name="pair_exchange", in_specs=[ANY], out_specs=ANY,
        out_shape=jax.ShapeDtypeStruct((N_CHIPS, R, C), g.dtype),
        scratch_shapes=[pltpu.SemaphoreType.DMA, pltpu.SemaphoreType.DMA],
    )(g)


def _pair_sum(g, sib, c_idx):
    _, _, R, C = g.shape
    rb = _pick(R, 512)

    def body(c_ref, g_ref, s_ref, o_ref):
        o_ref[...] = (g_ref[...].astype(F32) + s_ref[...].astype(F32)).astype(o_ref.dtype)

    return pl.pallas_call(
        body, name="pair_sum",
        grid_spec=pltpu.PrefetchScalarGridSpec(
            num_scalar_prefetch=1, grid=(N_CHIPS, R // rb),
            in_specs=[pl.BlockSpec((None, None, rb, C), lambda j, i, c_ref: (c_ref[0], j, i, 0)),
                      pl.BlockSpec((None, rb, C), lambda j, i, c_ref: (j, i, 0))],
            out_specs=pl.BlockSpec((None, rb, C), lambda j, i, c_ref: (j, i, 0))),
        out_shape=jax.ShapeDtypeStruct((N_CHIPS, R, C), g.dtype), compiler_params=_params(),
    )(c_idx, g, sib)


def _chip_exchange(p):
    _, R, C = p.shape

    def body(p_ref, out_ref, send_sems, recv_sems):
        x, y, c, others = _place()
        copies = []
        for j, (ox, oy) in enumerate(others):
            cp = pltpu.make_async_remote_copy(src_ref=p_ref.at[2 * ox + oy], dst_ref=out_ref.at[j],
                                              send_sem=send_sems.at[j], recv_sem=recv_sems.at[j],
                                              device_id=(ox, oy, c), device_id_type=MESH)
            cp.start()
            copies.append(cp)
        for cp in copies:
            cp.wait()

    return pl.pallas_call(
        body, name="chip_exchange", in_specs=[ANY], out_specs=ANY,
        out_shape=jax.ShapeDtypeStruct((N_CHIPS - 1, R, C), p.dtype),
        scratch_shapes=[pltpu.SemaphoreType.DMA((3,)), pltpu.SemaphoreType.DMA((3,))],
    )(p)


def _chip_sum(p, r, chip_idx):
    _, R, C = r.shape
    rb = _pick(R, 512)

    def body(chip_ref, p_ref, r_ref, o_ref):
        acc = p_ref[...].astype(F32)
        for j in range(N_CHIPS - 1):
            acc = acc + r_ref[j].astype(F32)
        o_ref[...] = acc

    return pl.pallas_call(
        body, name="chip_sum",
        grid_spec=pltpu.PrefetchScalarGridSpec(
            num_scalar_prefetch=1, grid=(R // rb,),
            in_specs=[pl.BlockSpec((None, rb, C), lambda i, chip_ref: (chip_ref[0], i, 0)),
                      pl.BlockSpec((N_CHIPS - 1, rb, C), lambda i, chip_ref: (0, i, 0))],
            out_specs=pl.BlockSpec((rb, C), lambda i, chip_ref: (i, 0))),
        out_shape=jax.ShapeDtypeStruct((R, C), F32), compiler_params=_params(),
    )(chip_idx, p, r)


def _pair_swap(rh):
    R, C = rh.shape

    def body(rh_ref, out_ref, send_sem, recv_sem):
        x, y, c, _ = _place()
        cp = pltpu.make_async_remote_copy(src_ref=rh_ref, dst_ref=out_ref, send_sem=send_sem,
                                          recv_sem=recv_sem, device_id=(x, y, 1 - c), device_id_type=MESH)
        cp.start()
        cp.wait()

    return pl.pallas_call(
        body, name="pair_swap", in_specs=[ANY], out_specs=ANY,
        out_shape=jax.ShapeDtypeStruct((R, C), rh.dtype),
        scratch_shapes=[pltpu.SemaphoreType.DMA, pltpu.SemaphoreType.DMA],
    )(rh)


def _all_reduce_small(s):
    R, C = s.shape

    def body(s_ref, o_ref, buf, send_sems, recv_sems):
        x, y, c, _ = _place()
        me = 4 * x + 2 * y + c
        sends = []
        for k in range(1, N_DEV):
            fx, fy, fc = (k >> 2) & 1, (k >> 1) & 1, k & 1
            to = (x ^ fx, y ^ fy, c ^ fc)
            cp = pltpu.make_async_remote_copy(src_ref=s_ref, dst_ref=buf.at[me], send_sem=send_sems.at[k - 1],
                                              recv_sem=recv_sems.at[k - 1], device_id=to, device_id_type=MESH)
            cp.start()
            sends.append(cp)
        buf[me] = s_ref[...]
        for k in range(1, N_DEV):
            fx, fy, fc = (k >> 2) & 1, (k >> 1) & 1, k & 1
            frm = 4 * (x ^ fx) + 2 * (y ^ fy) + (c ^ fc)
            pltpu.make_async_remote_copy(src_ref=s_ref, dst_ref=buf.at[frm], send_sem=send_sems.at[k - 1],
                                         recv_sem=recv_sems.at[k - 1], device_id=(x, y, c),
                                         device_id_type=MESH).wait_recv()
        acc = buf[0]
        for d in range(1, N_DEV):
            acc = acc + buf[d]
        o_ref[...] = acc
        for cp in sends:
            cp.wait_send()

    vm = pl.BlockSpec(memory_space=pltpu.VMEM)
    return pl.pallas_call(
        body, name="all_reduce_small", in_specs=[vm], out_specs=vm,
        out_shape=jax.ShapeDtypeStruct((R, C), F32),
        scratch_shapes=[pltpu.VMEM((N_DEV, R, C), F32), pltpu.SemaphoreType.DMA((N_DEV - 1,)),
                        pltpu.SemaphoreType.DMA((N_DEV - 1,))],
    )(s)


def _padded(n):
    return -(-n // FLAT_UNIT) * FLAT_UNIT


def _pack_flat(pieces, dtype, row_block=FLAT_ROW_BLOCK):
    flat = []
    for p in pieces:
        p = p.reshape(-1).astype(dtype)
        flat.append(jnp.pad(p, (0, _padded(p.size) - p.size)))
    total = sum(p.size for p in flat)
    flat.append(jnp.zeros((-total) % (row_block * FLAT_COLS), dtype))
    return jnp.concatenate(flat).reshape(-1, FLAT_COLS)


def _unpack_flat(flat, shapes):
    lead = flat.shape[:-2]
    flat = flat.reshape(lead + (-1,))
    out, off = [], 0
    for shp in shapes:
        n = math.prod(shp)
        out.append(flat[..., off:off + n].reshape(lead + tuple(shp)))
        off += _padded(n)
    return out


def _flat_offsets(shapes):
    offs, off = [], 0
    for shp in shapes:
        offs.append(off)
        off += _padded(math.prod(shp))
    return offs


def _slab(t, axis, j):
    n = t.shape[axis - 1] // N_CHIPS
    return lax.slice_in_dim(t, j * n, (j + 1) * n, axis=axis - 1)


def _layer_fwd(h0, mem_n, wl, dims):
    n_sb, n_fx, n_mem, sbw, fxw, memw = dims
    n1, gate1, up1, a1 = _ffn_fwd_up(h0, wl["ffn1_pre_g"], wl["ffn1_w_gate"], wl["ffn1_w_up"])
    h1, f1 = _ffn_fwd_down(a1, wl["ffn1_w_down"], h0, wl["ffn1_post_g"])

    u, proj, fl, sg = _mix_fwd_in(h1, wl["mix_pre_g"], wl["w_in"], wl["w_gate"], wl["b_gate"], wl["b_forget"])
    c = _fox_cumsum(fl)
    S = h0.shape[0]
    q_sb, k_sb, v_sb = (_to_heads(proj[:, k * sbw:(k + 1) * sbw], n_sb) for k in range(3))
    q_fx, k_fx, v_fx = (_to_heads(proj[:, 3 * sbw + k * fxw:3 * sbw + (k + 1) * fxw], n_fx) for k in range(3))
    q_mem = _to_heads(proj[:, 3 * sbw + 3 * fxw:], n_mem)
    tc = _attn_blocks("fox", S, S)[1]
    ct = c[:, :n_fx].T
    ccol, crow = ct.reshape(n_fx, S, 1), ct.reshape(n_fx, S // tc, tc)
    o_sb, tot_sb = _attn_fwd("sb", q_sb, k_sb, v_sb)
    o_fx, lse_fx = _attn_fwd("fox", q_fx, k_fx, v_fx, ccol, crow)
    kv = _matmul(mem_n, wl["w_mem_kv"], out_dtype=BF16, name="mem_kv")
    k_mem, v_mem = _to_heads(kv[:, :memw], n_mem), _to_heads(kv[:, memw:], n_mem)
    o_mem, lse_mem = _attn_fwd("mem", q_mem, k_mem, v_mem)
    o_sb_m, o_fx_m, o_mem_m = _from_heads(o_sb), _from_heads(o_fx), _from_heads(o_mem)
    h2, zmix, merged = _mix_fwd_out(o_sb_m, o_fx_m, o_mem_m, sg, wl["w_br_sb"], wl["w_br_fox"], wl["w_br_mem"],
                                    wl["w_out"], h1, wl["mix_post_g"])

    n2, gate2, up2, a2 = _ffn_fwd_up(h2, wl["ffn2_pre_g"], wl["ffn2_w_gate"], wl["ffn2_w_up"])
    h3, f2 = _ffn_fwd_down(a2, wl["ffn2_w_down"], h2, wl["ffn2_post_g"])
    saved = dict(h0=h0, n1=n1, gate1=gate1, up1=up1, a1=a1, f1=f1, h1=h1, u=u, fl=fl, sg=sg,
                 q_sb=q_sb, k_sb=k_sb, v_sb=v_sb, q_fx=q_fx, k_fx=k_fx, v_fx=v_fx, q_mem=q_mem,
                 k_mem=k_mem, v_mem=v_mem, ccol=ccol, crow=crow, o_sb=o_sb, o_fx=o_fx, o_mem=o_mem,
                 tot_sb=tot_sb, lse_fx=lse_fx, lse_mem=lse_mem, o_sb_m=o_sb_m, o_fx_m=o_fx_m, o_mem_m=o_mem_m,
                 zmix=zmix, merged=merged, h2=h2, n2=n2, gate2=gate2, up2=up2, a2=a2, f2=f2)
    return h3, saved


def _ffn_bwd(dh, sv, wl, tag, h_in):
    n, gate, up, a, f = (sv[k + tag] for k in ("n", "gate", "up", "a", "f"))
    pre = "ffn" + tag
    df, dgate, dup, dg_post = _ffn_bwd_down(dh, f, wl[pre + "_post_g"], wl[pre + "_w_down"], gate, up)
    dh_in, dg_pre = _ffn_bwd_up(dgate, dup, wl[pre + "_w_gate"], wl[pre + "_w_up"], h_in, wl[pre + "_pre_g"], dh)
    grads = {pre + "_post_g": dg_post, pre + "_pre_g": dg_pre,
             pre + "_w_down": _matmul(a, df, ta=True, name="dw_down"),
             pre + "_w_gate": _matmul(n, dgate, ta=True, name="dw_gate"),
             pre + "_w_up": _matmul(n, dup, ta=True, name="dw_up")}
    return dh_in, grads


def _layer_bwd(dh3, mem_n, wl, sv, dims):
    n_sb, n_fx, n_mem, sbw, fxw, memw = dims
    S = dh3.shape[0]
    dh2, grads = _ffn_bwd(dh3, sv, wl, "2", sv["h2"])

    (dz, db_sb, db_fx, db_mem, do_sb, do_fx, do_mem, dgp, db_gate, dg_post) = _mix_bwd_out(
        dh2, sv["zmix"], wl["mix_post_g"], wl["w_out"], sv["o_sb_m"], sv["o_fx_m"], sv["o_mem_m"],
        wl["w_br_sb"], wl["w_br_fox"], wl["w_br_mem"], sv["sg"])
    grads["mix_post_g"] = dg_post
    grads["b_gate"] = db_gate
    grads["w_out"] = _matmul(sv["merged"], dz, ta=True, name="dw_out")
    grads["w_br_sb"] = _matmul(sv["o_sb_m"], db_sb, ta=True, name="dw_br_sb")
    grads["w_br_fox"] = _matmul(sv["o_fx_m"], db_fx, ta=True, name="dw_br_fox")
    grads["w_br_mem"] = _matmul(sv["o_mem_m"], db_mem, ta=True, name="dw_br_mem")

    dq_sb, dk_sb, dv_sb = _attn_bwd("sb", sv["q_sb"], sv["k_sb"], sv["v_sb"], sv["o_sb"], _to_heads(do_sb, n_sb),
                                    lse=sv["tot_sb"])
    dq_fx, dk_fx, dv_fx, dcrow, dccol = _attn_bwd("fox", sv["q_fx"], sv["k_fx"], sv["v_fx"], sv["o_fx"],
                                                  _to_heads(do_fx, n_fx), sv["ccol"], sv["crow"], sv["lse_fx"])
    dq_mem, dk_mem, dv_mem = _attn_bwd("mem", sv["q_mem"], sv["k_mem"], sv["v_mem"], sv["o_mem"],
                                       _to_heads(do_mem, n_mem), lse=sv["lse_mem"])
    dkv = jnp.concatenate([_from_heads(dk_mem), _from_heads(dv_mem)], axis=1)
    grads["w_mem_kv"] = _matmul(mem_n, dkv, ta=True, name="dw_mem_kv")
    dmem_n = _matmul(dkv, wl["w_mem_kv"], tb=True, out_dtype=F32, name="dmem_n")

    dc = jnp.pad((dcrow.reshape(n_fx, S) + dccol.reshape(n_fx, S)).T, ((0, 0), (0, LANE - n_fx)))
    dfl, db_forget = _fox_dlogit(dc, sv["fl"])
    grads["b_forget"] = db_forget
    dproj = jnp.concatenate([_from_heads(t) for t in (dq_sb, dk_sb, dv_sb, dq_fx, dk_fx, dv_fx, dq_mem)] + [dfl],
                            axis=1)
    dh1, dg_pre = _mix_bwd_in(dproj, dgp, wl["w_in"], wl["w_gate"], sv["h1"], wl["mix_pre_g"], dh2)
    grads["mix_pre_g"] = dg_pre
    grads["w_in"] = _matmul(sv["u"], dproj, ta=True, name="dw_in")
    grads["w_gate"] = _matmul(sv["u"], dgp, ta=True, name="dw_gate_mix")

    dh0, g1 = _ffn_bwd(dh1, sv, wl, "1", sv["h0"])
    grads.update(g1)
    return dh0, grads, dmem_n


def kernel(x, mem, ffn1_pre_g, ffn1_post_g, ffn1_w_gate, ffn1_w_up, ffn1_w_down, mix_pre_g, mix_post_g, w_in, b_forget, mem_norm_g, w_mem_kv, w_gate, b_gate, w_br_sb, w_br_fox, w_br_mem, w_out, ffn2_pre_g, ffn2_post_g, ffn2_w_gate, ffn2_w_up, ffn2_w_down, loss_target, m_ffn1_pre_g, m_ffn1_post_g, m_ffn1_w_gate, m_ffn1_w_up, m_ffn1_w_down, m_mix_pre_g, m_mix_post_g, m_w_in, m_b_forget, m_mem_norm_g, m_w_mem_kv, m_w_gate, m_b_gate, m_w_br_sb, m_w_br_fox, m_w_br_mem, m_w_out, m_ffn2_pre_g, m_ffn2_post_g, m_ffn2_w_gate, m_ffn2_w_up, m_ffn2_w_down, v_ffn1_pre_g, v_ffn1_post_g, v_ffn1_w_gate, v_ffn1_w_up, v_ffn1_w_down, v_mix_pre_g, v_mix_post_g, v_w_in, v_b_forget, v_mem_norm_g, v_w_mem_kv, v_w_gate, v_b_gate, v_w_br_sb, v_w_br_fox, v_w_br_mem, v_w_out, v_ffn2_pre_g, v_ffn2_post_g, v_ffn2_w_gate, v_ffn2_w_up, v_ffn2_w_down):
    args = dict(locals())
    w = {n: args[n] for n in WEIGHTS}
    m = {n: args["m_" + n] for n in WEIGHTS}
    v = {n: args["v_" + n] for n in WEIGHTS}
    L = w["ffn1_pre_g"].shape[0]
    Lh = L // 2
    D = x.shape[2]
    sbw, fxw, memw = w["w_br_sb"].shape[1], w["w_br_fox"].shape[1], w["w_br_mem"].shape[1]
    n_sb, n_fx, n_mem = sbw // HEAD_DIM, fxw // HEAD_DIM, memw // MEM_HEAD_DIM
    dims = (n_sb, n_fx, n_mem, sbw, fxw, memw)
    qkv_w = 3 * sbw + 3 * fxw
    c_idx = lax.axis_index("c")
    c_arr = c_idx.reshape(1).astype(jnp.int32)
    chip_arr = (2 * lax.axis_index("x") + lax.axis_index("y")).reshape(1).astype(jnp.int32)

    local_shapes = [(Lh,) + w[n].shape[1:] for n, _ in BIG]
    offsets = _flat_offsets(local_shapes)
    loc = jnp.stack([_pack_flat([w[n][hf * Lh:(hf + 1) * Lh] for n, _ in BIG], BF16) for hf in range(2)])
    gathered = _gather_weights(loc, _place_own(loc, chip_arr))
    gathered = gathered.reshape(N_CHIPS, 2, -1)

    def layer_weights(l):
        hf, li = divmod(l, Lh)
        wl = {}
        for (n, axis), shp, off in zip(BIG, local_shapes, offsets):
            a, b = shp[1:]
            shards = gathered[:, hf, off + li * a * b:off + (li + 1) * a * b].reshape(N_CHIPS, a, b)
            wl[n] = shards.transpose(1, 0, 2).reshape(a, N_CHIPS * b) if axis == 2 else shards.reshape(N_CHIPS * a, b)
        wi = wl["w_in"]
        wl["w_in"] = jnp.concatenate([wi[:, :qkv_w], wi[:, qkv_w + n_fx:], wi[:, qkv_w:qkv_w + n_fx],
                                      jnp.zeros((D, LANE - n_fx), BF16)], axis=1)
        for n in SMALL:
            if n != "mem_norm_g":
                wl[n] = w[n][l][None, :]
        wl["b_forget"] = jnp.pad(wl["b_forget"], ((0, 0), (0, LANE - n_fx)))
        return wl

    g_mem = w["mem_norm_g"][None, :]

    mem_n = _mem_norm(mem[0], g_mem)
    h, wls, saved = x[0], [], []
    for l in range(L):
        wls.append(layer_weights(l))
        h, sv = _layer_fwd(h, mem_n, wls[l], dims)
        saved.append(sv)
    dh, loss_tile = _loss_head(h, loss_target[0])
    loss = lax.psum(loss_tile[0, 0], ("x", "y", "c"))
    gl, dmem_n = [None] * L, [None] * L
    for l in reversed(range(L)):
        dh, gl[l], dmem_n[l] = _layer_bwd(dh, mem_n, wls[l], saved[l], dims)
        gi = gl[l]["w_in"]
        gl[l]["w_in"] = jnp.concatenate([gi[:, :qkv_w], gi[:, qkv_w + memw:qkv_w + memw + n_fx],
                                         gi[:, qkv_w:qkv_w + memw]], axis=1)
    grad_x = dh
    g_mem_norm = _mem_norm_bwd(mem[0], g_mem, jnp.stack(dmem_n))

    partial = jnp.stack([
        jnp.stack([_pack_flat([jnp.stack([_slab(gl[l][n], axis, j) for l in range(hf * Lh, (hf + 1) * Lh)])
                               for n, axis in BIG], BF16) for j in range(N_CHIPS)]) for hf in range(2)])
    sib = _pair_exchange(partial)
    pair = _pair_sum(partial, sib, c_arr)
    mine = _chip_sum(pair, _chip_exchange(pair), chip_arr)
    theirs = _pair_swap(mine)
    mine, theirs = mine.reshape(-1), theirs.reshape(-1)
    grad = {}
    for (n, _), shp, off in zip(BIG, local_shapes, offsets):
        size = math.prod(shp)
        halves = [jnp.where(c_idx == hf, mine[off:off + size], theirs[off:off + size]) for hf in range(2)]
        grad[n] = jnp.concatenate(halves).reshape((L,) + shp[1:])

    small_local = {n: (g_mem_norm if n == "mem_norm_g" else
                       jnp.concatenate([gl[l][n][:, :n_fx] if n == "b_forget" else gl[l][n] for l in range(L)]))
                   for n in SMALL}
    small_shapes = [small_local[n].shape for n in SMALL]
    small_sum = _unpack_flat(_all_reduce_small(_pack_flat([small_local[n] for n in SMALL], F32, row_block=16)),
                             small_shapes)
    for n, t in zip(SMALL, small_sum):
        grad[n] = t.reshape(w[n].shape)

    delta, new_m, new_v = {}, {}, {}
    for n in WEIGHTS:
        shp = w[n].shape
        two_d = (1, shp[0]) if len(shp) == 1 else (-1, shp[-1])
        d_, m_, v_ = _adamw(w[n].reshape(two_d), grad[n].reshape(two_d), m[n].reshape(two_d), v[n].reshape(two_d),
                            name="adamw_" + n)
        delta[n], new_m[n], new_v[n] = d_.reshape(shp), m_.reshape(shp), v_.reshape(shp)

    return (loss, grad_x[None], *[grad[n] for n in WEIGHTS], *[delta[n] for n in WEIGHTS],
            *[new_m[n] for n in WEIGHTS], *[new_v[n] for n in WEIGHTS])
```

```python
import math

import jax
import jax.numpy as jnp
from jax import lax
from jax.experimental import pallas as pl
from jax.experimental.pallas import tpu as pltpu

F32 = jnp.float32
BF16 = jnp.bfloat16
RMS_EPS = 1e-6
HEAD_DIM = 64
MEM_HEAD_DIM = 128
LANE = 128
V7X_VMEM_LIMIT_BYTES = 56 * 1024 * 1024
FLAT_COLS = 512
FLAT_UNIT = 16 * FLAT_COLS
FLAT_ROW_BLOCK = 512
N_CHIPS = 4
N_DEV = 8
NEG = float(jnp.finfo(jnp.float32).min)

ADAM_LR = 0.001
ADAM_B1 = 0.9
ADAM_B2 = 0.999
ADAM_EPS = 1e-08
ADAM_WD = 0.01
ADAM_STEP = 10

BIG = (("ffn1_w_gate", 2), ("ffn1_w_up", 2), ("ffn1_w_down", 1), ("w_in", 2), ("w_mem_kv", 1), ("w_gate", 2),
       ("w_br_sb", 2), ("w_br_fox", 2), ("w_br_mem", 2), ("w_out", 1),
       ("ffn2_w_gate", 2), ("ffn2_w_up", 2), ("ffn2_w_down", 1))
SMALL = ("ffn1_pre_g", "ffn1_post_g", "mix_pre_g", "mix_post_g", "b_forget", "mem_norm_g", "b_gate",
         "ffn2_pre_g", "ffn2_post_g")
WEIGHTS = ("ffn1_pre_g", "ffn1_post_g", "ffn1_w_gate", "ffn1_w_up", "ffn1_w_down", "mix_pre_g", "mix_post_g", "w_in",
           "b_forget", "mem_norm_g", "w_mem_kv", "w_gate", "b_gate", "w_br_sb", "w_br_fox", "w_br_mem", "w_out",
           "ffn2_pre_g", "ffn2_post_g", "ffn2_w_gate", "ffn2_w_up", "ffn2_w_down")


def _params(**kw):
    return pltpu.CompilerParams(vmem_limit_bytes=V7X_VMEM_LIMIT_BYTES, **kw)


def _dot(a, b):
    return jnp.dot(a, b, preferred_element_type=F32)


def _dot_nt(a, b):
    return lax.dot_general(a, b, (((1,), (1,)), ((), ())), preferred_element_type=F32)


def _dot_tn(a, b):
    return lax.dot_general(a, b, (((0,), (0,)), ((), ())), preferred_element_type=F32)


def _rms(t, g):
    return t * lax.rsqrt(jnp.mean(t * t, axis=-1, keepdims=True) + RMS_EPS) * g


def _pick(dim, pref):
    if dim <= pref:
        return dim
    for cand in range(pref - pref % LANE, 0, -LANE):
        if dim % cand == 0:
            return cand
    return dim


def _rows(bm, cols):
    return pl.BlockSpec((bm, cols), lambda i: (i, 0))


def _whole(shape):
    nd = len(shape)
    return pl.BlockSpec(shape, lambda i: (0,) * nd)


def _split3(x):
    hi = x.astype(BF16)
    r1 = x - hi.astype(F32)
    mid = r1.astype(BF16)
    lo = (r1 - mid.astype(F32)).astype(BF16)
    return hi, mid, lo


def _cumdot(x, tri):
    hi = x.astype(BF16)
    lo = (x - hi.astype(F32)).astype(BF16)
    return _dot(hi, tri) + _dot(lo, tri)


def _ffn_fwd_up(h, g_pre, wg, wu):
    S, D = h.shape
    F = wg.shape[1]
    bm = _pick(S, 256)

    def body(h_ref, g_ref, wg_ref, wu_ref, n_ref, gate_ref, up_ref, a_ref):
        n = _rms(h_ref[...], g_ref[...]).astype(BF16)
        n_ref[...] = n
        gate = _dot(n, wg_ref[...])
        up = _dot(n, wu_ref[...])
        gate_ref[...] = gate.astype(BF16)
        up_ref[...] = up.astype(BF16)
        a_ref[...] = (gate * jax.nn.sigmoid(gate) * up).astype(BF16)

    return pl.pallas_call(
        body, name="ffn_fwd_up", grid=(S // bm,),
        in_specs=[_rows(bm, D), _whole((1, D)), _whole((D, F)), _whole((D, F))],
        out_specs=[_rows(bm, D), _rows(bm, F), _rows(bm, F), _rows(bm, F)],
        out_shape=[jax.ShapeDtypeStruct((S, D), BF16)] + [jax.ShapeDtypeStruct((S, F), BF16)] * 3,
        compiler_params=_params(),
    )(h, g_pre, wg, wu)


def _ffn_fwd_down(a, wd, h, g_post):
    S, F = a.shape
    D = wd.shape[1]
    bm = _pick(S, 256)

    def body(a_ref, wd_ref, h_ref, g_ref, hout_ref, f_ref):
        f = _dot(a_ref[...], wd_ref[...])
        f_ref[...] = f
        hout_ref[...] = h_ref[...] + 0.5 * _rms(f, g_ref[...])

    return pl.pallas_call(
        body, name="ffn_fwd_down", grid=(S // bm,),
        in_specs=[_rows(bm, F), _whole((F, D)), _rows(bm, D), _whole((1, D))],
        out_specs=[_rows(bm, D), _rows(bm, D)],
        out_shape=[jax.ShapeDtypeStruct((S, D), F32)] * 2,
        compiler_params=_params(),
    )(a, wd, h, g_post)


def _ffn_bwd_down(dh, f, g_post, wd, gate, up):
    S, D = dh.shape
    F = wd.shape[0]
    bm = _pick(S, 256)

    def body(dh_ref, f_ref, g_ref, wd_ref, gate_ref, up_ref, df_ref, dgate_ref, dup_ref, dg_ref):
        _, vjp = jax.vjp(lambda t, g: 0.5 * _rms(t, g), f_ref[...], g_ref[...])
        df, dg = vjp(dh_ref[...])

        @pl.when(pl.program_id(0) == 0)
        def _():
            dg_ref[...] = jnp.zeros_like(dg_ref)

        dg_ref[...] += dg
        dfb = df.astype(BF16)
        df_ref[...] = dfb
        da = _dot_nt(dfb, wd_ref[...])
        gt = gate_ref[...].astype(F32)
        sig = jax.nn.sigmoid(gt)
        silu = gt * sig
        dup_ref[...] = (da * silu).astype(BF16)
        dgate_ref[...] = (da * up_ref[...].astype(F32) * (sig + silu * (1.0 - sig))).astype(BF16)

    return pl.pallas_call(
        body, name="ffn_bwd_down", grid=(S // bm,),
        in_specs=[_rows(bm, D), _rows(bm, D), _whole((1, D)), _whole((F, D)), _rows(bm, F), _rows(bm, F)],
        out_specs=[_rows(bm, D), _rows(bm, F), _rows(bm, F), _whole((1, D))],
        out_shape=[jax.ShapeDtypeStruct((S, D), BF16), jax.ShapeDtypeStruct((S, F), BF16),
                   jax.ShapeDtypeStruct((S, F), BF16), jax.ShapeDtypeStruct((1, D), F32)],
        compiler_params=_params(),
    )(dh, f, g_post, wd, gate, up)


def _ffn_bwd_up(dgate, dup, wg, wu, h_in, g_pre, dh):
    S, F = dgate.shape
    D = wg.shape[0]
    bm = _pick(S, 256)

    def body(dgate_ref, dup_ref, wg_ref, wu_ref, h_ref, g_ref, dh_ref, dhin_ref, dg_ref):
        dn = _dot_nt(dgate_ref[...], wg_ref[...]) + _dot_nt(dup_ref[...], wu_ref[...])
        _, vjp = jax.vjp(_rms, h_ref[...], g_ref[...])
        dhx, dg = vjp(dn)

        @pl.when(pl.program_id(0) == 0)
        def _():
            dg_ref[...] = jnp.zeros_like(dg_ref)

        dg_ref[...] += dg
        dhin_ref[...] = dh_ref[...] + dhx

    return pl.pallas_call(
        body, name="ffn_bwd_up", grid=(S // bm,),
        in_specs=[_rows(bm, F), _rows(bm, F), _whole((D, F)), _whole((D, F)), _rows(bm, D), _whole((1, D)),
                  _rows(bm, D)],
        out_specs=[_rows(bm, D), _whole((1, D))],
        out_shape=[jax.ShapeDtypeStruct((S, D), F32), jax.ShapeDtypeStruct((1, D), F32)],
        compiler_params=_params(),
    )(dgate, dup, wg, wu, h_in, g_pre, dh)


def _matmul(a, b, *, ta=False, tb=False, out_dtype=BF16, name):
    M, K = (a.shape[1], a.shape[0]) if ta else a.shape
    N = b.shape[0] if tb else b.shape[1]
    bm, bk = _pick(M, 512), _pick(K, 512)
    bn = N if N * bm * 4 <= 8 * 1024 * 1024 else _pick(N, 1536)
    nk = K // bk

    def body(a_ref, b_ref, o_ref, acc_ref):
        kk = pl.program_id(2)

        @pl.when(kk == 0)
        def _():
            acc_ref[...] = jnp.zeros_like(acc_ref)

        av, bv = a_ref[...], b_ref[...]
        dims = (((0 if ta else 1,), (1 if tb else 0,)), ((), ()))
        acc_ref[...] += lax.dot_general(av, bv, dims, preferred_element_type=F32)

        @pl.when(kk == nk - 1)
        def _():
            o_ref[...] = acc_ref[...].astype(o_ref.dtype)

    a_spec = pl.BlockSpec((bk, bm), lambda i, j, k: (k, i)) if ta else pl.BlockSpec((bm, bk), lambda i, j, k: (i, k))
    b_spec = pl.BlockSpec((bn, bk), lambda i, j, k: (j, k)) if tb else pl.BlockSpec((bk, bn), lambda i, j, k: (k, j))
    return pl.pallas_call(
        body, name=name, grid=(M // bm, N // bn, nk),
        in_specs=[a_spec, b_spec],
        out_specs=pl.BlockSpec((bm, bn), lambda i, j, k: (i, j)),
        out_shape=jax.ShapeDtypeStruct((M, N), out_dtype),
        scratch_shapes=[pltpu.VMEM((bm, bn), F32)],
        compiler_params=_params(),
    )(a, b)


def _mix_fwd_in(h, g_pre, win, wgate, b_gate, b_forget):
    S, D = h.shape
    PW = win.shape[1] - LANE
    G = wgate.shape[1]
    bm = _pick(S, 256)

    def body(h_ref, g_ref, win_ref, wgate_ref, bg_ref, bf_ref, u_ref, proj_ref, fl_ref, sg_ref):
        u = _rms(h_ref[...], g_ref[...]).astype(BF16)
        u_ref[...] = u
        proj = _dot(u, win_ref[...])
        proj_ref[...] = proj[:, :PW].astype(BF16)
        fl_ref[...] = proj[:, PW:] + bf_ref[...]
        sg_ref[...] = jax.nn.sigmoid(_dot(u, wgate_ref[...]) + bg_ref[...]).astype(BF16)

    return pl.pallas_call(
        body, name="mix_fwd_in", grid=(S // bm,),
        in_specs=[_rows(bm, D), _whole((1, D)), _whole((D, PW + LANE)), _whole((D, G)), _whole((1, G)),
                  _whole((1, LANE))],
        out_specs=[_rows(bm, D), _rows(bm, PW), _rows(bm, LANE), _rows(bm, G)],
        out_shape=[jax.ShapeDtypeStruct((S, D), BF16), jax.ShapeDtypeStruct((S, PW), BF16),
                   jax.ShapeDtypeStruct((S, LANE), F32), jax.ShapeDtypeStruct((S, G), BF16)],
        compiler_params=_params(),
    )(h, g_pre, win, wgate, b_gate, b_forget)


def _mix_fwd_out(o_sb, o_fx, o_mem, sg, w_sb, w_fx, w_mem, w_out, h, g_post):
    S, D = h.shape
    bm = _pick(S, 256)
    widths = (o_sb.shape[1], o_fx.shape[1], o_mem.shape[1])

    def body(osb_ref, ofx_ref, omem_ref, sg_ref, wsb_ref, wfx_ref, wmem_ref, wout_ref, h_ref, g_ref,
             hout_ref, z_ref, merged_ref):
        s = sg_ref[...].astype(F32)
        merged = (s[:, :D] * _dot(osb_ref[...], wsb_ref[...]) + s[:, D:2 * D] * _dot(ofx_ref[...], wfx_ref[...])
                  + s[:, 2 * D:] * _dot(omem_ref[...], wmem_ref[...]))
        mb = merged.astype(BF16)
        merged_ref[...] = mb
        z = _dot(mb, wout_ref[...])
        z_ref[...] = z
        hout_ref[...] = h_ref[...] + _rms(z, g_ref[...])

    return pl.pallas_call(
        body, name="mix_fwd_out", grid=(S // bm,),
        in_specs=[_rows(bm, widths[0]), _rows(bm, widths[1]), _rows(bm, widths[2]), _rows(bm, 3 * D),
                  _whole((widths[0], D)), _whole((widths[1], D)), _whole((widths[2], D)), _whole((D, D)),
                  _rows(bm, D), _whole((1, D))],
        out_specs=[_rows(bm, D), _rows(bm, D), _rows(bm, D)],
        out_shape=[jax.ShapeDtypeStruct((S, D), F32), jax.ShapeDtypeStruct((S, D), F32),
                   jax.ShapeDtypeStruct((S, D), BF16)],
        compiler_params=_params(),
    )(o_sb, o_fx, o_mem, sg, w_sb, w_fx, w_mem, w_out, h, g_post)


def _mix_bwd_out(dh, z, g_post, w_out, o_sb, o_fx, o_mem, w_sb, w_fx, w_mem, sg):
    S, D = dh.shape
    bm = _pick(S, 256)
    widths = (o_sb.shape[1], o_fx.shape[1], o_mem.shape[1])

    def body(dh_ref, z_ref, g_ref, wout_ref, osb_ref, ofx_ref, omem_ref, wsb_ref, wfx_ref, wmem_ref, sg_ref,
             dz_ref, dbsb_ref, dbfx_ref, dbmem_ref, dosb_ref, dofx_ref, domem_ref, dgp_ref, dbg_ref, dg_ref):
        _, vjp = jax.vjp(_rms, z_ref[...], g_ref[...])
        dz, dg = vjp(dh_ref[...])

        @pl.when(pl.program_id(0) == 0)
        def _():
            dg_ref[...] = jnp.zeros_like(dg_ref)
            dbg_ref[...] = jnp.zeros_like(dbg_ref)

        dg_ref[...] += dg
        dzb = dz.astype(BF16)
        dz_ref[...] = dzb
        dmerged = _dot_nt(dzb, wout_ref[...])
        s = sg_ref[...].astype(F32)
        branches = ((osb_ref, wsb_ref, dbsb_ref, dosb_ref), (ofx_ref, wfx_ref, dbfx_ref, dofx_ref),
                    (omem_ref, wmem_ref, dbmem_ref, domem_ref))
        for k, (o_ref, w_ref, db_ref, do_ref) in enumerate(branches):
            gs = s[:, k * D:(k + 1) * D]
            dbb = (dmerged * gs).astype(BF16)
            db_ref[...] = dbb
            do_ref[...] = _dot_nt(dbb, w_ref[...]).astype(BF16)
            dgp = dmerged * _dot(o_ref[...], w_ref[...]) * gs * (1.0 - gs)
            dgp_ref[:, k * D:(k + 1) * D] = dgp.astype(BF16)
            dbg_ref[:, k * D:(k + 1) * D] += jnp.sum(dgp, axis=0, keepdims=True)

    return pl.pallas_call(
        body, name="mix_bwd_out", grid=(S // bm,),
        in_specs=[_rows(bm, D), _rows(bm, D), _whole((1, D)), _whole((D, D)),
                  _rows(bm, widths[0]), _rows(bm, widths[1]), _rows(bm, widths[2]),
                  _whole((widths[0], D)), _whole((widths[1], D)), _whole((widths[2], D)), _rows(bm, 3 * D)],
        out_specs=[_rows(bm, D)] * 4 + [_rows(bm, widths[0]), _rows(bm, widths[1]), _rows(bm, widths[2]),
                                        _rows(bm, 3 * D), _whole((1, 3 * D)), _whole((1, D))],
        out_shape=[jax.ShapeDtypeStruct((S, D), BF16)] * 4
        + [jax.ShapeDtypeStruct((S, w), BF16) for w in widths]
        + [jax.ShapeDtypeStruct((S, 3 * D), BF16), jax.ShapeDtypeStruct((1, 3 * D), F32),
           jax.ShapeDtypeStruct((1, D), F32)],
        compiler_params=_params(),
    )(dh, z, g_post, w_out, o_sb, o_fx, o_mem, w_sb, w_fx, w_mem, sg)


def _mix_bwd_in(dproj, dgp, win, wgate, h_in, g_pre, dh):
    S, PWL = dproj.shape
    G = dgp.shape[1]
    D = h_in.shape[1]
    bm = _pick(S, 256)

    def body(dproj_ref, dgp_ref, win_ref, wgate_ref, h_ref, g_ref, dh_ref, dhin_ref, dg_ref):
        du = _dot_nt(dproj_ref[...], win_ref[...]) + _dot_nt(dgp_ref[...], wgate_ref[...])
        _, vjp = jax.vjp(_rms, h_ref[...], g_ref[...])
        dhx, dg = vjp(du)

        @pl.when(pl.program_id(0) == 0)
        def _():
            dg_ref[...] = jnp.zeros_like(dg_ref)

        dg_ref[...] += dg
        dhin_ref[...] = dh_ref[...] + dhx

    return pl.pallas_call(
        body, name="mix_bwd_in", grid=(S // bm,),
        in_specs=[_rows(bm, PWL), _rows(bm, G), _whole((D, PWL)), _whole((D, G)), _rows(bm, D), _whole((1, D)),
                  _rows(bm, D)],
        out_specs=[_rows(bm, D), _whole((1, D))],
        out_shape=[jax.ShapeDtypeStruct((S, D), F32), jax.ShapeDtypeStruct((1, D), F32)],
        compiler_params=_params(),
    )(dproj, dgp, win, wgate, h_in, g_pre, dh)


def _log_sigmoid(x):
    return jnp.minimum(x, 0.0) - jnp.log(1.0 + jnp.exp(-jnp.abs(x)))


def _fox_cumsum(fl):
    S = fl.shape[0]
    rb = _pick(S, LANE)

    def body(fl_ref, c_ref, carry_ref):
        @pl.when(pl.program_id(0) == 0)
        def _():
            carry_ref[...] = jnp.zeros_like(carry_ref)

        r = lax.broadcasted_iota(jnp.int32, (rb, rb), 0)
        cidx = lax.broadcasted_iota(jnp.int32, (rb, rb), 1)
        tri = (cidx <= r).astype(BF16)
        hi, mid, lo = _split3(_log_sigmoid(fl_ref[...]))
        c = _dot(tri, hi) + _dot(tri, mid) + _dot(tri, lo) + carry_ref[...]
        c_ref[...] = c
        carry_ref[...] = c[rb - 1:rb, :]

    return pl.pallas_call(
        body, name="fox_cumsum", grid=(S // rb,),
        in_specs=[_rows(rb, LANE)], out_specs=_rows(rb, LANE),
        out_shape=jax.ShapeDtypeStruct((S, LANE), F32),
        scratch_shapes=[pltpu.VMEM((1, LANE), F32)],
        compiler_params=_params(),
    )(fl)


def _fox_dlogit(dc, fl):
    S = fl.shape[0]
    rb = _pick(S, LANE)
    nb = S // rb

    def body(dc_ref, fl_ref, dfl_ref, dbf_ref, carry_ref):
        @pl.when(pl.program_id(0) == 0)
        def _():
            carry_ref[...] = jnp.zeros_like(carry_ref)
            dbf_ref[...] = jnp.zeros_like(dbf_ref)

        r = lax.broadcasted_iota(jnp.int32, (rb, rb), 0)
        cidx = lax.broadcasted_iota(jnp.int32, (rb, rb), 1)
        tri = (cidx >= r).astype(BF16)
        hi, mid, lo = _split3(dc_ref[...])
        rc = _dot(tri, hi) + _dot(tri, mid) + _dot(tri, lo) + carry_ref[...]
        carry_ref[...] = rc[0:1, :]
        dfl = rc * jax.nn.sigmoid(-fl_ref[...])
        dfl_ref[...] = dfl.astype(BF16)
        dbf_ref[...] += jnp.sum(dfl, axis=0, keepdims=True)

    rev = pl.BlockSpec((rb, LANE), lambda i: (nb - 1 - i, 0))
    return pl.pallas_call(
        body, name="fox_dlogit", grid=(nb,),
        in_specs=[rev, rev], out_specs=[rev, _whole((1, LANE))],
        out_shape=[jax.ShapeDtypeStruct((S, LANE), BF16), jax.ShapeDtypeStruct((1, LANE), F32)],
        scratch_shapes=[pltpu.VMEM((1, LANE), F32)],
        compiler_params=_params(),
    )(dc, fl)


def _attn_blocks(kind, S, Sk):
    tq = _pick(S, 512)
    tc = LANE if kind == "sb" else _pick(Sk, 256)
    return tq, tc


def _heads_per_step(n_heads):
    return 2 if n_heads % 2 == 0 else 1


def _is_power_of_two(x):
    return math.frexp(x)[0] == 0.5


def _sb_logs(z):
    ln = -jnp.maximum(z, 0.0) - jnp.log(1.0 + jnp.exp(-jnp.abs(z)))
    return ln + z, ln


def _attn_fwd(kind, q, k, v, ccol=None, crow=None):
    H, S, dh = q.shape
    Sk = k.shape[1]
    tq, tc = _attn_blocks(kind, S, Sk)
    hb = _heads_per_step(H)
    scale = dh ** -0.5
    fold = _is_power_of_two(scale)
    causal = kind != "mem"
    n_diag = tq // tc if causal else 0

    def body(*refs):
        if kind == "fox":
            q_ref, k_ref, v_ref, cc_ref, cr_ref, o_ref, lse_ref = refs
        else:
            q_ref, k_ref, v_ref, o_ref, lse_ref = refs
        i = pl.program_id(1)
        n_full = (i * tq) // tc if causal else Sk // tc
        qpos = i * tq + lax.broadcasted_iota(jnp.int32, (tq, tc), 0)
        kio = lax.broadcasted_iota(jnp.int32, (tq, tc), 1)
        qs = [q_ref[hh] * scale if fold else q_ref[hh] for hh in range(hb)]
        heads = range(hb)

        if kind == "sb":
            tri = (lax.broadcasted_iota(jnp.int32, (tc, tc), 0) > lax.broadcasted_iota(jnp.int32, (tc, tc), 1)
                   ).astype(BF16)

            def chunk(hh, jc, masked, run, acc):
                off = pl.multiple_of(jc * tc, tc)
                ks = k_ref[hh, pl.ds(off, tc), :]
                vs = v_ref[hh, pl.ds(off, tc), :]
                lb, ln = _sb_logs(_dot_nt(qs[hh], ks))
                if masked:
                    mask = (off + kio) < qpos
                    ln = jnp.where(mask, ln, 0.0)
                w = jnp.exp(lb + _cumdot(ln, tri) + run)
                if masked:
                    w = jnp.where(mask, w, 0.0)
                return run + jnp.sum(ln, axis=1, keepdims=True), acc + _dot(w.astype(BF16), vs)

            state = tuple((jnp.zeros((tq, 1), F32), jnp.zeros((tq, dh), F32)) for _ in heads)
            for d in range(n_diag - 1, -1, -1):
                state = tuple(chunk(hh, n_full + d, True, *state[hh]) for hh in heads)
            state = lax.fori_loop(
                0, n_full, lambda t, st: tuple(chunk(hh, n_full - 1 - t, False, *st[hh]) for hh in heads), state)
            for hh in heads:
                o_ref[hh] = state[hh][1].astype(o_ref.dtype)
                lse_ref[hh] = state[hh][0]
        else:
            def chunk(hh, jc, masked, m, l, acc):
                off = pl.multiple_of(jc * tc, tc)
                ks = k_ref[hh, pl.ds(off, tc), :]
                vs = v_ref[hh, pl.ds(off, tc), :]
                z = _dot_nt(qs[hh], ks)
                if not fold:
                    z = z * scale
                if kind == "fox":
                    z = z + cc_ref[hh] - cr_ref[hh, pl.ds(jc, 1), :]
                if masked:
                    z = jnp.where((off + kio) <= qpos, z, NEG)
                m_new = jnp.maximum(m, jnp.max(z, axis=1, keepdims=True))
                alpha = jnp.exp(m - m_new)
                p = jnp.exp(z - m_new)
                return (m_new, alpha * l + jnp.sum(p, axis=1, keepdims=True),
                        alpha * acc + _dot(p.astype(BF16), vs))

            state = tuple((jnp.full((tq, 1), NEG, F32), jnp.zeros((tq, 1), F32), jnp.zeros((tq, dh), F32))
                          for _ in heads)
            state = lax.fori_loop(
                0, n_full, lambda jc, st: tuple(chunk(hh, jc, False, *st[hh]) for hh in heads), state)
            for d in range(n_diag):
                state = tuple(chunk(hh, n_full + d, True, *state[hh]) for hh in heads)
            for hh in heads:
                m, l, acc = state[hh]
                o_ref[hh] = (acc / l).astype(o_ref.dtype)
                lse_ref[hh] = m + jnp.log(l)

    qspec = pl.BlockSpec((hb, tq, dh), lambda h, i: (h, i, 0))
    kspec = pl.BlockSpec((hb, Sk, dh), lambda h, i: (h, 0, 0))
    colspec = pl.BlockSpec((hb, tq, 1), lambda h, i: (h, i, 0))
    in_specs, args = [qspec, kspec, kspec], [q, k, v]
    if kind == "fox":
        in_specs += [colspec, pl.BlockSpec((hb, Sk // tc, tc), lambda h, i: (h, 0, 0))]
        args += [ccol, crow]
    out_specs = [qspec, colspec]
    out_shape = [jax.ShapeDtypeStruct((H, S, dh), BF16), jax.ShapeDtypeStruct((H, S, 1), F32)]
    return pl.pallas_call(
        body, name="attn_fwd_" + kind, grid=(H // hb, S // tq),
        in_specs=in_specs, out_specs=out_specs, out_shape=out_shape,
        compiler_params=_params(),
    )(*args)


def _attn_bwd(kind, q, k, v, o, do, ccol=None, crow=None, lse=None):
    H, S, dh = q.shape
    Sk = k.shape[1]
    tq, tc = _attn_blocks(kind, S, Sk)
    hb = _heads_per_step(H)
    scale = dh ** -0.5
    fold = _is_power_of_two(scale)
    causal = kind != "mem"
    n_diag = tq // tc if causal else 0
    nq = S // tq

    def body(*refs):
        if kind == "fox":
            (q_ref, k_ref, v_ref, o_ref, do_ref, cc_ref, cr_ref, lse_ref,
             dq_ref, dk_ref, dv_ref, dc_ref, dcc_ref, dk_acc, dv_acc, dc_acc) = refs
        else:
            q_ref, k_ref, v_ref, o_ref, do_ref, lse_ref, dq_ref, dk_ref, dv_ref, dk_acc, dv_acc = refs
        i = pl.program_id(1)

        @pl.when(i == 0)
        def _():
            dk_acc[...] = jnp.zeros_like(dk_acc)
            dv_acc[...] = jnp.zeros_like(dv_acc)
            if kind == "fox":
                dc_acc[...] = jnp.zeros_like(dc_acc)

        n_full = (i * tq) // tc if causal else Sk // tc
        qpos = i * tq + lax.broadcasted_iota(jnp.int32, (tq, tc), 0)
        kio = lax.broadcasted_iota(jnp.int32, (tq, tc), 1)
        heads = range(hb)
        qs = [q_ref[hh] * scale if fold else q_ref[hh] for hh in heads]
        dos = [do_ref[hh] for hh in heads]

        if kind == "sb":
            r = lax.broadcasted_iota(jnp.int32, (tc, tc), 0)
            cidx = lax.broadcasted_iota(jnp.int32, (tc, tc), 1)
            tri_inc = (r <= cidx).astype(BF16)
            tri_exc = (r < cidx).astype(BF16)

            def chunk(hh, jc, masked, pre, pre_e, dq):
                off = pl.multiple_of(jc * tc, tc)
                ks = k_ref[hh, pl.ds(off, tc), :]
                vs = v_ref[hh, pl.ds(off, tc), :]
                lb, ln = _sb_logs(_dot_nt(qs[hh], ks))
                if masked:
                    mask = (off + kio) < qpos
                    ln = jnp.where(mask, ln, 0.0)
                w = jnp.exp(lb + (lse_ref[hh] - pre - _cumdot(ln, tri_inc)))
                if masked:
                    w = jnp.where(mask, w, 0.0)
                e = w * _dot_nt(dos[hh], vs)
                beta = jnp.exp(lb)
                dz = e * (1.0 - beta) - beta * (pre_e + _cumdot(e, tri_exc))
                if masked:
                    dz = jnp.where(mask, dz, 0.0)
                dzb = dz.astype(BF16)
                dk_acc[hh, pl.ds(off, tc), :] += _dot_tn(dzb, qs[hh])
                dv_acc[hh, pl.ds(off, tc), :] += _dot_tn(w.astype(BF16), dos[hh])
                return (pre + jnp.sum(ln, axis=1, keepdims=True), pre_e + jnp.sum(e, axis=1, keepdims=True),
                        dq + _dot(dzb, ks))

            state = tuple((jnp.zeros((tq, 1), F32), jnp.zeros((tq, 1), F32), jnp.zeros((tq, dh), F32)) for _ in heads)
            state = lax.fori_loop(
                0, n_full, lambda jc, st: tuple(chunk(hh, jc, False, *st[hh]) for hh in heads), state)
            for d in range(n_diag):
                state = tuple(chunk(hh, n_full + d, True, *state[hh]) for hh in heads)
            dqs = [state[hh][2] for hh in heads]
        else:
            dsum = [jnp.sum(o_ref[hh].astype(F32) * dos[hh].astype(F32), axis=1, keepdims=True) for hh in heads]

            def chunk(hh, jc, masked, dq, rowsum):
                off = pl.multiple_of(jc * tc, tc)
                ks = k_ref[hh, pl.ds(off, tc), :]
                vs = v_ref[hh, pl.ds(off, tc), :]
                z = _dot_nt(qs[hh], ks)
                if not fold:
                    z = z * scale
                if kind == "fox":
                    z = z + cc_ref[hh] - cr_ref[hh, pl.ds(jc, 1), :]
                if masked:
                    z = jnp.where((off + kio) <= qpos, z, NEG)
                p = jnp.exp(z - lse_ref[hh])
                ds = p * (_dot_nt(dos[hh], vs) - dsum[hh])
                dsb = ds.astype(BF16)
                dk_acc[hh, pl.ds(off, tc), :] += _dot_tn(dsb, qs[hh])
                dv_acc[hh, pl.ds(off, tc), :] += _dot_tn(p.astype(BF16), dos[hh])
                if kind == "fox":
                    dc_acc[hh, pl.ds(jc, 1), :] -= jnp.sum(ds, axis=0, keepdims=True)
                    rowsum = rowsum + jnp.sum(ds, axis=1, keepdims=True)
                return dq + _dot(dsb, ks), rowsum

            state = tuple((jnp.zeros((tq, dh), F32), jnp.zeros((tq, 1), F32)) for _ in heads)
            state = lax.fori_loop(
                0, n_full, lambda jc, st: tuple(chunk(hh, jc, False, *st[hh]) for hh in heads), state)
            for d in range(n_diag):
                state = tuple(chunk(hh, n_full + d, True, *state[hh]) for hh in heads)
            dqs = [state[hh][0] for hh in heads]
            if kind == "fox":
                for hh in heads:
                    dcc_ref[hh] = state[hh][1]
        for hh in heads:
            dq_ref[hh] = (dqs[hh] * scale).astype(dq_ref.dtype)

        @pl.when(i == nq - 1)
        def _():
            dk = dk_acc[...] if fold else dk_acc[...] * scale
            dk_ref[...] = dk.astype(dk_ref.dtype)
            dv_ref[...] = dv_acc[...].astype(dv_ref.dtype)
            if kind == "fox":
                dc_ref[...] = dc_acc[...]

    qspec = pl.BlockSpec((hb, tq, dh), lambda h, i: (h, i, 0))
    kspec = pl.BlockSpec((hb, Sk, dh), lambda h, i: (h, 0, 0))
    colspec = pl.BlockSpec((hb, tq, 1), lambda h, i: (h, i, 0))
    rowspec = pl.BlockSpec((hb, Sk // tc, tc), lambda h, i: (h, 0, 0))
    in_specs, args = [qspec, kspec, kspec, qspec, qspec], [q, k, v, o, do]
    if kind == "fox":
        in_specs += [colspec, rowspec]
        args += [ccol, crow]
    in_specs += [colspec]
    args += [lse]
    out_specs = [qspec, kspec, kspec]
    out_shape = [jax.ShapeDtypeStruct((H, S, dh), BF16), jax.ShapeDtypeStruct((H, Sk, dh), BF16),
                 jax.ShapeDtypeStruct((H, Sk, dh), BF16)]
    scratch = [pltpu.VMEM((hb, Sk, dh), F32), pltpu.VMEM((hb, Sk, dh), F32)]
    if kind == "fox":
        out_specs += [rowspec, colspec]
        out_shape += [jax.ShapeDtypeStruct((H, Sk // tc, tc), F32), jax.ShapeDtypeStruct((H, S, 1), F32)]
        scratch.append(pltpu.VMEM((hb, Sk // tc, tc), F32))
    return pl.pallas_call(
        body, name="attn_bwd_" + kind, grid=(H // hb, nq),
        in_specs=in_specs, out_specs=out_specs, out_shape=out_shape, scratch_shapes=scratch,
        compiler_params=_params(),
    )(*args)


def _to_heads(t, n_heads):
    s = t.shape[0]
    return t.reshape(s, n_heads, -1).transpose(1, 0, 2)


def _from_heads(t):
    h, s, d = t.shape
    return t.transpose(1, 0, 2).reshape(s, h * d)


def _mem_norm(mem, g):
    M, D = mem.shape

    def body(mem_ref, g_ref, out_ref):
        out_ref[...] = _rms(mem_ref[...], g_ref[...]).astype(BF16)

    return pl.pallas_call(
        body, name="mem_norm", grid=(1,),
        in_specs=[_whole((M, D)), _whole((1, D))], out_specs=_whole((M, D)),
        out_shape=jax.ShapeDtypeStruct((M, D), BF16), compiler_params=_params(),
    )(mem, g)


def _mem_norm_bwd(mem, g, dmem_n):
    M, D = mem.shape
    L = dmem_n.shape[0]

    def body(mem_ref, g_ref, d_ref, dg_ref):
        d = d_ref[0]
        for l in range(1, L):
            d = d + d_ref[l]
        _, vjp = jax.vjp(_rms, mem_ref[...], g_ref[...])
        dg_ref[...] = vjp(d)[1]

    return pl.pallas_call(
        body, name="mem_norm_bwd", grid=(1,),
        in_specs=[_whole((M, D)), _whole((1, D)), _whole((L, M, D))], out_specs=_whole((1, D)),
        out_shape=jax.ShapeDtypeStruct((1, D), F32), compiler_params=_params(),
    )(mem, g, dmem_n)


def _loss_head(h, target):
    S, D = h.shape
    bm = _pick(S, 512)

    def body(h_ref, t_ref, dh_ref, loss_ref):
        err = h_ref[...] - t_ref[...]
        dh_ref[...] = err * (1.0 / D)

        @pl.when(pl.program_id(0) == 0)
        def _():
            loss_ref[...] = jnp.zeros_like(loss_ref)

        loss_ref[...] += 0.5 * jnp.sum(jnp.mean(err * err, axis=-1, keepdims=True), axis=0, keepdims=True)

    return pl.pallas_call(
        body, name="loss_head", grid=(S // bm,),
        in_specs=[_rows(bm, D), _rows(bm, D)], out_specs=[_rows(bm, D), _whole((8, LANE))],
        out_shape=[jax.ShapeDtypeStruct((S, D), F32), jax.ShapeDtypeStruct((8, LANE), F32)],
        compiler_params=_params(),
    )(h, target)


def _adamw(w, g, m, v, name):
    R, C = w.shape
    rb = R if R * C * 4 <= (1 << 20) else _pick(R, 256)
    if R % rb:
        rb = R
    c1 = 1.0 - ADAM_B1 ** ADAM_STEP
    c2 = 1.0 - ADAM_B2 ** ADAM_STEP

    def body(w_ref, g_ref, m_ref, v_ref, d_ref, mo_ref, vo_ref):
        gv = g_ref[...]
        mn = ADAM_B1 * m_ref[...] + (1.0 - ADAM_B1) * gv
        vn = ADAM_B2 * v_ref[...] + (1.0 - ADAM_B2) * (gv * gv)
        mo_ref[...] = mn
        vo_ref[...] = vn
        d_ref[...] = -ADAM_LR * ((mn / c1) / (jnp.sqrt(vn / c2) + ADAM_EPS) + ADAM_WD * w_ref[...])

    return pl.pallas_call(
        body, name=name, grid=(R // rb,),
        in_specs=[_rows(rb, C)] * 4, out_specs=[_rows(rb, C)] * 3,
        out_shape=[jax.ShapeDtypeStruct((R, C), F32)] * 3, compiler_params=_params(),
    )(w, g, m, v)


ANY = pl.BlockSpec(memory_space=pl.ANY)
MESH = pl.DeviceIdType.MESH


def _place():
    x, y, c = lax.axis_index("x"), lax.axis_index("y"), lax.axis_index("c")
    others = [(1 - x, y), (x, 1 - y), (1 - x, 1 - y)]
    return x, y, c, others


def _place_own(loc, chip_idx):
    _, R, C = loc.shape
    rb = _pick(R, 2 * FLAT_ROW_BLOCK)

    def body(chip_ref, loc_ref, out_ref):
        out_ref[...] = loc_ref[...]

    return pl.pallas_call(
        body, name="place_own",
        grid_spec=pltpu.PrefetchScalarGridSpec(
            num_scalar_prefetch=1, grid=(2, R // rb),
            in_specs=[pl.BlockSpec((None, rb, C), lambda hf, i, chip_ref: (hf, i, 0))],
            out_specs=pl.BlockSpec((None, None, rb, C), lambda hf, i, chip_ref: (chip_ref[0], hf, i, 0))),
        out_shape=jax.ShapeDtypeStruct((N_CHIPS, 2, R, C), loc.dtype), compiler_params=_params(),
    )(chip_idx, loc)


def _gather_weights(loc, own):
    _, R, C = loc.shape

    def body(loc_ref, own_ref, out_ref, send_sems, recv_sems):
        del own_ref
        x, y, c, others = _place()
        me = 2 * x + y
        sibling = (x, y, 1 - c)

        def copy(k, src, dst, to):
            return pltpu.make_async_remote_copy(src_ref=src, dst_ref=dst, send_sem=send_sems.at[k],
                                                recv_sem=recv_sems.at[k], device_id=to, device_id_type=MESH)

        first = [copy(j, loc_ref.at[c], out_ref.at[me, c], (ox, oy, c)) for j, (ox, oy) in enumerate(others)]
        for cp in first:
            cp.start()
        passed = []
        for j, (ox, oy) in enumerate(others):
            landed = out_ref.at[2 * ox + oy, c]
            copy(j, loc_ref.at[c], landed, sibling).wait_recv()
            cp = copy(3 + j, landed, landed, sibling)
            cp.start()
            passed.append(cp)
        for j, (ox, oy) in enumerate(others):
            copy(3 + j, loc_ref.at[c], out_ref.at[2 * ox + oy, 1 - c], sibling).wait_recv()
        for cp in first + passed:
            cp.wait_send()

    return pl.pallas_call(
        body, name="gather_weights", in_specs=[ANY, ANY], out_specs=ANY,
        out_shape=jax.ShapeDtypeStruct((N_CHIPS, 2, R, C), loc.dtype), input_output_aliases={1: 0},
        scratch_shapes=[pltpu.SemaphoreType.DMA((6,)), pltpu.SemaphoreType.DMA((6,))],
    )(loc, own)


def _pair_exchange(g):
    _, _, R, C = g.shape

    def body(g_ref, out_ref, send_sem, recv_sem):
        x, y, c, _ = _place()
        cp = pltpu.make_async_remote_copy(src_ref=g_ref.at[1 - c], dst_ref=out_ref, send_sem=send_sem,
                                          recv_sem=recv_sem, device_id=(x, y, 1 - c), device_id_type=MESH)
        cp.start()
        cp.wait()

    return pl.pallas_call(
        body, name="pair_exchange", in_specs=[ANY], out_specs=ANY,
        out_shape=jax.ShapeDtypeStruct((N_CHIPS, R, C), g.dtype),
        scratch_shapes=[pltpu.SemaphoreType.DMA, pltpu.SemaphoreType.DMA],
    )(g)


def _pair_sum(g, sib, c_idx):
    _, _, R, C = g.shape
    rb = _pick(R, 512)

    def body(c_ref, g_ref, s_ref, o_ref):
        o_ref[...] = (g_ref[...].astype(F32) + s_ref[...].astype(F32)).astype(o_ref.dtype)

    return pl.pallas_call(
        body, name="pair_sum",
        grid_spec=pltpu.PrefetchScalarGridSpec(
            num_scalar_prefetch=1, grid=(N_CHIPS, R // rb),
            in_specs=[pl.BlockSpec((None, None, rb, C), lambda j, i, c_ref: (c_ref[0], j, i, 0)),
                      pl.BlockSpec((None, rb, C), lambda j, i, c_ref: (j, i, 0))],
            out_specs=pl.BlockSpec((None, rb, C), lambda j, i, c_ref: (j, i, 0))),
        out_shape=jax.ShapeDtypeStruct((N_CHIPS, R, C), g.dtype), compiler_params=_params(),
    )(c_idx, g, sib)


def _chip_exchange(p):
    _, R, C = p.shape

    def body(p_ref, out_ref, send_sems, recv_sems):
        x, y, c, others = _place()
        copies = []
        for j, (ox, oy) in enumerate(others):
            cp = pltpu.make_async_remote_copy(src_ref=p_ref.at[2 * ox + oy], dst_ref=out_ref.at[j],
                                              send_sem=send_sems.at[j], recv_sem=recv_sems.at[j],
                                              device_id=(ox, oy, c), device_id_type=MESH)
            cp.start()
            copies.append(cp)
        for cp in copies:
            cp.wait()

    return pl.pallas_call(
        body, name="chip_exchange", in_specs=[ANY], out_specs=ANY,
        out_shape=jax.ShapeDtypeStruct((N_CHIPS - 1, R, C), p.dtype),
        scratch_shapes=[pltpu.SemaphoreType.DMA((3,)), pltpu.SemaphoreType.DMA((3,))],
    )(p)


def _chip_sum(p, r, chip_idx):
    _, R, C = r.shape
    rb = _pick(R, 512)

    def body(chip_ref, p_ref, r_ref, o_ref):
        acc = p_ref[...].astype(F32)
        for j in range(N_CHIPS - 1):
            acc = acc + r_ref[j].astype(F32)
        o_ref[...] = acc

    return pl.pallas_call(
        body, name="chip_sum",
        grid_spec=pltpu.PrefetchScalarGridSpec(
            num_scalar_prefetch=1, grid=(R // rb,),
            in_specs=[pl.BlockSpec((None, rb, C), lambda i, chip_ref: (chip_ref[0], i, 0)),
                      pl.BlockSpec((N_CHIPS - 1, rb, C), lambda i, chip_ref: (0, i, 0))],
            out_specs=pl.BlockSpec((rb, C), lambda i, chip_ref: (i, 0))),
        out_shape=jax.ShapeDtypeStruct((R, C), F32), compiler_params=_params(),
    )(chip_idx, p, r)


def _pair_swap(rh):
    R, C = rh.shape

    def body(rh_ref, out_ref, send_sem, recv_sem):
        x, y, c, _ = _place()
        cp = pltpu.make_async_remote_copy(src_ref=rh_ref, dst_ref=out_ref, send_sem=send_sem,
                                          recv_sem=recv_sem, device_id=(x, y, 1 - c), device_id_type=MESH)
        cp.start()
        cp.wait()

    return pl.pallas_call(
        body, name="pair_swap", in_specs=[ANY], out_specs=ANY,
        out_shape=jax.ShapeDtypeStruct((R, C), rh.dtype),
        scratch_shapes=[pltpu.SemaphoreType.DMA, pltpu.SemaphoreType.DMA],
    )(rh)


def _all_reduce_small(s):
    R, C = s.shape

    def body(s_ref, o_ref, buf, send_sems, recv_sems):
        x, y, c, _ = _place()
        me = 4 * x + 2 * y + c
        sends = []
        for k in range(1, N_DEV):
            fx, fy, fc = (k >> 2) & 1, (k >> 1) & 1, k & 1
            to = (x ^ fx, y ^ fy, c ^ fc)
            cp = pltpu.make_async_remote_copy(src_ref=s_ref, dst_ref=buf.at[me], send_sem=send_sems.at[k - 1],
                                              recv_sem=recv_sems.at[k - 1], device_id=to, device_id_type=MESH)
            cp.start()
            sends.append(cp)
        buf[me] = s_ref[...]
        for k in range(1, N_DEV):
            fx, fy, fc = (k >> 2) & 1, (k >> 1) & 1, k & 1
            frm = 4 * (x ^ fx) + 2 * (y ^ fy) + (c ^ fc)
            pltpu.make_async_remote_copy(src_ref=s_ref, dst_ref=buf.at[frm], send_sem=send_sems.at[k - 1],
                                         recv_sem=recv_sems.at[k - 1], device_id=(x, y, c),
                                         device_id_type=MESH).wait_recv()
        acc = buf[0]
        for d in range(1, N_DEV):
            acc = acc + buf[d]
        o_ref[...] = acc
        for cp in sends:
            cp.wait_send()

    vm = pl.BlockSpec(memory_space=pltpu.VMEM)
    return pl.pallas_call(
        body, name="all_reduce_small", in_specs=[vm], out_specs=vm,
        out_shape=jax.ShapeDtypeStruct((R, C), F32),
        scratch_shapes=[pltpu.VMEM((N_DEV, R, C), F32), pltpu.SemaphoreType.DMA((N_DEV - 1,)),
                        pltpu.SemaphoreType.DMA((N_DEV - 1,))],
    )(s)


def _padded(n):
    return -(-n // FLAT_UNIT) * FLAT_UNIT


def _pack_flat(pieces, dtype, row_block=FLAT_ROW_BLOCK):
    flat = []
    for p in pieces:
        p = p.reshape(-1).astype(dtype)
        flat.append(jnp.pad(p, (0, _padded(p.size) - p.size)))
    total = sum(p.size for p in flat)
    flat.append(jnp.zeros((-total) % (row_block * FLAT_COLS), dtype))
    return jnp.concatenate(flat).reshape(-1, FLAT_COLS)


def _unpack_flat(flat, shapes):
    lead = flat.shape[:-2]
    flat = flat.reshape(lead + (-1,))
    out, off = [], 0
    for shp in shapes:
        n = math.prod(shp)
        out.append(flat[..., off:off + n].reshape(lead + tuple(shp)))
        off += _padded(n)
    return out


def _flat_offsets(shapes):
    offs, off = [], 0
    for shp in shapes:
        offs.append(off)
        off += _padded(math.prod(shp))
    return offs


def _slab(t, axis, j):
    n = t.shape[axis - 1] // N_CHIPS
    return lax.slice_in_dim(t, j * n, (j + 1) * n, axis=axis - 1)


def _layer_fwd(h0, mem_n, wl, dims):
    n_sb, n_fx, n_mem, sbw, fxw, memw = dims
    n1, gate1, up1, a1 = _ffn_fwd_up(h0, wl["ffn1_pre_g"], wl["ffn1_w_gate"], wl["ffn1_w_up"])
    h1, f1 = _ffn_fwd_down(a1, wl["ffn1_w_down"], h0, wl["ffn1_post_g"])

    u, proj, fl, sg = _mix_fwd_in(h1, wl["mix_pre_g"], wl["w_in"], wl["w_gate"], wl["b_gate"], wl["b_forget"])
    c = _fox_cumsum(fl)
    S = h0.shape[0]
    q_sb, k_sb, v_sb = (_to_heads(proj[:, k * sbw:(k + 1) * sbw], n_sb) for k in range(3))
    q_fx, k_fx, v_fx = (_to_heads(proj[:, 3 * sbw + k * fxw:3 * sbw + (k + 1) * fxw], n_fx) for k in range(3))
    q_mem = _to_heads(proj[:, 3 * sbw + 3 * fxw:], n_mem)
    tc = _attn_blocks("fox", S, S)[1]
    ct = c[:, :n_fx].T
    ccol, crow = ct.reshape(n_fx, S, 1), ct.reshape(n_fx, S // tc, tc)
    o_sb, tot_sb = _attn_fwd("sb", q_sb, k_sb, v_sb)
    o_fx, lse_fx = _attn_fwd("fox", q_fx, k_fx, v_fx, ccol, crow)
    kv = _matmul(mem_n, wl["w_mem_kv"], out_dtype=BF16, name="mem_kv")
    k_mem, v_mem = _to_heads(kv[:, :memw], n_mem), _to_heads(kv[:, memw:], n_mem)
    o_mem, lse_mem = _attn_fwd("mem", q_mem, k_mem, v_mem)
    o_sb_m, o_fx_m, o_mem_m = _from_heads(o_sb), _from_heads(o_fx), _from_heads(o_mem)
    h2, zmix, merged = _mix_fwd_out(o_sb_m, o_fx_m, o_mem_m, sg, wl["w_br_sb"], wl["w_br_fox"], wl["w_br_mem"],
                                    wl["w_out"], h1, wl["mix_post_g"])

    n2, gate2, up2, a2 = _ffn_fwd_up(h2, wl["ffn2_pre_g"], wl["ffn2_w_gate"], wl["ffn2_w_up"])
    h3, f2 = _ffn_fwd_down(a2, wl["ffn2_w_down"], h2, wl["ffn2_post_g"])
    saved = dict(h0=h0, n1=n1, gate1=gate1, up1=up1, a1=a1, f1=f1, h1=h1, u=u, fl=fl, sg=sg,
                 q_sb=q_sb, k_sb=k_sb, v_sb=v_sb, q_fx=q_fx, k_fx=k_fx, v_fx=v_fx, q_mem=q_mem,
                 k_mem=k_mem, v_mem=v_mem, ccol=ccol, crow=crow, o_sb=o_sb, o_fx=o_fx, o_mem=o_mem,
                 tot_sb=tot_sb, lse_fx=lse_fx, lse_mem=lse_mem, o_sb_m=o_sb_m, o_fx_m=o_fx_m, o_mem_m=o_mem_m,
                 zmix=zmix, merged=merged, h2=h2, n2=n2, gate2=gate2, up2=up2, a2=a2, f2=f2)
    return h3, saved


def _ffn_bwd(dh, sv, wl, tag, h_in):
    n, gate, up, a, f = (sv[k + tag] for k in ("n", "gate", "up", "a", "f"))
    pre = "ffn" + tag
    df, dgate, dup, dg_post = _ffn_bwd_down(dh, f, wl[pre + "_post_g"], wl[pre + "_w_down"], gate, up)
    dh_in, dg_pre = _ffn_bwd_up(dgate, dup, wl[pre + "_w_gate"], wl[pre + "_w_up"], h_in, wl[pre + "_pre_g"], dh)
    grads = {pre + "_post_g": dg_post, pre + "_pre_g": dg_pre,
             pre + "_w_down": _matmul(a, df, ta=True, name="dw_down"),
             pre + "_w_gate": _matmul(n, dgate, ta=True, name="dw_gate"),
             pre + "_w_up": _matmul(n, dup, ta=True, name="dw_up")}
    return dh_in, grads


def _layer_bwd(dh3, mem_n, wl, sv, dims):
    n_sb, n_fx, n_mem, sbw, fxw, memw = dims
    S = dh3.shape[0]
    dh2, grads = _ffn_bwd(dh3, sv, wl, "2", sv["h2"])

    (dz, db_sb, db_fx, db_mem, do_sb, do_fx, do_mem, dgp, db_gate, dg_post) = _mix_bwd_out(
        dh2, sv["zmix"], wl["mix_post_g"], wl["w_out"], sv["o_sb_m"], sv["o_fx_m"], sv["o_mem_m"],
        wl["w_br_sb"], wl["w_br_fox"], wl["w_br_mem"], sv["sg"])
    grads["mix_post_g"] = dg_post
    grads["b_gate"] = db_gate
    grads["w_out"] = _matmul(sv["merged"], dz, ta=True, name="dw_out")
    grads["w_br_sb"] = _matmul(sv["o_sb_m"], db_sb, ta=True, name="dw_br_sb")
    grads["w_br_fox"] = _matmul(sv["o_fx_m"], db_fx, ta=True, name="dw_br_fox")
    grads["w_br_mem"] = _matmul(sv["o_mem_m"], db_mem, ta=True, name="dw_br_mem")

    dq_sb, dk_sb, dv_sb = _attn_bwd("sb", sv["q_sb"], sv["k_sb"], sv["v_sb"], sv["o_sb"], _to_heads(do_sb, n_sb),
                                    lse=sv["tot_sb"])
    dq_fx, dk_fx, dv_fx, dcrow, dccol = _attn_bwd("fox", sv["q_fx"], sv["k_fx"], sv["v_fx"], sv["o_fx"],
                                                  _to_heads(do_fx, n_fx), sv["ccol"], sv["crow"], sv["lse_fx"])
    dq_mem, dk_mem, dv_mem = _attn_bwd("mem", sv["q_mem"], sv["k_mem"], sv["v_mem"], sv["o_mem"],
                                       _to_heads(do_mem, n_mem), lse=sv["lse_mem"])
    dkv = jnp.concatenate([_from_heads(dk_mem), _from_heads(dv_mem)], axis=1)
    grads["w_mem_kv"] = _matmul(mem_n, dkv, ta=True, name="dw_mem_kv")
    dmem_n = _matmul(dkv, wl["w_mem_kv"], tb=True, out_dtype=F32, name="dmem_n")

    dc = jnp.pad((dcrow.reshape(n_fx, S) + dccol.reshape(n_fx, S)).T, ((0, 0), (0, LANE - n_fx)))
    dfl, db_forget = _fox_dlogit(dc, sv["fl"])
    grads["b_forget"] = db_forget
    dproj = jnp.concatenate([_from_heads(t) for t in (dq_sb, dk_sb, dv_sb, dq_fx, dk_fx, dv_fx, dq_mem)] + [dfl],
                            axis=1)
    dh1, dg_pre = _mix_bwd_in(dproj, dgp, wl["w_in"], wl["w_gate"], sv["h1"], wl["mix_pre_g"], dh2)
    grads["mix_pre_g"] = dg_pre
    grads["w_in"] = _matmul(sv["u"], dproj, ta=True, name="dw_in")
    grads["w_gate"] = _matmul(sv["u"], dgp, ta=True, name="dw_gate_mix")

    dh0, g1 = _ffn_bwd(dh1, sv, wl, "1", sv["h0"])
    grads.update(g1)
    return dh0, grads, dmem_n


def kernel(x, mem, ffn1_pre_g, ffn1_post_g, ffn1_w_gate, ffn1_w_up, ffn1_w_down, mix_pre_g, mix_post_g, w_in, b_forget, mem_norm_g, w_mem_kv, w_gate, b_gate, w_br_sb, w_br_fox, w_br_mem, w_out, ffn2_pre_g, ffn2_post_g, ffn2_w_gate, ffn2_w_up, ffn2_w_down, loss_target, m_ffn1_pre_g, m_ffn1_post_g, m_ffn1_w_gate, m_ffn1_w_up, m_ffn1_w_down, m_mix_pre_g, m_mix_post_g, m_w_in, m_b_forget, m_mem_norm_g, m_w_mem_kv, m_w_gate, m_b_gate, m_w_br_sb, m_w_br_fox, m_w_br_mem, m_w_out, m_ffn2_pre_g, m_ffn2_post_g, m_ffn2_w_gate, m_ffn2_w_up, m_ffn2_w_down, v_ffn1_pre_g, v_ffn1_post_g, v_ffn1_w_gate, v_ffn1_w_up, v_ffn1_w_down, v_mix_pre_g, v_mix_post_g, v_w_in, v_b_forget, v_mem_norm_g, v_w_mem_kv, v_w_gate, v_b_gate, v_w_br_sb, v_w_br_fox, v_w_br_mem, v_w_out, v_ffn2_pre_g, v_ffn2_post_g, v_ffn2_w_gate, v_ffn2_w_up, v_ffn2_w_down):
    args = dict(locals())
    w = {n: args[n] for n in WEIGHTS}
    m = {n: args["m_" + n] for n in WEIGHTS}
    v = {n: args["v_" + n] for n in WEIGHTS}
    L = w["ffn1_pre_g"].shape[0]
    Lh = L // 2
    D = x.shape[2]
    sbw, fxw, memw = w["w_br_sb"].shape[1], w["w_br_fox"].shape[1], w["w_br_mem"].shape[1]
    n_sb, n_fx, n_mem = sbw // HEAD_DIM, fxw // HEAD_DIM, memw // MEM_HEAD_DIM
    dims = (n_sb, n_fx, n_mem, sbw, fxw, memw)
    qkv_w = 3 * sbw + 3 * fxw
    c_idx = lax.axis_index("c")
    c_arr = c_idx.reshape(1).astype(jnp.int32)
    chip_arr = (2 * lax.axis_index("x") + lax.axis_index("y")).reshape(1).astype(jnp.int32)

    piece_shapes = [w[n].shape[1:] for n, _ in BIG for _ in range(Lh)]
    row_offs = [off // FLAT_COLS for off in _flat_offsets(piece_shapes)]
    loc = jnp.stack([_pack_flat([w[n][hf * Lh + li] for n, _ in BIG for li in range(Lh)], BF16) for hf in range(2)])
    gathered = _gather_weights(loc, _place_own(loc, chip_arr))

    def piece(buf, k, li):
        a, b = piece_shapes[k * Lh + li]
        r0 = row_offs[k * Lh + li]
        rows = buf[..., r0:r0 + _padded(a * b) // FLAT_COLS, :]
        return rows.reshape(buf.shape[:-2] + (-1,))[..., :a * b].reshape(buf.shape[:-2] + (a, b))

    def layer_weights(l):
        hf, li = divmod(l, Lh)
        wl = {}
        for k, (n, axis) in enumerate(BIG):
            shards = piece(gathered[:, hf], k, li)
            a, b = shards.shape[1:]
            wl[n] = shards.transpose(1, 0, 2).reshape(a, N_CHIPS * b) if axis == 2 else shards.reshape(N_CHIPS * a, b)
        wi = wl["w_in"]
        wl["w_in"] = jnp.concatenate([wi[:, :qkv_w], wi[:, qkv_w + n_fx:], wi[:, qkv_w:qkv_w + n_fx],
                                      jnp.zeros((D, LANE - n_fx), BF16)], axis=1)
        for n in SMALL:
            if n != "mem_norm_g":
                wl[n] = w[n][l][None, :]
        wl["b_forget"] = jnp.pad(wl["b_forget"], ((0, 0), (0, LANE - n_fx)))
        return wl

    g_mem = w["mem_norm_g"][None, :]

    mem_n = _mem_norm(mem[0], g_mem)
    h, wls, saved = x[0], [], []
    for l in range(L):
        wls.append(layer_weights(l))
        h, sv = _layer_fwd(h, mem_n, wls[l], dims)
        saved.append(sv)
    dh, loss_tile = _loss_head(h, loss_target[0])
    loss = lax.psum(loss_tile[0, 0], ("x", "y", "c"))
    gl, dmem_n = [None] * L, [None] * L
    for l in reversed(range(L)):
        dh, gl[l], dmem_n[l] = _layer_bwd(dh, mem_n, wls[l], saved[l], dims)
        gi = gl[l]["w_in"]
        gl[l]["w_in"] = jnp.concatenate([gi[:, :qkv_w], gi[:, qkv_w + memw:qkv_w + memw + n_fx],
                                         gi[:, qkv_w:qkv_w + memw]], axis=1)
    grad_x = dh
    g_mem_norm = _mem_norm_bwd(mem[0], g_mem, jnp.stack(dmem_n))

    partial = jnp.stack([
        jnp.stack([_pack_flat([_slab(gl[hf * Lh + li][n], axis, j) for n, axis in BIG for li in range(Lh)], BF16)
                   for j in range(N_CHIPS)]) for hf in range(2)])
    sib = _pair_exchange(partial)
    pair = _pair_sum(partial, sib, c_arr)
    mine = _chip_sum(pair, _chip_exchange(pair), chip_arr)
    theirs = _pair_swap(mine)
    grad = {n: jnp.stack([jnp.where(c_idx == l // Lh, piece(mine, k, l % Lh), piece(theirs, k, l % Lh))
                          for l in range(L)]) for k, (n, _) in enumerate(BIG)}

    small_local = {n: (g_mem_norm if n == "mem_norm_g" else
                       jnp.concatenate([gl[l][n][:, :n_fx] if n == "b_forget" else gl[l][n] for l in range(L)]))
                   for n in SMALL}
    small_shapes = [small_local[n].shape for n in SMALL]
    small_sum = _unpack_flat(_all_reduce_small(_pack_flat([small_local[n] for n in SMALL], F32, row_block=16)),
                             small_shapes)
    for n, t in zip(SMALL, small_sum):
        grad[n] = t.reshape(w[n].shape)

    delta, new_m, new_v = {}, {}, {}
    for n in WEIGHTS:
        shp = w[n].shape
        two_d = (1, shp[0]) if len(shp) == 1 else (-1, shp[-1])
        d_, m_, v_ = _adamw(w[n].reshape(two_d), grad[n].reshape(two_d), m[n].reshape(two_d), v[n].reshape(two_d),
                            name="adamw_" + n)
        delta[n], new_m[n], new_v[n] = d_.reshape(shp), m_.reshape(shp), v_.reshape(shp)

    return (loss, grad_x[None], *[grad[n] for n in WEIGHTS], *[delta[n] for n in WEIGHTS],
            *[new_m[n] for n in WEIGHTS], *[new_v[n] for n in WEIGHTS])
```

```python
import math

import jax
import jax.numpy as jnp
from jax import lax
from jax.experimental import pallas as pl
from jax.experimental.pallas import tpu as pltpu

F32 = jnp.float32
BF16 = jnp.bfloat16
RMS_EPS = 1e-6
HEAD_DIM = 64
MEM_HEAD_DIM = 128
LANE = 128
V7X_VMEM_LIMIT_BYTES = 56 * 1024 * 1024
FLAT_COLS = 512
FLAT_UNIT = 16 * FLAT_COLS
FLAT_ROW_BLOCK = 512
N_CHIPS = 4
N_DEV = 8
NEG = float(jnp.finfo(jnp.float32).min)

ADAM_LR = 0.001
ADAM_B1 = 0.9
ADAM_B2 = 0.999
ADAM_EPS = 1e-08
ADAM_WD = 0.01
ADAM_STEP = 10

BIG = (("ffn1_w_gate", 2), ("ffn1_w_up", 2), ("ffn1_w_down", 1), ("w_in", 2), ("w_mem_kv", 1), ("w_gate", 2),
       ("w_br_sb", 2), ("w_br_fox", 2), ("w_br_mem", 2), ("w_out", 1),
       ("ffn2_w_gate", 2), ("ffn2_w_up", 2), ("ffn2_w_down", 1))
SMALL = ("ffn1_pre_g", "ffn1_post_g", "mix_pre_g", "mix_post_g", "b_forget", "mem_norm_g", "b_gate",
         "ffn2_pre_g", "ffn2_post_g")
WEIGHTS = ("ffn1_pre_g", "ffn1_post_g", "ffn1_w_gate", "ffn1_w_up", "ffn1_w_down", "mix_pre_g", "mix_post_g", "w_in",
           "b_forget", "mem_norm_g", "w_mem_kv", "w_gate", "b_gate", "w_br_sb", "w_br_fox", "w_br_mem", "w_out",
           "ffn2_pre_g", "ffn2_post_g", "ffn2_w_gate", "ffn2_w_up", "ffn2_w_down")


def _params(**kw):
    return pltpu.CompilerParams(vmem_limit_bytes=V7X_VMEM_LIMIT_BYTES, **kw)


def _dot(a, b):
    return jnp.dot(a, b, preferred_element_type=F32)


def _dot_nt(a, b):
    return lax.dot_general(a, b, (((1,), (1,)), ((), ())), preferred_element_type=F32)


def _dot_tn(a, b):
    return lax.dot_general(a, b, (((0,), (0,)), ((), ())), preferred_element_type=F32)


def _rms(t, g):
    return t * lax.rsqrt(jnp.mean(t * t, axis=-1, keepdims=True) + RMS_EPS) * g


def _pick(dim, pref):
    if dim <= pref:
        return dim
    for cand in range(pref - pref % LANE, 0, -LANE):
        if dim % cand == 0:
            return cand
    return dim


def _rows(bm, cols):
    return pl.BlockSpec((bm, cols), lambda i: (i, 0))


def _whole(shape):
    nd = len(shape)
    return pl.BlockSpec(shape, lambda i: (0,) * nd)


def _split3(x):
    hi = x.astype(BF16)
    r1 = x - hi.astype(F32)
    mid = r1.astype(BF16)
    lo = (r1 - mid.astype(F32)).astype(BF16)
    return hi, mid, lo


def _cumdot(x, tri):
    hi = x.astype(BF16)
    lo = (x - hi.astype(F32)).astype(BF16)
    return _dot(hi, tri) + _dot(lo, tri)


def _ffn_fwd_up(h, g_pre, wg, wu):
    S, D = h.shape
    F = wg.shape[1]
    bm = _pick(S, 256)

    def body(h_ref, g_ref, wg_ref, wu_ref, n_ref, gate_ref, up_ref, a_ref):
        n = _rms(h_ref[...], g_ref[...]).astype(BF16)
        n_ref[...] = n
        gate = _dot(n, wg_ref[...])
        up = _dot(n, wu_ref[...])
        gate_ref[...] = gate.astype(BF16)
        up_ref[...] = up.astype(BF16)
        a_ref[...] = (gate * jax.nn.sigmoid(gate) * up).astype(BF16)

    return pl.pallas_call(
        body, name="ffn_fwd_up", grid=(S // bm,),
        in_specs=[_rows(bm, D), _whole((1, D)), _whole((D, F)), _whole((D, F))],
        out_specs=[_rows(bm, D), _rows(bm, F), _rows(bm, F), _rows(bm, F)],
        out_shape=[jax.ShapeDtypeStruct((S, D), BF16)] + [jax.ShapeDtypeStruct((S, F), BF16)] * 3,
        compiler_params=_params(),
    )(h, g_pre, wg, wu)


def _ffn_fwd_down(a, wd, h, g_post):
    S, F = a.shape
    D = wd.shape[1]
    bm = _pick(S, 256)

    def body(a_ref, wd_ref, h_ref, g_ref, hout_ref, f_ref):
        f = _dot(a_ref[...], wd_ref[...])
        f_ref[...] = f
        hout_ref[...] = h_ref[...] + 0.5 * _rms(f, g_ref[...])

    return pl.pallas_call(
        body, name="ffn_fwd_down", grid=(S // bm,),
        in_specs=[_rows(bm, F), _whole((F, D)), _rows(bm, D), _whole((1, D))],
        out_specs=[_rows(bm, D), _rows(bm, D)],
        out_shape=[jax.ShapeDtypeStruct((S, D), F32)] * 2,
        compiler_params=_params(),
    )(a, wd, h, g_post)


def _ffn_bwd_down(dh, f, g_post, wd, gate, up):
    S, D = dh.shape
    F = wd.shape[0]
    bm = _pick(S, 256)

    def body(dh_ref, f_ref, g_ref, wd_ref, gate_ref, up_ref, df_ref, dgate_ref, dup_ref, dg_ref):
        _, vjp = jax.vjp(lambda t, g: 0.5 * _rms(t, g), f_ref[...], g_ref[...])
        df, dg = vjp(dh_ref[...])

        @pl.when(pl.program_id(0) == 0)
        def _():
            dg_ref[...] = jnp.zeros_like(dg_ref)

        dg_ref[...] += dg
        dfb = df.astype(BF16)
        df_ref[...] = dfb
        da = _dot_nt(dfb, wd_ref[...])
        gt = gate_ref[...].astype(F32)
        sig = jax.nn.sigmoid(gt)
        silu = gt * sig
        dup_ref[...] = (da * silu).astype(BF16)
        dgate_ref[...] = (da * up_ref[...].astype(F32) * (sig + silu * (1.0 - sig))).astype(BF16)

    return pl.pallas_call(
        body, name="ffn_bwd_down", grid=(S // bm,),
        in_specs=[_rows(bm, D), _rows(bm, D), _whole((1, D)), _whole((F, D)), _rows(bm, F), _rows(bm, F)],
        out_specs=[_rows(bm, D), _rows(bm, F), _rows(bm, F), _whole((1, D))],
        out_shape=[jax.ShapeDtypeStruct((S, D), BF16), jax.ShapeDtypeStruct((S, F), BF16),
                   jax.ShapeDtypeStruct((S, F), BF16), jax.ShapeDtypeStruct((1, D), F32)],
        compiler_params=_params(),
    )(dh, f, g_post, wd, gate, up)


def _ffn_bwd_up(dgate, dup, wg, wu, h_in, g_pre, dh):
    S, F = dgate.shape
    D = wg.shape[0]
    bm = _pick(S, 256)

    def body(dgate_ref, dup_ref, wg_ref, wu_ref, h_ref, g_ref, dh_ref, dhin_ref, dg_ref):
        dn = _dot_nt(dgate_ref[...], wg_ref[...]) + _dot_nt(dup_ref[...], wu_ref[...])
        _, vjp = jax.vjp(_rms, h_ref[...], g_ref[...])
        dhx, dg = vjp(dn)

        @pl.when(pl.program_id(0) == 0)
        def _():
            dg_ref[...] = jnp.zeros_like(dg_ref)

        dg_ref[...] += dg
        dhin_ref[...] = dh_ref[...] + dhx

    return pl.pallas_call(
        body, name="ffn_bwd_up", grid=(S // bm,),
        in_specs=[_rows(bm, F), _rows(bm, F), _whole((D, F)), _whole((D, F)), _rows(bm, D), _whole((1, D)),
                  _rows(bm, D)],
        out_specs=[_rows(bm, D), _whole((1, D))],
        out_shape=[jax.ShapeDtypeStruct((S, D), F32), jax.ShapeDtypeStruct((1, D), F32)],
        compiler_params=_params(),
    )(dgate, dup, wg, wu, h_in, g_pre, dh)


def _matmul(a, b, *, ta=False, tb=False, out_dtype=BF16, name):
    M, K = (a.shape[1], a.shape[0]) if ta else a.shape
    N = b.shape[0] if tb else b.shape[1]
    bm, bk = _pick(M, 512), _pick(K, 512)
    bn = N if N * bm * 4 <= 8 * 1024 * 1024 else _pick(N, 1536)
    nk = K // bk

    def body(a_ref, b_ref, o_ref, acc_ref):
        kk = pl.program_id(2)

        @pl.when(kk == 0)
        def _():
            acc_ref[...] = jnp.zeros_like(acc_ref)

        av, bv = a_ref[...], b_ref[...]
        dims = (((0 if ta else 1,), (1 if tb else 0,)), ((), ()))
        acc_ref[...] += lax.dot_general(av, bv, dims, preferred_element_type=F32)

        @pl.when(kk == nk - 1)
        def _():
            o_ref[...] = acc_ref[...].astype(o_ref.dtype)

    a_spec = pl.BlockSpec((bk, bm), lambda i, j, k: (k, i)) if ta else pl.BlockSpec((bm, bk), lambda i, j, k: (i, k))
    b_spec = pl.BlockSpec((bn, bk), lambda i, j, k: (j, k)) if tb else pl.BlockSpec((bk, bn), lambda i, j, k: (k, j))
    return pl.pallas_call(
        body, name=name, grid=(M // bm, N // bn, nk),
        in_specs=[a_spec, b_spec],
        out_specs=pl.BlockSpec((bm, bn), lambda i, j, k: (i, j)),
        out_shape=jax.ShapeDtypeStruct((M, N), out_dtype),
        scratch_shapes=[pltpu.VMEM((bm, bn), F32)],
        compiler_params=_params(),
    )(a, b)


def _mix_fwd_in(h, g_pre, win, wgate, b_gate, b_forget):
    S, D = h.shape
    PW = win.shape[1] - LANE
    G = wgate.shape[1]
    bm = _pick(S, 256)

    def body(h_ref, g_ref, win_ref, wgate_ref, bg_ref, bf_ref, u_ref, proj_ref, fl_ref, sg_ref):
        u = _rms(h_ref[...], g_ref[...]).astype(BF16)
        u_ref[...] = u
        proj = _dot(u, win_ref[...])
        proj_ref[...] = proj[:, :PW].astype(BF16)
        fl_ref[...] = proj[:, PW:] + bf_ref[...]
        sg_ref[...] = jax.nn.sigmoid(_dot(u, wgate_ref[...]) + bg_ref[...]).astype(BF16)

    return pl.pallas_call(
        body, name="mix_fwd_in", grid=(S // bm,),
        in_specs=[_rows(bm, D), _whole((1, D)), _whole((D, PW + LANE)), _whole((D, G)), _whole((1, G)),
                  _whole((1, LANE))],
        out_specs=[_rows(bm, D), _rows(bm, PW), _rows(bm, LANE), _rows(bm, G)],
        out_shape=[jax.ShapeDtypeStruct((S, D), BF16), jax.ShapeDtypeStruct((S, PW), BF16),
                   jax.ShapeDtypeStruct((S, LANE), F32), jax.ShapeDtypeStruct((S, G), BF16)],
        compiler_params=_params(),
    )(h, g_pre, win, wgate, b_gate, b_forget)


def _mix_fwd_out(o_sb, o_fx, o_mem, sg, w_sb, w_fx, w_mem, w_out, h, g_post):
    S, D = h.shape
    bm = _pick(S, 256)
    widths = (o_sb.shape[1], o_fx.shape[1], o_mem.shape[1])

    def body(osb_ref, ofx_ref, omem_ref, sg_ref, wsb_ref, wfx_ref, wmem_ref, wout_ref, h_ref, g_ref,
             hout_ref, z_ref, merged_ref):
        s = sg_ref[...].astype(F32)
        merged = (s[:, :D] * _dot(osb_ref[...], wsb_ref[...]) + s[:, D:2 * D] * _dot(ofx_ref[...], wfx_ref[...])
                  + s[:, 2 * D:] * _dot(omem_ref[...], wmem_ref[...]))
        mb = merged.astype(BF16)
        merged_ref[...] = mb
        z = _dot(mb, wout_ref[...])
        z_ref[...] = z
        hout_ref[...] = h_ref[...] + _rms(z, g_ref[...])

    return pl.pallas_call(
        body, name="mix_fwd_out", grid=(S // bm,),
        in_specs=[_rows(bm, widths[0]), _rows(bm, widths[1]), _rows(bm, widths[2]), _rows(bm, 3 * D),
                  _whole((widths[0], D)), _whole((widths[1], D)), _whole((widths[2], D)), _whole((D, D)),
                  _rows(bm, D), _whole((1, D))],
        out_specs=[_rows(bm, D), _rows(bm, D), _rows(bm, D)],
        out_shape=[jax.ShapeDtypeStruct((S, D), F32), jax.ShapeDtypeStruct((S, D), F32),
                   jax.ShapeDtypeStruct((S, D), BF16)],
        compiler_params=_params(),
    )(o_sb, o_fx, o_mem, sg, w_sb, w_fx, w_mem, w_out, h, g_post)


def _mix_bwd_out(dh, z, g_post, w_out, o_sb, o_fx, o_mem, w_sb, w_fx, w_mem, sg):
    S, D = dh.shape
    bm = _pick(S, 256)
    widths = (o_sb.shape[1], o_fx.shape[1], o_mem.shape[1])

    def body(dh_ref, z_ref, g_ref, wout_ref, osb_ref, ofx_ref, omem_ref, wsb_ref, wfx_ref, wmem_ref, sg_ref,
             dz_ref, dbsb_ref, dbfx_ref, dbmem_ref, dosb_ref, dofx_ref, domem_ref, dgp_ref, dbg_ref, dg_ref):
        _, vjp = jax.vjp(_rms, z_ref[...], g_ref[...])
        dz, dg = vjp(dh_ref[...])

        @pl.when(pl.program_id(0) == 0)
        def _():
            dg_ref[...] = jnp.zeros_like(dg_ref)
            dbg_ref[...] = jnp.zeros_like(dbg_ref)

        dg_ref[...] += dg
        dzb = dz.astype(BF16)
        dz_ref[...] = dzb
        dmerged = _dot_nt(dzb, wout_ref[...])
        s = sg_ref[...].astype(F32)
        branches = ((osb_ref, wsb_ref, dbsb_ref, dosb_ref), (ofx_ref, wfx_ref, dbfx_ref, dofx_ref),
                    (omem_ref, wmem_ref, dbmem_ref, domem_ref))
        for k, (o_ref, w_ref, db_ref, do_ref) in enumerate(branches):
            gs = s[:, k * D:(k + 1) * D]
            dbb = (dmerged * gs).astype(BF16)
            db_ref[...] = dbb
            do_ref[...] = _dot_nt(dbb, w_ref[...]).astype(BF16)
            dgp = dmerged * _dot(o_ref[...], w_ref[...]) * gs * (1.0 - gs)
            dgp_ref[:, k * D:(k + 1) * D] = dgp.astype(BF16)
            dbg_ref[:, k * D:(k + 1) * D] += jnp.sum(dgp, axis=0, keepdims=True)

    return pl.pallas_call(
        body, name="mix_bwd_out", grid=(S // bm,),
        in_specs=[_rows(bm, D), _rows(bm, D), _whole((1, D)), _whole((D, D)),
                  _rows(bm, widths[0]), _rows(bm, widths[1]), _rows(bm, widths[2]),
                  _whole((widths[0], D)), _whole((widths[1], D)), _whole((widths[2], D)), _rows(bm, 3 * D)],
        out_specs=[_rows(bm, D)] * 4 + [_rows(bm, widths[0]), _rows(bm, widths[1]), _rows(bm, widths[2]),
                                        _rows(bm, 3 * D), _whole((1, 3 * D)), _whole((1, D))],
        out_shape=[jax.ShapeDtypeStruct((S, D), BF16)] * 4
        + [jax.ShapeDtypeStruct((S, w), BF16) for w in widths]
        + [jax.ShapeDtypeStruct((S, 3 * D), BF16), jax.ShapeDtypeStruct((1, 3 * D), F32),
           jax.ShapeDtypeStruct((1, D), F32)],
        compiler_params=_params(),
    )(dh, z, g_post, w_out, o_sb, o_fx, o_mem, w_sb, w_fx, w_mem, sg)


def _mix_bwd_in(dproj, dgp, win, wgate, h_in, g_pre, dh):
    S, PWL = dproj.shape
    G = dgp.shape[1]
    D = h_in.shape[1]
    bm = _pick(S, 256)

    def body(dproj_ref, dgp_ref, win_ref, wgate_ref, h_ref, g_ref, dh_ref, dhin_ref, dg_ref):
        du = _dot_nt(dproj_ref[...], win_ref[...]) + _dot_nt(dgp_ref[...], wgate_ref[...])
        _, vjp = jax.vjp(_rms, h_ref[...], g_ref[...])
        dhx, dg = vjp(du)

        @pl.when(pl.program_id(0) == 0)
        def _():
            dg_ref[...] = jnp.zeros_like(dg_ref)

        dg_ref[...] += dg
        dhin_ref[...] = dh_ref[...] + dhx

    return pl.pallas_call(
        body, name="mix_bwd_in", grid=(S // bm,),
        in_specs=[_rows(bm, PWL), _rows(bm, G), _whole((D, PWL)), _whole((D, G)), _rows(bm, D), _whole((1, D)),
                  _rows(bm, D)],
        out_specs=[_rows(bm, D), _whole((1, D))],
        out_shape=[jax.ShapeDtypeStruct((S, D), F32), jax.ShapeDtypeStruct((1, D), F32)],
        compiler_params=_params(),
    )(dproj, dgp, win, wgate, h_in, g_pre, dh)


def _log_sigmoid(x):
    return jnp.minimum(x, 0.0) - jnp.log(1.0 + jnp.exp(-jnp.abs(x)))


def _fox_cumsum(fl):
    S = fl.shape[0]
    rb = _pick(S, LANE)

    def body(fl_ref, c_ref, carry_ref):
        @pl.when(pl.program_id(0) == 0)
        def _():
            carry_ref[...] = jnp.zeros_like(carry_ref)

        r = lax.broadcasted_iota(jnp.int32, (rb, rb), 0)
        cidx = lax.broadcasted_iota(jnp.int32, (rb, rb), 1)
        tri = (cidx <= r).astype(BF16)
        hi, mid, lo = _split3(_log_sigmoid(fl_ref[...]))
        c = _dot(tri, hi) + _dot(tri, mid) + _dot(tri, lo) + carry_ref[...]
        c_ref[...] = c
        carry_ref[...] = c[rb - 1:rb, :]

    return pl.pallas_call(
        body, name="fox_cumsum", grid=(S // rb,),
        in_specs=[_rows(rb, LANE)], out_specs=_rows(rb, LANE),
        out_shape=jax.ShapeDtypeStruct((S, LANE), F32),
        scratch_shapes=[pltpu.VMEM((1, LANE), F32)],
        compiler_params=_params(),
    )(fl)


def _fox_dlogit(dc, fl):
    S = fl.shape[0]
    rb = _pick(S, LANE)
    nb = S // rb

    def body(dc_ref, fl_ref, dfl_ref, dbf_ref, carry_ref):
        @pl.when(pl.program_id(0) == 0)
        def _():
            carry_ref[...] = jnp.zeros_like(carry_ref)
            dbf_ref[...] = jnp.zeros_like(dbf_ref)

        r = lax.broadcasted_iota(jnp.int32, (rb, rb), 0)
        cidx = lax.broadcasted_iota(jnp.int32, (rb, rb), 1)
        tri = (cidx >= r).astype(BF16)
        hi, mid, lo = _split3(dc_ref[...])
        rc = _dot(tri, hi) + _dot(tri, mid) + _dot(tri, lo) + carry_ref[...]
        carry_ref[...] = rc[0:1, :]
        dfl = rc * jax.nn.sigmoid(-fl_ref[...])
        dfl_ref[...] = dfl.astype(BF16)
        dbf_ref[...] += jnp.sum(dfl, axis=0, keepdims=True)

    rev = pl.BlockSpec((rb, LANE), lambda i: (nb - 1 - i, 0))
    return pl.pallas_call(
        body, name="fox_dlogit", grid=(nb,),
        in_specs=[rev, rev], out_specs=[rev, _whole((1, LANE))],
        out_shape=[jax.ShapeDtypeStruct((S, LANE), BF16), jax.ShapeDtypeStruct((1, LANE), F32)],
        scratch_shapes=[pltpu.VMEM((1, LANE), F32)],
        compiler_params=_params(),
    )(dc, fl)


def _attn_blocks(kind, S, Sk):
    tq = _pick(S, 512)
    tc = LANE if kind == "sb" else _pick(Sk, 256)
    return tq, tc


def _is_power_of_two(x):
    return math.frexp(x)[0] == 0.5


def _sb_logs(z):
    ln = -jnp.maximum(z, 0.0) - jnp.log(1.0 + jnp.exp(-jnp.abs(z)))
    return ln + z, ln


def _head_lanes(pack, dh):
    lane = lax.broadcasted_iota(jnp.int32, (1, LANE), 1)
    return [(lane >= hh * dh) & (lane < (hh + 1) * dh) for hh in range(pack)]


def _by_head(sel, parts):
    out = parts[0]
    for hh in range(1, len(parts)):
        out = jnp.where(sel[hh], parts[hh], out)
    return out


def _only_head(sel, hh, x):
    return x if len(sel) == 1 else jnp.where(sel[hh], x, jnp.zeros_like(x))


def _q_cols(tq, first):
    return pl.BlockSpec((tq, LANE), lambda g, i: (i, first // LANE + g))


def _k_cols(rows, first):
    return pl.BlockSpec((rows, LANE), lambda g, i: (0, first // LANE + g))


def _attn_fwd(kind, q, k, v, n_heads, dh, ccol=None, crow=None):
    (qa, q0), (ka, k0), (va, v0) = q, k, v
    S, Sk = qa.shape[0], ka.shape[0]
    pack = LANE // dh
    tq, tc = _attn_blocks(kind, S, Sk)
    scale = dh ** -0.5
    fold = _is_power_of_two(scale)
    causal = kind != "mem"
    n_diag = tq // tc if causal else 0
    unroll = 2 if causal else 1
    assert n_diag % unroll == 0 and (Sk // tc) % unroll == 0

    def body(*refs):
        if kind == "fox":
            q_ref, k_ref, v_ref, cc_ref, cr_ref, o_ref, lse_ref = refs
        else:
            q_ref, k_ref, v_ref, o_ref, lse_ref = refs
        i = pl.program_id(1)
        n_full = (i * tq) // tc if causal else Sk // tc
        qpos = i * tq + lax.broadcasted_iota(jnp.int32, (tq, tc), 0)
        kio = lax.broadcasted_iota(jnp.int32, (tq, tc), 1)
        heads = range(pack)
        sel = _head_lanes(pack, dh)
        q2 = q_ref[...] * scale if fold else q_ref[...]
        qs = [_only_head(sel, hh, q2) for hh in heads]

        def kv(jc):
            off = pl.multiple_of(jc * tc, tc)
            return off, k_ref[pl.ds(off, tc), :], v_ref[pl.ds(off, tc), :]

        if kind == "sb":
            tri = (lax.broadcasted_iota(jnp.int32, (tc, tc), 0) > lax.broadcasted_iota(jnp.int32, (tc, tc), 1)
                   ).astype(BF16)

            def chunk(jc, masked, runs, acc):
                off, k2, v2 = kv(jc)
                new_runs, pv = [], []
                for hh in heads:
                    lb, ln = _sb_logs(_dot_nt(qs[hh], k2))
                    if masked:
                        mask = (off + kio) < qpos
                        ln = jnp.where(mask, ln, 0.0)
                    w = jnp.exp(lb + _cumdot(ln, tri) + runs[hh])
                    if masked:
                        w = jnp.where(mask, w, 0.0)
                    pv.append(_dot(w.astype(BF16), v2))
                    new_runs.append(runs[hh] + jnp.sum(ln, axis=1, keepdims=True))
                return tuple(new_runs), acc + _by_head(sel, pv)

            state = (tuple(jnp.zeros((tq, 1), F32) for _ in heads), jnp.zeros((tq, LANE), F32))
            for d in range(n_diag - 1, -1, -1):
                state = chunk(n_full + d, True, *state)

            def trip(t, st):
                for u in range(unroll):
                    st = chunk(n_full - 1 - unroll * t - u, False, *st)
                return st

            runs, acc = lax.fori_loop(0, n_full // unroll, trip, state)
            o_ref[...] = acc.astype(o_ref.dtype)
            for hh in heads:
                lse_ref[hh] = runs[hh]
        else:
            def chunk(jc, masked, ms, ls, acc):
                off, k2, v2 = kv(jc)
                new_ms, new_ls, alphas, pv = [], [], [], []
                for hh in heads:
                    z = _dot_nt(qs[hh], k2)
                    if not fold:
                        z = z * scale
                    if kind == "fox":
                        z = z + cc_ref[hh] - cr_ref[hh, pl.ds(jc, 1), :]
                    if masked:
                        z = jnp.where((off + kio) <= qpos, z, NEG)
                    m_new = jnp.maximum(ms[hh], jnp.max(z, axis=1, keepdims=True))
                    alpha = jnp.exp(ms[hh] - m_new)
                    p = jnp.exp(z - m_new)
                    new_ms.append(m_new)
                    new_ls.append(alpha * ls[hh] + jnp.sum(p, axis=1, keepdims=True))
                    alphas.append(alpha)
                    pv.append(_dot(p.astype(BF16), v2))
                return tuple(new_ms), tuple(new_ls), _by_head(sel, alphas) * acc + _by_head(sel, pv)

            state = (tuple(jnp.full((tq, 1), NEG, F32) for _ in heads), tuple(jnp.zeros((tq, 1), F32) for _ in heads),
                     jnp.zeros((tq, LANE), F32))

            def trip(t, st):
                for u in range(unroll):
                    st = chunk(unroll * t + u, False, *st)
                return st

            state = lax.fori_loop(0, n_full // unroll, trip, state)
            for d in range(n_diag):
                state = chunk(n_full + d, True, *state)
            ms, ls, acc = state
            o_ref[...] = (acc / _by_head(sel, ls)).astype(o_ref.dtype)
            for hh in heads:
                lse_ref[hh] = ms[hh] + jnp.log(ls[hh])

    colspec = pl.BlockSpec((pack, tq, 1), lambda g, i: (g, i, 0))
    in_specs, args = [_q_cols(tq, q0), _k_cols(Sk, k0), _k_cols(Sk, v0)], [qa, ka, va]
    if kind == "fox":
        in_specs += [colspec, pl.BlockSpec((pack, Sk // tc, tc), lambda g, i: (g, 0, 0))]
        args += [ccol, crow]
    return pl.pallas_call(
        body, name="attn_fwd_" + kind, grid=(n_heads // pack, S // tq),
        in_specs=in_specs, out_specs=[_q_cols(tq, 0), colspec],
        out_shape=[jax.ShapeDtypeStruct((S, n_heads * dh), BF16), jax.ShapeDtypeStruct((n_heads, S, 1), F32)],
        compiler_params=_params(),
    )(*args)


def _attn_bwd(kind, q, k, v, o, do, n_heads, dh, ccol=None, crow=None, lse=None):
    (qa, q0), (ka, k0), (va, v0) = q, k, v
    S, Sk = qa.shape[0], ka.shape[0]
    pack = LANE // dh
    tq, tc = _attn_blocks(kind, S, Sk)
    scale = dh ** -0.5
    fold = _is_power_of_two(scale)
    causal = kind != "mem"
    n_diag = tq // tc if causal else 0
    unroll = 2 if causal else 1
    assert n_diag % unroll == 0 and (Sk // tc) % unroll == 0
    nq = S // tq

    def body(*refs):
        if kind == "fox":
            (q_ref, k_ref, v_ref, o_ref, do_ref, cc_ref, cr_ref, lse_ref,
             dq_ref, dk_ref, dv_ref, dc_ref, dcc_ref, dk_acc, dv_acc, dc_acc) = refs
        else:
            q_ref, k_ref, v_ref, o_ref, do_ref, lse_ref, dq_ref, dk_ref, dv_ref, dk_acc, dv_acc = refs
        i = pl.program_id(1)

        @pl.when(i == 0)
        def _():
            dk_acc[...] = jnp.zeros_like(dk_acc)
            dv_acc[...] = jnp.zeros_like(dv_acc)
            if kind == "fox":
                dc_acc[...] = jnp.zeros_like(dc_acc)

        n_full = (i * tq) // tc if causal else Sk // tc
        qpos = i * tq + lax.broadcasted_iota(jnp.int32, (tq, tc), 0)
        kio = lax.broadcasted_iota(jnp.int32, (tq, tc), 1)
        heads = range(pack)
        sel = _head_lanes(pack, dh)
        q2 = q_ref[...] * scale if fold else q_ref[...]
        do2 = do_ref[...]
        qs = [_only_head(sel, hh, q2) for hh in heads]
        dos = [_only_head(sel, hh, do2) for hh in heads]

        def kv(jc):
            off = pl.multiple_of(jc * tc, tc)
            return off, k_ref[pl.ds(off, tc), :], v_ref[pl.ds(off, tc), :]

        def accumulate(off, k2, dzb, wb, dq):
            dk_acc[pl.ds(off, tc), :] += _by_head(sel, [_dot_tn(dzb[hh], q2) for hh in heads])
            dv_acc[pl.ds(off, tc), :] += _by_head(sel, [_dot_tn(wb[hh], do2) for hh in heads])
            return dq + _by_head(sel, [_dot(dzb[hh], k2) for hh in heads])

        if kind == "sb":
            r = lax.broadcasted_iota(jnp.int32, (tc, tc), 0)
            cidx = lax.broadcasted_iota(jnp.int32, (tc, tc), 1)
            tri_inc = (r <= cidx).astype(BF16)
            tri_exc = (r < cidx).astype(BF16)

            def chunk(jc, masked, pres, pres_e, dq):
                off, k2, v2 = kv(jc)
                new_pres, new_pres_e, dzb, wb = [], [], [], []
                for hh in heads:
                    lb, ln = _sb_logs(_dot_nt(qs[hh], k2))
                    if masked:
                        mask = (off + kio) < qpos
                        ln = jnp.where(mask, ln, 0.0)
                    w = jnp.exp(lb + (lse_ref[hh] - pres[hh] - _cumdot(ln, tri_inc)))
                    if masked:
                        w = jnp.where(mask, w, 0.0)
                    e = w * _dot_nt(dos[hh], v2)
                    beta = jnp.exp(lb)
                    dz = e * (1.0 - beta) - beta * (pres_e[hh] + _cumdot(e, tri_exc))
                    if masked:
                        dz = jnp.where(mask, dz, 0.0)
                    dzb.append(dz.astype(BF16))
                    wb.append(w.astype(BF16))
                    new_pres.append(pres[hh] + jnp.sum(ln, axis=1, keepdims=True))
                    new_pres_e.append(pres_e[hh] + jnp.sum(e, axis=1, keepdims=True))
                return tuple(new_pres), tuple(new_pres_e), accumulate(off, k2, dzb, wb, dq)

            state = (tuple(jnp.zeros((tq, 1), F32) for _ in heads), tuple(jnp.zeros((tq, 1), F32) for _ in heads),
                     jnp.zeros((tq, LANE), F32))
        else:
            prod = o_ref[...].astype(F32) * do2.astype(F32)
            dsum = [jnp.sum(_only_head(sel, hh, prod), axis=1, keepdims=True) for hh in heads]

            def chunk(jc, masked, rowsums, dq):
                off, k2, v2 = kv(jc)
                new_rowsums, dsb, pb = [], [], []
                for hh in heads:
                    z = _dot_nt(qs[hh], k2)
                    if not fold:
                        z = z * scale
                    if kind == "fox":
                        z = z + cc_ref[hh] - cr_ref[hh, pl.ds(jc, 1), :]
                    if masked:
                        z = jnp.where((off + kio) <= qpos, z, NEG)
                    p = jnp.exp(z - lse_ref[hh])
                    ds = p * (_dot_nt(dos[hh], v2) - dsum[hh])
                    dsb.append(ds.astype(BF16))
                    pb.append(p.astype(BF16))
                    if kind == "fox":
                        dc_acc[hh, pl.ds(jc, 1), :] -= jnp.sum(ds, axis=0, keepdims=True)
                        new_rowsums.append(rowsums[hh] + jnp.sum(ds, axis=1, keepdims=True))
                    else:
                        new_rowsums.append(rowsums[hh])
                return tuple(new_rowsums), accumulate(off, k2, dsb, pb, dq)

            state = (tuple(jnp.zeros((tq, 1), F32) for _ in heads), jnp.zeros((tq, LANE), F32))

        def trip(t, st):
            for u in range(unroll):
                st = chunk(unroll * t + u, False, *st)
            return st

        state = lax.fori_loop(0, n_full // unroll, trip, state)
        for d in range(n_diag):
            state = chunk(n_full + d, True, *state)
        dq_ref[...] = (state[-1] * scale).astype(dq_ref.dtype)
        if kind == "fox":
            for hh in heads:
                dcc_ref[hh] = state[0][hh]

        @pl.when(i == nq - 1)
        def _():
            dk = dk_acc[...] if fold else dk_acc[...] * scale
            dk_ref[...] = dk.astype(dk_ref.dtype)
            dv_ref[...] = dv_acc[...].astype(dv_ref.dtype)
            if kind == "fox":
                dc_ref[...] = dc_acc[...]

    colspec = pl.BlockSpec((pack, tq, 1), lambda g, i: (g, i, 0))
    rowspec = pl.BlockSpec((pack, Sk // tc, tc), lambda g, i: (g, 0, 0))
    in_specs = [_q_cols(tq, q0), _k_cols(Sk, k0), _k_cols(Sk, v0), _q_cols(tq, 0), _q_cols(tq, 0)]
    args = [qa, ka, va, o, do]
    if kind == "fox":
        in_specs += [colspec, rowspec]
        args += [ccol, crow]
    in_specs += [colspec]
    args += [lse]
    width = n_heads * dh
    out_specs = [_q_cols(tq, 0), _k_cols(Sk, 0), _k_cols(Sk, 0)]
    out_shape = [jax.ShapeDtypeStruct((S, width), BF16), jax.ShapeDtypeStruct((Sk, width), BF16),
                 jax.ShapeDtypeStruct((Sk, width), BF16)]
    scratch = [pltpu.VMEM((Sk, LANE), F32), pltpu.VMEM((Sk, LANE), F32)]
    if kind == "fox":
        out_specs += [rowspec, colspec]
        out_shape += [jax.ShapeDtypeStruct((n_heads, Sk // tc, tc), F32), jax.ShapeDtypeStruct((n_heads, S, 1), F32)]
        scratch.append(pltpu.VMEM((pack, Sk // tc, tc), F32))
    return pl.pallas_call(
        body, name="attn_bwd_" + kind, grid=(n_heads // pack, nq),
        in_specs=in_specs, out_specs=out_specs, out_shape=out_shape, scratch_shapes=scratch,
        compiler_params=_params(),
    )(*args)


def _mem_norm(mem, g):
    M, D = mem.shape

    def body(mem_ref, g_ref, out_ref):
        out_ref[...] = _rms(mem_ref[...], g_ref[...]).astype(BF16)

    return pl.pallas_call(
        body, name="mem_norm", grid=(1,),
        in_specs=[_whole((M, D)), _whole((1, D))], out_specs=_whole((M, D)),
        out_shape=jax.ShapeDtypeStruct((M, D), BF16), compiler_params=_params(),
    )(mem, g)


def _mem_norm_bwd(mem, g, dmem_n):
    M, D = mem.shape
    L = dmem_n.shape[0]

    def body(mem_ref, g_ref, d_ref, dg_ref):
        d = d_ref[0]
        for l in range(1, L):
            d = d + d_ref[l]
        _, vjp = jax.vjp(_rms, mem_ref[...], g_ref[...])
        dg_ref[...] = vjp(d)[1]

    return pl.pallas_call(
        body, name="mem_norm_bwd", grid=(1,),
        in_specs=[_whole((M, D)), _whole((1, D)), _whole((L, M, D))], out_specs=_whole((1, D)),
        out_shape=jax.ShapeDtypeStruct((1, D), F32), compiler_params=_params(),
    )(mem, g, dmem_n)


def _loss_head(h, target):
    S, D = h.shape
    bm = _pick(S, 512)

    def body(h_ref, t_ref, dh_ref, loss_ref):
        err = h_ref[...] - t_ref[...]
        dh_ref[...] = err * (1.0 / D)

        @pl.when(pl.program_id(0) == 0)
        def _():
            loss_ref[...] = jnp.zeros_like(loss_ref)

        loss_ref[...] += 0.5 * jnp.sum(jnp.mean(err * err, axis=-1, keepdims=True), axis=0, keepdims=True)

    return pl.pallas_call(
        body, name="loss_head", grid=(S // bm,),
        in_specs=[_rows(bm, D), _rows(bm, D)], out_specs=[_rows(bm, D), _whole((8, LANE))],
        out_shape=[jax.ShapeDtypeStruct((S, D), F32), jax.ShapeDtypeStruct((8, LANE), F32)],
        compiler_params=_params(),
    )(h, target)


def _adamw(w, g, m, v, name):
    R, C = w.shape
    rb = R if R * C * 4 <= (1 << 20) else _pick(R, 256)
    if R % rb:
        rb = R
    c1 = 1.0 - ADAM_B1 ** ADAM_STEP
    c2 = 1.0 - ADAM_B2 ** ADAM_STEP

    def body(w_ref, g_ref, m_ref, v_ref, d_ref, mo_ref, vo_ref):
        gv = g_ref[...]
        mn = ADAM_B1 * m_ref[...] + (1.0 - ADAM_B1) * gv
        vn = ADAM_B2 * v_ref[...] + (1.0 - ADAM_B2) * (gv * gv)
        mo_ref[...] = mn
        vo_ref[...] = vn
        d_ref[...] = -ADAM_LR * ((mn / c1) / (jnp.sqrt(vn / c2) + ADAM_EPS) + ADAM_WD * w_ref[...])

    return pl.pallas_call(
        body, name=name, grid=(R // rb,),
        in_specs=[_rows(rb, C)] * 4, out_specs=[_rows(rb, C)] * 3,
        out_shape=[jax.ShapeDtypeStruct((R, C), F32)] * 3, compiler_params=_params(),
    )(w, g, m, v)


ANY = pl.BlockSpec(memory_space=pl.ANY)
MESH = pl.DeviceIdType.MESH


def _place():
    x, y, c = lax.axis_index("x"), lax.axis_index("y"), lax.axis_index("c")
    others = [(1 - x, y), (x, 1 - y), (1 - x, 1 - y)]
    return x, y, c, others


def _place_own(loc, chip_idx):
    _, R, C = loc.shape
    rb = _pick(R, 2 * FLAT_ROW_BLOCK)

    def body(chip_ref, loc_ref, out_ref):
        out_ref[...] = loc_ref[...]

    return pl.pallas_call(
        body, name="place_own",
        grid_spec=pltpu.PrefetchScalarGridSpec(
            num_scalar_prefetch=1, grid=(2, R // rb),
            in_specs=[pl.BlockSpec((None, rb, C), lambda hf, i, chip_ref: (hf, i, 0))],
            out_specs=pl.BlockSpec((None, None, rb, C), lambda hf, i, chip_ref: (chip_ref[0], hf, i, 0))),
        out_shape=jax.ShapeDtypeStruct((N_CHIPS, 2, R, C), loc.dtype), compiler_params=_params(),
    )(chip_idx, loc)


def _gather_weights(loc, own):
    _, R, C = loc.shape

    def body(loc_ref, own_ref, out_ref, send_sems, recv_sems):
        del own_ref
        x, y, c, others = _place()
        me = 2 * x + y
        sibling = (x, y, 1 - c)

        def copy(k, src, dst, to):
            return pltpu.make_async_remote_copy(src_ref=src, dst_ref=dst, send_sem=send_sems.at[k],
                                                recv_sem=recv_sems.at[k], device_id=to, device_id_type=MESH)

        first = [copy(j, loc_ref.at[c], out_ref.at[me, c], (ox, oy, c)) for j, (ox, oy) in enumerate(others)]
        for cp in first:
            cp.start()
        passed = []
        for j, (ox, oy) in enumerate(others):
            landed = out_ref.at[2 * ox + oy, c]
            copy(j, loc_ref.at[c], landed, sibling).wait_recv()
            cp = copy(3 + j, landed, landed, sibling)
            cp.start()
            passed.append(cp)
        for j, (ox, oy) in enumerate(others):
            copy(3 + j, loc_ref.at[c], out_ref.at[2 * ox + oy, 1 - c], sibling).wait_recv()
        for cp in first + passed:
            cp.wait_send()

    return pl.pallas_call(
        body, name="gather_weights", in_specs=[ANY, ANY], out_specs=ANY,
        out_shape=jax.ShapeDtypeStruct((N_CHIPS, 2, R, C), loc.dtype), input_output_aliases={1: 0},
        scratch_shapes=[pltpu.SemaphoreType.DMA((6,)), pltpu.SemaphoreType.DMA((6,))],
    )(loc, own)


def _pair_exchange(g):
    _, _, R, C = g.shape

    def body(g_ref, out_ref, send_sem, recv_sem):
        x, y, c, _ = _place()
        cp = pltpu.make_async_remote_copy(src_ref=g_ref.at[1 - c], dst_ref=out_ref, send_sem=send_sem,
                                          recv_sem=recv_sem, device_id=(x, y, 1 - c), device_id_type=MESH)
        cp.start()
        cp.wait()

    return pl.pallas_call(
        body, name="pair_exchange", in_specs=[ANY], out_specs=ANY,
        out_shape=jax.ShapeDtypeStruct((N_CHIPS, R, C), g.dtype),
        scratch_shapes=[pltpu.SemaphoreType.DMA, pltpu.SemaphoreType.DMA],
    )(g)


def _pair_sum(g, sib, c_idx):
    _, _, R, C = g.shape
    rb = _pick(R, 512)

    def body(c_ref, g_ref, s_ref, o_ref):
        o_ref[...] = (g_ref[...].astype(F32) + s_ref[...].astype(F32)).astype(o_ref.dtype)

    return pl.pallas_call(
        body, name="pair_sum",
        grid_spec=pltpu.PrefetchScalarGridSpec(
            num_scalar_prefetch=1, grid=(N_CHIPS, R // rb),
            in_specs=[pl.BlockSpec((None, None, rb, C), lambda j, i, c_ref: (c_ref[0], j, i, 0)),
                      pl.BlockSpec((None, rb, C), lambda j, i, c_ref: (j, i, 0))],
            out_specs=pl.BlockSpec((None, rb, C), lambda j, i, c_ref: (j, i, 0))),
        out_shape=jax.ShapeDtypeStruct((N_CHIPS, R, C), g.dtype), compiler_params=_params(),
    )(c_idx, g, sib)


def _chip_exchange(p):
    _, R, C = p.shape

    def body(p_ref, out_ref, send_sems, recv_sems):
        x, y, c, others = _place()
        copies = []
        for j, (ox, oy) in enumerate(others):
            cp = pltpu.make_async_remote_copy(src_ref=p_ref.at[2 * ox + oy], dst_ref=out_ref.at[j],
                                              send_sem=send_sems.at[j], recv_sem=recv_sems.at[j],
                                              device_id=(ox, oy, c), device_id_type=MESH)
            cp.start()
            copies.append(cp)
        for cp in copies:
            cp.wait()

    return pl.pallas_call(
        body, name="chip_exchange", in_specs=[ANY], out_specs=ANY,
        out_shape=jax.ShapeDtypeStruct((N_CHIPS - 1, R, C), p.dtype),
        scratch_shapes=[pltpu.SemaphoreType.DMA((3,)), pltpu.SemaphoreType.DMA((3,))],
    )(p)


def _chip_sum(p, r, chip_idx):
    _, R, C = r.shape
    rb = _pick(R, 512)

    def body(chip_ref, p_ref, r_ref, o_ref):
        acc = p_ref[...].astype(F32)
        for j in range(N_CHIPS - 1):
            acc = acc + r_ref[j].astype(F32)
        o_ref[...] = acc

    return pl.pallas_call(
        body, name="chip_sum",
        grid_spec=pltpu.PrefetchScalarGridSpec(
            num_scalar_prefetch=1, grid=(R // rb,),
            in_specs=[pl.BlockSpec((None, rb, C), lambda i, chip_ref: (chip_ref[0], i, 0)),
                      pl.BlockSpec((N_CHIPS - 1, rb, C), lambda i, chip_ref: (0, i, 0))],
            out_specs=pl.BlockSpec((rb, C), lambda i, chip_ref: (i, 0))),
        out_shape=jax.ShapeDtypeStruct((R, C), F32), compiler_params=_params(),
    )(chip_idx, p, r)


def _pair_swap(rh):
    R, C = rh.shape

    def body(rh_ref, out_ref, send_sem, recv_sem):
        x, y, c, _ = _place()
        cp = pltpu.make_async_remote_copy(src_ref=rh_ref, dst_ref=out_ref, send_sem=send_sem,
                                          recv_sem=recv_sem, device_id=(x, y, 1 - c), device_id_type=MESH)
        cp.start()
        cp.wait()

    return pl.pallas_call(
        body, name="pair_swap", in_specs=[ANY], out_specs=ANY,
        out_shape=jax.ShapeDtypeStruct((R, C), rh.dtype),
        scratch_shapes=[pltpu.SemaphoreType.DMA, pltpu.SemaphoreType.DMA],
    )(rh)


def _all_reduce_small(s):
    R, C = s.shape

    def body(s_ref, o_ref, buf, send_sems, recv_sems):
        x, y, c, _ = _place()
        me = 4 * x + 2 * y + c
        sends = []
        for k in range(1, N_DEV):
            fx, fy, fc = (k >> 2) & 1, (k >> 1) & 1, k & 1
            to = (x ^ fx, y ^ fy, c ^ fc)
            cp = pltpu.make_async_remote_copy(src_ref=s_ref, dst_ref=buf.at[me], send_sem=send_sems.at[k - 1],
                                              recv_sem=recv_sems.at[k - 1], device_id=to, device_id_type=MESH)
            cp.start()
            sends.append(cp)
        buf[me] = s_ref[...]
        for k in range(1, N_DEV):
            fx, fy, fc = (k >> 2) & 1, (k >> 1) & 1, k & 1
            frm = 4 * (x ^ fx) + 2 * (y ^ fy) + (c ^ fc)
            pltpu.make_async_remote_copy(src_ref=s_ref, dst_ref=buf.at[frm], send_sem=send_sems.at[k - 1],
                                         recv_sem=recv_sems.at[k - 1], device_id=(x, y, c),
                                         device_id_type=MESH).wait_recv()
        acc = buf[0]
        for d in range(1, N_DEV):
            acc = acc + buf[d]
        o_ref[...] = acc
        for cp in sends:
            cp.wait_send()

    vm = pl.BlockSpec(memory_space=pltpu.VMEM)
    return pl.pallas_call(
        body, name="all_reduce_small", in_specs=[vm], out_specs=vm,
        out_shape=jax.ShapeDtypeStruct((R, C), F32),
        scratch_shapes=[pltpu.VMEM((N_DEV, R, C), F32), pltpu.SemaphoreType.DMA((N_DEV - 1,)),
                        pltpu.SemaphoreType.DMA((N_DEV - 1,))],
    )(s)


def _padded(n):
    return -(-n // FLAT_UNIT) * FLAT_UNIT


def _pack_flat(pieces, dtype, row_block=FLAT_ROW_BLOCK):
    flat = []
    for p in pieces:
        p = p.reshape(-1).astype(dtype)
        flat.append(jnp.pad(p, (0, _padded(p.size) - p.size)))
    total = sum(p.size for p in flat)
    flat.append(jnp.zeros((-total) % (row_block * FLAT_COLS), dtype))
    return jnp.concatenate(flat).reshape(-1, FLAT_COLS)


def _unpack_flat(flat, shapes):
    lead = flat.shape[:-2]
    flat = flat.reshape(lead + (-1,))
    out, off = [], 0
    for shp in shapes:
        n = math.prod(shp)
        out.append(flat[..., off:off + n].reshape(lead + tuple(shp)))
        off += _padded(n)
    return out


def _flat_offsets(shapes):
    offs, off = [], 0
    for shp in shapes:
        offs.append(off)
        off += _padded(math.prod(shp))
    return offs


def _slab(t, axis, j):
    n = t.shape[axis - 1] // N_CHIPS
    return lax.slice_in_dim(t, j * n, (j + 1) * n, axis=axis - 1)


def _layer_fwd(h0, mem_n, wl, dims):
    n_sb, n_fx, n_mem, sbw, fxw, memw = dims
    n1, gate1, up1, a1 = _ffn_fwd_up(h0, wl["ffn1_pre_g"], wl["ffn1_w_gate"], wl["ffn1_w_up"])
    h1, f1 = _ffn_fwd_down(a1, wl["ffn1_w_down"], h0, wl["ffn1_post_g"])

    u, proj, fl, sg = _mix_fwd_in(h1, wl["mix_pre_g"], wl["w_in"], wl["w_gate"], wl["b_gate"], wl["b_forget"])
    c = _fox_cumsum(fl)
    S = h0.shape[0]
    tc = _attn_blocks("fox", S, S)[1]
    ct = c[:, :n_fx].T
    ccol, crow = ct.reshape(n_fx, S, 1), ct.reshape(n_fx, S // tc, tc)
    qkv_sb = [(proj, k * sbw) for k in range(3)]
    qkv_fx = [(proj, 3 * sbw + k * fxw) for k in range(3)]
    kv = _matmul(mem_n, wl["w_mem_kv"], out_dtype=BF16, name="mem_kv")
    qkv_mem = [(proj, 3 * sbw + 3 * fxw), (kv, 0), (kv, memw)]
    o_sb, tot_sb = _attn_fwd("sb", *qkv_sb, n_sb, HEAD_DIM)
    o_fx, lse_fx = _attn_fwd("fox", *qkv_fx, n_fx, HEAD_DIM, ccol, crow)
    o_mem, lse_mem = _attn_fwd("mem", *qkv_mem, n_mem, MEM_HEAD_DIM)
    h2, zmix, merged = _mix_fwd_out(o_sb, o_fx, o_mem, sg, wl["w_br_sb"], wl["w_br_fox"], wl["w_br_mem"],
                                    wl["w_out"], h1, wl["mix_post_g"])

    n2, gate2, up2, a2 = _ffn_fwd_up(h2, wl["ffn2_pre_g"], wl["ffn2_w_gate"], wl["ffn2_w_up"])
    h3, f2 = _ffn_fwd_down(a2, wl["ffn2_w_down"], h2, wl["ffn2_post_g"])
    saved = dict(h0=h0, n1=n1, gate1=gate1, up1=up1, a1=a1, f1=f1, h1=h1, u=u, fl=fl, sg=sg,
                 qkv_sb=qkv_sb, qkv_fx=qkv_fx, qkv_mem=qkv_mem, ccol=ccol, crow=crow, o_sb=o_sb, o_fx=o_fx, o_mem=o_mem,
                 tot_sb=tot_sb, lse_fx=lse_fx, lse_mem=lse_mem,
                 zmix=zmix, merged=merged, h2=h2, n2=n2, gate2=gate2, up2=up2, a2=a2, f2=f2)
    return h3, saved


def _ffn_bwd(dh, sv, wl, tag, h_in):
    n, gate, up, a, f = (sv[k + tag] for k in ("n", "gate", "up", "a", "f"))
    pre = "ffn" + tag
    df, dgate, dup, dg_post = _ffn_bwd_down(dh, f, wl[pre + "_post_g"], wl[pre + "_w_down"], gate, up)
    dh_in, dg_pre = _ffn_bwd_up(dgate, dup, wl[pre + "_w_gate"], wl[pre + "_w_up"], h_in, wl[pre + "_pre_g"], dh)
    grads = {pre + "_post_g": dg_post, pre + "_pre_g": dg_pre,
             pre + "_w_down": _matmul(a, df, ta=True, name="dw_down"),
             pre + "_w_gate": _matmul(n, dgate, ta=True, name="dw_gate"),
             pre + "_w_up": _matmul(n, dup, ta=True, name="dw_up")}
    return dh_in, grads


def _layer_bwd(dh3, mem_n, wl, sv, dims):
    n_sb, n_fx, n_mem, sbw, fxw, memw = dims
    S = dh3.shape[0]
    dh2, grads = _ffn_bwd(dh3, sv, wl, "2", sv["h2"])

    (dz, db_sb, db_fx, db_mem, do_sb, do_fx, do_mem, dgp, db_gate, dg_post) = _mix_bwd_out(
        dh2, sv["zmix"], wl["mix_post_g"], wl["w_out"], sv["o_sb"], sv["o_fx"], sv["o_mem"],
        wl["w_br_sb"], wl["w_br_fox"], wl["w_br_mem"], sv["sg"])
    grads["mix_post_g"] = dg_post
    grads["b_gate"] = db_gate
    grads["w_out"] = _matmul(sv["merged"], dz, ta=True, name="dw_out")
    grads["w_br_sb"] = _matmul(sv["o_sb"], db_sb, ta=True, name="dw_br_sb")
    grads["w_br_fox"] = _matmul(sv["o_fx"], db_fx, ta=True, name="dw_br_fox")
    grads["w_br_mem"] = _matmul(sv["o_mem"], db_mem, ta=True, name="dw_br_mem")

    dq_sb, dk_sb, dv_sb = _attn_bwd("sb", *sv["qkv_sb"], sv["o_sb"], do_sb, n_sb, HEAD_DIM, lse=sv["tot_sb"])
    dq_fx, dk_fx, dv_fx, dcrow, dccol = _attn_bwd("fox", *sv["qkv_fx"], sv["o_fx"], do_fx, n_fx, HEAD_DIM,
                                                  sv["ccol"], sv["crow"], sv["lse_fx"])
    dq_mem, dk_mem, dv_mem = _attn_bwd("mem", *sv["qkv_mem"], sv["o_mem"], do_mem, n_mem, MEM_HEAD_DIM,
                                       lse=sv["lse_mem"])
    dkv = jnp.concatenate([dk_mem, dv_mem], axis=1)
    grads["w_mem_kv"] = _matmul(mem_n, dkv, ta=True, name="dw_mem_kv")
    dmem_n = _matmul(dkv, wl["w_mem_kv"], tb=True, out_dtype=F32, name="dmem_n")

    dc = jnp.pad((dcrow.reshape(n_fx, S) + dccol.reshape(n_fx, S)).T, ((0, 0), (0, LANE - n_fx)))
    dfl, db_forget = _fox_dlogit(dc, sv["fl"])
    grads["b_forget"] = db_forget
    dproj = jnp.concatenate([dq_sb, dk_sb, dv_sb, dq_fx, dk_fx, dv_fx, dq_mem, dfl], axis=1)
    dh1, dg_pre = _mix_bwd_in(dproj, dgp, wl["w_in"], wl["w_gate"], sv["h1"], wl["mix_pre_g"], dh2)
    grads["mix_pre_g"] = dg_pre
    grads["w_in"] = _matmul(sv["u"], dproj, ta=True, name="dw_in")
    grads["w_gate"] = _matmul(sv["u"], dgp, ta=True, name="dw_gate_mix")

    dh0, g1 = _ffn_bwd(dh1, sv, wl, "1", sv["h0"])
    grads.update(g1)
    return dh0, grads, dmem_n


def kernel(x, mem, ffn1_pre_g, ffn1_post_g, ffn1_w_gate, ffn1_w_up, ffn1_w_down, mix_pre_g, mix_post_g, w_in, b_forget, mem_norm_g, w_mem_kv, w_gate, b_gate, w_br_sb, w_br_fox, w_br_mem, w_out, ffn2_pre_g, ffn2_post_g, ffn2_w_gate, ffn2_w_up, ffn2_w_down, loss_target, m_ffn1_pre_g, m_ffn1_post_g, m_ffn1_w_gate, m_ffn1_w_up, m_ffn1_w_down, m_mix_pre_g, m_mix_post_g, m_w_in, m_b_forget, m_mem_norm_g, m_w_mem_kv, m_w_gate, m_b_gate, m_w_br_sb, m_w_br_fox, m_w_br_mem, m_w_out, m_ffn2_pre_g, m_ffn2_post_g, m_ffn2_w_gate, m_ffn2_w_up, m_ffn2_w_down, v_ffn1_pre_g, v_ffn1_post_g, v_ffn1_w_gate, v_ffn1_w_up, v_ffn1_w_down, v_mix_pre_g, v_mix_post_g, v_w_in, v_b_forget, v_mem_norm_g, v_w_mem_kv, v_w_gate, v_b_gate, v_w_br_sb, v_w_br_fox, v_w_br_mem, v_w_out, v_ffn2_pre_g, v_ffn2_post_g, v_ffn2_w_gate, v_ffn2_w_up, v_ffn2_w_down):
    args = dict(locals())
    w = {n: args[n] for n in WEIGHTS}
    m = {n: args["m_" + n] for n in WEIGHTS}
    v = {n: args["v_" + n] for n in WEIGHTS}
    L = w["ffn1_pre_g"].shape[0]
    Lh = L // 2
    D = x.shape[2]
    sbw, fxw, memw = w["w_br_sb"].shape[1], w["w_br_fox"].shape[1], w["w_br_mem"].shape[1]
    n_sb, n_fx, n_mem = sbw // HEAD_DIM, fxw // HEAD_DIM, memw // MEM_HEAD_DIM
    dims = (n_sb, n_fx, n_mem, sbw, fxw, memw)
    qkv_w = 3 * sbw + 3 * fxw
    c_idx = lax.axis_index("c")
    c_arr = c_idx.reshape(1).astype(jnp.int32)
    chip_arr = (2 * lax.axis_index("x") + lax.axis_index("y")).reshape(1).astype(jnp.int32)

    piece_shapes = [w[n].shape[1:] for n, _ in BIG for _ in range(Lh)]
    row_offs = [off // FLAT_COLS for off in _flat_offsets(piece_shapes)]
    loc = jnp.stack([_pack_flat([w[n][hf * Lh + li] for n, _ in BIG for li in range(Lh)], BF16) for hf in range(2)])
    gathered = _gather_weights(loc, _place_own(loc, chip_arr))

    def piece(buf, k, li):
        a, b = piece_shapes[k * Lh + li]
        r0 = row_offs[k * Lh + li]
        rows = buf[..., r0:r0 + _padded(a * b) // FLAT_COLS, :]
        return rows.reshape(buf.shape[:-2] + (-1,))[..., :a * b].reshape(buf.shape[:-2] + (a, b))

    def layer_weights(l):
        hf, li = divmod(l, Lh)
        wl = {}
        for k, (n, axis) in enumerate(BIG):
            shards = piece(gathered[:, hf], k, li)
            a, b = shards.shape[1:]
            wl[n] = shards.transpose(1, 0, 2).reshape(a, N_CHIPS * b) if axis == 2 else shards.reshape(N_CHIPS * a, b)
        wi = wl["w_in"]
        wl["w_in"] = jnp.concatenate([wi[:, :qkv_w], wi[:, qkv_w + n_fx:], wi[:, qkv_w:qkv_w + n_fx],
                                      jnp.zeros((D, LANE - n_fx), BF16)], axis=1)
        for n in SMALL:
            if n != "mem_norm_g":
                wl[n] = w[n][l][None, :]
        wl["b_forget"] = jnp.pad(wl["b_forget"], ((0, 0), (0, LANE - n_fx)))
        return wl

    g_mem = w["mem_norm_g"][None, :]

    mem_n = _mem_norm(mem[0], g_mem)
    h, wls, saved = x[0], [], []
    for l in range(L):
        wls.append(layer_weights(l))
        h, sv = _layer_fwd(h, mem_n, wls[l], dims)
        saved.append(sv)
    dh, loss_tile = _loss_head(h, loss_target[0])
    loss = lax.psum(loss_tile[0, 0], ("x", "y", "c"))
    gl, dmem_n = [None] * L, [None] * L
    for l in reversed(range(L)):
        dh, gl[l], dmem_n[l] = _layer_bwd(dh, mem_n, wls[l], saved[l], dims)
        gi = gl[l]["w_in"]
        gl[l]["w_in"] = jnp.concatenate([gi[:, :qkv_w], gi[:, qkv_w + memw:qkv_w + memw + n_fx],
                                         gi[:, qkv_w:qkv_w + memw]], axis=1)
    grad_x = dh
    g_mem_norm = _mem_norm_bwd(mem[0], g_mem, jnp.stack(dmem_n))

    partial = jnp.stack([
        jnp.stack([_pack_flat([_slab(gl[hf * Lh + li][n], axis, j) for n, axis in BIG for li in range(Lh)], BF16)
                   for j in range(N_CHIPS)]) for hf in range(2)])
    sib = _pair_exchange(partial)
    pair = _pair_sum(partial, sib, c_arr)
    mine = _chip_sum(pair, _chip_exchange(pair), chip_arr)
    theirs = _pair_swap(mine)
    grad = {n: jnp.stack([jnp.where(c_idx == l // Lh, piece(mine, k, l % Lh), piece(theirs, k, l % Lh))
                          for l in range(L)]) for k, (n, _) in enumerate(BIG)}

    small_local = {n: (g_mem_norm if n == "mem_norm_g" else
                       jnp.concatenate([gl[l][n][:, :n_fx] if n == "b_forget" else gl[l][n] for l in range(L)]))
                   for n in SMALL}
    small_shapes = [small_local[n].shape for n in SMALL]
    small_sum = _unpack_flat(_all_reduce_small(_pack_flat([small_local[n] for n in SMALL], F32, row_block=16)),
                             small_shapes)
    for n, t in zip(SMALL, small_sum):
        grad[n] = t.reshape(w[n].shape)

    delta, new_m, new_v = {}, {}, {}
    for n in WEIGHTS:
        shp = w[n].shape
        two_d = (1, shp[0]) if len(shp) == 1 else (-1, shp[-1])
        d_, m_, v_ = _adamw(w[n].reshape(two_d), grad[n].reshape(two_d), m[n].reshape(two_d), v[n].reshape(two_d),
                            name="adamw_" + n)
        delta[n], new_m[n], new_v[n] = d_.reshape(shp), m_.reshape(shp), v_.reshape(shp)

    return (loss, grad_x[None], *[grad[n] for n in WEIGHTS], *[delta[n] for n in WEIGHTS],
            *[new_m[n] for n in WEIGHTS], *[new_v[n] for n in WEIGHTS])
```

```python
import math

import jax
import jax.numpy as jnp
from jax import lax
from jax.experimental import pallas as pl
from jax.experimental.pallas import tpu as pltpu

F32 = jnp.float32
BF16 = jnp.bfloat16
RMS_EPS = 1e-6
HEAD_DIM = 64
MEM_HEAD_DIM = 128
LANE = 128
V7X_VMEM_LIMIT_BYTES = 56 * 1024 * 1024
FLAT_COLS = 512
FLAT_UNIT = 16 * FLAT_COLS
FLAT_ROW_BLOCK = 512
N_CHIPS = 4
N_DEV = 8
NEG = float(jnp.finfo(jnp.float32).min)

ADAM_LR = 0.001
ADAM_B1 = 0.9
ADAM_B2 = 0.999
ADAM_EPS = 1e-08
ADAM_WD = 0.01
ADAM_STEP = 10

BIG = (("ffn1_w_gate", 2), ("ffn1_w_up", 2), ("ffn1_w_down", 1), ("w_in", 2), ("w_mem_kv", 1), ("w_gate", 2),
       ("w_br_sb", 2), ("w_br_fox", 2), ("w_br_mem", 2), ("w_out", 1),
       ("ffn2_w_gate", 2), ("ffn2_w_up", 2), ("ffn2_w_down", 1))
SMALL = ("ffn1_pre_g", "ffn1_post_g", "mix_pre_g", "mix_post_g", "b_forget", "mem_norm_g", "b_gate",
         "ffn2_pre_g", "ffn2_post_g")
WEIGHTS = ("ffn1_pre_g", "ffn1_post_g", "ffn1_w_gate", "ffn1_w_up", "ffn1_w_down", "mix_pre_g", "mix_post_g", "w_in",
           "b_forget", "mem_norm_g", "w_mem_kv", "w_gate", "b_gate", "w_br_sb", "w_br_fox", "w_br_mem", "w_out",
           "ffn2_pre_g", "ffn2_post_g", "ffn2_w_gate", "ffn2_w_up", "ffn2_w_down")


def _params(**kw):
    return pltpu.CompilerParams(vmem_limit_bytes=V7X_VMEM_LIMIT_BYTES, **kw)


def _dot(a, b):
    return jnp.dot(a, b, preferred_element_type=F32)


def _dot_nt(a, b):
    return lax.dot_general(a, b, (((1,), (1,)), ((), ())), preferred_element_type=F32)


def _dot_tn(a, b):
    return lax.dot_general(a, b, (((0,), (0,)), ((), ())), preferred_element_type=F32)


def _rms(t, g):
    return t * lax.rsqrt(jnp.mean(t * t, axis=-1, keepdims=True) + RMS_EPS) * g


def _pick(dim, pref):
    if dim <= pref:
        return dim
    for cand in range(pref - pref % LANE, 0, -LANE):
        if dim % cand == 0:
            return cand
    return dim


def _rows(bm, cols):
    return pl.BlockSpec((bm, cols), lambda i: (i, 0))


def _whole(shape):
    nd = len(shape)
    return pl.BlockSpec(shape, lambda i: (0,) * nd)


def _split3(x):
    hi = x.astype(BF16)
    r1 = x - hi.astype(F32)
    mid = r1.astype(BF16)
    lo = (r1 - mid.astype(F32)).astype(BF16)
    return hi, mid, lo


def _cumdot(x, tri):
    hi = x.astype(BF16)
    lo = (x - hi.astype(F32)).astype(BF16)
    return _dot(hi, tri) + _dot(lo, tri)


def _ffn_fwd_up(h, g_pre, wg, wu):
    S, D = h.shape
    F = wg.shape[1]
    bm = _pick(S, 256)

    def body(h_ref, g_ref, wg_ref, wu_ref, n_ref, gate_ref, up_ref, a_ref):
        n = _rms(h_ref[...], g_ref[...]).astype(BF16)
        n_ref[...] = n
        gate = _dot(n, wg_ref[...])
        up = _dot(n, wu_ref[...])
        gate_ref[...] = gate.astype(BF16)
        up_ref[...] = up.astype(BF16)
        a_ref[...] = (gate * jax.nn.sigmoid(gate) * up).astype(BF16)

    return pl.pallas_call(
        body, name="ffn_fwd_up", grid=(S // bm,),
        in_specs=[_rows(bm, D), _whole((1, D)), _whole((D, F)), _whole((D, F))],
        out_specs=[_rows(bm, D), _rows(bm, F), _rows(bm, F), _rows(bm, F)],
        out_shape=[jax.ShapeDtypeStruct((S, D), BF16)] + [jax.ShapeDtypeStruct((S, F), BF16)] * 3,
        compiler_params=_params(),
    )(h, g_pre, wg, wu)


def _ffn_fwd_down(a, wd, h, g_post):
    S, F = a.shape
    D = wd.shape[1]
    bm = _pick(S, 256)

    def body(a_ref, wd_ref, h_ref, g_ref, hout_ref, f_ref):
        f = _dot(a_ref[...], wd_ref[...])
        f_ref[...] = f
        hout_ref[...] = h_ref[...] + 0.5 * _rms(f, g_ref[...])

    return pl.pallas_call(
        body, name="ffn_fwd_down", grid=(S // bm,),
        in_specs=[_rows(bm, F), _whole((F, D)), _rows(bm, D), _whole((1, D))],
        out_specs=[_rows(bm, D), _rows(bm, D)],
        out_shape=[jax.ShapeDtypeStruct((S, D), F32)] * 2,
        compiler_params=_params(),
    )(a, wd, h, g_post)


def _ffn_bwd_down(dh, f, g_post, wd, gate, up):
    S, D = dh.shape
    F = wd.shape[0]
    bm = _pick(S, 256)

    def body(dh_ref, f_ref, g_ref, wd_ref, gate_ref, up_ref, df_ref, dgate_ref, dup_ref, dg_ref):
        _, vjp = jax.vjp(lambda t, g: 0.5 * _rms(t, g), f_ref[...], g_ref[...])
        df, dg = vjp(dh_ref[...])

        @pl.when(pl.program_id(0) == 0)
        def _():
            dg_ref[...] = jnp.zeros_like(dg_ref)

        dg_ref[...] += dg
        dfb = df.astype(BF16)
        df_ref[...] = dfb
        da = _dot_nt(dfb, wd_ref[...])
        gt = gate_ref[...].astype(F32)
        sig = jax.nn.sigmoid(gt)
        silu = gt * sig
        dup_ref[...] = (da * silu).astype(BF16)
        dgate_ref[...] = (da * up_ref[...].astype(F32) * (sig + silu * (1.0 - sig))).astype(BF16)

    return pl.pallas_call(
        body, name="ffn_bwd_down", grid=(S // bm,),
        in_specs=[_rows(bm, D), _rows(bm, D), _whole((1, D)), _whole((F, D)), _rows(bm, F), _rows(bm, F)],
        out_specs=[_rows(bm, D), _rows(bm, F), _rows(bm, F), _whole((1, D))],
        out_shape=[jax.ShapeDtypeStruct((S, D), BF16), jax.ShapeDtypeStruct((S, F), BF16),
                   jax.ShapeDtypeStruct((S, F), BF16), jax.ShapeDtypeStruct((1, D), F32)],
        compiler_params=_params(),
    )(dh, f, g_post, wd, gate, up)


def _ffn_bwd_up(dgate, dup, wg, wu, h_in, g_pre, dh):
    S, F = dgate.shape
    D = wg.shape[0]
    bm = _pick(S, 256)

    def body(dgate_ref, dup_ref, wg_ref, wu_ref, h_ref, g_ref, dh_ref, dhin_ref, dg_ref):
        dn = _dot_nt(dgate_ref[...], wg_ref[...]) + _dot_nt(dup_ref[...], wu_ref[...])
        _, vjp = jax.vjp(_rms, h_ref[...], g_ref[...])
        dhx, dg = vjp(dn)

        @pl.when(pl.program_id(0) == 0)
        def _():
            dg_ref[...] = jnp.zeros_like(dg_ref)

        dg_ref[...] += dg
        dhin_ref[...] = dh_ref[...] + dhx

    return pl.pallas_call(
        body, name="ffn_bwd_up", grid=(S // bm,),
        in_specs=[_rows(bm, F), _rows(bm, F), _whole((D, F)), _whole((D, F)), _rows(bm, D), _whole((1, D)),
                  _rows(bm, D)],
        out_specs=[_rows(bm, D), _whole((1, D))],
        out_shape=[jax.ShapeDtypeStruct((S, D), F32), jax.ShapeDtypeStruct((1, D), F32)],
        compiler_params=_params(),
    )(dgate, dup, wg, wu, h_in, g_pre, dh)


def _matmul(a, b, *, ta=False, tb=False, out_dtype=BF16, name):
    M, K = (a.shape[1], a.shape[0]) if ta else a.shape
    N = b.shape[0] if tb else b.shape[1]
    acc_budget = 12 * 1024 * 1024
    bm, bk = _pick(M, 1536), _pick(K, 512)
    while N * bm * 4 > acc_budget and bm % (2 * LANE) == 0:
        bm //= 2
    bn = N if N * bm * 4 <= acc_budget else _pick(N, 1536)
    nk = K // bk

    def body(a_ref, b_ref, o_ref, acc_ref):
        kk = pl.program_id(2)

        @pl.when(kk == 0)
        def _():
            acc_ref[...] = jnp.zeros_like(acc_ref)

        av, bv = a_ref[...], b_ref[...]
        dims = (((0 if ta else 1,), (1 if tb else 0,)), ((), ()))
        acc_ref[...] += lax.dot_general(av, bv, dims, preferred_element_type=F32)

        @pl.when(kk == nk - 1)
        def _():
            o_ref[...] = acc_ref[...].astype(o_ref.dtype)

    a_spec = pl.BlockSpec((bk, bm), lambda i, j, k: (k, i)) if ta else pl.BlockSpec((bm, bk), lambda i, j, k: (i, k))
    b_spec = pl.BlockSpec((bn, bk), lambda i, j, k: (j, k)) if tb else pl.BlockSpec((bk, bn), lambda i, j, k: (k, j))
    return pl.pallas_call(
        body, name=name, grid=(M // bm, N // bn, nk),
        in_specs=[a_spec, b_spec],
        out_specs=pl.BlockSpec((bm, bn), lambda i, j, k: (i, j)),
        out_shape=jax.ShapeDtypeStruct((M, N), out_dtype),
        scratch_shapes=[pltpu.VMEM((bm, bn), F32)],
        compiler_params=_params(),
    )(a, b)


def _mix_fwd_in(h, g_pre, win, wgate, b_gate, b_forget):
    S, D = h.shape
    PW = win.shape[1] - LANE
    G = wgate.shape[1]
    bm = _pick(S, 256)

    def body(h_ref, g_ref, win_ref, wgate_ref, bg_ref, bf_ref, u_ref, proj_ref, fl_ref, sg_ref):
        u = _rms(h_ref[...], g_ref[...]).astype(BF16)
        u_ref[...] = u
        proj = _dot(u, win_ref[...])
        proj_ref[...] = proj[:, :PW].astype(BF16)
        fl_ref[...] = proj[:, PW:] + bf_ref[...]
        sg_ref[...] = jax.nn.sigmoid(_dot(u, wgate_ref[...]) + bg_ref[...]).astype(BF16)

    return pl.pallas_call(
        body, name="mix_fwd_in", grid=(S // bm,),
        in_specs=[_rows(bm, D), _whole((1, D)), _whole((D, PW + LANE)), _whole((D, G)), _whole((1, G)),
                  _whole((1, LANE))],
        out_specs=[_rows(bm, D), _rows(bm, PW), _rows(bm, LANE), _rows(bm, G)],
        out_shape=[jax.ShapeDtypeStruct((S, D), BF16), jax.ShapeDtypeStruct((S, PW), BF16),
                   jax.ShapeDtypeStruct((S, LANE), F32), jax.ShapeDtypeStruct((S, G), BF16)],
        compiler_params=_params(),
    )(h, g_pre, win, wgate, b_gate, b_forget)


def _mix_fwd_out(o_sb, o_fx, o_mem, sg, w_sb, w_fx, w_mem, w_out, h, g_post):
    S, D = h.shape
    bm = _pick(S, 256)
    widths = (o_sb.shape[1], o_fx.shape[1], o_mem.shape[1])

    def body(osb_ref, ofx_ref, omem_ref, sg_ref, wsb_ref, wfx_ref, wmem_ref, wout_ref, h_ref, g_ref,
             hout_ref, z_ref, merged_ref):
        s = sg_ref[...].astype(F32)
        merged = (s[:, :D] * _dot(osb_ref[...], wsb_ref[...]) + s[:, D:2 * D] * _dot(ofx_ref[...], wfx_ref[...])
                  + s[:, 2 * D:] * _dot(omem_ref[...], wmem_ref[...]))
        mb = merged.astype(BF16)
        merged_ref[...] = mb
        z = _dot(mb, wout_ref[...])
        z_ref[...] = z
        hout_ref[...] = h_ref[...] + _rms(z, g_ref[...])

    return pl.pallas_call(
        body, name="mix_fwd_out", grid=(S // bm,),
        in_specs=[_rows(bm, widths[0]), _rows(bm, widths[1]), _rows(bm, widths[2]), _rows(bm, 3 * D),
                  _whole((widths[0], D)), _whole((widths[1], D)), _whole((widths[2], D)), _whole((D, D)),
                  _rows(bm, D), _whole((1, D))],
        out_specs=[_rows(bm, D), _rows(bm, D), _rows(bm, D)],
        out_shape=[jax.ShapeDtypeStruct((S, D), F32), jax.ShapeDtypeStruct((S, D), F32),
                   jax.ShapeDtypeStruct((S, D), BF16)],
        compiler_params=_params(),
    )(o_sb, o_fx, o_mem, sg, w_sb, w_fx, w_mem, w_out, h, g_post)


def _mix_bwd_out(dh, z, g_post, w_out, o_sb, o_fx, o_mem, w_sb, w_fx, w_mem, sg):
    S, D = dh.shape
    bm = _pick(S, 256)
    widths = (o_sb.shape[1], o_fx.shape[1], o_mem.shape[1])

    def body(dh_ref, z_ref, g_ref, wout_ref, osb_ref, ofx_ref, omem_ref, wsb_ref, wfx_ref, wmem_ref, sg_ref,
             dz_ref, dbsb_ref, dbfx_ref, dbmem_ref, dosb_ref, dofx_ref, domem_ref, dgp_ref, dbg_ref, dg_ref):
        _, vjp = jax.vjp(_rms, z_ref[...], g_ref[...])
        dz, dg = vjp(dh_ref[...])

        @pl.when(pl.program_id(0) == 0)
        def _():
            dg_ref[...] = jnp.zeros_like(dg_ref)
            dbg_ref[...] = jnp.zeros_like(dbg_ref)

        dg_ref[...] += dg
        dzb = dz.astype(BF16)
        dz_ref[...] = dzb
        dmerged = _dot_nt(dzb, wout_ref[...])
        s = sg_ref[...].astype(F32)
        branches = ((osb_ref, wsb_ref, dbsb_ref, dosb_ref), (ofx_ref, wfx_ref, dbfx_ref, dofx_ref),
                    (omem_ref, wmem_ref, dbmem_ref, domem_ref))
        for k, (o_ref, w_ref, db_ref, do_ref) in enumerate(branches):
            gs = s[:, k * D:(k + 1) * D]
            dbb = (dmerged * gs).astype(BF16)
            db_ref[...] = dbb
            do_ref[...] = _dot_nt(dbb, w_ref[...]).astype(BF16)
            dgp = dmerged * _dot(o_ref[...], w_ref[...]) * gs * (1.0 - gs)
            dgp_ref[:, k * D:(k + 1) * D] = dgp.astype(BF16)
            dbg_ref[:, k * D:(k + 1) * D] += jnp.sum(dgp, axis=0, keepdims=True)

    return pl.pallas_call(
        body, name="mix_bwd_out", grid=(S // bm,),
        in_specs=[_rows(bm, D), _rows(bm, D), _whole((1, D)), _whole((D, D)),
                  _rows(bm, widths[0]), _rows(bm, widths[1]), _rows(bm, widths[2]),
                  _whole((widths[0], D)), _whole((widths[1], D)), _whole((widths[2], D)), _rows(bm, 3 * D)],
        out_specs=[_rows(bm, D)] * 4 + [_rows(bm, widths[0]), _rows(bm, widths[1]), _rows(bm, widths[2]),
                                        _rows(bm, 3 * D), _whole((1, 3 * D)), _whole((1, D))],
        out_shape=[jax.ShapeDtypeStruct((S, D), BF16)] * 4
        + [jax.ShapeDtypeStruct((S, w), BF16) for w in widths]
        + [jax.ShapeDtypeStruct((S, 3 * D), BF16), jax.ShapeDtypeStruct((1, 3 * D), F32),
           jax.ShapeDtypeStruct((1, D), F32)],
        compiler_params=_params(),
    )(dh, z, g_post, w_out, o_sb, o_fx, o_mem, w_sb, w_fx, w_mem, sg)


def _mix_bwd_in(dproj, dgp, win, wgate, h_in, g_pre, dh):
    S, PWL = dproj.shape
    G = dgp.shape[1]
    D = h_in.shape[1]
    bm = _pick(S, 256)

    def body(dproj_ref, dgp_ref, win_ref, wgate_ref, h_ref, g_ref, dh_ref, dhin_ref, dg_ref):
        du = _dot_nt(dproj_ref[...], win_ref[...]) + _dot_nt(dgp_ref[...], wgate_ref[...])
        _, vjp = jax.vjp(_rms, h_ref[...], g_ref[...])
        dhx, dg = vjp(du)

        @pl.when(pl.program_id(0) == 0)
        def _():
            dg_ref[...] = jnp.zeros_like(dg_ref)

        dg_ref[...] += dg
        dhin_ref[...] = dh_ref[...] + dhx

    return pl.pallas_call(
        body, name="mix_bwd_in", grid=(S // bm,),
        in_specs=[_rows(bm, PWL), _rows(bm, G), _whole((D, PWL)), _whole((D, G)), _rows(bm, D), _whole((1, D)),
                  _rows(bm, D)],
        out_specs=[_rows(bm, D), _whole((1, D))],
        out_shape=[jax.ShapeDtypeStruct((S, D), F32), jax.ShapeDtypeStruct((1, D), F32)],
        compiler_params=_params(),
    )(dproj, dgp, win, wgate, h_in, g_pre, dh)


def _log_sigmoid(x):
    return jnp.minimum(x, 0.0) - jnp.log(1.0 + jnp.exp(-jnp.abs(x)))


def _fox_cumsum(fl):
    S = fl.shape[0]
    rb = _pick(S, LANE)

    def body(fl_ref, c_ref, carry_ref):
        @pl.when(pl.program_id(0) == 0)
        def _():
            carry_ref[...] = jnp.zeros_like(carry_ref)

        r = lax.broadcasted_iota(jnp.int32, (rb, rb), 0)
        cidx = lax.broadcasted_iota(jnp.int32, (rb, rb), 1)
        tri = (cidx <= r).astype(BF16)
        hi, mid, lo = _split3(_log_sigmoid(fl_ref[...]))
        c = _dot(tri, hi) + _dot(tri, mid) + _dot(tri, lo) + carry_ref[...]
        c_ref[...] = c
        carry_ref[...] = c[rb - 1:rb, :]

    return pl.pallas_call(
        body, name="fox_cumsum", grid=(S // rb,),
        in_specs=[_rows(rb, LANE)], out_specs=_rows(rb, LANE),
        out_shape=jax.ShapeDtypeStruct((S, LANE), F32),
        scratch_shapes=[pltpu.VMEM((1, LANE), F32)],
        compiler_params=_params(),
    )(fl)


def _fox_dlogit(dc, fl):
    S = fl.shape[0]
    rb = _pick(S, LANE)
    nb = S // rb

    def body(dc_ref, fl_ref, dfl_ref, dbf_ref, carry_ref):
        @pl.when(pl.program_id(0) == 0)
        def _():
            carry_ref[...] = jnp.zeros_like(carry_ref)
            dbf_ref[...] = jnp.zeros_like(dbf_ref)

        r = lax.broadcasted_iota(jnp.int32, (rb, rb), 0)
        cidx = lax.broadcasted_iota(jnp.int32, (rb, rb), 1)
        tri = (cidx >= r).astype(BF16)
        hi, mid, lo = _split3(dc_ref[...])
        rc = _dot(tri, hi) + _dot(tri, mid) + _dot(tri, lo) + carry_ref[...]
        carry_ref[...] = rc[0:1, :]
        dfl = rc * jax.nn.sigmoid(-fl_ref[...])
        dfl_ref[...] = dfl.astype(BF16)
        dbf_ref[...] += jnp.sum(dfl, axis=0, keepdims=True)

    rev = pl.BlockSpec((rb, LANE), lambda i: (nb - 1 - i, 0))
    return pl.pallas_call(
        body, name="fox_dlogit", grid=(nb,),
        in_specs=[rev, rev], out_specs=[rev, _whole((1, LANE))],
        out_shape=[jax.ShapeDtypeStruct((S, LANE), BF16), jax.ShapeDtypeStruct((1, LANE), F32)],
        scratch_shapes=[pltpu.VMEM((1, LANE), F32)],
        compiler_params=_params(),
    )(dc, fl)


def _attn_blocks(kind, S, Sk):
    tq = _pick(S, 512)
    tc = LANE if kind == "sb" else _pick(Sk, 256)
    return tq, tc


def _is_power_of_two(x):
    return math.frexp(x)[0] == 0.5


def _sb_logs(z):
    ln = -jnp.maximum(z, 0.0) - jnp.log(1.0 + jnp.exp(-jnp.abs(z)))
    return ln + z, ln


def _head_lanes(pack, dh):
    lane = lax.broadcasted_iota(jnp.int32, (1, LANE), 1)
    return [(lane >= hh * dh) & (lane < (hh + 1) * dh) for hh in range(pack)]


def _by_head(sel, parts):
    out = parts[0]
    for hh in range(1, len(parts)):
        out = jnp.where(sel[hh], parts[hh], out)
    return out


def _only_head(sel, hh, x):
    return x if len(sel) == 1 else jnp.where(sel[hh], x, jnp.zeros_like(x))


def _tail(x, r0):
    return x if not r0 else x[r0:]


def _put_tail(x, tail, r0):
    return tail if not r0 else jnp.concatenate([x[:r0], tail], axis=0)


def _add_tail(x, tail, r0):
    return x + tail if not r0 else jnp.concatenate([x[:r0], x[r0:] + tail], axis=0)


def _q_cols(tq, first):
    return pl.BlockSpec((tq, LANE), lambda g, i: (i, first // LANE + g))


def _k_cols(rows, first):
    return pl.BlockSpec((rows, LANE), lambda g, i: (0, first // LANE + g))


def _attn_fwd(kind, q, k, v, n_heads, dh, ccol=None, crow=None):
    (qa, q0), (ka, k0), (va, v0) = q, k, v
    S, Sk = qa.shape[0], ka.shape[0]
    pack = LANE // dh
    tq, tc = _attn_blocks(kind, S, Sk)
    scale = dh ** -0.5
    fold = _is_power_of_two(scale)
    causal = kind != "mem"
    n_diag = tq // tc if causal else 0
    unroll = 2 if causal else 1
    assert n_diag % unroll == 0 and (Sk // tc) % unroll == 0

    def body(*refs):
        if kind == "fox":
            q_ref, k_ref, v_ref, cc_ref, cr_ref, o_ref, lse_ref = refs
        else:
            q_ref, k_ref, v_ref, o_ref, lse_ref = refs
        i = pl.program_id(1)
        n_full = (i * tq) // tc if causal else Sk // tc
        qpos = i * tq + lax.broadcasted_iota(jnp.int32, (tq, tc), 0)
        kio = lax.broadcasted_iota(jnp.int32, (tq, tc), 1)
        heads = range(pack)
        sel = _head_lanes(pack, dh)
        q2 = q_ref[...] * scale if fold else q_ref[...]
        qs = [_only_head(sel, hh, q2) for hh in heads]

        def kv(jc):
            off = pl.multiple_of(jc * tc, tc)
            return off, k_ref[pl.ds(off, tc), :], v_ref[pl.ds(off, tc), :]

        if kind == "sb":
            tri = (lax.broadcasted_iota(jnp.int32, (tc, tc), 0) > lax.broadcasted_iota(jnp.int32, (tc, tc), 1)
                   ).astype(BF16)

            def chunk(jc, r0, runs, acc):
                off, k2, v2 = kv(jc)
                new_runs, pv = [], []
                for hh in heads:
                    lb, ln = _sb_logs(_dot_nt(_tail(qs[hh], r0), k2))
                    if r0 is not None:
                        mask = (off + _tail(kio, r0)) < _tail(qpos, r0)
                        ln = jnp.where(mask, ln, 0.0)
                    w = jnp.exp(lb + _cumdot(ln, tri) + _tail(runs[hh], r0))
                    if r0 is not None:
                        w = jnp.where(mask, w, 0.0)
                    pv.append(_dot(w.astype(BF16), v2))
                    new_runs.append(_add_tail(runs[hh], jnp.sum(ln, axis=1, keepdims=True), r0))
                return tuple(new_runs), _add_tail(acc, _by_head(sel, pv), r0)

            state = (tuple(jnp.zeros((tq, 1), F32) for _ in heads), jnp.zeros((tq, LANE), F32))
            for d in range(n_diag - 1, -1, -1):
                state = chunk(n_full + d, d * tc, *state)

            def trip(t, st):
                for u in range(unroll):
                    st = chunk(n_full - 1 - unroll * t - u, None, *st)
                return st

            runs, acc = lax.fori_loop(0, n_full // unroll, trip, state)
            o_ref[...] = acc.astype(o_ref.dtype)
            for hh in heads:
                lse_ref[hh] = runs[hh]
        else:
            def chunk(jc, r0, ms, ls, acc):
                off, k2, v2 = kv(jc)
                new_ms, new_ls, alphas, pv = [], [], [], []
                for hh in heads:
                    z = _dot_nt(_tail(qs[hh], r0), k2)
                    if not fold:
                        z = z * scale
                    if kind == "fox":
                        z = z + _tail(cc_ref[hh], r0) - cr_ref[hh, pl.ds(jc, 1), :]
                    if r0 is not None:
                        z = jnp.where((off + _tail(kio, r0)) <= _tail(qpos, r0), z, NEG)
                    m_old, l_old = _tail(ms[hh], r0), _tail(ls[hh], r0)
                    m_new = jnp.maximum(m_old, jnp.max(z, axis=1, keepdims=True))
                    alpha = jnp.exp(m_old - m_new)
                    p = jnp.exp(z - m_new)
                    new_ms.append(_put_tail(ms[hh], m_new, r0))
                    new_ls.append(_put_tail(ls[hh], alpha * l_old + jnp.sum(p, axis=1, keepdims=True), r0))
                    alphas.append(alpha)
                    pv.append(_dot(p.astype(BF16), v2))
                acc_new = _by_head(sel, alphas) * _tail(acc, r0) + _by_head(sel, pv)
                return tuple(new_ms), tuple(new_ls), _put_tail(acc, acc_new, r0)

            state = (tuple(jnp.full((tq, 1), NEG, F32) for _ in heads), tuple(jnp.zeros((tq, 1), F32) for _ in heads),
                     jnp.zeros((tq, LANE), F32))

            def trip(t, st):
                for u in range(unroll):
                    st = chunk(unroll * t + u, None, *st)
                return st

            state = lax.fori_loop(0, n_full // unroll, trip, state)
            for d in range(n_diag):
                state = chunk(n_full + d, d * tc, *state)
            ms, ls, acc = state
            o_ref[...] = (acc / _by_head(sel, ls)).astype(o_ref.dtype)
            for hh in heads:
                lse_ref[hh] = ms[hh] + jnp.log(ls[hh])

    colspec = pl.BlockSpec((pack, tq, 1), lambda g, i: (g, i, 0))
    in_specs, args = [_q_cols(tq, q0), _k_cols(Sk, k0), _k_cols(Sk, v0)], [qa, ka, va]
    if kind == "fox":
        in_specs += [colspec, pl.BlockSpec((pack, Sk // tc, tc), lambda g, i: (g, 0, 0))]
        args += [ccol, crow]
    return pl.pallas_call(
        body, name="attn_fwd_" + kind, grid=(n_heads // pack, S // tq),
        in_specs=in_specs, out_specs=[_q_cols(tq, 0), colspec],
        out_shape=[jax.ShapeDtypeStruct((S, n_heads * dh), BF16), jax.ShapeDtypeStruct((n_heads, S, 1), F32)],
        compiler_params=_params(),
    )(*args)


def _attn_bwd(kind, q, k, v, o, do, n_heads, dh, ccol=None, crow=None, lse=None):
    (qa, q0), (ka, k0), (va, v0) = q, k, v
    S, Sk = qa.shape[0], ka.shape[0]
    pack = LANE // dh
    tq, tc = _attn_blocks(kind, S, Sk)
    scale = dh ** -0.5
    fold = _is_power_of_two(scale)
    causal = kind != "mem"
    n_diag = tq // tc if causal else 0
    unroll = 2 if kind == "sb" else 1
    assert n_diag % unroll == 0 and (Sk // tc) % unroll == 0
    nq = S // tq

    def body(*refs):
        if kind == "fox":
            (q_ref, k_ref, v_ref, o_ref, do_ref, cc_ref, cr_ref, lse_ref,
             dq_ref, dk_ref, dv_ref, dc_ref, dcc_ref, dk_acc, dv_acc, dc_acc) = refs
        else:
            q_ref, k_ref, v_ref, o_ref, do_ref, lse_ref, dq_ref, dk_ref, dv_ref, dk_acc, dv_acc = refs
        i = pl.program_id(1)

        @pl.when(i == 0)
        def _():
            dk_acc[...] = jnp.zeros_like(dk_acc)
            dv_acc[...] = jnp.zeros_like(dv_acc)
            if kind == "fox":
                dc_acc[...] = jnp.zeros_like(dc_acc)

        n_full = (i * tq) // tc if causal else Sk // tc
        qpos = i * tq + lax.broadcasted_iota(jnp.int32, (tq, tc), 0)
        kio = lax.broadcasted_iota(jnp.int32, (tq, tc), 1)
        heads = range(pack)
        sel = _head_lanes(pack, dh)
        q2 = q_ref[...] * scale if fold else q_ref[...]
        do2 = do_ref[...]
        qs = [_only_head(sel, hh, q2) for hh in heads]
        dos = [_only_head(sel, hh, do2) for hh in heads]

        def kv(jc):
            off = pl.multiple_of(jc * tc, tc)
            return off, k_ref[pl.ds(off, tc), :], v_ref[pl.ds(off, tc), :]

        def accumulate(off, k2, dzb, wb, dq, r0):
            q2t, do2t = _tail(q2, r0), _tail(do2, r0)
            dk_acc[pl.ds(off, tc), :] += _by_head(sel, [_dot_tn(dzb[hh], q2t) for hh in heads])
            dv_acc[pl.ds(off, tc), :] += _by_head(sel, [_dot_tn(wb[hh], do2t) for hh in heads])
            return _add_tail(dq, _by_head(sel, [_dot(dzb[hh], k2) for hh in heads]), r0)

        if kind == "sb":
            r = lax.broadcasted_iota(jnp.int32, (tc, tc), 0)
            cidx = lax.broadcasted_iota(jnp.int32, (tc, tc), 1)
            tri_inc = (r <= cidx).astype(BF16)
            tri_exc = (r < cidx).astype(BF16)

            def chunk(jc, r0, pres, pres_e, dq):
                off, k2, v2 = kv(jc)
                new_pres, new_pres_e, dzb, wb = [], [], [], []
                for hh in heads:
                    lb, ln = _sb_logs(_dot_nt(_tail(qs[hh], r0), k2))
                    if r0 is not None:
                        mask = (off + _tail(kio, r0)) < _tail(qpos, r0)
                        ln = jnp.where(mask, ln, 0.0)
                    w = jnp.exp(lb + (_tail(lse_ref[hh], r0) - _tail(pres[hh], r0) - _cumdot(ln, tri_inc)))
                    if r0 is not None:
                        w = jnp.where(mask, w, 0.0)
                    e = w * _dot_nt(_tail(dos[hh], r0), v2)
                    beta = jnp.exp(lb)
                    dz = e * (1.0 - beta) - beta * (_tail(pres_e[hh], r0) + _cumdot(e, tri_exc))
                    if r0 is not None:
                        dz = jnp.where(mask, dz, 0.0)
                    dzb.append(dz.astype(BF16))
                    wb.append(w.astype(BF16))
                    new_pres.append(_add_tail(pres[hh], jnp.sum(ln, axis=1, keepdims=True), r0))
                    new_pres_e.append(_add_tail(pres_e[hh], jnp.sum(e, axis=1, keepdims=True), r0))
                return tuple(new_pres), tuple(new_pres_e), accumulate(off, k2, dzb, wb, dq, r0)

            state = (tuple(jnp.zeros((tq, 1), F32) for _ in heads), tuple(jnp.zeros((tq, 1), F32) for _ in heads),
                     jnp.zeros((tq, LANE), F32))
        else:
            prod = o_ref[...].astype(F32) * do2.astype(F32)
            dsum = [jnp.sum(_only_head(sel, hh, prod), axis=1, keepdims=True) for hh in heads]

            def chunk(jc, r0, rowsums, dq):
                off, k2, v2 = kv(jc)
                new_rowsums, dsb, pb = [], [], []
                for hh in heads:
                    z = _dot_nt(_tail(qs[hh], r0), k2)
                    if not fold:
                        z = z * scale
                    if kind == "fox":
                        z = z + _tail(cc_ref[hh], r0) - cr_ref[hh, pl.ds(jc, 1), :]
                    if r0 is not None:
                        z = jnp.where((off + _tail(kio, r0)) <= _tail(qpos, r0), z, NEG)
                    p = jnp.exp(z - _tail(lse_ref[hh], r0))
                    ds = p * (_dot_nt(_tail(dos[hh], r0), v2) - _tail(dsum[hh], r0))
                    dsb.append(ds.astype(BF16))
                    pb.append(p.astype(BF16))
                    if kind == "fox":
                        dc_acc[hh, pl.ds(jc, 1), :] -= jnp.sum(ds, axis=0, keepdims=True)
                        new_rowsums.append(_add_tail(rowsums[hh], jnp.sum(ds, axis=1, keepdims=True), r0))
                    else:
                        new_rowsums.append(rowsums[hh])
                return tuple(new_rowsums), accumulate(off, k2, dsb, pb, dq, r0)

            state = (tuple(jnp.zeros((tq, 1), F32) for _ in heads), jnp.zeros((tq, LANE), F32))

        def trip(t, st):
            for u in range(unroll):
                st = chunk(unroll * t + u, None, *st)
            return st

        state = lax.fori_loop(0, n_full // unroll, trip, state)
        for d in range(n_diag):
            state = chunk(n_full + d, d * tc, *state)
        dq_ref[...] = (state[-1] * scale).astype(dq_ref.dtype)
        if kind == "fox":
            for hh in heads:
                dcc_ref[hh] = state[0][hh]

        @pl.when(i == nq - 1)
        def _():
            dk = dk_acc[...] if fold else dk_acc[...] * scale
            dk_ref[...] = dk.astype(dk_ref.dtype)
            dv_ref[...] = dv_acc[...].astype(dv_ref.dtype)
            if kind == "fox":
                dc_ref[...] = dc_acc[...]

    colspec = pl.BlockSpec((pack, tq, 1), lambda g, i: (g, i, 0))
    rowspec = pl.BlockSpec((pack, Sk // tc, tc), lambda g, i: (g, 0, 0))
    in_specs = [_q_cols(tq, q0), _k_cols(Sk, k0), _k_cols(Sk, v0), _q_cols(tq, 0), _q_cols(tq, 0)]
    args = [qa, ka, va, o, do]
    if kind == "fox":
        in_specs += [colspec, rowspec]
        args += [ccol, crow]
    in_specs += [colspec]
    args += [lse]
    width = n_heads * dh
    out_specs = [_q_cols(tq, 0), _k_cols(Sk, 0), _k_cols(Sk, 0)]
    out_shape = [jax.ShapeDtypeStruct((S, width), BF16), jax.ShapeDtypeStruct((Sk, width), BF16),
                 jax.ShapeDtypeStruct((Sk, width), BF16)]
    scratch = [pltpu.VMEM((Sk, LANE), F32), pltpu.VMEM((Sk, LANE), F32)]
    if kind == "fox":
        out_specs += [rowspec, colspec]
        out_shape += [jax.ShapeDtypeStruct((n_heads, Sk // tc, tc), F32), jax.ShapeDtypeStruct((n_heads, S, 1), F32)]
        scratch.append(pltpu.VMEM((pack, Sk // tc, tc), F32))
    return pl.pallas_call(
        body, name="attn_bwd_" + kind, grid=(n_heads // pack, nq),
        in_specs=in_specs, out_specs=out_specs, out_shape=out_shape, scratch_shapes=scratch,
        compiler_params=_params(),
    )(*args)


def _mem_norm(mem, g):
    M, D = mem.shape

    def body(mem_ref, g_ref, out_ref):
        out_ref[...] = _rms(mem_ref[...], g_ref[...]).astype(BF16)

    return pl.pallas_call(
        body, name="mem_norm", grid=(1,),
        in_specs=[_whole((M, D)), _whole((1, D))], out_specs=_whole((M, D)),
        out_shape=jax.ShapeDtypeStruct((M, D), BF16), compiler_params=_params(),
    )(mem, g)


def _mem_norm_bwd(mem, g, dmem_n):
    M, D = mem.shape
    L = dmem_n.shape[0]

    def body(mem_ref, g_ref, d_ref, dg_ref):
        d = d_ref[0]
        for l in range(1, L):
            d = d + d_ref[l]
        _, vjp = jax.vjp(_rms, mem_ref[...], g_ref[...])
        dg_ref[...] = vjp(d)[1]

    return pl.pallas_call(
        body, name="mem_norm_bwd", grid=(1,),
        in_specs=[_whole((M, D)), _whole((1, D)), _whole((L, M, D))], out_specs=_whole((1, D)),
        out_shape=jax.ShapeDtypeStruct((1, D), F32), compiler_params=_params(),
    )(mem, g, dmem_n)


def _loss_head(h, target):
    S, D = h.shape
    bm = _pick(S, 512)

    def body(h_ref, t_ref, dh_ref, loss_ref):
        err = h_ref[...] - t_ref[...]
        dh_ref[...] = err * (1.0 / D)

        @pl.when(pl.program_id(0) == 0)
        def _():
            loss_ref[...] = jnp.zeros_like(loss_ref)

        loss_ref[...] += 0.5 * jnp.sum(jnp.mean(err * err, axis=-1, keepdims=True), axis=0, keepdims=True)

    return pl.pallas_call(
        body, name="loss_head", grid=(S // bm,),
        in_specs=[_rows(bm, D), _rows(bm, D)], out_specs=[_rows(bm, D), _whole((8, LANE))],
        out_shape=[jax.ShapeDtypeStruct((S, D), F32), jax.ShapeDtypeStruct((8, LANE), F32)],
        compiler_params=_params(),
    )(h, target)


def _adamw(w, g, m, v, name):
    R, C = w.shape
    rb = R if R * C * 4 <= (1 << 20) else _pick(R, 256)
    if R % rb:
        rb = R
    c1 = 1.0 - ADAM_B1 ** ADAM_STEP
    c2 = 1.0 - ADAM_B2 ** ADAM_STEP

    def body(w_ref, g_ref, m_ref, v_ref, d_ref, mo_ref, vo_ref):
        gv = g_ref[...]
        mn = ADAM_B1 * m_ref[...] + (1.0 - ADAM_B1) * gv
        vn = ADAM_B2 * v_ref[...] + (1.0 - ADAM_B2) * (gv * gv)
        mo_ref[...] = mn
        vo_ref[...] = vn
        d_ref[...] = -ADAM_LR * ((mn / c1) / (jnp.sqrt(vn / c2) + ADAM_EPS) + ADAM_WD * w_ref[...])

    return pl.pallas_call(
        body, name=name, grid=(R // rb,),
        in_specs=[_rows(rb, C)] * 4, out_specs=[_rows(rb, C)] * 3,
        out_shape=[jax.ShapeDtypeStruct((R, C), F32)] * 3, compiler_params=_params(),
    )(w, g, m, v)


ANY = pl.BlockSpec(memory_space=pl.ANY)
MESH = pl.DeviceIdType.MESH


def _place():
    x, y, c = lax.axis_index("x"), lax.axis_index("y"), lax.axis_index("c")
    others = [(1 - x, y), (x, 1 - y), (1 - x, 1 - y)]
    return x, y, c, others


def _place_own(loc, chip_idx):
    _, R, C = loc.shape
    rb = _pick(R, 2 * FLAT_ROW_BLOCK)

    def body(chip_ref, loc_ref, out_ref):
        out_ref[...] = loc_ref[...]

    return pl.pallas_call(
        body, name="place_own",
        grid_spec=pltpu.PrefetchScalarGridSpec(
            num_scalar_prefetch=1, grid=(2, R // rb),
            in_specs=[pl.BlockSpec((None, rb, C), lambda hf, i, chip_ref: (hf, i, 0))],
            out_specs=pl.BlockSpec((None, None, rb, C), lambda hf, i, chip_ref: (chip_ref[0], hf, i, 0))),
        out_shape=jax.ShapeDtypeStruct((N_CHIPS, 2, R, C), loc.dtype), compiler_params=_params(),
    )(chip_idx, loc)


def _gather_weights(loc, own):
    _, R, C = loc.shape

    def body(loc_ref, own_ref, out_ref, send_sems, recv_sems):
        del own_ref
        x, y, c, others = _place()
        me = 2 * x + y
        sibling = (x, y, 1 - c)

        def copy(k, src, dst, to):
            return pltpu.make_async_remote_copy(src_ref=src, dst_ref=dst, send_sem=send_sems.at[k],
                                                recv_sem=recv_sems.at[k], device_id=to, device_id_type=MESH)

        first = [copy(j, loc_ref.at[c], out_ref.at[me, c], (ox, oy, c)) for j, (ox, oy) in enumerate(others)]
        for cp in first:
            cp.start()
        passed = []
        for j, (ox, oy) in enumerate(others):
            landed = out_ref.at[2 * ox + oy, c]
            copy(j, loc_ref.at[c], landed, sibling).wait_recv()
            cp = copy(3 + j, landed, landed, sibling)
            cp.start()
            passed.append(cp)
        for j, (ox, oy) in enumerate(others):
            copy(3 + j, loc_ref.at[c], out_ref.at[2 * ox + oy, 1 - c], sibling).wait_recv()
        for cp in first + passed:
            cp.wait_send()

    return pl.pallas_call(
        body, name="gather_weights", in_specs=[ANY, ANY], out_specs=ANY,
        out_shape=jax.ShapeDtypeStruct((N_CHIPS, 2, R, C), loc.dtype), input_output_aliases={1: 0},
        scratch_shapes=[pltpu.SemaphoreType.DMA((6,)), pltpu.SemaphoreType.DMA((6,))],
    )(loc, own)


def _pair_exchange(g):
    _, _, R, C = g.shape

    def body(g_ref, out_ref, send_sem, recv_sem):
        x, y, c, _ = _place()
        cp = pltpu.make_async_remote_copy(src_ref=g_ref.at[1 - c], dst_ref=out_ref, send_sem=send_sem,
                                          recv_sem=recv_sem, device_id=(x, y, 1 - c), device_id_type=MESH)
        cp.start()
        cp.wait()

    return pl.pallas_call(
        body, name="pair_exchange", in_specs=[ANY], out_specs=ANY,
        out_shape=jax.ShapeDtypeStruct((N_CHIPS, R, C), g.dtype),
        scratch_shapes=[pltpu.SemaphoreType.DMA, pltpu.SemaphoreType.DMA],
    )(g)


def _pair_sum(g, sib, c_idx):
    _, _, R, C = g.shape
    rb = _pick(R, 512)

    def body(c_ref, g_ref, s_ref, o_ref):
        o_ref[...] = (g_ref[...].astype(F32) + s_ref[...].astype(F32)).astype(o_ref.dtype)

    return pl.pallas_call(
        body, name="pair_sum",
        grid_spec=pltpu.PrefetchScalarGridSpec(
            num_scalar_prefetch=1, grid=(N_CHIPS, R // rb),
            in_specs=[pl.BlockSpec((None, None, rb, C), lambda j, i, c_ref: (c_ref[0], j, i, 0)),
                      pl.BlockSpec((None, rb, C), lambda j, i, c_ref: (j, i, 0))],
            out_specs=pl.BlockSpec((None, rb, C), lambda j, i, c_ref: (j, i, 0))),
        out_shape=jax.ShapeDtypeStruct((N_CHIPS, R, C), g.dtype), compiler_params=_params(),
    )(c_idx, g, sib)


def _chip_exchange(p):
    _, R, C = p.shape

    def body(p_ref, out_ref, send_sems, recv_sems):
        x, y, c, others = _place()
        copies = []
        for j, (ox, oy) in enumerate(others):
            cp = pltpu.make_async_remote_copy(src_ref=p_ref.at[2 * ox + oy], dst_ref=out_ref.at[j],
                                              send_sem=send_sems.at[j], recv_sem=recv_sems.at[j],
                                              device_id=(ox, oy, c), device_id_type=MESH)
            cp.start()
            copies.append(cp)
        for cp in copies:
            cp.wait()

    return pl.pallas_call(
        body, name="chip_exchange", in_specs=[ANY], out_specs=ANY,
        out_shape=jax.ShapeDtypeStruct((N_CHIPS - 1, R, C), p.dtype),
        scratch_shapes=[pltpu.SemaphoreType.DMA((3,)), pltpu.SemaphoreType.DMA((3,))],
    )(p)


def _chip_sum(p, r, chip_idx):
    _, R, C = r.shape
    rb = _pick(R, 512)

    def body(chip_ref, p_ref, r_ref, o_ref):
        acc = p_ref[...].astype(F32)
        for j in range(N_CHIPS - 1):
            acc = acc + r_ref[j].astype(F32)
        o_ref[...] = acc

    return pl.pallas_call(
        body, name="chip_sum",
        grid_spec=pltpu.PrefetchScalarGridSpec(
            num_scalar_prefetch=1, grid=(R // rb,),
            in_specs=[pl.BlockSpec((None, rb, C), lambda i, chip_ref: (chip_ref[0], i, 0)),
                      pl.BlockSpec((N_CHIPS - 1, rb, C), lambda i, chip_ref: (0, i, 0))],
            out_specs=pl.BlockSpec((rb, C), lambda i, chip_ref: (i, 0))),
        out_shape=jax.ShapeDtypeStruct((R, C), F32), compiler_params=_params(),
    )(chip_idx, p, r)


def _pair_swap(rh):
    R, C = rh.shape

    def body(rh_ref, out_ref, send_sem, recv_sem):
        x, y, c, _ = _place()
        cp = pltpu.make_async_remote_copy(src_ref=rh_ref, dst_ref=out_ref, send_sem=send_sem,
                                          recv_sem=recv_sem, device_id=(x, y, 1 - c), device_id_type=MESH)
        cp.start()
        cp.wait()

    return pl.pallas_call(
        body, name="pair_swap", in_specs=[ANY], out_specs=ANY,
        out_shape=jax.ShapeDtypeStruct((R, C), rh.dtype),
        scratch_shapes=[pltpu.SemaphoreType.DMA, pltpu.SemaphoreType.DMA],
    )(rh)


def _all_reduce_small(s):
    R, C = s.shape

    def body(s_ref, o_ref, buf, send_sems, recv_sems):
        x, y, c, _ = _place()
        me = 4 * x + 2 * y + c
        sends = []
        for k in range(1, N_DEV):
            fx, fy, fc = (k >> 2) & 1, (k >> 1) & 1, k & 1
            to = (x ^ fx, y ^ fy, c ^ fc)
            cp = pltpu.make_async_remote_copy(src_ref=s_ref, dst_ref=buf.at[me], send_sem=send_sems.at[k - 1],
                                              recv_sem=recv_sems.at[k - 1], device_id=to, device_id_type=MESH)
            cp.start()
            sends.append(cp)
        buf[me] = s_ref[...]
        for k in range(1, N_DEV):
            fx, fy, fc = (k >> 2) & 1, (k >> 1) & 1, k & 1
            frm = 4 * (x ^ fx) + 2 * (y ^ fy) + (c ^ fc)
            pltpu.make_async_remote_copy(src_ref=s_ref, dst_ref=buf.at[frm], send_sem=send_sems.at[k - 1],
                                         recv_sem=recv_sems.at[k - 1], device_id=(x, y, c),
                                         device_id_type=MESH).wait_recv()
        acc = buf[0]
        for d in range(1, N_DEV):
            acc = acc + buf[d]
        o_ref[...] = acc
        for cp in sends:
            cp.wait_send()

    vm = pl.BlockSpec(memory_space=pltpu.VMEM)
    return pl.pallas_call(
        body, name="all_reduce_small", in_specs=[vm], out_specs=vm,
        out_shape=jax.ShapeDtypeStruct((R, C), F32),
        scratch_shapes=[pltpu.VMEM((N_DEV, R, C), F32), pltpu.SemaphoreType.DMA((N_DEV - 1,)),
                        pltpu.SemaphoreType.DMA((N_DEV - 1,))],
    )(s)


def _padded(n):
    return -(-n // FLAT_UNIT) * FLAT_UNIT


def _pack_flat(pieces, dtype, row_block=FLAT_ROW_BLOCK):
    flat = []
    for p in pieces:
        p = p.reshape(-1).astype(dtype)
        flat.append(jnp.pad(p, (0, _padded(p.size) - p.size)))
    total = sum(p.size for p in flat)
    flat.append(jnp.zeros((-total) % (row_block * FLAT_COLS), dtype))
    return jnp.concatenate(flat).reshape(-1, FLAT_COLS)


def _unpack_flat(flat, shapes):
    lead = flat.shape[:-2]
    flat = flat.reshape(lead + (-1,))
    out, off = [], 0
    for shp in shapes:
        n = math.prod(shp)
        out.append(flat[..., off:off + n].reshape(lead + tuple(shp)))
        off += _padded(n)
    return out


def _flat_offsets(shapes):
    offs, off = [], 0
    for shp in shapes:
        offs.append(off)
        off += _padded(math.prod(shp))
    return offs


def _slab(t, axis, j):
    n = t.shape[axis - 1] // N_CHIPS
    return lax.slice_in_dim(t, j * n, (j + 1) * n, axis=axis - 1)


def _layer_fwd(h0, mem_n, wl, dims):
    n_sb, n_fx, n_mem, sbw, fxw, memw = dims
    n1, gate1, up1, a1 = _ffn_fwd_up(h0, wl["ffn1_pre_g"], wl["ffn1_w_gate"], wl["ffn1_w_up"])
    h1, f1 = _ffn_fwd_down(a1, wl["ffn1_w_down"], h0, wl["ffn1_post_g"])

    u, proj, fl, sg = _mix_fwd_in(h1, wl["mix_pre_g"], wl["w_in"], wl["w_gate"], wl["b_gate"], wl["b_forget"])
    c = _fox_cumsum(fl)
    S = h0.shape[0]
    tc = _attn_blocks("fox", S, S)[1]
    ct = c[:, :n_fx].T
    ccol, crow = ct.reshape(n_fx, S, 1), ct.reshape(n_fx, S // tc, tc)
    qkv_sb = [(proj, k * sbw) for k in range(3)]
    qkv_fx = [(proj, 3 * sbw + k * fxw) for k in range(3)]
    kv = _matmul(mem_n, wl["w_mem_kv"], out_dtype=BF16, name="mem_kv")
    qkv_mem = [(proj, 3 * sbw + 3 * fxw), (kv, 0), (kv, memw)]
    o_sb, tot_sb = _attn_fwd("sb", *qkv_sb, n_sb, HEAD_DIM)
    o_fx, lse_fx = _attn_fwd("fox", *qkv_fx, n_fx, HEAD_DIM, ccol, crow)
    o_mem, lse_mem = _attn_fwd("mem", *qkv_mem, n_mem, MEM_HEAD_DIM)
    h2, zmix, merged = _mix_fwd_out(o_sb, o_fx, o_mem, sg, wl["w_br_sb"], wl["w_br_fox"], wl["w_br_mem"],
                                    wl["w_out"], h1, wl["mix_post_g"])

    n2, gate2, up2, a2 = _ffn_fwd_up(h2, wl["ffn2_pre_g"], wl["ffn2_w_gate"], wl["ffn2_w_up"])
    h3, f2 = _ffn_fwd_down(a2, wl["ffn2_w_down"], h2, wl["ffn2_post_g"])
    saved = dict(h0=h0, n1=n1, gate1=gate1, up1=up1, a1=a1, f1=f1, h1=h1, u=u, fl=fl, sg=sg,
                 qkv_sb=qkv_sb, qkv_fx=qkv_fx, qkv_mem=qkv_mem, ccol=ccol, crow=crow, o_sb=o_sb, o_fx=o_fx, o_mem=o_mem,
                 tot_sb=tot_sb, lse_fx=lse_fx, lse_mem=lse_mem,
                 zmix=zmix, merged=merged, h2=h2, n2=n2, gate2=gate2, up2=up2, a2=a2, f2=f2)
    return h3, saved


def _ffn_bwd(dh, sv, wl, tag, h_in):
    n, gate, up, a, f = (sv[k + tag] for k in ("n", "gate", "up", "a", "f"))
    pre = "ffn" + tag
    df, dgate, dup, dg_post = _ffn_bwd_down(dh, f, wl[pre + "_post_g"], wl[pre + "_w_down"], gate, up)
    dh_in, dg_pre = _ffn_bwd_up(dgate, dup, wl[pre + "_w_gate"], wl[pre + "_w_up"], h_in, wl[pre + "_pre_g"], dh)
    grads = {pre + "_post_g": dg_post, pre + "_pre_g": dg_pre,
             pre + "_w_down": _matmul(a, df, ta=True, name="dw_down"),
             pre + "_w_gate": _matmul(n, dgate, ta=True, name="dw_gate"),
             pre + "_w_up": _matmul(n, dup, ta=True, name="dw_up")}
    return dh_in, grads


def _layer_bwd(dh3, mem_n, wl, sv, dims):
    n_sb, n_fx, n_mem, sbw, fxw, memw = dims
    S = dh3.shape[0]
    dh2, grads = _ffn_bwd(dh3, sv, wl, "2", sv["h2"])

    (dz, db_sb, db_fx, db_mem, do_sb, do_fx, do_mem, dgp, db_gate, dg_post) = _mix_bwd_out(
        dh2, sv["zmix"], wl["mix_post_g"], wl["w_out"], sv["o_sb"], sv["o_fx"], sv["o_mem"],
        wl["w_br_sb"], wl["w_br_fox"], wl["w_br_mem"], sv["sg"])
    grads["mix_post_g"] = dg_post
    grads["b_gate"] = db_gate
    grads["w_out"] = _matmul(sv["merged"], dz, ta=True, name="dw_out")
    grads["w_br_sb"] = _matmul(sv["o_sb"], db_sb, ta=True, name="dw_br_sb")
    grads["w_br_fox"] = _matmul(sv["o_fx"], db_fx, ta=True, name="dw_br_fox")
    grads["w_br_mem"] = _matmul(sv["o_mem"], db_mem, ta=True, name="dw_br_mem")

    dq_sb, dk_sb, dv_sb = _attn_bwd("sb", *sv["qkv_sb"], sv["o_sb"], do_sb, n_sb, HEAD_DIM, lse=sv["tot_sb"])
    dq_fx, dk_fx, dv_fx, dcrow, dccol = _attn_bwd("fox", *sv["qkv_fx"], sv["o_fx"], do_fx, n_fx, HEAD_DIM,
                                                  sv["ccol"], sv["crow"], sv["lse_fx"])
    dq_mem, dk_mem, dv_mem = _attn_bwd("mem", *sv["qkv_mem"], sv["o_mem"], do_mem, n_mem, MEM_HEAD_DIM,
                                       lse=sv["lse_mem"])
    dkv = jnp.concatenate([dk_mem, dv_mem], axis=1)
    grads["w_mem_kv"] = _matmul(mem_n, dkv, ta=True, name="dw_mem_kv")
    dmem_n = _matmul(dkv, wl["w_mem_kv"], tb=True, out_dtype=F32, name="dmem_n")

    dc = jnp.pad((dcrow.reshape(n_fx, S) + dccol.reshape(n_fx, S)).T, ((0, 0), (0, LANE - n_fx)))
    dfl, db_forget = _fox_dlogit(dc, sv["fl"])
    grads["b_forget"] = db_forget
    dproj = jnp.concatenate([dq_sb, dk_sb, dv_sb, dq_fx, dk_fx, dv_fx, dq_mem, dfl], axis=1)
    dh1, dg_pre = _mix_bwd_in(dproj, dgp, wl["w_in"], wl["w_gate"], sv["h1"], wl["mix_pre_g"], dh2)
    grads["mix_pre_g"] = dg_pre
    grads["w_in"] = _matmul(sv["u"], dproj, ta=True, name="dw_in")
    grads["w_gate"] = _matmul(sv["u"], dgp, ta=True, name="dw_gate_mix")

    dh0, g1 = _ffn_bwd(dh1, sv, wl, "1", sv["h0"])
    grads.update(g1)
    return dh0, grads, dmem_n


def kernel(x, mem, ffn1_pre_g, ffn1_post_g, ffn1_w_gate, ffn1_w_up, ffn1_w_down, mix_pre_g, mix_post_g, w_in, b_forget, mem_norm_g, w_mem_kv, w_gate, b_gate, w_br_sb, w_br_fox, w_br_mem, w_out, ffn2_pre_g, ffn2_post_g, ffn2_w_gate, ffn2_w_up, ffn2_w_down, loss_target, m_ffn1_pre_g, m_ffn1_post_g, m_ffn1_w_gate, m_ffn1_w_up, m_ffn1_w_down, m_mix_pre_g, m_mix_post_g, m_w_in, m_b_forget, m_mem_norm_g, m_w_mem_kv, m_w_gate, m_b_gate, m_w_br_sb, m_w_br_fox, m_w_br_mem, m_w_out, m_ffn2_pre_g, m_ffn2_post_g, m_ffn2_w_gate, m_ffn2_w_up, m_ffn2_w_down, v_ffn1_pre_g, v_ffn1_post_g, v_ffn1_w_gate, v_ffn1_w_up, v_ffn1_w_down, v_mix_pre_g, v_mix_post_g, v_w_in, v_b_forget, v_mem_norm_g, v_w_mem_kv, v_w_gate, v_b_gate, v_w_br_sb, v_w_br_fox, v_w_br_mem, v_w_out, v_ffn2_pre_g, v_ffn2_post_g, v_ffn2_w_gate, v_ffn2_w_up, v_ffn2_w_down):
    args = dict(locals())
    w = {n: args[n] for n in WEIGHTS}
    m = {n: args["m_" + n] for n in WEIGHTS}
    v = {n: args["v_" + n] for n in WEIGHTS}
    L = w["ffn1_pre_g"].shape[0]
    Lh = L // 2
    D = x.shape[2]
    sbw, fxw, memw = w["w_br_sb"].shape[1], w["w_br_fox"].shape[1], w["w_br_mem"].shape[1]
    n_sb, n_fx, n_mem = sbw // HEAD_DIM, fxw // HEAD_DIM, memw // MEM_HEAD_DIM
    dims = (n_sb, n_fx, n_mem, sbw, fxw, memw)
    qkv_w = 3 * sbw + 3 * fxw
    c_idx = lax.axis_index("c")
    c_arr = c_idx.reshape(1).astype(jnp.int32)
    chip_arr = (2 * lax.axis_index("x") + lax.axis_index("y")).reshape(1).astype(jnp.int32)

    piece_shapes = [w[n].shape[1:] for n, _ in BIG for _ in range(Lh)]
    row_offs = [off // FLAT_COLS for off in _flat_offsets(piece_shapes)]
    loc = jnp.stack([_pack_flat([w[n][hf * Lh + li] for n, _ in BIG for li in range(Lh)], BF16) for hf in range(2)])
    gathered = _gather_weights(loc, _place_own(loc, chip_arr))

    def piece(buf, k, li):
        a, b = piece_shapes[k * Lh + li]
        r0 = row_offs[k * Lh + li]
        rows = buf[..., r0:r0 + _padded(a * b) // FLAT_COLS, :]
        return rows.reshape(buf.shape[:-2] + (-1,))[..., :a * b].reshape(buf.shape[:-2] + (a, b))

    def layer_weights(l):
        hf, li = divmod(l, Lh)
        wl = {}
        for k, (n, axis) in enumerate(BIG):
            shards = piece(gathered[:, hf], k, li)
            a, b = shards.shape[1:]
            wl[n] = shards.transpose(1, 0, 2).reshape(a, N_CHIPS * b) if axis == 2 else shards.reshape(N_CHIPS * a, b)
        wi = wl["w_in"]
        wl["w_in"] = jnp.concatenate([wi[:, :qkv_w], wi[:, qkv_w + n_fx:], wi[:, qkv_w:qkv_w + n_fx],
                                      jnp.zeros((D, LANE - n_fx), BF16)], axis=1)
        for n in SMALL:
            if n != "mem_norm_g":
                wl[n] = w[n][l][None, :]
        wl["b_forget"] = jnp.pad(wl["b_forget"], ((0, 0), (0, LANE - n_fx)))
        return wl

    g_mem = w["mem_norm_g"][None, :]

    mem_n = _mem_norm(mem[0], g_mem)
    h, wls, saved = x[0], [], []
    for l in range(L):
        wls.append(layer_weights(l))
        h, sv = _layer_fwd(h, mem_n, wls[l], dims)
        saved.append(sv)
    dh, loss_tile = _loss_head(h, loss_target[0])
    loss = lax.psum(loss_tile[0, 0], ("x", "y", "c"))
    gl, dmem_n = [None] * L, [None] * L
    for l in reversed(range(L)):
        dh, gl[l], dmem_n[l] = _layer_bwd(dh, mem_n, wls[l], saved[l], dims)
        gi = gl[l]["w_in"]
        gl[l]["w_in"] = jnp.concatenate([gi[:, :qkv_w], gi[:, qkv_w + memw:qkv_w + memw + n_fx],
                                         gi[:, qkv_w:qkv_w + memw]], axis=1)
    grad_x = dh
    g_mem_norm = _mem_norm_bwd(mem[0], g_mem, jnp.stack(dmem_n))

    partial = jnp.stack([
        jnp.stack([_pack_flat([_slab(gl[hf * Lh + li][n], axis, j) for n, axis in BIG for li in range(Lh)], BF16)
                   for j in range(N_CHIPS)]) for hf in range(2)])
    sib = _pair_exchange(partial)
    pair = _pair_sum(partial, sib, c_arr)
    mine = _chip_sum(pair, _chip_exchange(pair), chip_arr)
    theirs = _pair_swap(mine)
    grad = {n: jnp.stack([jnp.where(c_idx == l // Lh, piece(mine, k, l % Lh), piece(theirs, k, l % Lh))
                          for l in range(L)]) for k, (n, _) in enumerate(BIG)}

    small_local = {n: (g_mem_norm if n == "mem_norm_g" else
                       jnp.concatenate([gl[l][n][:, :n_fx] if n == "b_forget" else gl[l][n] for l in range(L)]))
                   for n in SMALL}
    small_shapes = [small_local[n].shape for n in SMALL]
    small_sum = _unpack_flat(_all_reduce_small(_pack_flat([small_local[n] for n in SMALL], F32, row_block=16)),
                             small_shapes)
    for n, t in zip(SMALL, small_sum):
        grad[n] = t.reshape(w[n].shape)

    delta, new_m, new_v = {}, {}, {}
    for n in WEIGHTS:
        shp = w[n].shape
        two_d = (1, shp[0]) if len(shp) == 1 else (-1, shp[-1])
        d_, m_, v_ = _adamw(w[n].reshape(two_d), grad[n].reshape(two_d), m[n].reshape(two_d), v[n].reshape(two_d),
                            name="adamw_" + n)
        delta[n], new_m[n], new_v[n] = d_.reshape(shp), m_.reshape(shp), v_.reshape(shp)

    return (loss, grad_x[None], *[grad[n] for n in WEIGHTS], *[delta[n] for n in WEIGHTS],
            *[new_m[n] for n in WEIGHTS], *[new_v[n] for n in WEIGHTS])
```

```python
import math

import jax
import jax.numpy as jnp
from jax import lax
from jax.experimental import pallas as pl
from jax.experimental.pallas import tpu as pltpu

F32 = jnp.float32
BF16 = jnp.bfloat16
RMS_EPS = 1e-6
HEAD_DIM = 64
MEM_HEAD_DIM = 128
LANE = 128
V7X_VMEM_LIMIT_BYTES = 56 * 1024 * 1024
FLAT_COLS = 512
FLAT_UNIT = 16 * FLAT_COLS
FLAT_ROW_BLOCK = 512
N_CHIPS = 4
N_DEV = 8
NEG = float(jnp.finfo(jnp.float32).min)

ADAM_LR = 0.001
ADAM_B1 = 0.9
ADAM_B2 = 0.999
ADAM_EPS = 1e-08
ADAM_WD = 0.01
ADAM_STEP = 10

BIG = (("ffn1_w_gate", 2), ("ffn1_w_up", 2), ("ffn1_w_down", 1), ("w_in", 2), ("w_mem_kv", 1), ("w_gate", 2),
       ("w_br_sb", 2), ("w_br_fox", 2), ("w_br_mem", 2), ("w_out", 1),
       ("ffn2_w_gate", 2), ("ffn2_w_up", 2), ("ffn2_w_down", 1))
SMALL = ("ffn1_pre_g", "ffn1_post_g", "mix_pre_g", "mix_post_g", "b_forget", "mem_norm_g", "b_gate",
         "ffn2_pre_g", "ffn2_post_g")
WEIGHTS = ("ffn1_pre_g", "ffn1_post_g", "ffn1_w_gate", "ffn1_w_up", "ffn1_w_down", "mix_pre_g", "mix_post_g", "w_in",
           "b_forget", "mem_norm_g", "w_mem_kv", "w_gate", "b_gate", "w_br_sb", "w_br_fox", "w_br_mem", "w_out",
           "ffn2_pre_g", "ffn2_post_g", "ffn2_w_gate", "ffn2_w_up", "ffn2_w_down")


def _params(**kw):
    return pltpu.CompilerParams(vmem_limit_bytes=V7X_VMEM_LIMIT_BYTES, **kw)


def _dot(a, b):
    return jnp.dot(a, b, preferred_element_type=F32)


def _dot_nt(a, b):
    return lax.dot_general(a, b, (((1,), (1,)), ((), ())), preferred_element_type=F32)


def _dot_tn(a, b):
    return lax.dot_general(a, b, (((0,), (0,)), ((), ())), preferred_element_type=F32)


def _rms(t, g):
    return t * lax.rsqrt(jnp.mean(t * t, axis=-1, keepdims=True) + RMS_EPS) * g


def _pick(dim, pref):
    if dim <= pref:
        return dim
    for cand in range(pref - pref % LANE, 0, -LANE):
        if dim % cand == 0:
            return cand
    return dim


def _rows(bm, cols):
    return pl.BlockSpec((bm, cols), lambda i: (i, 0))


def _whole(shape):
    nd = len(shape)
    return pl.BlockSpec(shape, lambda i: (0,) * nd)


def _split3(x):
    hi = x.astype(BF16)
    r1 = x - hi.astype(F32)
    mid = r1.astype(BF16)
    lo = (r1 - mid.astype(F32)).astype(BF16)
    return hi, mid, lo


def _cumdot(x, tri):
    hi = x.astype(BF16)
    lo = (x - hi.astype(F32)).astype(BF16)
    return _dot(hi, tri) + _dot(lo, tri)


def _ffn_fwd_up(h, g_pre, wg, wu):
    S, D = h.shape
    F = wg.shape[1]
    bm = _pick(S, 256)

    def body(h_ref, g_ref, wg_ref, wu_ref, n_ref, gate_ref, up_ref, a_ref):
        n = _rms(h_ref[...], g_ref[...]).astype(BF16)
        n_ref[...] = n
        gate = _dot(n, wg_ref[...])
        up = _dot(n, wu_ref[...])
        gate_ref[...] = gate.astype(BF16)
        up_ref[...] = up.astype(BF16)
        a_ref[...] = (gate * jax.nn.sigmoid(gate) * up).astype(BF16)

    return pl.pallas_call(
        body, name="ffn_fwd_up", grid=(S // bm,),
        in_specs=[_rows(bm, D), _whole((1, D)), _whole((D, F)), _whole((D, F))],
        out_specs=[_rows(bm, D), _rows(bm, F), _rows(bm, F), _rows(bm, F)],
        out_shape=[jax.ShapeDtypeStruct((S, D), BF16)] + [jax.ShapeDtypeStruct((S, F), BF16)] * 3,
        compiler_params=_params(),
    )(h, g_pre, wg, wu)


def _ffn_fwd_down(a, wd, h, g_post):
    S, F = a.shape
    D = wd.shape[1]
    bm = _pick(S, 256)

    def body(a_ref, wd_ref, h_ref, g_ref, hout_ref, f_ref):
        f = _dot(a_ref[...], wd_ref[...])
        f_ref[...] = f
        hout_ref[...] = h_ref[...] + 0.5 * _rms(f, g_ref[...])

    return pl.pallas_call(
        body, name="ffn_fwd_down", grid=(S // bm,),
        in_specs=[_rows(bm, F), _whole((F, D)), _rows(bm, D), _whole((1, D))],
        out_specs=[_rows(bm, D), _rows(bm, D)],
        out_shape=[jax.ShapeDtypeStruct((S, D), F32)] * 2,
        compiler_params=_params(),
    )(a, wd, h, g_post)


def _ffn_bwd_down(dh, f, g_post, wd, gate, up):
    S, D = dh.shape
    F = wd.shape[0]
    bm = _pick(S, 256)

    def body(dh_ref, f_ref, g_ref, wd_ref, gate_ref, up_ref, df_ref, dgate_ref, dup_ref, dg_ref):
        _, vjp = jax.vjp(lambda t, g: 0.5 * _rms(t, g), f_ref[...], g_ref[...])
        df, dg = vjp(dh_ref[...])

        @pl.when(pl.program_id(0) == 0)
        def _():
            dg_ref[...] = jnp.zeros_like(dg_ref)

        dg_ref[...] += dg
        dfb = df.astype(BF16)
        df_ref[...] = dfb
        da = _dot_nt(dfb, wd_ref[...])
        gt = gate_ref[...].astype(F32)
        sig = jax.nn.sigmoid(gt)
        silu = gt * sig
        dup_ref[...] = (da * silu).astype(BF16)
        dgate_ref[...] = (da * up_ref[...].astype(F32) * (sig + silu * (1.0 - sig))).astype(BF16)

    return pl.pallas_call(
        body, name="ffn_bwd_down", grid=(S // bm,),
        in_specs=[_rows(bm, D), _rows(bm, D), _whole((1, D)), _whole((F, D)), _rows(bm, F), _rows(bm, F)],
        out_specs=[_rows(bm, D), _rows(bm, F), _rows(bm, F), _whole((1, D))],
        out_shape=[jax.ShapeDtypeStruct((S, D), BF16), jax.ShapeDtypeStruct((S, F), BF16),
                   jax.ShapeDtypeStruct((S, F), BF16), jax.ShapeDtypeStruct((1, D), F32)],
        compiler_params=_params(),
    )(dh, f, g_post, wd, gate, up)


def _ffn_bwd_up(dgate, dup, wg, wu, h_in, g_pre, dh):
    S, F = dgate.shape
    D = wg.shape[0]
    bm = _pick(S, 256)

    def body(dgate_ref, dup_ref, wg_ref, wu_ref, h_ref, g_ref, dh_ref, dhin_ref, dg_ref):
        dn = _dot_nt(dgate_ref[...], wg_ref[...]) + _dot_nt(dup_ref[...], wu_ref[...])
        _, vjp = jax.vjp(_rms, h_ref[...], g_ref[...])
        dhx, dg = vjp(dn)

        @pl.when(pl.program_id(0) == 0)
        def _():
            dg_ref[...] = jnp.zeros_like(dg_ref)

        dg_ref[...] += dg
        dhin_ref[...] = dh_ref[...] + dhx

    return pl.pallas_call(
        body, name="ffn_bwd_up", grid=(S // bm,),
        in_specs=[_rows(bm, F), _rows(bm, F), _whole((D, F)), _whole((D, F)), _rows(bm, D), _whole((1, D)),
                  _rows(bm, D)],
        out_specs=[_rows(bm, D), _whole((1, D))],
        out_shape=[jax.ShapeDtypeStruct((S, D), F32), jax.ShapeDtypeStruct((1, D), F32)],
        compiler_params=_params(),
    )(dgate, dup, wg, wu, h_in, g_pre, dh)


def _matmul(a, b, *, ta=False, tb=False, out_dtype=BF16, name):
    M, K = (a.shape[1], a.shape[0]) if ta else a.shape
    N = b.shape[0] if tb else b.shape[1]
    acc_budget = 12 * 1024 * 1024
    bm, bk = _pick(M, 1536), _pick(K, 512)
    while N * bm * 4 > acc_budget and bm % (2 * LANE) == 0:
        bm //= 2
    bn = N if N * bm * 4 <= acc_budget else _pick(N, 1536)
    nk = K // bk

    def body(a_ref, b_ref, o_ref, acc_ref):
        kk = pl.program_id(2)

        @pl.when(kk == 0)
        def _():
            acc_ref[...] = jnp.zeros_like(acc_ref)

        av, bv = a_ref[...], b_ref[...]
        dims = (((0 if ta else 1,), (1 if tb else 0,)), ((), ()))
        acc_ref[...] += lax.dot_general(av, bv, dims, preferred_element_type=F32)

        @pl.when(kk == nk - 1)
        def _():
            o_ref[...] = acc_ref[...].astype(o_ref.dtype)

    a_spec = pl.BlockSpec((bk, bm), lambda i, j, k: (k, i)) if ta else pl.BlockSpec((bm, bk), lambda i, j, k: (i, k))
    b_spec = pl.BlockSpec((bn, bk), lambda i, j, k: (j, k)) if tb else pl.BlockSpec((bk, bn), lambda i, j, k: (k, j))
    return pl.pallas_call(
        body, name=name, grid=(M // bm, N // bn, nk),
        in_specs=[a_spec, b_spec],
        out_specs=pl.BlockSpec((bm, bn), lambda i, j, k: (i, j)),
        out_shape=jax.ShapeDtypeStruct((M, N), out_dtype),
        scratch_shapes=[pltpu.VMEM((bm, bn), F32)],
        compiler_params=_params(),
    )(a, b)


def _mix_fwd_in(h, g_pre, win, wgate, b_gate, b_forget):
    S, D = h.shape
    PW = win.shape[1] - LANE
    G = wgate.shape[1]
    bm = _pick(S, 256)

    def body(h_ref, g_ref, win_ref, wgate_ref, bg_ref, bf_ref, u_ref, proj_ref, fl_ref, sg_ref):
        u = _rms(h_ref[...], g_ref[...]).astype(BF16)
        u_ref[...] = u
        proj = _dot(u, win_ref[...])
        proj_ref[...] = proj[:, :PW].astype(BF16)
        fl_ref[...] = proj[:, PW:] + bf_ref[...]
        sg_ref[...] = jax.nn.sigmoid(_dot(u, wgate_ref[...]) + bg_ref[...]).astype(BF16)

    return pl.pallas_call(
        body, name="mix_fwd_in", grid=(S // bm,),
        in_specs=[_rows(bm, D), _whole((1, D)), _whole((D, PW + LANE)), _whole((D, G)), _whole((1, G)),
                  _whole((1, LANE))],
        out_specs=[_rows(bm, D), _rows(bm, PW), _rows(bm, LANE), _rows(bm, G)],
        out_shape=[jax.ShapeDtypeStruct((S, D), BF16), jax.ShapeDtypeStruct((S, PW), BF16),
                   jax.ShapeDtypeStruct((S, LANE), F32), jax.ShapeDtypeStruct((S, G), BF16)],
        compiler_params=_params(),
    )(h, g_pre, win, wgate, b_gate, b_forget)


def _mix_fwd_out(o_sb, o_fx, o_mem, sg, w_sb, w_fx, w_mem, w_out, h, g_post):
    S, D = h.shape
    bm = _pick(S, 256)
    widths = (o_sb.shape[1], o_fx.shape[1], o_mem.shape[1])

    def body(osb_ref, ofx_ref, omem_ref, sg_ref, wsb_ref, wfx_ref, wmem_ref, wout_ref, h_ref, g_ref,
             hout_ref, z_ref, merged_ref):
        s = sg_ref[...].astype(F32)
        merged = (s[:, :D] * _dot(osb_ref[...], wsb_ref[...]) + s[:, D:2 * D] * _dot(ofx_ref[...], wfx_ref[...])
                  + s[:, 2 * D:] * _dot(omem_ref[...], wmem_ref[...]))
        mb = merged.astype(BF16)
        merged_ref[...] = mb
        z = _dot(mb, wout_ref[...])
        z_ref[...] = z
        hout_ref[...] = h_ref[...] + _rms(z, g_ref[...])

    return pl.pallas_call(
        body, name="mix_fwd_out", grid=(S // bm,),
        in_specs=[_rows(bm, widths[0]), _rows(bm, widths[1]), _rows(bm, widths[2]), _rows(bm, 3 * D),
                  _whole((widths[0], D)), _whole((widths[1], D)), _whole((widths[2], D)), _whole((D, D)),
                  _rows(bm, D), _whole((1, D))],
        out_specs=[_rows(bm, D), _rows(bm, D), _rows(bm, D)],
        out_shape=[jax.ShapeDtypeStruct((S, D), F32), jax.ShapeDtypeStruct((S, D), F32),
                   jax.ShapeDtypeStruct((S, D), BF16)],
        compiler_params=_params(),
    )(o_sb, o_fx, o_mem, sg, w_sb, w_fx, w_mem, w_out, h, g_post)


def _mix_bwd_out(dh, z, g_post, w_out, o_sb, o_fx, o_mem, w_sb, w_fx, w_mem, sg):
    S, D = dh.shape
    bm = _pick(S, 256)
    widths = (o_sb.shape[1], o_fx.shape[1], o_mem.shape[1])

    def body(dh_ref, z_ref, g_ref, wout_ref, osb_ref, ofx_ref, omem_ref, wsb_ref, wfx_ref, wmem_ref, sg_ref,
             dz_ref, dbsb_ref, dbfx_ref, dbmem_ref, dosb_ref, dofx_ref, domem_ref, dgp_ref, dbg_ref, dg_ref):
        _, vjp = jax.vjp(_rms, z_ref[...], g_ref[...])
        dz, dg = vjp(dh_ref[...])

        @pl.when(pl.program_id(0) == 0)
        def _():
            dg_ref[...] = jnp.zeros_like(dg_ref)
            dbg_ref[...] = jnp.zeros_like(dbg_ref)

        dg_ref[...] += dg
        dzb = dz.astype(BF16)
        dz_ref[...] = dzb
        dmerged = _dot_nt(dzb, wout_ref[...])
        s = sg_ref[...].astype(F32)
        branches = ((osb_ref, wsb_ref, dbsb_ref, dosb_ref), (ofx_ref, wfx_ref, dbfx_ref, dofx_ref),
                    (omem_ref, wmem_ref, dbmem_ref, domem_ref))
        for k, (o_ref, w_ref, db_ref, do_ref) in enumerate(branches):
            gs = s[:, k * D:(k + 1) * D]
            dbb = (dmerged * gs).astype(BF16)
            db_ref[...] = dbb
            do_ref[...] = _dot_nt(dbb, w_ref[...]).astype(BF16)
            dgp = dmerged * _dot(o_ref[...], w_ref[...]) * gs * (1.0 - gs)
            dgp_ref[:, k * D:(k + 1) * D] = dgp.astype(BF16)
            dbg_ref[:, k * D:(k + 1) * D] += jnp.sum(dgp, axis=0, keepdims=True)

    return pl.pallas_call(
        body, name="mix_bwd_out", grid=(S // bm,),
        in_specs=[_rows(bm, D), _rows(bm, D), _whole((1, D)), _whole((D, D)),
                  _rows(bm, widths[0]), _rows(bm, widths[1]), _rows(bm, widths[2]),
                  _whole((widths[0], D)), _whole((widths[1], D)), _whole((widths[2], D)), _rows(bm, 3 * D)],
        out_specs=[_rows(bm, D)] * 4 + [_rows(bm, widths[0]), _rows(bm, widths[1]), _rows(bm, widths[2]),
                                        _rows(bm, 3 * D), _whole((1, 3 * D)), _whole((1, D))],
        out_shape=[jax.ShapeDtypeStruct((S, D), BF16)] * 4
        + [jax.ShapeDtypeStruct((S, w), BF16) for w in widths]
        + [jax.ShapeDtypeStruct((S, 3 * D), BF16), jax.ShapeDtypeStruct((1, 3 * D), F32),
           jax.ShapeDtypeStruct((1, D), F32)],
        compiler_params=_params(),
    )(dh, z, g_post, w_out, o_sb, o_fx, o_mem, w_sb, w_fx, w_mem, sg)


def _mix_bwd_in(dproj, dgp, win, wgate, h_in, g_pre, dh):
    S, PWL = dproj.shape
    G = dgp.shape[1]
    D = h_in.shape[1]
    bm = _pick(S, 256)

    def body(dproj_ref, dgp_ref, win_ref, wgate_ref, h_ref, g_ref, dh_ref, dhin_ref, dg_ref):
        du = _dot_nt(dproj_ref[...], win_ref[...]) + _dot_nt(dgp_ref[...], wgate_ref[...])
        _, vjp = jax.vjp(_rms, h_ref[...], g_ref[...])
        dhx, dg = vjp(du)

        @pl.when(pl.program_id(0) == 0)
        def _():
            dg_ref[...] = jnp.zeros_like(dg_ref)

        dg_ref[...] += dg
        dhin_ref[...] = dh_ref[...] + dhx

    return pl.pallas_call(
        body, name="mix_bwd_in", grid=(S // bm,),
        in_specs=[_rows(bm, PWL), _rows(bm, G), _whole((D, PWL)), _whole((D, G)), _rows(bm, D), _whole((1, D)),
                  _rows(bm, D)],
        out_specs=[_rows(bm, D), _whole((1, D))],
        out_shape=[jax.ShapeDtypeStruct((S, D), F32), jax.ShapeDtypeStruct((1, D), F32)],
        compiler_params=_params(),
    )(dproj, dgp, win, wgate, h_in, g_pre, dh)


def _log_sigmoid(x):
    return jnp.minimum(x, 0.0) - jnp.log(1.0 + jnp.exp(-jnp.abs(x)))


def _fox_cumsum(fl):
    S = fl.shape[0]
    rb = _pick(S, LANE)

    def body(fl_ref, c_ref, carry_ref):
        @pl.when(pl.program_id(0) == 0)
        def _():
            carry_ref[...] = jnp.zeros_like(carry_ref)

        r = lax.broadcasted_iota(jnp.int32, (rb, rb), 0)
        cidx = lax.broadcasted_iota(jnp.int32, (rb, rb), 1)
        tri = (cidx <= r).astype(BF16)
        hi, mid, lo = _split3(_log_sigmoid(fl_ref[...]))
        c = _dot(tri, hi) + _dot(tri, mid) + _dot(tri, lo) + carry_ref[...]
        c_ref[...] = c
        carry_ref[...] = c[rb - 1:rb, :]

    return pl.pallas_call(
        body, name="fox_cumsum", grid=(S // rb,),
        in_specs=[_rows(rb, LANE)], out_specs=_rows(rb, LANE),
        out_shape=jax.ShapeDtypeStruct((S, LANE), F32),
        scratch_shapes=[pltpu.VMEM((1, LANE), F32)],
        compiler_params=_params(),
    )(fl)


def _fox_dlogit(dc, fl):
    S = fl.shape[0]
    rb = _pick(S, LANE)
    nb = S // rb

    def body(dc_ref, fl_ref, dfl_ref, dbf_ref, carry_ref):
        @pl.when(pl.program_id(0) == 0)
        def _():
            carry_ref[...] = jnp.zeros_like(carry_ref)
            dbf_ref[...] = jnp.zeros_like(dbf_ref)

        r = lax.broadcasted_iota(jnp.int32, (rb, rb), 0)
        cidx = lax.broadcasted_iota(jnp.int32, (rb, rb), 1)
        tri = (cidx >= r).astype(BF16)
        hi, mid, lo = _split3(dc_ref[...])
        rc = _dot(tri, hi) + _dot(tri, mid) + _dot(tri, lo) + carry_ref[...]
        carry_ref[...] = rc[0:1, :]
        dfl = rc * jax.nn.sigmoid(-fl_ref[...])
        dfl_ref[...] = dfl.astype(BF16)
        dbf_ref[...] += jnp.sum(dfl, axis=0, keepdims=True)

    rev = pl.BlockSpec((rb, LANE), lambda i: (nb - 1 - i, 0))
    return pl.pallas_call(
        body, name="fox_dlogit", grid=(nb,),
        in_specs=[rev, rev], out_specs=[rev, _whole((1, LANE))],
        out_shape=[jax.ShapeDtypeStruct((S, LANE), BF16), jax.ShapeDtypeStruct((1, LANE), F32)],
        scratch_shapes=[pltpu.VMEM((1, LANE), F32)],
        compiler_params=_params(),
    )(dc, fl)


def _attn_blocks(kind, S, Sk):
    tq = _pick(S, 512)
    tc = LANE if kind == "sb" else _pick(Sk, 256)
    return tq, tc


def _is_power_of_two(x):
    return math.frexp(x)[0] == 0.5


def _sb_logs(z):
    ln = -jnp.maximum(z, 0.0) - jnp.log(1.0 + jnp.exp(-jnp.abs(z)))
    return ln + z, ln


def _head_lanes(pack, dh):
    lane = lax.broadcasted_iota(jnp.int32, (1, LANE), 1)
    return [(lane >= hh * dh) & (lane < (hh + 1) * dh) for hh in range(pack)]


def _by_head(sel, parts):
    out = parts[0]
    for hh in range(1, len(parts)):
        out = jnp.where(sel[hh], parts[hh], out)
    return out


def _only_head(sel, hh, x):
    return x if len(sel) == 1 else jnp.where(sel[hh], x, jnp.zeros_like(x))


def _tail(x, r0):
    return x if not r0 else x[r0:]


def _put_tail(x, tail, r0):
    return tail if not r0 else jnp.concatenate([x[:r0], tail], axis=0)


def _add_tail(x, tail, r0):
    return x + tail if not r0 else jnp.concatenate([x[:r0], x[r0:] + tail], axis=0)


def _q_cols(tq, first):
    return pl.BlockSpec((tq, LANE), lambda g, i: (i, first // LANE + g))


def _k_cols(rows, first):
    return pl.BlockSpec((rows, LANE), lambda g, i: (0, first // LANE + g))


def _attn_fwd(kind, q, k, v, n_heads, dh, ccol=None, crow=None):
    (qa, q0), (ka, k0), (va, v0) = q, k, v
    S, Sk = qa.shape[0], ka.shape[0]
    pack = LANE // dh
    tq, tc = _attn_blocks(kind, S, Sk)
    scale = dh ** -0.5
    fold = _is_power_of_two(scale)
    causal = kind != "mem"
    n_diag = tq // tc if causal else 0
    unroll = 2 if causal else 1
    assert n_diag % unroll == 0 and (Sk // tc) % unroll == 0

    def body(*refs):
        if kind == "fox":
            q_ref, k_ref, v_ref, cc_ref, cr_ref, o_ref, lse_ref = refs
        else:
            q_ref, k_ref, v_ref, o_ref, lse_ref = refs
        i = pl.program_id(1)
        n_full = (i * tq) // tc if causal else Sk // tc
        qpos = i * tq + lax.broadcasted_iota(jnp.int32, (tq, tc), 0)
        kio = lax.broadcasted_iota(jnp.int32, (tq, tc), 1)
        heads = range(pack)
        sel = _head_lanes(pack, dh)
        q2 = q_ref[...] * scale if fold else q_ref[...]
        qs = [_only_head(sel, hh, q2) for hh in heads]

        def kv(jc):
            off = pl.multiple_of(jc * tc, tc)
            return off, k_ref[pl.ds(off, tc), :], v_ref[pl.ds(off, tc), :]

        if kind == "sb":
            tri = (lax.broadcasted_iota(jnp.int32, (tc, tc), 0) > lax.broadcasted_iota(jnp.int32, (tc, tc), 1)
                   ).astype(BF16)

            def chunk(jc, r0, runs, acc):
                off, k2, v2 = kv(jc)
                new_runs, pv = [], []
                for hh in heads:
                    lb, ln = _sb_logs(_dot_nt(_tail(qs[hh], r0), k2))
                    if r0 is not None:
                        mask = (off + _tail(kio, r0)) < _tail(qpos, r0)
                        ln = jnp.where(mask, ln, 0.0)
                    w = jnp.exp(lb + _cumdot(ln, tri) + _tail(runs[hh], r0))
                    if r0 is not None:
                        w = jnp.where(mask, w, 0.0)
                    pv.append(_dot(w.astype(BF16), v2))
                    new_runs.append(_add_tail(runs[hh], jnp.sum(ln, axis=1, keepdims=True), r0))
                return tuple(new_runs), _add_tail(acc, _by_head(sel, pv), r0)

            state = (tuple(jnp.zeros((tq, 1), F32) for _ in heads), jnp.zeros((tq, LANE), F32))
            for d in range(n_diag - 1, -1, -1):
                state = chunk(n_full + d, d * tc, *state)

            def trip(t, st):
                for u in range(unroll):
                    st = chunk(n_full - 1 - unroll * t - u, None, *st)
                return st

            runs, acc = lax.fori_loop(0, n_full // unroll, trip, state)
            o_ref[...] = acc.astype(o_ref.dtype)
            for hh in heads:
                lse_ref[hh] = runs[hh]
        else:
            def chunk(jc, r0, ms, ls, acc):
                off, k2, v2 = kv(jc)
                new_ms, new_ls, alphas, pv = [], [], [], []
                for hh in heads:
                    z = _dot_nt(_tail(qs[hh], r0), k2)
                    if not fold:
                        z = z * scale
                    if kind == "fox":
                        z = z + _tail(cc_ref[hh], r0) - cr_ref[hh, pl.ds(jc, 1), :]
                    if r0 is not None:
                        z = jnp.where((off + _tail(kio, r0)) <= _tail(qpos, r0), z, NEG)
                    m_old, l_old = _tail(ms[hh], r0), _tail(ls[hh], r0)
                    m_new = jnp.maximum(m_old, jnp.max(z, axis=1, keepdims=True))
                    alpha = jnp.exp(m_old - m_new)
                    p = jnp.exp(z - m_new)
                    new_ms.append(_put_tail(ms[hh], m_new, r0))
                    new_ls.append(_put_tail(ls[hh], alpha * l_old + jnp.sum(p, axis=1, keepdims=True), r0))
                    alphas.append(alpha)
                    pv.append(_dot(p.astype(BF16), v2))
                acc_new = _by_head(sel, alphas) * _tail(acc, r0) + _by_head(sel, pv)
                return tuple(new_ms), tuple(new_ls), _put_tail(acc, acc_new, r0)

            state = (tuple(jnp.full((tq, 1), NEG, F32) for _ in heads), tuple(jnp.zeros((tq, 1), F32) for _ in heads),
                     jnp.zeros((tq, LANE), F32))

            def trip(t, st):
                for u in range(unroll):
                    st = chunk(unroll * t + u, None, *st)
                return st

            state = lax.fori_loop(0, n_full // unroll, trip, state)
            for d in range(n_diag):
                state = chunk(n_full + d, d * tc, *state)
            ms, ls, acc = state
            o_ref[...] = (acc / _by_head(sel, ls)).astype(o_ref.dtype)
            for hh in heads:
                lse_ref[hh] = ms[hh] + jnp.log(ls[hh])

    colspec = pl.BlockSpec((pack, tq, 1), lambda g, i: (g, i, 0))
    in_specs, args = [_q_cols(tq, q0), _k_cols(Sk, k0), _k_cols(Sk, v0)], [qa, ka, va]
    if kind == "fox":
        in_specs += [colspec, pl.BlockSpec((pack, Sk // tc, tc), lambda g, i: (g, 0, 0))]
        args += [ccol, crow]
    return pl.pallas_call(
        body, name="attn_fwd_" + kind, grid=(n_heads // pack, S // tq),
        in_specs=in_specs, out_specs=[_q_cols(tq, 0), colspec],
        out_shape=[jax.ShapeDtypeStruct((S, n_heads * dh), BF16), jax.ShapeDtypeStruct((n_heads, S, 1), F32)],
        compiler_params=_params(),
    )(*args)


def _attn_bwd(kind, q, k, v, o, do, n_heads, dh, ccol=None, crow=None, lse=None):
    (qa, q0), (ka, k0), (va, v0) = q, k, v
    S, Sk = qa.shape[0], ka.shape[0]
    pack = LANE // dh
    tq, tc = _attn_blocks(kind, S, Sk)
    scale = dh ** -0.5
    fold = _is_power_of_two(scale)
    causal = kind != "mem"
    n_diag = tq // tc if causal else 0
    unroll = 2 if kind == "sb" else 1
    assert n_diag % unroll == 0 and (Sk // tc) % unroll == 0
    nq = S // tq

    def body(*refs):
        if kind == "fox":
            (q_ref, k_ref, v_ref, o_ref, do_ref, cc_ref, cr_ref, lse_ref,
             dq_ref, dk_ref, dv_ref, dc_ref, dcc_ref, dk_acc, dv_acc, dc_acc) = refs
        else:
            q_ref, k_ref, v_ref, o_ref, do_ref, lse_ref, dq_ref, dk_ref, dv_ref, dk_acc, dv_acc = refs
        i = pl.program_id(1)

        @pl.when(i == 0)
        def _():
            dk_acc[...] = jnp.zeros_like(dk_acc)
            dv_acc[...] = jnp.zeros_like(dv_acc)
            if kind == "fox":
                dc_acc[...] = jnp.zeros_like(dc_acc)

        n_full = (i * tq) // tc if causal else Sk // tc
        qpos = i * tq + lax.broadcasted_iota(jnp.int32, (tq, tc), 0)
        kio = lax.broadcasted_iota(jnp.int32, (tq, tc), 1)
        heads = range(pack)
        sel = _head_lanes(pack, dh)
        q2 = q_ref[...] * scale if fold else q_ref[...]
        do2 = do_ref[...]
        qs = [_only_head(sel, hh, q2) for hh in heads]
        dos = [_only_head(sel, hh, do2) for hh in heads]

        def kv(jc):
            off = pl.multiple_of(jc * tc, tc)
            return off, k_ref[pl.ds(off, tc), :], v_ref[pl.ds(off, tc), :]

        def accumulate(off, k2, dzb, wb, dq, r0):
            q2t, do2t = _tail(q2, r0), _tail(do2, r0)
            dk_acc[pl.ds(off, tc), :] += _by_head(sel, [_dot_tn(dzb[hh], q2t) for hh in heads])
            dv_acc[pl.ds(off, tc), :] += _by_head(sel, [_dot_tn(wb[hh], do2t) for hh in heads])
            return _add_tail(dq, _by_head(sel, [_dot(dzb[hh], k2) for hh in heads]), r0)

        if kind == "sb":
            r = lax.broadcasted_iota(jnp.int32, (tc, tc), 0)
            cidx = lax.broadcasted_iota(jnp.int32, (tc, tc), 1)
            tri_inc = (r <= cidx).astype(BF16)
            tri_exc = (r < cidx).astype(BF16)

            def chunk(jc, r0, pres, pres_e, dq):
                off, k2, v2 = kv(jc)
                new_pres, new_pres_e, dzb, wb = [], [], [], []
                for hh in heads:
                    lb, ln = _sb_logs(_dot_nt(_tail(qs[hh], r0), k2))
                    if r0 is not None:
                        mask = (off + _tail(kio, r0)) < _tail(qpos, r0)
                        ln = jnp.where(mask, ln, 0.0)
                    w = jnp.exp(lb + (_tail(lse_ref[hh], r0) - _tail(pres[hh], r0) - _cumdot(ln, tri_inc)))
                    if r0 is not None:
                        w = jnp.where(mask, w, 0.0)
                    e = w * _dot_nt(_tail(dos[hh], r0), v2)
                    beta = jnp.exp(lb)
                    dz = e * (1.0 - beta) - beta * (_tail(pres_e[hh], r0) + _cumdot(e, tri_exc))
                    if r0 is not None:
                        dz = jnp.where(mask, dz, 0.0)
                    dzb.append(dz.astype(BF16))
                    wb.append(w.astype(BF16))
                    new_pres.append(_add_tail(pres[hh], jnp.sum(ln, axis=1, keepdims=True), r0))
                    new_pres_e.append(_add_tail(pres_e[hh], jnp.sum(e, axis=1, keepdims=True), r0))
                return tuple(new_pres), tuple(new_pres_e), accumulate(off, k2, dzb, wb, dq, r0)

            state = (tuple(jnp.zeros((tq, 1), F32) for _ in heads), tuple(jnp.zeros((tq, 1), F32) for _ in heads),
                     jnp.zeros((tq, LANE), F32))
        else:
            prod = o_ref[...].astype(F32) * do2.astype(F32)
            dsum = [jnp.sum(_only_head(sel, hh, prod), axis=1, keepdims=True) for hh in heads]

            def chunk(jc, r0, rowsums, dq):
                off, k2, v2 = kv(jc)
                new_rowsums, dsb, pb = [], [], []
                for hh in heads:
                    z = _dot_nt(_tail(qs[hh], r0), k2)
                    if not fold:
                        z = z * scale
                    if kind == "fox":
                        z = z + _tail(cc_ref[hh], r0) - cr_ref[hh, pl.ds(jc, 1), :]
                    if r0 is not None:
                        z = jnp.where((off + _tail(kio, r0)) <= _tail(qpos, r0), z, NEG)
                    p = jnp.exp(z - _tail(lse_ref[hh], r0))
                    ds = p * (_dot_nt(_tail(dos[hh], r0), v2) - _tail(dsum[hh], r0))
                    dsb.append(ds.astype(BF16))
                    pb.append(p.astype(BF16))
                    if kind == "fox":
                        dc_acc[hh, pl.ds(jc, 1), :] -= jnp.sum(ds, axis=0, keepdims=True)
                        new_rowsums.append(_add_tail(rowsums[hh], jnp.sum(ds, axis=1, keepdims=True), r0))
                    else:
                        new_rowsums.append(rowsums[hh])
                return tuple(new_rowsums), accumulate(off, k2, dsb, pb, dq, r0)

            state = (tuple(jnp.zeros((tq, 1), F32) for _ in heads), jnp.zeros((tq, LANE), F32))

        def trip(t, st):
            for u in range(unroll):
                st = chunk(unroll * t + u, None, *st)
            return st

        state = lax.fori_loop(0, n_full // unroll, trip, state)
        for d in range(n_diag):
            state = chunk(n_full + d, d * tc, *state)
        dq_ref[...] = (state[-1] * scale).astype(dq_ref.dtype)
        if kind == "fox":
            for hh in heads:
                dcc_ref[hh] = state[0][hh]

        @pl.when(i == nq - 1)
        def _():
            dk = dk_acc[...] if fold else dk_acc[...] * scale
            dk_ref[...] = dk.astype(dk_ref.dtype)
            dv_ref[...] = dv_acc[...].astype(dv_ref.dtype)
            if kind == "fox":
                dc_ref[...] = dc_acc[...]

    colspec = pl.BlockSpec((pack, tq, 1), lambda g, i: (g, i, 0))
    rowspec = pl.BlockSpec((pack, Sk // tc, tc), lambda g, i: (g, 0, 0))
    in_specs = [_q_cols(tq, q0), _k_cols(Sk, k0), _k_cols(Sk, v0), _q_cols(tq, 0), _q_cols(tq, 0)]
    args = [qa, ka, va, o, do]
    if kind == "fox":
        in_specs += [colspec, rowspec]
        args += [ccol, crow]
    in_specs += [colspec]
    args += [lse]
    width = n_heads * dh
    out_specs = [_q_cols(tq, 0), _k_cols(Sk, 0), _k_cols(Sk, 0)]
    out_shape = [jax.ShapeDtypeStruct((S, width), BF16), jax.ShapeDtypeStruct((Sk, width), BF16),
                 jax.ShapeDtypeStruct((Sk, width), BF16)]
    scratch = [pltpu.VMEM((Sk, LANE), F32), pltpu.VMEM((Sk, LANE), F32)]
    if kind == "fox":
        out_specs += [rowspec, colspec]
        out_shape += [jax.ShapeDtypeStruct((n_heads, Sk // tc, tc), F32), jax.ShapeDtypeStruct((n_heads, S, 1), F32)]
        scratch.append(pltpu.VMEM((pack, Sk // tc, tc), F32))
    return pl.pallas_call(
        body, name="attn_bwd_" + kind, grid=(n_heads // pack, nq),
        in_specs=in_specs, out_specs=out_specs, out_shape=out_shape, scratch_shapes=scratch,
        compiler_params=_params(),
    )(*args)


def _mem_norm(mem, g):
    M, D = mem.shape

    def body(mem_ref, g_ref, out_ref):
        out_ref[...] = _rms(mem_ref[...], g_ref[...]).astype(BF16)

    return pl.pallas_call(
        body, name="mem_norm", grid=(1,),
        in_specs=[_whole((M, D)), _whole((1, D))], out_specs=_whole((M, D)),
        out_shape=jax.ShapeDtypeStruct((M, D), BF16), compiler_params=_params(),
    )(mem, g)


def _mem_norm_bwd(mem, g, dmem_n):
    M, D = mem.shape
    L = dmem_n.shape[0]

    def body(mem_ref, g_ref, d_ref, dg_ref):
        d = d_ref[0]
        for l in range(1, L):
            d = d + d_ref[l]
        _, vjp = jax.vjp(_rms, mem_ref[...], g_ref[...])
        dg_ref[...] = vjp(d)[1]

    return pl.pallas_call(
        body, name="mem_norm_bwd", grid=(1,),
        in_specs=[_whole((M, D)), _whole((1, D)), _whole((L, M, D))], out_specs=_whole((1, D)),
        out_shape=jax.ShapeDtypeStruct((1, D), F32), compiler_params=_params(),
    )(mem, g, dmem_n)


def _loss_head(h, target):
    S, D = h.shape
    bm = _pick(S, 512)

    def body(h_ref, t_ref, dh_ref, loss_ref):
        err = h_ref[...] - t_ref[...]
        dh_ref[...] = err * (1.0 / D)

        @pl.when(pl.program_id(0) == 0)
        def _():
            loss_ref[...] = jnp.zeros_like(loss_ref)

        loss_ref[...] += 0.5 * jnp.sum(jnp.mean(err * err, axis=-1, keepdims=True), axis=0, keepdims=True)

    return pl.pallas_call(
        body, name="loss_head", grid=(S // bm,),
        in_specs=[_rows(bm, D), _rows(bm, D)], out_specs=[_rows(bm, D), _whole((8, LANE))],
        out_shape=[jax.ShapeDtypeStruct((S, D), F32), jax.ShapeDtypeStruct((8, LANE), F32)],
        compiler_params=_params(),
    )(h, target)


def _adamw(w, g, m, v, name):
    R, C = w.shape
    rb = R if R * C * 4 <= (1 << 20) else _pick(R, 256)
    if R % rb:
        rb = R
    c1 = 1.0 - ADAM_B1 ** ADAM_STEP
    c2 = 1.0 - ADAM_B2 ** ADAM_STEP

    def body(w_ref, g_ref, m_ref, v_ref, d_ref, mo_ref, vo_ref):
        gv = g_ref[...]
        mn = ADAM_B1 * m_ref[...] + (1.0 - ADAM_B1) * gv
        vn = ADAM_B2 * v_ref[...] + (1.0 - ADAM_B2) * (gv * gv)
        mo_ref[...] = mn
        vo_ref[...] = vn
        d_ref[...] = -ADAM_LR * ((mn / c1) / (jnp.sqrt(vn / c2) + ADAM_EPS) + ADAM_WD * w_ref[...])

    return pl.pallas_call(
        body, name=name, grid=(R // rb,),
        in_specs=[_rows(rb, C)] * 4, out_specs=[_rows(rb, C)] * 3,
        out_shape=[jax.ShapeDtypeStruct((R, C), F32)] * 3, compiler_params=_params(),
    )(w, g, m, v)


def _adamw_reduced(w, m, v, mine, theirs, c_idx, first_row, name):
    L, a, b = w.shape
    Lh = L // 2
    rb = _shard_row_block(a)
    nb = a // rb
    assert first_row % rb == 0
    c1 = 1.0 - ADAM_B1 ** ADAM_STEP
    c2 = 1.0 - ADAM_B2 ** ADAM_STEP

    def own(i, c_ref):
        return (i, 0)

    def reduced(i, c_ref):
        return (first_row // rb + ((i // nb) % Lh) * nb + i % nb, 0)

    def body(c_ref, w_ref, m_ref, v_ref, mine_ref, theirs_ref, g_ref, d_ref, mo_ref, vo_ref):
        half = (pl.program_id(0) // nb) // Lh
        gv = jnp.where(c_ref[0] == half, mine_ref[...], theirs_ref[...])
        g_ref[...] = gv
        mn = ADAM_B1 * m_ref[...] + (1.0 - ADAM_B1) * gv
        vn = ADAM_B2 * v_ref[...] + (1.0 - ADAM_B2) * (gv * gv)
        mo_ref[...] = mn
        vo_ref[...] = vn
        d_ref[...] = -ADAM_LR * ((mn / c1) / (jnp.sqrt(vn / c2) + ADAM_EPS) + ADAM_WD * w_ref[...])

    outs = pl.pallas_call(
        body, name=name,
        grid_spec=pltpu.PrefetchScalarGridSpec(
            num_scalar_prefetch=1, grid=(L * nb,),
            in_specs=[pl.BlockSpec((rb, b), own)] * 3 + [pl.BlockSpec((rb, b), reduced)] * 2,
            out_specs=[pl.BlockSpec((rb, b), own)] * 4),
        out_shape=[jax.ShapeDtypeStruct((L * a, b), F32)] * 4, compiler_params=_params(),
    )(c_idx, w.reshape(L * a, b), m.reshape(L * a, b), v.reshape(L * a, b), mine, theirs)
    return [t.reshape(L, a, b) for t in outs]


ANY = pl.BlockSpec(memory_space=pl.ANY)
MESH = pl.DeviceIdType.MESH


def _place():
    x, y, c = lax.axis_index("x"), lax.axis_index("y"), lax.axis_index("c")
    others = [(1 - x, y), (x, 1 - y), (1 - x, 1 - y)]
    return x, y, c, others


def _place_own(loc, chip_idx):
    _, R, C = loc.shape
    rb = _pick(R, 2 * FLAT_ROW_BLOCK)

    def body(chip_ref, loc_ref, out_ref):
        out_ref[...] = loc_ref[...]

    return pl.pallas_call(
        body, name="place_own",
        grid_spec=pltpu.PrefetchScalarGridSpec(
            num_scalar_prefetch=1, grid=(2, R // rb),
            in_specs=[pl.BlockSpec((None, rb, C), lambda hf, i, chip_ref: (hf, i, 0))],
            out_specs=pl.BlockSpec((None, None, rb, C), lambda hf, i, chip_ref: (chip_ref[0], hf, i, 0))),
        out_shape=jax.ShapeDtypeStruct((N_CHIPS, 2, R, C), loc.dtype), compiler_params=_params(),
    )(chip_idx, loc)


def _gather_weights(locs, owns):
    n = len(locs)

    def body(*refs):
        loc_refs, out_refs, (send_sems, recv_sems) = refs[:n], refs[2 * n:3 * n], refs[3 * n:]
        x, y, c, others = _place()
        me = 2 * x + y
        sibling = (x, y, 1 - c)

        def copy(a, k, src, dst, to):
            return pltpu.make_async_remote_copy(src_ref=src, dst_ref=dst, send_sem=send_sems.at[a, k],
                                                recv_sem=recv_sems.at[a, k], device_id=to, device_id_type=MESH)

        first = [copy(a, j, loc_refs[a].at[c], out_refs[a].at[me, c], (ox, oy, c))
                 for j, (ox, oy) in enumerate(others) for a in range(n)]
        for cp in first:
            cp.start()
        passed = []
        for j, (ox, oy) in enumerate(others):
            for a in range(n):
                landed = out_refs[a].at[2 * ox + oy, c]
                copy(a, j, loc_refs[a].at[c], landed, sibling).wait_recv()
                cp = copy(a, 3 + j, landed, landed, sibling)
                cp.start()
                passed.append(cp)
        for j, (ox, oy) in enumerate(others):
            for a in range(n):
                copy(a, 3 + j, loc_refs[a].at[c], out_refs[a].at[2 * ox + oy, 1 - c], sibling).wait_recv()
        for cp in first + passed:
            cp.wait_send()

    return pl.pallas_call(
        body, name="gather_weights", in_specs=[ANY] * (2 * n), out_specs=[ANY] * n,
        out_shape=[jax.ShapeDtypeStruct(own.shape, own.dtype) for own in owns],
        input_output_aliases={n + a: a for a in range(n)},
        scratch_shapes=[pltpu.SemaphoreType.DMA((n, 6)), pltpu.SemaphoreType.DMA((n, 6))],
    )(*locs, *owns)


def _pair_exchange(gs):
    n = len(gs)

    def body(*refs):
        g_refs, out_refs, (send_sems, recv_sems) = refs[:n], refs[n:2 * n], refs[2 * n:]
        x, y, c, _ = _place()
        copies = [pltpu.make_async_remote_copy(src_ref=g_refs[a].at[1 - c], dst_ref=out_refs[a],
                                               send_sem=send_sems.at[a], recv_sem=recv_sems.at[a],
                                               device_id=(x, y, 1 - c), device_id_type=MESH) for a in range(n)]
        for cp in copies:
            cp.start()
        for cp in copies:
            cp.wait()

    return pl.pallas_call(
        body, name="pair_exchange", in_specs=[ANY] * n, out_specs=[ANY] * n,
        out_shape=[jax.ShapeDtypeStruct(g.shape[1:], g.dtype) for g in gs],
        scratch_shapes=[pltpu.SemaphoreType.DMA((n,)), pltpu.SemaphoreType.DMA((n,))],
    )(*gs)


def _pair_sum(g, sib, c_idx):
    _, _, R, C = g.shape
    rb = _pick(R, 512)

    def body(c_ref, g_ref, s_ref, o_ref):
        o_ref[...] = (g_ref[...].astype(F32) + s_ref[...].astype(F32)).astype(o_ref.dtype)

    return pl.pallas_call(
        body, name="pair_sum",
        grid_spec=pltpu.PrefetchScalarGridSpec(
            num_scalar_prefetch=1, grid=(N_CHIPS, R // rb),
            in_specs=[pl.BlockSpec((None, None, rb, C), lambda j, i, c_ref: (c_ref[0], j, i, 0)),
                      pl.BlockSpec((None, rb, C), lambda j, i, c_ref: (j, i, 0))],
            out_specs=pl.BlockSpec((None, rb, C), lambda j, i, c_ref: (j, i, 0))),
        out_shape=jax.ShapeDtypeStruct((N_CHIPS, R, C), g.dtype), compiler_params=_params(),
    )(c_idx, g, sib)


def _chip_exchange(ps):
    n = len(ps)

    def body(*refs):
        p_refs, out_refs, (send_sems, recv_sems) = refs[:n], refs[n:2 * n], refs[2 * n:]
        x, y, c, others = _place()
        copies = []
        for j, (ox, oy) in enumerate(others):
            for a in range(n):
                cp = pltpu.make_async_remote_copy(src_ref=p_refs[a].at[2 * ox + oy], dst_ref=out_refs[a].at[j],
                                                  send_sem=send_sems.at[a, j], recv_sem=recv_sems.at[a, j],
                                                  device_id=(ox, oy, c), device_id_type=MESH)
                cp.start()
                copies.append(cp)
        for cp in copies:
            cp.wait()

    return pl.pallas_call(
        body, name="chip_exchange", in_specs=[ANY] * n, out_specs=[ANY] * n,
        out_shape=[jax.ShapeDtypeStruct((N_CHIPS - 1,) + p.shape[1:], p.dtype) for p in ps],
        scratch_shapes=[pltpu.SemaphoreType.DMA((n, 3)), pltpu.SemaphoreType.DMA((n, 3))],
    )(*ps)


def _chip_sum(p, r, chip_idx):
    _, R, C = r.shape
    rb = _pick(R, 512)

    def body(chip_ref, p_ref, r_ref, o_ref):
        acc = p_ref[...].astype(F32)
        for j in range(N_CHIPS - 1):
            acc = acc + r_ref[j].astype(F32)
        o_ref[...] = acc

    return pl.pallas_call(
        body, name="chip_sum",
        grid_spec=pltpu.PrefetchScalarGridSpec(
            num_scalar_prefetch=1, grid=(R // rb,),
            in_specs=[pl.BlockSpec((None, rb, C), lambda i, chip_ref: (chip_ref[0], i, 0)),
                      pl.BlockSpec((N_CHIPS - 1, rb, C), lambda i, chip_ref: (0, i, 0))],
            out_specs=pl.BlockSpec((rb, C), lambda i, chip_ref: (i, 0))),
        out_shape=jax.ShapeDtypeStruct((R, C), F32), compiler_params=_params(),
    )(chip_idx, p, r)


def _pair_swap(rhs):
    n = len(rhs)

    def body(*refs):
        rh_refs, out_refs, (send_sems, recv_sems) = refs[:n], refs[n:2 * n], refs[2 * n:]
        x, y, c, _ = _place()
        copies = [pltpu.make_async_remote_copy(src_ref=rh_refs[a], dst_ref=out_refs[a], send_sem=send_sems.at[a],
                                               recv_sem=recv_sems.at[a], device_id=(x, y, 1 - c),
                                               device_id_type=MESH) for a in range(n)]
        for cp in copies:
            cp.start()
        for cp in copies:
            cp.wait()

    return pl.pallas_call(
        body, name="pair_swap", in_specs=[ANY] * n, out_specs=[ANY] * n,
        out_shape=[jax.ShapeDtypeStruct(rh.shape, rh.dtype) for rh in rhs],
        scratch_shapes=[pltpu.SemaphoreType.DMA((n,)), pltpu.SemaphoreType.DMA((n,))],
    )(*rhs)


def _all_reduce_small(s):
    R, C = s.shape

    def body(s_ref, o_ref, buf, send_sems, recv_sems):
        x, y, c, _ = _place()
        me = 4 * x + 2 * y + c
        sends = []
        for k in range(1, N_DEV):
            fx, fy, fc = (k >> 2) & 1, (k >> 1) & 1, k & 1
            to = (x ^ fx, y ^ fy, c ^ fc)
            cp = pltpu.make_async_remote_copy(src_ref=s_ref, dst_ref=buf.at[me], send_sem=send_sems.at[k - 1],
                                              recv_sem=recv_sems.at[k - 1], device_id=to, device_id_type=MESH)
            cp.start()
            sends.append(cp)
        buf[me] = s_ref[...]
        for k in range(1, N_DEV):
            fx, fy, fc = (k >> 2) & 1, (k >> 1) & 1, k & 1
            frm = 4 * (x ^ fx) + 2 * (y ^ fy) + (c ^ fc)
            pltpu.make_async_remote_copy(src_ref=s_ref, dst_ref=buf.at[frm], send_sem=send_sems.at[k - 1],
                                         recv_sem=recv_sems.at[k - 1], device_id=(x, y, c),
                                         device_id_type=MESH).wait_recv()
        acc = buf[0]
        for d in range(1, N_DEV):
            acc = acc + buf[d]
        o_ref[...] = acc
        for cp in sends:
            cp.wait_send()

    vm = pl.BlockSpec(memory_space=pltpu.VMEM)
    return pl.pallas_call(
        body, name="all_reduce_small", in_specs=[vm], out_specs=vm,
        out_shape=jax.ShapeDtypeStruct((R, C), F32),
        scratch_shapes=[pltpu.VMEM((N_DEV, R, C), F32), pltpu.SemaphoreType.DMA((N_DEV - 1,)),
                        pltpu.SemaphoreType.DMA((N_DEV - 1,))],
    )(s)


def _padded(n):
    return -(-n // FLAT_UNIT) * FLAT_UNIT


def _pack_flat(pieces, dtype, row_block=FLAT_ROW_BLOCK):
    flat = []
    for p in pieces:
        p = p.reshape(-1).astype(dtype)
        flat.append(jnp.pad(p, (0, _padded(p.size) - p.size)))
    total = sum(p.size for p in flat)
    flat.append(jnp.zeros((-total) % (row_block * FLAT_COLS), dtype))
    return jnp.concatenate(flat).reshape(-1, FLAT_COLS)


def _unpack_flat(flat, shapes):
    lead = flat.shape[:-2]
    flat = flat.reshape(lead + (-1,))
    out, off = [], 0
    for shp in shapes:
        n = math.prod(shp)
        out.append(flat[..., off:off + n].reshape(lead + tuple(shp)))
        off += _padded(n)
    return out


def _shard_row_block(a):
    for rb in range(min(a, 512) // 16 * 16, 0, -16):
        if a % rb == 0:
            return rb
    return a


def _row_layout(shapes, n_layers):
    groups = {}
    for name, (a, b) in shapes.items():
        names, first, rows = groups.get(b, ((), {}, 0))
        rb = _shard_row_block(a)
        start = -(-rows // rb) * rb
        groups[b] = (names + (name,), {**first, name: start}, start + n_layers * a)
    return {b: (names, first, -(-rows // FLAT_ROW_BLOCK) * FLAT_ROW_BLOCK) for b, (names, first, rows) in groups.items()}


def _pack_rows(group, width, pieces, dtype):
    names, first, rows = group
    parts, at = [], 0
    for name in names:
        if first[name] > at:
            parts.append(jnp.zeros((first[name] - at, width), dtype))
        parts.append(pieces[name].astype(dtype))
        at = first[name] + pieces[name].shape[0]
    if rows > at:
        parts.append(jnp.zeros((rows - at, width), dtype))
    return jnp.concatenate(parts, axis=0)


def _slab(t, axis, j):
    n = t.shape[axis - 1] // N_CHIPS
    return lax.slice_in_dim(t, j * n, (j + 1) * n, axis=axis - 1)


def _layer_fwd(h0, mem_n, wl, dims):
    n_sb, n_fx, n_mem, sbw, fxw, memw = dims
    n1, gate1, up1, a1 = _ffn_fwd_up(h0, wl["ffn1_pre_g"], wl["ffn1_w_gate"], wl["ffn1_w_up"])
    h1, f1 = _ffn_fwd_down(a1, wl["ffn1_w_down"], h0, wl["ffn1_post_g"])

    u, proj, fl, sg = _mix_fwd_in(h1, wl["mix_pre_g"], wl["w_in"], wl["w_gate"], wl["b_gate"], wl["b_forget"])
    c = _fox_cumsum(fl)
    S = h0.shape[0]
    tc = _attn_blocks("fox", S, S)[1]
    ct = c[:, :n_fx].T
    ccol, crow = ct.reshape(n_fx, S, 1), ct.reshape(n_fx, S // tc, tc)
    qkv_sb = [(proj, k * sbw) for k in range(3)]
    qkv_fx = [(proj, 3 * sbw + k * fxw) for k in range(3)]
    kv = _matmul(mem_n, wl["w_mem_kv"], out_dtype=BF16, name="mem_kv")
    qkv_mem = [(proj, 3 * sbw + 3 * fxw), (kv, 0), (kv, memw)]
    o_sb, tot_sb = _attn_fwd("sb", *qkv_sb, n_sb, HEAD_DIM)
    o_fx, lse_fx = _attn_fwd("fox", *qkv_fx, n_fx, HEAD_DIM, ccol, crow)
    o_mem, lse_mem = _attn_fwd("mem", *qkv_mem, n_mem, MEM_HEAD_DIM)
    h2, zmix, merged = _mix_fwd_out(o_sb, o_fx, o_mem, sg, wl["w_br_sb"], wl["w_br_fox"], wl["w_br_mem"],
                                    wl["w_out"], h1, wl["mix_post_g"])

    n2, gate2, up2, a2 = _ffn_fwd_up(h2, wl["ffn2_pre_g"], wl["ffn2_w_gate"], wl["ffn2_w_up"])
    h3, f2 = _ffn_fwd_down(a2, wl["ffn2_w_down"], h2, wl["ffn2_post_g"])
    saved = dict(h0=h0, n1=n1, gate1=gate1, up1=up1, a1=a1, f1=f1, h1=h1, u=u, fl=fl, sg=sg,
                 qkv_sb=qkv_sb, qkv_fx=qkv_fx, qkv_mem=qkv_mem, ccol=ccol, crow=crow, o_sb=o_sb, o_fx=o_fx, o_mem=o_mem,
                 tot_sb=tot_sb, lse_fx=lse_fx, lse_mem=lse_mem,
                 zmix=zmix, merged=merged, h2=h2, n2=n2, gate2=gate2, up2=up2, a2=a2, f2=f2)
    return h3, saved


def _ffn_bwd(dh, sv, wl, tag, h_in):
    n, gate, up, a, f = (sv[k + tag] for k in ("n", "gate", "up", "a", "f"))
    pre = "ffn" + tag
    df, dgate, dup, dg_post = _ffn_bwd_down(dh, f, wl[pre + "_post_g"], wl[pre + "_w_down"], gate, up)
    dh_in, dg_pre = _ffn_bwd_up(dgate, dup, wl[pre + "_w_gate"], wl[pre + "_w_up"], h_in, wl[pre + "_pre_g"], dh)
    grads = {pre + "_post_g": dg_post, pre + "_pre_g": dg_pre,
             pre + "_w_down": _matmul(a, df, ta=True, name="dw_down"),
             pre + "_w_gate": _matmul(n, dgate, ta=True, name="dw_gate"),
             pre + "_w_up": _matmul(n, dup, ta=True, name="dw_up")}
    return dh_in, grads


def _layer_bwd(dh3, mem_n, wl, sv, dims):
    n_sb, n_fx, n_mem, sbw, fxw, memw = dims
    S = dh3.shape[0]
    dh2, grads = _ffn_bwd(dh3, sv, wl, "2", sv["h2"])

    (dz, db_sb, db_fx, db_mem, do_sb, do_fx, do_mem, dgp, db_gate, dg_post) = _mix_bwd_out(
        dh2, sv["zmix"], wl["mix_post_g"], wl["w_out"], sv["o_sb"], sv["o_fx"], sv["o_mem"],
        wl["w_br_sb"], wl["w_br_fox"], wl["w_br_mem"], sv["sg"])
    grads["mix_post_g"] = dg_post
    grads["b_gate"] = db_gate
    grads["w_out"] = _matmul(sv["merged"], dz, ta=True, name="dw_out")
    grads["w_br_sb"] = _matmul(sv["o_sb"], db_sb, ta=True, name="dw_br_sb")
    grads["w_br_fox"] = _matmul(sv["o_fx"], db_fx, ta=True, name="dw_br_fox")
    grads["w_br_mem"] = _matmul(sv["o_mem"], db_mem, ta=True, name="dw_br_mem")

    dq_sb, dk_sb, dv_sb = _attn_bwd("sb", *sv["qkv_sb"], sv["o_sb"], do_sb, n_sb, HEAD_DIM, lse=sv["tot_sb"])
    dq_fx, dk_fx, dv_fx, dcrow, dccol = _attn_bwd("fox", *sv["qkv_fx"], sv["o_fx"], do_fx, n_fx, HEAD_DIM,
                                                  sv["ccol"], sv["crow"], sv["lse_fx"])
    dq_mem, dk_mem, dv_mem = _attn_bwd("mem", *sv["qkv_mem"], sv["o_mem"], do_mem, n_mem, MEM_HEAD_DIM,
                                       lse=sv["lse_mem"])
    dkv = jnp.concatenate([dk_mem, dv_mem], axis=1)
    grads["w_mem_kv"] = _matmul(mem_n, dkv, ta=True, name="dw_mem_kv")
    dmem_n = _matmul(dkv, wl["w_mem_kv"], tb=True, out_dtype=F32, name="dmem_n")

    dc = jnp.pad((dcrow.reshape(n_fx, S) + dccol.reshape(n_fx, S)).T, ((0, 0), (0, LANE - n_fx)))
    dfl, db_forget = _fox_dlogit(dc, sv["fl"])
    grads["b_forget"] = db_forget
    dproj = jnp.concatenate([dq_sb, dk_sb, dv_sb, dq_fx, dk_fx, dv_fx, dq_mem, dfl], axis=1)
    dh1, dg_pre = _mix_bwd_in(dproj, dgp, wl["w_in"], wl["w_gate"], sv["h1"], wl["mix_pre_g"], dh2)
    grads["mix_pre_g"] = dg_pre
    grads["w_in"] = _matmul(sv["u"], dproj, ta=True, name="dw_in")
    grads["w_gate"] = _matmul(sv["u"], dgp, ta=True, name="dw_gate_mix")

    dh0, g1 = _ffn_bwd(dh1, sv, wl, "1", sv["h0"])
    grads.update(g1)
    return dh0, grads, dmem_n


def kernel(x, mem, ffn1_pre_g, ffn1_post_g, ffn1_w_gate, ffn1_w_up, ffn1_w_down, mix_pre_g, mix_post_g, w_in, b_forget, mem_norm_g, w_mem_kv, w_gate, b_gate, w_br_sb, w_br_fox, w_br_mem, w_out, ffn2_pre_g, ffn2_post_g, ffn2_w_gate, ffn2_w_up, ffn2_w_down, loss_target, m_ffn1_pre_g, m_ffn1_post_g, m_ffn1_w_gate, m_ffn1_w_up, m_ffn1_w_down, m_mix_pre_g, m_mix_post_g, m_w_in, m_b_forget, m_mem_norm_g, m_w_mem_kv, m_w_gate, m_b_gate, m_w_br_sb, m_w_br_fox, m_w_br_mem, m_w_out, m_ffn2_pre_g, m_ffn2_post_g, m_ffn2_w_gate, m_ffn2_w_up, m_ffn2_w_down, v_ffn1_pre_g, v_ffn1_post_g, v_ffn1_w_gate, v_ffn1_w_up, v_ffn1_w_down, v_mix_pre_g, v_mix_post_g, v_w_in, v_b_forget, v_mem_norm_g, v_w_mem_kv, v_w_gate, v_b_gate, v_w_br_sb, v_w_br_fox, v_w_br_mem, v_w_out, v_ffn2_pre_g, v_ffn2_post_g, v_ffn2_w_gate, v_ffn2_w_up, v_ffn2_w_down):
    args = dict(locals())
    w = {n: args[n] for n in WEIGHTS}
    m = {n: args["m_" + n] for n in WEIGHTS}
    v = {n: args["v_" + n] for n in WEIGHTS}
    L = w["ffn1_pre_g"].shape[0]
    Lh = L // 2
    D = x.shape[2]
    sbw, fxw, memw = w["w_br_sb"].shape[1], w["w_br_fox"].shape[1], w["w_br_mem"].shape[1]
    n_sb, n_fx, n_mem = sbw // HEAD_DIM, fxw // HEAD_DIM, memw // MEM_HEAD_DIM
    dims = (n_sb, n_fx, n_mem, sbw, fxw, memw)
    qkv_w = 3 * sbw + 3 * fxw
    c_idx = lax.axis_index("c")
    c_arr = c_idx.reshape(1).astype(jnp.int32)
    chip_arr = (2 * lax.axis_index("x") + lax.axis_index("y")).reshape(1).astype(jnp.int32)

    shard_shapes = {n: w[n].shape[1:] for n, _ in BIG}
    layout = _row_layout(shard_shapes, Lh)
    widths = list(layout)
    locs = [jnp.stack([_pack_rows(layout[b], b, {n: w[n][hf * Lh:(hf + 1) * Lh].reshape(-1, b) for n in layout[b][0]},
                                  BF16) for hf in range(2)]) for b in widths]
    gathered = dict(zip(widths, _gather_weights(locs, [_place_own(loc, chip_arr) for loc in locs])))

    def layer_weights(l):
        hf, li = divmod(l, Lh)
        wl = {}
        for n, axis in BIG:
            a, b = shard_shapes[n]
            r0 = layout[b][1][n] + li * a
            shards = gathered[b][:, hf, r0:r0 + a]
            wl[n] = shards.transpose(1, 0, 2).reshape(a, N_CHIPS * b) if axis == 2 else shards.reshape(N_CHIPS * a, b)
        wi = wl["w_in"]
        wl["w_in"] = jnp.concatenate([wi[:, :qkv_w], wi[:, qkv_w + n_fx:], wi[:, qkv_w:qkv_w + n_fx],
                                      jnp.zeros((D, LANE - n_fx), BF16)], axis=1)
        for n in SMALL:
            if n != "mem_norm_g":
                wl[n] = w[n][l][None, :]
        wl["b_forget"] = jnp.pad(wl["b_forget"], ((0, 0), (0, LANE - n_fx)))
        return wl

    g_mem = w["mem_norm_g"][None, :]

    mem_n = _mem_norm(mem[0], g_mem)
    h, wls, saved = x[0], [], []
    for l in range(L):
        wls.append(layer_weights(l))
        h, sv = _layer_fwd(h, mem_n, wls[l], dims)
        saved.append(sv)
    dh, loss_tile = _loss_head(h, loss_target[0])
    loss = lax.psum(loss_tile[0, 0], ("x", "y", "c"))
    gl, dmem_n = [None] * L, [None] * L
    for l in reversed(range(L)):
        dh, gl[l], dmem_n[l] = _layer_bwd(dh, mem_n, wls[l], saved[l], dims)
        gi = gl[l]["w_in"]
        gl[l]["w_in"] = jnp.concatenate([gi[:, :qkv_w], gi[:, qkv_w + memw:qkv_w + memw + n_fx],
                                         gi[:, qkv_w:qkv_w + memw]], axis=1)
    grad_x = dh
    g_mem_norm = _mem_norm_bwd(mem[0], g_mem, jnp.stack(dmem_n))

    axis_of = dict(BIG)
    partials = [jnp.stack([jnp.stack([
        _pack_rows(layout[b], b, {n: jnp.concatenate([_slab(gl[hf * Lh + li][n], axis_of[n], j) for li in range(Lh)])
                                  for n in layout[b][0]}, BF16)
        for j in range(N_CHIPS)]) for hf in range(2)]) for b in widths]
    pairs = [_pair_sum(g, sib, c_arr) for g, sib in zip(partials, _pair_exchange(partials))]
    mines = [_chip_sum(p, r, chip_arr) for p, r in zip(pairs, _chip_exchange(pairs))]
    theirs = _pair_swap(mines)

    grad, delta, new_m, new_v = {}, {}, {}, {}
    for n, _ in BIG:
        k = widths.index(shard_shapes[n][1])
        grad[n], delta[n], new_m[n], new_v[n] = _adamw_reduced(
            w[n], m[n], v[n], mines[k], theirs[k], c_arr, layout[widths[k]][1][n], name="adamw_" + n)

    small_local = {n: (g_mem_norm if n == "mem_norm_g" else
                       jnp.concatenate([gl[l][n][:, :n_fx] if n == "b_forget" else gl[l][n] for l in range(L)]))
                   for n in SMALL}
    small_shapes = [small_local[n].shape for n in SMALL]
    small_sum = _unpack_flat(_all_reduce_small(_pack_flat([small_local[n] for n in SMALL], F32, row_block=16)),
                             small_shapes)
    for n, t in zip(SMALL, small_sum):
        shp = w[n].shape
        two_d = (1, shp[0]) if len(shp) == 1 else shp
        grad[n] = t.reshape(shp)
        d_, m_, v_ = _adamw(w[n].reshape(two_d), t.reshape(two_d), m[n].reshape(two_d), v[n].reshape(two_d),
                            name="adamw_" + n)
        delta[n], new_m[n], new_v[n] = d_.reshape(shp), m_.reshape(shp), v_.reshape(shp)

    return (loss, grad_x[None], *[grad[n] for n in WEIGHTS], *[delta[n] for n in WEIGHTS],
            *[new_m[n] for n in WEIGHTS], *[new_v[n] for n in WEIGHTS])
```

```python
import math

import jax
import jax.numpy as jnp
from jax import lax
from jax.experimental import pallas as pl
from jax.experimental.pallas import tpu as pltpu

F32 = jnp.float32
BF16 = jnp.bfloat16
RMS_EPS = 1e-6
HEAD_DIM = 64
MEM_HEAD_DIM = 128
LANE = 128
V7X_VMEM_LIMIT_BYTES = 56 * 1024 * 1024
FLAT_COLS = 512
FLAT_UNIT = 16 * FLAT_COLS
FLAT_ROW_BLOCK = 512
N_CHIPS = 4
N_DEV = 8
NEG = float(jnp.finfo(jnp.float32).min)

ADAM_LR = 0.001
ADAM_B1 = 0.9
ADAM_B2 = 0.999
ADAM_EPS = 1e-08
ADAM_WD = 0.01
ADAM_STEP = 10

BIG = (("ffn1_w_gate", 2), ("ffn1_w_up", 2), ("ffn1_w_down", 1), ("w_in", 2), ("w_mem_kv", 1), ("w_gate", 2),
       ("w_br_sb", 2), ("w_br_fox", 2), ("w_br_mem", 2), ("w_out", 1),
       ("ffn2_w_gate", 2), ("ffn2_w_up", 2), ("ffn2_w_down", 1))
SMALL = ("ffn1_pre_g", "ffn1_post_g", "mix_pre_g", "mix_post_g", "b_forget", "mem_norm_g", "b_gate",
         "ffn2_pre_g", "ffn2_post_g")
WEIGHTS = ("ffn1_pre_g", "ffn1_post_g", "ffn1_w_gate", "ffn1_w_up", "ffn1_w_down", "mix_pre_g", "mix_post_g", "w_in",
           "b_forget", "mem_norm_g", "w_mem_kv", "w_gate", "b_gate", "w_br_sb", "w_br_fox", "w_br_mem", "w_out",
           "ffn2_pre_g", "ffn2_post_g", "ffn2_w_gate", "ffn2_w_up", "ffn2_w_down")


def _params(**kw):
    return pltpu.CompilerParams(vmem_limit_bytes=V7X_VMEM_LIMIT_BYTES, **kw)


def _dot(a, b):
    return jnp.dot(a, b, preferred_element_type=F32)


def _dot_nt(a, b):
    return lax.dot_general(a, b, (((1,), (1,)), ((), ())), preferred_element_type=F32)


def _dot_tn(a, b):
    return lax.dot_general(a, b, (((0,), (0,)), ((), ())), preferred_element_type=F32)


def _rms(t, g):
    return t * lax.rsqrt(jnp.mean(t * t, axis=-1, keepdims=True) + RMS_EPS) * g


def _pick(dim, pref):
    if dim <= pref:
        return dim
    for cand in range(pref - pref % LANE, 0, -LANE):
        if dim % cand == 0:
            return cand
    return dim


def _rows(bm, cols):
    return pl.BlockSpec((bm, cols), lambda i: (i, 0))


def _whole(shape):
    nd = len(shape)
    return pl.BlockSpec(shape, lambda i: (0,) * nd)


def _split3(x):
    hi = x.astype(BF16)
    r1 = x - hi.astype(F32)
    mid = r1.astype(BF16)
    lo = (r1 - mid.astype(F32)).astype(BF16)
    return hi, mid, lo


def _cumdot(x, tri):
    hi = x.astype(BF16)
    lo = (x - hi.astype(F32)).astype(BF16)
    return _dot(hi, tri) + _dot(lo, tri)


def _ffn_fwd_up(h, g_pre, wg, wu):
    S, D = h.shape
    F = wg.shape[1]
    bm = _pick(S, 256)

    def body(h_ref, g_ref, wg_ref, wu_ref, n_ref, gate_ref, up_ref, a_ref):
        n = _rms(h_ref[...], g_ref[...]).astype(BF16)
        n_ref[...] = n
        gate = _dot(n, wg_ref[...])
        up = _dot(n, wu_ref[...])
        gate_ref[...] = gate.astype(BF16)
        up_ref[...] = up.astype(BF16)
        a_ref[...] = (gate * jax.nn.sigmoid(gate) * up).astype(BF16)

    return pl.pallas_call(
        body, name="ffn_fwd_up", grid=(S // bm,),
        in_specs=[_rows(bm, D), _whole((1, D)), _whole((D, F)), _whole((D, F))],
        out_specs=[_rows(bm, D), _rows(bm, F), _rows(bm, F), _rows(bm, F)],
        out_shape=[jax.ShapeDtypeStruct((S, D), BF16)] + [jax.ShapeDtypeStruct((S, F), BF16)] * 3,
        compiler_params=_params(),
    )(h, g_pre, wg, wu)


def _ffn_fwd_down(a, wd, h, g_post):
    S, F = a.shape
    D = wd.shape[1]
    bm = _pick(S, 256)

    def body(a_ref, wd_ref, h_ref, g_ref, hout_ref, f_ref):
        f = _dot(a_ref[...], wd_ref[...])
        f_ref[...] = f
        hout_ref[...] = h_ref[...] + 0.5 * _rms(f, g_ref[...])

    return pl.pallas_call(
        body, name="ffn_fwd_down", grid=(S // bm,),
        in_specs=[_rows(bm, F), _whole((F, D)), _rows(bm, D), _whole((1, D))],
        out_specs=[_rows(bm, D), _rows(bm, D)],
        out_shape=[jax.ShapeDtypeStruct((S, D), F32)] * 2,
        compiler_params=_params(),
    )(a, wd, h, g_post)


def _ffn_bwd_down(dh, f, g_post, wd, gate, up):
    S, D = dh.shape
    F = wd.shape[0]
    bm = _pick(S, 256)

    def body(dh_ref, f_ref, g_ref, wd_ref, gate_ref, up_ref, df_ref, dgate_ref, dup_ref, dg_ref):
        _, vjp = jax.vjp(lambda t, g: 0.5 * _rms(t, g), f_ref[...], g_ref[...])
        df, dg = vjp(dh_ref[...])

        @pl.when(pl.program_id(0) == 0)
        def _():
            dg_ref[...] = jnp.zeros_like(dg_ref)

        dg_ref[...] += dg
        dfb = df.astype(BF16)
        df_ref[...] = dfb
        da = _dot_nt(dfb, wd_ref[...])
        gt = gate_ref[...].astype(F32)
        sig = jax.nn.sigmoid(gt)
        silu = gt * sig
        dup_ref[...] = (da * silu).astype(BF16)
        dgate_ref[...] = (da * up_ref[...].astype(F32) * (sig + silu * (1.0 - sig))).astype(BF16)

    return pl.pallas_call(
        body, name="ffn_bwd_down", grid=(S // bm,),
        in_specs=[_rows(bm, D), _rows(bm, D), _whole((1, D)), _whole((F, D)), _rows(bm, F), _rows(bm, F)],
        out_specs=[_rows(bm, D), _rows(bm, F), _rows(bm, F), _whole((1, D))],
        out_shape=[jax.ShapeDtypeStruct((S, D), BF16), jax.ShapeDtypeStruct((S, F), BF16),
                   jax.ShapeDtypeStruct((S, F), BF16), jax.ShapeDtypeStruct((1, D), F32)],
        compiler_params=_params(),
    )(dh, f, g_post, wd, gate, up)


def _ffn_bwd_up(dgate, dup, wg, wu, h_in, g_pre, dh):
    S, F = dgate.shape
    D = wg.shape[0]
    bm = _pick(S, 256)

    def body(dgate_ref, dup_ref, wg_ref, wu_ref, h_ref, g_ref, dh_ref, dhin_ref, dg_ref):
        dn = _dot_nt(dgate_ref[...], wg_ref[...]) + _dot_nt(dup_ref[...], wu_ref[...])
        _, vjp = jax.vjp(_rms, h_ref[...], g_ref[...])
        dhx, dg = vjp(dn)

        @pl.when(pl.program_id(0) == 0)
        def _():
            dg_ref[...] = jnp.zeros_like(dg_ref)

        dg_ref[...] += dg
        dhin_ref[...] = dh_ref[...] + dhx

    return pl.pallas_call(
        body, name="ffn_bwd_up", grid=(S // bm,),
        in_specs=[_rows(bm, F), _rows(bm, F), _whole((D, F)), _whole((D, F)), _rows(bm, D), _whole((1, D)),
                  _rows(bm, D)],
        out_specs=[_rows(bm, D), _whole((1, D))],
        out_shape=[jax.ShapeDtypeStruct((S, D), F32), jax.ShapeDtypeStruct((1, D), F32)],
        compiler_params=_params(),
    )(dgate, dup, wg, wu, h_in, g_pre, dh)


def _matmul(a, b, *, ta=False, tb=False, out_dtype=BF16, name):
    M, K = (a.shape[1], a.shape[0]) if ta else a.shape
    N = b.shape[0] if tb else b.shape[1]
    acc_budget = 12 * 1024 * 1024
    bm, bk = _pick(M, 1536), _pick(K, 512)
    while N * bm * 4 > acc_budget and bm % (2 * LANE) == 0:
        bm //= 2
    bn = N if N * bm * 4 <= acc_budget else _pick(N, 1536)
    nk = K // bk

    def body(a_ref, b_ref, o_ref, acc_ref):
        kk = pl.program_id(2)

        @pl.when(kk == 0)
        def _():
            acc_ref[...] = jnp.zeros_like(acc_ref)

        av, bv = a_ref[...], b_ref[...]
        dims = (((0 if ta else 1,), (1 if tb else 0,)), ((), ()))
        acc_ref[...] += lax.dot_general(av, bv, dims, preferred_element_type=F32)

        @pl.when(kk == nk - 1)
        def _():
            o_ref[...] = acc_ref[...].astype(o_ref.dtype)

    a_spec = pl.BlockSpec((bk, bm), lambda i, j, k: (k, i)) if ta else pl.BlockSpec((bm, bk), lambda i, j, k: (i, k))
    b_spec = pl.BlockSpec((bn, bk), lambda i, j, k: (j, k)) if tb else pl.BlockSpec((bk, bn), lambda i, j, k: (k, j))
    return pl.pallas_call(
        body, name=name, grid=(M // bm, N // bn, nk),
        in_specs=[a_spec, b_spec],
        out_specs=pl.BlockSpec((bm, bn), lambda i, j, k: (i, j)),
        out_shape=jax.ShapeDtypeStruct((M, N), out_dtype),
        scratch_shapes=[pltpu.VMEM((bm, bn), F32)],
        compiler_params=_params(),
    )(a, b)


def _mix_fwd_in(h, g_pre, win, wgate, b_gate, b_forget):
    S, D = h.shape
    PW = win.shape[1] - LANE
    G = wgate.shape[1]
    bm = _pick(S, 256)

    def body(h_ref, g_ref, win_ref, wgate_ref, bg_ref, bf_ref, u_ref, proj_ref, fl_ref, sg_ref):
        u = _rms(h_ref[...], g_ref[...]).astype(BF16)
        u_ref[...] = u
        proj = _dot(u, win_ref[...])
        proj_ref[...] = proj[:, :PW].astype(BF16)
        fl_ref[...] = proj[:, PW:] + bf_ref[...]
        sg_ref[...] = jax.nn.sigmoid(_dot(u, wgate_ref[...]) + bg_ref[...]).astype(BF16)

    return pl.pallas_call(
        body, name="mix_fwd_in", grid=(S // bm,),
        in_specs=[_rows(bm, D), _whole((1, D)), _whole((D, PW + LANE)), _whole((D, G)), _whole((1, G)),
                  _whole((1, LANE))],
        out_specs=[_rows(bm, D), _rows(bm, PW), _rows(bm, LANE), _rows(bm, G)],
        out_shape=[jax.ShapeDtypeStruct((S, D), BF16), jax.ShapeDtypeStruct((S, PW), BF16),
                   jax.ShapeDtypeStruct((S, LANE), F32), jax.ShapeDtypeStruct((S, G), BF16)],
        compiler_params=_params(),
    )(h, g_pre, win, wgate, b_gate, b_forget)


def _mix_fwd_out(o_sb, o_fx, o_mem, sg, w_sb, w_fx, w_mem, w_out, h, g_post):
    S, D = h.shape
    bm = _pick(S, 256)
    widths = (o_sb.shape[1], o_fx.shape[1], o_mem.shape[1])

    def body(osb_ref, ofx_ref, omem_ref, sg_ref, wsb_ref, wfx_ref, wmem_ref, wout_ref, h_ref, g_ref,
             hout_ref, z_ref, merged_ref):
        s = sg_ref[...].astype(F32)
        merged = (s[:, :D] * _dot(osb_ref[...], wsb_ref[...]) + s[:, D:2 * D] * _dot(ofx_ref[...], wfx_ref[...])
                  + s[:, 2 * D:] * _dot(omem_ref[...], wmem_ref[...]))
        mb = merged.astype(BF16)
        merged_ref[...] = mb
        z = _dot(mb, wout_ref[...])
        z_ref[...] = z
        hout_ref[...] = h_ref[...] + _rms(z, g_ref[...])

    return pl.pallas_call(
        body, name="mix_fwd_out", grid=(S // bm,),
        in_specs=[_rows(bm, widths[0]), _rows(bm, widths[1]), _rows(bm, widths[2]), _rows(bm, 3 * D),
                  _whole((widths[0], D)), _whole((widths[1], D)), _whole((widths[2], D)), _whole((D, D)),
                  _rows(bm, D), _whole((1, D))],
        out_specs=[_rows(bm, D), _rows(bm, D), _rows(bm, D)],
        out_shape=[jax.ShapeDtypeStruct((S, D), F32), jax.ShapeDtypeStruct((S, D), F32),
                   jax.ShapeDtypeStruct((S, D), BF16)],
        compiler_params=_params(),
    )(o_sb, o_fx, o_mem, sg, w_sb, w_fx, w_mem, w_out, h, g_post)


def _mix_bwd_out(dh, z, g_post, w_out, o_sb, o_fx, o_mem, w_sb, w_fx, w_mem, sg):
    S, D = dh.shape
    bm = _pick(S, 256)
    widths = (o_sb.shape[1], o_fx.shape[1], o_mem.shape[1])

    def body(dh_ref, z_ref, g_ref, wout_ref, osb_ref, ofx_ref, omem_ref, wsb_ref, wfx_ref, wmem_ref, sg_ref,
             dz_ref, dbsb_ref, dbfx_ref, dbmem_ref, dosb_ref, dofx_ref, domem_ref, dgp_ref, dbg_ref, dg_ref):
        _, vjp = jax.vjp(_rms, z_ref[...], g_ref[...])
        dz, dg = vjp(dh_ref[...])

        @pl.when(pl.program_id(0) == 0)
        def _():
            dg_ref[...] = jnp.zeros_like(dg_ref)
            dbg_ref[...] = jnp.zeros_like(dbg_ref)

        dg_ref[...] += dg
        dzb = dz.astype(BF16)
        dz_ref[...] = dzb
        dmerged = _dot_nt(dzb, wout_ref[...])
        s = sg_ref[...].astype(F32)
        branches = ((osb_ref, wsb_ref, dbsb_ref, dosb_ref), (ofx_ref, wfx_ref, dbfx_ref, dofx_ref),
                    (omem_ref, wmem_ref, dbmem_ref, domem_ref))
        for k, (o_ref, w_ref, db_ref, do_ref) in enumerate(branches):
            gs = s[:, k * D:(k + 1) * D]
            dbb = (dmerged * gs).astype(BF16)
            db_ref[...] = dbb
            do_ref[...] = _dot_nt(dbb, w_ref[...]).astype(BF16)
            dgp = dmerged * _dot(o_ref[...], w_ref[...]) * gs * (1.0 - gs)
            dgp_ref[:, k * D:(k + 1) * D] = dgp.astype(BF16)
            dbg_ref[:, k * D:(k + 1) * D] += jnp.sum(dgp, axis=0, keepdims=True)

    return pl.pallas_call(
        body, name="mix_bwd_out", grid=(S // bm,),
        in_specs=[_rows(bm, D), _rows(bm, D), _whole((1, D)), _whole((D, D)),
                  _rows(bm, widths[0]), _rows(bm, widths[1]), _rows(bm, widths[2]),
                  _whole((widths[0], D)), _whole((widths[1], D)), _whole((widths[2], D)), _rows(bm, 3 * D)],
        out_specs=[_rows(bm, D)] * 4 + [_rows(bm, widths[0]), _rows(bm, widths[1]), _rows(bm, widths[2]),
                                        _rows(bm, 3 * D), _whole((1, 3 * D)), _whole((1, D))],
        out_shape=[jax.ShapeDtypeStruct((S, D), BF16)] * 4
        + [jax.ShapeDtypeStruct((S, w), BF16) for w in widths]
        + [jax.ShapeDtypeStruct((S, 3 * D), BF16), jax.ShapeDtypeStruct((1, 3 * D), F32),
           jax.ShapeDtypeStruct((1, D), F32)],
        compiler_params=_params(),
    )(dh, z, g_post, w_out, o_sb, o_fx, o_mem, w_sb, w_fx, w_mem, sg)


def _mix_bwd_in(dproj, dgp, win, wgate, h_in, g_pre, dh):
    S, PWL = dproj.shape
    G = dgp.shape[1]
    D = h_in.shape[1]
    bm = _pick(S, 256)

    def body(dproj_ref, dgp_ref, win_ref, wgate_ref, h_ref, g_ref, dh_ref, dhin_ref, dg_ref):
        du = _dot_nt(dproj_ref[...], win_ref[...]) + _dot_nt(dgp_ref[...], wgate_ref[...])
        _, vjp = jax.vjp(_rms, h_ref[...], g_ref[...])
        dhx, dg = vjp(du)

        @pl.when(pl.program_id(0) == 0)
        def _():
            dg_ref[...] = jnp.zeros_like(dg_ref)

        dg_ref[...] += dg
        dhin_ref[...] = dh_ref[...] + dhx

    return pl.pallas_call(
        body, name="mix_bwd_in", grid=(S // bm,),
        in_specs=[_rows(bm, PWL), _rows(bm, G), _whole((D, PWL)), _whole((D, G)), _rows(bm, D), _whole((1, D)),
                  _rows(bm, D)],
        out_specs=[_rows(bm, D), _whole((1, D))],
        out_shape=[jax.ShapeDtypeStruct((S, D), F32), jax.ShapeDtypeStruct((1, D), F32)],
        compiler_params=_params(),
    )(dproj, dgp, win, wgate, h_in, g_pre, dh)


def _log_sigmoid(x):
    return jnp.minimum(x, 0.0) - jnp.log(1.0 + jnp.exp(-jnp.abs(x)))


def _fox_cumsum(fl):
    S = fl.shape[0]
    rb = _pick(S, LANE)

    def body(fl_ref, c_ref, carry_ref):
        @pl.when(pl.program_id(0) == 0)
        def _():
            carry_ref[...] = jnp.zeros_like(carry_ref)

        r = lax.broadcasted_iota(jnp.int32, (rb, rb), 0)
        cidx = lax.broadcasted_iota(jnp.int32, (rb, rb), 1)
        tri = (cidx <= r).astype(BF16)
        hi, mid, lo = _split3(_log_sigmoid(fl_ref[...]))
        c = _dot(tri, hi) + _dot(tri, mid) + _dot(tri, lo) + carry_ref[...]
        c_ref[...] = c
        carry_ref[...] = c[rb - 1:rb, :]

    return pl.pallas_call(
        body, name="fox_cumsum", grid=(S // rb,),
        in_specs=[_rows(rb, LANE)], out_specs=_rows(rb, LANE),
        out_shape=jax.ShapeDtypeStruct((S, LANE), F32),
        scratch_shapes=[pltpu.VMEM((1, LANE), F32)],
        compiler_params=_params(),
    )(fl)


def _fox_dlogit(dc, fl):
    S = fl.shape[0]
    rb = _pick(S, LANE)
    nb = S // rb

    def body(dc_ref, fl_ref, dfl_ref, dbf_ref, carry_ref):
        @pl.when(pl.program_id(0) == 0)
        def _():
            carry_ref[...] = jnp.zeros_like(carry_ref)
            dbf_ref[...] = jnp.zeros_like(dbf_ref)

        r = lax.broadcasted_iota(jnp.int32, (rb, rb), 0)
        cidx = lax.broadcasted_iota(jnp.int32, (rb, rb), 1)
        tri = (cidx >= r).astype(BF16)
        hi, mid, lo = _split3(dc_ref[...])
        rc = _dot(tri, hi) + _dot(tri, mid) + _dot(tri, lo) + carry_ref[...]
        carry_ref[...] = rc[0:1, :]
        dfl = rc * jax.nn.sigmoid(-fl_ref[...])
        dfl_ref[...] = dfl.astype(BF16)
        dbf_ref[...] += jnp.sum(dfl, axis=0, keepdims=True)

    rev = pl.BlockSpec((rb, LANE), lambda i: (nb - 1 - i, 0))
    return pl.pallas_call(
        body, name="fox_dlogit", grid=(nb,),
        in_specs=[rev, rev], out_specs=[rev, _whole((1, LANE))],
        out_shape=[jax.ShapeDtypeStruct((S, LANE), BF16), jax.ShapeDtypeStruct((1, LANE), F32)],
        scratch_shapes=[pltpu.VMEM((1, LANE), F32)],
        compiler_params=_params(),
    )(dc, fl)


def _attn_blocks(kind, S, Sk):
    tq = _pick(S, 1024)
    tc = LANE if kind == "sb" else _pick(Sk, 256)
    return tq, tc


def _is_power_of_two(x):
    return math.frexp(x)[0] == 0.5


def _sb_logs(z):
    ln = -jnp.maximum(z, 0.0) - jnp.log(1.0 + jnp.exp(-jnp.abs(z)))
    return ln + z, ln


def _head_lanes(pack, dh):
    lane = lax.broadcasted_iota(jnp.int32, (1, LANE), 1)
    return [(lane >= hh * dh) & (lane < (hh + 1) * dh) for hh in range(pack)]


def _by_head(sel, parts):
    out = parts[0]
    for hh in range(1, len(parts)):
        out = jnp.where(sel[hh], parts[hh], out)
    return out


def _only_head(sel, hh, x):
    return x if len(sel) == 1 else jnp.where(sel[hh], x, jnp.zeros_like(x))


def _tail(x, r0):
    return x if not r0 else x[r0:]


def _put_tail(x, tail, r0):
    return tail if not r0 else jnp.concatenate([x[:r0], tail], axis=0)


def _add_tail(x, tail, r0):
    return x + tail if not r0 else jnp.concatenate([x[:r0], x[r0:] + tail], axis=0)


def _q_cols(tq, first):
    return pl.BlockSpec((tq, LANE), lambda g, i: (i, first // LANE + g))


def _k_cols(rows, first):
    return pl.BlockSpec((rows, LANE), lambda g, i: (0, first // LANE + g))


def _attn_fwd(kind, q, k, v, n_heads, dh, ccol=None, crow=None):
    (qa, q0), (ka, k0), (va, v0) = q, k, v
    S, Sk = qa.shape[0], ka.shape[0]
    pack = LANE // dh
    tq, tc = _attn_blocks(kind, S, Sk)
    scale = dh ** -0.5
    fold = _is_power_of_two(scale)
    causal = kind != "mem"
    n_diag = tq // tc if causal else 0
    unroll = 2 if causal else 1
    assert n_diag % unroll == 0 and (Sk // tc) % unroll == 0

    def body(*refs):
        if kind == "fox":
            q_ref, k_ref, v_ref, cc_ref, cr_ref, o_ref, lse_ref = refs
        else:
            q_ref, k_ref, v_ref, o_ref, lse_ref = refs
        i = pl.program_id(1)
        n_full = (i * tq) // tc if causal else Sk // tc
        qpos = i * tq + lax.broadcasted_iota(jnp.int32, (tq, tc), 0)
        kio = lax.broadcasted_iota(jnp.int32, (tq, tc), 1)
        heads = range(pack)
        sel = _head_lanes(pack, dh)
        q2 = q_ref[...] * scale if fold else q_ref[...]
        qs = [_only_head(sel, hh, q2) for hh in heads]

        def kv(jc):
            off = pl.multiple_of(jc * tc, tc)
            return off, k_ref[pl.ds(off, tc), :], v_ref[pl.ds(off, tc), :]

        if kind == "sb":
            tri = (lax.broadcasted_iota(jnp.int32, (tc, tc), 0) > lax.broadcasted_iota(jnp.int32, (tc, tc), 1)
                   ).astype(BF16)

            def chunk(jc, r0, runs, acc):
                off, k2, v2 = kv(jc)
                new_runs, pv = [], []
                for hh in heads:
                    lb, ln = _sb_logs(_dot_nt(_tail(qs[hh], r0), k2))
                    if r0 is not None:
                        mask = (off + _tail(kio, r0)) < _tail(qpos, r0)
                        ln = jnp.where(mask, ln, 0.0)
                    w = jnp.exp(lb + _cumdot(ln, tri) + _tail(runs[hh], r0))
                    if r0 is not None:
                        w = jnp.where(mask, w, 0.0)
                    pv.append(_dot(w.astype(BF16), v2))
                    new_runs.append(_add_tail(runs[hh], jnp.sum(ln, axis=1, keepdims=True), r0))
                return tuple(new_runs), _add_tail(acc, _by_head(sel, pv), r0)

            state = (tuple(jnp.zeros((tq, 1), F32) for _ in heads), jnp.zeros((tq, LANE), F32))
            for d in range(n_diag - 1, -1, -1):
                state = chunk(n_full + d, d * tc, *state)

            def trip(t, st):
                for u in range(unroll):
                    st = chunk(n_full - 1 - unroll * t - u, None, *st)
                return st

            runs, acc = lax.fori_loop(0, n_full // unroll, trip, state)
            o_ref[...] = acc.astype(o_ref.dtype)
            for hh in heads:
                lse_ref[hh] = runs[hh]
        else:
            def chunk(jc, r0, ms, ls, acc):
                off, k2, v2 = kv(jc)
                new_ms, new_ls, alphas, pv = [], [], [], []
                for hh in heads:
                    z = _dot_nt(_tail(qs[hh], r0), k2)
                    if not fold:
                        z = z * scale
                    if kind == "fox":
                        z = z + _tail(cc_ref[hh], r0) - cr_ref[hh, pl.ds(jc, 1), :]
                    if r0 is not None:
                        z = jnp.where((off + _tail(kio, r0)) <= _tail(qpos, r0), z, NEG)
                    m_old, l_old = _tail(ms[hh], r0), _tail(ls[hh], r0)
                    m_new = jnp.maximum(m_old, jnp.max(z, axis=1, keepdims=True))
                    alpha = jnp.exp(m_old - m_new)
                    p = jnp.exp(z - m_new)
                    new_ms.append(_put_tail(ms[hh], m_new, r0))
                    new_ls.append(_put_tail(ls[hh], alpha * l_old + jnp.sum(p, axis=1, keepdims=True), r0))
                    alphas.append(alpha)
                    pv.append(_dot(p.astype(BF16), v2))
                acc_new = _by_head(sel, alphas) * _tail(acc, r0) + _by_head(sel, pv)
                return tuple(new_ms), tuple(new_ls), _put_tail(acc, acc_new, r0)

            state = (tuple(jnp.full((tq, 1), NEG, F32) for _ in heads), tuple(jnp.zeros((tq, 1), F32) for _ in heads),
                     jnp.zeros((tq, LANE), F32))

            def trip(t, st):
                for u in range(unroll):
                    st = chunk(unroll * t + u, None, *st)
                return st

            state = lax.fori_loop(0, n_full // unroll, trip, state)
            for d in range(n_diag):
                state = chunk(n_full + d, d * tc, *state)
            ms, ls, acc = state
            o_ref[...] = (acc / _by_head(sel, ls)).astype(o_ref.dtype)
            for hh in heads:
                lse_ref[hh] = ms[hh] + jnp.log(ls[hh])

    colspec = pl.BlockSpec((pack, tq, 1), lambda g, i: (g, i, 0))
    in_specs, args = [_q_cols(tq, q0), _k_cols(Sk, k0), _k_cols(Sk, v0)], [qa, ka, va]
    if kind == "fox":
        in_specs += [colspec, pl.BlockSpec((pack, Sk // tc, tc), lambda g, i: (g, 0, 0))]
        args += [ccol, crow]
    return pl.pallas_call(
        body, name="attn_fwd_" + kind, grid=(n_heads // pack, S // tq),
        in_specs=in_specs, out_specs=[_q_cols(tq, 0), colspec],
        out_shape=[jax.ShapeDtypeStruct((S, n_heads * dh), BF16), jax.ShapeDtypeStruct((n_heads, S, 1), F32)],
        compiler_params=_params(),
    )(*args)


def _attn_bwd(kind, q, k, v, o, do, n_heads, dh, ccol=None, crow=None, lse=None):
    (qa, q0), (ka, k0), (va, v0) = q, k, v
    S, Sk = qa.shape[0], ka.shape[0]
    pack = LANE // dh
    tq, tc = _attn_blocks(kind, S, Sk)
    scale = dh ** -0.5
    fold = _is_power_of_two(scale)
    causal = kind != "mem"
    n_diag = tq // tc if causal else 0
    unroll = 2 if kind == "sb" else 1
    assert n_diag % unroll == 0 and (Sk // tc) % unroll == 0
    nq = S // tq

    def body(*refs):
        if kind == "fox":
            (q_ref, k_ref, v_ref, o_ref, do_ref, cc_ref, cr_ref, lse_ref,
             dq_ref, dk_ref, dv_ref, dc_ref, dcc_ref, dk_acc, dv_acc, dc_acc) = refs
        else:
            q_ref, k_ref, v_ref, o_ref, do_ref, lse_ref, dq_ref, dk_ref, dv_ref, dk_acc, dv_acc = refs
        i = pl.program_id(1)

        @pl.when(i == 0)
        def _():
            dk_acc[...] = jnp.zeros_like(dk_acc)
            dv_acc[...] = jnp.zeros_like(dv_acc)
            if kind == "fox":
                dc_acc[...] = jnp.zeros_like(dc_acc)

        n_full = (i * tq) // tc if causal else Sk // tc
        qpos = i * tq + lax.broadcasted_iota(jnp.int32, (tq, tc), 0)
        kio = lax.broadcasted_iota(jnp.int32, (tq, tc), 1)
        heads = range(pack)
        sel = _head_lanes(pack, dh)
        q2 = q_ref[...] * scale if fold else q_ref[...]
        do2 = do_ref[...]
        qs = [_only_head(sel, hh, q2) for hh in heads]
        dos = [_only_head(sel, hh, do2) for hh in heads]

        def kv(jc):
            off = pl.multiple_of(jc * tc, tc)
            return off, k_ref[pl.ds(off, tc), :], v_ref[pl.ds(off, tc), :]

        def accumulate(off, k2, dzb, wb, dq, r0):
            q2t, do2t = _tail(q2, r0), _tail(do2, r0)
            dk_acc[pl.ds(off, tc), :] += _by_head(sel, [_dot_tn(dzb[hh], q2t) for hh in heads])
            dv_acc[pl.ds(off, tc), :] += _by_head(sel, [_dot_tn(wb[hh], do2t) for hh in heads])
            return _add_tail(dq, _by_head(sel, [_dot(dzb[hh], k2) for hh in heads]), r0)

        if kind == "sb":
            r = lax.broadcasted_iota(jnp.int32, (tc, tc), 0)
            cidx = lax.broadcasted_iota(jnp.int32, (tc, tc), 1)
            tri_inc = (r <= cidx).astype(BF16)
            tri_exc = (r < cidx).astype(BF16)

            def chunk(jc, r0, pres, pres_e, dq):
                off, k2, v2 = kv(jc)
                new_pres, new_pres_e, dzb, wb = [], [], [], []
                for hh in heads:
                    lb, ln = _sb_logs(_dot_nt(_tail(qs[hh], r0), k2))
                    if r0 is not None:
                        mask = (off + _tail(kio, r0)) < _tail(qpos, r0)
                        ln = jnp.where(mask, ln, 0.0)
                    w = jnp.exp(lb + (_tail(lse_ref[hh], r0) - _tail(pres[hh], r0) - _cumdot(ln, tri_inc)))
                    if r0 is not None:
                        w = jnp.where(mask, w, 0.0)
                    e = w * _dot_nt(_tail(dos[hh], r0), v2)
                    beta = jnp.exp(lb)
                    dz = e * (1.0 - beta) - beta * (_tail(pres_e[hh], r0) + _cumdot(e, tri_exc))
                    if r0 is not None:
                        dz = jnp.where(mask, dz, 0.0)
                    dzb.append(dz.astype(BF16))
                    wb.append(w.astype(BF16))
                    new_pres.append(_add_tail(pres[hh], jnp.sum(ln, axis=1, keepdims=True), r0))
                    new_pres_e.append(_add_tail(pres_e[hh], jnp.sum(e, axis=1, keepdims=True), r0))
                return tuple(new_pres), tuple(new_pres_e), accumulate(off, k2, dzb, wb, dq, r0)

            state = (tuple(jnp.zeros((tq, 1), F32) for _ in heads), tuple(jnp.zeros((tq, 1), F32) for _ in heads),
                     jnp.zeros((tq, LANE), F32))
        else:
            prod = o_ref[...].astype(F32) * do2.astype(F32)
            dsum = [jnp.sum(_only_head(sel, hh, prod), axis=1, keepdims=True) for hh in heads]

            def chunk(jc, r0, rowsums, dq):
                off, k2, v2 = kv(jc)
                new_rowsums, dsb, pb = [], [], []
                for hh in heads:
                    z = _dot_nt(_tail(qs[hh], r0), k2)
                    if not fold:
                        z = z * scale
                    if kind == "fox":
                        z = z + _tail(cc_ref[hh], r0) - cr_ref[hh, pl.ds(jc, 1), :]
                    if r0 is not None:
                        z = jnp.where((off + _tail(kio, r0)) <= _tail(qpos, r0), z, NEG)
                    p = jnp.exp(z - _tail(lse_ref[hh], r0))
                    ds = p * (_dot_nt(_tail(dos[hh], r0), v2) - _tail(dsum[hh], r0))
                    dsb.append(ds.astype(BF16))
                    pb.append(p.astype(BF16))
                    if kind == "fox":
                        dc_acc[hh, pl.ds(jc, 1), :] -= jnp.sum(ds, axis=0, keepdims=True)
                        new_rowsums.append(_add_tail(rowsums[hh], jnp.sum(ds, axis=1, keepdims=True), r0))
                    else:
                        new_rowsums.append(rowsums[hh])
                return tuple(new_rowsums), accumulate(off, k2, dsb, pb, dq, r0)

            state = (tuple(jnp.zeros((tq, 1), F32) for _ in heads), jnp.zeros((tq, LANE), F32))

        def trip(t, st):
            for u in range(unroll):
                st = chunk(unroll * t + u, None, *st)
            return st

        state = lax.fori_loop(0, n_full // unroll, trip, state)
        for d in range(n_diag):
            state = chunk(n_full + d, d * tc, *state)
        dq_ref[...] = (state[-1] * scale).astype(dq_ref.dtype)
        if kind == "fox":
            for hh in heads:
                dcc_ref[hh] = state[0][hh]

        @pl.when(i == nq - 1)
        def _():
            dk = dk_acc[...] if fold else dk_acc[...] * scale
            dk_ref[...] = dk.astype(dk_ref.dtype)
            dv_ref[...] = dv_acc[...].astype(dv_ref.dtype)
            if kind == "fox":
                dc_ref[...] = dc_acc[...]

    colspec = pl.BlockSpec((pack, tq, 1), lambda g, i: (g, i, 0))
    rowspec = pl.BlockSpec((pack, Sk // tc, tc), lambda g, i: (g, 0, 0))
    in_specs = [_q_cols(tq, q0), _k_cols(Sk, k0), _k_cols(Sk, v0), _q_cols(tq, 0), _q_cols(tq, 0)]
    args = [qa, ka, va, o, do]
    if kind == "fox":
        in_specs += [colspec, rowspec]
        args += [ccol, crow]
    in_specs += [colspec]
    args += [lse]
    width = n_heads * dh
    out_specs = [_q_cols(tq, 0), _k_cols(Sk, 0), _k_cols(Sk, 0)]
    out_shape = [jax.ShapeDtypeStruct((S, width), BF16), jax.ShapeDtypeStruct((Sk, width), BF16),
                 jax.ShapeDtypeStruct((Sk, width), BF16)]
    scratch = [pltpu.VMEM((Sk, LANE), F32), pltpu.VMEM((Sk, LANE), F32)]
    if kind == "fox":
        out_specs += [rowspec, colspec]
        out_shape += [jax.ShapeDtypeStruct((n_heads, Sk // tc, tc), F32), jax.ShapeDtypeStruct((n_heads, S, 1), F32)]
        scratch.append(pltpu.VMEM((pack, Sk // tc, tc), F32))
    return pl.pallas_call(
        body, name="attn_bwd_" + kind, grid=(n_heads // pack, nq),
        in_specs=in_specs, out_specs=out_specs, out_shape=out_shape, scratch_shapes=scratch,
        compiler_params=_params(),
    )(*args)


def _mem_norm(mem, g):
    M, D = mem.shape

    def body(mem_ref, g_ref, out_ref):
        out_ref[...] = _rms(mem_ref[...], g_ref[...]).astype(BF16)

    return pl.pallas_call(
        body, name="mem_norm", grid=(1,),
        in_specs=[_whole((M, D)), _whole((1, D))], out_specs=_whole((M, D)),
        out_shape=jax.ShapeDtypeStruct((M, D), BF16), compiler_params=_params(),
    )(mem, g)


def _mem_norm_bwd(mem, g, dmem_n):
    M, D = mem.shape
    L = dmem_n.shape[0]

    def body(mem_ref, g_ref, d_ref, dg_ref):
        d = d_ref[0]
        for l in range(1, L):
            d = d + d_ref[l]
        _, vjp = jax.vjp(_rms, mem_ref[...], g_ref[...])
        dg_ref[...] = vjp(d)[1]

    return pl.pallas_call(
        body, name="mem_norm_bwd", grid=(1,),
        in_specs=[_whole((M, D)), _whole((1, D)), _whole((L, M, D))], out_specs=_whole((1, D)),
        out_shape=jax.ShapeDtypeStruct((1, D), F32), compiler_params=_params(),
    )(mem, g, dmem_n)


def _loss_head(h, target):
    S, D = h.shape
    bm = _pick(S, 512)

    def body(h_ref, t_ref, dh_ref, loss_ref):
        err = h_ref[...] - t_ref[...]
        dh_ref[...] = err * (1.0 / D)

        @pl.when(pl.program_id(0) == 0)
        def _():
            loss_ref[...] = jnp.zeros_like(loss_ref)

        loss_ref[...] += 0.5 * jnp.sum(jnp.mean(err * err, axis=-1, keepdims=True), axis=0, keepdims=True)

    return pl.pallas_call(
        body, name="loss_head", grid=(S // bm,),
        in_specs=[_rows(bm, D), _rows(bm, D)], out_specs=[_rows(bm, D), _whole((8, LANE))],
        out_shape=[jax.ShapeDtypeStruct((S, D), F32), jax.ShapeDtypeStruct((8, LANE), F32)],
        compiler_params=_params(),
    )(h, target)


def _adamw(w, g, m, v, name):
    R, C = w.shape
    rb = R if R * C * 4 <= (1 << 20) else _pick(R, 256)
    if R % rb:
        rb = R
    c1 = 1.0 - ADAM_B1 ** ADAM_STEP
    c2 = 1.0 - ADAM_B2 ** ADAM_STEP

    def body(w_ref, g_ref, m_ref, v_ref, d_ref, mo_ref, vo_ref):
        gv = g_ref[...]
        mn = ADAM_B1 * m_ref[...] + (1.0 - ADAM_B1) * gv
        vn = ADAM_B2 * v_ref[...] + (1.0 - ADAM_B2) * (gv * gv)
        mo_ref[...] = mn
        vo_ref[...] = vn
        d_ref[...] = -ADAM_LR * ((mn / c1) / (jnp.sqrt(vn / c2) + ADAM_EPS) + ADAM_WD * w_ref[...])

    return pl.pallas_call(
        body, name=name, grid=(R // rb,),
        in_specs=[_rows(rb, C)] * 4, out_specs=[_rows(rb, C)] * 3,
        out_shape=[jax.ShapeDtypeStruct((R, C), F32)] * 3, compiler_params=_params(),
    )(w, g, m, v)


def _adamw_reduced(w, m, v, mine, theirs, c_idx, first_row, name):
    L, a, b = w.shape
    Lh = L // 2
    rb = _shard_row_block(a)
    nb = a // rb
    assert first_row % rb == 0
    c1 = 1.0 - ADAM_B1 ** ADAM_STEP
    c2 = 1.0 - ADAM_B2 ** ADAM_STEP

    def own(i, c_ref):
        return (i, 0)

    def reduced(i, c_ref):
        return (first_row // rb + ((i // nb) % Lh) * nb + i % nb, 0)

    def body(c_ref, w_ref, m_ref, v_ref, mine_ref, theirs_ref, g_ref, d_ref, mo_ref, vo_ref):
        half = (pl.program_id(0) // nb) // Lh
        gv = jnp.where(c_ref[0] == half, mine_ref[...], theirs_ref[...])
        g_ref[...] = gv
        mn = ADAM_B1 * m_ref[...] + (1.0 - ADAM_B1) * gv
        vn = ADAM_B2 * v_ref[...] + (1.0 - ADAM_B2) * (gv * gv)
        mo_ref[...] = mn
        vo_ref[...] = vn
        d_ref[...] = -ADAM_LR * ((mn / c1) / (jnp.sqrt(vn / c2) + ADAM_EPS) + ADAM_WD * w_ref[...])

    outs = pl.pallas_call(
        body, name=name,
        grid_spec=pltpu.PrefetchScalarGridSpec(
            num_scalar_prefetch=1, grid=(L * nb,),
            in_specs=[pl.BlockSpec((rb, b), own)] * 3 + [pl.BlockSpec((rb, b), reduced)] * 2,
            out_specs=[pl.BlockSpec((rb, b), own)] * 4),
        out_shape=[jax.ShapeDtypeStruct((L * a, b), F32)] * 4, compiler_params=_params(),
    )(c_idx, w.reshape(L * a, b), m.reshape(L * a, b), v.reshape(L * a, b), mine, theirs)
    return [t.reshape(L, a, b) for t in outs]


ANY = pl.BlockSpec(memory_space=pl.ANY)
MESH = pl.DeviceIdType.MESH


def _place():
    x, y, c = lax.axis_index("x"), lax.axis_index("y"), lax.axis_index("c")
    others = [(1 - x, y), (x, 1 - y), (1 - x, 1 - y)]
    return x, y, c, others


def _place_own(loc, chip_idx):
    _, R, C = loc.shape
    rb = _pick(R, 2 * FLAT_ROW_BLOCK)

    def body(chip_ref, loc_ref, out_ref):
        out_ref[...] = loc_ref[...]

    return pl.pallas_call(
        body, name="place_own",
        grid_spec=pltpu.PrefetchScalarGridSpec(
            num_scalar_prefetch=1, grid=(2, R // rb),
            in_specs=[pl.BlockSpec((None, rb, C), lambda hf, i, chip_ref: (hf, i, 0))],
            out_specs=pl.BlockSpec((None, None, rb, C), lambda hf, i, chip_ref: (chip_ref[0], hf, i, 0))),
        out_shape=jax.ShapeDtypeStruct((N_CHIPS, 2, R, C), loc.dtype), compiler_params=_params(),
    )(chip_idx, loc)


def _gather_weights(locs, owns):
    n = len(locs)

    def body(*refs):
        loc_refs, out_refs, (send_sems, recv_sems) = refs[:n], refs[2 * n:3 * n], refs[3 * n:]
        x, y, c, others = _place()
        me = 2 * x + y
        sibling = (x, y, 1 - c)

        def copy(a, k, src, dst, to):
            return pltpu.make_async_remote_copy(src_ref=src, dst_ref=dst, send_sem=send_sems.at[a, k],
                                                recv_sem=recv_sems.at[a, k], device_id=to, device_id_type=MESH)

        first = [copy(a, j, loc_refs[a].at[c], out_refs[a].at[me, c], (ox, oy, c))
                 for j, (ox, oy) in enumerate(others) for a in range(n)]
        for cp in first:
            cp.start()
        passed = []
        for j, (ox, oy) in enumerate(others):
            for a in range(n):
                landed = out_refs[a].at[2 * ox + oy, c]
                copy(a, j, loc_refs[a].at[c], landed, sibling).wait_recv()
                cp = copy(a, 3 + j, landed, landed, sibling)
                cp.start()
                passed.append(cp)
        for j, (ox, oy) in enumerate(others):
            for a in range(n):
                copy(a, 3 + j, loc_refs[a].at[c], out_refs[a].at[2 * ox + oy, 1 - c], sibling).wait_recv()
        for cp in first + passed:
            cp.wait_send()

    return pl.pallas_call(
        body, name="gather_weights", in_specs=[ANY] * (2 * n), out_specs=[ANY] * n,
        out_shape=[jax.ShapeDtypeStruct(own.shape, own.dtype) for own in owns],
        input_output_aliases={n + a: a for a in range(n)},
        scratch_shapes=[pltpu.SemaphoreType.DMA((n, 6)), pltpu.SemaphoreType.DMA((n, 6))],
    )(*locs, *owns)


def _pair_exchange(gs):
    n = len(gs)

    def body(*refs):
        g_refs, out_refs, (send_sems, recv_sems) = refs[:n], refs[n:2 * n], refs[2 * n:]
        x, y, c, _ = _place()
        copies = [pltpu.make_async_remote_copy(src_ref=g_refs[a].at[1 - c], dst_ref=out_refs[a],
                                               send_sem=send_sems.at[a], recv_sem=recv_sems.at[a],
                                               device_id=(x, y, 1 - c), device_id_type=MESH) for a in range(n)]
        for cp in copies:
            cp.start()
        for cp in copies:
            cp.wait()

    return pl.pallas_call(
        body, name="pair_exchange", in_specs=[ANY] * n, out_specs=[ANY] * n,
        out_shape=[jax.ShapeDtypeStruct(g.shape[1:], g.dtype) for g in gs],
        scratch_shapes=[pltpu.SemaphoreType.DMA((n,)), pltpu.SemaphoreType.DMA((n,))],
    )(*gs)


def _pair_sum(g, sib, c_idx):
    _, _, R, C = g.shape
    rb = _pick(R, 512)

    def body(c_ref, g_ref, s_ref, o_ref):
        o_ref[...] = (g_ref[...].astype(F32) + s_ref[...].astype(F32)).astype(o_ref.dtype)

    return pl.pallas_call(
        body, name="pair_sum",
        grid_spec=pltpu.PrefetchScalarGridSpec(
            num_scalar_prefetch=1, grid=(N_CHIPS, R // rb),
            in_specs=[pl.BlockSpec((None, None, rb, C), lambda j, i, c_ref: (c_ref[0], j, i, 0)),
                      pl.BlockSpec((None, rb, C), lambda j, i, c_ref: (j, i, 0))],
            out_specs=pl.BlockSpec((None, rb, C), lambda j, i, c_ref: (j, i, 0))),
        out_shape=jax.ShapeDtypeStruct((N_CHIPS, R, C), g.dtype), compiler_params=_params(),
    )(c_idx, g, sib)


def _chip_exchange(ps):
    n = len(ps)

    def body(*refs):
        p_refs, out_refs, (send_sems, recv_sems) = refs[:n], refs[n:2 * n], refs[2 * n:]
        x, y, c, others = _place()
        copies = []
        for j, (ox, oy) in enumerate(others):
            for a in range(n):
                cp = pltpu.make_async_remote_copy(src_ref=p_refs[a].at[2 * ox + oy], dst_ref=out_refs[a].at[j],
                                                  send_sem=send_sems.at[a, j], recv_sem=recv_sems.at[a, j],
                                                  device_id=(ox, oy, c), device_id_type=MESH)
                cp.start()
                copies.append(cp)
        for cp in copies:
            cp.wait()

    return pl.pallas_call(
        body, name="chip_exchange", in_specs=[ANY] * n, out_specs=[ANY] * n,
        out_shape=[jax.ShapeDtypeStruct((N_CHIPS - 1,) + p.shape[1:], p.dtype) for p in ps],
        scratch_shapes=[pltpu.SemaphoreType.DMA((n, 3)), pltpu.SemaphoreType.DMA((n, 3))],
    )(*ps)


def _chip_sum(p, r, chip_idx):
    _, R, C = r.shape
    rb = _pick(R, 512)

    def body(chip_ref, p_ref, r_ref, o_ref):
        acc = p_ref[...].astype(F32)
        for j in range(N_CHIPS - 1):
            acc = acc + r_ref[j].astype(F32)
        o_ref[...] = acc

    return pl.pallas_call(
        body, name="chip_sum",
        grid_spec=pltpu.PrefetchScalarGridSpec(
            num_scalar_prefetch=1, grid=(R // rb,),
            in_specs=[pl.BlockSpec((None, rb, C), lambda i, chip_ref: (chip_ref[0], i, 0)),
                      pl.BlockSpec((N_CHIPS - 1, rb, C), lambda i, chip_ref: (0, i, 0))],
            out_specs=pl.BlockSpec((rb, C), lambda i, chip_ref: (i, 0))),
        out_shape=jax.ShapeDtypeStruct((R, C), F32), compiler_params=_params(),
    )(chip_idx, p, r)


def _pair_swap(rhs):
    n = len(rhs)

    def body(*refs):
        rh_refs, out_refs, (send_sems, recv_sems) = refs[:n], refs[n:2 * n], refs[2 * n:]
        x, y, c, _ = _place()
        copies = [pltpu.make_async_remote_copy(src_ref=rh_refs[a], dst_ref=out_refs[a], send_sem=send_sems.at[a],
                                               recv_sem=recv_sems.at[a], device_id=(x, y, 1 - c),
                                               device_id_type=MESH) for a in range(n)]
        for cp in copies:
            cp.start()
        for cp in copies:
            cp.wait()

    return pl.pallas_call(
        body, name="pair_swap", in_specs=[ANY] * n, out_specs=[ANY] * n,
        out_shape=[jax.ShapeDtypeStruct(rh.shape, rh.dtype) for rh in rhs],
        scratch_shapes=[pltpu.SemaphoreType.DMA((n,)), pltpu.SemaphoreType.DMA((n,))],
    )(*rhs)


def _all_reduce_small(s):
    R, C = s.shape

    def body(s_ref, o_ref, buf, send_sems, recv_sems):
        x, y, c, _ = _place()
        me = 4 * x + 2 * y + c
        sends = []
        for k in range(1, N_DEV):
            fx, fy, fc = (k >> 2) & 1, (k >> 1) & 1, k & 1
            to = (x ^ fx, y ^ fy, c ^ fc)
            cp = pltpu.make_async_remote_copy(src_ref=s_ref, dst_ref=buf.at[me], send_sem=send_sems.at[k - 1],
                                              recv_sem=recv_sems.at[k - 1], device_id=to, device_id_type=MESH)
            cp.start()
            sends.append(cp)
        buf[me] = s_ref[...]
        for k in range(1, N_DEV):
            fx, fy, fc = (k >> 2) & 1, (k >> 1) & 1, k & 1
            frm = 4 * (x ^ fx) + 2 * (y ^ fy) + (c ^ fc)
            pltpu.make_async_remote_copy(src_ref=s_ref, dst_ref=buf.at[frm], send_sem=send_sems.at[k - 1],
                                         recv_sem=recv_sems.at[k - 1], device_id=(x, y, c),
                                         device_id_type=MESH).wait_recv()
        acc = buf[0]
        for d in range(1, N_DEV):
            acc = acc + buf[d]
        o_ref[...] = acc
        for cp in sends:
            cp.wait_send()

    vm = pl.BlockSpec(memory_space=pltpu.VMEM)
    return pl.pallas_call(
        body, name="all_reduce_small", in_specs=[vm], out_specs=vm,
        out_shape=jax.ShapeDtypeStruct((R, C), F32),
        scratch_shapes=[pltpu.VMEM((N_DEV, R, C), F32), pltpu.SemaphoreType.DMA((N_DEV - 1,)),
                        pltpu.SemaphoreType.DMA((N_DEV - 1,))],
    )(s)


def _padded(n):
    return -(-n // FLAT_UNIT) * FLAT_UNIT


def _pack_flat(pieces, dtype, row_block=FLAT_ROW_BLOCK):
    flat = []
    for p in pieces:
        p = p.reshape(-1).astype(dtype)
        flat.append(jnp.pad(p, (0, _padded(p.size) - p.size)))
    total = sum(p.size for p in flat)
    flat.append(jnp.zeros((-total) % (row_block * FLAT_COLS), dtype))
    return jnp.concatenate(flat).reshape(-1, FLAT_COLS)


def _unpack_flat(flat, shapes):
    lead = flat.shape[:-2]
    flat = flat.reshape(lead + (-1,))
    out, off = [], 0
    for shp in shapes:
        n = math.prod(shp)
        out.append(flat[..., off:off + n].reshape(lead + tuple(shp)))
        off += _padded(n)
    return out


def _shard_row_block(a):
    for rb in range(min(a, 512) // 16 * 16, 0, -16):
        if a % rb == 0:
            return rb
    return a


def _row_layout(shapes, n_layers):
    groups = {}
    for name, (a, b) in shapes.items():
        names, first, rows = groups.get(b, ((), {}, 0))
        rb = _shard_row_block(a)
        start = -(-rows // rb) * rb
        groups[b] = (names + (name,), {**first, name: start}, start + n_layers * a)
    return {b: (names, first, -(-rows // FLAT_ROW_BLOCK) * FLAT_ROW_BLOCK) for b, (names, first, rows) in groups.items()}


def _pack_rows(group, width, pieces, dtype):
    names, first, rows = group
    parts, at = [], 0
    for name in names:
        if first[name] > at:
            parts.append(jnp.zeros((first[name] - at, width), dtype))
        parts.append(pieces[name].astype(dtype))
        at = first[name] + pieces[name].shape[0]
    if rows > at:
        parts.append(jnp.zeros((rows - at, width), dtype))
    return jnp.concatenate(parts, axis=0)


def _slab(t, axis, j):
    n = t.shape[axis - 1] // N_CHIPS
    return lax.slice_in_dim(t, j * n, (j + 1) * n, axis=axis - 1)


def _layer_fwd(h0, mem_n, wl, dims):
    n_sb, n_fx, n_mem, sbw, fxw, memw = dims
    n1, gate1, up1, a1 = _ffn_fwd_up(h0, wl["ffn1_pre_g"], wl["ffn1_w_gate"], wl["ffn1_w_up"])
    h1, f1 = _ffn_fwd_down(a1, wl["ffn1_w_down"], h0, wl["ffn1_post_g"])

    u, proj, fl, sg = _mix_fwd_in(h1, wl["mix_pre_g"], wl["w_in"], wl["w_gate"], wl["b_gate"], wl["b_forget"])
    c = _fox_cumsum(fl)
    S = h0.shape[0]
    tc = _attn_blocks("fox", S, S)[1]
    ct = c[:, :n_fx].T
    ccol, crow = ct.reshape(n_fx, S, 1), ct.reshape(n_fx, S // tc, tc)
    qkv_sb = [(proj, k * sbw) for k in range(3)]
    qkv_fx = [(proj, 3 * sbw + k * fxw) for k in range(3)]
    kv = _matmul(mem_n, wl["w_mem_kv"], out_dtype=BF16, name="mem_kv")
    qkv_mem = [(proj, 3 * sbw + 3 * fxw), (kv, 0), (kv, memw)]
    o_sb, tot_sb = _attn_fwd("sb", *qkv_sb, n_sb, HEAD_DIM)
    o_fx, lse_fx = _attn_fwd("fox", *qkv_fx, n_fx, HEAD_DIM, ccol, crow)
    o_mem, lse_mem = _attn_fwd("mem", *qkv_mem, n_mem, MEM_HEAD_DIM)
    h2, zmix, merged = _mix_fwd_out(o_sb, o_fx, o_mem, sg, wl["w_br_sb"], wl["w_br_fox"], wl["w_br_mem"],
                                    wl["w_out"], h1, wl["mix_post_g"])

    n2, gate2, up2, a2 = _ffn_fwd_up(h2, wl["ffn2_pre_g"], wl["ffn2_w_gate"], wl["ffn2_w_up"])
    h3, f2 = _ffn_fwd_down(a2, wl["ffn2_w_down"], h2, wl["ffn2_post_g"])
    saved = dict(h0=h0, n1=n1, gate1=gate1, up1=up1, a1=a1, f1=f1, h1=h1, u=u, fl=fl, sg=sg,
                 qkv_sb=qkv_sb, qkv_fx=qkv_fx, qkv_mem=qkv_mem, ccol=ccol, crow=crow, o_sb=o_sb, o_fx=o_fx, o_mem=o_mem,
                 tot_sb=tot_sb, lse_fx=lse_fx, lse_mem=lse_mem,
                 zmix=zmix, merged=merged, h2=h2, n2=n2, gate2=gate2, up2=up2, a2=a2, f2=f2)
    return h3, saved


def _ffn_bwd(dh, sv, wl, tag, h_in):
    n, gate, up, a, f = (sv[k + tag] for k in ("n", "gate", "up", "a", "f"))
    pre = "ffn" + tag
    df, dgate, dup, dg_post = _ffn_bwd_down(dh, f, wl[pre + "_post_g"], wl[pre + "_w_down"], gate, up)
    dh_in, dg_pre = _ffn_bwd_up(dgate, dup, wl[pre + "_w_gate"], wl[pre + "_w_up"], h_in, wl[pre + "_pre_g"], dh)
    grads = {pre + "_post_g": dg_post, pre + "_pre_g": dg_pre,
             pre + "_w_down": _matmul(a, df, ta=True, name="dw_down"),
             pre + "_w_gate": _matmul(n, dgate, ta=True, name="dw_gate"),
             pre + "_w_up": _matmul(n, dup, ta=True, name="dw_up")}
    return dh_in, grads


def _layer_bwd(dh3, mem_n, wl, sv, dims):
    n_sb, n_fx, n_mem, sbw, fxw, memw = dims
    S = dh3.shape[0]
    dh2, grads = _ffn_bwd(dh3, sv, wl, "2", sv["h2"])

    (dz, db_sb, db_fx, db_mem, do_sb, do_fx, do_mem, dgp, db_gate, dg_post) = _mix_bwd_out(
        dh2, sv["zmix"], wl["mix_post_g"], wl["w_out"], sv["o_sb"], sv["o_fx"], sv["o_mem"],
        wl["w_br_sb"], wl["w_br_fox"], wl["w_br_mem"], sv["sg"])
    grads["mix_post_g"] = dg_post
    grads["b_gate"] = db_gate
    grads["w_out"] = _matmul(sv["merged"], dz, ta=True, name="dw_out")
    grads["w_br_sb"] = _matmul(sv["o_sb"], db_sb, ta=True, name="dw_br_sb")
    grads["w_br_fox"] = _matmul(sv["o_fx"], db_fx, ta=True, name="dw_br_fox")
    grads["w_br_mem"] = _matmul(sv["o_mem"], db_mem, ta=True, name="dw_br_mem")

    dq_sb, dk_sb, dv_sb = _attn_bwd("sb", *sv["qkv_sb"], sv["o_sb"], do_sb, n_sb, HEAD_DIM, lse=sv["tot_sb"])
    dq_fx, dk_fx, dv_fx, dcrow, dccol = _attn_bwd("fox", *sv["qkv_fx"], sv["o_fx"], do_fx, n_fx, HEAD_DIM,
                                                  sv["ccol"], sv["crow"], sv["lse_fx"])
    dq_mem, dk_mem, dv_mem = _attn_bwd("mem", *sv["qkv_mem"], sv["o_mem"], do_mem, n_mem, MEM_HEAD_DIM,
                                       lse=sv["lse_mem"])
    dkv = jnp.concatenate([dk_mem, dv_mem], axis=1)
    grads["w_mem_kv"] = _matmul(mem_n, dkv, ta=True, name="dw_mem_kv")
    dmem_n = _matmul(dkv, wl["w_mem_kv"], tb=True, out_dtype=F32, name="dmem_n")

    dc = jnp.pad((dcrow.reshape(n_fx, S) + dccol.reshape(n_fx, S)).T, ((0, 0), (0, LANE - n_fx)))
    dfl, db_forget = _fox_dlogit(dc, sv["fl"])
    grads["b_forget"] = db_forget
    dproj = jnp.concatenate([dq_sb, dk_sb, dv_sb, dq_fx, dk_fx, dv_fx, dq_mem, dfl], axis=1)
    dh1, dg_pre = _mix_bwd_in(dproj, dgp, wl["w_in"], wl["w_gate"], sv["h1"], wl["mix_pre_g"], dh2)
    grads["mix_pre_g"] = dg_pre
    grads["w_in"] = _matmul(sv["u"], dproj, ta=True, name="dw_in")
    grads["w_gate"] = _matmul(sv["u"], dgp, ta=True, name="dw_gate_mix")

    dh0, g1 = _ffn_bwd(dh1, sv, wl, "1", sv["h0"])
    grads.update(g1)
    return dh0, grads, dmem_n


def kernel(x, mem, ffn1_pre_g, ffn1_post_g, ffn1_w_gate, ffn1_w_up, ffn1_w_down, mix_pre_g, mix_post_g, w_in, b_forget, mem_norm_g, w_mem_kv, w_gate, b_gate, w_br_sb, w_br_fox, w_br_mem, w_out, ffn2_pre_g, ffn2_post_g, ffn2_w_gate, ffn2_w_up, ffn2_w_down, loss_target, m_ffn1_pre_g, m_ffn1_post_g, m_ffn1_w_gate, m_ffn1_w_up, m_ffn1_w_down, m_mix_pre_g, m_mix_post_g, m_w_in, m_b_forget, m_mem_norm_g, m_w_mem_kv, m_w_gate, m_b_gate, m_w_br_sb, m_w_br_fox, m_w_br_mem, m_w_out, m_ffn2_pre_g, m_ffn2_post_g, m_ffn2_w_gate, m_ffn2_w_up, m_ffn2_w_down, v_ffn1_pre_g, v_ffn1_post_g, v_ffn1_w_gate, v_ffn1_w_up, v_ffn1_w_down, v_mix_pre_g, v_mix_post_g, v_w_in, v_b_forget, v_mem_norm_g, v_w_mem_kv, v_w_gate, v_b_gate, v_w_br_sb, v_w_br_fox, v_w_br_mem, v_w_out, v_ffn2_pre_g, v_ffn2_post_g, v_ffn2_w_gate, v_ffn2_w_up, v_ffn2_w_down):
    args = dict(locals())
    w = {n: args[n] for n in WEIGHTS}
    m = {n: args["m_" + n] for n in WEIGHTS}
    v = {n: args["v_" + n] for n in WEIGHTS}
    L = w["ffn1_pre_g"].shape[0]
    Lh = L // 2
    D = x.shape[2]
    sbw, fxw, memw = w["w_br_sb"].shape[1], w["w_br_fox"].shape[1], w["w_br_mem"].shape[1]
    n_sb, n_fx, n_mem = sbw // HEAD_DIM, fxw // HEAD_DIM, memw // MEM_HEAD_DIM
    dims = (n_sb, n_fx, n_mem, sbw, fxw, memw)
    qkv_w = 3 * sbw + 3 * fxw
    c_idx = lax.axis_index("c")
    c_arr = c_idx.reshape(1).astype(jnp.int32)
    chip_arr = (2 * lax.axis_index("x") + lax.axis_index("y")).reshape(1).astype(jnp.int32)

    shard_shapes = {n: w[n].shape[1:] for n, _ in BIG}
    layout = _row_layout(shard_shapes, Lh)
    widths = list(layout)
    locs = [jnp.stack([_pack_rows(layout[b], b, {n: w[n][hf * Lh:(hf + 1) * Lh].reshape(-1, b) for n in layout[b][0]},
                                  BF16) for hf in range(2)]) for b in widths]
    gathered = dict(zip(widths, _gather_weights(locs, [_place_own(loc, chip_arr) for loc in locs])))

    def layer_weights(l):
        hf, li = divmod(l, Lh)
        wl = {}
        for n, axis in BIG:
            a, b = shard_shapes[n]
            r0 = layout[b][1][n] + li * a
            shards = gathered[b][:, hf, r0:r0 + a]
            wl[n] = shards.transpose(1, 0, 2).reshape(a, N_CHIPS * b) if axis == 2 else shards.reshape(N_CHIPS * a, b)
        wi = wl["w_in"]
        wl["w_in"] = jnp.concatenate([wi[:, :qkv_w], wi[:, qkv_w + n_fx:], wi[:, qkv_w:qkv_w + n_fx],
                                      jnp.zeros((D, LANE - n_fx), BF16)], axis=1)
        for n in SMALL:
            if n != "mem_norm_g":
                wl[n] = w[n][l][None, :]
        wl["b_forget"] = jnp.pad(wl["b_forget"], ((0, 0), (0, LANE - n_fx)))
        return wl

    g_mem = w["mem_norm_g"][None, :]

    mem_n = _mem_norm(mem[0], g_mem)
    h, wls, saved = x[0], [], []
    for l in range(L):
        wls.append(layer_weights(l))
        h, sv = _layer_fwd(h, mem_n, wls[l], dims)
        saved.append(sv)
    dh, loss_tile = _loss_head(h, loss_target[0])
    loss = lax.psum(loss_tile[0, 0], ("x", "y", "c"))
    gl, dmem_n = [None] * L, [None] * L
    for l in reversed(range(L)):
        dh, gl[l], dmem_n[l] = _layer_bwd(dh, mem_n, wls[l], saved[l], dims)
        gi = gl[l]["w_in"]
        gl[l]["w_in"] = jnp.concatenate([gi[:, :qkv_w], gi[:, qkv_w + memw:qkv_w + memw + n_fx],
                                         gi[:, qkv_w:qkv_w + memw]], axis=1)
    grad_x = dh
    g_mem_norm = _mem_norm_bwd(mem[0], g_mem, jnp.stack(dmem_n))

    axis_of = dict(BIG)
    partials = [jnp.stack([jnp.stack([
        _pack_rows(layout[b], b, {n: jnp.concatenate([_slab(gl[hf * Lh + li][n], axis_of[n], j) for li in range(Lh)])
                                  for n in layout[b][0]}, BF16)
        for j in range(N_CHIPS)]) for hf in range(2)]) for b in widths]
    pairs = [_pair_sum(g, sib, c_arr) for g, sib in zip(partials, _pair_exchange(partials))]
    mines = [_chip_sum(p, r, chip_arr) for p, r in zip(pairs, _chip_exchange(pairs))]
    theirs = _pair_swap(mines)

    grad, delta, new_m, new_v = {}, {}, {}, {}
    for n, _ in BIG:
        k = widths.index(shard_shapes[n][1])
        grad[n], delta[n], new_m[n], new_v[n] = _adamw_reduced(
            w[n], m[n], v[n], mines[k], theirs[k], c_arr, layout[widths[k]][1][n], name="adamw_" + n)

    small_local = {n: (g_mem_norm if n == "mem_norm_g" else
                       jnp.concatenate([gl[l][n][:, :n_fx] if n == "b_forget" else gl[l][n] for l in range(L)]))
                   for n in SMALL}
    small_shapes = [small_local[n].shape for n in SMALL]
    small_sum = _unpack_flat(_all_reduce_small(_pack_flat([small_local[n] for n in SMALL], F32, row_block=16)),
                             small_shapes)
    for n, t in zip(SMALL, small_sum):
        shp = w[n].shape
        two_d = (1, shp[0]) if len(shp) == 1 else shp
        grad[n] = t.reshape(shp)
        d_, m_, v_ = _adamw(w[n].reshape(two_d), t.reshape(two_d), m[n].reshape(two_d), v[n].reshape(two_d),
                            name="adamw_" + n)
        delta[n], new_m[n], new_v[n] = d_.reshape(shp), m_.reshape(shp), v_.reshape(shp)

    return (loss, grad_x[None], *[grad[n] for n in WEIGHTS], *[delta[n] for n in WEIGHTS],
            *[new_m[n] for n in WEIGHTS], *[new_v[n] for n in WEIGHTS])
```

```python
import math

import jax
import jax.numpy as jnp
from jax import lax
from jax.experimental import pallas as pl
from jax.experimental.pallas import tpu as pltpu

F32 = jnp.float32
BF16 = jnp.bfloat16
RMS_EPS = 1e-6
HEAD_DIM = 64
MEM_HEAD_DIM = 128
LANE = 128
V7X_VMEM_LIMIT_BYTES = 56 * 1024 * 1024
FLAT_COLS = 512
FLAT_UNIT = 16 * FLAT_COLS
FLAT_ROW_BLOCK = 512
N_CHIPS = 4
N_DEV = 8
NEG = float(jnp.finfo(jnp.float32).min)

ADAM_LR = 0.001
ADAM_B1 = 0.9
ADAM_B2 = 0.999
ADAM_EPS = 1e-08
ADAM_WD = 0.01
ADAM_STEP = 10

BIG = (("ffn1_w_gate", 2), ("ffn1_w_up", 2), ("ffn1_w_down", 1), ("w_in", 2), ("w_mem_kv", 1), ("w_gate", 2),
       ("w_br_sb", 2), ("w_br_fox", 2), ("w_br_mem", 2), ("w_out", 1),
       ("ffn2_w_gate", 2), ("ffn2_w_up", 2), ("ffn2_w_down", 1))
SMALL = ("ffn1_pre_g", "ffn1_post_g", "mix_pre_g", "mix_post_g", "b_forget", "mem_norm_g", "b_gate",
         "ffn2_pre_g", "ffn2_post_g")
WEIGHTS = ("ffn1_pre_g", "ffn1_post_g", "ffn1_w_gate", "ffn1_w_up", "ffn1_w_down", "mix_pre_g", "mix_post_g", "w_in",
           "b_forget", "mem_norm_g", "w_mem_kv", "w_gate", "b_gate", "w_br_sb", "w_br_fox", "w_br_mem", "w_out",
           "ffn2_pre_g", "ffn2_post_g", "ffn2_w_gate", "ffn2_w_up", "ffn2_w_down")


def _params(**kw):
    return pltpu.CompilerParams(vmem_limit_bytes=V7X_VMEM_LIMIT_BYTES, **kw)


def _dot(a, b):
    return jnp.dot(a, b, preferred_element_type=F32)


def _dot_nt(a, b):
    return lax.dot_general(a, b, (((1,), (1,)), ((), ())), preferred_element_type=F32)


def _dot_tn(a, b):
    return lax.dot_general(a, b, (((0,), (0,)), ((), ())), preferred_element_type=F32)


def _rms(t, g):
    return t * lax.rsqrt(jnp.mean(t * t, axis=-1, keepdims=True) + RMS_EPS) * g


def _pick(dim, pref):
    if dim <= pref:
        return dim
    for cand in range(pref - pref % LANE, 0, -LANE):
        if dim % cand == 0:
            return cand
    return dim


def _rows(bm, cols):
    return pl.BlockSpec((bm, cols), lambda i: (i, 0))


def _whole(shape):
    nd = len(shape)
    return pl.BlockSpec(shape, lambda i: (0,) * nd)


def _split3(x):
    hi = x.astype(BF16)
    r1 = x - hi.astype(F32)
    mid = r1.astype(BF16)
    lo = (r1 - mid.astype(F32)).astype(BF16)
    return hi, mid, lo


def _cumdot(x, tri):
    hi = x.astype(BF16)
    lo = (x - hi.astype(F32)).astype(BF16)
    return _dot(hi, tri) + _dot(lo, tri)


def _slabs(bm, cols):
    return pl.BlockSpec((N_CHIPS, bm, cols), lambda i: (0, i, 0))


def _ffn_fwd_up(h, g_pre, wg, wu):
    S, D = h.shape
    Fs = wg.shape[2]
    bm = _pick(S, 256)

    def body(h_ref, g_ref, wg_ref, wu_ref, n_ref, gate_ref, up_ref, a_ref):
        n = _rms(h_ref[...], g_ref[...]).astype(BF16)
        n_ref[...] = n
        for j in range(N_CHIPS):
            gate = _dot(n, wg_ref[j])
            up = _dot(n, wu_ref[j])
            gate_ref[j] = gate.astype(BF16)
            up_ref[j] = up.astype(BF16)
            a_ref[j] = (gate * jax.nn.sigmoid(gate) * up).astype(BF16)

    return pl.pallas_call(
        body, name="ffn_fwd_up", grid=(S // bm,),
        in_specs=[_rows(bm, D), _whole((1, D)), _whole((N_CHIPS, D, Fs)), _whole((N_CHIPS, D, Fs))],
        out_specs=[_rows(bm, D), _slabs(bm, Fs), _slabs(bm, Fs), _slabs(bm, Fs)],
        out_shape=[jax.ShapeDtypeStruct((S, D), BF16)] + [jax.ShapeDtypeStruct((N_CHIPS, S, Fs), BF16)] * 3,
        compiler_params=_params(),
    )(h, g_pre, wg, wu)


def _ffn_fwd_down(a, wd, h, g_post):
    _, S, Fs = a.shape
    D = wd.shape[2]
    bm = _pick(S, 256)

    def body(a_ref, wd_ref, h_ref, g_ref, hout_ref, f_ref):
        f = _dot(a_ref[0], wd_ref[0])
        for j in range(1, N_CHIPS):
            f = f + _dot(a_ref[j], wd_ref[j])
        f_ref[...] = f
        hout_ref[...] = h_ref[...] + 0.5 * _rms(f, g_ref[...])

    return pl.pallas_call(
        body, name="ffn_fwd_down", grid=(S // bm,),
        in_specs=[_slabs(bm, Fs), _whole((N_CHIPS, Fs, D)), _rows(bm, D), _whole((1, D))],
        out_specs=[_rows(bm, D), _rows(bm, D)],
        out_shape=[jax.ShapeDtypeStruct((S, D), F32)] * 2,
        compiler_params=_params(),
    )(a, wd, h, g_post)


def _ffn_bwd_down(dh, f, g_post, wd, gate, up):
    S, D = dh.shape
    Fs = wd.shape[1]
    bm = _pick(S, 256)

    def body(dh_ref, f_ref, g_ref, wd_ref, gate_ref, up_ref, df_ref, dgate_ref, dup_ref, dg_ref):
        _, vjp = jax.vjp(lambda t, g: 0.5 * _rms(t, g), f_ref[...], g_ref[...])
        df, dg = vjp(dh_ref[...])

        @pl.when(pl.program_id(0) == 0)
        def _():
            dg_ref[...] = jnp.zeros_like(dg_ref)

        dg_ref[...] += dg
        dfb = df.astype(BF16)
        df_ref[...] = dfb
        for j in range(N_CHIPS):
            da = _dot_nt(dfb, wd_ref[j])
            gt = gate_ref[j].astype(F32)
            sig = jax.nn.sigmoid(gt)
            silu = gt * sig
            dup_ref[j] = (da * silu).astype(BF16)
            dgate_ref[j] = (da * up_ref[j].astype(F32) * (sig + silu * (1.0 - sig))).astype(BF16)

    return pl.pallas_call(
        body, name="ffn_bwd_down", grid=(S // bm,),
        in_specs=[_rows(bm, D), _rows(bm, D), _whole((1, D)), _whole((N_CHIPS, Fs, D)), _slabs(bm, Fs),
                  _slabs(bm, Fs)],
        out_specs=[_rows(bm, D), _slabs(bm, Fs), _slabs(bm, Fs), _whole((1, D))],
        out_shape=[jax.ShapeDtypeStruct((S, D), BF16), jax.ShapeDtypeStruct((N_CHIPS, S, Fs), BF16),
                   jax.ShapeDtypeStruct((N_CHIPS, S, Fs), BF16), jax.ShapeDtypeStruct((1, D), F32)],
        compiler_params=_params(),
    )(dh, f, g_post, wd, gate, up)


def _ffn_bwd_up(dgate, dup, wg, wu, h_in, g_pre, dh):
    _, S, Fs = dgate.shape
    D = wg.shape[1]
    bm = _pick(S, 256)

    def body(dgate_ref, dup_ref, wg_ref, wu_ref, h_ref, g_ref, dh_ref, dhin_ref, dg_ref):
        dn = _dot_nt(dgate_ref[0], wg_ref[0]) + _dot_nt(dup_ref[0], wu_ref[0])
        for j in range(1, N_CHIPS):
            dn = dn + _dot_nt(dgate_ref[j], wg_ref[j]) + _dot_nt(dup_ref[j], wu_ref[j])
        _, vjp = jax.vjp(_rms, h_ref[...], g_ref[...])
        dhx, dg = vjp(dn)

        @pl.when(pl.program_id(0) == 0)
        def _():
            dg_ref[...] = jnp.zeros_like(dg_ref)

        dg_ref[...] += dg
        dhin_ref[...] = dh_ref[...] + dhx

    return pl.pallas_call(
        body, name="ffn_bwd_up", grid=(S // bm,),
        in_specs=[_slabs(bm, Fs), _slabs(bm, Fs), _whole((N_CHIPS, D, Fs)), _whole((N_CHIPS, D, Fs)), _rows(bm, D),
                  _whole((1, D)), _rows(bm, D)],
        out_specs=[_rows(bm, D), _whole((1, D))],
        out_shape=[jax.ShapeDtypeStruct((S, D), F32), jax.ShapeDtypeStruct((1, D), F32)],
        compiler_params=_params(),
    )(dgate, dup, wg, wu, h_in, g_pre, dh)


def _matmul(a, b, *, ta=False, tb=False, out_dtype=BF16, name, batch=None):
    n_batch = a.shape[0] if batch == "a" else b.shape[0] if batch == "b" else 1
    a_shape = a.shape[1:] if batch == "a" else a.shape
    b_shape = b.shape[1:] if batch == "b" else b.shape
    M, K = (a_shape[1], a_shape[0]) if ta else a_shape
    N = b_shape[0] if tb else b_shape[1]
    acc_budget = 12 * 1024 * 1024
    bm, bk = _pick(M, 1536), _pick(K, 512)
    while N * bm * 4 > acc_budget and bm % (2 * LANE) == 0:
        bm //= 2
    bn = N if N * bm * 4 <= acc_budget else _pick(N, 1536)
    nk = K // bk

    def body(a_ref, b_ref, o_ref, acc_ref):
        kk = pl.program_id(3)

        @pl.when(kk == 0)
        def _():
            acc_ref[...] = jnp.zeros_like(acc_ref)

        av, bv = a_ref[...], b_ref[...]
        dims = (((0 if ta else 1,), (1 if tb else 0,)), ((), ()))
        acc_ref[...] += lax.dot_general(av, bv, dims, preferred_element_type=F32)

        @pl.when(kk == nk - 1)
        def _():
            o_ref[...] = acc_ref[...].astype(o_ref.dtype)

    def spec(block, index, batched):
        if batched:
            return pl.BlockSpec((None,) + block, lambda g, i, j, k: (g,) + index(i, j, k))
        return pl.BlockSpec(block, lambda g, i, j, k: index(i, j, k))

    a_spec = spec((bk, bm), lambda i, j, k: (k, i), batch == "a") if ta else \
        spec((bm, bk), lambda i, j, k: (i, k), batch == "a")
    b_spec = spec((bn, bk), lambda i, j, k: (j, k), batch == "b") if tb else \
        spec((bk, bn), lambda i, j, k: (k, j), batch == "b")
    return pl.pallas_call(
        body, name=name, grid=(n_batch, M // bm, N // bn, nk),
        in_specs=[a_spec, b_spec],
        out_specs=spec((bm, bn), lambda i, j, k: (i, j), batch is not None),
        out_shape=jax.ShapeDtypeStruct(((n_batch,) if batch else ()) + (M, N), out_dtype),
        scratch_shapes=[pltpu.VMEM((bm, bn), F32)],
        compiler_params=_params(),
    )(a, b)


def _mix_fwd_in(h, g_pre, win, wgate, b_gate, b_forget):
    S, D = h.shape
    PW = win.shape[1] - LANE
    G = wgate.shape[1]
    bm = _pick(S, 256)

    def body(h_ref, g_ref, win_ref, wgate_ref, bg_ref, bf_ref, u_ref, proj_ref, fl_ref, sg_ref):
        u = _rms(h_ref[...], g_ref[...]).astype(BF16)
        u_ref[...] = u
        proj = _dot(u, win_ref[...])
        proj_ref[...] = proj[:, :PW].astype(BF16)
        fl_ref[...] = proj[:, PW:] + bf_ref[...]
        sg_ref[...] = jax.nn.sigmoid(_dot(u, wgate_ref[...]) + bg_ref[...]).astype(BF16)

    return pl.pallas_call(
        body, name="mix_fwd_in", grid=(S // bm,),
        in_specs=[_rows(bm, D), _whole((1, D)), _whole((D, PW + LANE)), _whole((D, G)), _whole((1, G)),
                  _whole((1, LANE))],
        out_specs=[_rows(bm, D), _rows(bm, PW), _rows(bm, LANE), _rows(bm, G)],
        out_shape=[jax.ShapeDtypeStruct((S, D), BF16), jax.ShapeDtypeStruct((S, PW), BF16),
                   jax.ShapeDtypeStruct((S, LANE), F32), jax.ShapeDtypeStruct((S, G), BF16)],
        compiler_params=_params(),
    )(h, g_pre, win, wgate, b_gate, b_forget)


def _mix_fwd_out(o_sb, o_fx, o_mem, sg, w_sb, w_fx, w_mem, w_out, h, g_post):
    S, D = h.shape
    bm = _pick(S, 256)
    widths = (o_sb.shape[1], o_fx.shape[1], o_mem.shape[1])

    def body(osb_ref, ofx_ref, omem_ref, sg_ref, wsb_ref, wfx_ref, wmem_ref, wout_ref, h_ref, g_ref,
             hout_ref, z_ref, merged_ref):
        s = sg_ref[...].astype(F32)
        merged = (s[:, :D] * _dot(osb_ref[...], wsb_ref[...]) + s[:, D:2 * D] * _dot(ofx_ref[...], wfx_ref[...])
                  + s[:, 2 * D:] * _dot(omem_ref[...], wmem_ref[...]))
        mb = merged.astype(BF16)
        merged_ref[...] = mb
        z = _dot(mb, wout_ref[...])
        z_ref[...] = z
        hout_ref[...] = h_ref[...] + _rms(z, g_ref[...])

    return pl.pallas_call(
        body, name="mix_fwd_out", grid=(S // bm,),
        in_specs=[_rows(bm, widths[0]), _rows(bm, widths[1]), _rows(bm, widths[2]), _rows(bm, 3 * D),
                  _whole((widths[0], D)), _whole((widths[1], D)), _whole((widths[2], D)), _whole((D, D)),
                  _rows(bm, D), _whole((1, D))],
        out_specs=[_rows(bm, D), _rows(bm, D), _rows(bm, D)],
        out_shape=[jax.ShapeDtypeStruct((S, D), F32), jax.ShapeDtypeStruct((S, D), F32),
                   jax.ShapeDtypeStruct((S, D), BF16)],
        compiler_params=_params(),
    )(o_sb, o_fx, o_mem, sg, w_sb, w_fx, w_mem, w_out, h, g_post)


def _mix_bwd_out(dh, z, g_post, w_out, o_sb, o_fx, o_mem, w_sb, w_fx, w_mem, sg):
    S, D = dh.shape
    bm = _pick(S, 256)
    widths = (o_sb.shape[1], o_fx.shape[1], o_mem.shape[1])

    def body(dh_ref, z_ref, g_ref, wout_ref, osb_ref, ofx_ref, omem_ref, wsb_ref, wfx_ref, wmem_ref, sg_ref,
             dz_ref, dbsb_ref, dbfx_ref, dbmem_ref, dosb_ref, dofx_ref, domem_ref, dgp_ref, dbg_ref, dg_ref):
        _, vjp = jax.vjp(_rms, z_ref[...], g_ref[...])
        dz, dg = vjp(dh_ref[...])

        @pl.when(pl.program_id(0) == 0)
        def _():
            dg_ref[...] = jnp.zeros_like(dg_ref)
            dbg_ref[...] = jnp.zeros_like(dbg_ref)

        dg_ref[...] += dg
        dzb = dz.astype(BF16)
        dz_ref[...] = dzb
        dmerged = _dot_nt(dzb, wout_ref[...])
        s = sg_ref[...].astype(F32)
        branches = ((osb_ref, wsb_ref, dbsb_ref, dosb_ref), (ofx_ref, wfx_ref, dbfx_ref, dofx_ref),
                    (omem_ref, wmem_ref, dbmem_ref, domem_ref))
        for k, (o_ref, w_ref, db_ref, do_ref) in enumerate(branches):
            gs = s[:, k * D:(k + 1) * D]
            dbb = (dmerged * gs).astype(BF16)
            db_ref[...] = dbb
            do_ref[...] = _dot_nt(dbb, w_ref[...]).astype(BF16)
            dgp = dmerged * _dot(o_ref[...], w_ref[...]) * gs * (1.0 - gs)
            dgp_ref[:, k * D:(k + 1) * D] = dgp.astype(BF16)
            dbg_ref[:, k * D:(k + 1) * D] += jnp.sum(dgp, axis=0, keepdims=True)

    return pl.pallas_call(
        body, name="mix_bwd_out", grid=(S // bm,),
        in_specs=[_rows(bm, D), _rows(bm, D), _whole((1, D)), _whole((D, D)),
                  _rows(bm, widths[0]), _rows(bm, widths[1]), _rows(bm, widths[2]),
                  _whole((widths[0], D)), _whole((widths[1], D)), _whole((widths[2], D)), _rows(bm, 3 * D)],
        out_specs=[_rows(bm, D)] * 4 + [_rows(bm, widths[0]), _rows(bm, widths[1]), _rows(bm, widths[2]),
                                        _rows(bm, 3 * D), _whole((1, 3 * D)), _whole((1, D))],
        out_shape=[jax.ShapeDtypeStruct((S, D), BF16)] * 4
        + [jax.ShapeDtypeStruct((S, w), BF16) for w in widths]
        + [jax.ShapeDtypeStruct((S, 3 * D), BF16), jax.ShapeDtypeStruct((1, 3 * D), F32),
           jax.ShapeDtypeStruct((1, D), F32)],
        compiler_params=_params(),
    )(dh, z, g_post, w_out, o_sb, o_fx, o_mem, w_sb, w_fx, w_mem, sg)


def _mix_bwd_in(dproj, dgp, win, wgate, h_in, g_pre, dh):
    S, PWL = dproj.shape
    G = dgp.shape[1]
    D = h_in.shape[1]
    bm = _pick(S, 256)

    def body(dproj_ref, dgp_ref, win_ref, wgate_ref, h_ref, g_ref, dh_ref, dhin_ref, dg_ref):
        du = _dot_nt(dproj_ref[...], win_ref[...]) + _dot_nt(dgp_ref[...], wgate_ref[...])
        _, vjp = jax.vjp(_rms, h_ref[...], g_ref[...])
        dhx, dg = vjp(du)

        @pl.when(pl.program_id(0) == 0)
        def _():
            dg_ref[...] = jnp.zeros_like(dg_ref)

        dg_ref[...] += dg
        dhin_ref[...] = dh_ref[...] + dhx

    return pl.pallas_call(
        body, name="mix_bwd_in", grid=(S // bm,),
        in_specs=[_rows(bm, PWL), _rows(bm, G), _whole((D, PWL)), _whole((D, G)), _rows(bm, D), _whole((1, D)),
                  _rows(bm, D)],
        out_specs=[_rows(bm, D), _whole((1, D))],
        out_shape=[jax.ShapeDtypeStruct((S, D), F32), jax.ShapeDtypeStruct((1, D), F32)],
        compiler_params=_params(),
    )(dproj, dgp, win, wgate, h_in, g_pre, dh)


def _log_sigmoid(x):
    return jnp.minimum(x, 0.0) - jnp.log(1.0 + jnp.exp(-jnp.abs(x)))


def _fox_cumsum(fl):
    S = fl.shape[0]
    rb = _pick(S, LANE)

    def body(fl_ref, c_ref, carry_ref):
        @pl.when(pl.program_id(0) == 0)
        def _():
            carry_ref[...] = jnp.zeros_like(carry_ref)

        r = lax.broadcasted_iota(jnp.int32, (rb, rb), 0)
        cidx = lax.broadcasted_iota(jnp.int32, (rb, rb), 1)
        tri = (cidx <= r).astype(BF16)
        hi, mid, lo = _split3(_log_sigmoid(fl_ref[...]))
        c = _dot(tri, hi) + _dot(tri, mid) + _dot(tri, lo) + carry_ref[...]
        c_ref[...] = c
        carry_ref[...] = c[rb - 1:rb, :]

    return pl.pallas_call(
        body, name="fox_cumsum", grid=(S // rb,),
        in_specs=[_rows(rb, LANE)], out_specs=_rows(rb, LANE),
        out_shape=jax.ShapeDtypeStruct((S, LANE), F32),
        scratch_shapes=[pltpu.VMEM((1, LANE), F32)],
        compiler_params=_params(),
    )(fl)


def _fox_dlogit(dc, fl):
    S = fl.shape[0]
    rb = _pick(S, LANE)
    nb = S // rb

    def body(dc_ref, fl_ref, dfl_ref, dbf_ref, carry_ref):
        @pl.when(pl.program_id(0) == 0)
        def _():
            carry_ref[...] = jnp.zeros_like(carry_ref)
            dbf_ref[...] = jnp.zeros_like(dbf_ref)

        r = lax.broadcasted_iota(jnp.int32, (rb, rb), 0)
        cidx = lax.broadcasted_iota(jnp.int32, (rb, rb), 1)
        tri = (cidx >= r).astype(BF16)
        hi, mid, lo = _split3(dc_ref[...])
        rc = _dot(tri, hi) + _dot(tri, mid) + _dot(tri, lo) + carry_ref[...]
        carry_ref[...] = rc[0:1, :]
        dfl = rc * jax.nn.sigmoid(-fl_ref[...])
        dfl_ref[...] = dfl.astype(BF16)
        dbf_ref[...] += jnp.sum(dfl, axis=0, keepdims=True)

    rev = pl.BlockSpec((rb, LANE), lambda i: (nb - 1 - i, 0))
    return pl.pallas_call(
        body, name="fox_dlogit", grid=(nb,),
        in_specs=[rev, rev], out_specs=[rev, _whole((1, LANE))],
        out_shape=[jax.ShapeDtypeStruct((S, LANE), BF16), jax.ShapeDtypeStruct((1, LANE), F32)],
        scratch_shapes=[pltpu.VMEM((1, LANE), F32)],
        compiler_params=_params(),
    )(dc, fl)


def _attn_blocks(kind, S, Sk, backward=False):
    tq = _pick(S, 1024 if backward else 2048)
    tc = LANE if kind == "sb" else _pick(Sk, 256)
    return tq, tc


def _is_power_of_two(x):
    return math.frexp(x)[0] == 0.5


def _sb_logs(z):
    ln = -jnp.maximum(z, 0.0) - jnp.log(1.0 + jnp.exp(-jnp.abs(z)))
    return ln + z, ln


def _head_lanes(pack, dh):
    lane = lax.broadcasted_iota(jnp.int32, (1, LANE), 1)
    return [(lane >= hh * dh) & (lane < (hh + 1) * dh) for hh in range(pack)]


def _by_head(sel, parts):
    out = parts[0]
    for hh in range(1, len(parts)):
        out = jnp.where(sel[hh], parts[hh], out)
    return out


def _only_head(sel, hh, x):
    return x if len(sel) == 1 else jnp.where(sel[hh], x, jnp.zeros_like(x))


def _tail(x, r0):
    return x if not r0 else x[r0:]


def _put_tail(x, tail, r0):
    return tail if not r0 else jnp.concatenate([x[:r0], tail], axis=0)


def _add_tail(x, tail, r0):
    return x + tail if not r0 else jnp.concatenate([x[:r0], x[r0:] + tail], axis=0)


def _q_cols(tq, first):
    return pl.BlockSpec((tq, LANE), lambda g, i: (i, first // LANE + g))


def _k_cols(rows, first):
    return pl.BlockSpec((rows, LANE), lambda g, i: (0, first // LANE + g))


def _attn_fwd(kind, q, k, v, n_heads, dh, ccol=None, crow=None):
    (qa, q0), (ka, k0), (va, v0) = q, k, v
    S, Sk = qa.shape[0], ka.shape[0]
    pack = LANE // dh
    tq, tc = _attn_blocks(kind, S, Sk)
    scale = dh ** -0.5
    fold = _is_power_of_two(scale)
    causal = kind != "mem"
    n_diag = tq // tc if causal else 0
    unroll = 2 if causal else 1
    assert n_diag % unroll == 0 and (Sk // tc) % unroll == 0

    def body(*refs):
        if kind == "fox":
            q_ref, k_ref, v_ref, cc_ref, cr_ref, o_ref, lse_ref = refs
        else:
            q_ref, k_ref, v_ref, o_ref, lse_ref = refs
        i = pl.program_id(1)
        n_full = (i * tq) // tc if causal else Sk // tc
        qpos = i * tq + lax.broadcasted_iota(jnp.int32, (tq, tc), 0)
        kio = lax.broadcasted_iota(jnp.int32, (tq, tc), 1)
        heads = range(pack)
        sel = _head_lanes(pack, dh)
        q2 = q_ref[...] * scale if fold else q_ref[...]
        qs = [_only_head(sel, hh, q2) for hh in heads]

        def kv(jc):
            off = pl.multiple_of(jc * tc, tc)
            return off, k_ref[pl.ds(off, tc), :], v_ref[pl.ds(off, tc), :]

        if kind == "sb":
            tri = (lax.broadcasted_iota(jnp.int32, (tc, tc), 0) > lax.broadcasted_iota(jnp.int32, (tc, tc), 1)
                   ).astype(BF16)

            def chunk(jc, r0, runs, acc):
                off, k2, v2 = kv(jc)
                new_runs, pv = [], []
                for hh in heads:
                    lb, ln = _sb_logs(_dot_nt(_tail(qs[hh], r0), k2))
                    if r0 is not None:
                        mask = (off + _tail(kio, r0)) < _tail(qpos, r0)
                        ln = jnp.where(mask, ln, 0.0)
                    w = jnp.exp(lb + _cumdot(ln, tri) + _tail(runs[hh], r0))
                    if r0 is not None:
                        w = jnp.where(mask, w, 0.0)
                    pv.append(_dot(w.astype(BF16), v2))
                    new_runs.append(_add_tail(runs[hh], jnp.sum(ln, axis=1, keepdims=True), r0))
                return tuple(new_runs), _add_tail(acc, _by_head(sel, pv), r0)

            state = (tuple(jnp.zeros((tq, 1), F32) for _ in heads), jnp.zeros((tq, LANE), F32))
            for d in range(n_diag - 1, -1, -1):
                state = chunk(n_full + d, d * tc, *state)

            def trip(t, st):
                for u in range(unroll):
                    st = chunk(n_full - 1 - unroll * t - u, None, *st)
                return st

            runs, acc = lax.fori_loop(0, n_full // unroll, trip, state)
            o_ref[...] = acc.astype(o_ref.dtype)
            for hh in heads:
                lse_ref[hh] = runs[hh]
        else:
            def chunk(jc, r0, ms, ls, acc):
                off, k2, v2 = kv(jc)
                new_ms, new_ls, alphas, pv = [], [], [], []
                for hh in heads:
                    z = _dot_nt(_tail(qs[hh], r0), k2)
                    if not fold:
                        z = z * scale
                    if kind == "fox":
                        z = z + _tail(cc_ref[hh], r0) - cr_ref[hh, pl.ds(jc, 1), :]
                    if r0 is not None:
                        z = jnp.where((off + _tail(kio, r0)) <= _tail(qpos, r0), z, NEG)
                    m_old, l_old = _tail(ms[hh], r0), _tail(ls[hh], r0)
                    m_new = jnp.maximum(m_old, jnp.max(z, axis=1, keepdims=True))
                    alpha = jnp.exp(m_old - m_new)
                    p = jnp.exp(z - m_new)
                    new_ms.append(_put_tail(ms[hh], m_new, r0))
                    new_ls.append(_put_tail(ls[hh], alpha * l_old + jnp.sum(p, axis=1, keepdims=True), r0))
                    alphas.append(alpha)
                    pv.append(_dot(p.astype(BF16), v2))
                acc_new = _by_head(sel, alphas) * _tail(acc, r0) + _by_head(sel, pv)
                return tuple(new_ms), tuple(new_ls), _put_tail(acc, acc_new, r0)

            state = (tuple(jnp.full((tq, 1), NEG, F32) for _ in heads), tuple(jnp.zeros((tq, 1), F32) for _ in heads),
                     jnp.zeros((tq, LANE), F32))

            def trip(t, st):
                for u in range(unroll):
                    st = chunk(unroll * t + u, None, *st)
                return st

            state = lax.fori_loop(0, n_full // unroll, trip, state)
            for d in range(n_diag):
                state = chunk(n_full + d, d * tc, *state)
            ms, ls, acc = state
            o_ref[...] = (acc / _by_head(sel, ls)).astype(o_ref.dtype)
            for hh in heads:
                lse_ref[hh] = ms[hh] + jnp.log(ls[hh])

    colspec = pl.BlockSpec((pack, tq, 1), lambda g, i: (g, i, 0))
    in_specs, args = [_q_cols(tq, q0), _k_cols(Sk, k0), _k_cols(Sk, v0)], [qa, ka, va]
    if kind == "fox":
        in_specs += [colspec, pl.BlockSpec((pack, Sk // tc, tc), lambda g, i: (g, 0, 0))]
        args += [ccol, crow]
    return pl.pallas_call(
        body, name="attn_fwd_" + kind, grid=(n_heads // pack, S // tq),
        in_specs=in_specs, out_specs=[_q_cols(tq, 0), colspec],
        out_shape=[jax.ShapeDtypeStruct((S, n_heads * dh), BF16), jax.ShapeDtypeStruct((n_heads, S, 1), F32)],
        compiler_params=_params(),
    )(*args)


def _attn_bwd(kind, q, k, v, o, do, n_heads, dh, ccol=None, crow=None, lse=None):
    (qa, q0), (ka, k0), (va, v0) = q, k, v
    S, Sk = qa.shape[0], ka.shape[0]
    pack = LANE // dh
    tq, tc = _attn_blocks(kind, S, Sk, backward=True)
    scale = dh ** -0.5
    fold = _is_power_of_two(scale)
    causal = kind != "mem"
    n_diag = tq // tc if causal else 0
    unroll = 2 if kind == "sb" else 1
    assert n_diag % unroll == 0 and (Sk // tc) % unroll == 0
    nq = S // tq

    def body(*refs):
        if kind == "fox":
            (q_ref, k_ref, v_ref, o_ref, do_ref, cc_ref, cr_ref, lse_ref,
             dq_ref, dk_ref, dv_ref, dc_ref, dcc_ref, dk_acc, dv_acc, dc_acc) = refs
        else:
            q_ref, k_ref, v_ref, o_ref, do_ref, lse_ref, dq_ref, dk_ref, dv_ref, dk_acc, dv_acc = refs
        i = pl.program_id(1)

        @pl.when(i == 0)
        def _():
            dk_acc[...] = jnp.zeros_like(dk_acc)
            dv_acc[...] = jnp.zeros_like(dv_acc)
            if kind == "fox":
                dc_acc[...] = jnp.zeros_like(dc_acc)

        n_full = (i * tq) // tc if causal else Sk // tc
        qpos = i * tq + lax.broadcasted_iota(jnp.int32, (tq, tc), 0)
        kio = lax.broadcasted_iota(jnp.int32, (tq, tc), 1)
        heads = range(pack)
        sel = _head_lanes(pack, dh)
        q2 = q_ref[...] * scale if fold else q_ref[...]
        do2 = do_ref[...]
        qs = [_only_head(sel, hh, q2) for hh in heads]
        dos = [_only_head(sel, hh, do2) for hh in heads]

        def kv(jc):
            off = pl.multiple_of(jc * tc, tc)
            return off, k_ref[pl.ds(off, tc), :], v_ref[pl.ds(off, tc), :]

        def accumulate(off, k2, dzb, wb, dq, r0):
            q2t, do2t = _tail(q2, r0), _tail(do2, r0)
            dk_acc[pl.ds(off, tc), :] += _by_head(sel, [_dot_tn(dzb[hh], q2t) for hh in heads])
            dv_acc[pl.ds(off, tc), :] += _by_head(sel, [_dot_tn(wb[hh], do2t) for hh in heads])
            return _add_tail(dq, _by_head(sel, [_dot(dzb[hh], k2) for hh in heads]), r0)

        if kind == "sb":
            r = lax.broadcasted_iota(jnp.int32, (tc, tc), 0)
            cidx = lax.broadcasted_iota(jnp.int32, (tc, tc), 1)
            tri_inc = (r <= cidx).astype(BF16)
            tri_exc = (r < cidx).astype(BF16)

            def chunk(jc, r0, pres, pres_e, dq):
                off, k2, v2 = kv(jc)
                new_pres, new_pres_e, dzb, wb = [], [], [], []
                for hh in heads:
                    lb, ln = _sb_logs(_dot_nt(_tail(qs[hh], r0), k2))
                    if r0 is not None:
                        mask = (off + _tail(kio, r0)) < _tail(qpos, r0)
                        ln = jnp.where(mask, ln, 0.0)
                    w = jnp.exp(lb + (_tail(lse_ref[hh], r0) - _tail(pres[hh], r0) - _cumdot(ln, tri_inc)))
                    if r0 is not None:
                        w = jnp.where(mask, w, 0.0)
                    e = w * _dot_nt(_tail(dos[hh], r0), v2)
                    beta = jnp.exp(lb)
                    dz = e * (1.0 - beta) - beta * (_tail(pres_e[hh], r0) + _cumdot(e, tri_exc))
                    if r0 is not None:
                        dz = jnp.where(mask, dz, 0.0)
                    dzb.append(dz.astype(BF16))
                    wb.append(w.astype(BF16))
                    new_pres.append(_add_tail(pres[hh], jnp.sum(ln, axis=1, keepdims=True), r0))
                    new_pres_e.append(_add_tail(pres_e[hh], jnp.sum(e, axis=1, keepdims=True), r0))
                return tuple(new_pres), tuple(new_pres_e), accumulate(off, k2, dzb, wb, dq, r0)

            state = (tuple(jnp.zeros((tq, 1), F32) for _ in heads), tuple(jnp.zeros((tq, 1), F32) for _ in heads),
                     jnp.zeros((tq, LANE), F32))
        else:
            prod = o_ref[...].astype(F32) * do2.astype(F32)
            dsum = [jnp.sum(_only_head(sel, hh, prod), axis=1, keepdims=True) for hh in heads]

            def chunk(jc, r0, rowsums, dq):
                off, k2, v2 = kv(jc)
                new_rowsums, dsb, pb = [], [], []
                for hh in heads:
                    z = _dot_nt(_tail(qs[hh], r0), k2)
                    if not fold:
                        z = z * scale
                    if kind == "fox":
                        z = z + _tail(cc_ref[hh], r0) - cr_ref[hh, pl.ds(jc, 1), :]
                    if r0 is not None:
                        z = jnp.where((off + _tail(kio, r0)) <= _tail(qpos, r0), z, NEG)
                    p = jnp.exp(z - _tail(lse_ref[hh], r0))
                    ds = p * (_dot_nt(_tail(dos[hh], r0), v2) - _tail(dsum[hh], r0))
                    dsb.append(ds.astype(BF16))
                    pb.append(p.astype(BF16))
                    if kind == "fox":
                        dc_acc[hh, pl.ds(jc, 1), :] -= jnp.sum(ds, axis=0, keepdims=True)
                        new_rowsums.append(_add_tail(rowsums[hh], jnp.sum(ds, axis=1, keepdims=True), r0))
                    else:
                        new_rowsums.append(rowsums[hh])
                return tuple(new_rowsums), accumulate(off, k2, dsb, pb, dq, r0)

            state = (tuple(jnp.zeros((tq, 1), F32) for _ in heads), jnp.zeros((tq, LANE), F32))

        def trip(t, st):
            for u in range(unroll):
                st = chunk(unroll * t + u, None, *st)
            return st

        state = lax.fori_loop(0, n_full // unroll, trip, state)
        for d in range(n_diag):
            state = chunk(n_full + d, d * tc, *state)
        dq_ref[...] = (state[-1] * scale).astype(dq_ref.dtype)
        if kind == "fox":
            for hh in heads:
                dcc_ref[hh] = state[0][hh]

        @pl.when(i == nq - 1)
        def _():
            dk = dk_acc[...] if fold else dk_acc[...] * scale
            dk_ref[...] = dk.astype(dk_ref.dtype)
            dv_ref[...] = dv_acc[...].astype(dv_ref.dtype)
            if kind == "fox":
                dc_ref[...] = dc_acc[...]

    colspec = pl.BlockSpec((pack, tq, 1), lambda g, i: (g, i, 0))
    rowspec = pl.BlockSpec((pack, Sk // tc, tc), lambda g, i: (g, 0, 0))
    in_specs = [_q_cols(tq, q0), _k_cols(Sk, k0), _k_cols(Sk, v0), _q_cols(tq, 0), _q_cols(tq, 0)]
    args = [qa, ka, va, o, do]
    if kind == "fox":
        in_specs += [colspec, rowspec]
        args += [ccol, crow]
    in_specs += [colspec]
    args += [lse]
    width = n_heads * dh
    out_specs = [_q_cols(tq, 0), _k_cols(Sk, 0), _k_cols(Sk, 0)]
    out_shape = [jax.ShapeDtypeStruct((S, width), BF16), jax.ShapeDtypeStruct((Sk, width), BF16),
                 jax.ShapeDtypeStruct((Sk, width), BF16)]
    scratch = [pltpu.VMEM((Sk, LANE), F32), pltpu.VMEM((Sk, LANE), F32)]
    if kind == "fox":
        out_specs += [rowspec, colspec]
        out_shape += [jax.ShapeDtypeStruct((n_heads, Sk // tc, tc), F32), jax.ShapeDtypeStruct((n_heads, S, 1), F32)]
        scratch.append(pltpu.VMEM((pack, Sk // tc, tc), F32))
    return pl.pallas_call(
        body, name="attn_bwd_" + kind, grid=(n_heads // pack, nq),
        in_specs=in_specs, out_specs=out_specs, out_shape=out_shape, scratch_shapes=scratch,
        compiler_params=_params(),
    )(*args)


def _mem_norm(mem, g):
    M, D = mem.shape

    def body(mem_ref, g_ref, out_ref):
        out_ref[...] = _rms(mem_ref[...], g_ref[...]).astype(BF16)

    return pl.pallas_call(
        body, name="mem_norm", grid=(1,),
        in_specs=[_whole((M, D)), _whole((1, D))], out_specs=_whole((M, D)),
        out_shape=jax.ShapeDtypeStruct((M, D), BF16), compiler_params=_params(),
    )(mem, g)


def _mem_norm_bwd(mem, g, dmem_n):
    M, D = mem.shape
    L = dmem_n.shape[0]

    def body(mem_ref, g_ref, d_ref, dg_ref):
        d = d_ref[0]
        for l in range(1, L):
            d = d + d_ref[l]
        _, vjp = jax.vjp(_rms, mem_ref[...], g_ref[...])
        dg_ref[...] = vjp(d)[1]

    return pl.pallas_call(
        body, name="mem_norm_bwd", grid=(1,),
        in_specs=[_whole((M, D)), _whole((1, D)), _whole((L, M, D))], out_specs=_whole((1, D)),
        out_shape=jax.ShapeDtypeStruct((1, D), F32), compiler_params=_params(),
    )(mem, g, dmem_n)


def _loss_head(h, target):
    S, D = h.shape
    bm = _pick(S, 512)

    def body(h_ref, t_ref, dh_ref, loss_ref):
        err = h_ref[...] - t_ref[...]
        dh_ref[...] = err * (1.0 / D)

        @pl.when(pl.program_id(0) == 0)
        def _():
            loss_ref[...] = jnp.zeros_like(loss_ref)

        loss_ref[...] += 0.5 * jnp.sum(jnp.mean(err * err, axis=-1, keepdims=True), axis=0, keepdims=True)

    return pl.pallas_call(
        body, name="loss_head", grid=(S // bm,),
        in_specs=[_rows(bm, D), _rows(bm, D)], out_specs=[_rows(bm, D), _whole((8, LANE))],
        out_shape=[jax.ShapeDtypeStruct((S, D), F32), jax.ShapeDtypeStruct((8, LANE), F32)],
        compiler_params=_params(),
    )(h, target)


def _adamw(w, g, m, v, name):
    R, C = w.shape
    rb = R if R * C * 4 <= (1 << 20) else _pick(R, 256)
    if R % rb:
        rb = R
    c1 = 1.0 - ADAM_B1 ** ADAM_STEP
    c2 = 1.0 - ADAM_B2 ** ADAM_STEP

    def body(w_ref, g_ref, m_ref, v_ref, d_ref, mo_ref, vo_ref):
        gv = g_ref[...]
        mn = ADAM_B1 * m_ref[...] + (1.0 - ADAM_B1) * gv
        vn = ADAM_B2 * v_ref[...] + (1.0 - ADAM_B2) * (gv * gv)
        mo_ref[...] = mn
        vo_ref[...] = vn
        d_ref[...] = -ADAM_LR * ((mn / c1) / (jnp.sqrt(vn / c2) + ADAM_EPS) + ADAM_WD * w_ref[...])

    return pl.pallas_call(
        body, name=name, grid=(R // rb,),
        in_specs=[_rows(rb, C)] * 4, out_specs=[_rows(rb, C)] * 3,
        out_shape=[jax.ShapeDtypeStruct((R, C), F32)] * 3, compiler_params=_params(),
    )(w, g, m, v)


def _adamw_reduced(w, m, v, mine, theirs, c_idx, first_row, name):
    L, a, b = w.shape
    Lh = L // 2
    rb = _shard_row_block(a)
    nb = a // rb
    assert first_row % rb == 0
    c1 = 1.0 - ADAM_B1 ** ADAM_STEP
    c2 = 1.0 - ADAM_B2 ** ADAM_STEP

    def own(i, c_ref):
        return (i, 0)

    def reduced(i, c_ref):
        return (first_row // rb + ((i // nb) % Lh) * nb + i % nb, 0)

    def body(c_ref, w_ref, m_ref, v_ref, mine_ref, theirs_ref, g_ref, d_ref, mo_ref, vo_ref):
        half = (pl.program_id(0) // nb) // Lh
        gv = jnp.where(c_ref[0] == half, mine_ref[...], theirs_ref[...])
        g_ref[...] = gv
        mn = ADAM_B1 * m_ref[...] + (1.0 - ADAM_B1) * gv
        vn = ADAM_B2 * v_ref[...] + (1.0 - ADAM_B2) * (gv * gv)
        mo_ref[...] = mn
        vo_ref[...] = vn
        d_ref[...] = -ADAM_LR * ((mn / c1) / (jnp.sqrt(vn / c2) + ADAM_EPS) + ADAM_WD * w_ref[...])

    outs = pl.pallas_call(
        body, name=name,
        grid_spec=pltpu.PrefetchScalarGridSpec(
            num_scalar_prefetch=1, grid=(L * nb,),
            in_specs=[pl.BlockSpec((rb, b), own)] * 3 + [pl.BlockSpec((rb, b), reduced)] * 2,
            out_specs=[pl.BlockSpec((rb, b), own)] * 4),
        out_shape=[jax.ShapeDtypeStruct((L * a, b), F32)] * 4, compiler_params=_params(),
    )(c_idx, w.reshape(L * a, b), m.reshape(L * a, b), v.reshape(L * a, b), mine, theirs)
    return [t.reshape(L, a, b) for t in outs]


ANY = pl.BlockSpec(memory_space=pl.ANY)
MESH = pl.DeviceIdType.MESH


def _place():
    x, y, c = lax.axis_index("x"), lax.axis_index("y"), lax.axis_index("c")
    others = [(1 - x, y), (x, 1 - y), (1 - x, 1 - y)]
    return x, y, c, others


def _place_own(loc, chip_idx):
    _, R, C = loc.shape
    rb = _pick(R, 2 * FLAT_ROW_BLOCK)

    def body(chip_ref, loc_ref, out_ref):
        out_ref[...] = loc_ref[...]

    return pl.pallas_call(
        body, name="place_own",
        grid_spec=pltpu.PrefetchScalarGridSpec(
            num_scalar_prefetch=1, grid=(2, R // rb),
            in_specs=[pl.BlockSpec((None, rb, C), lambda hf, i, chip_ref: (hf, i, 0))],
            out_specs=pl.BlockSpec((None, None, rb, C), lambda hf, i, chip_ref: (chip_ref[0], hf, i, 0))),
        out_shape=jax.ShapeDtypeStruct((N_CHIPS, 2, R, C), loc.dtype), compiler_params=_params(),
    )(chip_idx, loc)


def _gather_weights(locs, owns):
    n = len(locs)

    def body(*refs):
        loc_refs, out_refs, (send_sems, recv_sems) = refs[:n], refs[2 * n:3 * n], refs[3 * n:]
        x, y, c, others = _place()
        me = 2 * x + y
        sibling = (x, y, 1 - c)

        def copy(a, k, src, dst, to):
            return pltpu.make_async_remote_copy(src_ref=src, dst_ref=dst, send_sem=send_sems.at[a, k],
                                                recv_sem=recv_sems.at[a, k], device_id=to, device_id_type=MESH)

        first = [copy(a, j, loc_refs[a].at[c], out_refs[a].at[me, c], (ox, oy, c))
                 for j, (ox, oy) in enumerate(others) for a in range(n)]
        for cp in first:
            cp.start()
        passed = []
        for j, (ox, oy) in enumerate(others):
            for a in range(n):
                landed = out_refs[a].at[2 * ox + oy, c]
                copy(a, j, loc_refs[a].at[c], landed, sibling).wait_recv()
                cp = copy(a, 3 + j, landed, landed, sibling)
                cp.start()
                passed.append(cp)
        for j, (ox, oy) in enumerate(others):
            for a in range(n):
                copy(a, 3 + j, loc_refs[a].at[c], out_refs[a].at[2 * ox + oy, 1 - c], sibling).wait_recv()
        for cp in first + passed:
            cp.wait_send()

    return pl.pallas_call(
        body, name="gather_weights", in_specs=[ANY] * (2 * n), out_specs=[ANY] * n,
        out_shape=[jax.ShapeDtypeStruct(own.shape, own.dtype) for own in owns],
        input_output_aliases={n + a: a for a in range(n)},
        scratch_shapes=[pltpu.SemaphoreType.DMA((n, 6)), pltpu.SemaphoreType.DMA((n, 6))],
    )(*locs, *owns)


def _pair_exchange(gs):
    n = len(gs)

    def body(*refs):
        g_refs, out_refs, (send_sems, recv_sems) = refs[:n], refs[n:2 * n], refs[2 * n:]
        x, y, c, _ = _place()
        copies = [pltpu.make_async_remote_copy(src_ref=g_refs[a].at[1 - c], dst_ref=out_refs[a],
                                               send_sem=send_sems.at[a], recv_sem=recv_sems.at[a],
                                               device_id=(x, y, 1 - c), device_id_type=MESH) for a in range(n)]
        for cp in copies:
            cp.start()
        for cp in copies:
            cp.wait()

    return pl.pallas_call(
        body, name="pair_exchange", in_specs=[ANY] * n, out_specs=[ANY] * n,
        out_shape=[jax.ShapeDtypeStruct(g.shape[1:], g.dtype) for g in gs],
        scratch_shapes=[pltpu.SemaphoreType.DMA((n,)), pltpu.SemaphoreType.DMA((n,))],
    )(*gs)


def _pair_sum(g, sib, c_idx):
    _, _, R, C = g.shape
    rb = _pick(R, 512)

    def body(c_ref, g_ref, s_ref, o_ref):
        o_ref[...] = (g_ref[...].astype(F32) + s_ref[...].astype(F32)).astype(o_ref.dtype)

    return pl.pallas_call(
        body, name="pair_sum",
        grid_spec=pltpu.PrefetchScalarGridSpec(
            num_scalar_prefetch=1, grid=(N_CHIPS, R // rb),
            in_specs=[pl.BlockSpec((None, None, rb, C), lambda j, i, c_ref: (c_ref[0], j, i, 0)),
                      pl.BlockSpec((None, rb, C), lambda j, i, c_ref: (j, i, 0))],
            out_specs=pl.BlockSpec((None, rb, C), lambda j, i, c_ref: (j, i, 0))),
        out_shape=jax.ShapeDtypeStruct((N_CHIPS, R, C), g.dtype), compiler_params=_params(),
    )(c_idx, g, sib)


def _chip_exchange(ps):
    n = len(ps)

    def body(*refs):
        p_refs, out_refs, (send_sems, recv_sems) = refs[:n], refs[n:2 * n], refs[2 * n:]
        x, y, c, others = _place()
        copies = []
        for j, (ox, oy) in enumerate(others):
            for a in range(n):
                cp = pltpu.make_async_remote_copy(src_ref=p_refs[a].at[2 * ox + oy], dst_ref=out_refs[a].at[j],
                                                  send_sem=send_sems.at[a, j], recv_sem=recv_sems.at[a, j],
                                                  device_id=(ox, oy, c), device_id_type=MESH)
                cp.start()
                copies.append(cp)
        for cp in copies:
            cp.wait()

    return pl.pallas_call(
        body, name="chip_exchange", in_specs=[ANY] * n, out_specs=[ANY] * n,
        out_shape=[jax.ShapeDtypeStruct((N_CHIPS - 1,) + p.shape[1:], p.dtype) for p in ps],
        scratch_shapes=[pltpu.SemaphoreType.DMA((n, 3)), pltpu.SemaphoreType.DMA((n, 3))],
    )(*ps)


def _chip_sum(p, r, chip_idx):
    _, R, C = r.shape
    rb = _pick(R, 512)

    def body(chip_ref, p_ref, r_ref, o_ref):
        acc = p_ref[...].astype(F32)
        for j in range(N_CHIPS - 1):
            acc = acc + r_ref[j].astype(F32)
        o_ref[...] = acc

    return pl.pallas_call(
        body, name="chip_sum",
        grid_spec=pltpu.PrefetchScalarGridSpec(
            num_scalar_prefetch=1, grid=(R // rb,),
            in_specs=[pl.BlockSpec((None, rb, C), lambda i, chip_ref: (chip_ref[0], i, 0)),
                      pl.BlockSpec((N_CHIPS - 1, rb, C), lambda i, chip_ref: (0, i, 0))],
            out_specs=pl.BlockSpec((rb, C), lambda i, chip_ref: (i, 0))),
        out_shape=jax.ShapeDtypeStruct((R, C), F32), compiler_params=_params(),
    )(chip_idx, p, r)


def _pair_swap(rhs):
    n = len(rhs)

    def body(*refs):
        rh_refs, out_refs, (send_sems, recv_sems) = refs[:n], refs[n:2 * n], refs[2 * n:]
        x, y, c, _ = _place()
        copies = [pltpu.make_async_remote_copy(src_ref=rh_refs[a], dst_ref=out_refs[a], send_sem=send_sems.at[a],
                                               recv_sem=recv_sems.at[a], device_id=(x, y, 1 - c),
                                               device_id_type=MESH) for a in range(n)]
        for cp in copies:
            cp.start()
        for cp in copies:
            cp.wait()

    return pl.pallas_call(
        body, name="pair_swap", in_specs=[ANY] * n, out_specs=[ANY] * n,
        out_shape=[jax.ShapeDtypeStruct(rh.shape, rh.dtype) for rh in rhs],
        scratch_shapes=[pltpu.SemaphoreType.DMA((n,)), pltpu.SemaphoreType.DMA((n,))],
    )(*rhs)


def _all_reduce_small(s):
    R, C = s.shape

    def body(s_ref, o_ref, buf, send_sems, recv_sems):
        x, y, c, _ = _place()
        me = 4 * x + 2 * y + c
        sends = []
        for k in range(1, N_DEV):
            fx, fy, fc = (k >> 2) & 1, (k >> 1) & 1, k & 1
            to = (x ^ fx, y ^ fy, c ^ fc)
            cp = pltpu.make_async_remote_copy(src_ref=s_ref, dst_ref=buf.at[me], send_sem=send_sems.at[k - 1],
                                              recv_sem=recv_sems.at[k - 1], device_id=to, device_id_type=MESH)
            cp.start()
            sends.append(cp)
        buf[me] = s_ref[...]
        for k in range(1, N_DEV):
            fx, fy, fc = (k >> 2) & 1, (k >> 1) & 1, k & 1
            frm = 4 * (x ^ fx) + 2 * (y ^ fy) + (c ^ fc)
            pltpu.make_async_remote_copy(src_ref=s_ref, dst_ref=buf.at[frm], send_sem=send_sems.at[k - 1],
                                         recv_sem=recv_sems.at[k - 1], device_id=(x, y, c),
                                         device_id_type=MESH).wait_recv()
        acc = buf[0]
        for d in range(1, N_DEV):
            acc = acc + buf[d]
        o_ref[...] = acc
        for cp in sends:
            cp.wait_send()

    vm = pl.BlockSpec(memory_space=pltpu.VMEM)
    return pl.pallas_call(
        body, name="all_reduce_small", in_specs=[vm], out_specs=vm,
        out_shape=jax.ShapeDtypeStruct((R, C), F32),
        scratch_shapes=[pltpu.VMEM((N_DEV, R, C), F32), pltpu.SemaphoreType.DMA((N_DEV - 1,)),
                        pltpu.SemaphoreType.DMA((N_DEV - 1,))],
    )(s)


def _padded(n):
    return -(-n // FLAT_UNIT) * FLAT_UNIT


def _pack_flat(pieces, dtype, row_block=FLAT_ROW_BLOCK):
    flat = []
    for p in pieces:
        p = p.reshape(-1).astype(dtype)
        flat.append(jnp.pad(p, (0, _padded(p.size) - p.size)))
    total = sum(p.size for p in flat)
    flat.append(jnp.zeros((-total) % (row_block * FLAT_COLS), dtype))
    return jnp.concatenate(flat).reshape(-1, FLAT_COLS)


def _unpack_flat(flat, shapes):
    lead = flat.shape[:-2]
    flat = flat.reshape(lead + (-1,))
    out, off = [], 0
    for shp in shapes:
        n = math.prod(shp)
        out.append(flat[..., off:off + n].reshape(lead + tuple(shp)))
        off += _padded(n)
    return out


def _shard_row_block(a):
    for rb in range(min(a, 512) // 16 * 16, 0, -16):
        if a % rb == 0:
            return rb
    return a


def _row_layout(shapes, n_layers):
    groups = {}
    for name, (a, b) in shapes.items():
        names, first, rows = groups.get(b, ((), {}, 0))
        rb = _shard_row_block(a)
        start = -(-rows // rb) * rb
        groups[b] = (names + (name,), {**first, name: start}, start + n_layers * a)
    return {b: (names, first, -(-rows // FLAT_ROW_BLOCK) * FLAT_ROW_BLOCK) for b, (names, first, rows) in groups.items()}


def _pack_rows(group, width, pieces, dtype):
    names, first, rows = group
    parts, at = [], 0
    for name in names:
        if first[name] > at:
            parts.append(jnp.zeros((first[name] - at, width), dtype))
        parts.append(pieces[name].astype(dtype))
        at = first[name] + pieces[name].shape[0]
    if rows > at:
        parts.append(jnp.zeros((rows - at, width), dtype))
    return jnp.concatenate(parts, axis=0)


def _slab(t, axis, j):
    if t.ndim == 3:
        return t[j]
    n = t.shape[axis - 1] // N_CHIPS
    return lax.slice_in_dim(t, j * n, (j + 1) * n, axis=axis - 1)


def _layer_fwd(h0, mem_n, wl, dims):
    n_sb, n_fx, n_mem, sbw, fxw, memw = dims
    n1, gate1, up1, a1 = _ffn_fwd_up(h0, wl["ffn1_pre_g"], wl["ffn1_w_gate"], wl["ffn1_w_up"])
    h1, f1 = _ffn_fwd_down(a1, wl["ffn1_w_down"], h0, wl["ffn1_post_g"])

    u, proj, fl, sg = _mix_fwd_in(h1, wl["mix_pre_g"], wl["w_in"], wl["w_gate"], wl["b_gate"], wl["b_forget"])
    c = _fox_cumsum(fl)
    S = h0.shape[0]
    tc = _attn_blocks("fox", S, S)[1]
    ct = c[:, :n_fx].T
    ccol, crow = ct.reshape(n_fx, S, 1), ct.reshape(n_fx, S // tc, tc)
    qkv_sb = [(proj, k * sbw) for k in range(3)]
    qkv_fx = [(proj, 3 * sbw + k * fxw) for k in range(3)]
    kv = _matmul(mem_n, wl["w_mem_kv"], out_dtype=BF16, name="mem_kv")
    qkv_mem = [(proj, 3 * sbw + 3 * fxw), (kv, 0), (kv, memw)]
    o_sb, tot_sb = _attn_fwd("sb", *qkv_sb, n_sb, HEAD_DIM)
    o_fx, lse_fx = _attn_fwd("fox", *qkv_fx, n_fx, HEAD_DIM, ccol, crow)
    o_mem, lse_mem = _attn_fwd("mem", *qkv_mem, n_mem, MEM_HEAD_DIM)
    h2, zmix, merged = _mix_fwd_out(o_sb, o_fx, o_mem, sg, wl["w_br_sb"], wl["w_br_fox"], wl["w_br_mem"],
                                    wl["w_out"], h1, wl["mix_post_g"])

    n2, gate2, up2, a2 = _ffn_fwd_up(h2, wl["ffn2_pre_g"], wl["ffn2_w_gate"], wl["ffn2_w_up"])
    h3, f2 = _ffn_fwd_down(a2, wl["ffn2_w_down"], h2, wl["ffn2_post_g"])
    saved = dict(h0=h0, n1=n1, gate1=gate1, up1=up1, a1=a1, f1=f1, h1=h1, u=u, fl=fl, sg=sg,
                 qkv_sb=qkv_sb, qkv_fx=qkv_fx, qkv_mem=qkv_mem, ccol=ccol, crow=crow, o_sb=o_sb, o_fx=o_fx, o_mem=o_mem,
                 tot_sb=tot_sb, lse_fx=lse_fx, lse_mem=lse_mem,
                 zmix=zmix, merged=merged, h2=h2, n2=n2, gate2=gate2, up2=up2, a2=a2, f2=f2)
    return h3, saved


def _ffn_bwd(dh, sv, wl, tag, h_in):
    n, gate, up, a, f = (sv[k + tag] for k in ("n", "gate", "up", "a", "f"))
    pre = "ffn" + tag
    df, dgate, dup, dg_post = _ffn_bwd_down(dh, f, wl[pre + "_post_g"], wl[pre + "_w_down"], gate, up)
    dh_in, dg_pre = _ffn_bwd_up(dgate, dup, wl[pre + "_w_gate"], wl[pre + "_w_up"], h_in, wl[pre + "_pre_g"], dh)
    grads = {pre + "_post_g": dg_post, pre + "_pre_g": dg_pre,
             pre + "_w_down": _matmul(a, df, ta=True, batch="a", name="dw_down"),
             pre + "_w_gate": _matmul(n, dgate, ta=True, batch="b", name="dw_gate"),
             pre + "_w_up": _matmul(n, dup, ta=True, batch="b", name="dw_up")}
    return dh_in, grads


def _layer_bwd(dh3, mem_n, wl, sv, dims):
    n_sb, n_fx, n_mem, sbw, fxw, memw = dims
    S = dh3.shape[0]
    dh2, grads = _ffn_bwd(dh3, sv, wl, "2", sv["h2"])

    (dz, db_sb, db_fx, db_mem, do_sb, do_fx, do_mem, dgp, db_gate, dg_post) = _mix_bwd_out(
        dh2, sv["zmix"], wl["mix_post_g"], wl["w_out"], sv["o_sb"], sv["o_fx"], sv["o_mem"],
        wl["w_br_sb"], wl["w_br_fox"], wl["w_br_mem"], sv["sg"])
    grads["mix_post_g"] = dg_post
    grads["b_gate"] = db_gate
    grads["w_out"] = _matmul(sv["merged"], dz, ta=True, name="dw_out")
    grads["w_br_sb"] = _matmul(sv["o_sb"], db_sb, ta=True, name="dw_br_sb")
    grads["w_br_fox"] = _matmul(sv["o_fx"], db_fx, ta=True, name="dw_br_fox")
    grads["w_br_mem"] = _matmul(sv["o_mem"], db_mem, ta=True, name="dw_br_mem")

    dq_sb, dk_sb, dv_sb = _attn_bwd("sb", *sv["qkv_sb"], sv["o_sb"], do_sb, n_sb, HEAD_DIM, lse=sv["tot_sb"])
    dq_fx, dk_fx, dv_fx, dcrow, dccol = _attn_bwd("fox", *sv["qkv_fx"], sv["o_fx"], do_fx, n_fx, HEAD_DIM,
                                                  sv["ccol"], sv["crow"], sv["lse_fx"])
    dq_mem, dk_mem, dv_mem = _attn_bwd("mem", *sv["qkv_mem"], sv["o_mem"], do_mem, n_mem, MEM_HEAD_DIM,
                                       lse=sv["lse_mem"])
    dkv = jnp.concatenate([dk_mem, dv_mem], axis=1)
    grads["w_mem_kv"] = _matmul(mem_n, dkv, ta=True, name="dw_mem_kv")
    dmem_n = _matmul(dkv, wl["w_mem_kv"], tb=True, out_dtype=F32, name="dmem_n")

    dc = jnp.pad((dcrow.reshape(n_fx, S) + dccol.reshape(n_fx, S)).T, ((0, 0), (0, LANE - n_fx)))
    dfl, db_forget = _fox_dlogit(dc, sv["fl"])
    grads["b_forget"] = db_forget
    dproj = jnp.concatenate([dq_sb, dk_sb, dv_sb, dq_fx, dk_fx, dv_fx, dq_mem, dfl], axis=1)
    dh1, dg_pre = _mix_bwd_in(dproj, dgp, wl["w_in"], wl["w_gate"], sv["h1"], wl["mix_pre_g"], dh2)
    grads["mix_pre_g"] = dg_pre
    grads["w_in"] = _matmul(sv["u"], dproj, ta=True, name="dw_in")
    grads["w_gate"] = _matmul(sv["u"], dgp, ta=True, name="dw_gate_mix")

    dh0, g1 = _ffn_bwd(dh1, sv, wl, "1", sv["h0"])
    grads.update(g1)
    return dh0, grads, dmem_n


def kernel(x, mem, ffn1_pre_g, ffn1_post_g, ffn1_w_gate, ffn1_w_up, ffn1_w_down, mix_pre_g, mix_post_g, w_in, b_forget, mem_norm_g, w_mem_kv, w_gate, b_gate, w_br_sb, w_br_fox, w_br_mem, w_out, ffn2_pre_g, ffn2_post_g, ffn2_w_gate, ffn2_w_up, ffn2_w_down, loss_target, m_ffn1_pre_g, m_ffn1_post_g, m_ffn1_w_gate, m_ffn1_w_up, m_ffn1_w_down, m_mix_pre_g, m_mix_post_g, m_w_in, m_b_forget, m_mem_norm_g, m_w_mem_kv, m_w_gate, m_b_gate, m_w_br_sb, m_w_br_fox, m_w_br_mem, m_w_out, m_ffn2_pre_g, m_ffn2_post_g, m_ffn2_w_gate, m_ffn2_w_up, m_ffn2_w_down, v_ffn1_pre_g, v_ffn1_post_g, v_ffn1_w_gate, v_ffn1_w_up, v_ffn1_w_down, v_mix_pre_g, v_mix_post_g, v_w_in, v_b_forget, v_mem_norm_g, v_w_mem_kv, v_w_gate, v_b_gate, v_w_br_sb, v_w_br_fox, v_w_br_mem, v_w_out, v_ffn2_pre_g, v_ffn2_post_g, v_ffn2_w_gate, v_ffn2_w_up, v_ffn2_w_down):
    args = dict(locals())
    w = {n: args[n] for n in WEIGHTS}
    m = {n: args["m_" + n] for n in WEIGHTS}
    v = {n: args["v_" + n] for n in WEIGHTS}
    L = w["ffn1_pre_g"].shape[0]
    Lh = L // 2
    D = x.shape[2]
    sbw, fxw, memw = w["w_br_sb"].shape[1], w["w_br_fox"].shape[1], w["w_br_mem"].shape[1]
    n_sb, n_fx, n_mem = sbw // HEAD_DIM, fxw // HEAD_DIM, memw // MEM_HEAD_DIM
    dims = (n_sb, n_fx, n_mem, sbw, fxw, memw)
    qkv_w = 3 * sbw + 3 * fxw
    c_idx = lax.axis_index("c")
    c_arr = c_idx.reshape(1).astype(jnp.int32)
    chip_arr = (2 * lax.axis_index("x") + lax.axis_index("y")).reshape(1).astype(jnp.int32)

    shard_shapes = {n: w[n].shape[1:] for n, _ in BIG}
    layout = _row_layout(shard_shapes, Lh)
    widths = list(layout)
    locs = [jnp.stack([_pack_rows(layout[b], b, {n: w[n][hf * Lh:(hf + 1) * Lh].reshape(-1, b) for n in layout[b][0]},
                                  BF16) for hf in range(2)]) for b in widths]
    gathered = dict(zip(widths, _gather_weights(locs, [_place_own(loc, chip_arr) for loc in locs])))

    def layer_weights(l):
        hf, li = divmod(l, Lh)
        wl = {}
        for n, axis in BIG:
            a, b = shard_shapes[n]
            r0 = layout[b][1][n] + li * a
            shards = gathered[b][:, hf, r0:r0 + a]
            if n.startswith("ffn"):
                wl[n] = shards
            else:
                wl[n] = (shards.transpose(1, 0, 2).reshape(a, N_CHIPS * b) if axis == 2 else
                         shards.reshape(N_CHIPS * a, b))
        wi = wl["w_in"]
        wl["w_in"] = jnp.concatenate([wi[:, :qkv_w], wi[:, qkv_w + n_fx:], wi[:, qkv_w:qkv_w + n_fx],
                                      jnp.zeros((D, LANE - n_fx), BF16)], axis=1)
        for n in SMALL:
            if n != "mem_norm_g":
                wl[n] = w[n][l][None, :]
        wl["b_forget"] = jnp.pad(wl["b_forget"], ((0, 0), (0, LANE - n_fx)))
        return wl

    g_mem = w["mem_norm_g"][None, :]

    mem_n = _mem_norm(mem[0], g_mem)
    h, wls, saved = x[0], [], []
    for l in range(L):
        wls.append(layer_weights(l))
        h, sv = _layer_fwd(h, mem_n, wls[l], dims)
        saved.append(sv)
    dh, loss_tile = _loss_head(h, loss_target[0])
    loss = lax.psum(loss_tile[0, 0], ("x", "y", "c"))
    gl, dmem_n = [None] * L, [None] * L
    for l in reversed(range(L)):
        dh, gl[l], dmem_n[l] = _layer_bwd(dh, mem_n, wls[l], saved[l], dims)
        gi = gl[l]["w_in"]
        gl[l]["w_in"] = jnp.concatenate([gi[:, :qkv_w], gi[:, qkv_w + memw:qkv_w + memw + n_fx],
                                         gi[:, qkv_w:qkv_w + memw]], axis=1)
    grad_x = dh
    g_mem_norm = _mem_norm_bwd(mem[0], g_mem, jnp.stack(dmem_n))

    axis_of = dict(BIG)
    partials = [jnp.stack([jnp.stack([
        _pack_rows(layout[b], b, {n: jnp.concatenate([_slab(gl[hf * Lh + li][n], axis_of[n], j) for li in range(Lh)])
                                  for n in layout[b][0]}, BF16)
        for j in range(N_CHIPS)]) for hf in range(2)]) for b in widths]
    pairs = [_pair_sum(g, sib, c_arr) for g, sib in zip(partials, _pair_exchange(partials))]
    mines = [_chip_sum(p, r, chip_arr) for p, r in zip(pairs, _chip_exchange(pairs))]
    theirs = _pair_swap(mines)

    grad, delta, new_m, new_v = {}, {}, {}, {}
    for n, _ in BIG:
        k = widths.index(shard_shapes[n][1])
        grad[n], delta[n], new_m[n], new_v[n] = _adamw_reduced(
            w[n], m[n], v[n], mines[k], theirs[k], c_arr, layout[widths[k]][1][n], name="adamw_" + n)

    small_local = {n: (g_mem_norm if n == "mem_norm_g" else
                       jnp.concatenate([gl[l][n][:, :n_fx] if n == "b_forget" else gl[l][n] for l in range(L)]))
                   for n in SMALL}
    small_shapes = [small_local[n].shape for n in SMALL]
    small_sum = _unpack_flat(_all_reduce_small(_pack_flat([small_local[n] for n in SMALL], F32, row_block=16)),
                             small_shapes)
    for n, t in zip(SMALL, small_sum):
        shp = w[n].shape
        two_d = (1, shp[0]) if len(shp) == 1 else shp
        grad[n] = t.reshape(shp)
        d_, m_, v_ = _adamw(w[n].reshape(two_d), t.reshape(two_d), m[n].reshape(two_d), v[n].reshape(two_d),
                            name="adamw_" + n)
        delta[n], new_m[n], new_v[n] = d_.reshape(shp), m_.reshape(shp), v_.reshape(shp)

    return (loss, grad_x[None], *[grad[n] for n in WEIGHTS], *[delta[n] for n in WEIGHTS],
            *[new_m[n] for n in WEIGHTS], *[new_v[n] for n in WEIGHTS])
```

```python
import math

import jax
import jax.numpy as jnp
from jax import lax
from jax.experimental import pallas as pl
from jax.experimental.pallas import tpu as pltpu

F32 = jnp.float32
BF16 = jnp.bfloat16
RMS_EPS = 1e-6
HEAD_DIM = 64
MEM_HEAD_DIM = 128
LANE = 128
V7X_VMEM_LIMIT_BYTES = 56 * 1024 * 1024
FLAT_COLS = 512
FLAT_UNIT = 16 * FLAT_COLS
FLAT_ROW_BLOCK = 512
N_CHIPS = 4
N_DEV = 8
NEG = float(jnp.finfo(jnp.float32).min)

ADAM_LR = 0.001
ADAM_B1 = 0.9
ADAM_B2 = 0.999
ADAM_EPS = 1e-08
ADAM_WD = 0.01
ADAM_STEP = 10

BIG = (("ffn1_w_gate", 2), ("ffn1_w_up", 2), ("ffn1_w_down", 1), ("w_in", 2), ("w_mem_kv", 1), ("w_gate", 2),
       ("w_br_sb", 2), ("w_br_fox", 2), ("w_br_mem", 2), ("w_out", 1),
       ("ffn2_w_gate", 2), ("ffn2_w_up", 2), ("ffn2_w_down", 1))
SMALL = ("ffn1_pre_g", "ffn1_post_g", "mix_pre_g", "mix_post_g", "b_forget", "mem_norm_g", "b_gate",
         "ffn2_pre_g", "ffn2_post_g")
WEIGHTS = ("ffn1_pre_g", "ffn1_post_g", "ffn1_w_gate", "ffn1_w_up", "ffn1_w_down", "mix_pre_g", "mix_post_g", "w_in",
           "b_forget", "mem_norm_g", "w_mem_kv", "w_gate", "b_gate", "w_br_sb", "w_br_fox", "w_br_mem", "w_out",
           "ffn2_pre_g", "ffn2_post_g", "ffn2_w_gate", "ffn2_w_up", "ffn2_w_down")


def _params(**kw):
    return pltpu.CompilerParams(vmem_limit_bytes=V7X_VMEM_LIMIT_BYTES, **kw)


def _dot(a, b):
    return jnp.dot(a, b, preferred_element_type=F32)


def _dot_nt(a, b):
    return lax.dot_general(a, b, (((1,), (1,)), ((), ())), preferred_element_type=F32)


def _dot_tn(a, b):
    return lax.dot_general(a, b, (((0,), (0,)), ((), ())), preferred_element_type=F32)


def _rms(t, g):
    return t * lax.rsqrt(jnp.mean(t * t, axis=-1, keepdims=True) + RMS_EPS) * g


def _pick(dim, pref):
    if dim <= pref:
        return dim
    for cand in range(pref - pref % LANE, 0, -LANE):
        if dim % cand == 0:
            return cand
    return dim


def _rows(bm, cols):
    return pl.BlockSpec((bm, cols), lambda i: (i, 0))


def _whole(shape):
    nd = len(shape)
    return pl.BlockSpec(shape, lambda i: (0,) * nd)


def _split3(x):
    hi = x.astype(BF16)
    r1 = x - hi.astype(F32)
    mid = r1.astype(BF16)
    lo = (r1 - mid.astype(F32)).astype(BF16)
    return hi, mid, lo


def _cumdot(x, tri):
    hi = x.astype(BF16)
    lo = (x - hi.astype(F32)).astype(BF16)
    return _dot(hi, tri) + _dot(lo, tri)


FFN_ROWS = 512


def _slab_rows(bm, cols):
    return pl.BlockSpec((None, bm, cols), lambda i, j: (j, i, 0))


def _slab_weight(rows, cols):
    return pl.BlockSpec((None, rows, cols), lambda i, j: (j, 0, 0))


def _token_rows(bm, cols):
    return pl.BlockSpec((bm, cols), lambda i, j: (i, 0))


def _gain(cols):
    return pl.BlockSpec((1, cols), lambda i, j: (0, 0))


def _ffn_fwd_up(h, g_pre, wg, wu):
    S, D = h.shape
    Fs = wg.shape[2]
    bm = _pick(S, FFN_ROWS)

    def body(h_ref, g_ref, wg_ref, wu_ref, n_ref, gate_ref, up_ref, a_ref):
        @pl.when(pl.program_id(1) == 0)
        def _():
            n_ref[...] = _rms(h_ref[...], g_ref[...]).astype(BF16)

        n = n_ref[...]
        gate = _dot(n, wg_ref[...])
        up = _dot(n, wu_ref[...])
        gate_ref[...] = gate.astype(BF16)
        up_ref[...] = up.astype(BF16)
        a_ref[...] = (gate * jax.nn.sigmoid(gate) * up).astype(BF16)

    return pl.pallas_call(
        body, name="ffn_fwd_up", grid=(S // bm, N_CHIPS),
        in_specs=[_token_rows(bm, D), _gain(D), _slab_weight(D, Fs), _slab_weight(D, Fs)],
        out_specs=[_token_rows(bm, D), _slab_rows(bm, Fs), _slab_rows(bm, Fs), _slab_rows(bm, Fs)],
        out_shape=[jax.ShapeDtypeStruct((S, D), BF16)] + [jax.ShapeDtypeStruct((N_CHIPS, S, Fs), BF16)] * 3,
        compiler_params=_params(),
    )(h, g_pre, wg, wu)


def _ffn_fwd_down(a, wd, h, g_post):
    _, S, Fs = a.shape
    D = wd.shape[2]
    bm = _pick(S, FFN_ROWS)

    def body(a_ref, wd_ref, h_ref, g_ref, hout_ref, f_ref):
        j = pl.program_id(1)
        part = _dot(a_ref[...], wd_ref[...])

        @pl.when(j == 0)
        def _():
            f_ref[...] = part

        @pl.when(j > 0)
        def _():
            f_ref[...] += part

        @pl.when(j == N_CHIPS - 1)
        def _():
            hout_ref[...] = h_ref[...] + 0.5 * _rms(f_ref[...], g_ref[...])

    return pl.pallas_call(
        body, name="ffn_fwd_down", grid=(S // bm, N_CHIPS),
        in_specs=[_slab_rows(bm, Fs), _slab_weight(Fs, D), _token_rows(bm, D), _gain(D)],
        out_specs=[_token_rows(bm, D), _token_rows(bm, D)],
        out_shape=[jax.ShapeDtypeStruct((S, D), F32)] * 2,
        compiler_params=_params(),
    )(a, wd, h, g_post)


def _ffn_bwd_down(dh, f, g_post, wd, gate, up):
    S, D = dh.shape
    Fs = wd.shape[1]
    bm = _pick(S, FFN_ROWS)

    def body(dh_ref, f_ref, g_ref, wd_ref, gate_ref, up_ref, df_ref, dgate_ref, dup_ref, dg_ref):
        i, j = pl.program_id(0), pl.program_id(1)

        @pl.when((i == 0) & (j == 0))
        def _():
            dg_ref[...] = jnp.zeros_like(dg_ref)

        @pl.when(j == 0)
        def _():
            _, vjp = jax.vjp(lambda t, g: 0.5 * _rms(t, g), f_ref[...], g_ref[...])
            df, dg = vjp(dh_ref[...])
            dg_ref[...] += dg
            df_ref[...] = df.astype(BF16)

        da = _dot_nt(df_ref[...], wd_ref[...])
        gt = gate_ref[...].astype(F32)
        sig = jax.nn.sigmoid(gt)
        silu = gt * sig
        dup_ref[...] = (da * silu).astype(BF16)
        dgate_ref[...] = (da * up_ref[...].astype(F32) * (sig + silu * (1.0 - sig))).astype(BF16)

    return pl.pallas_call(
        body, name="ffn_bwd_down", grid=(S // bm, N_CHIPS),
        in_specs=[_token_rows(bm, D), _token_rows(bm, D), _gain(D), _slab_weight(Fs, D), _slab_rows(bm, Fs),
                  _slab_rows(bm, Fs)],
        out_specs=[_token_rows(bm, D), _slab_rows(bm, Fs), _slab_rows(bm, Fs), _gain(D)],
        out_shape=[jax.ShapeDtypeStruct((S, D), BF16), jax.ShapeDtypeStruct((N_CHIPS, S, Fs), BF16),
                   jax.ShapeDtypeStruct((N_CHIPS, S, Fs), BF16), jax.ShapeDtypeStruct((1, D), F32)],
        compiler_params=_params(),
    )(dh, f, g_post, wd, gate, up)


def _ffn_bwd_up(dgate, dup, wg, wu, h_in, g_pre, dh):
    _, S, Fs = dgate.shape
    D = wg.shape[1]
    bm = _pick(S, FFN_ROWS)

    def body(dgate_ref, dup_ref, wg_ref, wu_ref, h_ref, g_ref, dh_ref, dhin_ref, dg_ref, dn_ref):
        i, j = pl.program_id(0), pl.program_id(1)
        part = _dot_nt(dgate_ref[...], wg_ref[...]) + _dot_nt(dup_ref[...], wu_ref[...])

        @pl.when(j == 0)
        def _():
            dn_ref[...] = part

        @pl.when(j > 0)
        def _():
            dn_ref[...] += part

        @pl.when((i == 0) & (j == 0))
        def _():
            dg_ref[...] = jnp.zeros_like(dg_ref)

        @pl.when(j == N_CHIPS - 1)
        def _():
            _, vjp = jax.vjp(_rms, h_ref[...], g_ref[...])
            dhx, dg = vjp(dn_ref[...])
            dg_ref[...] += dg
            dhin_ref[...] = dh_ref[...] + dhx

    return pl.pallas_call(
        body, name="ffn_bwd_up", grid=(S // bm, N_CHIPS),
        in_specs=[_slab_rows(bm, Fs), _slab_rows(bm, Fs), _slab_weight(D, Fs), _slab_weight(D, Fs),
                  _token_rows(bm, D), _gain(D), _token_rows(bm, D)],
        out_specs=[_token_rows(bm, D), _gain(D)],
        out_shape=[jax.ShapeDtypeStruct((S, D), F32), jax.ShapeDtypeStruct((1, D), F32)],
        scratch_shapes=[pltpu.VMEM((bm, D), F32)],
        compiler_params=_params(),
    )(dgate, dup, wg, wu, h_in, g_pre, dh)


def _matmul(a, b, *, ta=False, tb=False, out_dtype=BF16, name, batch=None):
    n_batch = a.shape[0] if batch == "a" else b.shape[0] if batch == "b" else 1
    a_shape = a.shape[1:] if batch == "a" else a.shape
    b_shape = b.shape[1:] if batch == "b" else b.shape
    M, K = (a_shape[1], a_shape[0]) if ta else a_shape
    N = b_shape[0] if tb else b_shape[1]
    acc_budget = 12 * 1024 * 1024
    bm, bk = _pick(M, 1536), _pick(K, 512)
    while n_batch * N * bm * 4 > acc_budget and bm % (2 * LANE) == 0:
        bm //= 2
    bn = N if n_batch * N * bm * 4 <= acc_budget else _pick(N, 1536)
    nk = K // bk

    def body(a_ref, b_ref, o_ref, acc_ref):
        kk = pl.program_id(2)

        @pl.when(kk == 0)
        def _():
            acc_ref[...] = jnp.zeros_like(acc_ref)

        dims = (((0 if ta else 1,), (1 if tb else 0,)), ((), ()))
        if batch is None:
            acc_ref[...] += lax.dot_general(a_ref[...], b_ref[...], dims, preferred_element_type=F32)
        else:
            for g in range(n_batch):
                av = a_ref[g] if batch == "a" else a_ref[...]
                bv = b_ref[g] if batch == "b" else b_ref[...]
                acc_ref[g] += lax.dot_general(av, bv, dims, preferred_element_type=F32)

        @pl.when(kk == nk - 1)
        def _():
            o_ref[...] = acc_ref[...].astype(o_ref.dtype)

    def spec(block, index, batched):
        if batched:
            return pl.BlockSpec((n_batch,) + block, lambda i, j, k: (0,) + index(i, j, k))
        return pl.BlockSpec(block, index)

    a_spec = spec((bk, bm), lambda i, j, k: (k, i), batch == "a") if ta else \
        spec((bm, bk), lambda i, j, k: (i, k), batch == "a")
    b_spec = spec((bn, bk), lambda i, j, k: (j, k), batch == "b") if tb else \
        spec((bk, bn), lambda i, j, k: (k, j), batch == "b")
    lead = (n_batch,) if batch else ()
    return pl.pallas_call(
        body, name=name, grid=(M // bm, N // bn, nk),
        in_specs=[a_spec, b_spec],
        out_specs=spec((bm, bn), lambda i, j, k: (i, j), batch is not None),
        out_shape=jax.ShapeDtypeStruct(lead + (M, N), out_dtype),
        scratch_shapes=[pltpu.VMEM(lead + (bm, bn), F32)],
        compiler_params=_params(),
    )(a, b)


def _mix_fwd_in(h, g_pre, win, wgate, b_gate, b_forget):
    S, D = h.shape
    PW = win.shape[1] - LANE
    G = wgate.shape[1]
    bm = _pick(S, 256)

    def body(h_ref, g_ref, win_ref, wgate_ref, bg_ref, bf_ref, u_ref, proj_ref, fl_ref, sg_ref):
        u = _rms(h_ref[...], g_ref[...]).astype(BF16)
        u_ref[...] = u
        proj = _dot(u, win_ref[...])
        proj_ref[...] = proj[:, :PW].astype(BF16)
        fl_ref[...] = proj[:, PW:] + bf_ref[...]
        sg_ref[...] = jax.nn.sigmoid(_dot(u, wgate_ref[...]) + bg_ref[...]).astype(BF16)

    return pl.pallas_call(
        body, name="mix_fwd_in", grid=(S // bm,),
        in_specs=[_rows(bm, D), _whole((1, D)), _whole((D, PW + LANE)), _whole((D, G)), _whole((1, G)),
                  _whole((1, LANE))],
        out_specs=[_rows(bm, D), _rows(bm, PW), _rows(bm, LANE), _rows(bm, G)],
        out_shape=[jax.ShapeDtypeStruct((S, D), BF16), jax.ShapeDtypeStruct((S, PW), BF16),
                   jax.ShapeDtypeStruct((S, LANE), F32), jax.ShapeDtypeStruct((S, G), BF16)],
        compiler_params=_params(),
    )(h, g_pre, win, wgate, b_gate, b_forget)


def _mix_fwd_out(o_sb, o_fx, o_mem, sg, w_sb, w_fx, w_mem, w_out, h, g_post):
    S, D = h.shape
    bm = _pick(S, 256)
    widths = (o_sb.shape[1], o_fx.shape[1], o_mem.shape[1])

    def body(osb_ref, ofx_ref, omem_ref, sg_ref, wsb_ref, wfx_ref, wmem_ref, wout_ref, h_ref, g_ref,
             hout_ref, z_ref, merged_ref):
        s = sg_ref[...].astype(F32)
        merged = (s[:, :D] * _dot(osb_ref[...], wsb_ref[...]) + s[:, D:2 * D] * _dot(ofx_ref[...], wfx_ref[...])
                  + s[:, 2 * D:] * _dot(omem_ref[...], wmem_ref[...]))
        mb = merged.astype(BF16)
        merged_ref[...] = mb
        z = _dot(mb, wout_ref[...])
        z_ref[...] = z
        hout_ref[...] = h_ref[...] + _rms(z, g_ref[...])

    return pl.pallas_call(
        body, name="mix_fwd_out", grid=(S // bm,),
        in_specs=[_rows(bm, widths[0]), _rows(bm, widths[1]), _rows(bm, widths[2]), _rows(bm, 3 * D),
                  _whole((widths[0], D)), _whole((widths[1], D)), _whole((widths[2], D)), _whole((D, D)),
                  _rows(bm, D), _whole((1, D))],
        out_specs=[_rows(bm, D), _rows(bm, D), _rows(bm, D)],
        out_shape=[jax.ShapeDtypeStruct((S, D), F32), jax.ShapeDtypeStruct((S, D), F32),
                   jax.ShapeDtypeStruct((S, D), BF16)],
        compiler_params=_params(),
    )(o_sb, o_fx, o_mem, sg, w_sb, w_fx, w_mem, w_out, h, g_post)


def _mix_bwd_out(dh, z, g_post, w_out, o_sb, o_fx, o_mem, w_sb, w_fx, w_mem, sg):
    S, D = dh.shape
    bm = _pick(S, 256)
    widths = (o_sb.shape[1], o_fx.shape[1], o_mem.shape[1])

    def body(dh_ref, z_ref, g_ref, wout_ref, osb_ref, ofx_ref, omem_ref, wsb_ref, wfx_ref, wmem_ref, sg_ref,
             dz_ref, dbsb_ref, dbfx_ref, dbmem_ref, dosb_ref, dofx_ref, domem_ref, dgp_ref, dbg_ref, dg_ref):
        _, vjp = jax.vjp(_rms, z_ref[...], g_ref[...])
        dz, dg = vjp(dh_ref[...])

        @pl.when(pl.program_id(0) == 0)
        def _():
            dg_ref[...] = jnp.zeros_like(dg_ref)
            dbg_ref[...] = jnp.zeros_like(dbg_ref)

        dg_ref[...] += dg
        dzb = dz.astype(BF16)
        dz_ref[...] = dzb
        dmerged = _dot_nt(dzb, wout_ref[...])
        s = sg_ref[...].astype(F32)
        branches = ((osb_ref, wsb_ref, dbsb_ref, dosb_ref), (ofx_ref, wfx_ref, dbfx_ref, dofx_ref),
                    (omem_ref, wmem_ref, dbmem_ref, domem_ref))
        for k, (o_ref, w_ref, db_ref, do_ref) in enumerate(branches):
            gs = s[:, k * D:(k + 1) * D]
            dbb = (dmerged * gs).astype(BF16)
            db_ref[...] = dbb
            do_ref[...] = _dot_nt(dbb, w_ref[...]).astype(BF16)
            dgp = dmerged * _dot(o_ref[...], w_ref[...]) * gs * (1.0 - gs)
            dgp_ref[:, k * D:(k + 1) * D] = dgp.astype(BF16)
            dbg_ref[:, k * D:(k + 1) * D] += jnp.sum(dgp, axis=0, keepdims=True)

    return pl.pallas_call(
        body, name="mix_bwd_out", grid=(S // bm,),
        in_specs=[_rows(bm, D), _rows(bm, D), _whole((1, D)), _whole((D, D)),
                  _rows(bm, widths[0]), _rows(bm, widths[1]), _rows(bm, widths[2]),
                  _whole((widths[0], D)), _whole((widths[1], D)), _whole((widths[2], D)), _rows(bm, 3 * D)],
        out_specs=[_rows(bm, D)] * 4 + [_rows(bm, widths[0]), _rows(bm, widths[1]), _rows(bm, widths[2]),
                                        _rows(bm, 3 * D), _whole((1, 3 * D)), _whole((1, D))],
        out_shape=[jax.ShapeDtypeStruct((S, D), BF16)] * 4
        + [jax.ShapeDtypeStruct((S, w), BF16) for w in widths]
        + [jax.ShapeDtypeStruct((S, 3 * D), BF16), jax.ShapeDtypeStruct((1, 3 * D), F32),
           jax.ShapeDtypeStruct((1, D), F32)],
        compiler_params=_params(),
    )(dh, z, g_post, w_out, o_sb, o_fx, o_mem, w_sb, w_fx, w_mem, sg)


def _mix_bwd_in(dproj, dgp, win, wgate, h_in, g_pre, dh):
    S, PWL = dproj.shape
    G = dgp.shape[1]
    D = h_in.shape[1]
    bm = _pick(S, 256)

    def body(dproj_ref, dgp_ref, win_ref, wgate_ref, h_ref, g_ref, dh_ref, dhin_ref, dg_ref):
        du = _dot_nt(dproj_ref[...], win_ref[...]) + _dot_nt(dgp_ref[...], wgate_ref[...])
        _, vjp = jax.vjp(_rms, h_ref[...], g_ref[...])
        dhx, dg = vjp(du)

        @pl.when(pl.program_id(0) == 0)
        def _():
            dg_ref[...] = jnp.zeros_like(dg_ref)

        dg_ref[...] += dg
        dhin_ref[...] = dh_ref[...] + dhx

    return pl.pallas_call(
        body, name="mix_bwd_in", grid=(S // bm,),
        in_specs=[_rows(bm, PWL), _rows(bm, G), _whole((D, PWL)), _whole((D, G)), _rows(bm, D), _whole((1, D)),
                  _rows(bm, D)],
        out_specs=[_rows(bm, D), _whole((1, D))],
        out_shape=[jax.ShapeDtypeStruct((S, D), F32), jax.ShapeDtypeStruct((1, D), F32)],
        compiler_params=_params(),
    )(dproj, dgp, win, wgate, h_in, g_pre, dh)


def _log_sigmoid(x):
    return jnp.minimum(x, 0.0) - jnp.log(1.0 + jnp.exp(-jnp.abs(x)))


def _fox_cumsum(fl):
    S = fl.shape[0]
    rb = _pick(S, LANE)

    def body(fl_ref, c_ref, carry_ref):
        @pl.when(pl.program_id(0) == 0)
        def _():
            carry_ref[...] = jnp.zeros_like(carry_ref)

        r = lax.broadcasted_iota(jnp.int32, (rb, rb), 0)
        cidx = lax.broadcasted_iota(jnp.int32, (rb, rb), 1)
        tri = (cidx <= r).astype(BF16)
        hi, mid, lo = _split3(_log_sigmoid(fl_ref[...]))
        c = _dot(tri, hi) + _dot(tri, mid) + _dot(tri, lo) + carry_ref[...]
        c_ref[...] = c
        carry_ref[...] = c[rb - 1:rb, :]

    return pl.pallas_call(
        body, name="fox_cumsum", grid=(S // rb,),
        in_specs=[_rows(rb, LANE)], out_specs=_rows(rb, LANE),
        out_shape=jax.ShapeDtypeStruct((S, LANE), F32),
        scratch_shapes=[pltpu.VMEM((1, LANE), F32)],
        compiler_params=_params(),
    )(fl)


def _fox_dlogit(dc, fl):
    S = fl.shape[0]
    rb = _pick(S, LANE)
    nb = S // rb

    def body(dc_ref, fl_ref, dfl_ref, dbf_ref, carry_ref):
        @pl.when(pl.program_id(0) == 0)
        def _():
            carry_ref[...] = jnp.zeros_like(carry_ref)
            dbf_ref[...] = jnp.zeros_like(dbf_ref)

        r = lax.broadcasted_iota(jnp.int32, (rb, rb), 0)
        cidx = lax.broadcasted_iota(jnp.int32, (rb, rb), 1)
        tri = (cidx >= r).astype(BF16)
        hi, mid, lo = _split3(dc_ref[...])
        rc = _dot(tri, hi) + _dot(tri, mid) + _dot(tri, lo) + carry_ref[...]
        carry_ref[...] = rc[0:1, :]
        dfl = rc * jax.nn.sigmoid(-fl_ref[...])
        dfl_ref[...] = dfl.astype(BF16)
        dbf_ref[...] += jnp.sum(dfl, axis=0, keepdims=True)

    rev = pl.BlockSpec((rb, LANE), lambda i: (nb - 1 - i, 0))
    return pl.pallas_call(
        body, name="fox_dlogit", grid=(nb,),
        in_specs=[rev, rev], out_specs=[rev, _whole((1, LANE))],
        out_shape=[jax.ShapeDtypeStruct((S, LANE), BF16), jax.ShapeDtypeStruct((1, LANE), F32)],
        scratch_shapes=[pltpu.VMEM((1, LANE), F32)],
        compiler_params=_params(),
    )(dc, fl)


def _attn_blocks(kind, S, Sk, backward=False):
    tq = _pick(S, 1024 if backward else 2048)
    tc = LANE if kind == "sb" else _pick(Sk, 256)
    return tq, tc


def _is_power_of_two(x):
    return math.frexp(x)[0] == 0.5


def _sb_logs(z):
    ln = -jnp.maximum(z, 0.0) - jnp.log(1.0 + jnp.exp(-jnp.abs(z)))
    return ln + z, ln


def _head_lanes(pack, dh):
    lane = lax.broadcasted_iota(jnp.int32, (1, LANE), 1)
    return [(lane >= hh * dh) & (lane < (hh + 1) * dh) for hh in range(pack)]


def _by_head(sel, parts):
    out = parts[0]
    for hh in range(1, len(parts)):
        out = jnp.where(sel[hh], parts[hh], out)
    return out


def _only_head(sel, hh, x):
    return x if len(sel) == 1 else jnp.where(sel[hh], x, jnp.zeros_like(x))


def _tail(x, r0):
    return x if not r0 else x[r0:]


def _put_tail(x, tail, r0):
    return tail if not r0 else jnp.concatenate([x[:r0], tail], axis=0)


def _add_tail(x, tail, r0):
    return x + tail if not r0 else jnp.concatenate([x[:r0], x[r0:] + tail], axis=0)


def _q_cols(tq, first):
    return pl.BlockSpec((tq, LANE), lambda g, i: (i, first // LANE + g))


def _k_cols(rows, first):
    return pl.BlockSpec((rows, LANE), lambda g, i: (0, first // LANE + g))


def _attn_fwd(kind, q, k, v, n_heads, dh, ccol=None, crow=None):
    (qa, q0), (ka, k0), (va, v0) = q, k, v
    S, Sk = qa.shape[0], ka.shape[0]
    pack = LANE // dh
    tq, tc = _attn_blocks(kind, S, Sk)
    scale = dh ** -0.5
    fold = _is_power_of_two(scale)
    causal = kind != "mem"
    n_diag = tq // tc if causal else 0
    unroll = 2 if causal else 1
    assert n_diag % unroll == 0 and (Sk // tc) % unroll == 0

    def body(*refs):
        if kind == "fox":
            q_ref, k_ref, v_ref, cc_ref, cr_ref, o_ref, lse_ref = refs
        else:
            q_ref, k_ref, v_ref, o_ref, lse_ref = refs
        i = pl.program_id(1)
        n_full = (i * tq) // tc if causal else Sk // tc
        qpos = i * tq + lax.broadcasted_iota(jnp.int32, (tq, tc), 0)
        kio = lax.broadcasted_iota(jnp.int32, (tq, tc), 1)
        heads = range(pack)
        sel = _head_lanes(pack, dh)
        q2 = q_ref[...] * scale if fold else q_ref[...]
        qs = [_only_head(sel, hh, q2) for hh in heads]

        def kv(jc):
            off = pl.multiple_of(jc * tc, tc)
            return off, k_ref[pl.ds(off, tc), :], v_ref[pl.ds(off, tc), :]

        if kind == "sb":
            tri = (lax.broadcasted_iota(jnp.int32, (tc, tc), 0) > lax.broadcasted_iota(jnp.int32, (tc, tc), 1)
                   ).astype(BF16)

            def chunk(jc, r0, runs, acc):
                off, k2, v2 = kv(jc)
                new_runs, pv = [], []
                for hh in heads:
                    lb, ln = _sb_logs(_dot_nt(_tail(qs[hh], r0), k2))
                    if r0 is not None:
                        mask = (off + _tail(kio, r0)) < _tail(qpos, r0)
                        ln = jnp.where(mask, ln, 0.0)
                    w = jnp.exp(lb + _cumdot(ln, tri) + _tail(runs[hh], r0))
                    if r0 is not None:
                        w = jnp.where(mask, w, 0.0)
                    pv.append(_dot(w.astype(BF16), v2))
                    new_runs.append(_add_tail(runs[hh], jnp.sum(ln, axis=1, keepdims=True), r0))
                return tuple(new_runs), _add_tail(acc, _by_head(sel, pv), r0)

            state = (tuple(jnp.zeros((tq, 1), F32) for _ in heads), jnp.zeros((tq, LANE), F32))
            for d in range(n_diag - 1, -1, -1):
                state = chunk(n_full + d, d * tc, *state)

            def trip(t, st):
                for u in range(unroll):
                    st = chunk(n_full - 1 - unroll * t - u, None, *st)
                return st

            runs, acc = lax.fori_loop(0, n_full // unroll, trip, state)
            o_ref[...] = acc.astype(o_ref.dtype)
            for hh in heads:
                lse_ref[hh] = runs[hh]
        else:
            def chunk(jc, r0, ms, ls, acc):
                off, k2, v2 = kv(jc)
                new_ms, new_ls, alphas, pv = [], [], [], []
                for hh in heads:
                    z = _dot_nt(_tail(qs[hh], r0), k2)
                    if not fold:
                        z = z * scale
                    if kind == "fox":
                        z = z + _tail(cc_ref[hh], r0) - cr_ref[hh, pl.ds(jc, 1), :]
                    if r0 is not None:
                        z = jnp.where((off + _tail(kio, r0)) <= _tail(qpos, r0), z, NEG)
                    m_old, l_old = _tail(ms[hh], r0), _tail(ls[hh], r0)
                    m_new = jnp.maximum(m_old, jnp.max(z, axis=1, keepdims=True))
                    alpha = jnp.exp(m_old - m_new)
                    p = jnp.exp(z - m_new)
                    new_ms.append(_put_tail(ms[hh], m_new, r0))
                    new_ls.append(_put_tail(ls[hh], alpha * l_old + jnp.sum(p, axis=1, keepdims=True), r0))
                    alphas.append(alpha)
                    pv.append(_dot(p.astype(BF16), v2))
                acc_new = _by_head(sel, alphas) * _tail(acc, r0) + _by_head(sel, pv)
                return tuple(new_ms), tuple(new_ls), _put_tail(acc, acc_new, r0)

            state = (tuple(jnp.full((tq, 1), NEG, F32) for _ in heads), tuple(jnp.zeros((tq, 1), F32) for _ in heads),
                     jnp.zeros((tq, LANE), F32))

            def trip(t, st):
                for u in range(unroll):
                    st = chunk(unroll * t + u, None, *st)
                return st

            state = lax.fori_loop(0, n_full // unroll, trip, state)
            for d in range(n_diag):
                state = chunk(n_full + d, d * tc, *state)
            ms, ls, acc = state
            o_ref[...] = (acc / _by_head(sel, ls)).astype(o_ref.dtype)
            for hh in heads:
                lse_ref[hh] = ms[hh] + jnp.log(ls[hh])

    colspec = pl.BlockSpec((pack, tq, 1), lambda g, i: (g, i, 0))
    in_specs, args = [_q_cols(tq, q0), _k_cols(Sk, k0), _k_cols(Sk, v0)], [qa, ka, va]
    if kind == "fox":
        in_specs += [colspec, pl.BlockSpec((pack, Sk // tc, tc), lambda g, i: (g, 0, 0))]
        args += [ccol, crow]
    return pl.pallas_call(
        body, name="attn_fwd_" + kind, grid=(n_heads // pack, S // tq),
        in_specs=in_specs, out_specs=[_q_cols(tq, 0), colspec],
        out_shape=[jax.ShapeDtypeStruct((S, n_heads * dh), BF16), jax.ShapeDtypeStruct((n_heads, S, 1), F32)],
        compiler_params=_params(),
    )(*args)


def _attn_bwd(kind, q, k, v, o, do, n_heads, dh, ccol=None, crow=None, lse=None):
    (qa, q0), (ka, k0), (va, v0) = q, k, v
    S, Sk = qa.shape[0], ka.shape[0]
    pack = LANE // dh
    tq, tc = _attn_blocks(kind, S, Sk, backward=True)
    scale = dh ** -0.5
    fold = _is_power_of_two(scale)
    causal = kind != "mem"
    n_diag = tq // tc if causal else 0
    unroll = 2 if kind == "sb" else 1
    assert n_diag % unroll == 0 and (Sk // tc) % unroll == 0
    nq = S // tq

    def body(*refs):
        if kind == "fox":
            (q_ref, k_ref, v_ref, o_ref, do_ref, cc_ref, cr_ref, lse_ref,
             dq_ref, dk_ref, dv_ref, dc_ref, dcc_ref, dk_acc, dv_acc, dc_acc) = refs
        else:
            q_ref, k_ref, v_ref, o_ref, do_ref, lse_ref, dq_ref, dk_ref, dv_ref, dk_acc, dv_acc = refs
        i = pl.program_id(1)

        @pl.when(i == 0)
        def _():
            dk_acc[...] = jnp.zeros_like(dk_acc)
            dv_acc[...] = jnp.zeros_like(dv_acc)
            if kind == "fox":
                dc_acc[...] = jnp.zeros_like(dc_acc)

        n_full = (i * tq) // tc if causal else Sk // tc
        qpos = i * tq + lax.broadcasted_iota(jnp.int32, (tq, tc), 0)
        kio = lax.broadcasted_iota(jnp.int32, (tq, tc), 1)
        heads = range(pack)
        sel = _head_lanes(pack, dh)
        q2 = q_ref[...] * scale if fold else q_ref[...]
        do2 = do_ref[...]
        qs = [_only_head(sel, hh, q2) for hh in heads]
        dos = [_only_head(sel, hh, do2) for hh in heads]

        def kv(jc):
            off = pl.multiple_of(jc * tc, tc)
            return off, k_ref[pl.ds(off, tc), :], v_ref[pl.ds(off, tc), :]

        def accumulate(off, k2, dzb, wb, dq, r0):
            q2t, do2t = _tail(q2, r0), _tail(do2, r0)
            dk_acc[pl.ds(off, tc), :] += _by_head(sel, [_dot_tn(dzb[hh], q2t) for hh in heads])
            dv_acc[pl.ds(off, tc), :] += _by_head(sel, [_dot_tn(wb[hh], do2t) for hh in heads])
            return _add_tail(dq, _by_head(sel, [_dot(dzb[hh], k2) for hh in heads]), r0)

        if kind == "sb":
            r = lax.broadcasted_iota(jnp.int32, (tc, tc), 0)
            cidx = lax.broadcasted_iota(jnp.int32, (tc, tc), 1)
            tri_inc = (r <= cidx).astype(BF16)
            tri_exc = (r < cidx).astype(BF16)

            def chunk(jc, r0, pres, pres_e, dq):
                off, k2, v2 = kv(jc)
                new_pres, new_pres_e, dzb, wb = [], [], [], []
                for hh in heads:
                    lb, ln = _sb_logs(_dot_nt(_tail(qs[hh], r0), k2))
                    if r0 is not None:
                        mask = (off + _tail(kio, r0)) < _tail(qpos, r0)
                        ln = jnp.where(mask, ln, 0.0)
                    w = jnp.exp(lb + (_tail(lse_ref[hh], r0) - _tail(pres[hh], r0) - _cumdot(ln, tri_inc)))
                    if r0 is not None:
                        w = jnp.where(mask, w, 0.0)
                    e = w * _dot_nt(_tail(dos[hh], r0), v2)
                    beta = jnp.exp(lb)
                    dz = e * (1.0 - beta) - beta * (_tail(pres_e[hh], r0) + _cumdot(e, tri_exc))
                    if r0 is not None:
                        dz = jnp.where(mask, dz, 0.0)
                    dzb.append(dz.astype(BF16))
                    wb.append(w.astype(BF16))
                    new_pres.append(_add_tail(pres[hh], jnp.sum(ln, axis=1, keepdims=True), r0))
                    new_pres_e.append(_add_tail(pres_e[hh], jnp.sum(e, axis=1, keepdims=True), r0))
                return tuple(new_pres), tuple(new_pres_e), accumulate(off, k2, dzb, wb, dq, r0)

            state = (tuple(jnp.zeros((tq, 1), F32) for _ in heads), tuple(jnp.zeros((tq, 1), F32) for _ in heads),
                     jnp.zeros((tq, LANE), F32))
        else:
            prod = o_ref[...].astype(F32) * do2.astype(F32)
            dsum = [jnp.sum(_only_head(sel, hh, prod), axis=1, keepdims=True) for hh in heads]

            def chunk(jc, r0, rowsums, dq):
                off, k2, v2 = kv(jc)
                new_rowsums, dsb, pb = [], [], []
                for hh in heads:
                    z = _dot_nt(_tail(qs[hh], r0), k2)
                    if not fold:
                        z = z * scale
                    if kind == "fox":
                        z = z + _tail(cc_ref[hh], r0) - cr_ref[hh, pl.ds(jc, 1), :]
                    if r0 is not None:
                        z = jnp.where((off + _tail(kio, r0)) <= _tail(qpos, r0), z, NEG)
                    p = jnp.exp(z - _tail(lse_ref[hh], r0))
                    ds = p * (_dot_nt(_tail(dos[hh], r0), v2) - _tail(dsum[hh], r0))
                    dsb.append(ds.astype(BF16))
                    pb.append(p.astype(BF16))
                    if kind == "fox":
                        dc_acc[hh, pl.ds(jc, 1), :] -= jnp.sum(ds, axis=0, keepdims=True)
                        new_rowsums.append(_add_tail(rowsums[hh], jnp.sum(ds, axis=1, keepdims=True), r0))
                    else:
                        new_rowsums.append(rowsums[hh])
                return tuple(new_rowsums), accumulate(off, k2, dsb, pb, dq, r0)

            state = (tuple(jnp.zeros((tq, 1), F32) for _ in heads), jnp.zeros((tq, LANE), F32))

        def trip(t, st):
            for u in range(unroll):
                st = chunk(unroll * t + u, None, *st)
            return st

        state = lax.fori_loop(0, n_full // unroll, trip, state)
        for d in range(n_diag):
            state = chunk(n_full + d, d * tc, *state)
        dq_ref[...] = (state[-1] * scale).astype(dq_ref.dtype)
        if kind == "fox":
            for hh in heads:
                dcc_ref[hh] = state[0][hh]

        @pl.when(i == nq - 1)
        def _():
            dk = dk_acc[...] if fold else dk_acc[...] * scale
            dk_ref[...] = dk.astype(dk_ref.dtype)
            dv_ref[...] = dv_acc[...].astype(dv_ref.dtype)
            if kind == "fox":
                dc_ref[...] = dc_acc[...]

    colspec = pl.BlockSpec((pack, tq, 1), lambda g, i: (g, i, 0))
    rowspec = pl.BlockSpec((pack, Sk // tc, tc), lambda g, i: (g, 0, 0))
    in_specs = [_q_cols(tq, q0), _k_cols(Sk, k0), _k_cols(Sk, v0), _q_cols(tq, 0), _q_cols(tq, 0)]
    args = [qa, ka, va, o, do]
    if kind == "fox":
        in_specs += [colspec, rowspec]
        args += [ccol, crow]
    in_specs += [colspec]
    args += [lse]
    width = n_heads * dh
    out_specs = [_q_cols(tq, 0), _k_cols(Sk, 0), _k_cols(Sk, 0)]
    out_shape = [jax.ShapeDtypeStruct((S, width), BF16), jax.ShapeDtypeStruct((Sk, width), BF16),
                 jax.ShapeDtypeStruct((Sk, width), BF16)]
    scratch = [pltpu.VMEM((Sk, LANE), F32), pltpu.VMEM((Sk, LANE), F32)]
    if kind == "fox":
        out_specs += [rowspec, colspec]
        out_shape += [jax.ShapeDtypeStruct((n_heads, Sk // tc, tc), F32), jax.ShapeDtypeStruct((n_heads, S, 1), F32)]
        scratch.append(pltpu.VMEM((pack, Sk // tc, tc), F32))
    return pl.pallas_call(
        body, name="attn_bwd_" + kind, grid=(n_heads // pack, nq),
        in_specs=in_specs, out_specs=out_specs, out_shape=out_shape, scratch_shapes=scratch,
        compiler_params=_params(),
    )(*args)


def _mem_norm(mem, g):
    M, D = mem.shape

    def body(mem_ref, g_ref, out_ref):
        out_ref[...] = _rms(mem_ref[...], g_ref[...]).astype(BF16)

    return pl.pallas_call(
        body, name="mem_norm", grid=(1,),
        in_specs=[_whole((M, D)), _whole((1, D))], out_specs=_whole((M, D)),
        out_shape=jax.ShapeDtypeStruct((M, D), BF16), compiler_params=_params(),
    )(mem, g)


def _mem_norm_bwd(mem, g, dmem_n):
    M, D = mem.shape
    L = dmem_n.shape[0]

    def body(mem_ref, g_ref, d_ref, dg_ref):
        d = d_ref[0]
        for l in range(1, L):
            d = d + d_ref[l]
        _, vjp = jax.vjp(_rms, mem_ref[...], g_ref[...])
        dg_ref[...] = vjp(d)[1]

    return pl.pallas_call(
        body, name="mem_norm_bwd", grid=(1,),
        in_specs=[_whole((M, D)), _whole((1, D)), _whole((L, M, D))], out_specs=_whole((1, D)),
        out_shape=jax.ShapeDtypeStruct((1, D), F32), compiler_params=_params(),
    )(mem, g, dmem_n)


def _loss_head(h, target):
    S, D = h.shape
    bm = _pick(S, 512)

    def body(h_ref, t_ref, dh_ref, loss_ref):
        err = h_ref[...] - t_ref[...]
        dh_ref[...] = err * (1.0 / D)

        @pl.when(pl.program_id(0) == 0)
        def _():
            loss_ref[...] = jnp.zeros_like(loss_ref)

        loss_ref[...] += 0.5 * jnp.sum(jnp.mean(err * err, axis=-1, keepdims=True), axis=0, keepdims=True)

    return pl.pallas_call(
        body, name="loss_head", grid=(S // bm,),
        in_specs=[_rows(bm, D), _rows(bm, D)], out_specs=[_rows(bm, D), _whole((8, LANE))],
        out_shape=[jax.ShapeDtypeStruct((S, D), F32), jax.ShapeDtypeStruct((8, LANE), F32)],
        compiler_params=_params(),
    )(h, target)


def _adamw(w, g, m, v, name):
    R, C = w.shape
    rb = R if R * C * 4 <= (1 << 20) else _pick(R, 256)
    if R % rb:
        rb = R
    c1 = 1.0 - ADAM_B1 ** ADAM_STEP
    c2 = 1.0 - ADAM_B2 ** ADAM_STEP

    def body(w_ref, g_ref, m_ref, v_ref, d_ref, mo_ref, vo_ref):
        gv = g_ref[...]
        mn = ADAM_B1 * m_ref[...] + (1.0 - ADAM_B1) * gv
        vn = ADAM_B2 * v_ref[...] + (1.0 - ADAM_B2) * (gv * gv)
        mo_ref[...] = mn
        vo_ref[...] = vn
        d_ref[...] = -ADAM_LR * ((mn / c1) / (jnp.sqrt(vn / c2) + ADAM_EPS) + ADAM_WD * w_ref[...])

    return pl.pallas_call(
        body, name=name, grid=(R // rb,),
        in_specs=[_rows(rb, C)] * 4, out_specs=[_rows(rb, C)] * 3,
        out_shape=[jax.ShapeDtypeStruct((R, C), F32)] * 3, compiler_params=_params(),
    )(w, g, m, v)


def _adamw_reduced(w, m, v, mine, theirs, c_idx, first_row, name):
    L, a, b = w.shape
    Lh = L // 2
    rb = _shard_row_block(a)
    nb = a // rb
    assert first_row % rb == 0
    c1 = 1.0 - ADAM_B1 ** ADAM_STEP
    c2 = 1.0 - ADAM_B2 ** ADAM_STEP

    def own(i, c_ref):
        return (i, 0)

    def reduced(i, c_ref):
        return (first_row // rb + ((i // nb) % Lh) * nb + i % nb, 0)

    def body(c_ref, w_ref, m_ref, v_ref, mine_ref, theirs_ref, g_ref, d_ref, mo_ref, vo_ref):
        half = (pl.program_id(0) // nb) // Lh
        gv = jnp.where(c_ref[0] == half, mine_ref[...], theirs_ref[...])
        g_ref[...] = gv
        mn = ADAM_B1 * m_ref[...] + (1.0 - ADAM_B1) * gv
        vn = ADAM_B2 * v_ref[...] + (1.0 - ADAM_B2) * (gv * gv)
        mo_ref[...] = mn
        vo_ref[...] = vn
        d_ref[...] = -ADAM_LR * ((mn / c1) / (jnp.sqrt(vn / c2) + ADAM_EPS) + ADAM_WD * w_ref[...])

    outs = pl.pallas_call(
        body, name=name,
        grid_spec=pltpu.PrefetchScalarGridSpec(
            num_scalar_prefetch=1, grid=(L * nb,),
            in_specs=[pl.BlockSpec((rb, b), own)] * 3 + [pl.BlockSpec((rb, b), reduced)] * 2,
            out_specs=[pl.BlockSpec((rb, b), own)] * 4),
        out_shape=[jax.ShapeDtypeStruct((L * a, b), F32)] * 4, compiler_params=_params(),
    )(c_idx, w.reshape(L * a, b), m.reshape(L * a, b), v.reshape(L * a, b), mine, theirs)
    return [t.reshape(L, a, b) for t in outs]


ANY = pl.BlockSpec(memory_space=pl.ANY)
MESH = pl.DeviceIdType.MESH


def _place():
    x, y, c = lax.axis_index("x"), lax.axis_index("y"), lax.axis_index("c")
    others = [(1 - x, y), (x, 1 - y), (1 - x, 1 - y)]
    return x, y, c, others


def _place_own(loc, chip_idx):
    _, R, C = loc.shape
    rb = _pick(R, 2 * FLAT_ROW_BLOCK)

    def body(chip_ref, loc_ref, out_ref):
        out_ref[...] = loc_ref[...]

    return pl.pallas_call(
        body, name="place_own",
        grid_spec=pltpu.PrefetchScalarGridSpec(
            num_scalar_prefetch=1, grid=(2, R // rb),
            in_specs=[pl.BlockSpec((None, rb, C), lambda hf, i, chip_ref: (hf, i, 0))],
            out_specs=pl.BlockSpec((None, None, rb, C), lambda hf, i, chip_ref: (chip_ref[0], hf, i, 0))),
        out_shape=jax.ShapeDtypeStruct((N_CHIPS, 2, R, C), loc.dtype), compiler_params=_params(),
    )(chip_idx, loc)


def _gather_weights(locs, owns):
    n = len(locs)

    def body(*refs):
        loc_refs, out_refs, (send_sems, recv_sems) = refs[:n], refs[2 * n:3 * n], refs[3 * n:]
        x, y, c, others = _place()
        me = 2 * x + y
        sibling = (x, y, 1 - c)

        def copy(a, k, src, dst, to):
            return pltpu.make_async_remote_copy(src_ref=src, dst_ref=dst, send_sem=send_sems.at[a, k],
                                                recv_sem=recv_sems.at[a, k], device_id=to, device_id_type=MESH)

        first = [copy(a, j, loc_refs[a].at[c], out_refs[a].at[me, c], (ox, oy, c))
                 for j, (ox, oy) in enumerate(others) for a in range(n)]
        for cp in first:
            cp.start()
        passed = []
        for j, (ox, oy) in enumerate(others):
            for a in range(n):
                landed = out_refs[a].at[2 * ox + oy, c]
                copy(a, j, loc_refs[a].at[c], landed, sibling).wait_recv()
                cp = copy(a, 3 + j, landed, landed, sibling)
                cp.start()
                passed.append(cp)
        for j, (ox, oy) in enumerate(others):
            for a in range(n):
                copy(a, 3 + j, loc_refs[a].at[c], out_refs[a].at[2 * ox + oy, 1 - c], sibling).wait_recv()
        for cp in first + passed:
            cp.wait_send()

    return pl.pallas_call(
        body, name="gather_weights", in_specs=[ANY] * (2 * n), out_specs=[ANY] * n,
        out_shape=[jax.ShapeDtypeStruct(own.shape, own.dtype) for own in owns],
        input_output_aliases={n + a: a for a in range(n)},
        scratch_shapes=[pltpu.SemaphoreType.DMA((n, 6)), pltpu.SemaphoreType.DMA((n, 6))],
    )(*locs, *owns)


def _pair_exchange(gs):
    n = len(gs)

    def body(*refs):
        g_refs, out_refs, (send_sems, recv_sems) = refs[:n], refs[n:2 * n], refs[2 * n:]
        x, y, c, _ = _place()
        copies = [pltpu.make_async_remote_copy(src_ref=g_refs[a].at[1 - c], dst_ref=out_refs[a],
                                               send_sem=send_sems.at[a], recv_sem=recv_sems.at[a],
                                               device_id=(x, y, 1 - c), device_id_type=MESH) for a in range(n)]
        for cp in copies:
            cp.start()
        for cp in copies:
            cp.wait()

    return pl.pallas_call(
        body, name="pair_exchange", in_specs=[ANY] * n, out_specs=[ANY] * n,
        out_shape=[jax.ShapeDtypeStruct(g.shape[1:], g.dtype) for g in gs],
        scratch_shapes=[pltpu.SemaphoreType.DMA((n,)), pltpu.SemaphoreType.DMA((n,))],
    )(*gs)


def _pair_sum(g, sib, c_idx):
    _, _, R, C = g.shape
    rb = _pick(R, 512)

    def body(c_ref, g_ref, s_ref, o_ref):
        o_ref[...] = (g_ref[...].astype(F32) + s_ref[...].astype(F32)).astype(o_ref.dtype)

    return pl.pallas_call(
        body, name="pair_sum",
        grid_spec=pltpu.PrefetchScalarGridSpec(
            num_scalar_prefetch=1, grid=(N_CHIPS, R // rb),
            in_specs=[pl.BlockSpec((None, None, rb, C), lambda j, i, c_ref: (c_ref[0], j, i, 0)),
                      pl.BlockSpec((None, rb, C), lambda j, i, c_ref: (j, i, 0))],
            out_specs=pl.BlockSpec((None, rb, C), lambda j, i, c_ref: (j, i, 0))),
        out_shape=jax.ShapeDtypeStruct((N_CHIPS, R, C), g.dtype), compiler_params=_params(),
    )(c_idx, g, sib)


def _chip_exchange(ps):
    n = len(ps)

    def body(*refs):
        p_refs, out_refs, (send_sems, recv_sems) = refs[:n], refs[n:2 * n], refs[2 * n:]
        x, y, c, others = _place()
        copies = []
        for j, (ox, oy) in enumerate(others):
            for a in range(n):
                cp = pltpu.make_async_remote_copy(src_ref=p_refs[a].at[2 * ox + oy], dst_ref=out_refs[a].at[j],
                                                  send_sem=send_sems.at[a, j], recv_sem=recv_sems.at[a, j],
                                                  device_id=(ox, oy, c), device_id_type=MESH)
                cp.start()
                copies.append(cp)
        for cp in copies:
            cp.wait()

    return pl.pallas_call(
        body, name="chip_exchange", in_specs=[ANY] * n, out_specs=[ANY] * n,
        out_shape=[jax.ShapeDtypeStruct((N_CHIPS - 1,) + p.shape[1:], p.dtype) for p in ps],
        scratch_shapes=[pltpu.SemaphoreType.DMA((n, 3)), pltpu.SemaphoreType.DMA((n, 3))],
    )(*ps)


def _chip_sum(p, r, chip_idx):
    _, R, C = r.shape
    rb = _pick(R, 512)

    def body(chip_ref, p_ref, r_ref, o_ref):
        acc = p_ref[...].astype(F32)
        for j in range(N_CHIPS - 1):
            acc = acc + r_ref[j].astype(F32)
        o_ref[...] = acc

    return pl.pallas_call(
        body, name="chip_sum",
        grid_spec=pltpu.PrefetchScalarGridSpec(
            num_scalar_prefetch=1, grid=(R // rb,),
            in_specs=[pl.BlockSpec((None, rb, C), lambda i, chip_ref: (chip_ref[0], i, 0)),
                      pl.BlockSpec((N_CHIPS - 1, rb, C), lambda i, chip_ref: (0, i, 0))],
            out_specs=pl.BlockSpec((rb, C), lambda i, chip_ref: (i, 0))),
        out_shape=jax.ShapeDtypeStruct((R, C), F32), compiler_params=_params(),
    )(chip_idx, p, r)


def _pair_swap(rhs):
    n = len(rhs)

    def body(*refs):
        rh_refs, out_refs, (send_sems, recv_sems) = refs[:n], refs[n:2 * n], refs[2 * n:]
        x, y, c, _ = _place()
        copies = [pltpu.make_async_remote_copy(src_ref=rh_refs[a], dst_ref=out_refs[a], send_sem=send_sems.at[a],
                                               recv_sem=recv_sems.at[a], device_id=(x, y, 1 - c),
                                               device_id_type=MESH) for a in range(n)]
        for cp in copies:
            cp.start()
        for cp in copies:
            cp.wait()

    return pl.pallas_call(
        body, name="pair_swap", in_specs=[ANY] * n, out_specs=[ANY] * n,
        out_shape=[jax.ShapeDtypeStruct(rh.shape, rh.dtype) for rh in rhs],
        scratch_shapes=[pltpu.SemaphoreType.DMA((n,)), pltpu.SemaphoreType.DMA((n,))],
    )(*rhs)


def _all_reduce_small(s):
    R, C = s.shape

    def body(s_ref, o_ref, buf, send_sems, recv_sems):
        x, y, c, _ = _place()
        me = 4 * x + 2 * y + c
        sends = []
        for k in range(1, N_DEV):
            fx, fy, fc = (k >> 2) & 1, (k >> 1) & 1, k & 1
            to = (x ^ fx, y ^ fy, c ^ fc)
            cp = pltpu.make_async_remote_copy(src_ref=s_ref, dst_ref=buf.at[me], send_sem=send_sems.at[k - 1],
                                              recv_sem=recv_sems.at[k - 1], device_id=to, device_id_type=MESH)
            cp.start()
            sends.append(cp)
        buf[me] = s_ref[...]
        for k in range(1, N_DEV):
            fx, fy, fc = (k >> 2) & 1, (k >> 1) & 1, k & 1
            frm = 4 * (x ^ fx) + 2 * (y ^ fy) + (c ^ fc)
            pltpu.make_async_remote_copy(src_ref=s_ref, dst_ref=buf.at[frm], send_sem=send_sems.at[k - 1],
                                         recv_sem=recv_sems.at[k - 1], device_id=(x, y, c),
                                         device_id_type=MESH).wait_recv()
        acc = buf[0]
        for d in range(1, N_DEV):
            acc = acc + buf[d]
        o_ref[...] = acc
        for cp in sends:
            cp.wait_send()

    vm = pl.BlockSpec(memory_space=pltpu.VMEM)
    return pl.pallas_call(
        body, name="all_reduce_small", in_specs=[vm], out_specs=vm,
        out_shape=jax.ShapeDtypeStruct((R, C), F32),
        scratch_shapes=[pltpu.VMEM((N_DEV, R, C), F32), pltpu.SemaphoreType.DMA((N_DEV - 1,)),
                        pltpu.SemaphoreType.DMA((N_DEV - 1,))],
    )(s)


def _padded(n):
    return -(-n // FLAT_UNIT) * FLAT_UNIT


def _pack_flat(pieces, dtype, row_block=FLAT_ROW_BLOCK):
    flat = []
    for p in pieces:
        p = p.reshape(-1).astype(dtype)
        flat.append(jnp.pad(p, (0, _padded(p.size) - p.size)))
    total = sum(p.size for p in flat)
    flat.append(jnp.zeros((-total) % (row_block * FLAT_COLS), dtype))
    return jnp.concatenate(flat).reshape(-1, FLAT_COLS)


def _unpack_flat(flat, shapes):
    lead = flat.shape[:-2]
    flat = flat.reshape(lead + (-1,))
    out, off = [], 0
    for shp in shapes:
        n = math.prod(shp)
        out.append(flat[..., off:off + n].reshape(lead + tuple(shp)))
        off += _padded(n)
    return out


def _shard_row_block(a):
    for rb in range(min(a, 512) // 16 * 16, 0, -16):
        if a % rb == 0:
            return rb
    return a


def _row_layout(shapes, n_layers):
    groups = {}
    for name, (a, b) in shapes.items():
        names, first, rows = groups.get(b, ((), {}, 0))
        rb = _shard_row_block(a)
        start = -(-rows // rb) * rb
        groups[b] = (names + (name,), {**first, name: start}, start + n_layers * a)
    return {b: (names, first, -(-rows // FLAT_ROW_BLOCK) * FLAT_ROW_BLOCK) for b, (names, first, rows) in groups.items()}


def _pack_rows(group, width, pieces, dtype):
    names, first, rows = group
    parts, at = [], 0
    for name in names:
        if first[name] > at:
            parts.append(jnp.zeros((first[name] - at, width), dtype))
        parts.append(pieces[name].astype(dtype))
        at = first[name] + pieces[name].shape[0]
    if rows > at:
        parts.append(jnp.zeros((rows - at, width), dtype))
    return jnp.concatenate(parts, axis=0)


def _slab(t, axis, j):
    if t.ndim == 3:
        return t[j]
    n = t.shape[axis - 1] // N_CHIPS
    return lax.slice_in_dim(t, j * n, (j + 1) * n, axis=axis - 1)


def _layer_fwd(h0, mem_n, wl, dims):
    n_sb, n_fx, n_mem, sbw, fxw, memw = dims
    n1, gate1, up1, a1 = _ffn_fwd_up(h0, wl["ffn1_pre_g"], wl["ffn1_w_gate"], wl["ffn1_w_up"])
    h1, f1 = _ffn_fwd_down(a1, wl["ffn1_w_down"], h0, wl["ffn1_post_g"])

    u, proj, fl, sg = _mix_fwd_in(h1, wl["mix_pre_g"], wl["w_in"], wl["w_gate"], wl["b_gate"], wl["b_forget"])
    c = _fox_cumsum(fl)
    S = h0.shape[0]
    tc = _attn_blocks("fox", S, S)[1]
    ct = c[:, :n_fx].T
    ccol, crow = ct.reshape(n_fx, S, 1), ct.reshape(n_fx, S // tc, tc)
    qkv_sb = [(proj, k * sbw) for k in range(3)]
    qkv_fx = [(proj, 3 * sbw + k * fxw) for k in range(3)]
    kv = _matmul(mem_n, wl["w_mem_kv"], out_dtype=BF16, name="mem_kv")
    qkv_mem = [(proj, 3 * sbw + 3 * fxw), (kv, 0), (kv, memw)]
    o_sb, tot_sb = _attn_fwd("sb", *qkv_sb, n_sb, HEAD_DIM)
    o_fx, lse_fx = _attn_fwd("fox", *qkv_fx, n_fx, HEAD_DIM, ccol, crow)
    o_mem, lse_mem = _attn_fwd("mem", *qkv_mem, n_mem, MEM_HEAD_DIM)
    h2, zmix, merged = _mix_fwd_out(o_sb, o_fx, o_mem, sg, wl["w_br_sb"], wl["w_br_fox"], wl["w_br_mem"],
                                    wl["w_out"], h1, wl["mix_post_g"])

    n2, gate2, up2, a2 = _ffn_fwd_up(h2, wl["ffn2_pre_g"], wl["ffn2_w_gate"], wl["ffn2_w_up"])
    h3, f2 = _ffn_fwd_down(a2, wl["ffn2_w_down"], h2, wl["ffn2_post_g"])
    saved = dict(h0=h0, n1=n1, gate1=gate1, up1=up1, a1=a1, f1=f1, h1=h1, u=u, fl=fl, sg=sg,
                 qkv_sb=qkv_sb, qkv_fx=qkv_fx, qkv_mem=qkv_mem, ccol=ccol, crow=crow, o_sb=o_sb, o_fx=o_fx, o_mem=o_mem,
                 tot_sb=tot_sb, lse_fx=lse_fx, lse_mem=lse_mem,
                 zmix=zmix, merged=merged, h2=h2, n2=n2, gate2=gate2, up2=up2, a2=a2, f2=f2)
    return h3, saved


def _ffn_bwd(dh, sv, wl, tag, h_in):
    n, gate, up, a, f = (sv[k + tag] for k in ("n", "gate", "up", "a", "f"))
    pre = "ffn" + tag
    df, dgate, dup, dg_post = _ffn_bwd_down(dh, f, wl[pre + "_post_g"], wl[pre + "_w_down"], gate, up)
    dh_in, dg_pre = _ffn_bwd_up(dgate, dup, wl[pre + "_w_gate"], wl[pre + "_w_up"], h_in, wl[pre + "_pre_g"], dh)
    grads = {pre + "_post_g": dg_post, pre + "_pre_g": dg_pre,
             pre + "_w_down": _matmul(a, df, ta=True, batch="a", name="dw_down"),
             pre + "_w_gate": _matmul(n, dgate, ta=True, batch="b", name="dw_gate"),
             pre + "_w_up": _matmul(n, dup, ta=True, batch="b", name="dw_up")}
    return dh_in, grads


def _layer_bwd(dh3, mem_n, wl, sv, dims):
    n_sb, n_fx, n_mem, sbw, fxw, memw = dims
    S = dh3.shape[0]
    dh2, grads = _ffn_bwd(dh3, sv, wl, "2", sv["h2"])

    (dz, db_sb, db_fx, db_mem, do_sb, do_fx, do_mem, dgp, db_gate, dg_post) = _mix_bwd_out(
        dh2, sv["zmix"], wl["mix_post_g"], wl["w_out"], sv["o_sb"], sv["o_fx"], sv["o_mem"],
        wl["w_br_sb"], wl["w_br_fox"], wl["w_br_mem"], sv["sg"])
    grads["mix_post_g"] = dg_post
    grads["b_gate"] = db_gate
    grads["w_out"] = _matmul(sv["merged"], dz, ta=True, name="dw_out")
    grads["w_br_sb"] = _matmul(sv["o_sb"], db_sb, ta=True, name="dw_br_sb")
    grads["w_br_fox"] = _matmul(sv["o_fx"], db_fx, ta=True, name="dw_br_fox")
    grads["w_br_mem"] = _matmul(sv["o_mem"], db_mem, ta=True, name="dw_br_mem")

    dq_sb, dk_sb, dv_sb = _attn_bwd("sb", *sv["qkv_sb"], sv["o_sb"], do_sb, n_sb, HEAD_DIM, lse=sv["tot_sb"])
    dq_fx, dk_fx, dv_fx, dcrow, dccol = _attn_bwd("fox", *sv["qkv_fx"], sv["o_fx"], do_fx, n_fx, HEAD_DIM,
                                                  sv["ccol"], sv["crow"], sv["lse_fx"])
    dq_mem, dk_mem, dv_mem = _attn_bwd("mem", *sv["qkv_mem"], sv["o_mem"], do_mem, n_mem, MEM_HEAD_DIM,
                                       lse=sv["lse_mem"])
    dkv = jnp.concatenate([dk_mem, dv_mem], axis=1)
    grads["w_mem_kv"] = _matmul(mem_n, dkv, ta=True, name="dw_mem_kv")
    dmem_n = _matmul(dkv, wl["w_mem_kv"], tb=True, out_dtype=F32, name="dmem_n")

    dc = jnp.pad((dcrow.reshape(n_fx, S) + dccol.reshape(n_fx, S)).T, ((0, 0), (0, LANE - n_fx)))
    dfl, db_forget = _fox_dlogit(dc, sv["fl"])
    grads["b_forget"] = db_forget
    dproj = jnp.concatenate([dq_sb, dk_sb, dv_sb, dq_fx, dk_fx, dv_fx, dq_mem, dfl], axis=1)
    dh1, dg_pre = _mix_bwd_in(dproj, dgp, wl["w_in"], wl["w_gate"], sv["h1"], wl["mix_pre_g"], dh2)
    grads["mix_pre_g"] = dg_pre
    grads["w_in"] = _matmul(sv["u"], dproj, ta=True, name="dw_in")
    grads["w_gate"] = _matmul(sv["u"], dgp, ta=True, name="dw_gate_mix")

    dh0, g1 = _ffn_bwd(dh1, sv, wl, "1", sv["h0"])
    grads.update(g1)
    return dh0, grads, dmem_n


def kernel(x, mem, ffn1_pre_g, ffn1_post_g, ffn1_w_gate, ffn1_w_up, ffn1_w_down, mix_pre_g, mix_post_g, w_in, b_forget, mem_norm_g, w_mem_kv, w_gate, b_gate, w_br_sb, w_br_fox, w_br_mem, w_out, ffn2_pre_g, ffn2_post_g, ffn2_w_gate, ffn2_w_up, ffn2_w_down, loss_target, m_ffn1_pre_g, m_ffn1_post_g, m_ffn1_w_gate, m_ffn1_w_up, m_ffn1_w_down, m_mix_pre_g, m_mix_post_g, m_w_in, m_b_forget, m_mem_norm_g, m_w_mem_kv, m_w_gate, m_b_gate, m_w_br_sb, m_w_br_fox, m_w_br_mem, m_w_out, m_ffn2_pre_g, m_ffn2_post_g, m_ffn2_w_gate, m_ffn2_w_up, m_ffn2_w_down, v_ffn1_pre_g, v_ffn1_post_g, v_ffn1_w_gate, v_ffn1_w_up, v_ffn1_w_down, v_mix_pre_g, v_mix_post_g, v_w_in, v_b_forget, v_mem_norm_g, v_w_mem_kv, v_w_gate, v_b_gate, v_w_br_sb, v_w_br_fox, v_w_br_mem, v_w_out, v_ffn2_pre_g, v_ffn2_post_g, v_ffn2_w_gate, v_ffn2_w_up, v_ffn2_w_down):
    args = dict(locals())
    w = {n: args[n] for n in WEIGHTS}
    m = {n: args["m_" + n] for n in WEIGHTS}
    v = {n: args["v_" + n] for n in WEIGHTS}
    L = w["ffn1_pre_g"].shape[0]
    Lh = L // 2
    D = x.shape[2]
    sbw, fxw, memw = w["w_br_sb"].shape[1], w["w_br_fox"].shape[1], w["w_br_mem"].shape[1]
    n_sb, n_fx, n_mem = sbw // HEAD_DIM, fxw // HEAD_DIM, memw // MEM_HEAD_DIM
    dims = (n_sb, n_fx, n_mem, sbw, fxw, memw)
    qkv_w = 3 * sbw + 3 * fxw
    c_idx = lax.axis_index("c")
    c_arr = c_idx.reshape(1).astype(jnp.int32)
    chip_arr = (2 * lax.axis_index("x") + lax.axis_index("y")).reshape(1).astype(jnp.int32)

    shard_shapes = {n: w[n].shape[1:] for n, _ in BIG}
    layout = _row_layout(shard_shapes, Lh)
    widths = list(layout)
    locs = [jnp.stack([_pack_rows(layout[b], b, {n: w[n][hf * Lh:(hf + 1) * Lh].reshape(-1, b) for n in layout[b][0]},
                                  BF16) for hf in range(2)]) for b in widths]
    gathered = dict(zip(widths, _gather_weights(locs, [_place_own(loc, chip_arr) for loc in locs])))

    def layer_weights(l):
        hf, li = divmod(l, Lh)
        wl = {}
        for n, axis in BIG:
            a, b = shard_shapes[n]
            r0 = layout[b][1][n] + li * a
            shards = gathered[b][:, hf, r0:r0 + a]
            if n.startswith("ffn"):
                wl[n] = shards
            else:
                wl[n] = (shards.transpose(1, 0, 2).reshape(a, N_CHIPS * b) if axis == 2 else
                         shards.reshape(N_CHIPS * a, b))
        wi = wl["w_in"]
        wl["w_in"] = jnp.concatenate([wi[:, :qkv_w], wi[:, qkv_w + n_fx:], wi[:, qkv_w:qkv_w + n_fx],
                                      jnp.zeros((D, LANE - n_fx), BF16)], axis=1)
        for n in SMALL:
            if n != "mem_norm_g":
                wl[n] = w[n][l][None, :]
        wl["b_forget"] = jnp.pad(wl["b_forget"], ((0, 0), (0, LANE - n_fx)))
        return wl

    g_mem = w["mem_norm_g"][None, :]

    mem_n = _mem_norm(mem[0], g_mem)
    h, wls, saved = x[0], [], []
    for l in range(L):
        wls.append(layer_weights(l))
        h, sv = _layer_fwd(h, mem_n, wls[l], dims)
        saved.append(sv)
    dh, loss_tile = _loss_head(h, loss_target[0])
    loss = lax.psum(loss_tile[0, 0], ("x", "y", "c"))
    gl, dmem_n = [None] * L, [None] * L
    for l in reversed(range(L)):
        dh, gl[l], dmem_n[l] = _layer_bwd(dh, mem_n, wls[l], saved[l], dims)
        gi = gl[l]["w_in"]
        gl[l]["w_in"] = jnp.concatenate([gi[:, :qkv_w], gi[:, qkv_w + memw:qkv_w + memw + n_fx],
                                         gi[:, qkv_w:qkv_w + memw]], axis=1)
    grad_x = dh
    g_mem_norm = _mem_norm_bwd(mem[0], g_mem, jnp.stack(dmem_n))

    axis_of = dict(BIG)
    partials = [jnp.stack([jnp.stack([
        _pack_rows(layout[b], b, {n: jnp.concatenate([_slab(gl[hf * Lh + li][n], axis_of[n], j) for li in range(Lh)])
                                  for n in layout[b][0]}, BF16)
        for j in range(N_CHIPS)]) for hf in range(2)]) for b in widths]
    pairs = [_pair_sum(g, sib, c_arr) for g, sib in zip(partials, _pair_exchange(partials))]
    mines = [_chip_sum(p, r, chip_arr) for p, r in zip(pairs, _chip_exchange(pairs))]
    theirs = _pair_swap(mines)

    grad, delta, new_m, new_v = {}, {}, {}, {}
    for n, _ in BIG:
        k = widths.index(shard_shapes[n][1])
        grad[n], delta[n], new_m[n], new_v[n] = _adamw_reduced(
            w[n], m[n], v[n], mines[k], theirs[k], c_arr, layout[widths[k]][1][n], name="adamw_" + n)

    small_local = {n: (g_mem_norm if n == "mem_norm_g" else
                       jnp.concatenate([gl[l][n][:, :n_fx] if n == "b_forget" else gl[l][n] for l in range(L)]))
                   for n in SMALL}
    small_shapes = [small_local[n].shape for n in SMALL]
    small_sum = _unpack_flat(_all_reduce_small(_pack_flat([small_local[n] for n in SMALL], F32, row_block=16)),
                             small_shapes)
    for n, t in zip(SMALL, small_sum):
        shp = w[n].shape
        two_d = (1, shp[0]) if len(shp) == 1 else shp
        grad[n] = t.reshape(shp)
        d_, m_, v_ = _adamw(w[n].reshape(two_d), t.reshape(two_d), m[n].reshape(two_d), v[n].reshape(two_d),
                            name="adamw_" + n)
        delta[n], new_m[n], new_v[n] = d_.reshape(shp), m_.reshape(shp), v_.reshape(shp)

    return (loss, grad_x[None], *[grad[n] for n in WEIGHTS], *[delta[n] for n in WEIGHTS],
            *[new_m[n] for n in WEIGHTS], *[new_v[n] for n in WEIGHTS])
```

```python
import math

import jax
import jax.numpy as jnp
from jax import lax
from jax.experimental import pallas as pl
from jax.experimental.pallas import tpu as pltpu

F32 = jnp.float32
BF16 = jnp.bfloat16
RMS_EPS = 1e-6
HEAD_DIM = 64
MEM_HEAD_DIM = 128
LANE = 128
V7X_VMEM_LIMIT_BYTES = 56 * 1024 * 1024
FLAT_COLS = 512
FLAT_UNIT = 16 * FLAT_COLS
FLAT_ROW_BLOCK = 512
N_CHIPS = 4
N_DEV = 8
NEG = float(jnp.finfo(jnp.float32).min)

ADAM_LR = 0.001
ADAM_B1 = 0.9
ADAM_B2 = 0.999
ADAM_EPS = 1e-08
ADAM_WD = 0.01
ADAM_STEP = 10

BIG = (("ffn1_w_gate", 2), ("ffn1_w_up", 2), ("ffn1_w_down", 1), ("w_in", 2), ("w_mem_kv", 1), ("w_gate", 2),
       ("w_br_sb", 2), ("w_br_fox", 2), ("w_br_mem", 2), ("w_out", 1),
       ("ffn2_w_gate", 2), ("ffn2_w_up", 2), ("ffn2_w_down", 1))
SMALL = ("ffn1_pre_g", "ffn1_post_g", "mix_pre_g", "mix_post_g", "b_forget", "mem_norm_g", "b_gate",
         "ffn2_pre_g", "ffn2_post_g")
WEIGHTS = ("ffn1_pre_g", "ffn1_post_g", "ffn1_w_gate", "ffn1_w_up", "ffn1_w_down", "mix_pre_g", "mix_post_g", "w_in",
           "b_forget", "mem_norm_g", "w_mem_kv", "w_gate", "b_gate", "w_br_sb", "w_br_fox", "w_br_mem", "w_out",
           "ffn2_pre_g", "ffn2_post_g", "ffn2_w_gate", "ffn2_w_up", "ffn2_w_down")


def _params(**kw):
    return pltpu.CompilerParams(vmem_limit_bytes=V7X_VMEM_LIMIT_BYTES, **kw)


def _dot(a, b):
    return jnp.dot(a, b, preferred_element_type=F32)


def _dot_nt(a, b):
    return lax.dot_general(a, b, (((1,), (1,)), ((), ())), preferred_element_type=F32)


def _dot_tn(a, b):
    return lax.dot_general(a, b, (((0,), (0,)), ((), ())), preferred_element_type=F32)


def _rms(t, g):
    return t * lax.rsqrt(jnp.mean(t * t, axis=-1, keepdims=True) + RMS_EPS) * g


def _pick(dim, pref):
    if dim <= pref:
        return dim
    for cand in range(pref - pref % LANE, 0, -LANE):
        if dim % cand == 0:
            return cand
    return dim


def _rows(bm, cols):
    return pl.BlockSpec((bm, cols), lambda i: (i, 0))


def _whole(shape):
    nd = len(shape)
    return pl.BlockSpec(shape, lambda i: (0,) * nd)


def _split3(x):
    hi = x.astype(BF16)
    r1 = x - hi.astype(F32)
    mid = r1.astype(BF16)
    lo = (r1 - mid.astype(F32)).astype(BF16)
    return hi, mid, lo


def _cumdot(x, tri):
    hi = x.astype(BF16)
    lo = (x - hi.astype(F32)).astype(BF16)
    return _dot(hi, tri) + _dot(lo, tri)


def _slabs(bm, cols):
    return pl.BlockSpec((N_CHIPS, bm, cols), lambda i: (0, i, 0))


def _ffn_fwd_up(h, g_pre, wg, wu):
    S, D = h.shape
    Fs = wg.shape[2]
    bm = _pick(S, 256)

    def body(h_ref, g_ref, wg_ref, wu_ref, n_ref, gate_ref, up_ref, a_ref):
        n = _rms(h_ref[...], g_ref[...]).astype(BF16)
        n_ref[...] = n
        for j in range(N_CHIPS):
            gate = _dot(n, wg_ref[j])
            up = _dot(n, wu_ref[j])
            gate_ref[j] = gate.astype(BF16)
            up_ref[j] = up.astype(BF16)
            a_ref[j] = (gate * jax.nn.sigmoid(gate) * up).astype(BF16)

    return pl.pallas_call(
        body, name="ffn_fwd_up", grid=(S // bm,),
        in_specs=[_rows(bm, D), _whole((1, D)), _whole((N_CHIPS, D, Fs)), _whole((N_CHIPS, D, Fs))],
        out_specs=[_rows(bm, D), _slabs(bm, Fs), _slabs(bm, Fs), _slabs(bm, Fs)],
        out_shape=[jax.ShapeDtypeStruct((S, D), BF16)] + [jax.ShapeDtypeStruct((N_CHIPS, S, Fs), BF16)] * 3,
        compiler_params=_params(),
    )(h, g_pre, wg, wu)


def _ffn_fwd_down(a, wd, h, g_post):
    _, S, Fs = a.shape
    D = wd.shape[2]
    bm = _pick(S, 256)

    def body(a_ref, wd_ref, h_ref, g_ref, hout_ref, f_ref):
        f = _dot(a_ref[0], wd_ref[0])
        for j in range(1, N_CHIPS):
            f = f + _dot(a_ref[j], wd_ref[j])
        f_ref[...] = f
        hout_ref[...] = h_ref[...] + 0.5 * _rms(f, g_ref[...])

    return pl.pallas_call(
        body, name="ffn_fwd_down", grid=(S // bm,),
        in_specs=[_slabs(bm, Fs), _whole((N_CHIPS, Fs, D)), _rows(bm, D), _whole((1, D))],
        out_specs=[_rows(bm, D), _rows(bm, D)],
        out_shape=[jax.ShapeDtypeStruct((S, D), F32)] * 2,
        compiler_params=_params(),
    )(a, wd, h, g_post)


def _ffn_bwd_down(dh, f, g_post, wd, gate, up):
    S, D = dh.shape
    Fs = wd.shape[1]
    bm = _pick(S, 256)

    def body(dh_ref, f_ref, g_ref, wd_ref, gate_ref, up_ref, df_ref, dgate_ref, dup_ref, dg_ref):
        _, vjp = jax.vjp(lambda t, g: 0.5 * _rms(t, g), f_ref[...], g_ref[...])
        df, dg = vjp(dh_ref[...])

        @pl.when(pl.program_id(0) == 0)
        def _():
            dg_ref[...] = jnp.zeros_like(dg_ref)

        dg_ref[...] += dg
        dfb = df.astype(BF16)
        df_ref[...] = dfb
        for j in range(N_CHIPS):
            da = _dot_nt(dfb, wd_ref[j])
            gt = gate_ref[j].astype(F32)
            sig = jax.nn.sigmoid(gt)
            silu = gt * sig
            dup_ref[j] = (da * silu).astype(BF16)
            dgate_ref[j] = (da * up_ref[j].astype(F32) * (sig + silu * (1.0 - sig))).astype(BF16)

    return pl.pallas_call(
        body, name="ffn_bwd_down", grid=(S // bm,),
        in_specs=[_rows(bm, D), _rows(bm, D), _whole((1, D)), _whole((N_CHIPS, Fs, D)), _slabs(bm, Fs),
                  _slabs(bm, Fs)],
        out_specs=[_rows(bm, D), _slabs(bm, Fs), _slabs(bm, Fs), _whole((1, D))],
        out_shape=[jax.ShapeDtypeStruct((S, D), BF16), jax.ShapeDtypeStruct((N_CHIPS, S, Fs), BF16),
                   jax.ShapeDtypeStruct((N_CHIPS, S, Fs), BF16), jax.ShapeDtypeStruct((1, D), F32)],
        compiler_params=_params(),
    )(dh, f, g_post, wd, gate, up)


def _ffn_bwd_up(dgate, dup, wg, wu, h_in, g_pre, dh):
    _, S, Fs = dgate.shape
    D = wg.shape[1]
    bm = _pick(S, 256)

    def body(dgate_ref, dup_ref, wg_ref, wu_ref, h_ref, g_ref, dh_ref, dhin_ref, dg_ref):
        dn = _dot_nt(dgate_ref[0], wg_ref[0]) + _dot_nt(dup_ref[0], wu_ref[0])
        for j in range(1, N_CHIPS):
            dn = dn + _dot_nt(dgate_ref[j], wg_ref[j]) + _dot_nt(dup_ref[j], wu_ref[j])
        _, vjp = jax.vjp(_rms, h_ref[...], g_ref[...])
        dhx, dg = vjp(dn)

        @pl.when(pl.program_id(0) == 0)
        def _():
            dg_ref[...] = jnp.zeros_like(dg_ref)

        dg_ref[...] += dg
        dhin_ref[...] = dh_ref[...] + dhx

    return pl.pallas_call(
        body, name="ffn_bwd_up", grid=(S // bm,),
        in_specs=[_slabs(bm, Fs), _slabs(bm, Fs), _whole((N_CHIPS, D, Fs)), _whole((N_CHIPS, D, Fs)), _rows(bm, D),
                  _whole((1, D)), _rows(bm, D)],
        out_specs=[_rows(bm, D), _whole((1, D))],
        out_shape=[jax.ShapeDtypeStruct((S, D), F32), jax.ShapeDtypeStruct((1, D), F32)],
        compiler_params=_params(),
    )(dgate, dup, wg, wu, h_in, g_pre, dh)


def _matmul(a, b, *, ta=False, tb=False, out_dtype=BF16, name, batch=None):
    n_batch = a.shape[0] if batch == "a" else b.shape[0] if batch == "b" else 1
    a_shape = a.shape[1:] if batch == "a" else a.shape
    b_shape = b.shape[1:] if batch == "b" else b.shape
    M, K = (a_shape[1], a_shape[0]) if ta else a_shape
    N = b_shape[0] if tb else b_shape[1]
    acc_budget = 12 * 1024 * 1024
    bm, bk = _pick(M, 1536), _pick(K, 512)
    while n_batch * N * bm * 4 > acc_budget and bm % (2 * LANE) == 0:
        bm //= 2
    bn = N if n_batch * N * bm * 4 <= acc_budget else _pick(N, 1536)
    nk = K // bk

    def body(a_ref, b_ref, o_ref, acc_ref):
        kk = pl.program_id(2)

        @pl.when(kk == 0)
        def _():
            acc_ref[...] = jnp.zeros_like(acc_ref)

        dims = (((0 if ta else 1,), (1 if tb else 0,)), ((), ()))
        if batch is None:
            acc_ref[...] += lax.dot_general(a_ref[...], b_ref[...], dims, preferred_element_type=F32)
        else:
            for g in range(n_batch):
                av = a_ref[g] if batch == "a" else a_ref[...]
                bv = b_ref[g] if batch == "b" else b_ref[...]
                acc_ref[g] += lax.dot_general(av, bv, dims, preferred_element_type=F32)

        @pl.when(kk == nk - 1)
        def _():
            o_ref[...] = acc_ref[...].astype(o_ref.dtype)

    def spec(block, index, batched):
        if batched:
            return pl.BlockSpec((n_batch,) + block, lambda i, j, k: (0,) + index(i, j, k))
        return pl.BlockSpec(block, index)

    a_spec = spec((bk, bm), lambda i, j, k: (k, i), batch == "a") if ta else \
        spec((bm, bk), lambda i, j, k: (i, k), batch == "a")
    b_spec = spec((bn, bk), lambda i, j, k: (j, k), batch == "b") if tb else \
        spec((bk, bn), lambda i, j, k: (k, j), batch == "b")
    lead = (n_batch,) if batch else ()
    return pl.pallas_call(
        body, name=name, grid=(M // bm, N // bn, nk),
        in_specs=[a_spec, b_spec],
        out_specs=spec((bm, bn), lambda i, j, k: (i, j), batch is not None),
        out_shape=jax.ShapeDtypeStruct(lead + (M, N), out_dtype),
        scratch_shapes=[pltpu.VMEM(lead + (bm, bn), F32)],
        compiler_params=_params(),
    )(a, b)


def _mix_fwd_in(h, g_pre, win, wgate, b_gate, b_forget):
    S, D = h.shape
    PW = win.shape[1] - LANE
    G = wgate.shape[1]
    bm = _pick(S, 256)

    def body(h_ref, g_ref, win_ref, wgate_ref, bg_ref, bf_ref, u_ref, proj_ref, fl_ref, sg_ref):
        u = _rms(h_ref[...], g_ref[...]).astype(BF16)
        u_ref[...] = u
        proj = _dot(u, win_ref[...])
        proj_ref[...] = proj[:, :PW].astype(BF16)
        fl_ref[...] = proj[:, PW:] + bf_ref[...]
        sg_ref[...] = jax.nn.sigmoid(_dot(u, wgate_ref[...]) + bg_ref[...]).astype(BF16)

    return pl.pallas_call(
        body, name="mix_fwd_in", grid=(S // bm,),
        in_specs=[_rows(bm, D), _whole((1, D)), _whole((D, PW + LANE)), _whole((D, G)), _whole((1, G)),
                  _whole((1, LANE))],
        out_specs=[_rows(bm, D), _rows(bm, PW), _rows(bm, LANE), _rows(bm, G)],
        out_shape=[jax.ShapeDtypeStruct((S, D), BF16), jax.ShapeDtypeStruct((S, PW), BF16),
                   jax.ShapeDtypeStruct((S, LANE), F32), jax.ShapeDtypeStruct((S, G), BF16)],
        compiler_params=_params(),
    )(h, g_pre, win, wgate, b_gate, b_forget)


def _mix_fwd_out(o_sb, o_fx, o_mem, sg, w_sb, w_fx, w_mem, w_out, h, g_post):
    S, D = h.shape
    bm = _pick(S, 256)
    widths = (o_sb.shape[1], o_fx.shape[1], o_mem.shape[1])

    def body(osb_ref, ofx_ref, omem_ref, sg_ref, wsb_ref, wfx_ref, wmem_ref, wout_ref, h_ref, g_ref,
             hout_ref, z_ref, merged_ref):
        s = sg_ref[...].astype(F32)
        merged = (s[:, :D] * _dot(osb_ref[...], wsb_ref[...]) + s[:, D:2 * D] * _dot(ofx_ref[...], wfx_ref[...])
                  + s[:, 2 * D:] * _dot(omem_ref[...], wmem_ref[...]))
        mb = merged.astype(BF16)
        merged_ref[...] = mb
        z = _dot(mb, wout_ref[...])
        z_ref[...] = z
        hout_ref[...] = h_ref[...] + _rms(z, g_ref[...])

    return pl.pallas_call(
        body, name="mix_fwd_out", grid=(S // bm,),
        in_specs=[_rows(bm, widths[0]), _rows(bm, widths[1]), _rows(bm, widths[2]), _rows(bm, 3 * D),
                  _whole((widths[0], D)), _whole((widths[1], D)), _whole((widths[2], D)), _whole((D, D)),
                  _rows(bm, D), _whole((1, D))],
        out_specs=[_rows(bm, D), _rows(bm, D), _rows(bm, D)],
        out_shape=[jax.ShapeDtypeStruct((S, D), F32), jax.ShapeDtypeStruct((S, D), F32),
                   jax.ShapeDtypeStruct((S, D), BF16)],
        compiler_params=_params(),
    )(o_sb, o_fx, o_mem, sg, w_sb, w_fx, w_mem, w_out, h, g_post)


def _mix_bwd_out(dh, z, g_post, w_out, o_sb, o_fx, o_mem, w_sb, w_fx, w_mem, sg):
    S, D = dh.shape
    bm = _pick(S, 256)
    widths = (o_sb.shape[1], o_fx.shape[1], o_mem.shape[1])

    def body(dh_ref, z_ref, g_ref, wout_ref, osb_ref, ofx_ref, omem_ref, wsb_ref, wfx_ref, wmem_ref, sg_ref,
             dz_ref, dbsb_ref, dbfx_ref, dbmem_ref, dosb_ref, dofx_ref, domem_ref, dgp_ref, dbg_ref, dg_ref):
        _, vjp = jax.vjp(_rms, z_ref[...], g_ref[...])
        dz, dg = vjp(dh_ref[...])

        @pl.when(pl.program_id(0) == 0)
        def _():
            dg_ref[...] = jnp.zeros_like(dg_ref)
            dbg_ref[...] = jnp.zeros_like(dbg_ref)

        dg_ref[...] += dg
        dzb = dz.astype(BF16)
        dz_ref[...] = dzb
        dmerged = _dot_nt(dzb, wout_ref[...])
        s = sg_ref[...].astype(F32)
        branches = ((osb_ref, wsb_ref, dbsb_ref, dosb_ref), (ofx_ref, wfx_ref, dbfx_ref, dofx_ref),
                    (omem_ref, wmem_ref, dbmem_ref, domem_ref))
        for k, (o_ref, w_ref, db_ref, do_ref) in enumerate(branches):
            gs = s[:, k * D:(k + 1) * D]
            dbb = (dmerged * gs).astype(BF16)
            db_ref[...] = dbb
            do_ref[...] = _dot_nt(dbb, w_ref[...]).astype(BF16)
            dgp = dmerged * _dot(o_ref[...], w_ref[...]) * gs * (1.0 - gs)
            dgp_ref[:, k * D:(k + 1) * D] = dgp.astype(BF16)
            dbg_ref[:, k * D:(k + 1) * D] += jnp.sum(dgp, axis=0, keepdims=True)

    return pl.pallas_call(
        body, name="mix_bwd_out", grid=(S // bm,),
        in_specs=[_rows(bm, D), _rows(bm, D), _whole((1, D)), _whole((D, D)),
                  _rows(bm, widths[0]), _rows(bm, widths[1]), _rows(bm, widths[2]),
                  _whole((widths[0], D)), _whole((widths[1], D)), _whole((widths[2], D)), _rows(bm, 3 * D)],
        out_specs=[_rows(bm, D)] * 4 + [_rows(bm, widths[0]), _rows(bm, widths[1]), _rows(bm, widths[2]),
                                        _rows(bm, 3 * D), _whole((1, 3 * D)), _whole((1, D))],
        out_shape=[jax.ShapeDtypeStruct((S, D), BF16)] * 4
        + [jax.ShapeDtypeStruct((S, w), BF16) for w in widths]
        + [jax.ShapeDtypeStruct((S, 3 * D), BF16), jax.ShapeDtypeStruct((1, 3 * D), F32),
           jax.ShapeDtypeStruct((1, D), F32)],
        compiler_params=_params(),
    )(dh, z, g_post, w_out, o_sb, o_fx, o_mem, w_sb, w_fx, w_mem, sg)


def _mix_bwd_in(dproj, dgp, win, wgate, h_in, g_pre, dh):
    S, PWL = dproj.shape
    G = dgp.shape[1]
    D = h_in.shape[1]
    bm = _pick(S, 256)

    def body(dproj_ref, dgp_ref, win_ref, wgate_ref, h_ref, g_ref, dh_ref, dhin_ref, dg_ref):
        du = _dot_nt(dproj_ref[...], win_ref[...]) + _dot_nt(dgp_ref[...], wgate_ref[...])
        _, vjp = jax.vjp(_rms, h_ref[...], g_ref[...])
        dhx, dg = vjp(du)

        @pl.when(pl.program_id(0) == 0)
        def _():
            dg_ref[...] = jnp.zeros_like(dg_ref)

        dg_ref[...] += dg
        dhin_ref[...] = dh_ref[...] + dhx

    return pl.pallas_call(
        body, name="mix_bwd_in", grid=(S // bm,),
        in_specs=[_rows(bm, PWL), _rows(bm, G), _whole((D, PWL)), _whole((D, G)), _rows(bm, D), _whole((1, D)),
                  _rows(bm, D)],
        out_specs=[_rows(bm, D), _whole((1, D))],
        out_shape=[jax.ShapeDtypeStruct((S, D), F32), jax.ShapeDtypeStruct((1, D), F32)],
        compiler_params=_params(),
    )(dproj, dgp, win, wgate, h_in, g_pre, dh)


def _log_sigmoid(x):
    return jnp.minimum(x, 0.0) - jnp.log(1.0 + jnp.exp(-jnp.abs(x)))


def _fox_cumsum(fl):
    S = fl.shape[0]
    rb = _pick(S, LANE)

    def body(fl_ref, c_ref, carry_ref):
        @pl.when(pl.program_id(0) == 0)
        def _():
            carry_ref[...] = jnp.zeros_like(carry_ref)

        r = lax.broadcasted_iota(jnp.int32, (rb, rb), 0)
        cidx = lax.broadcasted_iota(jnp.int32, (rb, rb), 1)
        tri = (cidx <= r).astype(BF16)
        hi, mid, lo = _split3(_log_sigmoid(fl_ref[...]))
        c = _dot(tri, hi) + _dot(tri, mid) + _dot(tri, lo) + carry_ref[...]
        c_ref[...] = c
        carry_ref[...] = c[rb - 1:rb, :]

    return pl.pallas_call(
        body, name="fox_cumsum", grid=(S // rb,),
        in_specs=[_rows(rb, LANE)], out_specs=_rows(rb, LANE),
        out_shape=jax.ShapeDtypeStruct((S, LANE), F32),
        scratch_shapes=[pltpu.VMEM((1, LANE), F32)],
        compiler_params=_params(),
    )(fl)


def _fox_dlogit(dc, fl):
    S = fl.shape[0]
    rb = _pick(S, LANE)
    nb = S // rb

    def body(dc_ref, fl_ref, dfl_ref, dbf_ref, carry_ref):
        @pl.when(pl.program_id(0) == 0)
        def _():
            carry_ref[...] = jnp.zeros_like(carry_ref)
            dbf_ref[...] = jnp.zeros_like(dbf_ref)

        r = lax.broadcasted_iota(jnp.int32, (rb, rb), 0)
        cidx = lax.broadcasted_iota(jnp.int32, (rb, rb), 1)
        tri = (cidx >= r).astype(BF16)
        hi, mid, lo = _split3(dc_ref[...])
        rc = _dot(tri, hi) + _dot(tri, mid) + _dot(tri, lo) + carry_ref[...]
        carry_ref[...] = rc[0:1, :]
        dfl = rc * jax.nn.sigmoid(-fl_ref[...])
        dfl_ref[...] = dfl.astype(BF16)
        dbf_ref[...] += jnp.sum(dfl, axis=0, keepdims=True)

    rev = pl.BlockSpec((rb, LANE), lambda i: (nb - 1 - i, 0))
    return pl.pallas_call(
        body, name="fox_dlogit", grid=(nb,),
        in_specs=[rev, rev], out_specs=[rev, _whole((1, LANE))],
        out_shape=[jax.ShapeDtypeStruct((S, LANE), BF16), jax.ShapeDtypeStruct((1, LANE), F32)],
        scratch_shapes=[pltpu.VMEM((1, LANE), F32)],
        compiler_params=_params(),
    )(dc, fl)


def _attn_blocks(kind, S, Sk, backward=False):
    tq = _pick(S, 1024 if backward else 2048)
    tc = LANE if kind == "sb" else _pick(Sk, 256)
    return tq, tc


def _is_power_of_two(x):
    return math.frexp(x)[0] == 0.5


def _sb_logs(z):
    ln = -jnp.maximum(z, 0.0) - jnp.log(1.0 + jnp.exp(-jnp.abs(z)))
    return ln + z, ln


def _head_lanes(pack, dh):
    lane = lax.broadcasted_iota(jnp.int32, (1, LANE), 1)
    return [(lane >= hh * dh) & (lane < (hh + 1) * dh) for hh in range(pack)]


def _by_head(sel, parts):
    out = parts[0]
    for hh in range(1, len(parts)):
        out = jnp.where(sel[hh], parts[hh], out)
    return out


def _only_head(sel, hh, x):
    return x if len(sel) == 1 else jnp.where(sel[hh], x, jnp.zeros_like(x))


def _tail(x, r0):
    return x if not r0 else x[r0:]


def _put_tail(x, tail, r0):
    return tail if not r0 else jnp.concatenate([x[:r0], tail], axis=0)


def _add_tail(x, tail, r0):
    return x + tail if not r0 else jnp.concatenate([x[:r0], x[r0:] + tail], axis=0)


def _q_cols(tq, first):
    return pl.BlockSpec((tq, LANE), lambda g, i: (i, first // LANE + g))


def _k_cols(rows, first):
    return pl.BlockSpec((rows, LANE), lambda g, i: (0, first // LANE + g))


def _attn_fwd(kind, q, k, v, n_heads, dh, ccol=None, crow=None):
    (qa, q0), (ka, k0), (va, v0) = q, k, v
    S, Sk = qa.shape[0], ka.shape[0]
    pack = LANE // dh
    tq, tc = _attn_blocks(kind, S, Sk)
    scale = dh ** -0.5
    fold = _is_power_of_two(scale)
    causal = kind != "mem"
    n_diag = tq // tc if causal else 0
    unroll = 2 if causal else 1
    assert n_diag % unroll == 0 and (Sk // tc) % unroll == 0

    def body(*refs):
        if kind == "fox":
            q_ref, k_ref, v_ref, cc_ref, cr_ref, o_ref, lse_ref = refs
        else:
            q_ref, k_ref, v_ref, o_ref, lse_ref = refs
        i = pl.program_id(1)
        n_full = (i * tq) // tc if causal else Sk // tc
        qpos = i * tq + lax.broadcasted_iota(jnp.int32, (tq, tc), 0)
        kio = lax.broadcasted_iota(jnp.int32, (tq, tc), 1)
        heads = range(pack)
        sel = _head_lanes(pack, dh)
        q2 = q_ref[...] * scale if fold else q_ref[...]
        qs = [_only_head(sel, hh, q2) for hh in heads]

        def kv(jc):
            off = pl.multiple_of(jc * tc, tc)
            return off, k_ref[pl.ds(off, tc), :], v_ref[pl.ds(off, tc), :]

        if kind == "sb":
            tri = (lax.broadcasted_iota(jnp.int32, (tc, tc), 0) > lax.broadcasted_iota(jnp.int32, (tc, tc), 1)
                   ).astype(BF16)

            def chunk(jc, r0, runs, acc):
                off, k2, v2 = kv(jc)
                new_runs, pv = [], []
                for hh in heads:
                    lb, ln = _sb_logs(_dot_nt(_tail(qs[hh], r0), k2))
                    if r0 is not None:
                        mask = (off + _tail(kio, r0)) < _tail(qpos, r0)
                        ln = jnp.where(mask, ln, 0.0)
                    w = jnp.exp(lb + _cumdot(ln, tri) + _tail(runs[hh], r0))
                    if r0 is not None:
                        w = jnp.where(mask, w, 0.0)
                    pv.append(_dot(w.astype(BF16), v2))
                    new_runs.append(_add_tail(runs[hh], jnp.sum(ln, axis=1, keepdims=True), r0))
                return tuple(new_runs), _add_tail(acc, _by_head(sel, pv), r0)

            state = (tuple(jnp.zeros((tq, 1), F32) for _ in heads), jnp.zeros((tq, LANE), F32))
            for d in range(n_diag - 1, -1, -1):
                state = chunk(n_full + d, d * tc, *state)

            def trip(t, st):
                for u in range(unroll):
                    st = chunk(n_full - 1 - unroll * t - u, None, *st)
                return st

            runs, acc = lax.fori_loop(0, n_full // unroll, trip, state)
            o_ref[...] = acc.astype(o_ref.dtype)
            for hh in heads:
                lse_ref[hh] = runs[hh]
        else:
            def chunk(jc, r0, ms, ls, acc):
                off, k2, v2 = kv(jc)
                new_ms, new_ls, alphas, pv = [], [], [], []
                for hh in heads:
                    z = _dot_nt(_tail(qs[hh], r0), k2)
                    if not fold:
                        z = z * scale
                    if kind == "fox":
                        z = z + _tail(cc_ref[hh], r0) - cr_ref[hh, pl.ds(jc, 1), :]
                    if r0 is not None:
                        z = jnp.where((off + _tail(kio, r0)) <= _tail(qpos, r0), z, NEG)
                    m_old, l_old = _tail(ms[hh], r0), _tail(ls[hh], r0)
                    m_new = jnp.maximum(m_old, jnp.max(z, axis=1, keepdims=True))
                    alpha = jnp.exp(m_old - m_new)
                    p = jnp.exp(z - m_new)
                    new_ms.append(_put_tail(ms[hh], m_new, r0))
                    new_ls.append(_put_tail(ls[hh], alpha * l_old + jnp.sum(p, axis=1, keepdims=True), r0))
                    alphas.append(alpha)
                    pv.append(_dot(p.astype(BF16), v2))
                acc_new = _by_head(sel, alphas) * _tail(acc, r0) + _by_head(sel, pv)
                return tuple(new_ms), tuple(new_ls), _put_tail(acc, acc_new, r0)

            state = (tuple(jnp.full((tq, 1), NEG, F32) for _ in heads), tuple(jnp.zeros((tq, 1), F32) for _ in heads),
                     jnp.zeros((tq, LANE), F32))

            def trip(t, st):
                for u in range(unroll):
                    st = chunk(unroll * t + u, None, *st)
                return st

            state = lax.fori_loop(0, n_full // unroll, trip, state)
            for d in range(n_diag):
                state = chunk(n_full + d, d * tc, *state)
            ms, ls, acc = state
            o_ref[...] = (acc / _by_head(sel, ls)).astype(o_ref.dtype)
            for hh in heads:
                lse_ref[hh] = ms[hh] + jnp.log(ls[hh])

    colspec = pl.BlockSpec((pack, tq, 1), lambda g, i: (g, i, 0))
    in_specs, args = [_q_cols(tq, q0), _k_cols(Sk, k0), _k_cols(Sk, v0)], [qa, ka, va]
    if kind == "fox":
        in_specs += [colspec, pl.BlockSpec((pack, Sk // tc, tc), lambda g, i: (g, 0, 0))]
        args += [ccol, crow]
    return pl.pallas_call(
        body, name="attn_fwd_" + kind, grid=(n_heads // pack, S // tq),
        in_specs=in_specs, out_specs=[_q_cols(tq, 0), colspec],
        out_shape=[jax.ShapeDtypeStruct((S, n_heads * dh), BF16), jax.ShapeDtypeStruct((n_heads, S, 1), F32)],
        compiler_params=_params(),
    )(*args)


def _attn_bwd(kind, q, k, v, o, do, n_heads, dh, ccol=None, crow=None, lse=None):
    (qa, q0), (ka, k0), (va, v0) = q, k, v
    S, Sk = qa.shape[0], ka.shape[0]
    pack = LANE // dh
    tq, tc = _attn_blocks(kind, S, Sk, backward=True)
    scale = dh ** -0.5
    fold = _is_power_of_two(scale)
    causal = kind != "mem"
    n_diag = tq // tc if causal else 0
    unroll = 2 if kind == "sb" else 1
    assert n_diag % unroll == 0 and (Sk // tc) % unroll == 0
    nq = S // tq

    def body(*refs):
        if kind == "fox":
            (q_ref, k_ref, v_ref, o_ref, do_ref, cc_ref, cr_ref, lse_ref,
             dq_ref, dk_ref, dv_ref, dc_ref, dcc_ref, dk_acc, dv_acc, dc_acc) = refs
        else:
            q_ref, k_ref, v_ref, o_ref, do_ref, lse_ref, dq_ref, dk_ref, dv_ref, dk_acc, dv_acc = refs
        i = pl.program_id(1)

        @pl.when(i == 0)
        def _():
            dk_acc[...] = jnp.zeros_like(dk_acc)
            dv_acc[...] = jnp.zeros_like(dv_acc)
            if kind == "fox":
                dc_acc[...] = jnp.zeros_like(dc_acc)

        n_full = (i * tq) // tc if causal else Sk // tc
        qpos = i * tq + lax.broadcasted_iota(jnp.int32, (tq, tc), 0)
        kio = lax.broadcasted_iota(jnp.int32, (tq, tc), 1)
        heads = range(pack)
        sel = _head_lanes(pack, dh)
        q2 = q_ref[...] * scale if fold else q_ref[...]
        do2 = do_ref[...]
        qs = [_only_head(sel, hh, q2) for hh in heads]
        dos = [_only_head(sel, hh, do2) for hh in heads]

        def kv(jc):
            off = pl.multiple_of(jc * tc, tc)
            return off, k_ref[pl.ds(off, tc), :], v_ref[pl.ds(off, tc), :]

        def accumulate(off, k2, dzb, wb, dq, r0):
            q2t, do2t = _tail(q2, r0), _tail(do2, r0)
            dk_acc[pl.ds(off, tc), :] += _by_head(sel, [_dot_tn(dzb[hh], q2t) for hh in heads])
            dv_acc[pl.ds(off, tc), :] += _by_head(sel, [_dot_tn(wb[hh], do2t) for hh in heads])
            return _add_tail(dq, _by_head(sel, [_dot(dzb[hh], k2) for hh in heads]), r0)

        if kind == "sb":
            r = lax.broadcasted_iota(jnp.int32, (tc, tc), 0)
            cidx = lax.broadcasted_iota(jnp.int32, (tc, tc), 1)
            tri_inc = (r <= cidx).astype(BF16)
            tri_exc = (r < cidx).astype(BF16)

            def chunk(jc, r0, pres, pres_e, dq):
                off, k2, v2 = kv(jc)
                new_pres, new_pres_e, dzb, wb = [], [], [], []
                for hh in heads:
                    lb, ln = _sb_logs(_dot_nt(_tail(qs[hh], r0), k2))
                    if r0 is not None:
                        mask = (off + _tail(kio, r0)) < _tail(qpos, r0)
                        ln = jnp.where(mask, ln, 0.0)
                    w = jnp.exp(lb + (_tail(lse_ref[hh], r0) - _tail(pres[hh], r0) - _cumdot(ln, tri_inc)))
                    if r0 is not None:
                        w = jnp.where(mask, w, 0.0)
                    e = w * _dot_nt(_tail(dos[hh], r0), v2)
                    beta = jnp.exp(lb)
                    dz = e * (1.0 - beta) - beta * (_tail(pres_e[hh], r0) + _cumdot(e, tri_exc))
                    if r0 is not None:
                        dz = jnp.where(mask, dz, 0.0)
                    dzb.append(dz.astype(BF16))
                    wb.append(w.astype(BF16))
                    new_pres.append(_add_tail(pres[hh], jnp.sum(ln, axis=1, keepdims=True), r0))
                    new_pres_e.append(_add_tail(pres_e[hh], jnp.sum(e, axis=1, keepdims=True), r0))
                return tuple(new_pres), tuple(new_pres_e), accumulate(off, k2, dzb, wb, dq, r0)

            state = (tuple(jnp.zeros((tq, 1), F32) for _ in heads), tuple(jnp.zeros((tq, 1), F32) for _ in heads),
                     jnp.zeros((tq, LANE), F32))
        else:
            prod = o_ref[...].astype(F32) * do2.astype(F32)
            dsum = [jnp.sum(_only_head(sel, hh, prod), axis=1, keepdims=True) for hh in heads]

            def chunk(jc, r0, rowsums, dq):
                off, k2, v2 = kv(jc)
                new_rowsums, dsb, pb = [], [], []
                for hh in heads:
                    z = _dot_nt(_tail(qs[hh], r0), k2)
                    if not fold:
                        z = z * scale
                    if kind == "fox":
                        z = z + _tail(cc_ref[hh], r0) - cr_ref[hh, pl.ds(jc, 1), :]
                    if r0 is not None:
                        z = jnp.where((off + _tail(kio, r0)) <= _tail(qpos, r0), z, NEG)
                    p = jnp.exp(z - _tail(lse_ref[hh], r0))
                    ds = p * (_dot_nt(_tail(dos[hh], r0), v2) - _tail(dsum[hh], r0))
                    dsb.append(ds.astype(BF16))
                    pb.append(p.astype(BF16))
                    if kind == "fox":
                        dc_acc[hh, pl.ds(jc, 1), :] -= jnp.sum(ds, axis=0, keepdims=True)
                        new_rowsums.append(_add_tail(rowsums[hh], jnp.sum(ds, axis=1, keepdims=True), r0))
                    else:
                        new_rowsums.append(rowsums[hh])
                return tuple(new_rowsums), accumulate(off, k2, dsb, pb, dq, r0)

            state = (tuple(jnp.zeros((tq, 1), F32) for _ in heads), jnp.zeros((tq, LANE), F32))

        def trip(t, st):
            for u in range(unroll):
                st = chunk(unroll * t + u, None, *st)
            return st

        state = lax.fori_loop(0, n_full // unroll, trip, state)
        for d in range(n_diag):
            state = chunk(n_full + d, d * tc, *state)
        dq_ref[...] = (state[-1] * scale).astype(dq_ref.dtype)
        if kind == "fox":
            for hh in heads:
                dcc_ref[hh] = state[0][hh]

        @pl.when(i == nq - 1)
        def _():
            dk = dk_acc[...] if fold else dk_acc[...] * scale
            dk_ref[...] = dk.astype(dk_ref.dtype)
            dv_ref[...] = dv_acc[...].astype(dv_ref.dtype)
            if kind == "fox":
                dc_ref[...] = dc_acc[...]

    colspec = pl.BlockSpec((pack, tq, 1), lambda g, i: (g, i, 0))
    rowspec = pl.BlockSpec((pack, Sk // tc, tc), lambda g, i: (g, 0, 0))
    in_specs = [_q_cols(tq, q0), _k_cols(Sk, k0), _k_cols(Sk, v0), _q_cols(tq, 0), _q_cols(tq, 0)]
    args = [qa, ka, va, o, do]
    if kind == "fox":
        in_specs += [colspec, rowspec]
        args += [ccol, crow]
    in_specs += [colspec]
    args += [lse]
    width = n_heads * dh
    out_specs = [_q_cols(tq, 0), _k_cols(Sk, 0), _k_cols(Sk, 0)]
    out_shape = [jax.ShapeDtypeStruct((S, width), BF16), jax.ShapeDtypeStruct((Sk, width), BF16),
                 jax.ShapeDtypeStruct((Sk, width), BF16)]
    scratch = [pltpu.VMEM((Sk, LANE), F32), pltpu.VMEM((Sk, LANE), F32)]
    if kind == "fox":
        out_specs += [rowspec, colspec]
        out_shape += [jax.ShapeDtypeStruct((n_heads, Sk // tc, tc), F32), jax.ShapeDtypeStruct((n_heads, S, 1), F32)]
        scratch.append(pltpu.VMEM((pack, Sk // tc, tc), F32))
    return pl.pallas_call(
        body, name="attn_bwd_" + kind, grid=(n_heads // pack, nq),
        in_specs=in_specs, out_specs=out_specs, out_shape=out_shape, scratch_shapes=scratch,
        compiler_params=_params(),
    )(*args)


def _mem_norm(mem, g):
    M, D = mem.shape

    def body(mem_ref, g_ref, out_ref):
        out_ref[...] = _rms(mem_ref[...], g_ref[...]).astype(BF16)

    return pl.pallas_call(
        body, name="mem_norm", grid=(1,),
        in_specs=[_whole((M, D)), _whole((1, D))], out_specs=_whole((M, D)),
        out_shape=jax.ShapeDtypeStruct((M, D), BF16), compiler_params=_params(),
    )(mem, g)


def _mem_norm_bwd(mem, g, dmem_n):
    M, D = mem.shape
    L = dmem_n.shape[0]

    def body(mem_ref, g_ref, d_ref, dg_ref):
        d = d_ref[0]
        for l in range(1, L):
            d = d + d_ref[l]
        _, vjp = jax.vjp(_rms, mem_ref[...], g_ref[...])
        dg_ref[...] = vjp(d)[1]

    return pl.pallas_call(
        body, name="mem_norm_bwd", grid=(1,),
        in_specs=[_whole((M, D)), _whole((1, D)), _whole((L, M, D))], out_specs=_whole((1, D)),
        out_shape=jax.ShapeDtypeStruct((1, D), F32), compiler_params=_params(),
    )(mem, g, dmem_n)


def _loss_head(h, target):
    S, D = h.shape
    bm = _pick(S, 512)

    def body(h_ref, t_ref, dh_ref, loss_ref):
        err = h_ref[...] - t_ref[...]
        dh_ref[...] = err * (1.0 / D)

        @pl.when(pl.program_id(0) == 0)
        def _():
            loss_ref[...] = jnp.zeros_like(loss_ref)

        loss_ref[...] += 0.5 * jnp.sum(jnp.mean(err * err, axis=-1, keepdims=True), axis=0, keepdims=True)

    return pl.pallas_call(
        body, name="loss_head", grid=(S // bm,),
        in_specs=[_rows(bm, D), _rows(bm, D)], out_specs=[_rows(bm, D), _whole((8, LANE))],
        out_shape=[jax.ShapeDtypeStruct((S, D), F32), jax.ShapeDtypeStruct((8, LANE), F32)],
        compiler_params=_params(),
    )(h, target)


def _adamw(w, g, m, v, name):
    R, C = w.shape
    rb = R if R * C * 4 <= (1 << 20) else _pick(R, 256)
    if R % rb:
        rb = R
    c1 = 1.0 - ADAM_B1 ** ADAM_STEP
    c2 = 1.0 - ADAM_B2 ** ADAM_STEP

    def body(w_ref, g_ref, m_ref, v_ref, d_ref, mo_ref, vo_ref):
        gv = g_ref[...]
        mn = ADAM_B1 * m_ref[...] + (1.0 - ADAM_B1) * gv
        vn = ADAM_B2 * v_ref[...] + (1.0 - ADAM_B2) * (gv * gv)
        mo_ref[...] = mn
        vo_ref[...] = vn
        d_ref[...] = -ADAM_LR * ((mn / c1) / (jnp.sqrt(vn / c2) + ADAM_EPS) + ADAM_WD * w_ref[...])

    return pl.pallas_call(
        body, name=name, grid=(R // rb,),
        in_specs=[_rows(rb, C)] * 4, out_specs=[_rows(rb, C)] * 3,
        out_shape=[jax.ShapeDtypeStruct((R, C), F32)] * 3, compiler_params=_params(),
    )(w, g, m, v)


def _adamw_reduced(w, m, v, mine, theirs, c_idx, first_row, name):
    L, a, b = w.shape
    Lh = L // 2
    rb = _shard_row_block(a)
    nb = a // rb
    assert first_row % rb == 0
    c1 = 1.0 - ADAM_B1 ** ADAM_STEP
    c2 = 1.0 - ADAM_B2 ** ADAM_STEP

    def own(i, c_ref):
        return (i, 0)

    def reduced(i, c_ref):
        return (first_row // rb + ((i // nb) % Lh) * nb + i % nb, 0)

    def body(c_ref, w_ref, m_ref, v_ref, mine_ref, theirs_ref, g_ref, d_ref, mo_ref, vo_ref):
        half = (pl.program_id(0) // nb) // Lh
        gv = jnp.where(c_ref[0] == half, mine_ref[...], theirs_ref[...])
        g_ref[...] = gv
        mn = ADAM_B1 * m_ref[...] + (1.0 - ADAM_B1) * gv
        vn = ADAM_B2 * v_ref[...] + (1.0 - ADAM_B2) * (gv * gv)
        mo_ref[...] = mn
        vo_ref[...] = vn
        d_ref[...] = -ADAM_LR * ((mn / c1) / (jnp.sqrt(vn / c2) + ADAM_EPS) + ADAM_WD * w_ref[...])

    outs = pl.pallas_call(
        body, name=name,
        grid_spec=pltpu.PrefetchScalarGridSpec(
            num_scalar_prefetch=1, grid=(L * nb,),
            in_specs=[pl.BlockSpec((rb, b), own)] * 3 + [pl.BlockSpec((rb, b), reduced)] * 2,
            out_specs=[pl.BlockSpec((rb, b), own)] * 4),
        out_shape=[jax.ShapeDtypeStruct((L * a, b), F32)] * 4, compiler_params=_params(),
    )(c_idx, w.reshape(L * a, b), m.reshape(L * a, b), v.reshape(L * a, b), mine, theirs)
    return [t.reshape(L, a, b) for t in outs]


ANY = pl.BlockSpec(memory_space=pl.ANY)
MESH = pl.DeviceIdType.MESH


def _place():
    x, y, c = lax.axis_index("x"), lax.axis_index("y"), lax.axis_index("c")
    others = [(1 - x, y), (x, 1 - y), (1 - x, 1 - y)]
    return x, y, c, others


def _place_own(loc, chip_idx):
    _, R, C = loc.shape
    rb = _pick(R, 2 * FLAT_ROW_BLOCK)

    def body(chip_ref, loc_ref, out_ref):
        out_ref[...] = loc_ref[...]

    return pl.pallas_call(
        body, name="place_own",
        grid_spec=pltpu.PrefetchScalarGridSpec(
            num_scalar_prefetch=1, grid=(2, R // rb),
            in_specs=[pl.BlockSpec((None, rb, C), lambda hf, i, chip_ref: (hf, i, 0))],
            out_specs=pl.BlockSpec((None, None, rb, C), lambda hf, i, chip_ref: (chip_ref[0], hf, i, 0))),
        out_shape=jax.ShapeDtypeStruct((N_CHIPS, 2, R, C), loc.dtype), compiler_params=_params(),
    )(chip_idx, loc)


def _gather_weights(locs, owns):
    n = len(locs)

    def body(*refs):
        loc_refs, out_refs, (send_sems, recv_sems) = refs[:n], refs[2 * n:3 * n], refs[3 * n:]
        x, y, c, others = _place()
        me = 2 * x + y
        sibling = (x, y, 1 - c)

        def copy(a, k, src, dst, to):
            return pltpu.make_async_remote_copy(src_ref=src, dst_ref=dst, send_sem=send_sems.at[a, k],
                                                recv_sem=recv_sems.at[a, k], device_id=to, device_id_type=MESH)

        first = [copy(a, j, loc_refs[a].at[c], out_refs[a].at[me, c], (ox, oy, c))
                 for j, (ox, oy) in enumerate(others) for a in range(n)]
        for cp in first:
            cp.start()
        passed = []
        for j, (ox, oy) in enumerate(others):
            for a in range(n):
                landed = out_refs[a].at[2 * ox + oy, c]
                copy(a, j, loc_refs[a].at[c], landed, sibling).wait_recv()
                cp = copy(a, 3 + j, landed, landed, sibling)
                cp.start()
                passed.append(cp)
        for j, (ox, oy) in enumerate(others):
            for a in range(n):
                copy(a, 3 + j, loc_refs[a].at[c], out_refs[a].at[2 * ox + oy, 1 - c], sibling).wait_recv()
        for cp in first + passed:
            cp.wait_send()

    return pl.pallas_call(
        body, name="gather_weights", in_specs=[ANY] * (2 * n), out_specs=[ANY] * n,
        out_shape=[jax.ShapeDtypeStruct(own.shape, own.dtype) for own in owns],
        input_output_aliases={n + a: a for a in range(n)},
        scratch_shapes=[pltpu.SemaphoreType.DMA((n, 6)), pltpu.SemaphoreType.DMA((n, 6))],
    )(*locs, *owns)


def _pair_exchange(gs):
    n = len(gs)

    def body(*refs):
        g_refs, out_refs, (send_sems, recv_sems) = refs[:n], refs[n:2 * n], refs[2 * n:]
        x, y, c, _ = _place()
        copies = [pltpu.make_async_remote_copy(src_ref=g_refs[a].at[1 - c], dst_ref=out_refs[a],
                                               send_sem=send_sems.at[a], recv_sem=recv_sems.at[a],
                                               device_id=(x, y, 1 - c), device_id_type=MESH) for a in range(n)]
        for cp in copies:
            cp.start()
        for cp in copies:
            cp.wait()

    return pl.pallas_call(
        body, name="pair_exchange", in_specs=[ANY] * n, out_specs=[ANY] * n,
        out_shape=[jax.ShapeDtypeStruct(g.shape[1:], g.dtype) for g in gs],
        scratch_shapes=[pltpu.SemaphoreType.DMA((n,)), pltpu.SemaphoreType.DMA((n,))],
    )(*gs)


def _pair_sum(g, sib, c_idx):
    _, _, R, C = g.shape
    rb = _pick(R, 512)

    def body(c_ref, g_ref, s_ref, o_ref):
        o_ref[...] = (g_ref[...].astype(F32) + s_ref[...].astype(F32)).astype(o_ref.dtype)

    return pl.pallas_call(
        body, name="pair_sum",
        grid_spec=pltpu.PrefetchScalarGridSpec(
            num_scalar_prefetch=1, grid=(N_CHIPS, R // rb),
            in_specs=[pl.BlockSpec((None, None, rb, C), lambda j, i, c_ref: (c_ref[0], j, i, 0)),
                      pl.BlockSpec((None, rb, C), lambda j, i, c_ref: (j, i, 0))],
            out_specs=pl.BlockSpec((None, rb, C), lambda j, i, c_ref: (j, i, 0))),
        out_shape=jax.ShapeDtypeStruct((N_CHIPS, R, C), g.dtype), compiler_params=_params(),
    )(c_idx, g, sib)


def _chip_exchange(ps):
    n = len(ps)

    def body(*refs):
        p_refs, out_refs, (send_sems, recv_sems) = refs[:n], refs[n:2 * n], refs[2 * n:]
        x, y, c, others = _place()
        copies = []
        for j, (ox, oy) in enumerate(others):
            for a in range(n):
                cp = pltpu.make_async_remote_copy(src_ref=p_refs[a].at[2 * ox + oy], dst_ref=out_refs[a].at[j],
                                                  send_sem=send_sems.at[a, j], recv_sem=recv_sems.at[a, j],
                                                  device_id=(ox, oy, c), device_id_type=MESH)
                cp.start()
                copies.append(cp)
        for cp in copies:
            cp.wait()

    return pl.pallas_call(
        body, name="chip_exchange", in_specs=[ANY] * n, out_specs=[ANY] * n,
        out_shape=[jax.ShapeDtypeStruct((N_CHIPS - 1,) + p.shape[1:], p.dtype) for p in ps],
        scratch_shapes=[pltpu.SemaphoreType.DMA((n, 3)), pltpu.SemaphoreType.DMA((n, 3))],
    )(*ps)


def _chip_sum(p, r, chip_idx):
    _, R, C = r.shape
    rb = _pick(R, 512)

    def body(chip_ref, p_ref, r_ref, o_ref):
        acc = p_ref[...].astype(F32)
        for j in range(N_CHIPS - 1):
            acc = acc + r_ref[j].astype(F32)
        o_ref[...] = acc

    return pl.pallas_call(
        body, name="chip_sum",
        grid_spec=pltpu.PrefetchScalarGridSpec(
            num_scalar_prefetch=1, grid=(R // rb,),
            in_specs=[pl.BlockSpec((None, rb, C), lambda i, chip_ref: (chip_ref[0], i, 0)),
                      pl.BlockSpec((N_CHIPS - 1, rb, C), lambda i, chip_ref: (0, i, 0))],
            out_specs=pl.BlockSpec((rb, C), lambda i, chip_ref: (i, 0))),
        out_shape=jax.ShapeDtypeStruct((R, C), F32), compiler_params=_params(),
    )(chip_idx, p, r)


def _pair_swap(rhs):
    n = len(rhs)

    def body(*refs):
        rh_refs, out_refs, (send_sems, recv_sems) = refs[:n], refs[n:2 * n], refs[2 * n:]
        x, y, c, _ = _place()
        copies = [pltpu.make_async_remote_copy(src_ref=rh_refs[a], dst_ref=out_refs[a], send_sem=send_sems.at[a],
                                               recv_sem=recv_sems.at[a], device_id=(x, y, 1 - c),
                                               device_id_type=MESH) for a in range(n)]
        for cp in copies:
            cp.start()
        for cp in copies:
            cp.wait()

    return pl.pallas_call(
        body, name="pair_swap", in_specs=[ANY] * n, out_specs=[ANY] * n,
        out_shape=[jax.ShapeDtypeStruct(rh.shape, rh.dtype) for rh in rhs],
        scratch_shapes=[pltpu.SemaphoreType.DMA((n,)), pltpu.SemaphoreType.DMA((n,))],
    )(*rhs)


def _all_reduce_small(s):
    R, C = s.shape

    def body(s_ref, o_ref, buf, send_sems, recv_sems):
        x, y, c, _ = _place()
        me = 4 * x + 2 * y + c
        sends = []
        for k in range(1, N_DEV):
            fx, fy, fc = (k >> 2) & 1, (k >> 1) & 1, k & 1
            to = (x ^ fx, y ^ fy, c ^ fc)
            cp = pltpu.make_async_remote_copy(src_ref=s_ref, dst_ref=buf.at[me], send_sem=send_sems.at[k - 1],
                                              recv_sem=recv_sems.at[k - 1], device_id=to, device_id_type=MESH)
            cp.start()
            sends.append(cp)
        buf[me] = s_ref[...]
        for k in range(1, N_DEV):
            fx, fy, fc = (k >> 2) & 1, (k >> 1) & 1, k & 1
            frm = 4 * (x ^ fx) + 2 * (y ^ fy) + (c ^ fc)
            pltpu.make_async_remote_copy(src_ref=s_ref, dst_ref=buf.at[frm], send_sem=send_sems.at[k - 1],
                                         recv_sem=recv_sems.at[k - 1], device_id=(x, y, c),
                                         device_id_type=MESH).wait_recv()
        acc = buf[0]
        for d in range(1, N_DEV):
            acc = acc + buf[d]
        o_ref[...] = acc
        for cp in sends:
            cp.wait_send()

    vm = pl.BlockSpec(memory_space=pltpu.VMEM)
    return pl.pallas_call(
        body, name="all_reduce_small", in_specs=[vm], out_specs=vm,
        out_shape=jax.ShapeDtypeStruct((R, C), F32),
        scratch_shapes=[pltpu.VMEM((N_DEV, R, C), F32), pltpu.SemaphoreType.DMA((N_DEV - 1,)),
                        pltpu.SemaphoreType.DMA((N_DEV - 1,))],
    )(s)


def _padded(n):
    return -(-n // FLAT_UNIT) * FLAT_UNIT


def _pack_flat(pieces, dtype, row_block=FLAT_ROW_BLOCK):
    flat = []
    for p in pieces:
        p = p.reshape(-1).astype(dtype)
        flat.append(jnp.pad(p, (0, _padded(p.size) - p.size)))
    total = sum(p.size for p in flat)
    flat.append(jnp.zeros((-total) % (row_block * FLAT_COLS), dtype))
    return jnp.concatenate(flat).reshape(-1, FLAT_COLS)


def _unpack_flat(flat, shapes):
    lead = flat.shape[:-2]
    flat = flat.reshape(lead + (-1,))
    out, off = [], 0
    for shp in shapes:
        n = math.prod(shp)
        out.append(flat[..., off:off + n].reshape(lead + tuple(shp)))
        off += _padded(n)
    return out


def _shard_row_block(a):
    for rb in range(min(a, 512) // 16 * 16, 0, -16):
        if a % rb == 0:
            return rb
    return a


def _row_layout(shapes, n_layers):
    groups = {}
    for name, (a, b) in shapes.items():
        names, first, rows = groups.get(b, ((), {}, 0))
        rb = _shard_row_block(a)
        start = -(-rows // rb) * rb
        groups[b] = (names + (name,), {**first, name: start}, start + n_layers * a)
    return {b: (names, first, -(-rows // FLAT_ROW_BLOCK) * FLAT_ROW_BLOCK) for b, (names, first, rows) in groups.items()}


def _pack_rows(group, width, pieces, dtype):
    names, first, rows = group
    parts, at = [], 0
    for name in names:
        if first[name] > at:
            parts.append(jnp.zeros((first[name] - at, width), dtype))
        parts.append(pieces[name].astype(dtype))
        at = first[name] + pieces[name].shape[0]
    if rows > at:
        parts.append(jnp.zeros((rows - at, width), dtype))
    return jnp.concatenate(parts, axis=0)


def _slab(t, axis, j):
    if t.ndim == 3:
        return t[j]
    n = t.shape[axis - 1] // N_CHIPS
    return lax.slice_in_dim(t, j * n, (j + 1) * n, axis=axis - 1)


def _layer_fwd(h0, mem_n, wl, dims):
    n_sb, n_fx, n_mem, sbw, fxw, memw = dims
    n1, gate1, up1, a1 = _ffn_fwd_up(h0, wl["ffn1_pre_g"], wl["ffn1_w_gate"], wl["ffn1_w_up"])
    h1, f1 = _ffn_fwd_down(a1, wl["ffn1_w_down"], h0, wl["ffn1_post_g"])

    u, proj, fl, sg = _mix_fwd_in(h1, wl["mix_pre_g"], wl["w_in"], wl["w_gate"], wl["b_gate"], wl["b_forget"])
    c = _fox_cumsum(fl)
    S = h0.shape[0]
    tc = _attn_blocks("fox", S, S)[1]
    ct = c[:, :n_fx].T
    ccol, crow = ct.reshape(n_fx, S, 1), ct.reshape(n_fx, S // tc, tc)
    qkv_sb = [(proj, k * sbw) for k in range(3)]
    qkv_fx = [(proj, 3 * sbw + k * fxw) for k in range(3)]
    kv = _matmul(mem_n, wl["w_mem_kv"], out_dtype=BF16, name="mem_kv")
    qkv_mem = [(proj, 3 * sbw + 3 * fxw), (kv, 0), (kv, memw)]
    o_sb, tot_sb = _attn_fwd("sb", *qkv_sb, n_sb, HEAD_DIM)
    o_fx, lse_fx = _attn_fwd("fox", *qkv_fx, n_fx, HEAD_DIM, ccol, crow)
    o_mem, lse_mem = _attn_fwd("mem", *qkv_mem, n_mem, MEM_HEAD_DIM)
    h2, zmix, merged = _mix_fwd_out(o_sb, o_fx, o_mem, sg, wl["w_br_sb"], wl["w_br_fox"], wl["w_br_mem"],
                                    wl["w_out"], h1, wl["mix_post_g"])

    n2, gate2, up2, a2 = _ffn_fwd_up(h2, wl["ffn2_pre_g"], wl["ffn2_w_gate"], wl["ffn2_w_up"])
    h3, f2 = _ffn_fwd_down(a2, wl["ffn2_w_down"], h2, wl["ffn2_post_g"])
    saved = dict(h0=h0, n1=n1, gate1=gate1, up1=up1, a1=a1, f1=f1, h1=h1, u=u, fl=fl, sg=sg,
                 qkv_sb=qkv_sb, qkv_fx=qkv_fx, qkv_mem=qkv_mem, ccol=ccol, crow=crow, o_sb=o_sb, o_fx=o_fx, o_mem=o_mem,
                 tot_sb=tot_sb, lse_fx=lse_fx, lse_mem=lse_mem,
                 zmix=zmix, merged=merged, h2=h2, n2=n2, gate2=gate2, up2=up2, a2=a2, f2=f2)
    return h3, saved


def _ffn_bwd(dh, sv, wl, tag, h_in):
    n, gate, up, a, f = (sv[k + tag] for k in ("n", "gate", "up", "a", "f"))
    pre = "ffn" + tag
    df, dgate, dup, dg_post = _ffn_bwd_down(dh, f, wl[pre + "_post_g"], wl[pre + "_w_down"], gate, up)
    dh_in, dg_pre = _ffn_bwd_up(dgate, dup, wl[pre + "_w_gate"], wl[pre + "_w_up"], h_in, wl[pre + "_pre_g"], dh)
    grads = {pre + "_post_g": dg_post, pre + "_pre_g": dg_pre,
             pre + "_w_down": _matmul(a, df, ta=True, batch="a", name="dw_down"),
             pre + "_w_gate": _matmul(n, dgate, ta=True, batch="b", name="dw_gate"),
             pre + "_w_up": _matmul(n, dup, ta=True, batch="b", name="dw_up")}
    return dh_in, grads


def _layer_bwd(dh3, mem_n, wl, sv, dims):
    n_sb, n_fx, n_mem, sbw, fxw, memw = dims
    S = dh3.shape[0]
    dh2, grads = _ffn_bwd(dh3, sv, wl, "2", sv["h2"])

    (dz, db_sb, db_fx, db_mem, do_sb, do_fx, do_mem, dgp, db_gate, dg_post) = _mix_bwd_out(
        dh2, sv["zmix"], wl["mix_post_g"], wl["w_out"], sv["o_sb"], sv["o_fx"], sv["o_mem"],
        wl["w_br_sb"], wl["w_br_fox"], wl["w_br_mem"], sv["sg"])
    grads["mix_post_g"] = dg_post
    grads["b_gate"] = db_gate
    grads["w_out"] = _matmul(sv["merged"], dz, ta=True, name="dw_out")
    grads["w_br_sb"] = _matmul(sv["o_sb"], db_sb, ta=True, name="dw_br_sb")
    grads["w_br_fox"] = _matmul(sv["o_fx"], db_fx, ta=True, name="dw_br_fox")
    grads["w_br_mem"] = _matmul(sv["o_mem"], db_mem, ta=True, name="dw_br_mem")

    dq_sb, dk_sb, dv_sb = _attn_bwd("sb", *sv["qkv_sb"], sv["o_sb"], do_sb, n_sb, HEAD_DIM, lse=sv["tot_sb"])
    dq_fx, dk_fx, dv_fx, dcrow, dccol = _attn_bwd("fox", *sv["qkv_fx"], sv["o_fx"], do_fx, n_fx, HEAD_DIM,
                                                  sv["ccol"], sv["crow"], sv["lse_fx"])
    dq_mem, dk_mem, dv_mem = _attn_bwd("mem", *sv["qkv_mem"], sv["o_mem"], do_mem, n_mem, MEM_HEAD_DIM,
                                       lse=sv["lse_mem"])
    dkv = jnp.concatenate([dk_mem, dv_mem], axis=1)
    grads["w_mem_kv"] = _matmul(mem_n, dkv, ta=True, name="dw_mem_kv")
    dmem_n = _matmul(dkv, wl["w_mem_kv"], tb=True, out_dtype=F32, name="dmem_n")

    dc = jnp.pad((dcrow.reshape(n_fx, S) + dccol.reshape(n_fx, S)).T, ((0, 0), (0, LANE - n_fx)))
    dfl, db_forget = _fox_dlogit(dc, sv["fl"])
    grads["b_forget"] = db_forget
    dproj = jnp.concatenate([dq_sb, dk_sb, dv_sb, dq_fx, dk_fx, dv_fx, dq_mem, dfl], axis=1)
    dh1, dg_pre = _mix_bwd_in(dproj, dgp, wl["w_in"], wl["w_gate"], sv["h1"], wl["mix_pre_g"], dh2)
    grads["mix_pre_g"] = dg_pre
    grads["w_in"] = _matmul(sv["u"], dproj, ta=True, name="dw_in")
    grads["w_gate"] = _matmul(sv["u"], dgp, ta=True, name="dw_gate_mix")

    dh0, g1 = _ffn_bwd(dh1, sv, wl, "1", sv["h0"])
    grads.update(g1)
    return dh0, grads, dmem_n


def kernel(x, mem, ffn1_pre_g, ffn1_post_g, ffn1_w_gate, ffn1_w_up, ffn1_w_down, mix_pre_g, mix_post_g, w_in, b_forget, mem_norm_g, w_mem_kv, w_gate, b_gate, w_br_sb, w_br_fox, w_br_mem, w_out, ffn2_pre_g, ffn2_post_g, ffn2_w_gate, ffn2_w_up, ffn2_w_down, loss_target, m_ffn1_pre_g, m_ffn1_post_g, m_ffn1_w_gate, m_ffn1_w_up, m_ffn1_w_down, m_mix_pre_g, m_mix_post_g, m_w_in, m_b_forget, m_mem_norm_g, m_w_mem_kv, m_w_gate, m_b_gate, m_w_br_sb, m_w_br_fox, m_w_br_mem, m_w_out, m_ffn2_pre_g, m_ffn2_post_g, m_ffn2_w_gate, m_ffn2_w_up, m_ffn2_w_down, v_ffn1_pre_g, v_ffn1_post_g, v_ffn1_w_gate, v_ffn1_w_up, v_ffn1_w_down, v_mix_pre_g, v_mix_post_g, v_w_in, v_b_forget, v_mem_norm_g, v_w_mem_kv, v_w_gate, v_b_gate, v_w_br_sb, v_w_br_fox, v_w_br_mem, v_w_out, v_ffn2_pre_g, v_ffn2_post_g, v_ffn2_w_gate, v_ffn2_w_up, v_ffn2_w_down):
    args = dict(locals())
    w = {n: args[n] for n in WEIGHTS}
    m = {n: args["m_" + n] for n in WEIGHTS}
    v = {n: args["v_" + n] for n in WEIGHTS}
    L = w["ffn1_pre_g"].shape[0]
    Lh = L // 2
    D = x.shape[2]
    sbw, fxw, memw = w["w_br_sb"].shape[1], w["w_br_fox"].shape[1], w["w_br_mem"].shape[1]
    n_sb, n_fx, n_mem = sbw // HEAD_DIM, fxw // HEAD_DIM, memw // MEM_HEAD_DIM
    dims = (n_sb, n_fx, n_mem, sbw, fxw, memw)
    qkv_w = 3 * sbw + 3 * fxw
    c_idx = lax.axis_index("c")
    c_arr = c_idx.reshape(1).astype(jnp.int32)
    chip_arr = (2 * lax.axis_index("x") + lax.axis_index("y")).reshape(1).astype(jnp.int32)

    shard_shapes = {n: w[n].shape[1:] for n, _ in BIG}
    layout = _row_layout(shard_shapes, Lh)
    widths = list(layout)
    locs = [jnp.stack([_pack_rows(layout[b], b, {n: w[n][hf * Lh:(hf + 1) * Lh].reshape(-1, b) for n in layout[b][0]},
                                  BF16) for hf in range(2)]) for b in widths]
    gathered = dict(zip(widths, _gather_weights(locs, [_place_own(loc, chip_arr) for loc in locs])))

    def layer_weights(l):
        hf, li = divmod(l, Lh)
        wl = {}
        for n, axis in BIG:
            a, b = shard_shapes[n]
            r0 = layout[b][1][n] + li * a
            shards = gathered[b][:, hf, r0:r0 + a]
            if n.startswith("ffn"):
                wl[n] = shards
            else:
                wl[n] = (shards.transpose(1, 0, 2).reshape(a, N_CHIPS * b) if axis == 2 else
                         shards.reshape(N_CHIPS * a, b))
        wi = wl["w_in"]
        wl["w_in"] = jnp.concatenate([wi[:, :qkv_w], wi[:, qkv_w + n_fx:], wi[:, qkv_w:qkv_w + n_fx],
                                      jnp.zeros((D, LANE - n_fx), BF16)], axis=1)
        for n in SMALL:
            if n != "mem_norm_g":
                wl[n] = w[n][l][None, :]
        wl["b_forget"] = jnp.pad(wl["b_forget"], ((0, 0), (0, LANE - n_fx)))
        return wl

    g_mem = w["mem_norm_g"][None, :]

    mem_n = _mem_norm(mem[0], g_mem)
    h, wls, saved = x[0], [], []
    for l in range(L):
        wls.append(layer_weights(l))
        h, sv = _layer_fwd(h, mem_n, wls[l], dims)
        saved.append(sv)
    dh, loss_tile = _loss_head(h, loss_target[0])
    loss = lax.psum(loss_tile[0, 0], ("x", "y", "c"))
    gl, dmem_n = [None] * L, [None] * L
    for l in reversed(range(L)):
        dh, gl[l], dmem_n[l] = _layer_bwd(dh, mem_n, wls[l], saved[l], dims)
        gi = gl[l]["w_in"]
        gl[l]["w_in"] = jnp.concatenate([gi[:, :qkv_w], gi[:, qkv_w + memw:qkv_w + memw + n_fx],
                                         gi[:, qkv_w:qkv_w + memw]], axis=1)
    grad_x = dh
    g_mem_norm = _mem_norm_bwd(mem[0], g_mem, jnp.stack(dmem_n))

    axis_of = dict(BIG)
    partials = [jnp.stack([jnp.stack([
        _pack_rows(layout[b], b, {n: jnp.concatenate([_slab(gl[hf * Lh + li][n], axis_of[n], j) for li in range(Lh)])
                                  for n in layout[b][0]}, BF16)
        for j in range(N_CHIPS)]) for hf in range(2)]) for b in widths]
    pairs = [_pair_sum(g, sib, c_arr) for g, sib in zip(partials, _pair_exchange(partials))]
    mines = [_chip_sum(p, r, chip_arr) for p, r in zip(pairs, _chip_exchange(pairs))]
    theirs = _pair_swap(mines)

    grad, delta, new_m, new_v = {}, {}, {}, {}
    for n, _ in BIG:
        k = widths.index(shard_shapes[n][1])
        grad[n], delta[n], new_m[n], new_v[n] = _adamw_reduced(
            w[n], m[n], v[n], mines[k], theirs[k], c_arr, layout[widths[k]][1][n], name="adamw_" + n)

    small_local = {n: (g_mem_norm if n == "mem_norm_g" else
                       jnp.concatenate([gl[l][n][:, :n_fx] if n == "b_forget" else gl[l][n] for l in range(L)]))
                   for n in SMALL}
    small_shapes = [small_local[n].shape for n in SMALL]
    small_sum = _unpack_flat(_all_reduce_small(_pack_flat([small_local[n] for n in SMALL], F32, row_block=16)),
                             small_shapes)
    for n, t in zip(SMALL, small_sum):
        shp = w[n].shape
        two_d = (1, shp[0]) if len(shp) == 1 else shp
        grad[n] = t.reshape(shp)
        d_, m_, v_ = _adamw(w[n].reshape(two_d), t.reshape(two_d), m[n].reshape(two_d), v[n].reshape(two_d),
                            name="adamw_" + n)
        delta[n], new_m[n], new_v[n] = d_.reshape(shp), m_.reshape(shp), v_.reshape(shp)

    return (loss, grad_x[None], *[grad[n] for n in WEIGHTS], *[delta[n] for n in WEIGHTS],
            *[new_m[n] for n in WEIGHTS], *[new_v[n] for n in WEIGHTS])
```

```python
import math

import jax
import jax.numpy as jnp
from jax import lax
from jax.experimental import pallas as pl
from jax.experimental.pallas import tpu as pltpu

F32 = jnp.float32
BF16 = jnp.bfloat16
RMS_EPS = 1e-6
HEAD_DIM = 64
MEM_HEAD_DIM = 128
LANE = 128
V7X_VMEM_BYTES = 64 * 1024 * 1024
V7X_VMEM_LIMIT_BYTES = V7X_VMEM_BYTES - 3 * 1024 * 1024
FLAT_COLS = 512
FLAT_UNIT = 16 * FLAT_COLS
FLAT_ROW_BLOCK = 512
N_CHIPS = 4
N_DEV = 8
NEG = float(jnp.finfo(jnp.float32).min)

ADAM_LR = 0.001
ADAM_B1 = 0.9
ADAM_B2 = 0.999
ADAM_EPS = 1e-08
ADAM_WD = 0.01
ADAM_STEP = 10

BIG = (("ffn1_w_gate", 2), ("ffn1_w_up", 2), ("ffn1_w_down", 1), ("w_in", 2), ("w_mem_kv", 1), ("w_gate", 2),
       ("w_br_sb", 2), ("w_br_fox", 2), ("w_br_mem", 2), ("w_out", 1),
       ("ffn2_w_gate", 2), ("ffn2_w_up", 2), ("ffn2_w_down", 1))
SMALL = ("ffn1_pre_g", "ffn1_post_g", "mix_pre_g", "mix_post_g", "b_forget", "mem_norm_g", "b_gate",
         "ffn2_pre_g", "ffn2_post_g")
WEIGHTS = ("ffn1_pre_g", "ffn1_post_g", "ffn1_w_gate", "ffn1_w_up", "ffn1_w_down", "mix_pre_g", "mix_post_g", "w_in",
           "b_forget", "mem_norm_g", "w_mem_kv", "w_gate", "b_gate", "w_br_sb", "w_br_fox", "w_br_mem", "w_out",
           "ffn2_pre_g", "ffn2_post_g", "ffn2_w_gate", "ffn2_w_up", "ffn2_w_down")


def _params(**kw):
    return pltpu.CompilerParams(vmem_limit_bytes=V7X_VMEM_LIMIT_BYTES, **kw)


def _dot(a, b):
    return jnp.dot(a, b, preferred_element_type=F32)


def _dot_nt(a, b):
    return lax.dot_general(a, b, (((1,), (1,)), ((), ())), preferred_element_type=F32)


def _dot_tn(a, b):
    return lax.dot_general(a, b, (((0,), (0,)), ((), ())), preferred_element_type=F32)


def _rms(t, g):
    return t * lax.rsqrt(jnp.mean(t * t, axis=-1, keepdims=True) + RMS_EPS) * g


def _pick(dim, pref):
    if dim <= pref:
        return dim
    for cand in range(pref - pref % LANE, 0, -LANE):
        if dim % cand == 0:
            return cand
    return dim


def _rows(bm, cols):
    return pl.BlockSpec((bm, cols), lambda i: (i, 0))


def _whole(shape):
    nd = len(shape)
    return pl.BlockSpec(shape, lambda i: (0,) * nd)


def _split3(x):
    hi = x.astype(BF16)
    r1 = x - hi.astype(F32)
    mid = r1.astype(BF16)
    lo = (r1 - mid.astype(F32)).astype(BF16)
    return hi, mid, lo


def _cumdot(x, tri):
    hi = x.astype(BF16)
    lo = (x - hi.astype(F32)).astype(BF16)
    return _dot(hi, tri) + _dot(lo, tri)


def _slabs(bm, cols):
    return pl.BlockSpec((N_CHIPS, bm, cols), lambda i: (0, i, 0))


def _ffn_fwd_up(h, g_pre, wg, wu):
    S, D = h.shape
    Fs = wg.shape[2]
    bm = _pick(S, 256)

    def body(h_ref, g_ref, wg_ref, wu_ref, n_ref, gate_ref, up_ref, a_ref):
        n = _rms(h_ref[...], g_ref[...]).astype(BF16)
        n_ref[...] = n
        for j in range(N_CHIPS):
            gate = _dot(n, wg_ref[j])
            up = _dot(n, wu_ref[j])
            gate_ref[j] = gate.astype(BF16)
            up_ref[j] = up.astype(BF16)
            a_ref[j] = (gate * jax.nn.sigmoid(gate) * up).astype(BF16)

    return pl.pallas_call(
        body, name="ffn_fwd_up", grid=(S // bm,),
        in_specs=[_rows(bm, D), _whole((1, D)), _whole((N_CHIPS, D, Fs)), _whole((N_CHIPS, D, Fs))],
        out_specs=[_rows(bm, D), _slabs(bm, Fs), _slabs(bm, Fs), _slabs(bm, Fs)],
        out_shape=[jax.ShapeDtypeStruct((S, D), BF16)] + [jax.ShapeDtypeStruct((N_CHIPS, S, Fs), BF16)] * 3,
        compiler_params=_params(),
    )(h, g_pre, wg, wu)


def _ffn_fwd_down(a, wd, h, g_post):
    _, S, Fs = a.shape
    D = wd.shape[2]
    bm = _pick(S, 256)

    def body(a_ref, wd_ref, h_ref, g_ref, hout_ref, f_ref):
        f = _dot(a_ref[0], wd_ref[0])
        for j in range(1, N_CHIPS):
            f = f + _dot(a_ref[j], wd_ref[j])
        f_ref[...] = f
        hout_ref[...] = h_ref[...] + 0.5 * _rms(f, g_ref[...])

    return pl.pallas_call(
        body, name="ffn_fwd_down", grid=(S // bm,),
        in_specs=[_slabs(bm, Fs), _whole((N_CHIPS, Fs, D)), _rows(bm, D), _whole((1, D))],
        out_specs=[_rows(bm, D), _rows(bm, D)],
        out_shape=[jax.ShapeDtypeStruct((S, D), F32)] * 2,
        compiler_params=_params(),
    )(a, wd, h, g_post)


def _ffn_bwd_down(dh, f, g_post, wd, gate, up):
    S, D = dh.shape
    Fs = wd.shape[1]
    bm = _pick(S, 256)

    def body(dh_ref, f_ref, g_ref, wd_ref, gate_ref, up_ref, df_ref, dgate_ref, dup_ref, dg_ref):
        _, vjp = jax.vjp(lambda t, g: 0.5 * _rms(t, g), f_ref[...], g_ref[...])
        df, dg = vjp(dh_ref[...])

        @pl.when(pl.program_id(0) == 0)
        def _():
            dg_ref[...] = jnp.zeros_like(dg_ref)

        dg_ref[...] += dg
        dfb = df.astype(BF16)
        df_ref[...] = dfb
        for j in range(N_CHIPS):
            da = _dot_nt(dfb, wd_ref[j])
            gt = gate_ref[j].astype(F32)
            sig = jax.nn.sigmoid(gt)
            silu = gt * sig
            dup_ref[j] = (da * silu).astype(BF16)
            dgate_ref[j] = (da * up_ref[j].astype(F32) * (sig + silu * (1.0 - sig))).astype(BF16)

    return pl.pallas_call(
        body, name="ffn_bwd_down", grid=(S // bm,),
        in_specs=[_rows(bm, D), _rows(bm, D), _whole((1, D)), _whole((N_CHIPS, Fs, D)), _slabs(bm, Fs),
                  _slabs(bm, Fs)],
        out_specs=[_rows(bm, D), _slabs(bm, Fs), _slabs(bm, Fs), _whole((1, D))],
        out_shape=[jax.ShapeDtypeStruct((S, D), BF16), jax.ShapeDtypeStruct((N_CHIPS, S, Fs), BF16),
                   jax.ShapeDtypeStruct((N_CHIPS, S, Fs), BF16), jax.ShapeDtypeStruct((1, D), F32)],
        compiler_params=_params(),
    )(dh, f, g_post, wd, gate, up)


def _ffn_bwd_up(dgate, dup, wg, wu, h_in, g_pre, dh):
    _, S, Fs = dgate.shape
    D = wg.shape[1]
    bm = _pick(S, 256)

    def body(dgate_ref, dup_ref, wg_ref, wu_ref, h_ref, g_ref, dh_ref, dhin_ref, dg_ref):
        dn = _dot_nt(dgate_ref[0], wg_ref[0]) + _dot_nt(dup_ref[0], wu_ref[0])
        for j in range(1, N_CHIPS):
            dn = dn + _dot_nt(dgate_ref[j], wg_ref[j]) + _dot_nt(dup_ref[j], wu_ref[j])
        _, vjp = jax.vjp(_rms, h_ref[...], g_ref[...])
        dhx, dg = vjp(dn)

        @pl.when(pl.program_id(0) == 0)
        def _():
            dg_ref[...] = jnp.zeros_like(dg_ref)

        dg_ref[...] += dg
        dhin_ref[...] = dh_ref[...] + dhx

    return pl.pallas_call(
        body, name="ffn_bwd_up", grid=(S // bm,),
        in_specs=[_slabs(bm, Fs), _slabs(bm, Fs), _whole((N_CHIPS, D, Fs)), _whole((N_CHIPS, D, Fs)), _rows(bm, D),
                  _whole((1, D)), _rows(bm, D)],
        out_specs=[_rows(bm, D), _whole((1, D))],
        out_shape=[jax.ShapeDtypeStruct((S, D), F32), jax.ShapeDtypeStruct((1, D), F32)],
        compiler_params=_params(),
    )(dgate, dup, wg, wu, h_in, g_pre, dh)


def _matmul(a, b, *, ta=False, tb=False, out_dtype=BF16, name, batch=None):
    n_batch = a.shape[0] if batch == "a" else b.shape[0] if batch == "b" else 1
    a_shape = a.shape[1:] if batch == "a" else a.shape
    b_shape = b.shape[1:] if batch == "b" else b.shape
    M, K = (a_shape[1], a_shape[0]) if ta else a_shape
    N = b_shape[0] if tb else b_shape[1]
    acc_budget = 12 * 1024 * 1024
    bm, bk = _pick(M, 1536), _pick(K, 512)
    while n_batch * N * bm * 4 > acc_budget and bm % (2 * LANE) == 0:
        bm //= 2
    bn = N if n_batch * N * bm * 4 <= acc_budget else _pick(N, 1536)
    nk = K // bk

    def body(a_ref, b_ref, o_ref, acc_ref):
        kk = pl.program_id(2)

        @pl.when(kk == 0)
        def _():
            acc_ref[...] = jnp.zeros_like(acc_ref)

        dims = (((0 if ta else 1,), (1 if tb else 0,)), ((), ()))
        if batch is None:
            acc_ref[...] += lax.dot_general(a_ref[...], b_ref[...], dims, preferred_element_type=F32)
        else:
            for g in range(n_batch):
                av = a_ref[g] if batch == "a" else a_ref[...]
                bv = b_ref[g] if batch == "b" else b_ref[...]
                acc_ref[g] += lax.dot_general(av, bv, dims, preferred_element_type=F32)

        @pl.when(kk == nk - 1)
        def _():
            o_ref[...] = acc_ref[...].astype(o_ref.dtype)

    def spec(block, index, batched):
        if batched:
            return pl.BlockSpec((n_batch,) + block, lambda i, j, k: (0,) + index(i, j, k))
        return pl.BlockSpec(block, index)

    a_spec = spec((bk, bm), lambda i, j, k: (k, i), batch == "a") if ta else \
        spec((bm, bk), lambda i, j, k: (i, k), batch == "a")
    b_spec = spec((bn, bk), lambda i, j, k: (j, k), batch == "b") if tb else \
        spec((bk, bn), lambda i, j, k: (k, j), batch == "b")
    lead = (n_batch,) if batch else ()
    return pl.pallas_call(
        body, name=name, grid=(M // bm, N // bn, nk),
        in_specs=[a_spec, b_spec],
        out_specs=spec((bm, bn), lambda i, j, k: (i, j), batch is not None),
        out_shape=jax.ShapeDtypeStruct(lead + (M, N), out_dtype),
        scratch_shapes=[pltpu.VMEM(lead + (bm, bn), F32)],
        compiler_params=_params(),
    )(a, b)


def _mix_fwd_in(h, g_pre, win, wgate, b_gate, b_forget):
    S, D = h.shape
    PW = win.shape[1] - LANE
    G = wgate.shape[1]
    bm = _pick(S, 256)

    def body(h_ref, g_ref, win_ref, wgate_ref, bg_ref, bf_ref, u_ref, proj_ref, fl_ref, sg_ref):
        u = _rms(h_ref[...], g_ref[...]).astype(BF16)
        u_ref[...] = u
        proj = _dot(u, win_ref[...])
        proj_ref[...] = proj[:, :PW].astype(BF16)
        fl_ref[...] = proj[:, PW:] + bf_ref[...]
        sg_ref[...] = jax.nn.sigmoid(_dot(u, wgate_ref[...]) + bg_ref[...]).astype(BF16)

    return pl.pallas_call(
        body, name="mix_fwd_in", grid=(S // bm,),
        in_specs=[_rows(bm, D), _whole((1, D)), _whole((D, PW + LANE)), _whole((D, G)), _whole((1, G)),
                  _whole((1, LANE))],
        out_specs=[_rows(bm, D), _rows(bm, PW), _rows(bm, LANE), _rows(bm, G)],
        out_shape=[jax.ShapeDtypeStruct((S, D), BF16), jax.ShapeDtypeStruct((S, PW), BF16),
                   jax.ShapeDtypeStruct((S, LANE), F32), jax.ShapeDtypeStruct((S, G), BF16)],
        compiler_params=_params(),
    )(h, g_pre, win, wgate, b_gate, b_forget)


def _mix_fwd_out(o_sb, o_fx, o_mem, sg, w_sb, w_fx, w_mem, w_out, h, g_post):
    S, D = h.shape
    bm = _pick(S, 256)
    widths = (o_sb.shape[1], o_fx.shape[1], o_mem.shape[1])

    def body(osb_ref, ofx_ref, omem_ref, sg_ref, wsb_ref, wfx_ref, wmem_ref, wout_ref, h_ref, g_ref,
             hout_ref, z_ref, merged_ref):
        s = sg_ref[...].astype(F32)
        merged = (s[:, :D] * _dot(osb_ref[...], wsb_ref[...]) + s[:, D:2 * D] * _dot(ofx_ref[...], wfx_ref[...])
                  + s[:, 2 * D:] * _dot(omem_ref[...], wmem_ref[...]))
        mb = merged.astype(BF16)
        merged_ref[...] = mb
        z = _dot(mb, wout_ref[...])
        z_ref[...] = z
        hout_ref[...] = h_ref[...] + _rms(z, g_ref[...])

    return pl.pallas_call(
        body, name="mix_fwd_out", grid=(S // bm,),
        in_specs=[_rows(bm, widths[0]), _rows(bm, widths[1]), _rows(bm, widths[2]), _rows(bm, 3 * D),
                  _whole((widths[0], D)), _whole((widths[1], D)), _whole((widths[2], D)), _whole((D, D)),
                  _rows(bm, D), _whole((1, D))],
        out_specs=[_rows(bm, D), _rows(bm, D), _rows(bm, D)],
        out_shape=[jax.ShapeDtypeStruct((S, D), F32), jax.ShapeDtypeStruct((S, D), F32),
                   jax.ShapeDtypeStruct((S, D), BF16)],
        compiler_params=_params(),
    )(o_sb, o_fx, o_mem, sg, w_sb, w_fx, w_mem, w_out, h, g_post)


def _mix_bwd_out(dh, z, g_post, w_out, o_sb, o_fx, o_mem, w_sb, w_fx, w_mem, sg):
    S, D = dh.shape
    bm = _pick(S, 256)
    widths = (o_sb.shape[1], o_fx.shape[1], o_mem.shape[1])

    def body(dh_ref, z_ref, g_ref, wout_ref, osb_ref, ofx_ref, omem_ref, wsb_ref, wfx_ref, wmem_ref, sg_ref,
             dz_ref, dbsb_ref, dbfx_ref, dbmem_ref, dosb_ref, dofx_ref, domem_ref, dgp_ref, dbg_ref, dg_ref):
        _, vjp = jax.vjp(_rms, z_ref[...], g_ref[...])
        dz, dg = vjp(dh_ref[...])

        @pl.when(pl.program_id(0) == 0)
        def _():
            dg_ref[...] = jnp.zeros_like(dg_ref)
            dbg_ref[...] = jnp.zeros_like(dbg_ref)

        dg_ref[...] += dg
        dzb = dz.astype(BF16)
        dz_ref[...] = dzb
        dmerged = _dot_nt(dzb, wout_ref[...])
        s = sg_ref[...].astype(F32)
        branches = ((osb_ref, wsb_ref, dbsb_ref, dosb_ref), (ofx_ref, wfx_ref, dbfx_ref, dofx_ref),
                    (omem_ref, wmem_ref, dbmem_ref, domem_ref))
        for k, (o_ref, w_ref, db_ref, do_ref) in enumerate(branches):
            gs = s[:, k * D:(k + 1) * D]
            dbb = (dmerged * gs).astype(BF16)
            db_ref[...] = dbb
            do_ref[...] = _dot_nt(dbb, w_ref[...]).astype(BF16)
            dgp = dmerged * _dot(o_ref[...], w_ref[...]) * gs * (1.0 - gs)
            dgp_ref[:, k * D:(k + 1) * D] = dgp.astype(BF16)
            dbg_ref[:, k * D:(k + 1) * D] += jnp.sum(dgp, axis=0, keepdims=True)

    return pl.pallas_call(
        body, name="mix_bwd_out", grid=(S // bm,),
        in_specs=[_rows(bm, D), _rows(bm, D), _whole((1, D)), _whole((D, D)),
                  _rows(bm, widths[0]), _rows(bm, widths[1]), _rows(bm, widths[2]),
                  _whole((widths[0], D)), _whole((widths[1], D)), _whole((widths[2], D)), _rows(bm, 3 * D)],
        out_specs=[_rows(bm, D)] * 4 + [_rows(bm, widths[0]), _rows(bm, widths[1]), _rows(bm, widths[2]),
                                        _rows(bm, 3 * D), _whole((1, 3 * D)), _whole((1, D))],
        out_shape=[jax.ShapeDtypeStruct((S, D), BF16)] * 4
        + [jax.ShapeDtypeStruct((S, w), BF16) for w in widths]
        + [jax.ShapeDtypeStruct((S, 3 * D), BF16), jax.ShapeDtypeStruct((1, 3 * D), F32),
           jax.ShapeDtypeStruct((1, D), F32)],
        compiler_params=_params(),
    )(dh, z, g_post, w_out, o_sb, o_fx, o_mem, w_sb, w_fx, w_mem, sg)


def _mix_bwd_in(dproj, dgp, win, wgate, h_in, g_pre, dh):
    S, PWL = dproj.shape
    G = dgp.shape[1]
    D = h_in.shape[1]
    bm = _pick(S, 256)

    def body(dproj_ref, dgp_ref, win_ref, wgate_ref, h_ref, g_ref, dh_ref, dhin_ref, dg_ref):
        du = _dot_nt(dproj_ref[...], win_ref[...]) + _dot_nt(dgp_ref[...], wgate_ref[...])
        _, vjp = jax.vjp(_rms, h_ref[...], g_ref[...])
        dhx, dg = vjp(du)

        @pl.when(pl.program_id(0) == 0)
        def _():
            dg_ref[...] = jnp.zeros_like(dg_ref)

        dg_ref[...] += dg
        dhin_ref[...] = dh_ref[...] + dhx

    return pl.pallas_call(
        body, name="mix_bwd_in", grid=(S // bm,),
        in_specs=[_rows(bm, PWL), _rows(bm, G), _whole((D, PWL)), _whole((D, G)), _rows(bm, D), _whole((1, D)),
                  _rows(bm, D)],
        out_specs=[_rows(bm, D), _whole((1, D))],
        out_shape=[jax.ShapeDtypeStruct((S, D), F32), jax.ShapeDtypeStruct((1, D), F32)],
        compiler_params=_params(),
    )(dproj, dgp, win, wgate, h_in, g_pre, dh)


def _log_sigmoid(x):
    return jnp.minimum(x, 0.0) - jnp.log(1.0 + jnp.exp(-jnp.abs(x)))


def _fox_cumsum(fl):
    S = fl.shape[0]
    rb = _pick(S, LANE)

    def body(fl_ref, c_ref, carry_ref):
        @pl.when(pl.program_id(0) == 0)
        def _():
            carry_ref[...] = jnp.zeros_like(carry_ref)

        r = lax.broadcasted_iota(jnp.int32, (rb, rb), 0)
        cidx = lax.broadcasted_iota(jnp.int32, (rb, rb), 1)
        tri = (cidx <= r).astype(BF16)
        hi, mid, lo = _split3(_log_sigmoid(fl_ref[...]))
        c = _dot(tri, hi) + _dot(tri, mid) + _dot(tri, lo) + carry_ref[...]
        c_ref[...] = c
        carry_ref[...] = c[rb - 1:rb, :]

    return pl.pallas_call(
        body, name="fox_cumsum", grid=(S // rb,),
        in_specs=[_rows(rb, LANE)], out_specs=_rows(rb, LANE),
        out_shape=jax.ShapeDtypeStruct((S, LANE), F32),
        scratch_shapes=[pltpu.VMEM((1, LANE), F32)],
        compiler_params=_params(),
    )(fl)


def _fox_dlogit(dc, fl):
    S = fl.shape[0]
    rb = _pick(S, LANE)
    nb = S // rb

    def body(dc_ref, fl_ref, dfl_ref, dbf_ref, carry_ref):
        @pl.when(pl.program_id(0) == 0)
        def _():
            carry_ref[...] = jnp.zeros_like(carry_ref)
            dbf_ref[...] = jnp.zeros_like(dbf_ref)

        r = lax.broadcasted_iota(jnp.int32, (rb, rb), 0)
        cidx = lax.broadcasted_iota(jnp.int32, (rb, rb), 1)
        tri = (cidx >= r).astype(BF16)
        hi, mid, lo = _split3(dc_ref[...])
        rc = _dot(tri, hi) + _dot(tri, mid) + _dot(tri, lo) + carry_ref[...]
        carry_ref[...] = rc[0:1, :]
        dfl = rc * jax.nn.sigmoid(-fl_ref[...])
        dfl_ref[...] = dfl.astype(BF16)
        dbf_ref[...] += jnp.sum(dfl, axis=0, keepdims=True)

    rev = pl.BlockSpec((rb, LANE), lambda i: (nb - 1 - i, 0))
    return pl.pallas_call(
        body, name="fox_dlogit", grid=(nb,),
        in_specs=[rev, rev], out_specs=[rev, _whole((1, LANE))],
        out_shape=[jax.ShapeDtypeStruct((S, LANE), BF16), jax.ShapeDtypeStruct((1, LANE), F32)],
        scratch_shapes=[pltpu.VMEM((1, LANE), F32)],
        compiler_params=_params(),
    )(dc, fl)


def _attn_blocks(kind, S, Sk):
    tq = _pick(S, 2048)
    tc = LANE if kind == "sb" else _pick(Sk, 256)
    return tq, tc


def _is_power_of_two(x):
    return math.frexp(x)[0] == 0.5


def _sb_logs(z):
    ln = -jnp.maximum(z, 0.0) - jnp.log(1.0 + jnp.exp(-jnp.abs(z)))
    return ln + z, ln


def _head_lanes(pack, dh):
    lane = lax.broadcasted_iota(jnp.int32, (1, LANE), 1)
    return [(lane >= hh * dh) & (lane < (hh + 1) * dh) for hh in range(pack)]


def _by_head(sel, parts):
    out = parts[0]
    for hh in range(1, len(parts)):
        out = jnp.where(sel[hh], parts[hh], out)
    return out


def _only_head(sel, hh, x):
    return x if len(sel) == 1 else jnp.where(sel[hh], x, jnp.zeros_like(x))


def _tail(x, r0):
    return x if not r0 else x[r0:]


def _put_tail(x, tail, r0):
    return tail if not r0 else jnp.concatenate([x[:r0], tail], axis=0)


def _add_tail(x, tail, r0):
    return x + tail if not r0 else jnp.concatenate([x[:r0], x[r0:] + tail], axis=0)


def _q_cols(tq, first):
    return pl.BlockSpec((tq, LANE), lambda g, i: (i, first // LANE + g))


def _k_cols(rows, first):
    return pl.BlockSpec((rows, LANE), lambda g, i: (0, first // LANE + g))


def _attn_fwd(kind, q, k, v, n_heads, dh, ccol=None, crow=None):
    (qa, q0), (ka, k0), (va, v0) = q, k, v
    S, Sk = qa.shape[0], ka.shape[0]
    pack = LANE // dh
    tq, tc = _attn_blocks(kind, S, Sk)
    scale = dh ** -0.5
    fold = _is_power_of_two(scale)
    causal = kind != "mem"
    n_diag = tq // tc if causal else 0
    unroll = 2 if causal else 1
    assert n_diag % unroll == 0 and (Sk // tc) % unroll == 0

    def body(*refs):
        if kind == "fox":
            q_ref, k_ref, v_ref, cc_ref, cr_ref, o_ref, lse_ref = refs
        else:
            q_ref, k_ref, v_ref, o_ref, lse_ref = refs
        i = pl.program_id(1)
        n_full = (i * tq) // tc if causal else Sk // tc
        qpos = i * tq + lax.broadcasted_iota(jnp.int32, (tq, tc), 0)
        kio = lax.broadcasted_iota(jnp.int32, (tq, tc), 1)
        heads = range(pack)
        sel = _head_lanes(pack, dh)
        q2 = q_ref[...] * scale if fold else q_ref[...]
        qs = [_only_head(sel, hh, q2) for hh in heads]

        def kv(jc):
            off = pl.multiple_of(jc * tc, tc)
            return off, k_ref[pl.ds(off, tc), :], v_ref[pl.ds(off, tc), :]

        if kind == "sb":
            tri = (lax.broadcasted_iota(jnp.int32, (tc, tc), 0) > lax.broadcasted_iota(jnp.int32, (tc, tc), 1)
                   ).astype(BF16)

            def chunk(jc, r0, runs, acc):
                off, k2, v2 = kv(jc)
                new_runs, pv = [], []
                for hh in heads:
                    lb, ln = _sb_logs(_dot_nt(_tail(qs[hh], r0), k2))
                    if r0 is not None:
                        mask = (off + _tail(kio, r0)) < _tail(qpos, r0)
                        ln = jnp.where(mask, ln, 0.0)
                    w = jnp.exp(lb + _cumdot(ln, tri) + _tail(runs[hh], r0))
                    if r0 is not None:
                        w = jnp.where(mask, w, 0.0)
                    pv.append(_dot(w.astype(BF16), v2))
                    new_runs.append(_add_tail(runs[hh], jnp.sum(ln, axis=1, keepdims=True), r0))
                return tuple(new_runs), _add_tail(acc, _by_head(sel, pv), r0)

            state = (tuple(jnp.zeros((tq, 1), F32) for _ in heads), jnp.zeros((tq, LANE), F32))
            for d in range(n_diag - 1, -1, -1):
                state = chunk(n_full + d, d * tc, *state)

            def trip(t, st):
                for u in range(unroll):
                    st = chunk(n_full - 1 - unroll * t - u, None, *st)
                return st

            runs, acc = lax.fori_loop(0, n_full // unroll, trip, state)
            o_ref[...] = acc.astype(o_ref.dtype)
            for hh in heads:
                lse_ref[hh] = runs[hh]
        else:
            def chunk(jc, r0, ms, ls, acc):
                off, k2, v2 = kv(jc)
                new_ms, new_ls, alphas, pv = [], [], [], []
                for hh in heads:
                    z = _dot_nt(_tail(qs[hh], r0), k2)
                    if not fold:
                        z = z * scale
                    if kind == "fox":
                        z = z + _tail(cc_ref[hh], r0) - cr_ref[hh, pl.ds(jc, 1), :]
                    if r0 is not None:
                        z = jnp.where((off + _tail(kio, r0)) <= _tail(qpos, r0), z, NEG)
                    m_old, l_old = _tail(ms[hh], r0), _tail(ls[hh], r0)
                    m_new = jnp.maximum(m_old, jnp.max(z, axis=1, keepdims=True))
                    alpha = jnp.exp(m_old - m_new)
                    p = jnp.exp(z - m_new)
                    new_ms.append(_put_tail(ms[hh], m_new, r0))
                    new_ls.append(_put_tail(ls[hh], alpha * l_old + jnp.sum(p, axis=1, keepdims=True), r0))
                    alphas.append(alpha)
                    pv.append(_dot(p.astype(BF16), v2))
                acc_new = _by_head(sel, alphas) * _tail(acc, r0) + _by_head(sel, pv)
                return tuple(new_ms), tuple(new_ls), _put_tail(acc, acc_new, r0)

            state = (tuple(jnp.full((tq, 1), NEG, F32) for _ in heads), tuple(jnp.zeros((tq, 1), F32) for _ in heads),
                     jnp.zeros((tq, LANE), F32))

            def trip(t, st):
                for u in range(unroll):
                    st = chunk(unroll * t + u, None, *st)
                return st

            state = lax.fori_loop(0, n_full // unroll, trip, state)
            for d in range(n_diag):
                state = chunk(n_full + d, d * tc, *state)
            ms, ls, acc = state
            o_ref[...] = (acc / _by_head(sel, ls)).astype(o_ref.dtype)
            for hh in heads:
                lse_ref[hh] = ms[hh] + jnp.log(ls[hh])

    colspec = pl.BlockSpec((pack, tq, 1), lambda g, i: (g, i, 0))
    in_specs, args = [_q_cols(tq, q0), _k_cols(Sk, k0), _k_cols(Sk, v0)], [qa, ka, va]
    if kind == "fox":
        in_specs += [colspec, pl.BlockSpec((pack, Sk // tc, tc), lambda g, i: (g, 0, 0))]
        args += [ccol, crow]
    return pl.pallas_call(
        body, name="attn_fwd_" + kind, grid=(n_heads // pack, S // tq),
        in_specs=in_specs, out_specs=[_q_cols(tq, 0), colspec],
        out_shape=[jax.ShapeDtypeStruct((S, n_heads * dh), BF16), jax.ShapeDtypeStruct((n_heads, S, 1), F32)],
        compiler_params=_params(),
    )(*args)


def _attn_bwd(kind, q, k, v, o, do, n_heads, dh, ccol=None, crow=None, lse=None):
    (qa, q0), (ka, k0), (va, v0) = q, k, v
    S, Sk = qa.shape[0], ka.shape[0]
    pack = LANE // dh
    tq, tc = _attn_blocks(kind, S, Sk)
    scale = dh ** -0.5
    fold = _is_power_of_two(scale)
    causal = kind != "mem"
    n_diag = tq // tc if causal else 0
    unroll = 2 if kind == "sb" else 1
    assert n_diag % unroll == 0 and (Sk // tc) % unroll == 0
    nq = S // tq

    def body(*refs):
        if kind == "fox":
            (q_ref, k_ref, v_ref, o_ref, do_ref, cc_ref, cr_ref, lse_ref,
             dq_ref, dk_ref, dv_ref, dc_ref, dcc_ref, dk_acc, dv_acc, dc_acc) = refs
        else:
            q_ref, k_ref, v_ref, o_ref, do_ref, lse_ref, dq_ref, dk_ref, dv_ref, dk_acc, dv_acc = refs
        i = pl.program_id(1)

        @pl.when(i == 0)
        def _():
            dk_acc[...] = jnp.zeros_like(dk_acc)
            dv_acc[...] = jnp.zeros_like(dv_acc)
            if kind == "fox":
                dc_acc[...] = jnp.zeros_like(dc_acc)

        n_full = (i * tq) // tc if causal else Sk // tc
        qpos = i * tq + lax.broadcasted_iota(jnp.int32, (tq, tc), 0)
        kio = lax.broadcasted_iota(jnp.int32, (tq, tc), 1)
        heads = range(pack)
        sel = _head_lanes(pack, dh)
        q2 = q_ref[...] * scale if fold else q_ref[...]
        do2 = do_ref[...]
        qs = [_only_head(sel, hh, q2) for hh in heads]
        dos = [_only_head(sel, hh, do2) for hh in heads]

        def kv(jc):
            off = pl.multiple_of(jc * tc, tc)
            return off, k_ref[pl.ds(off, tc), :], v_ref[pl.ds(off, tc), :]

        def accumulate(off, k2, dzb, wb, dq, r0):
            q2t, do2t = _tail(q2, r0), _tail(do2, r0)
            dk_acc[pl.ds(off, tc), :] += _by_head(sel, [_dot_tn(dzb[hh], q2t) for hh in heads])
            dv_acc[pl.ds(off, tc), :] += _by_head(sel, [_dot_tn(wb[hh], do2t) for hh in heads])
            return _add_tail(dq, _by_head(sel, [_dot(dzb[hh], k2) for hh in heads]), r0)

        if kind == "sb":
            r = lax.broadcasted_iota(jnp.int32, (tc, tc), 0)
            cidx = lax.broadcasted_iota(jnp.int32, (tc, tc), 1)
            tri_inc = (r <= cidx).astype(BF16)
            tri_exc = (r < cidx).astype(BF16)

            def chunk(jc, r0, pres, pres_e, dq):
                off, k2, v2 = kv(jc)
                new_pres, new_pres_e, dzb, wb = [], [], [], []
                for hh in heads:
                    lb, ln = _sb_logs(_dot_nt(_tail(qs[hh], r0), k2))
                    if r0 is not None:
                        mask = (off + _tail(kio, r0)) < _tail(qpos, r0)
                        ln = jnp.where(mask, ln, 0.0)
                    w = jnp.exp(lb + (_tail(lse_ref[hh], r0) - _tail(pres[hh], r0) - _cumdot(ln, tri_inc)))
                    if r0 is not None:
                        w = jnp.where(mask, w, 0.0)
                    e = w * _dot_nt(_tail(dos[hh], r0), v2)
                    beta = jnp.exp(lb)
                    dz = e * (1.0 - beta) - beta * (_tail(pres_e[hh], r0) + _cumdot(e, tri_exc))
                    if r0 is not None:
                        dz = jnp.where(mask, dz, 0.0)
                    dzb.append(dz.astype(BF16))
                    wb.append(w.astype(BF16))
                    new_pres.append(_add_tail(pres[hh], jnp.sum(ln, axis=1, keepdims=True), r0))
                    new_pres_e.append(_add_tail(pres_e[hh], jnp.sum(e, axis=1, keepdims=True), r0))
                return tuple(new_pres), tuple(new_pres_e), accumulate(off, k2, dzb, wb, dq, r0)

            state = (tuple(jnp.zeros((tq, 1), F32) for _ in heads), tuple(jnp.zeros((tq, 1), F32) for _ in heads),
                     jnp.zeros((tq, LANE), F32))
        else:
            prod = o_ref[...].astype(F32) * do2.astype(F32)
            dsum = [jnp.sum(_only_head(sel, hh, prod), axis=1, keepdims=True) for hh in heads]

            def chunk(jc, r0, rowsums, dq):
                off, k2, v2 = kv(jc)
                new_rowsums, dsb, pb = [], [], []
                for hh in heads:
                    z = _dot_nt(_tail(qs[hh], r0), k2)
                    if not fold:
                        z = z * scale
                    if kind == "fox":
                        z = z + _tail(cc_ref[hh], r0) - cr_ref[hh, pl.ds(jc, 1), :]
                    if r0 is not None:
                        z = jnp.where((off + _tail(kio, r0)) <= _tail(qpos, r0), z, NEG)
                    p = jnp.exp(z - _tail(lse_ref[hh], r0))
                    ds = p * (_dot_nt(_tail(dos[hh], r0), v2) - _tail(dsum[hh], r0))
                    dsb.append(ds.astype(BF16))
                    pb.append(p.astype(BF16))
                    if kind == "fox":
                        dc_acc[hh, pl.ds(jc, 1), :] -= jnp.sum(ds, axis=0, keepdims=True)
                        new_rowsums.append(_add_tail(rowsums[hh], jnp.sum(ds, axis=1, keepdims=True), r0))
                    else:
                        new_rowsums.append(rowsums[hh])
                return tuple(new_rowsums), accumulate(off, k2, dsb, pb, dq, r0)

            state = (tuple(jnp.zeros((tq, 1), F32) for _ in heads), jnp.zeros((tq, LANE), F32))

        def trip(t, st):
            for u in range(unroll):
                st = chunk(unroll * t + u, None, *st)
            return st

        state = lax.fori_loop(0, n_full // unroll, trip, state)
        for d in range(n_diag):
            state = chunk(n_full + d, d * tc, *state)
        dq_ref[...] = (state[-1] * scale).astype(dq_ref.dtype)
        if kind == "fox":
            for hh in heads:
                dcc_ref[hh] = state[0][hh]

        @pl.when(i == nq - 1)
        def _():
            dk = dk_acc[...] if fold else dk_acc[...] * scale
            dk_ref[...] = dk.astype(dk_ref.dtype)
            dv_ref[...] = dv_acc[...].astype(dv_ref.dtype)
            if kind == "fox":
                dc_ref[...] = dc_acc[...]

    colspec = pl.BlockSpec((pack, tq, 1), lambda g, i: (g, i, 0))
    rowspec = pl.BlockSpec((pack, Sk // tc, tc), lambda g, i: (g, 0, 0))
    in_specs = [_q_cols(tq, q0), _k_cols(Sk, k0), _k_cols(Sk, v0), _q_cols(tq, 0), _q_cols(tq, 0)]
    args = [qa, ka, va, o, do]
    if kind == "fox":
        in_specs += [colspec, rowspec]
        args += [ccol, crow]
    in_specs += [colspec]
    args += [lse]
    width = n_heads * dh
    out_specs = [_q_cols(tq, 0), _k_cols(Sk, 0), _k_cols(Sk, 0)]
    out_shape = [jax.ShapeDtypeStruct((S, width), BF16), jax.ShapeDtypeStruct((Sk, width), BF16),
                 jax.ShapeDtypeStruct((Sk, width), BF16)]
    scratch = [pltpu.VMEM((Sk, LANE), F32), pltpu.VMEM((Sk, LANE), F32)]
    if kind == "fox":
        out_specs += [rowspec, colspec]
        out_shape += [jax.ShapeDtypeStruct((n_heads, Sk // tc, tc), F32), jax.ShapeDtypeStruct((n_heads, S, 1), F32)]
        scratch.append(pltpu.VMEM((pack, Sk // tc, tc), F32))
    return pl.pallas_call(
        body, name="attn_bwd_" + kind, grid=(n_heads // pack, nq),
        in_specs=in_specs, out_specs=out_specs, out_shape=out_shape, scratch_shapes=scratch,
        compiler_params=_params(),
    )(*args)


def _mem_norm(mem, g):
    M, D = mem.shape

    def body(mem_ref, g_ref, out_ref):
        out_ref[...] = _rms(mem_ref[...], g_ref[...]).astype(BF16)

    return pl.pallas_call(
        body, name="mem_norm", grid=(1,),
        in_specs=[_whole((M, D)), _whole((1, D))], out_specs=_whole((M, D)),
        out_shape=jax.ShapeDtypeStruct((M, D), BF16), compiler_params=_params(),
    )(mem, g)


def _mem_norm_bwd(mem, g, dmem_n):
    M, D = mem.shape
    L = dmem_n.shape[0]

    def body(mem_ref, g_ref, d_ref, dg_ref):
        d = d_ref[0]
        for l in range(1, L):
            d = d + d_ref[l]
        _, vjp = jax.vjp(_rms, mem_ref[...], g_ref[...])
        dg_ref[...] = vjp(d)[1]

    return pl.pallas_call(
        body, name="mem_norm_bwd", grid=(1,),
        in_specs=[_whole((M, D)), _whole((1, D)), _whole((L, M, D))], out_specs=_whole((1, D)),
        out_shape=jax.ShapeDtypeStruct((1, D), F32), compiler_params=_params(),
    )(mem, g, dmem_n)


def _loss_head(h, target):
    S, D = h.shape
    bm = _pick(S, 512)

    def body(h_ref, t_ref, dh_ref, loss_ref):
        err = h_ref[...] - t_ref[...]
        dh_ref[...] = err * (1.0 / D)

        @pl.when(pl.program_id(0) == 0)
        def _():
            loss_ref[...] = jnp.zeros_like(loss_ref)

        loss_ref[...] += 0.5 * jnp.sum(jnp.mean(err * err, axis=-1, keepdims=True), axis=0, keepdims=True)

    return pl.pallas_call(
        body, name="loss_head", grid=(S // bm,),
        in_specs=[_rows(bm, D), _rows(bm, D)], out_specs=[_rows(bm, D), _whole((8, LANE))],
        out_shape=[jax.ShapeDtypeStruct((S, D), F32), jax.ShapeDtypeStruct((8, LANE), F32)],
        compiler_params=_params(),
    )(h, target)


def _adamw(w, g, m, v, name):
    R, C = w.shape
    rb = R if R * C * 4 <= (1 << 20) else _pick(R, 256)
    if R % rb:
        rb = R
    c1 = 1.0 - ADAM_B1 ** ADAM_STEP
    c2 = 1.0 - ADAM_B2 ** ADAM_STEP

    def body(w_ref, g_ref, m_ref, v_ref, d_ref, mo_ref, vo_ref):
        gv = g_ref[...]
        mn = ADAM_B1 * m_ref[...] + (1.0 - ADAM_B1) * gv
        vn = ADAM_B2 * v_ref[...] + (1.0 - ADAM_B2) * (gv * gv)
        mo_ref[...] = mn
        vo_ref[...] = vn
        d_ref[...] = -ADAM_LR * ((mn / c1) / (jnp.sqrt(vn / c2) + ADAM_EPS) + ADAM_WD * w_ref[...])

    return pl.pallas_call(
        body, name=name, grid=(R // rb,),
        in_specs=[_rows(rb, C)] * 4, out_specs=[_rows(rb, C)] * 3,
        out_shape=[jax.ShapeDtypeStruct((R, C), F32)] * 3, compiler_params=_params(),
    )(w, g, m, v)


def _adamw_reduced(w, m, v, mine, theirs, c_idx, first_row, name):
    L, a, b = w.shape
    Lh = L // 2
    rb = _shard_row_block(a)
    nb = a // rb
    assert first_row % rb == 0
    c1 = 1.0 - ADAM_B1 ** ADAM_STEP
    c2 = 1.0 - ADAM_B2 ** ADAM_STEP

    def own(i, c_ref):
        return (i, 0)

    def reduced(i, c_ref):
        return (first_row // rb + ((i // nb) % Lh) * nb + i % nb, 0)

    def body(c_ref, w_ref, m_ref, v_ref, mine_ref, theirs_ref, g_ref, d_ref, mo_ref, vo_ref):
        half = (pl.program_id(0) // nb) // Lh
        gv = jnp.where(c_ref[0] == half, mine_ref[...], theirs_ref[...])
        g_ref[...] = gv
        mn = ADAM_B1 * m_ref[...] + (1.0 - ADAM_B1) * gv
        vn = ADAM_B2 * v_ref[...] + (1.0 - ADAM_B2) * (gv * gv)
        mo_ref[...] = mn
        vo_ref[...] = vn
        d_ref[...] = -ADAM_LR * ((mn / c1) / (jnp.sqrt(vn / c2) + ADAM_EPS) + ADAM_WD * w_ref[...])

    outs = pl.pallas_call(
        body, name=name,
        grid_spec=pltpu.PrefetchScalarGridSpec(
            num_scalar_prefetch=1, grid=(L * nb,),
            in_specs=[pl.BlockSpec((rb, b), own)] * 3 + [pl.BlockSpec((rb, b), reduced)] * 2,
            out_specs=[pl.BlockSpec((rb, b), own)] * 4),
        out_shape=[jax.ShapeDtypeStruct((L * a, b), F32)] * 4, compiler_params=_params(),
    )(c_idx, w.reshape(L * a, b), m.reshape(L * a, b), v.reshape(L * a, b), mine, theirs)
    return [t.reshape(L, a, b) for t in outs]


ANY = pl.BlockSpec(memory_space=pl.ANY)
MESH = pl.DeviceIdType.MESH


def _place():
    x, y, c = lax.axis_index("x"), lax.axis_index("y"), lax.axis_index("c")
    others = [(1 - x, y), (x, 1 - y), (1 - x, 1 - y)]
    return x, y, c, others


def _place_own(loc, chip_idx):
    _, R, C = loc.shape
    rb = _pick(R, 2 * FLAT_ROW_BLOCK)

    def body(chip_ref, loc_ref, out_ref):
        out_ref[...] = loc_ref[...]

    return pl.pallas_call(
        body, name="place_own",
        grid_spec=pltpu.PrefetchScalarGridSpec(
            num_scalar_prefetch=1, grid=(2, R // rb),
            in_specs=[pl.BlockSpec((None, rb, C), lambda hf, i, chip_ref: (hf, i, 0))],
            out_specs=pl.BlockSpec((None, None, rb, C), lambda hf, i, chip_ref: (chip_ref[0], hf, i, 0))),
        out_shape=jax.ShapeDtypeStruct((N_CHIPS, 2, R, C), loc.dtype), compiler_params=_params(),
    )(chip_idx, loc)


def _gather_weights(locs, owns):
    n = len(locs)

    def body(*refs):
        loc_refs, out_refs, (send_sems, recv_sems) = refs[:n], refs[2 * n:3 * n], refs[3 * n:]
        x, y, c, others = _place()
        me = 2 * x + y
        sibling = (x, y, 1 - c)

        def copy(a, k, src, dst, to):
            return pltpu.make_async_remote_copy(src_ref=src, dst_ref=dst, send_sem=send_sems.at[a, k],
                                                recv_sem=recv_sems.at[a, k], device_id=to, device_id_type=MESH)

        first = [copy(a, j, loc_refs[a].at[c], out_refs[a].at[me, c], (ox, oy, c))
                 for j, (ox, oy) in enumerate(others) for a in range(n)]
        for cp in first:
            cp.start()
        passed = []
        for j, (ox, oy) in enumerate(others):
            for a in range(n):
                landed = out_refs[a].at[2 * ox + oy, c]
                copy(a, j, loc_refs[a].at[c], landed, sibling).wait_recv()
                cp = copy(a, 3 + j, landed, landed, sibling)
                cp.start()
                passed.append(cp)
        for j, (ox, oy) in enumerate(others):
            for a in range(n):
                copy(a, 3 + j, loc_refs[a].at[c], out_refs[a].at[2 * ox + oy, 1 - c], sibling).wait_recv()
        for cp in first + passed:
            cp.wait_send()

    return pl.pallas_call(
        body, name="gather_weights", in_specs=[ANY] * (2 * n), out_specs=[ANY] * n,
        out_shape=[jax.ShapeDtypeStruct(own.shape, own.dtype) for own in owns],
        input_output_aliases={n + a: a for a in range(n)},
        scratch_shapes=[pltpu.SemaphoreType.DMA((n, 6)), pltpu.SemaphoreType.DMA((n, 6))],
    )(*locs, *owns)


def _pair_exchange(gs):
    n = len(gs)

    def body(*refs):
        g_refs, out_refs, (send_sems, recv_sems) = refs[:n], refs[n:2 * n], refs[2 * n:]
        x, y, c, _ = _place()
        copies = [pltpu.make_async_remote_copy(src_ref=g_refs[a].at[1 - c], dst_ref=out_refs[a],
                                               send_sem=send_sems.at[a], recv_sem=recv_sems.at[a],
                                               device_id=(x, y, 1 - c), device_id_type=MESH) for a in range(n)]
        for cp in copies:
            cp.start()
        for cp in copies:
            cp.wait()

    return pl.pallas_call(
        body, name="pair_exchange", in_specs=[ANY] * n, out_specs=[ANY] * n,
        out_shape=[jax.ShapeDtypeStruct(g.shape[1:], g.dtype) for g in gs],
        scratch_shapes=[pltpu.SemaphoreType.DMA((n,)), pltpu.SemaphoreType.DMA((n,))],
    )(*gs)


def _pair_sum(g, sib, c_idx):
    _, _, R, C = g.shape
    rb = _pick(R, 512)

    def body(c_ref, g_ref, s_ref, o_ref):
        o_ref[...] = (g_ref[...].astype(F32) + s_ref[...].astype(F32)).astype(o_ref.dtype)

    return pl.pallas_call(
        body, name="pair_sum",
        grid_spec=pltpu.PrefetchScalarGridSpec(
            num_scalar_prefetch=1, grid=(N_CHIPS, R // rb),
            in_specs=[pl.BlockSpec((None, None, rb, C), lambda j, i, c_ref: (c_ref[0], j, i, 0)),
                      pl.BlockSpec((None, rb, C), lambda j, i, c_ref: (j, i, 0))],
            out_specs=pl.BlockSpec((None, rb, C), lambda j, i, c_ref: (j, i, 0))),
        out_shape=jax.ShapeDtypeStruct((N_CHIPS, R, C), g.dtype), compiler_params=_params(),
    )(c_idx, g, sib)


def _chip_exchange(ps):
    n = len(ps)

    def body(*refs):
        p_refs, out_refs, (send_sems, recv_sems) = refs[:n], refs[n:2 * n], refs[2 * n:]
        x, y, c, others = _place()
        copies = []
        for j, (ox, oy) in enumerate(others):
            for a in range(n):
                cp = pltpu.make_async_remote_copy(src_ref=p_refs[a].at[2 * ox + oy], dst_ref=out_refs[a].at[j],
                                                  send_sem=send_sems.at[a, j], recv_sem=recv_sems.at[a, j],
                                                  device_id=(ox, oy, c), device_id_type=MESH)
                cp.start()
                copies.append(cp)
        for cp in copies:
            cp.wait()

    return pl.pallas_call(
        body, name="chip_exchange", in_specs=[ANY] * n, out_specs=[ANY] * n,
        out_shape=[jax.ShapeDtypeStruct((N_CHIPS - 1,) + p.shape[1:], p.dtype) for p in ps],
        scratch_shapes=[pltpu.SemaphoreType.DMA((n, 3)), pltpu.SemaphoreType.DMA((n, 3))],
    )(*ps)


def _chip_sum(p, r, chip_idx):
    _, R, C = r.shape
    rb = _pick(R, 512)

    def body(chip_ref, p_ref, r_ref, o_ref):
        acc = p_ref[...].astype(F32)
        for j in range(N_CHIPS - 1):
            acc = acc + r_ref[j].astype(F32)
        o_ref[...] = acc

    return pl.pallas_call(
        body, name="chip_sum",
        grid_spec=pltpu.PrefetchScalarGridSpec(
            num_scalar_prefetch=1, grid=(R // rb,),
            in_specs=[pl.BlockSpec((None, rb, C), lambda i, chip_ref: (chip_ref[0], i, 0)),
                      pl.BlockSpec((N_CHIPS - 1, rb, C), lambda i, chip_ref: (0, i, 0))],
            out_specs=pl.BlockSpec((rb, C), lambda i, chip_ref: (i, 0))),
        out_shape=jax.ShapeDtypeStruct((R, C), F32), compiler_params=_params(),
    )(chip_idx, p, r)


def _pair_swap(rhs):
    n = len(rhs)

    def body(*refs):
        rh_refs, out_refs, (send_sems, recv_sems) = refs[:n], refs[n:2 * n], refs[2 * n:]
        x, y, c, _ = _place()
        copies = [pltpu.make_async_remote_copy(src_ref=rh_refs[a], dst_ref=out_refs[a], send_sem=send_sems.at[a],
                                               recv_sem=recv_sems.at[a], device_id=(x, y, 1 - c),
                                               device_id_type=MESH) for a in range(n)]
        for cp in copies:
            cp.start()
        for cp in copies:
            cp.wait()

    return pl.pallas_call(
        body, name="pair_swap", in_specs=[ANY] * n, out_specs=[ANY] * n,
        out_shape=[jax.ShapeDtypeStruct(rh.shape, rh.dtype) for rh in rhs],
        scratch_shapes=[pltpu.SemaphoreType.DMA((n,)), pltpu.SemaphoreType.DMA((n,))],
    )(*rhs)


def _all_reduce_small(s):
    R, C = s.shape

    def body(s_ref, o_ref, buf, send_sems, recv_sems):
        x, y, c, _ = _place()
        me = 4 * x + 2 * y + c
        sends = []
        for k in range(1, N_DEV):
            fx, fy, fc = (k >> 2) & 1, (k >> 1) & 1, k & 1
            to = (x ^ fx, y ^ fy, c ^ fc)
            cp = pltpu.make_async_remote_copy(src_ref=s_ref, dst_ref=buf.at[me], send_sem=send_sems.at[k - 1],
                                              recv_sem=recv_sems.at[k - 1], device_id=to, device_id_type=MESH)
            cp.start()
            sends.append(cp)
        buf[me] = s_ref[...]
        for k in range(1, N_DEV):
            fx, fy, fc = (k >> 2) & 1, (k >> 1) & 1, k & 1
            frm = 4 * (x ^ fx) + 2 * (y ^ fy) + (c ^ fc)
            pltpu.make_async_remote_copy(src_ref=s_ref, dst_ref=buf.at[frm], send_sem=send_sems.at[k - 1],
                                         recv_sem=recv_sems.at[k - 1], device_id=(x, y, c),
                                         device_id_type=MESH).wait_recv()
        acc = buf[0]
        for d in range(1, N_DEV):
            acc = acc + buf[d]
        o_ref[...] = acc
        for cp in sends:
            cp.wait_send()

    vm = pl.BlockSpec(memory_space=pltpu.VMEM)
    return pl.pallas_call(
        body, name="all_reduce_small", in_specs=[vm], out_specs=vm,
        out_shape=jax.ShapeDtypeStruct((R, C), F32),
        scratch_shapes=[pltpu.VMEM((N_DEV, R, C), F32), pltpu.SemaphoreType.DMA((N_DEV - 1,)),
                        pltpu.SemaphoreType.DMA((N_DEV - 1,))],
    )(s)


def _padded(n):
    return -(-n // FLAT_UNIT) * FLAT_UNIT


def _pack_flat(pieces, dtype, row_block=FLAT_ROW_BLOCK):
    flat = []
    for p in pieces:
        p = p.reshape(-1).astype(dtype)
        flat.append(jnp.pad(p, (0, _padded(p.size) - p.size)))
    total = sum(p.size for p in flat)
    flat.append(jnp.zeros((-total) % (row_block * FLAT_COLS), dtype))
    return jnp.concatenate(flat).reshape(-1, FLAT_COLS)


def _unpack_flat(flat, shapes):
    lead = flat.shape[:-2]
    flat = flat.reshape(lead + (-1,))
    out, off = [], 0
    for shp in shapes:
        n = math.prod(shp)
        out.append(flat[..., off:off + n].reshape(lead + tuple(shp)))
        off += _padded(n)
    return out


def _shard_row_block(a):
    for rb in range(min(a, 512) // 16 * 16, 0, -16):
        if a % rb == 0:
            return rb
    return a


def _row_layout(shapes, n_layers):
    groups = {}
    for name, (a, b) in shapes.items():
        names, first, rows = groups.get(b, ((), {}, 0))
        rb = _shard_row_block(a)
        start = -(-rows // rb) * rb
        groups[b] = (names + (name,), {**first, name: start}, start + n_layers * a)
    return {b: (names, first, -(-rows // FLAT_ROW_BLOCK) * FLAT_ROW_BLOCK) for b, (names, first, rows) in groups.items()}


def _pack_rows(group, width, pieces, dtype):
    names, first, rows = group
    parts, at = [], 0
    for name in names:
        if first[name] > at:
            parts.append(jnp.zeros((first[name] - at, width), dtype))
        parts.append(pieces[name].astype(dtype))
        at = first[name] + pieces[name].shape[0]
    if rows > at:
        parts.append(jnp.zeros((rows - at, width), dtype))
    return jnp.concatenate(parts, axis=0)


def _slab(t, axis, j):
    if t.ndim == 3:
        return t[j]
    n = t.shape[axis - 1] // N_CHIPS
    return lax.slice_in_dim(t, j * n, (j + 1) * n, axis=axis - 1)


def _layer_fwd(h0, mem_n, wl, dims):
    n_sb, n_fx, n_mem, sbw, fxw, memw = dims
    n1, gate1, up1, a1 = _ffn_fwd_up(h0, wl["ffn1_pre_g"], wl["ffn1_w_gate"], wl["ffn1_w_up"])
    h1, f1 = _ffn_fwd_down(a1, wl["ffn1_w_down"], h0, wl["ffn1_post_g"])

    u, proj, fl, sg = _mix_fwd_in(h1, wl["mix_pre_g"], wl["w_in"], wl["w_gate"], wl["b_gate"], wl["b_forget"])
    c = _fox_cumsum(fl)
    S = h0.shape[0]
    tc = _attn_blocks("fox", S, S)[1]
    ct = c[:, :n_fx].T
    ccol, crow = ct.reshape(n_fx, S, 1), ct.reshape(n_fx, S // tc, tc)
    qkv_sb = [(proj, k * sbw) for k in range(3)]
    qkv_fx = [(proj, 3 * sbw + k * fxw) for k in range(3)]
    kv = _matmul(mem_n, wl["w_mem_kv"], out_dtype=BF16, name="mem_kv")
    qkv_mem = [(proj, 3 * sbw + 3 * fxw), (kv, 0), (kv, memw)]
    o_sb, tot_sb = _attn_fwd("sb", *qkv_sb, n_sb, HEAD_DIM)
    o_fx, lse_fx = _attn_fwd("fox", *qkv_fx, n_fx, HEAD_DIM, ccol, crow)
    o_mem, lse_mem = _attn_fwd("mem", *qkv_mem, n_mem, MEM_HEAD_DIM)
    h2, zmix, merged = _mix_fwd_out(o_sb, o_fx, o_mem, sg, wl["w_br_sb"], wl["w_br_fox"], wl["w_br_mem"],
                                    wl["w_out"], h1, wl["mix_post_g"])

    n2, gate2, up2, a2 = _ffn_fwd_up(h2, wl["ffn2_pre_g"], wl["ffn2_w_gate"], wl["ffn2_w_up"])
    h3, f2 = _ffn_fwd_down(a2, wl["ffn2_w_down"], h2, wl["ffn2_post_g"])
    saved = dict(h0=h0, n1=n1, gate1=gate1, up1=up1, a1=a1, f1=f1, h1=h1, u=u, fl=fl, sg=sg,
                 qkv_sb=qkv_sb, qkv_fx=qkv_fx, qkv_mem=qkv_mem, ccol=ccol, crow=crow, o_sb=o_sb, o_fx=o_fx, o_mem=o_mem,
                 tot_sb=tot_sb, lse_fx=lse_fx, lse_mem=lse_mem,
                 zmix=zmix, merged=merged, h2=h2, n2=n2, gate2=gate2, up2=up2, a2=a2, f2=f2)
    return h3, saved


def _ffn_bwd(dh, sv, wl, tag, h_in):
    n, gate, up, a, f = (sv[k + tag] for k in ("n", "gate", "up", "a", "f"))
    pre = "ffn" + tag
    df, dgate, dup, dg_post = _ffn_bwd_down(dh, f, wl[pre + "_post_g"], wl[pre + "_w_down"], gate, up)
    dh_in, dg_pre = _ffn_bwd_up(dgate, dup, wl[pre + "_w_gate"], wl[pre + "_w_up"], h_in, wl[pre + "_pre_g"], dh)
    grads = {pre + "_post_g": dg_post, pre + "_pre_g": dg_pre,
             pre + "_w_down": _matmul(a, df, ta=True, batch="a", name="dw_down"),
             pre + "_w_gate": _matmul(n, dgate, ta=True, batch="b", name="dw_gate"),
             pre + "_w_up": _matmul(n, dup, ta=True, batch="b", name="dw_up")}
    return dh_in, grads


def _layer_bwd(dh3, mem_n, wl, sv, dims):
    n_sb, n_fx, n_mem, sbw, fxw, memw = dims
    S = dh3.shape[0]
    dh2, grads = _ffn_bwd(dh3, sv, wl, "2", sv["h2"])

    (dz, db_sb, db_fx, db_mem, do_sb, do_fx, do_mem, dgp, db_gate, dg_post) = _mix_bwd_out(
        dh2, sv["zmix"], wl["mix_post_g"], wl["w_out"], sv["o_sb"], sv["o_fx"], sv["o_mem"],
        wl["w_br_sb"], wl["w_br_fox"], wl["w_br_mem"], sv["sg"])
    grads["mix_post_g"] = dg_post
    grads["b_gate"] = db_gate
    grads["w_out"] = _matmul(sv["merged"], dz, ta=True, name="dw_out")
    grads["w_br_sb"] = _matmul(sv["o_sb"], db_sb, ta=True, name="dw_br_sb")
    grads["w_br_fox"] = _matmul(sv["o_fx"], db_fx, ta=True, name="dw_br_fox")
    grads["w_br_mem"] = _matmul(sv["o_mem"], db_mem, ta=True, name="dw_br_mem")

    dq_sb, dk_sb, dv_sb = _attn_bwd("sb", *sv["qkv_sb"], sv["o_sb"], do_sb, n_sb, HEAD_DIM, lse=sv["tot_sb"])
    dq_fx, dk_fx, dv_fx, dcrow, dccol = _attn_bwd("fox", *sv["qkv_fx"], sv["o_fx"], do_fx, n_fx, HEAD_DIM,
                                                  sv["ccol"], sv["crow"], sv["lse_fx"])
    dq_mem, dk_mem, dv_mem = _attn_bwd("mem", *sv["qkv_mem"], sv["o_mem"], do_mem, n_mem, MEM_HEAD_DIM,
                                       lse=sv["lse_mem"])
    dkv = jnp.concatenate([dk_mem, dv_mem], axis=1)
    grads["w_mem_kv"] = _matmul(mem_n, dkv, ta=True, name="dw_mem_kv")
    dmem_n = _matmul(dkv, wl["w_mem_kv"], tb=True, out_dtype=F32, name="dmem_n")

    dc = jnp.pad((dcrow.reshape(n_fx, S) + dccol.reshape(n_fx, S)).T, ((0, 0), (0, LANE - n_fx)))
    dfl, db_forget = _fox_dlogit(dc, sv["fl"])
    grads["b_forget"] = db_forget
    dproj = jnp.concatenate([dq_sb, dk_sb, dv_sb, dq_fx, dk_fx, dv_fx, dq_mem, dfl], axis=1)
    dh1, dg_pre = _mix_bwd_in(dproj, dgp, wl["w_in"], wl["w_gate"], sv["h1"], wl["mix_pre_g"], dh2)
    grads["mix_pre_g"] = dg_pre
    grads["w_in"] = _matmul(sv["u"], dproj, ta=True, name="dw_in")
    grads["w_gate"] = _matmul(sv["u"], dgp, ta=True, name="dw_gate_mix")

    dh0, g1 = _ffn_bwd(dh1, sv, wl, "1", sv["h0"])
    grads.update(g1)
    return dh0, grads, dmem_n


def kernel(x, mem, ffn1_pre_g, ffn1_post_g, ffn1_w_gate, ffn1_w_up, ffn1_w_down, mix_pre_g, mix_post_g, w_in, b_forget, mem_norm_g, w_mem_kv, w_gate, b_gate, w_br_sb, w_br_fox, w_br_mem, w_out, ffn2_pre_g, ffn2_post_g, ffn2_w_gate, ffn2_w_up, ffn2_w_down, loss_target, m_ffn1_pre_g, m_ffn1_post_g, m_ffn1_w_gate, m_ffn1_w_up, m_ffn1_w_down, m_mix_pre_g, m_mix_post_g, m_w_in, m_b_forget, m_mem_norm_g, m_w_mem_kv, m_w_gate, m_b_gate, m_w_br_sb, m_w_br_fox, m_w_br_mem, m_w_out, m_ffn2_pre_g, m_ffn2_post_g, m_ffn2_w_gate, m_ffn2_w_up, m_ffn2_w_down, v_ffn1_pre_g, v_ffn1_post_g, v_ffn1_w_gate, v_ffn1_w_up, v_ffn1_w_down, v_mix_pre_g, v_mix_post_g, v_w_in, v_b_forget, v_mem_norm_g, v_w_mem_kv, v_w_gate, v_b_gate, v_w_br_sb, v_w_br_fox, v_w_br_mem, v_w_out, v_ffn2_pre_g, v_ffn2_post_g, v_ffn2_w_gate, v_ffn2_w_up, v_ffn2_w_down):
    args = dict(locals())
    w = {n: args[n] for n in WEIGHTS}
    m = {n: args["m_" + n] for n in WEIGHTS}
    v = {n: args["v_" + n] for n in WEIGHTS}
    L = w["ffn1_pre_g"].shape[0]
    Lh = L // 2
    D = x.shape[2]
    sbw, fxw, memw = w["w_br_sb"].shape[1], w["w_br_fox"].shape[1], w["w_br_mem"].shape[1]
    n_sb, n_fx, n_mem = sbw // HEAD_DIM, fxw // HEAD_DIM, memw // MEM_HEAD_DIM
    dims = (n_sb, n_fx, n_mem, sbw, fxw, memw)
    qkv_w = 3 * sbw + 3 * fxw
    c_idx = lax.axis_index("c")
    c_arr = c_idx.reshape(1).astype(jnp.int32)
    chip_arr = (2 * lax.axis_index("x") + lax.axis_index("y")).reshape(1).astype(jnp.int32)

    shard_shapes = {n: w[n].shape[1:] for n, _ in BIG}
    layout = _row_layout(shard_shapes, Lh)
    widths = list(layout)
    locs = [jnp.stack([_pack_rows(layout[b], b, {n: w[n][hf * Lh:(hf + 1) * Lh].reshape(-1, b) for n in layout[b][0]},
                                  BF16) for hf in range(2)]) for b in widths]
    gathered = dict(zip(widths, _gather_weights(locs, [_place_own(loc, chip_arr) for loc in locs])))

    def layer_weights(l):
        hf, li = divmod(l, Lh)
        wl = {}
        for n, axis in BIG:
            a, b = shard_shapes[n]
            r0 = layout[b][1][n] + li * a
            shards = gathered[b][:, hf, r0:r0 + a]
            if n.startswith("ffn"):
                wl[n] = shards
            else:
                wl[n] = (shards.transpose(1, 0, 2).reshape(a, N_CHIPS * b) if axis == 2 else
                         shards.reshape(N_CHIPS * a, b))
        wi = wl["w_in"]
        wl["w_in"] = jnp.concatenate([wi[:, :qkv_w], wi[:, qkv_w + n_fx:], wi[:, qkv_w:qkv_w + n_fx],
                                      jnp.zeros((D, LANE - n_fx), BF16)], axis=1)
        for n in SMALL:
            if n != "mem_norm_g":
                wl[n] = w[n][l][None, :]
        wl["b_forget"] = jnp.pad(wl["b_forget"], ((0, 0), (0, LANE - n_fx)))
        return wl

    g_mem = w["mem_norm_g"][None, :]

    mem_n = _mem_norm(mem[0], g_mem)
    h, wls, saved = x[0], [], []
    for l in range(L):
        wls.append(layer_weights(l))
        h, sv = _layer_fwd(h, mem_n, wls[l], dims)
        saved.append(sv)
    dh, loss_tile = _loss_head(h, loss_target[0])
    loss = lax.psum(loss_tile[0, 0], ("x", "y", "c"))
    gl, dmem_n = [None] * L, [None] * L
    for l in reversed(range(L)):
        dh, gl[l], dmem_n[l] = _layer_bwd(dh, mem_n, wls[l], saved[l], dims)
        gi = gl[l]["w_in"]
        gl[l]["w_in"] = jnp.concatenate([gi[:, :qkv_w], gi[:, qkv_w + memw:qkv_w + memw + n_fx],
                                         gi[:, qkv_w:qkv_w + memw]], axis=1)
    grad_x = dh
    g_mem_norm = _mem_norm_bwd(mem[0], g_mem, jnp.stack(dmem_n))

    axis_of = dict(BIG)
    partials = [jnp.stack([jnp.stack([
        _pack_rows(layout[b], b, {n: jnp.concatenate([_slab(gl[hf * Lh + li][n], axis_of[n], j) for li in range(Lh)])
                                  for n in layout[b][0]}, BF16)
        for j in range(N_CHIPS)]) for hf in range(2)]) for b in widths]
    pairs = [_pair_sum(g, sib, c_arr) for g, sib in zip(partials, _pair_exchange(partials))]
    mines = [_chip_sum(p, r, chip_arr) for p, r in zip(pairs, _chip_exchange(pairs))]
    theirs = _pair_swap(mines)

    grad, delta, new_m, new_v = {}, {}, {}, {}
    for n, _ in BIG:
        k = widths.index(shard_shapes[n][1])
        grad[n], delta[n], new_m[n], new_v[n] = _adamw_reduced(
            w[n], m[n], v[n], mines[k], theirs[k], c_arr, layout[widths[k]][1][n], name="adamw_" + n)

    small_local = {n: (g_mem_norm if n == "mem_norm_g" else
                       jnp.concatenate([gl[l][n][:, :n_fx] if n == "b_forget" else gl[l][n] for l in range(L)]))
                   for n in SMALL}
    small_shapes = [small_local[n].shape for n in SMALL]
    small_sum = _unpack_flat(_all_reduce_small(_pack_flat([small_local[n] for n in SMALL], F32, row_block=16)),
                             small_shapes)
    for n, t in zip(SMALL, small_sum):
        shp = w[n].shape
        two_d = (1, shp[0]) if len(shp) == 1 else shp
        grad[n] = t.reshape(shp)
        d_, m_, v_ = _adamw(w[n].reshape(two_d), t.reshape(two_d), m[n].reshape(two_d), v[n].reshape(two_d),
                            name="adamw_" + n)
        delta[n], new_m[n], new_v[n] = d_.reshape(shp), m_.reshape(shp), v_.reshape(shp)

    return (loss, grad_x[None], *[grad[n] for n in WEIGHTS], *[delta[n] for n in WEIGHTS],
            *[new_m[n] for n in WEIGHTS], *[new_v[n] for n in WEIGHTS])
```

```python
import math

import jax
import jax.numpy as jnp
from jax import lax
from jax.experimental import pallas as pl
from jax.experimental.pallas import tpu as pltpu

F32 = jnp.float32
BF16 = jnp.bfloat16
RMS_EPS = 1e-6
HEAD_DIM = 64
MEM_HEAD_DIM = 128
LANE = 128
V7X_VMEM_LIMIT_BYTES = 56 * 1024 * 1024
FLAT_COLS = 512
FLAT_UNIT = 16 * FLAT_COLS
FLAT_ROW_BLOCK = 512
N_CHIPS = 4
N_DEV = 8
NEG = float(jnp.finfo(jnp.float32).min)

ADAM_LR = 0.001
ADAM_B1 = 0.9
ADAM_B2 = 0.999
ADAM_EPS = 1e-08
ADAM_WD = 0.01
ADAM_STEP = 10

BIG = (("ffn1_w_gate", 2), ("ffn1_w_up", 2), ("ffn1_w_down", 1), ("w_in", 2), ("w_mem_kv", 1), ("w_gate", 2),
       ("w_br_sb", 2), ("w_br_fox", 2), ("w_br_mem", 2), ("w_out", 1),
       ("ffn2_w_gate", 2), ("ffn2_w_up", 2), ("ffn2_w_down", 1))
SMALL = ("ffn1_pre_g", "ffn1_post_g", "mix_pre_g", "mix_post_g", "b_forget", "mem_norm_g", "b_gate",
         "ffn2_pre_g", "ffn2_post_g")
WEIGHTS = ("ffn1_pre_g", "ffn1_post_g", "ffn1_w_gate", "ffn1_w_up", "ffn1_w_down", "mix_pre_g", "mix_post_g", "w_in",
           "b_forget", "mem_norm_g", "w_mem_kv", "w_gate", "b_gate", "w_br_sb", "w_br_fox", "w_br_mem", "w_out",
           "ffn2_pre_g", "ffn2_post_g", "ffn2_w_gate", "ffn2_w_up", "ffn2_w_down")


def _params(**kw):
    return pltpu.CompilerParams(vmem_limit_bytes=V7X_VMEM_LIMIT_BYTES, **kw)


def _dot(a, b):
    return jnp.dot(a, b, preferred_element_type=F32)


def _dot_nt(a, b):
    return lax.dot_general(a, b, (((1,), (1,)), ((), ())), preferred_element_type=F32)


def _dot_tn(a, b):
    return lax.dot_general(a, b, (((0,), (0,)), ((), ())), preferred_element_type=F32)


def _rms(t, g):
    return t * lax.rsqrt(jnp.mean(t * t, axis=-1, keepdims=True) + RMS_EPS) * g


def _pick(dim, pref):
    if dim <= pref:
        return dim
    for cand in range(pref - pref % LANE, 0, -LANE):
        if dim % cand == 0:
            return cand
    return dim


def _rows(bm, cols):
    return pl.BlockSpec((bm, cols), lambda i: (i, 0))


def _whole(shape):
    nd = len(shape)
    return pl.BlockSpec(shape, lambda i: (0,) * nd)


def _resident(shape):
    nd = len(shape)
    return pl.BlockSpec(shape, lambda i: (0,) * nd, pipeline_mode=pl.Buffered(1))


def _split3(x):
    hi = x.astype(BF16)
    r1 = x - hi.astype(F32)
    mid = r1.astype(BF16)
    lo = (r1 - mid.astype(F32)).astype(BF16)
    return hi, mid, lo


def _cumdot(x, tri):
    hi = x.astype(BF16)
    lo = (x - hi.astype(F32)).astype(BF16)
    return _dot(hi, tri) + _dot(lo, tri)


FFN_ROWS = 512


def _slabs(bm, cols):
    return pl.BlockSpec((N_CHIPS, bm, cols), lambda i: (0, i, 0))


def _ffn_fwd_up(h, g_pre, wg, wu):
    S, D = h.shape
    Fs = wg.shape[2]
    bm = _pick(S, FFN_ROWS)

    def body(h_ref, g_ref, wg_ref, wu_ref, n_ref, gate_ref, up_ref, a_ref):
        n = _rms(h_ref[...], g_ref[...]).astype(BF16)
        n_ref[...] = n
        for j in range(N_CHIPS):
            gate = _dot(n, wg_ref[j])
            up = _dot(n, wu_ref[j])
            gate_ref[j] = gate.astype(BF16)
            up_ref[j] = up.astype(BF16)
            a_ref[j] = (gate * jax.nn.sigmoid(gate) * up).astype(BF16)

    return pl.pallas_call(
        body, name="ffn_fwd_up", grid=(S // bm,),
        in_specs=[_rows(bm, D), _whole((1, D)), _resident((N_CHIPS, D, Fs)), _resident((N_CHIPS, D, Fs))],
        out_specs=[_rows(bm, D), _slabs(bm, Fs), _slabs(bm, Fs), _slabs(bm, Fs)],
        out_shape=[jax.ShapeDtypeStruct((S, D), BF16)] + [jax.ShapeDtypeStruct((N_CHIPS, S, Fs), BF16)] * 3,
        compiler_params=_params(),
    )(h, g_pre, wg, wu)


def _ffn_fwd_down(a, wd, h, g_post):
    _, S, Fs = a.shape
    D = wd.shape[2]
    bm = _pick(S, FFN_ROWS)

    def body(a_ref, wd_ref, h_ref, g_ref, hout_ref, f_ref):
        f = _dot(a_ref[0], wd_ref[0])
        for j in range(1, N_CHIPS):
            f = f + _dot(a_ref[j], wd_ref[j])
        f_ref[...] = f
        hout_ref[...] = h_ref[...] + 0.5 * _rms(f, g_ref[...])

    return pl.pallas_call(
        body, name="ffn_fwd_down", grid=(S // bm,),
        in_specs=[_slabs(bm, Fs), _resident((N_CHIPS, Fs, D)), _rows(bm, D), _whole((1, D))],
        out_specs=[_rows(bm, D), _rows(bm, D)],
        out_shape=[jax.ShapeDtypeStruct((S, D), F32)] * 2,
        compiler_params=_params(),
    )(a, wd, h, g_post)


def _ffn_bwd_down(dh, f, g_post, wd, gate, up):
    S, D = dh.shape
    Fs = wd.shape[1]
    bm = _pick(S, FFN_ROWS)

    def body(dh_ref, f_ref, g_ref, wd_ref, gate_ref, up_ref, df_ref, dgate_ref, dup_ref, dg_ref):
        _, vjp = jax.vjp(lambda t, g: 0.5 * _rms(t, g), f_ref[...], g_ref[...])
        df, dg = vjp(dh_ref[...])

        @pl.when(pl.program_id(0) == 0)
        def _():
            dg_ref[...] = jnp.zeros_like(dg_ref)

        dg_ref[...] += dg
        dfb = df.astype(BF16)
        df_ref[...] = dfb
        for j in range(N_CHIPS):
            da = _dot_nt(dfb, wd_ref[j])
            gt = gate_ref[j].astype(F32)
            sig = jax.nn.sigmoid(gt)
            silu = gt * sig
            dup_ref[j] = (da * silu).astype(BF16)
            dgate_ref[j] = (da * up_ref[j].astype(F32) * (sig + silu * (1.0 - sig))).astype(BF16)

    return pl.pallas_call(
        body, name="ffn_bwd_down", grid=(S // bm,),
        in_specs=[_rows(bm, D), _rows(bm, D), _whole((1, D)), _resident((N_CHIPS, Fs, D)), _slabs(bm, Fs),
                  _slabs(bm, Fs)],
        out_specs=[_rows(bm, D), _slabs(bm, Fs), _slabs(bm, Fs), _whole((1, D))],
        out_shape=[jax.ShapeDtypeStruct((S, D), BF16), jax.ShapeDtypeStruct((N_CHIPS, S, Fs), BF16),
                   jax.ShapeDtypeStruct((N_CHIPS, S, Fs), BF16), jax.ShapeDtypeStruct((1, D), F32)],
        compiler_params=_params(),
    )(dh, f, g_post, wd, gate, up)


def _ffn_bwd_up(dgate, dup, wg, wu, h_in, g_pre, dh):
    _, S, Fs = dgate.shape
    D = wg.shape[1]
    bm = _pick(S, FFN_ROWS)

    def body(dgate_ref, dup_ref, wg_ref, wu_ref, h_ref, g_ref, dh_ref, dhin_ref, dg_ref):
        dn = _dot_nt(dgate_ref[0], wg_ref[0]) + _dot_nt(dup_ref[0], wu_ref[0])
        for j in range(1, N_CHIPS):
            dn = dn + _dot_nt(dgate_ref[j], wg_ref[j]) + _dot_nt(dup_ref[j], wu_ref[j])
        _, vjp = jax.vjp(_rms, h_ref[...], g_ref[...])
        dhx, dg = vjp(dn)

        @pl.when(pl.program_id(0) == 0)
        def _():
            dg_ref[...] = jnp.zeros_like(dg_ref)

        dg_ref[...] += dg
        dhin_ref[...] = dh_ref[...] + dhx

    return pl.pallas_call(
        body, name="ffn_bwd_up", grid=(S // bm,),
        in_specs=[_slabs(bm, Fs), _slabs(bm, Fs), _resident((N_CHIPS, D, Fs)), _resident((N_CHIPS, D, Fs)), _rows(bm, D),
                  _whole((1, D)), _rows(bm, D)],
        out_specs=[_rows(bm, D), _whole((1, D))],
        out_shape=[jax.ShapeDtypeStruct((S, D), F32), jax.ShapeDtypeStruct((1, D), F32)],
        compiler_params=_params(),
    )(dgate, dup, wg, wu, h_in, g_pre, dh)


def _matmul(a, b, *, ta=False, tb=False, out_dtype=BF16, name, batch=None):
    n_batch = a.shape[0] if batch == "a" else b.shape[0] if batch == "b" else 1
    a_shape = a.shape[1:] if batch == "a" else a.shape
    b_shape = b.shape[1:] if batch == "b" else b.shape
    M, K = (a_shape[1], a_shape[0]) if ta else a_shape
    N = b_shape[0] if tb else b_shape[1]
    acc_budget = 12 * 1024 * 1024
    bm, bk = _pick(M, 1536), _pick(K, 512)
    while n_batch * N * bm * 4 > acc_budget and bm % (2 * LANE) == 0:
        bm //= 2
    bn = N if n_batch * N * bm * 4 <= acc_budget else _pick(N, 1536)
    nk = K // bk

    def body(a_ref, b_ref, o_ref, acc_ref):
        kk = pl.program_id(2)

        @pl.when(kk == 0)
        def _():
            acc_ref[...] = jnp.zeros_like(acc_ref)

        dims = (((0 if ta else 1,), (1 if tb else 0,)), ((), ()))
        if batch is None:
            acc_ref[...] += lax.dot_general(a_ref[...], b_ref[...], dims, preferred_element_type=F32)
        else:
            for g in range(n_batch):
                av = a_ref[g] if batch == "a" else a_ref[...]
                bv = b_ref[g] if batch == "b" else b_ref[...]
                acc_ref[g] += lax.dot_general(av, bv, dims, preferred_element_type=F32)

        @pl.when(kk == nk - 1)
        def _():
            o_ref[...] = acc_ref[...].astype(o_ref.dtype)

    def spec(block, index, batched):
        if batched:
            return pl.BlockSpec((n_batch,) + block, lambda i, j, k: (0,) + index(i, j, k))
        return pl.BlockSpec(block, index)

    a_spec = spec((bk, bm), lambda i, j, k: (k, i), batch == "a") if ta else \
        spec((bm, bk), lambda i, j, k: (i, k), batch == "a")
    b_spec = spec((bn, bk), lambda i, j, k: (j, k), batch == "b") if tb else \
        spec((bk, bn), lambda i, j, k: (k, j), batch == "b")
    lead = (n_batch,) if batch else ()
    return pl.pallas_call(
        body, name=name, grid=(M // bm, N // bn, nk),
        in_specs=[a_spec, b_spec],
        out_specs=spec((bm, bn), lambda i, j, k: (i, j), batch is not None),
        out_shape=jax.ShapeDtypeStruct(lead + (M, N), out_dtype),
        scratch_shapes=[pltpu.VMEM(lead + (bm, bn), F32)],
        compiler_params=_params(),
    )(a, b)


def _mix_fwd_in(h, g_pre, win, wgate, b_gate, b_forget):
    S, D = h.shape
    PW = win.shape[1] - LANE
    G = wgate.shape[1]
    bm = _pick(S, 256)

    def body(h_ref, g_ref, win_ref, wgate_ref, bg_ref, bf_ref, u_ref, proj_ref, fl_ref, sg_ref):
        u = _rms(h_ref[...], g_ref[...]).astype(BF16)
        u_ref[...] = u
        proj = _dot(u, win_ref[...])
        proj_ref[...] = proj[:, :PW].astype(BF16)
        fl_ref[...] = proj[:, PW:] + bf_ref[...]
        sg_ref[...] = jax.nn.sigmoid(_dot(u, wgate_ref[...]) + bg_ref[...]).astype(BF16)

    return pl.pallas_call(
        body, name="mix_fwd_in", grid=(S // bm,),
        in_specs=[_rows(bm, D), _whole((1, D)), _whole((D, PW + LANE)), _whole((D, G)), _whole((1, G)),
                  _whole((1, LANE))],
        out_specs=[_rows(bm, D), _rows(bm, PW), _rows(bm, LANE), _rows(bm, G)],
        out_shape=[jax.ShapeDtypeStruct((S, D), BF16), jax.ShapeDtypeStruct((S, PW), BF16),
                   jax.ShapeDtypeStruct((S, LANE), F32), jax.ShapeDtypeStruct((S, G), BF16)],
        compiler_params=_params(),
    )(h, g_pre, win, wgate, b_gate, b_forget)


def _mix_fwd_out(o_sb, o_fx, o_mem, sg, w_sb, w_fx, w_mem, w_out, h, g_post):
    S, D = h.shape
    bm = _pick(S, 256)
    widths = (o_sb.shape[1], o_fx.shape[1], o_mem.shape[1])

    def body(osb_ref, ofx_ref, omem_ref, sg_ref, wsb_ref, wfx_ref, wmem_ref, wout_ref, h_ref, g_ref,
             hout_ref, z_ref, merged_ref):
        s = sg_ref[...].astype(F32)
        merged = (s[:, :D] * _dot(osb_ref[...], wsb_ref[...]) + s[:, D:2 * D] * _dot(ofx_ref[...], wfx_ref[...])
                  + s[:, 2 * D:] * _dot(omem_ref[...], wmem_ref[...]))
        mb = merged.astype(BF16)
        merged_ref[...] = mb
        z = _dot(mb, wout_ref[...])
        z_ref[...] = z
        hout_ref[...] = h_ref[...] + _rms(z, g_ref[...])

    return pl.pallas_call(
        body, name="mix_fwd_out", grid=(S // bm,),
        in_specs=[_rows(bm, widths[0]), _rows(bm, widths[1]), _rows(bm, widths[2]), _rows(bm, 3 * D),
                  _whole((widths[0], D)), _whole((widths[1], D)), _whole((widths[2], D)), _whole((D, D)),
                  _rows(bm, D), _whole((1, D))],
        out_specs=[_rows(bm, D), _rows(bm, D), _rows(bm, D)],
        out_shape=[jax.ShapeDtypeStruct((S, D), F32), jax.ShapeDtypeStruct((S, D), F32),
                   jax.ShapeDtypeStruct((S, D), BF16)],
        compiler_params=_params(),
    )(o_sb, o_fx, o_mem, sg, w_sb, w_fx, w_mem, w_out, h, g_post)


def _mix_bwd_out(dh, z, g_post, w_out, o_sb, o_fx, o_mem, w_sb, w_fx, w_mem, sg):
    S, D = dh.shape
    bm = _pick(S, 256)
    widths = (o_sb.shape[1], o_fx.shape[1], o_mem.shape[1])

    def body(dh_ref, z_ref, g_ref, wout_ref, osb_ref, ofx_ref, omem_ref, wsb_ref, wfx_ref, wmem_ref, sg_ref,
             dz_ref, dbsb_ref, dbfx_ref, dbmem_ref, dosb_ref, dofx_ref, domem_ref, dgp_ref, dbg_ref, dg_ref):
        _, vjp = jax.vjp(_rms, z_ref[...], g_ref[...])
        dz, dg = vjp(dh_ref[...])

        @pl.when(pl.program_id(0) == 0)
        def _():
            dg_ref[...] = jnp.zeros_like(dg_ref)
            dbg_ref[...] = jnp.zeros_like(dbg_ref)

        dg_ref[...] += dg
        dzb = dz.astype(BF16)
        dz_ref[...] = dzb
        dmerged = _dot_nt(dzb, wout_ref[...])
        s = sg_ref[...].astype(F32)
        branches = ((osb_ref, wsb_ref, dbsb_ref, dosb_ref), (ofx_ref, wfx_ref, dbfx_ref, dofx_ref),
                    (omem_ref, wmem_ref, dbmem_ref, domem_ref))
        for k, (o_ref, w_ref, db_ref, do_ref) in enumerate(branches):
            gs = s[:, k * D:(k + 1) * D]
            dbb = (dmerged * gs).astype(BF16)
            db_ref[...] = dbb
            do_ref[...] = _dot_nt(dbb, w_ref[...]).astype(BF16)
            dgp = dmerged * _dot(o_ref[...], w_ref[...]) * gs * (1.0 - gs)
            dgp_ref[:, k * D:(k + 1) * D] = dgp.astype(BF16)
            dbg_ref[:, k * D:(k + 1) * D] += jnp.sum(dgp, axis=0, keepdims=True)

    return pl.pallas_call(
        body, name="mix_bwd_out", grid=(S // bm,),
        in_specs=[_rows(bm, D), _rows(bm, D), _whole((1, D)), _whole((D, D)),
                  _rows(bm, widths[0]), _rows(bm, widths[1]), _rows(bm, widths[2]),
                  _whole((widths[0], D)), _whole((widths[1], D)), _whole((widths[2], D)), _rows(bm, 3 * D)],
        out_specs=[_rows(bm, D)] * 4 + [_rows(bm, widths[0]), _rows(bm, widths[1]), _rows(bm, widths[2]),
                                        _rows(bm, 3 * D), _whole((1, 3 * D)), _whole((1, D))],
        out_shape=[jax.ShapeDtypeStruct((S, D), BF16)] * 4
        + [jax.ShapeDtypeStruct((S, w), BF16) for w in widths]
        + [jax.ShapeDtypeStruct((S, 3 * D), BF16), jax.ShapeDtypeStruct((1, 3 * D), F32),
           jax.ShapeDtypeStruct((1, D), F32)],
        compiler_params=_params(),
    )(dh, z, g_post, w_out, o_sb, o_fx, o_mem, w_sb, w_fx, w_mem, sg)


def _mix_bwd_in(dproj, dgp, win, wgate, h_in, g_pre, dh):
    S, PWL = dproj.shape
    G = dgp.shape[1]
    D = h_in.shape[1]
    bm = _pick(S, 256)

    def body(dproj_ref, dgp_ref, win_ref, wgate_ref, h_ref, g_ref, dh_ref, dhin_ref, dg_ref):
        du = _dot_nt(dproj_ref[...], win_ref[...]) + _dot_nt(dgp_ref[...], wgate_ref[...])
        _, vjp = jax.vjp(_rms, h_ref[...], g_ref[...])
        dhx, dg = vjp(du)

        @pl.when(pl.program_id(0) == 0)
        def _():
            dg_ref[...] = jnp.zeros_like(dg_ref)

        dg_ref[...] += dg
        dhin_ref[...] = dh_ref[...] + dhx

    return pl.pallas_call(
        body, name="mix_bwd_in", grid=(S // bm,),
        in_specs=[_rows(bm, PWL), _rows(bm, G), _whole((D, PWL)), _whole((D, G)), _rows(bm, D), _whole((1, D)),
                  _rows(bm, D)],
        out_specs=[_rows(bm, D), _whole((1, D))],
        out_shape=[jax.ShapeDtypeStruct((S, D), F32), jax.ShapeDtypeStruct((1, D), F32)],
        compiler_params=_params(),
    )(dproj, dgp, win, wgate, h_in, g_pre, dh)


def _log_sigmoid(x):
    return jnp.minimum(x, 0.0) - jnp.log(1.0 + jnp.exp(-jnp.abs(x)))


def _fox_cumsum(fl):
    S = fl.shape[0]
    rb = _pick(S, LANE)

    def body(fl_ref, c_ref, carry_ref):
        @pl.when(pl.program_id(0) == 0)
        def _():
            carry_ref[...] = jnp.zeros_like(carry_ref)

        r = lax.broadcasted_iota(jnp.int32, (rb, rb), 0)
        cidx = lax.broadcasted_iota(jnp.int32, (rb, rb), 1)
        tri = (cidx <= r).astype(BF16)
        hi, mid, lo = _split3(_log_sigmoid(fl_ref[...]))
        c = _dot(tri, hi) + _dot(tri, mid) + _dot(tri, lo) + carry_ref[...]
        c_ref[...] = c
        carry_ref[...] = c[rb - 1:rb, :]

    return pl.pallas_call(
        body, name="fox_cumsum", grid=(S // rb,),
        in_specs=[_rows(rb, LANE)], out_specs=_rows(rb, LANE),
        out_shape=jax.ShapeDtypeStruct((S, LANE), F32),
        scratch_shapes=[pltpu.VMEM((1, LANE), F32)],
        compiler_params=_params(),
    )(fl)


def _fox_dlogit(dc, fl):
    S = fl.shape[0]
    rb = _pick(S, LANE)
    nb = S // rb

    def body(dc_ref, fl_ref, dfl_ref, dbf_ref, carry_ref):
        @pl.when(pl.program_id(0) == 0)
        def _():
            carry_ref[...] = jnp.zeros_like(carry_ref)
            dbf_ref[...] = jnp.zeros_like(dbf_ref)

        r = lax.broadcasted_iota(jnp.int32, (rb, rb), 0)
        cidx = lax.broadcasted_iota(jnp.int32, (rb, rb), 1)
        tri = (cidx >= r).astype(BF16)
        hi, mid, lo = _split3(dc_ref[...])
        rc = _dot(tri, hi) + _dot(tri, mid) + _dot(tri, lo) + carry_ref[...]
        carry_ref[...] = rc[0:1, :]
        dfl = rc * jax.nn.sigmoid(-fl_ref[...])
        dfl_ref[...] = dfl.astype(BF16)
        dbf_ref[...] += jnp.sum(dfl, axis=0, keepdims=True)

    rev = pl.BlockSpec((rb, LANE), lambda i: (nb - 1 - i, 0))
    return pl.pallas_call(
        body, name="fox_dlogit", grid=(nb,),
        in_specs=[rev, rev], out_specs=[rev, _whole((1, LANE))],
        out_shape=[jax.ShapeDtypeStruct((S, LANE), BF16), jax.ShapeDtypeStruct((1, LANE), F32)],
        scratch_shapes=[pltpu.VMEM((1, LANE), F32)],
        compiler_params=_params(),
    )(dc, fl)


def _attn_blocks(kind, S, Sk, backward=False):
    tq = _pick(S, 1024 if backward else 2048)
    tc = LANE if kind == "sb" else _pick(Sk, 256)
    return tq, tc


def _is_power_of_two(x):
    return math.frexp(x)[0] == 0.5


def _sb_logs(z):
    ln = -jnp.maximum(z, 0.0) - jnp.log(1.0 + jnp.exp(-jnp.abs(z)))
    return ln + z, ln


def _head_lanes(pack, dh):
    lane = lax.broadcasted_iota(jnp.int32, (1, LANE), 1)
    return [(lane >= hh * dh) & (lane < (hh + 1) * dh) for hh in range(pack)]


def _by_head(sel, parts):
    out = parts[0]
    for hh in range(1, len(parts)):
        out = jnp.where(sel[hh], parts[hh], out)
    return out


def _only_head(sel, hh, x):
    return x if len(sel) == 1 else jnp.where(sel[hh], x, jnp.zeros_like(x))


def _tail(x, r0):
    return x if not r0 else x[r0:]


def _put_tail(x, tail, r0):
    return tail if not r0 else jnp.concatenate([x[:r0], tail], axis=0)


def _add_tail(x, tail, r0):
    return x + tail if not r0 else jnp.concatenate([x[:r0], x[r0:] + tail], axis=0)


def _q_cols(tq, first):
    return pl.BlockSpec((tq, LANE), lambda g, i: (i, first // LANE + g))


def _k_cols(rows, first):
    return pl.BlockSpec((rows, LANE), lambda g, i: (0, first // LANE + g))


def _attn_fwd(kind, q, k, v, n_heads, dh, ccol=None, crow=None):
    (qa, q0), (ka, k0), (va, v0) = q, k, v
    S, Sk = qa.shape[0], ka.shape[0]
    pack = LANE // dh
    tq, tc = _attn_blocks(kind, S, Sk)
    scale = dh ** -0.5
    fold = _is_power_of_two(scale)
    causal = kind != "mem"
    n_diag = tq // tc if causal else 0
    unroll = 2 if causal else 1
    assert n_diag % unroll == 0 and (Sk // tc) % unroll == 0

    def body(*refs):
        if kind == "fox":
            q_ref, k_ref, v_ref, cc_ref, cr_ref, o_ref, lse_ref = refs
        else:
            q_ref, k_ref, v_ref, o_ref, lse_ref = refs
        i = pl.program_id(1)
        n_full = (i * tq) // tc if causal else Sk // tc
        qpos = i * tq + lax.broadcasted_iota(jnp.int32, (tq, tc), 0)
        kio = lax.broadcasted_iota(jnp.int32, (tq, tc), 1)
        heads = range(pack)
        sel = _head_lanes(pack, dh)
        q2 = q_ref[...] * scale if fold else q_ref[...]
        qs = [_only_head(sel, hh, q2) for hh in heads]

        def kv(jc):
            off = pl.multiple_of(jc * tc, tc)
            return off, k_ref[pl.ds(off, tc), :], v_ref[pl.ds(off, tc), :]

        if kind == "sb":
            tri = (lax.broadcasted_iota(jnp.int32, (tc, tc), 0) > lax.broadcasted_iota(jnp.int32, (tc, tc), 1)
                   ).astype(BF16)

            def chunk(jc, r0, runs, acc):
                off, k2, v2 = kv(jc)
                new_runs, pv = [], []
                for hh in heads:
                    lb, ln = _sb_logs(_dot_nt(_tail(qs[hh], r0), k2))
                    if r0 is not None:
                        mask = (off + _tail(kio, r0)) < _tail(qpos, r0)
                        ln = jnp.where(mask, ln, 0.0)
                    w = jnp.exp(lb + _cumdot(ln, tri) + _tail(runs[hh], r0))
                    if r0 is not None:
                        w = jnp.where(mask, w, 0.0)
                    pv.append(_dot(w.astype(BF16), v2))
                    new_runs.append(_add_tail(runs[hh], jnp.sum(ln, axis=1, keepdims=True), r0))
                return tuple(new_runs), _add_tail(acc, _by_head(sel, pv), r0)

            state = (tuple(jnp.zeros((tq, 1), F32) for _ in heads), jnp.zeros((tq, LANE), F32))
            for d in range(n_diag - 1, -1, -1):
                state = chunk(n_full + d, d * tc, *state)

            def trip(t, st):
                for u in range(unroll):
                    st = chunk(n_full - 1 - unroll * t - u, None, *st)
                return st

            runs, acc = lax.fori_loop(0, n_full // unroll, trip, state)
            o_ref[...] = acc.astype(o_ref.dtype)
            for hh in heads:
                lse_ref[hh] = runs[hh]
        else:
            def chunk(jc, r0, ms, ls, acc):
                off, k2, v2 = kv(jc)
                new_ms, new_ls, alphas, pv = [], [], [], []
                for hh in heads:
                    z = _dot_nt(_tail(qs[hh], r0), k2)
                    if not fold:
                        z = z * scale
                    if kind == "fox":
                        z = z + _tail(cc_ref[hh], r0) - cr_ref[hh, pl.ds(jc, 1), :]
                    if r0 is not None:
                        z = jnp.where((off + _tail(kio, r0)) <= _tail(qpos, r0), z, NEG)
                    m_old, l_old = _tail(ms[hh], r0), _tail(ls[hh], r0)
                    m_new = jnp.maximum(m_old, jnp.max(z, axis=1, keepdims=True))
                    alpha = jnp.exp(m_old - m_new)
                    p = jnp.exp(z - m_new)
                    new_ms.append(_put_tail(ms[hh], m_new, r0))
                    new_ls.append(_put_tail(ls[hh], alpha * l_old + jnp.sum(p, axis=1, keepdims=True), r0))
                    alphas.append(alpha)
                    pv.append(_dot(p.astype(BF16), v2))
                acc_new = _by_head(sel, alphas) * _tail(acc, r0) + _by_head(sel, pv)
                return tuple(new_ms), tuple(new_ls), _put_tail(acc, acc_new, r0)

            state = (tuple(jnp.full((tq, 1), NEG, F32) for _ in heads), tuple(jnp.zeros((tq, 1), F32) for _ in heads),
                     jnp.zeros((tq, LANE), F32))

            def trip(t, st):
                for u in range(unroll):
                    st = chunk(unroll * t + u, None, *st)
                return st

            state = lax.fori_loop(0, n_full // unroll, trip, state)
            for d in range(n_diag):
                state = chunk(n_full + d, d * tc, *state)
            ms, ls, acc = state
            o_ref[...] = (acc / _by_head(sel, ls)).astype(o_ref.dtype)
            for hh in heads:
                lse_ref[hh] = ms[hh] + jnp.log(ls[hh])

    colspec = pl.BlockSpec((pack, tq, 1), lambda g, i: (g, i, 0))
    in_specs, args = [_q_cols(tq, q0), _k_cols(Sk, k0), _k_cols(Sk, v0)], [qa, ka, va]
    if kind == "fox":
        in_specs += [colspec, pl.BlockSpec((pack, Sk // tc, tc), lambda g, i: (g, 0, 0))]
        args += [ccol, crow]
    return pl.pallas_call(
        body, name="attn_fwd_" + kind, grid=(n_heads // pack, S // tq),
        in_specs=in_specs, out_specs=[_q_cols(tq, 0), colspec],
        out_shape=[jax.ShapeDtypeStruct((S, n_heads * dh), BF16), jax.ShapeDtypeStruct((n_heads, S, 1), F32)],
        compiler_params=_params(),
    )(*args)


def _attn_bwd(kind, q, k, v, o, do, n_heads, dh, ccol=None, crow=None, lse=None):
    (qa, q0), (ka, k0), (va, v0) = q, k, v
    S, Sk = qa.shape[0], ka.shape[0]
    pack = LANE // dh
    tq, tc = _attn_blocks(kind, S, Sk, backward=True)
    scale = dh ** -0.5
    fold = _is_power_of_two(scale)
    causal = kind != "mem"
    n_diag = tq // tc if causal else 0
    unroll = 2 if kind == "sb" else 1
    assert n_diag % unroll == 0 and (Sk // tc) % unroll == 0
    nq = S // tq

    def body(*refs):
        if kind == "fox":
            (q_ref, k_ref, v_ref, o_ref, do_ref, cc_ref, cr_ref, lse_ref,
             dq_ref, dk_ref, dv_ref, dc_ref, dcc_ref, dk_acc, dv_acc, dc_acc) = refs
        else:
            q_ref, k_ref, v_ref, o_ref, do_ref, lse_ref, dq_ref, dk_ref, dv_ref, dk_acc, dv_acc = refs
        i = pl.program_id(1)

        @pl.when(i == 0)
        def _():
            dk_acc[...] = jnp.zeros_like(dk_acc)
            dv_acc[...] = jnp.zeros_like(dv_acc)
            if kind == "fox":
                dc_acc[...] = jnp.zeros_like(dc_acc)

        n_full = (i * tq) // tc if causal else Sk // tc
        qpos = i * tq + lax.broadcasted_iota(jnp.int32, (tq, tc), 0)
        kio = lax.broadcasted_iota(jnp.int32, (tq, tc), 1)
        heads = range(pack)
        sel = _head_lanes(pack, dh)
        q2 = q_ref[...] * scale if fold else q_ref[...]
        do2 = do_ref[...]
        qs = [_only_head(sel, hh, q2) for hh in heads]
        dos = [_only_head(sel, hh, do2) for hh in heads]

        def kv(jc):
            off = pl.multiple_of(jc * tc, tc)
            return off, k_ref[pl.ds(off, tc), :], v_ref[pl.ds(off, tc), :]

        def accumulate(off, k2, dzb, wb, dq, r0):
            q2t, do2t = _tail(q2, r0), _tail(do2, r0)
            dk_acc[pl.ds(off, tc), :] += _by_head(sel, [_dot_tn(dzb[hh], q2t) for hh in heads])
            dv_acc[pl.ds(off, tc), :] += _by_head(sel, [_dot_tn(wb[hh], do2t) for hh in heads])
            return _add_tail(dq, _by_head(sel, [_dot(dzb[hh], k2) for hh in heads]), r0)

        if kind == "sb":
            r = lax.broadcasted_iota(jnp.int32, (tc, tc), 0)
            cidx = lax.broadcasted_iota(jnp.int32, (tc, tc), 1)
            tri_inc = (r <= cidx).astype(BF16)
            tri_exc = (r < cidx).astype(BF16)

            def chunk(jc, r0, pres, pres_e, dq):
                off, k2, v2 = kv(jc)
                new_pres, new_pres_e, dzb, wb = [], [], [], []
                for hh in heads:
                    lb, ln = _sb_logs(_dot_nt(_tail(qs[hh], r0), k2))
                    if r0 is not None:
                        mask = (off + _tail(kio, r0)) < _tail(qpos, r0)
                        ln = jnp.where(mask, ln, 0.0)
                    w = jnp.exp(lb + (_tail(lse_ref[hh], r0) - _tail(pres[hh], r0) - _cumdot(ln, tri_inc)))
                    if r0 is not None:
                        w = jnp.where(mask, w, 0.0)
                    e = w * _dot_nt(_tail(dos[hh], r0), v2)
                    beta = jnp.exp(lb)
                    dz = e * (1.0 - beta) - beta * (_tail(pres_e[hh], r0) + _cumdot(e, tri_exc))
                    if r0 is not None:
                        dz = jnp.where(mask, dz, 0.0)
                    dzb.append(dz.astype(BF16))
                    wb.append(w.astype(BF16))
                    new_pres.append(_add_tail(pres[hh], jnp.sum(ln, axis=1, keepdims=True), r0))
                    new_pres_e.append(_add_tail(pres_e[hh], jnp.sum(e, axis=1, keepdims=True), r0))
                return tuple(new_pres), tuple(new_pres_e), accumulate(off, k2, dzb, wb, dq, r0)

            state = (tuple(jnp.zeros((tq, 1), F32) for _ in heads), tuple(jnp.zeros((tq, 1), F32) for _ in heads),
                     jnp.zeros((tq, LANE), F32))
        else:
            prod = o_ref[...].astype(F32) * do2.astype(F32)
            dsum = [jnp.sum(_only_head(sel, hh, prod), axis=1, keepdims=True) for hh in heads]

            def chunk(jc, r0, rowsums, dq):
                off, k2, v2 = kv(jc)
                new_rowsums, dsb, pb = [], [], []
                for hh in heads:
                    z = _dot_nt(_tail(qs[hh], r0), k2)
                    if not fold:
                        z = z * scale
                    if kind == "fox":
                        z = z + _tail(cc_ref[hh], r0) - cr_ref[hh, pl.ds(jc, 1), :]
                    if r0 is not None:
                        z = jnp.where((off + _tail(kio, r0)) <= _tail(qpos, r0), z, NEG)
                    p = jnp.exp(z - _tail(lse_ref[hh], r0))
                    ds = p * (_dot_nt(_tail(dos[hh], r0), v2) - _tail(dsum[hh], r0))
                    dsb.append(ds.astype(BF16))
                    pb.append(p.astype(BF16))
                    if kind == "fox":
                        dc_acc[hh, pl.ds(jc, 1), :] -= jnp.sum(ds, axis=0, keepdims=True)
                        new_rowsums.append(_add_tail(rowsums[hh], jnp.sum(ds, axis=1, keepdims=True), r0))
                    else:
                        new_rowsums.append(rowsums[hh])
                return tuple(new_rowsums), accumulate(off, k2, dsb, pb, dq, r0)

            state = (tuple(jnp.zeros((tq, 1), F32) for _ in heads), jnp.zeros((tq, LANE), F32))

        def trip(t, st):
            for u in range(unroll):
                st = chunk(unroll * t + u, None, *st)
            return st

        state = lax.fori_loop(0, n_full // unroll, trip, state)
        for d in range(n_diag):
            state = chunk(n_full + d, d * tc, *state)
        dq_ref[...] = (state[-1] * scale).astype(dq_ref.dtype)
        if kind == "fox":
            for hh in heads:
                dcc_ref[hh] = state[0][hh]

        @pl.when(i == nq - 1)
        def _():
            dk = dk_acc[...] if fold else dk_acc[...] * scale
            dk_ref[...] = dk.astype(dk_ref.dtype)
            dv_ref[...] = dv_acc[...].astype(dv_ref.dtype)
            if kind == "fox":
                dc_ref[...] = dc_acc[...]

    colspec = pl.BlockSpec((pack, tq, 1), lambda g, i: (g, i, 0))
    rowspec = pl.BlockSpec((pack, Sk // tc, tc), lambda g, i: (g, 0, 0))
    in_specs = [_q_cols(tq, q0), _k_cols(Sk, k0), _k_cols(Sk, v0), _q_cols(tq, 0), _q_cols(tq, 0)]
    args = [qa, ka, va, o, do]
    if kind == "fox":
        in_specs += [colspec, rowspec]
        args += [ccol, crow]
    in_specs += [colspec]
    args += [lse]
    width = n_heads * dh
    out_specs = [_q_cols(tq, 0), _k_cols(Sk, 0), _k_cols(Sk, 0)]
    out_shape = [jax.ShapeDtypeStruct((S, width), BF16), jax.ShapeDtypeStruct((Sk, width), BF16),
                 jax.ShapeDtypeStruct((Sk, width), BF16)]
    scratch = [pltpu.VMEM((Sk, LANE), F32), pltpu.VMEM((Sk, LANE), F32)]
    if kind == "fox":
        out_specs += [rowspec, colspec]
        out_shape += [jax.ShapeDtypeStruct((n_heads, Sk // tc, tc), F32), jax.ShapeDtypeStruct((n_heads, S, 1), F32)]
        scratch.append(pltpu.VMEM((pack, Sk // tc, tc), F32))
    return pl.pallas_call(
        body, name="attn_bwd_" + kind, grid=(n_heads // pack, nq),
        in_specs=in_specs, out_specs=out_specs, out_shape=out_shape, scratch_shapes=scratch,
        compiler_params=_params(),
    )(*args)


def _mem_norm(mem, g):
    M, D = mem.shape

    def body(mem_ref, g_ref, out_ref):
        out_ref[...] = _rms(mem_ref[...], g_ref[...]).astype(BF16)

    return pl.pallas_call(
        body, name="mem_norm", grid=(1,),
        in_specs=[_whole((M, D)), _whole((1, D))], out_specs=_whole((M, D)),
        out_shape=jax.ShapeDtypeStruct((M, D), BF16), compiler_params=_params(),
    )(mem, g)


def _mem_norm_bwd(mem, g, dmem_n):
    M, D = mem.shape
    L = dmem_n.shape[0]

    def body(mem_ref, g_ref, d_ref, dg_ref):
        d = d_ref[0]
        for l in range(1, L):
            d = d + d_ref[l]
        _, vjp = jax.vjp(_rms, mem_ref[...], g_ref[...])
        dg_ref[...] = vjp(d)[1]

    return pl.pallas_call(
        body, name="mem_norm_bwd", grid=(1,),
        in_specs=[_whole((M, D)), _whole((1, D)), _whole((L, M, D))], out_specs=_whole((1, D)),
        out_shape=jax.ShapeDtypeStruct((1, D), F32), compiler_params=_params(),
    )(mem, g, dmem_n)


def _loss_head(h, target):
    S, D = h.shape
    bm = _pick(S, 512)

    def body(h_ref, t_ref, dh_ref, loss_ref):
        err = h_ref[...] - t_ref[...]
        dh_ref[...] = err * (1.0 / D)

        @pl.when(pl.program_id(0) == 0)
        def _():
            loss_ref[...] = jnp.zeros_like(loss_ref)

        loss_ref[...] += 0.5 * jnp.sum(jnp.mean(err * err, axis=-1, keepdims=True), axis=0, keepdims=True)

    return pl.pallas_call(
        body, name="loss_head", grid=(S // bm,),
        in_specs=[_rows(bm, D), _rows(bm, D)], out_specs=[_rows(bm, D), _whole((8, LANE))],
        out_shape=[jax.ShapeDtypeStruct((S, D), F32), jax.ShapeDtypeStruct((8, LANE), F32)],
        compiler_params=_params(),
    )(h, target)


def _adamw(w, g, m, v, name):
    R, C = w.shape
    rb = R if R * C * 4 <= (1 << 20) else _pick(R, 256)
    if R % rb:
        rb = R
    c1 = 1.0 - ADAM_B1 ** ADAM_STEP
    c2 = 1.0 - ADAM_B2 ** ADAM_STEP

    def body(w_ref, g_ref, m_ref, v_ref, d_ref, mo_ref, vo_ref):
        gv = g_ref[...]
        mn = ADAM_B1 * m_ref[...] + (1.0 - ADAM_B1) * gv
        vn = ADAM_B2 * v_ref[...] + (1.0 - ADAM_B2) * (gv * gv)
        mo_ref[...] = mn
        vo_ref[...] = vn
        d_ref[...] = -ADAM_LR * ((mn / c1) / (jnp.sqrt(vn / c2) + ADAM_EPS) + ADAM_WD * w_ref[...])

    return pl.pallas_call(
        body, name=name, grid=(R // rb,),
        in_specs=[_rows(rb, C)] * 4, out_specs=[_rows(rb, C)] * 3,
        out_shape=[jax.ShapeDtypeStruct((R, C), F32)] * 3, compiler_params=_params(),
    )(w, g, m, v)


def _adamw_reduced(w, m, v, mine, theirs, c_idx, first_row, name):
    L, a, b = w.shape
    Lh = L // 2
    rb = _shard_row_block(a)
    nb = a // rb
    assert first_row % rb == 0
    c1 = 1.0 - ADAM_B1 ** ADAM_STEP
    c2 = 1.0 - ADAM_B2 ** ADAM_STEP

    def own(i, c_ref):
        return (i, 0)

    def reduced(i, c_ref):
        return (first_row // rb + ((i // nb) % Lh) * nb + i % nb, 0)

    def body(c_ref, w_ref, m_ref, v_ref, mine_ref, theirs_ref, g_ref, d_ref, mo_ref, vo_ref):
        half = (pl.program_id(0) // nb) // Lh
        gv = jnp.where(c_ref[0] == half, mine_ref[...], theirs_ref[...])
        g_ref[...] = gv
        mn = ADAM_B1 * m_ref[...] + (1.0 - ADAM_B1) * gv
        vn = ADAM_B2 * v_ref[...] + (1.0 - ADAM_B2) * (gv * gv)
        mo_ref[...] = mn
        vo_ref[...] = vn
        d_ref[...] = -ADAM_LR * ((mn / c1) / (jnp.sqrt(vn / c2) + ADAM_EPS) + ADAM_WD * w_ref[...])

    outs = pl.pallas_call(
        body, name=name,
        grid_spec=pltpu.PrefetchScalarGridSpec(
            num_scalar_prefetch=1, grid=(L * nb,),
            in_specs=[pl.BlockSpec((rb, b), own)] * 3 + [pl.BlockSpec((rb, b), reduced)] * 2,
            out_specs=[pl.BlockSpec((rb, b), own)] * 4),
        out_shape=[jax.ShapeDtypeStruct((L * a, b), F32)] * 4, compiler_params=_params(),
    )(c_idx, w.reshape(L * a, b), m.reshape(L * a, b), v.reshape(L * a, b), mine, theirs)
    return [t.reshape(L, a, b) for t in outs]


ANY = pl.BlockSpec(memory_space=pl.ANY)
MESH = pl.DeviceIdType.MESH


def _place():
    x, y, c = lax.axis_index("x"), lax.axis_index("y"), lax.axis_index("c")
    others = [(1 - x, y), (x, 1 - y), (1 - x, 1 - y)]
    return x, y, c, others


def _place_own(loc, chip_idx):
    _, R, C = loc.shape
    rb = _pick(R, 2 * FLAT_ROW_BLOCK)

    def body(chip_ref, loc_ref, out_ref):
        out_ref[...] = loc_ref[...]

    return pl.pallas_call(
        body, name="place_own",
        grid_spec=pltpu.PrefetchScalarGridSpec(
            num_scalar_prefetch=1, grid=(2, R // rb),
            in_specs=[pl.BlockSpec((None, rb, C), lambda hf, i, chip_ref: (hf, i, 0))],
            out_specs=pl.BlockSpec((None, None, rb, C), lambda hf, i, chip_ref: (chip_ref[0], hf, i, 0))),
        out_shape=jax.ShapeDtypeStruct((N_CHIPS, 2, R, C), loc.dtype), compiler_params=_params(),
    )(chip_idx, loc)


def _gather_weights(locs, owns):
    n = len(locs)

    def body(*refs):
        loc_refs, out_refs, (send_sems, recv_sems) = refs[:n], refs[2 * n:3 * n], refs[3 * n:]
        x, y, c, others = _place()
        me = 2 * x + y
        sibling = (x, y, 1 - c)

        def copy(a, k, src, dst, to):
            return pltpu.make_async_remote_copy(src_ref=src, dst_ref=dst, send_sem=send_sems.at[a, k],
                                                recv_sem=recv_sems.at[a, k], device_id=to, device_id_type=MESH)

        first = [copy(a, j, loc_refs[a].at[c], out_refs[a].at[me, c], (ox, oy, c))
                 for j, (ox, oy) in enumerate(others) for a in range(n)]
        for cp in first:
            cp.start()
        passed = []
        for j, (ox, oy) in enumerate(others):
            for a in range(n):
                landed = out_refs[a].at[2 * ox + oy, c]
                copy(a, j, loc_refs[a].at[c], landed, sibling).wait_recv()
                cp = copy(a, 3 + j, landed, landed, sibling)
                cp.start()
                passed.append(cp)
        for j, (ox, oy) in enumerate(others):
            for a in range(n):
                copy(a, 3 + j, loc_refs[a].at[c], out_refs[a].at[2 * ox + oy, 1 - c], sibling).wait_recv()
        for cp in first + passed:
            cp.wait_send()

    return pl.pallas_call(
        body, name="gather_weights", in_specs=[ANY] * (2 * n), out_specs=[ANY] * n,
        out_shape=[jax.ShapeDtypeStruct(own.shape, own.dtype) for own in owns],
        input_output_aliases={n + a: a for a in range(n)},
        scratch_shapes=[pltpu.SemaphoreType.DMA((n, 6)), pltpu.SemaphoreType.DMA((n, 6))],
    )(*locs, *owns)


def _pair_exchange(gs):
    n = len(gs)

    def body(*refs):
        g_refs, out_refs, (send_sems, recv_sems) = refs[:n], refs[n:2 * n], refs[2 * n:]
        x, y, c, _ = _place()
        copies = [pltpu.make_async_remote_copy(src_ref=g_refs[a].at[1 - c], dst_ref=out_refs[a],
                                               send_sem=send_sems.at[a], recv_sem=recv_sems.at[a],
                                               device_id=(x, y, 1 - c), device_id_type=MESH) for a in range(n)]
        for cp in copies:
            cp.start()
        for cp in copies:
            cp.wait()

    return pl.pallas_call(
        body, name="pair_exchange", in_specs=[ANY] * n, out_specs=[ANY] * n,
        out_shape=[jax.ShapeDtypeStruct(g.shape[1:], g.dtype) for g in gs],
        scratch_shapes=[pltpu.SemaphoreType.DMA((n,)), pltpu.SemaphoreType.DMA((n,))],
    )(*gs)


def _pair_sum(g, sib, c_idx):
    _, _, R, C = g.shape
    rb = _pick(R, 512)

    def body(c_ref, g_ref, s_ref, o_ref):
        o_ref[...] = (g_ref[...].astype(F32) + s_ref[...].astype(F32)).astype(o_ref.dtype)

    return pl.pallas_call(
        body, name="pair_sum",
        grid_spec=pltpu.PrefetchScalarGridSpec(
            num_scalar_prefetch=1, grid=(N_CHIPS, R // rb),
            in_specs=[pl.BlockSpec((None, None, rb, C), lambda j, i, c_ref: (c_ref[0], j, i, 0)),
                      pl.BlockSpec((None, rb, C), lambda j, i, c_ref: (j, i, 0))],
            out_specs=pl.BlockSpec((None, rb, C), lambda j, i, c_ref: (j, i, 0))),
        out_shape=jax.ShapeDtypeStruct((N_CHIPS, R, C), g.dtype), compiler_params=_params(),
    )(c_idx, g, sib)


def _chip_exchange(ps):
    n = len(ps)

    def body(*refs):
        p_refs, out_refs, (send_sems, recv_sems) = refs[:n], refs[n:2 * n], refs[2 * n:]
        x, y, c, others = _place()
        copies = []
        for j, (ox, oy) in enumerate(others):
            for a in range(n):
                cp = pltpu.make_async_remote_copy(src_ref=p_refs[a].at[2 * ox + oy], dst_ref=out_refs[a].at[j],
                                                  send_sem=send_sems.at[a, j], recv_sem=recv_sems.at[a, j],
                                                  device_id=(ox, oy, c), device_id_type=MESH)
                cp.start()
                copies.append(cp)
        for cp in copies:
            cp.wait()

    return pl.pallas_call(
        body, name="chip_exchange", in_specs=[ANY] * n, out_specs=[ANY] * n,
        out_shape=[jax.ShapeDtypeStruct((N_CHIPS - 1,) + p.shape[1:], p.dtype) for p in ps],
        scratch_shapes=[pltpu.SemaphoreType.DMA((n, 3)), pltpu.SemaphoreType.DMA((n, 3))],
    )(*ps)


def _chip_sum(p, r, chip_idx):
    _, R, C = r.shape
    rb = _pick(R, 512)

    def body(chip_ref, p_ref, r_ref, o_ref):
        acc = p_ref[...].astype(F32)
        for j in range(N_CHIPS - 1):
            acc = acc + r_ref[j].astype(F32)
        o_ref[...] = acc

    return pl.pallas_call(
        body, name="chip_sum",
        grid_spec=pltpu.PrefetchScalarGridSpec(
            num_scalar_prefetch=1, grid=(R // rb,),
            in_specs=[pl.BlockSpec((None, rb, C), lambda i, chip_ref: (chip_ref[0], i, 0)),
                      pl.BlockSpec((N_CHIPS - 1, rb, C), lambda i, chip_ref: (0, i, 0))],
            out_specs=pl.BlockSpec((rb, C), lambda i, chip_ref: (i, 0))),
        out_shape=jax.ShapeDtypeStruct((R, C), F32), compiler_params=_params(),
    )(chip_idx, p, r)


def _pair_swap(rhs):
    n = len(rhs)

    def body(*refs):
        rh_refs, out_refs, (send_sems, recv_sems) = refs[:n], refs[n:2 * n], refs[2 * n:]
        x, y, c, _ = _place()
        copies = [pltpu.make_async_remote_copy(src_ref=rh_refs[a], dst_ref=out_refs[a], send_sem=send_sems.at[a],
                                               recv_sem=recv_sems.at[a], device_id=(x, y, 1 - c),
                                               device_id_type=MESH) for a in range(n)]
        for cp in copies:
            cp.start()
        for cp in copies:
            cp.wait()

    return pl.pallas_call(
        body, name="pair_swap", in_specs=[ANY] * n, out_specs=[ANY] * n,
        out_shape=[jax.ShapeDtypeStruct(rh.shape, rh.dtype) for rh in rhs],
        scratch_shapes=[pltpu.SemaphoreType.DMA((n,)), pltpu.SemaphoreType.DMA((n,))],
    )(*rhs)


def _all_reduce_small(s):
    R, C = s.shape

    def body(s_ref, o_ref, buf, send_sems, recv_sems):
        x, y, c, _ = _place()
        me = 4 * x + 2 * y + c
        sends = []
        for k in range(1, N_DEV):
            fx, fy, fc = (k >> 2) & 1, (k >> 1) & 1, k & 1
            to = (x ^ fx, y ^ fy, c ^ fc)
            cp = pltpu.make_async_remote_copy(src_ref=s_ref, dst_ref=buf.at[me], send_sem=send_sems.at[k - 1],
                                              recv_sem=recv_sems.at[k - 1], device_id=to, device_id_type=MESH)
            cp.start()
            sends.append(cp)
        buf[me] = s_ref[...]
        for k in range(1, N_DEV):
            fx, fy, fc = (k >> 2) & 1, (k >> 1) & 1, k & 1
            frm = 4 * (x ^ fx) + 2 * (y ^ fy) + (c ^ fc)
            pltpu.make_async_remote_copy(src_ref=s_ref, dst_ref=buf.at[frm], send_sem=send_sems.at[k - 1],
                                         recv_sem=recv_sems.at[k - 1], device_id=(x, y, c),
                                         device_id_type=MESH).wait_recv()
        acc = buf[0]
        for d in range(1, N_DEV):
            acc = acc + buf[d]
        o_ref[...] = acc
        for cp in sends:
            cp.wait_send()

    vm = pl.BlockSpec(memory_space=pltpu.VMEM)
    return pl.pallas_call(
        body, name="all_reduce_small", in_specs=[vm], out_specs=vm,
        out_shape=jax.ShapeDtypeStruct((R, C), F32),
        scratch_shapes=[pltpu.VMEM((N_DEV, R, C), F32), pltpu.SemaphoreType.DMA((N_DEV - 1,)),
                        pltpu.SemaphoreType.DMA((N_DEV - 1,))],
    )(s)


def _padded(n):
    return -(-n // FLAT_UNIT) * FLAT_UNIT


def _pack_flat(pieces, dtype, row_block=FLAT_ROW_BLOCK):
    flat = []
    for p in pieces:
        p = p.reshape(-1).astype(dtype)
        flat.append(jnp.pad(p, (0, _padded(p.size) - p.size)))
    total = sum(p.size for p in flat)
    flat.append(jnp.zeros((-total) % (row_block * FLAT_COLS), dtype))
    return jnp.concatenate(flat).reshape(-1, FLAT_COLS)


def _unpack_flat(flat, shapes):
    lead = flat.shape[:-2]
    flat = flat.reshape(lead + (-1,))
    out, off = [], 0
    for shp in shapes:
        n = math.prod(shp)
        out.append(flat[..., off:off + n].reshape(lead + tuple(shp)))
        off += _padded(n)
    return out


def _shard_row_block(a):
    for rb in range(min(a, 512) // 16 * 16, 0, -16):
        if a % rb == 0:
            return rb
    return a


def _row_layout(shapes, n_layers):
    groups = {}
    for name, (a, b) in shapes.items():
        names, first, rows = groups.get(b, ((), {}, 0))
        rb = _shard_row_block(a)
        start = -(-rows // rb) * rb
        groups[b] = (names + (name,), {**first, name: start}, start + n_layers * a)
    return {b: (names, first, -(-rows // FLAT_ROW_BLOCK) * FLAT_ROW_BLOCK) for b, (names, first, rows) in groups.items()}


def _pack_rows(group, width, pieces, dtype):
    names, first, rows = group
    parts, at = [], 0
    for name in names:
        if first[name] > at:
            parts.append(jnp.zeros((first[name] - at, width), dtype))
        parts.append(pieces[name].astype(dtype))
        at = first[name] + pieces[name].shape[0]
    if rows > at:
        parts.append(jnp.zeros((rows - at, width), dtype))
    return jnp.concatenate(parts, axis=0)


def _slab(t, axis, j):
    if t.ndim == 3:
        return t[j]
    n = t.shape[axis - 1] // N_CHIPS
    return lax.slice_in_dim(t, j * n, (j + 1) * n, axis=axis - 1)


def _layer_fwd(h0, mem_n, wl, dims):
    n_sb, n_fx, n_mem, sbw, fxw, memw = dims
    n1, gate1, up1, a1 = _ffn_fwd_up(h0, wl["ffn1_pre_g"], wl["ffn1_w_gate"], wl["ffn1_w_up"])
    h1, f1 = _ffn_fwd_down(a1, wl["ffn1_w_down"], h0, wl["ffn1_post_g"])

    u, proj, fl, sg = _mix_fwd_in(h1, wl["mix_pre_g"], wl["w_in"], wl["w_gate"], wl["b_gate"], wl["b_forget"])
    c = _fox_cumsum(fl)
    S = h0.shape[0]
    tc = _attn_blocks("fox", S, S)[1]
    ct = c[:, :n_fx].T
    ccol, crow = ct.reshape(n_fx, S, 1), ct.reshape(n_fx, S // tc, tc)
    qkv_sb = [(proj, k * sbw) for k in range(3)]
    qkv_fx = [(proj, 3 * sbw + k * fxw) for k in range(3)]
    kv = _matmul(mem_n, wl["w_mem_kv"], out_dtype=BF16, name="mem_kv")
    qkv_mem = [(proj, 3 * sbw + 3 * fxw), (kv, 0), (kv, memw)]
    o_sb, tot_sb = _attn_fwd("sb", *qkv_sb, n_sb, HEAD_DIM)
    o_fx, lse_fx = _attn_fwd("fox", *qkv_fx, n_fx, HEAD_DIM, ccol, crow)
    o_mem, lse_mem = _attn_fwd("mem", *qkv_mem, n_mem, MEM_HEAD_DIM)
    h2, zmix, merged = _mix_fwd_out(o_sb, o_fx, o_mem, sg, wl["w_br_sb"], wl["w_br_fox"], wl["w_br_mem"],
                                    wl["w_out"], h1, wl["mix_post_g"])

    n2, gate2, up2, a2 = _ffn_fwd_up(h2, wl["ffn2_pre_g"], wl["ffn2_w_gate"], wl["ffn2_w_up"])
    h3, f2 = _ffn_fwd_down(a2, wl["ffn2_w_down"], h2, wl["ffn2_post_g"])
    saved = dict(h0=h0, n1=n1, gate1=gate1, up1=up1, a1=a1, f1=f1, h1=h1, u=u, fl=fl, sg=sg,
                 qkv_sb=qkv_sb, qkv_fx=qkv_fx, qkv_mem=qkv_mem, ccol=ccol, crow=crow, o_sb=o_sb, o_fx=o_fx, o_mem=o_mem,
                 tot_sb=tot_sb, lse_fx=lse_fx, lse_mem=lse_mem,
                 zmix=zmix, merged=merged, h2=h2, n2=n2, gate2=gate2, up2=up2, a2=a2, f2=f2)
    return h3, saved


def _ffn_bwd(dh, sv, wl, tag, h_in):
    n, gate, up, a, f = (sv[k + tag] for k in ("n", "gate", "up", "a", "f"))
    pre = "ffn" + tag
    df, dgate, dup, dg_post = _ffn_bwd_down(dh, f, wl[pre + "_post_g"], wl[pre + "_w_down"], gate, up)
    dh_in, dg_pre = _ffn_bwd_up(dgate, dup, wl[pre + "_w_gate"], wl[pre + "_w_up"], h_in, wl[pre + "_pre_g"], dh)
    grads = {pre + "_post_g": dg_post, pre + "_pre_g": dg_pre,
             pre + "_w_down": _matmul(a, df, ta=True, batch="a", name="dw_down"),
             pre + "_w_gate": _matmul(n, dgate, ta=True, batch="b", name="dw_gate"),
             pre + "_w_up": _matmul(n, dup, ta=True, batch="b", name="dw_up")}
    return dh_in, grads


def _layer_bwd(dh3, mem_n, wl, sv, dims):
    n_sb, n_fx, n_mem, sbw, fxw, memw = dims
    S = dh3.shape[0]
    dh2, grads = _ffn_bwd(dh3, sv, wl, "2", sv["h2"])

    (dz, db_sb, db_fx, db_mem, do_sb, do_fx, do_mem, dgp, db_gate, dg_post) = _mix_bwd_out(
        dh2, sv["zmix"], wl["mix_post_g"], wl["w_out"], sv["o_sb"], sv["o_fx"], sv["o_mem"],
        wl["w_br_sb"], wl["w_br_fox"], wl["w_br_mem"], sv["sg"])
    grads["mix_post_g"] = dg_post
    grads["b_gate"] = db_gate
    grads["w_out"] = _matmul(sv["merged"], dz, ta=True, name="dw_out")
    grads["w_br_sb"] = _matmul(sv["o_sb"], db_sb, ta=True, name="dw_br_sb")
    grads["w_br_fox"] = _matmul(sv["o_fx"], db_fx, ta=True, name="dw_br_fox")
    grads["w_br_mem"] = _matmul(sv["o_mem"], db_mem, ta=True, name="dw_br_mem")

    dq_sb, dk_sb, dv_sb = _attn_bwd("sb", *sv["qkv_sb"], sv["o_sb"], do_sb, n_sb, HEAD_DIM, lse=sv["tot_sb"])
    dq_fx, dk_fx, dv_fx, dcrow, dccol = _attn_bwd("fox", *sv["qkv_fx"], sv["o_fx"], do_fx, n_fx, HEAD_DIM,
                                                  sv["ccol"], sv["crow"], sv["lse_fx"])
    dq_mem, dk_mem, dv_mem = _attn_bwd("mem", *sv["qkv_mem"], sv["o_mem"], do_mem, n_mem, MEM_HEAD_DIM,
                                       lse=sv["lse_mem"])
    dkv = jnp.concatenate([dk_mem, dv_mem], axis=1)
    grads["w_mem_kv"] = _matmul(mem_n, dkv, ta=True, name="dw_mem_kv")
    dmem_n = _matmul(dkv, wl["w_mem_kv"], tb=True, out_dtype=F32, name="dmem_n")

    dc = jnp.pad((dcrow.reshape(n_fx, S) + dccol.reshape(n_fx, S)).T, ((0, 0), (0, LANE - n_fx)))
    dfl, db_forget = _fox_dlogit(dc, sv["fl"])
    grads["b_forget"] = db_forget
    dproj = jnp.concatenate([dq_sb, dk_sb, dv_sb, dq_fx, dk_fx, dv_fx, dq_mem, dfl], axis=1)
    dh1, dg_pre = _mix_bwd_in(dproj, dgp, wl["w_in"], wl["w_gate"], sv["h1"], wl["mix_pre_g"], dh2)
    grads["mix_pre_g"] = dg_pre
    grads["w_in"] = _matmul(sv["u"], dproj, ta=True, name="dw_in")
    grads["w_gate"] = _matmul(sv["u"], dgp, ta=True, name="dw_gate_mix")

    dh0, g1 = _ffn_bwd(dh1, sv, wl, "1", sv["h0"])
    grads.update(g1)
    return dh0, grads, dmem_n


def kernel(x, mem, ffn1_pre_g, ffn1_post_g, ffn1_w_gate, ffn1_w_up, ffn1_w_down, mix_pre_g, mix_post_g, w_in, b_forget, mem_norm_g, w_mem_kv, w_gate, b_gate, w_br_sb, w_br_fox, w_br_mem, w_out, ffn2_pre_g, ffn2_post_g, ffn2_w_gate, ffn2_w_up, ffn2_w_down, loss_target, m_ffn1_pre_g, m_ffn1_post_g, m_ffn1_w_gate, m_ffn1_w_up, m_ffn1_w_down, m_mix_pre_g, m_mix_post_g, m_w_in, m_b_forget, m_mem_norm_g, m_w_mem_kv, m_w_gate, m_b_gate, m_w_br_sb, m_w_br_fox, m_w_br_mem, m_w_out, m_ffn2_pre_g, m_ffn2_post_g, m_ffn2_w_gate, m_ffn2_w_up, m_ffn2_w_down, v_ffn1_pre_g, v_ffn1_post_g, v_ffn1_w_gate, v_ffn1_w_up, v_ffn1_w_down, v_mix_pre_g, v_mix_post_g, v_w_in, v_b_forget, v_mem_norm_g, v_w_mem_kv, v_w_gate, v_b_gate, v_w_br_sb, v_w_br_fox, v_w_br_mem, v_w_out, v_ffn2_pre_g, v_ffn2_post_g, v_ffn2_w_gate, v_ffn2_w_up, v_ffn2_w_down):
    args = dict(locals())
    w = {n: args[n] for n in WEIGHTS}
    m = {n: args["m_" + n] for n in WEIGHTS}
    v = {n: args["v_" + n] for n in WEIGHTS}
    L = w["ffn1_pre_g"].shape[0]
    Lh = L // 2
    D = x.shape[2]
    sbw, fxw, memw = w["w_br_sb"].shape[1], w["w_br_fox"].shape[1], w["w_br_mem"].shape[1]
    n_sb, n_fx, n_mem = sbw // HEAD_DIM, fxw // HEAD_DIM, memw // MEM_HEAD_DIM
    dims = (n_sb, n_fx, n_mem, sbw, fxw, memw)
    qkv_w = 3 * sbw + 3 * fxw
    c_idx = lax.axis_index("c")
    c_arr = c_idx.reshape(1).astype(jnp.int32)
    chip_arr = (2 * lax.axis_index("x") + lax.axis_index("y")).reshape(1).astype(jnp.int32)

    shard_shapes = {n: w[n].shape[1:] for n, _ in BIG}
    layout = _row_layout(shard_shapes, Lh)
    widths = list(layout)
    locs = [jnp.stack([_pack_rows(layout[b], b, {n: w[n][hf * Lh:(hf + 1) * Lh].reshape(-1, b) for n in layout[b][0]},
                                  BF16) for hf in range(2)]) for b in widths]
    gathered = dict(zip(widths, _gather_weights(locs, [_place_own(loc, chip_arr) for loc in locs])))

    def layer_weights(l):
        hf, li = divmod(l, Lh)
        wl = {}
        for n, axis in BIG:
            a, b = shard_shapes[n]
            r0 = layout[b][1][n] + li * a
            shards = gathered[b][:, hf, r0:r0 + a]
            if n.startswith("ffn"):
                wl[n] = shards
            else:
                wl[n] = (shards.transpose(1, 0, 2).reshape(a, N_CHIPS * b) if axis == 2 else
                         shards.reshape(N_CHIPS * a, b))
        wi = wl["w_in"]
        wl["w_in"] = jnp.concatenate([wi[:, :qkv_w], wi[:, qkv_w + n_fx:], wi[:, qkv_w:qkv_w + n_fx],
                                      jnp.zeros((D, LANE - n_fx), BF16)], axis=1)
        for n in SMALL:
            if n != "mem_norm_g":
                wl[n] = w[n][l][None, :]
        wl["b_forget"] = jnp.pad(wl["b_forget"], ((0, 0), (0, LANE - n_fx)))
        return wl

    g_mem = w["mem_norm_g"][None, :]

    mem_n = _mem_norm(mem[0], g_mem)
    h, wls, saved = x[0], [], []
    for l in range(L):
        wls.append(layer_weights(l))
        h, sv = _layer_fwd(h, mem_n, wls[l], dims)
        saved.append(sv)
    dh, loss_tile = _loss_head(h, loss_target[0])
    loss = lax.psum(loss_tile[0, 0], ("x", "y", "c"))
    gl, dmem_n = [None] * L, [None] * L
    for l in reversed(range(L)):
        dh, gl[l], dmem_n[l] = _layer_bwd(dh, mem_n, wls[l], saved[l], dims)
        gi = gl[l]["w_in"]
        gl[l]["w_in"] = jnp.concatenate([gi[:, :qkv_w], gi[:, qkv_w + memw:qkv_w + memw + n_fx],
                                         gi[:, qkv_w:qkv_w + memw]], axis=1)
    grad_x = dh
    g_mem_norm = _mem_norm_bwd(mem[0], g_mem, jnp.stack(dmem_n))

    axis_of = dict(BIG)
    partials = [jnp.stack([jnp.stack([
        _pack_rows(layout[b], b, {n: jnp.concatenate([_slab(gl[hf * Lh + li][n], axis_of[n], j) for li in range(Lh)])
                                  for n in layout[b][0]}, BF16)
        for j in range(N_CHIPS)]) for hf in range(2)]) for b in widths]
    pairs = [_pair_sum(g, sib, c_arr) for g, sib in zip(partials, _pair_exchange(partials))]
    mines = [_chip_sum(p, r, chip_arr) for p, r in zip(pairs, _chip_exchange(pairs))]
    theirs = _pair_swap(mines)

    grad, delta, new_m, new_v = {}, {}, {}, {}
    for n, _ in BIG:
        k = widths.index(shard_shapes[n][1])
        grad[n], delta[n], new_m[n], new_v[n] = _adamw_reduced(
            w[n], m[n], v[n], mines[k], theirs[k], c_arr, layout[widths[k]][1][n], name="adamw_" + n)

    small_local = {n: (g_mem_norm if n == "mem_norm_g" else
                       jnp.concatenate([gl[l][n][:, :n_fx] if n == "b_forget" else gl[l][n] for l in range(L)]))
                   for n in SMALL}
    small_shapes = [small_local[n].shape for n in SMALL]
    small_sum = _unpack_flat(_all_reduce_small(_pack_flat([small_local[n] for n in SMALL], F32, row_block=16)),
                             small_shapes)
    for n, t in zip(SMALL, small_sum):
        shp = w[n].shape
        two_d = (1, shp[0]) if len(shp) == 1 else shp
        grad[n] = t.reshape(shp)
        d_, m_, v_ = _adamw(w[n].reshape(two_d), t.reshape(two_d), m[n].reshape(two_d), v[n].reshape(two_d),
                            name="adamw_" + n)
        delta[n], new_m[n], new_v[n] = d_.reshape(shp), m_.reshape(shp), v_.reshape(shp)

    return (loss, grad_x[None], *[grad[n] for n in WEIGHTS], *[delta[n] for n in WEIGHTS],
            *[new_m[n] for n in WEIGHTS], *[new_v[n] for n in WEIGHTS])
```

```python
import math

import jax
import jax.numpy as jnp
from jax import lax
from jax.experimental import pallas as pl
from jax.experimental.pallas import tpu as pltpu

F32 = jnp.float32
BF16 = jnp.bfloat16
RMS_EPS = 1e-6
HEAD_DIM = 64
MEM_HEAD_DIM = 128
LANE = 128
V7X_VMEM_LIMIT_BYTES = 56 * 1024 * 1024
FLAT_COLS = 512
FLAT_UNIT = 16 * FLAT_COLS
FLAT_ROW_BLOCK = 512
N_CHIPS = 4
N_DEV = 8
NEG = float(jnp.finfo(jnp.float32).min)

ADAM_LR = 0.001
ADAM_B1 = 0.9
ADAM_B2 = 0.999
ADAM_EPS = 1e-08
ADAM_WD = 0.01
ADAM_STEP = 10

BIG = (("ffn1_w_gate", 2), ("ffn1_w_up", 2), ("ffn1_w_down", 1), ("w_in", 2), ("w_mem_kv", 1), ("w_gate", 2),
       ("w_br_sb", 2), ("w_br_fox", 2), ("w_br_mem", 2), ("w_out", 1),
       ("ffn2_w_gate", 2), ("ffn2_w_up", 2), ("ffn2_w_down", 1))
SMALL = ("ffn1_pre_g", "ffn1_post_g", "mix_pre_g", "mix_post_g", "b_forget", "mem_norm_g", "b_gate",
         "ffn2_pre_g", "ffn2_post_g")
WEIGHTS = ("ffn1_pre_g", "ffn1_post_g", "ffn1_w_gate", "ffn1_w_up", "ffn1_w_down", "mix_pre_g", "mix_post_g", "w_in",
           "b_forget", "mem_norm_g", "w_mem_kv", "w_gate", "b_gate", "w_br_sb", "w_br_fox", "w_br_mem", "w_out",
           "ffn2_pre_g", "ffn2_post_g", "ffn2_w_gate", "ffn2_w_up", "ffn2_w_down")


def _params(**kw):
    return pltpu.CompilerParams(vmem_limit_bytes=V7X_VMEM_LIMIT_BYTES, **kw)


def _dot(a, b):
    return jnp.dot(a, b, preferred_element_type=F32)


def _dot_nt(a, b):
    return lax.dot_general(a, b, (((1,), (1,)), ((), ())), preferred_element_type=F32)


def _dot_tn(a, b):
    return lax.dot_general(a, b, (((0,), (0,)), ((), ())), preferred_element_type=F32)


def _rms(t, g):
    return t * lax.rsqrt(jnp.mean(t * t, axis=-1, keepdims=True) + RMS_EPS) * g


def _pick(dim, pref):
    if dim <= pref:
        return dim
    for cand in range(pref - pref % LANE, 0, -LANE):
        if dim % cand == 0:
            return cand
    return dim


def _rows(bm, cols):
    return pl.BlockSpec((bm, cols), lambda i: (i, 0))


def _whole(shape):
    nd = len(shape)
    return pl.BlockSpec(shape, lambda i: (0,) * nd)


def _split3(x):
    hi = x.astype(BF16)
    r1 = x - hi.astype(F32)
    mid = r1.astype(BF16)
    lo = (r1 - mid.astype(F32)).astype(BF16)
    return hi, mid, lo


def _cumdot(x, tri):
    hi = x.astype(BF16)
    lo = (x - hi.astype(F32)).astype(BF16)
    return _dot(hi, tri) + _dot(lo, tri)


def _slabs(bm, cols):
    return pl.BlockSpec((N_CHIPS, bm, cols), lambda i: (0, i, 0))


def _ffn_fwd_up(h, g_pre, wg, wu):
    S, D = h.shape
    Fs = wg.shape[2]
    bm = _pick(S, 256)

    def body(h_ref, g_ref, wg_ref, wu_ref, n_ref, gate_ref, up_ref, a_ref):
        n = _rms(h_ref[...], g_ref[...]).astype(BF16)
        n_ref[...] = n
        for j in range(N_CHIPS):
            gate = _dot(n, wg_ref[j])
            up = _dot(n, wu_ref[j])
            gate_ref[j] = gate.astype(BF16)
            up_ref[j] = up.astype(BF16)
            a_ref[j] = (gate * jax.nn.sigmoid(gate) * up).astype(BF16)

    return pl.pallas_call(
        body, name="ffn_fwd_up", grid=(S // bm,),
        in_specs=[_rows(bm, D), _whole((1, D)), _whole((N_CHIPS, D, Fs)), _whole((N_CHIPS, D, Fs))],
        out_specs=[_rows(bm, D), _slabs(bm, Fs), _slabs(bm, Fs), _slabs(bm, Fs)],
        out_shape=[jax.ShapeDtypeStruct((S, D), BF16)] + [jax.ShapeDtypeStruct((N_CHIPS, S, Fs), BF16)] * 3,
        compiler_params=_params(),
    )(h, g_pre, wg, wu)


def _ffn_fwd_down(a, wd, h, g_post):
    _, S, Fs = a.shape
    D = wd.shape[2]
    bm = _pick(S, 256)

    def body(a_ref, wd_ref, h_ref, g_ref, hout_ref, f_ref):
        f = _dot(a_ref[0], wd_ref[0])
        for j in range(1, N_CHIPS):
            f = f + _dot(a_ref[j], wd_ref[j])
        f_ref[...] = f
        hout_ref[...] = h_ref[...] + 0.5 * _rms(f, g_ref[...])

    return pl.pallas_call(
        body, name="ffn_fwd_down", grid=(S // bm,),
        in_specs=[_slabs(bm, Fs), _whole((N_CHIPS, Fs, D)), _rows(bm, D), _whole((1, D))],
        out_specs=[_rows(bm, D), _rows(bm, D)],
        out_shape=[jax.ShapeDtypeStruct((S, D), F32)] * 2,
        compiler_params=_params(),
    )(a, wd, h, g_post)


def _ffn_bwd_down(dh, f, g_post, wd, gate, up):
    S, D = dh.shape
    Fs = wd.shape[1]
    bm = _pick(S, 256)

    def body(dh_ref, f_ref, g_ref, wd_ref, gate_ref, up_ref, df_ref, dgate_ref, dup_ref, dg_ref):
        _, vjp = jax.vjp(lambda t, g: 0.5 * _rms(t, g), f_ref[...], g_ref[...])
        df, dg = vjp(dh_ref[...])

        @pl.when(pl.program_id(0) == 0)
        def _():
            dg_ref[...] = jnp.zeros_like(dg_ref)

        dg_ref[...] += dg
        dfb = df.astype(BF16)
        df_ref[...] = dfb
        for j in range(N_CHIPS):
            da = _dot_nt(dfb, wd_ref[j])
            gt = gate_ref[j].astype(F32)
            sig = jax.nn.sigmoid(gt)
            silu = gt * sig
            dup_ref[j] = (da * silu).astype(BF16)
            dgate_ref[j] = (da * up_ref[j].astype(F32) * (sig + silu * (1.0 - sig))).astype(BF16)

    return pl.pallas_call(
        body, name="ffn_bwd_down", grid=(S // bm,),
        in_specs=[_rows(bm, D), _rows(bm, D), _whole((1, D)), _whole((N_CHIPS, Fs, D)), _slabs(bm, Fs),
                  _slabs(bm, Fs)],
        out_specs=[_rows(bm, D), _slabs(bm, Fs), _slabs(bm, Fs), _whole((1, D))],
        out_shape=[jax.ShapeDtypeStruct((S, D), BF16), jax.ShapeDtypeStruct((N_CHIPS, S, Fs), BF16),
                   jax.ShapeDtypeStruct((N_CHIPS, S, Fs), BF16), jax.ShapeDtypeStruct((1, D), F32)],
        compiler_params=_params(),
    )(dh, f, g_post, wd, gate, up)


def _ffn_bwd_up(dgate, dup, wg, wu, h_in, g_pre, dh):
    _, S, Fs = dgate.shape
    D = wg.shape[1]
    bm = _pick(S, 256)

    def body(dgate_ref, dup_ref, wg_ref, wu_ref, h_ref, g_ref, dh_ref, dhin_ref, dg_ref):
        dn = _dot_nt(dgate_ref[0], wg_ref[0]) + _dot_nt(dup_ref[0], wu_ref[0])
        for j in range(1, N_CHIPS):
            dn = dn + _dot_nt(dgate_ref[j], wg_ref[j]) + _dot_nt(dup_ref[j], wu_ref[j])
        _, vjp = jax.vjp(_rms, h_ref[...], g_ref[...])
        dhx, dg = vjp(dn)

        @pl.when(pl.program_id(0) == 0)
        def _():
            dg_ref[...] = jnp.zeros_like(dg_ref)

        dg_ref[...] += dg
        dhin_ref[...] = dh_ref[...] + dhx

    return pl.pallas_call(
        body, name="ffn_bwd_up", grid=(S // bm,),
        in_specs=[_slabs(bm, Fs), _slabs(bm, Fs), _whole((N_CHIPS, D, Fs)), _whole((N_CHIPS, D, Fs)), _rows(bm, D),
                  _whole((1, D)), _rows(bm, D)],
        out_specs=[_rows(bm, D), _whole((1, D))],
        out_shape=[jax.ShapeDtypeStruct((S, D), F32), jax.ShapeDtypeStruct((1, D), F32)],
        compiler_params=_params(),
    )(dgate, dup, wg, wu, h_in, g_pre, dh)


def _matmul(a, b, *, ta=False, tb=False, out_dtype=BF16, name, batch=None):
    n_batch = a.shape[0] if batch == "a" else b.shape[0] if batch == "b" else 1
    a_shape = a.shape[1:] if batch == "a" else a.shape
    b_shape = b.shape[1:] if batch == "b" else b.shape
    M, K = (a_shape[1], a_shape[0]) if ta else a_shape
    N = b_shape[0] if tb else b_shape[1]
    acc_budget = 12 * 1024 * 1024
    bm, bk = _pick(M, 1536), _pick(K, 512)
    while n_batch * N * bm * 4 > acc_budget and bm % (2 * LANE) == 0:
        bm //= 2
    bn = N if n_batch * N * bm * 4 <= acc_budget else _pick(N, 1536)
    nk = K // bk

    def body(a_ref, b_ref, o_ref, acc_ref):
        kk = pl.program_id(2)

        @pl.when(kk == 0)
        def _():
            acc_ref[...] = jnp.zeros_like(acc_ref)

        dims = (((0 if ta else 1,), (1 if tb else 0,)), ((), ()))
        if batch is None:
            acc_ref[...] += lax.dot_general(a_ref[...], b_ref[...], dims, preferred_element_type=F32)
        else:
            for g in range(n_batch):
                av = a_ref[g] if batch == "a" else a_ref[...]
                bv = b_ref[g] if batch == "b" else b_ref[...]
                acc_ref[g] += lax.dot_general(av, bv, dims, preferred_element_type=F32)

        @pl.when(kk == nk - 1)
        def _():
            o_ref[...] = acc_ref[...].astype(o_ref.dtype)

    def spec(block, index, batched):
        if batched:
            return pl.BlockSpec((n_batch,) + block, lambda i, j, k: (0,) + index(i, j, k))
        return pl.BlockSpec(block, index)

    a_spec = spec((bk, bm), lambda i, j, k: (k, i), batch == "a") if ta else \
        spec((bm, bk), lambda i, j, k: (i, k), batch == "a")
    b_spec = spec((bn, bk), lambda i, j, k: (j, k), batch == "b") if tb else \
        spec((bk, bn), lambda i, j, k: (k, j), batch == "b")
    lead = (n_batch,) if batch else ()
    return pl.pallas_call(
        body, name=name, grid=(M // bm, N // bn, nk),
        in_specs=[a_spec, b_spec],
        out_specs=spec((bm, bn), lambda i, j, k: (i, j), batch is not None),
        out_shape=jax.ShapeDtypeStruct(lead + (M, N), out_dtype),
        scratch_shapes=[pltpu.VMEM(lead + (bm, bn), F32)],
        compiler_params=_params(),
    )(a, b)


def _mix_fwd_in(h, g_pre, win, wgate, b_gate, b_forget):
    S, D = h.shape
    PW = win.shape[1] - LANE
    G = wgate.shape[1]
    bm = _pick(S, 256)

    def body(h_ref, g_ref, win_ref, wgate_ref, bg_ref, bf_ref, u_ref, proj_ref, fl_ref, sg_ref):
        u = _rms(h_ref[...], g_ref[...]).astype(BF16)
        u_ref[...] = u
        proj = _dot(u, win_ref[...])
        proj_ref[...] = proj[:, :PW].astype(BF16)
        fl_ref[...] = proj[:, PW:] + bf_ref[...]
        sg_ref[...] = jax.nn.sigmoid(_dot(u, wgate_ref[...]) + bg_ref[...]).astype(BF16)

    return pl.pallas_call(
        body, name="mix_fwd_in", grid=(S // bm,),
        in_specs=[_rows(bm, D), _whole((1, D)), _whole((D, PW + LANE)), _whole((D, G)), _whole((1, G)),
                  _whole((1, LANE))],
        out_specs=[_rows(bm, D), _rows(bm, PW), _rows(bm, LANE), _rows(bm, G)],
        out_shape=[jax.ShapeDtypeStruct((S, D), BF16), jax.ShapeDtypeStruct((S, PW), BF16),
                   jax.ShapeDtypeStruct((S, LANE), F32), jax.ShapeDtypeStruct((S, G), BF16)],
        compiler_params=_params(),
    )(h, g_pre, win, wgate, b_gate, b_forget)


def _mix_fwd_out(o_sb, o_fx, o_mem, sg, w_sb, w_fx, w_mem, w_out, h, g_post):
    S, D = h.shape
    bm = _pick(S, 256)
    widths = (o_sb.shape[1], o_fx.shape[1], o_mem.shape[1])

    def body(osb_ref, ofx_ref, omem_ref, sg_ref, wsb_ref, wfx_ref, wmem_ref, wout_ref, h_ref, g_ref,
             hout_ref, z_ref, merged_ref):
        s = sg_ref[...].astype(F32)
        merged = (s[:, :D] * _dot(osb_ref[...], wsb_ref[...]) + s[:, D:2 * D] * _dot(ofx_ref[...], wfx_ref[...])
                  + s[:, 2 * D:] * _dot(omem_ref[...], wmem_ref[...]))
        mb = merged.astype(BF16)
        merged_ref[...] = mb
        z = _dot(mb, wout_ref[...])
        z_ref[...] = z
        hout_ref[...] = h_ref[...] + _rms(z, g_ref[...])

    return pl.pallas_call(
        body, name="mix_fwd_out", grid=(S // bm,),
        in_specs=[_rows(bm, widths[0]), _rows(bm, widths[1]), _rows(bm, widths[2]), _rows(bm, 3 * D),
                  _whole((widths[0], D)), _whole((widths[1], D)), _whole((widths[2], D)), _whole((D, D)),
                  _rows(bm, D), _whole((1, D))],
        out_specs=[_rows(bm, D), _rows(bm, D), _rows(bm, D)],
        out_shape=[jax.ShapeDtypeStruct((S, D), F32), jax.ShapeDtypeStruct((S, D), F32),
                   jax.ShapeDtypeStruct((S, D), BF16)],
        compiler_params=_params(),
    )(o_sb, o_fx, o_mem, sg, w_sb, w_fx, w_mem, w_out, h, g_post)


def _mix_bwd_out(dh, z, g_post, w_out, o_sb, o_fx, o_mem, w_sb, w_fx, w_mem, sg):
    S, D = dh.shape
    bm = _pick(S, 256)
    widths = (o_sb.shape[1], o_fx.shape[1], o_mem.shape[1])

    def body(dh_ref, z_ref, g_ref, wout_ref, osb_ref, ofx_ref, omem_ref, wsb_ref, wfx_ref, wmem_ref, sg_ref,
             dz_ref, dbsb_ref, dbfx_ref, dbmem_ref, dosb_ref, dofx_ref, domem_ref, dgp_ref, dbg_ref, dg_ref):
        _, vjp = jax.vjp(_rms, z_ref[...], g_ref[...])
        dz, dg = vjp(dh_ref[...])

        @pl.when(pl.program_id(0) == 0)
        def _():
            dg_ref[...] = jnp.zeros_like(dg_ref)
            dbg_ref[...] = jnp.zeros_like(dbg_ref)

        dg_ref[...] += dg
        dzb = dz.astype(BF16)
        dz_ref[...] = dzb
        dmerged = _dot_nt(dzb, wout_ref[...])
        s = sg_ref[...].astype(F32)
        branches = ((osb_ref, wsb_ref, dbsb_ref, dosb_ref), (ofx_ref, wfx_ref, dbfx_ref, dofx_ref),
                    (omem_ref, wmem_ref, dbmem_ref, domem_ref))
        for k, (o_ref, w_ref, db_ref, do_ref) in enumerate(branches):
            gs = s[:, k * D:(k + 1) * D]
            dbb = (dmerged * gs).astype(BF16)
            db_ref[...] = dbb
            do_ref[...] = _dot_nt(dbb, w_ref[...]).astype(BF16)
            dgp = dmerged * _dot(o_ref[...], w_ref[...]) * gs * (1.0 - gs)
            dgp_ref[:, k * D:(k + 1) * D] = dgp.astype(BF16)
            dbg_ref[:, k * D:(k + 1) * D] += jnp.sum(dgp, axis=0, keepdims=True)

    return pl.pallas_call(
        body, name="mix_bwd_out", grid=(S // bm,),
        in_specs=[_rows(bm, D), _rows(bm, D), _whole((1, D)), _whole((D, D)),
                  _rows(bm, widths[0]), _rows(bm, widths[1]), _rows(bm, widths[2]),
                  _whole((widths[0], D)), _whole((widths[1], D)), _whole((widths[2], D)), _rows(bm, 3 * D)],
        out_specs=[_rows(bm, D)] * 4 + [_rows(bm, widths[0]), _rows(bm, widths[1]), _rows(bm, widths[2]),
                                        _rows(bm, 3 * D), _whole((1, 3 * D)), _whole((1, D))],
        out_shape=[jax.ShapeDtypeStruct((S, D), BF16)] * 4
        + [jax.ShapeDtypeStruct((S, w), BF16) for w in widths]
        + [jax.ShapeDtypeStruct((S, 3 * D), BF16), jax.ShapeDtypeStruct((1, 3 * D), F32),
           jax.ShapeDtypeStruct((1, D), F32)],
        compiler_params=_params(),
    )(dh, z, g_post, w_out, o_sb, o_fx, o_mem, w_sb, w_fx, w_mem, sg)


def _mix_bwd_in(dproj, dgp, win, wgate, h_in, g_pre, dh):
    S, PWL = dproj.shape
    G = dgp.shape[1]
    D = h_in.shape[1]
    bm = _pick(S, 256)

    def body(dproj_ref, dgp_ref, win_ref, wgate_ref, h_ref, g_ref, dh_ref, dhin_ref, dg_ref):
        du = _dot_nt(dproj_ref[...], win_ref[...]) + _dot_nt(dgp_ref[...], wgate_ref[...])
        _, vjp = jax.vjp(_rms, h_ref[...], g_ref[...])
        dhx, dg = vjp(du)

        @pl.when(pl.program_id(0) == 0)
        def _():
            dg_ref[...] = jnp.zeros_like(dg_ref)

        dg_ref[...] += dg
        dhin_ref[...] = dh_ref[...] + dhx

    return pl.pallas_call(
        body, name="mix_bwd_in", grid=(S // bm,),
        in_specs=[_rows(bm, PWL), _rows(bm, G), _whole((D, PWL)), _whole((D, G)), _rows(bm, D), _whole((1, D)),
                  _rows(bm, D)],
        out_specs=[_rows(bm, D), _whole((1, D))],
        out_shape=[jax.ShapeDtypeStruct((S, D), F32), jax.ShapeDtypeStruct((1, D), F32)],
        compiler_params=_params(),
    )(dproj, dgp, win, wgate, h_in, g_pre, dh)


def _log_sigmoid(x):
    return jnp.minimum(x, 0.0) - jnp.log(1.0 + jnp.exp(-jnp.abs(x)))


def _fox_cumsum(fl):
    S = fl.shape[0]
    rb = _pick(S, LANE)

    def body(fl_ref, c_ref, carry_ref):
        @pl.when(pl.program_id(0) == 0)
        def _():
            carry_ref[...] = jnp.zeros_like(carry_ref)

        r = lax.broadcasted_iota(jnp.int32, (rb, rb), 0)
        cidx = lax.broadcasted_iota(jnp.int32, (rb, rb), 1)
        tri = (cidx <= r).astype(BF16)
        hi, mid, lo = _split3(_log_sigmoid(fl_ref[...]))
        c = _dot(tri, hi) + _dot(tri, mid) + _dot(tri, lo) + carry_ref[...]
        c_ref[...] = c
        carry_ref[...] = c[rb - 1:rb, :]

    return pl.pallas_call(
        body, name="fox_cumsum", grid=(S // rb,),
        in_specs=[_rows(rb, LANE)], out_specs=_rows(rb, LANE),
        out_shape=jax.ShapeDtypeStruct((S, LANE), F32),
        scratch_shapes=[pltpu.VMEM((1, LANE), F32)],
        compiler_params=_params(),
    )(fl)


def _fox_dlogit(dc, fl):
    S = fl.shape[0]
    rb = _pick(S, LANE)
    nb = S // rb

    def body(dc_ref, fl_ref, dfl_ref, dbf_ref, carry_ref):
        @pl.when(pl.program_id(0) == 0)
        def _():
            carry_ref[...] = jnp.zeros_like(carry_ref)
            dbf_ref[...] = jnp.zeros_like(dbf_ref)

        r = lax.broadcasted_iota(jnp.int32, (rb, rb), 0)
        cidx = lax.broadcasted_iota(jnp.int32, (rb, rb), 1)
        tri = (cidx >= r).astype(BF16)
        hi, mid, lo = _split3(dc_ref[...])
        rc = _dot(tri, hi) + _dot(tri, mid) + _dot(tri, lo) + carry_ref[...]
        carry_ref[...] = rc[0:1, :]
        dfl = rc * jax.nn.sigmoid(-fl_ref[...])
        dfl_ref[...] = dfl.astype(BF16)
        dbf_ref[...] += jnp.sum(dfl, axis=0, keepdims=True)

    rev = pl.BlockSpec((rb, LANE), lambda i: (nb - 1 - i, 0))
    return pl.pallas_call(
        body, name="fox_dlogit", grid=(nb,),
        in_specs=[rev, rev], out_specs=[rev, _whole((1, LANE))],
        out_shape=[jax.ShapeDtypeStruct((S, LANE), BF16), jax.ShapeDtypeStruct((1, LANE), F32)],
        scratch_shapes=[pltpu.VMEM((1, LANE), F32)],
        compiler_params=_params(),
    )(dc, fl)


def _attn_blocks(kind, S, Sk, backward=False):
    tq = _pick(S, 1024 if backward else 2048)
    tc = LANE if kind == "sb" else _pick(Sk, 256)
    return tq, tc


def _is_power_of_two(x):
    return math.frexp(x)[0] == 0.5


def _sb_logs(z):
    ln = -jnp.maximum(z, 0.0) - jnp.log(1.0 + jnp.exp(-jnp.abs(z)))
    return ln + z, ln


def _head_lanes(pack, dh):
    lane = lax.broadcasted_iota(jnp.int32, (1, LANE), 1)
    return [(lane >= hh * dh) & (lane < (hh + 1) * dh) for hh in range(pack)]


def _by_head(sel, parts):
    out = parts[0]
    for hh in range(1, len(parts)):
        out = jnp.where(sel[hh], parts[hh], out)
    return out


def _only_head(sel, hh, x):
    return x if len(sel) == 1 else jnp.where(sel[hh], x, jnp.zeros_like(x))


def _tail(x, r0):
    return x if not r0 else x[r0:]


def _put_tail(x, tail, r0):
    return tail if not r0 else jnp.concatenate([x[:r0], tail], axis=0)


def _add_tail(x, tail, r0):
    return x + tail if not r0 else jnp.concatenate([x[:r0], x[r0:] + tail], axis=0)


def _q_cols(tq, first):
    return pl.BlockSpec((tq, LANE), lambda g, i: (i, first // LANE + g))


def _k_cols(rows, first):
    return pl.BlockSpec((rows, LANE), lambda g, i: (0, first // LANE + g))


def _attn_fwd(kind, q, k, v, n_heads, dh, ccol=None, crow=None):
    (qa, q0), (ka, k0), (va, v0) = q, k, v
    S, Sk = qa.shape[0], ka.shape[0]
    pack = LANE // dh
    tq, tc = _attn_blocks(kind, S, Sk)
    scale = dh ** -0.5
    fold = _is_power_of_two(scale)
    causal = kind != "mem"
    n_diag = tq // tc if causal else 0
    unroll = 2 if causal else 1
    assert n_diag % unroll == 0 and (Sk // tc) % unroll == 0

    def body(*refs):
        if kind == "fox":
            q_ref, k_ref, v_ref, cc_ref, cr_ref, o_ref, lse_ref = refs
        else:
            q_ref, k_ref, v_ref, o_ref, lse_ref = refs
        i = pl.program_id(1)
        n_full = (i * tq) // tc if causal else Sk // tc
        qpos = i * tq + lax.broadcasted_iota(jnp.int32, (tq, tc), 0)
        kio = lax.broadcasted_iota(jnp.int32, (tq, tc), 1)
        heads = range(pack)
        sel = _head_lanes(pack, dh)
        q2 = q_ref[...] * scale if fold else q_ref[...]
        qs = [_only_head(sel, hh, q2) for hh in heads]

        def kv(jc):
            off = pl.multiple_of(jc * tc, tc)
            return off, k_ref[pl.ds(off, tc), :], v_ref[pl.ds(off, tc), :]

        if kind == "sb":
            tri = (lax.broadcasted_iota(jnp.int32, (tc, tc), 0) > lax.broadcasted_iota(jnp.int32, (tc, tc), 1)
                   ).astype(BF16)

            def chunk(jc, r0, runs, acc):
                off, k2, v2 = kv(jc)
                new_runs, pv = [], []
                for hh in heads:
                    lb, ln = _sb_logs(_dot_nt(_tail(qs[hh], r0), k2))
                    if r0 is not None:
                        mask = (off + _tail(kio, r0)) < _tail(qpos, r0)
                        ln = jnp.where(mask, ln, 0.0)
                    w = jnp.exp(lb + _cumdot(ln, tri) + _tail(runs[hh], r0))
                    if r0 is not None:
                        w = jnp.where(mask, w, 0.0)
                    pv.append(_dot(w.astype(BF16), v2))
                    new_runs.append(_add_tail(runs[hh], jnp.sum(ln, axis=1, keepdims=True), r0))
                return tuple(new_runs), _add_tail(acc, _by_head(sel, pv), r0)

            state = (tuple(jnp.zeros((tq, 1), F32) for _ in heads), jnp.zeros((tq, LANE), F32))
            for d in range(n_diag - 1, -1, -1):
                state = chunk(n_full + d, d * tc, *state)

            def trip(t, st):
                for u in range(unroll):
                    st = chunk(n_full - 1 - unroll * t - u, None, *st)
                return st

            runs, acc = lax.fori_loop(0, n_full // unroll, trip, state)
            o_ref[...] = acc.astype(o_ref.dtype)
            for hh in heads:
                lse_ref[hh] = runs[hh]
        else:
            def chunk(jc, r0, ms, ls, acc):
                off, k2, v2 = kv(jc)
                new_ms, new_ls, alphas, pv = [], [], [], []
                for hh in heads:
                    z = _dot_nt(_tail(qs[hh], r0), k2)
                    if not fold:
                        z = z * scale
                    if kind == "fox":
                        z = z + _tail(cc_ref[hh], r0) - cr_ref[hh, pl.ds(jc, 1), :]
                    if r0 is not None:
                        z = jnp.where((off + _tail(kio, r0)) <= _tail(qpos, r0), z, NEG)
                    m_old, l_old = _tail(ms[hh], r0), _tail(ls[hh], r0)
                    m_new = jnp.maximum(m_old, jnp.max(z, axis=1, keepdims=True))
                    alpha = jnp.exp(m_old - m_new)
                    p = jnp.exp(z - m_new)
                    new_ms.append(_put_tail(ms[hh], m_new, r0))
                    new_ls.append(_put_tail(ls[hh], alpha * l_old + jnp.sum(p, axis=1, keepdims=True), r0))
                    alphas.append(alpha)
                    pv.append(_dot(p.astype(BF16), v2))
                acc_new = _by_head(sel, alphas) * _tail(acc, r0) + _by_head(sel, pv)
                return tuple(new_ms), tuple(new_ls), _put_tail(acc, acc_new, r0)

            state = (tuple(jnp.full((tq, 1), NEG, F32) for _ in heads), tuple(jnp.zeros((tq, 1), F32) for _ in heads),
                     jnp.zeros((tq, LANE), F32))

            def trip(t, st):
                for u in range(unroll):
                    st = chunk(unroll * t + u, None, *st)
                return st

            state = lax.fori_loop(0, n_full // unroll, trip, state)
            for d in range(n_diag):
                state = chunk(n_full + d, d * tc, *state)
            ms, ls, acc = state
            o_ref[...] = (acc / _by_head(sel, ls)).astype(o_ref.dtype)
            for hh in heads:
                lse_ref[hh] = ms[hh] + jnp.log(ls[hh])

    colspec = pl.BlockSpec((pack, tq, 1), lambda g, i: (g, i, 0))
    in_specs, args = [_q_cols(tq, q0), _k_cols(Sk, k0), _k_cols(Sk, v0)], [qa, ka, va]
    if kind == "fox":
        in_specs += [colspec, pl.BlockSpec((pack, Sk // tc, tc), lambda g, i: (g, 0, 0))]
        args += [ccol, crow]
    return pl.pallas_call(
        body, name="attn_fwd_" + kind, grid=(n_heads // pack, S // tq),
        in_specs=in_specs, out_specs=[_q_cols(tq, 0), colspec],
        out_shape=[jax.ShapeDtypeStruct((S, n_heads * dh), BF16), jax.ShapeDtypeStruct((n_heads, S, 1), F32)],
        compiler_params=_params(),
    )(*args)


def _attn_bwd(kind, q, k, v, o, do, n_heads, dh, ccol=None, crow=None, lse=None):
    (qa, q0), (ka, k0), (va, v0) = q, k, v
    S, Sk = qa.shape[0], ka.shape[0]
    pack = LANE // dh
    tq, tc = _attn_blocks(kind, S, Sk, backward=True)
    scale = dh ** -0.5
    fold = _is_power_of_two(scale)
    causal = kind != "mem"
    n_diag = tq // tc if causal else 0
    unroll = 2 if kind == "sb" else 1
    assert n_diag % unroll == 0 and (Sk // tc) % unroll == 0
    nq = S // tq

    def body(*refs):
        if kind == "fox":
            (q_ref, k_ref, v_ref, o_ref, do_ref, cc_ref, cr_ref, lse_ref,
             dq_ref, dk_ref, dv_ref, dc_ref, dcc_ref, dk_acc, dv_acc, dc_acc) = refs
        else:
            q_ref, k_ref, v_ref, o_ref, do_ref, lse_ref, dq_ref, dk_ref, dv_ref, dk_acc, dv_acc = refs
        i = pl.program_id(1)

        @pl.when(i == 0)
        def _():
            dk_acc[...] = jnp.zeros_like(dk_acc)
            dv_acc[...] = jnp.zeros_like(dv_acc)
            if kind == "fox":
                dc_acc[...] = jnp.zeros_like(dc_acc)

        n_full = (i * tq) // tc if causal else Sk // tc
        qpos = i * tq + lax.broadcasted_iota(jnp.int32, (tq, tc), 0)
        kio = lax.broadcasted_iota(jnp.int32, (tq, tc), 1)
        heads = range(pack)
        sel = _head_lanes(pack, dh)
        q2 = q_ref[...] * scale if fold else q_ref[...]
        do2 = do_ref[...]
        qs = [_only_head(sel, hh, q2) for hh in heads]
        dos = [_only_head(sel, hh, do2) for hh in heads]

        def kv(jc):
            off = pl.multiple_of(jc * tc, tc)
            return off, k_ref[pl.ds(off, tc), :], v_ref[pl.ds(off, tc), :]

        def accumulate(off, k2, dzb, wb, dq, r0):
            q2t, do2t = _tail(q2, r0), _tail(do2, r0)
            dk_acc[pl.ds(off, tc), :] += _by_head(sel, [_dot_tn(dzb[hh], q2t) for hh in heads])
            dv_acc[pl.ds(off, tc), :] += _by_head(sel, [_dot_tn(wb[hh], do2t) for hh in heads])
            return _add_tail(dq, _by_head(sel, [_dot(dzb[hh], k2) for hh in heads]), r0)

        if kind == "sb":
            r = lax.broadcasted_iota(jnp.int32, (tc, tc), 0)
            cidx = lax.broadcasted_iota(jnp.int32, (tc, tc), 1)
            tri_inc = (r <= cidx).astype(BF16)
            tri_exc = (r < cidx).astype(BF16)

            def chunk(jc, r0, pres, pres_e, dq):
                off, k2, v2 = kv(jc)
                new_pres, new_pres_e, dzb, wb = [], [], [], []
                for hh in heads:
                    lb, ln = _sb_logs(_dot_nt(_tail(qs[hh], r0), k2))
                    if r0 is not None:
                        mask = (off + _tail(kio, r0)) < _tail(qpos, r0)
                        ln = jnp.where(mask, ln, 0.0)
                    w = jnp.exp(lb + (_tail(lse_ref[hh], r0) - _tail(pres[hh], r0) - _cumdot(ln, tri_inc)))
                    if r0 is not None:
                        w = jnp.where(mask, w, 0.0)
                    e = w * _dot_nt(_tail(dos[hh], r0), v2)
                    beta = jnp.exp(lb)
                    dz = e * (1.0 - beta) - beta * (_tail(pres_e[hh], r0) + _dot(e.astype(BF16), tri_exc))
                    if r0 is not None:
                        dz = jnp.where(mask, dz, 0.0)
                    dzb.append(dz.astype(BF16))
                    wb.append(w.astype(BF16))
                    new_pres.append(_add_tail(pres[hh], jnp.sum(ln, axis=1, keepdims=True), r0))
                    new_pres_e.append(_add_tail(pres_e[hh], jnp.sum(e, axis=1, keepdims=True), r0))
                return tuple(new_pres), tuple(new_pres_e), accumulate(off, k2, dzb, wb, dq, r0)

            state = (tuple(jnp.zeros((tq, 1), F32) for _ in heads), tuple(jnp.zeros((tq, 1), F32) for _ in heads),
                     jnp.zeros((tq, LANE), F32))
        else:
            prod = o_ref[...].astype(F32) * do2.astype(F32)
            dsum = [jnp.sum(_only_head(sel, hh, prod), axis=1, keepdims=True) for hh in heads]

            def chunk(jc, r0, rowsums, dq):
                off, k2, v2 = kv(jc)
                new_rowsums, dsb, pb = [], [], []
                for hh in heads:
                    z = _dot_nt(_tail(qs[hh], r0), k2)
                    if not fold:
                        z = z * scale
                    if kind == "fox":
                        z = z + _tail(cc_ref[hh], r0) - cr_ref[hh, pl.ds(jc, 1), :]
                    if r0 is not None:
                        z = jnp.where((off + _tail(kio, r0)) <= _tail(qpos, r0), z, NEG)
                    p = jnp.exp(z - _tail(lse_ref[hh], r0))
                    ds = p * (_dot_nt(_tail(dos[hh], r0), v2) - _tail(dsum[hh], r0))
                    dsb.append(ds.astype(BF16))
                    pb.append(p.astype(BF16))
                    if kind == "fox":
                        dc_acc[hh, pl.ds(jc, 1), :] -= jnp.sum(ds, axis=0, keepdims=True)
                        new_rowsums.append(_add_tail(rowsums[hh], jnp.sum(ds, axis=1, keepdims=True), r0))
                    else:
                        new_rowsums.append(rowsums[hh])
                return tuple(new_rowsums), accumulate(off, k2, dsb, pb, dq, r0)

            state = (tuple(jnp.zeros((tq, 1), F32) for _ in heads), jnp.zeros((tq, LANE), F32))

        def trip(t, st):
            for u in range(unroll):
                st = chunk(unroll * t + u, None, *st)
            return st

        state = lax.fori_loop(0, n_full // unroll, trip, state)
        for d in range(n_diag):
            state = chunk(n_full + d, d * tc, *state)
        dq_ref[...] = (state[-1] * scale).astype(dq_ref.dtype)
        if kind == "fox":
            for hh in heads:
                dcc_ref[hh] = state[0][hh]

        @pl.when(i == nq - 1)
        def _():
            dk = dk_acc[...] if fold else dk_acc[...] * scale
            dk_ref[...] = dk.astype(dk_ref.dtype)
            dv_ref[...] = dv_acc[...].astype(dv_ref.dtype)
            if kind == "fox":
                dc_ref[...] = dc_acc[...]

    colspec = pl.BlockSpec((pack, tq, 1), lambda g, i: (g, i, 0))
    rowspec = pl.BlockSpec((pack, Sk // tc, tc), lambda g, i: (g, 0, 0))
    in_specs = [_q_cols(tq, q0), _k_cols(Sk, k0), _k_cols(Sk, v0), _q_cols(tq, 0), _q_cols(tq, 0)]
    args = [qa, ka, va, o, do]
    if kind == "fox":
        in_specs += [colspec, rowspec]
        args += [ccol, crow]
    in_specs += [colspec]
    args += [lse]
    width = n_heads * dh
    out_specs = [_q_cols(tq, 0), _k_cols(Sk, 0), _k_cols(Sk, 0)]
    out_shape = [jax.ShapeDtypeStruct((S, width), BF16), jax.ShapeDtypeStruct((Sk, width), BF16),
                 jax.ShapeDtypeStruct((Sk, width), BF16)]
    scratch = [pltpu.VMEM((Sk, LANE), F32), pltpu.VMEM((Sk, LANE), F32)]
    if kind == "fox":
        out_specs += [rowspec, colspec]
        out_shape += [jax.ShapeDtypeStruct((n_heads, Sk // tc, tc), F32), jax.ShapeDtypeStruct((n_heads, S, 1), F32)]
        scratch.append(pltpu.VMEM((pack, Sk // tc, tc), F32))
    return pl.pallas_call(
        body, name="attn_bwd_" + kind, grid=(n_heads // pack, nq),
        in_specs=in_specs, out_specs=out_specs, out_shape=out_shape, scratch_shapes=scratch,
        compiler_params=_params(),
    )(*args)


def _mem_norm(mem, g):
    M, D = mem.shape

    def body(mem_ref, g_ref, out_ref):
        out_ref[...] = _rms(mem_ref[...], g_ref[...]).astype(BF16)

    return pl.pallas_call(
        body, name="mem_norm", grid=(1,),
        in_specs=[_whole((M, D)), _whole((1, D))], out_specs=_whole((M, D)),
        out_shape=jax.ShapeDtypeStruct((M, D), BF16), compiler_params=_params(),
    )(mem, g)


def _mem_norm_bwd(mem, g, dmem_n):
    M, D = mem.shape
    L = dmem_n.shape[0]

    def body(mem_ref, g_ref, d_ref, dg_ref):
        d = d_ref[0]
        for l in range(1, L):
            d = d + d_ref[l]
        _, vjp = jax.vjp(_rms, mem_ref[...], g_ref[...])
        dg_ref[...] = vjp(d)[1]

    return pl.pallas_call(
        body, name="mem_norm_bwd", grid=(1,),
        in_specs=[_whole((M, D)), _whole((1, D)), _whole((L, M, D))], out_specs=_whole((1, D)),
        out_shape=jax.ShapeDtypeStruct((1, D), F32), compiler_params=_params(),
    )(mem, g, dmem_n)


def _loss_head(h, target):
    S, D = h.shape
    bm = _pick(S, 512)

    def body(h_ref, t_ref, dh_ref, loss_ref):
        err = h_ref[...] - t_ref[...]
        dh_ref[...] = err * (1.0 / D)

        @pl.when(pl.program_id(0) == 0)
        def _():
            loss_ref[...] = jnp.zeros_like(loss_ref)

        loss_ref[...] += 0.5 * jnp.sum(jnp.mean(err * err, axis=-1, keepdims=True), axis=0, keepdims=True)

    return pl.pallas_call(
        body, name="loss_head", grid=(S // bm,),
        in_specs=[_rows(bm, D), _rows(bm, D)], out_specs=[_rows(bm, D), _whole((8, LANE))],
        out_shape=[jax.ShapeDtypeStruct((S, D), F32), jax.ShapeDtypeStruct((8, LANE), F32)],
        compiler_params=_params(),
    )(h, target)


def _adamw(w, g, m, v, name):
    R, C = w.shape
    rb = R if R * C * 4 <= (1 << 20) else _pick(R, 256)
    if R % rb:
        rb = R
    c1 = 1.0 - ADAM_B1 ** ADAM_STEP
    c2 = 1.0 - ADAM_B2 ** ADAM_STEP

    def body(w_ref, g_ref, m_ref, v_ref, d_ref, mo_ref, vo_ref):
        gv = g_ref[...]
        mn = ADAM_B1 * m_ref[...] + (1.0 - ADAM_B1) * gv
        vn = ADAM_B2 * v_ref[...] + (1.0 - ADAM_B2) * (gv * gv)
        mo_ref[...] = mn
        vo_ref[...] = vn
        d_ref[...] = -ADAM_LR * ((mn / c1) / (jnp.sqrt(vn / c2) + ADAM_EPS) + ADAM_WD * w_ref[...])

    return pl.pallas_call(
        body, name=name, grid=(R // rb,),
        in_specs=[_rows(rb, C)] * 4, out_specs=[_rows(rb, C)] * 3,
        out_shape=[jax.ShapeDtypeStruct((R, C), F32)] * 3, compiler_params=_params(),
    )(w, g, m, v)


def _adamw_reduced(w, m, v, mine, theirs, c_idx, first_row, name):
    L, a, b = w.shape
    Lh = L // 2
    rb = _shard_row_block(a)
    nb = a // rb
    assert first_row % rb == 0
    c1 = 1.0 - ADAM_B1 ** ADAM_STEP
    c2 = 1.0 - ADAM_B2 ** ADAM_STEP

    def own(i, c_ref):
        return (i, 0)

    def reduced(i, c_ref):
        return (first_row // rb + ((i // nb) % Lh) * nb + i % nb, 0)

    def body(c_ref, w_ref, m_ref, v_ref, mine_ref, theirs_ref, g_ref, d_ref, mo_ref, vo_ref):
        half = (pl.program_id(0) // nb) // Lh
        gv = jnp.where(c_ref[0] == half, mine_ref[...], theirs_ref[...])
        g_ref[...] = gv
        mn = ADAM_B1 * m_ref[...] + (1.0 - ADAM_B1) * gv
        vn = ADAM_B2 * v_ref[...] + (1.0 - ADAM_B2) * (gv * gv)
        mo_ref[...] = mn
        vo_ref[...] = vn
        d_ref[...] = -ADAM_LR * ((mn / c1) / (jnp.sqrt(vn / c2) + ADAM_EPS) + ADAM_WD * w_ref[...])

    outs = pl.pallas_call(
        body, name=name,
        grid_spec=pltpu.PrefetchScalarGridSpec(
            num_scalar_prefetch=1, grid=(L * nb,),
            in_specs=[pl.BlockSpec((rb, b), own)] * 3 + [pl.BlockSpec((rb, b), reduced)] * 2,
            out_specs=[pl.BlockSpec((rb, b), own)] * 4),
        out_shape=[jax.ShapeDtypeStruct((L * a, b), F32)] * 4, compiler_params=_params(),
    )(c_idx, w.reshape(L * a, b), m.reshape(L * a, b), v.reshape(L * a, b), mine, theirs)
    return [t.reshape(L, a, b) for t in outs]


ANY = pl.BlockSpec(memory_space=pl.ANY)
MESH = pl.DeviceIdType.MESH


def _place():
    x, y, c = lax.axis_index("x"), lax.axis_index("y"), lax.axis_index("c")
    others = [(1 - x, y), (x, 1 - y), (1 - x, 1 - y)]
    return x, y, c, others


def _place_own(loc, chip_idx):
    _, R, C = loc.shape
    rb = _pick(R, 2 * FLAT_ROW_BLOCK)

    def body(chip_ref, loc_ref, out_ref):
        out_ref[...] = loc_ref[...]

    return pl.pallas_call(
        body, name="place_own",
        grid_spec=pltpu.PrefetchScalarGridSpec(
            num_scalar_prefetch=1, grid=(2, R // rb),
            in_specs=[pl.BlockSpec((None, rb, C), lambda hf, i, chip_ref: (hf, i, 0))],
            out_specs=pl.BlockSpec((None, None, rb, C), lambda hf, i, chip_ref: (chip_ref[0], hf, i, 0))),
        out_shape=jax.ShapeDtypeStruct((N_CHIPS, 2, R, C), loc.dtype), compiler_params=_params(),
    )(chip_idx, loc)


def _gather_weights(locs, owns):
    n = len(locs)

    def body(*refs):
        loc_refs, out_refs, (send_sems, recv_sems) = refs[:n], refs[2 * n:3 * n], refs[3 * n:]
        x, y, c, others = _place()
        me = 2 * x + y
        sibling = (x, y, 1 - c)

        def copy(a, k, src, dst, to):
            return pltpu.make_async_remote_copy(src_ref=src, dst_ref=dst, send_sem=send_sems.at[a, k],
                                                recv_sem=recv_sems.at[a, k], device_id=to, device_id_type=MESH)

        first = [copy(a, j, loc_refs[a].at[c], out_refs[a].at[me, c], (ox, oy, c))
                 for j, (ox, oy) in enumerate(others) for a in range(n)]
        for cp in first:
            cp.start()
        passed = []
        for j, (ox, oy) in enumerate(others):
            for a in range(n):
                landed = out_refs[a].at[2 * ox + oy, c]
                copy(a, j, loc_refs[a].at[c], landed, sibling).wait_recv()
                cp = copy(a, 3 + j, landed, landed, sibling)
                cp.start()
                passed.append(cp)
        for j, (ox, oy) in enumerate(others):
            for a in range(n):
                copy(a, 3 + j, loc_refs[a].at[c], out_refs[a].at[2 * ox + oy, 1 - c], sibling).wait_recv()
        for cp in first + passed:
            cp.wait_send()

    return pl.pallas_call(
        body, name="gather_weights", in_specs=[ANY] * (2 * n), out_specs=[ANY] * n,
        out_shape=[jax.ShapeDtypeStruct(own.shape, own.dtype) for own in owns],
        input_output_aliases={n + a: a for a in range(n)},
        scratch_shapes=[pltpu.SemaphoreType.DMA((n, 6)), pltpu.SemaphoreType.DMA((n, 6))],
    )(*locs, *owns)


def _pair_exchange(gs):
    n = len(gs)

    def body(*refs):
        g_refs, out_refs, (send_sems, recv_sems) = refs[:n], refs[n:2 * n], refs[2 * n:]
        x, y, c, _ = _place()
        copies = [pltpu.make_async_remote_copy(src_ref=g_refs[a].at[1 - c], dst_ref=out_refs[a],
                                               send_sem=send_sems.at[a], recv_sem=recv_sems.at[a],
                                               device_id=(x, y, 1 - c), device_id_type=MESH) for a in range(n)]
        for cp in copies:
            cp.start()
        for cp in copies:
            cp.wait()

    return pl.pallas_call(
        body, name="pair_exchange", in_specs=[ANY] * n, out_specs=[ANY] * n,
        out_shape=[jax.ShapeDtypeStruct(g.shape[1:], g.dtype) for g in gs],
        scratch_shapes=[pltpu.SemaphoreType.DMA((n,)), pltpu.SemaphoreType.DMA((n,))],
    )(*gs)


def _pair_sum(g, sib, c_idx):
    _, _, R, C = g.shape
    rb = _pick(R, 512)

    def body(c_ref, g_ref, s_ref, o_ref):
        o_ref[...] = (g_ref[...].astype(F32) + s_ref[...].astype(F32)).astype(o_ref.dtype)

    return pl.pallas_call(
        body, name="pair_sum",
        grid_spec=pltpu.PrefetchScalarGridSpec(
            num_scalar_prefetch=1, grid=(N_CHIPS, R // rb),
            in_specs=[pl.BlockSpec((None, None, rb, C), lambda j, i, c_ref: (c_ref[0], j, i, 0)),
                      pl.BlockSpec((None, rb, C), lambda j, i, c_ref: (j, i, 0))],
            out_specs=pl.BlockSpec((None, rb, C), lambda j, i, c_ref: (j, i, 0))),
        out_shape=jax.ShapeDtypeStruct((N_CHIPS, R, C), g.dtype), compiler_params=_params(),
    )(c_idx, g, sib)


def _chip_exchange(ps):
    n = len(ps)

    def body(*refs):
        p_refs, out_refs, (send_sems, recv_sems) = refs[:n], refs[n:2 * n], refs[2 * n:]
        x, y, c, others = _place()
        copies = []
        for j, (ox, oy) in enumerate(others):
            for a in range(n):
                cp = pltpu.make_async_remote_copy(src_ref=p_refs[a].at[2 * ox + oy], dst_ref=out_refs[a].at[j],
                                                  send_sem=send_sems.at[a, j], recv_sem=recv_sems.at[a, j],
                                                  device_id=(ox, oy, c), device_id_type=MESH)
                cp.start()
                copies.append(cp)
        for cp in copies:
            cp.wait()

    return pl.pallas_call(
        body, name="chip_exchange", in_specs=[ANY] * n, out_specs=[ANY] * n,
        out_shape=[jax.ShapeDtypeStruct((N_CHIPS - 1,) + p.shape[1:], p.dtype) for p in ps],
        scratch_shapes=[pltpu.SemaphoreType.DMA((n, 3)), pltpu.SemaphoreType.DMA((n, 3))],
    )(*ps)


def _chip_sum(p, r, chip_idx):
    _, R, C = r.shape
    rb = _pick(R, 512)

    def body(chip_ref, p_ref, r_ref, o_ref):
        acc = p_ref[...].astype(F32)
        for j in range(N_CHIPS - 1):
            acc = acc + r_ref[j].astype(F32)
        o_ref[...] = acc

    return pl.pallas_call(
        body, name="chip_sum",
        grid_spec=pltpu.PrefetchScalarGridSpec(
            num_scalar_prefetch=1, grid=(R // rb,),
            in_specs=[pl.BlockSpec((None, rb, C), lambda i, chip_ref: (chip_ref[0], i, 0)),
                      pl.BlockSpec((N_CHIPS - 1, rb, C), lambda i, chip_ref: (0, i, 0))],
            out_specs=pl.BlockSpec((rb, C), lambda i, chip_ref: (i, 0))),
        out_shape=jax.ShapeDtypeStruct((R, C), F32), compiler_params=_params(),
    )(chip_idx, p, r)


def _pair_swap(rhs):
    n = len(rhs)

    def body(*refs):
        rh_refs, out_refs, (send_sems, recv_sems) = refs[:n], refs[n:2 * n], refs[2 * n:]
        x, y, c, _ = _place()
        copies = [pltpu.make_async_remote_copy(src_ref=rh_refs[a], dst_ref=out_refs[a], send_sem=send_sems.at[a],
                                               recv_sem=recv_sems.at[a], device_id=(x, y, 1 - c),
                                               device_id_type=MESH) for a in range(n)]
        for cp in copies:
            cp.start()
        for cp in copies:
            cp.wait()

    return pl.pallas_call(
        body, name="pair_swap", in_specs=[ANY] * n, out_specs=[ANY] * n,
        out_shape=[jax.ShapeDtypeStruct(rh.shape, rh.dtype) for rh in rhs],
        scratch_shapes=[pltpu.SemaphoreType.DMA((n,)), pltpu.SemaphoreType.DMA((n,))],
    )(*rhs)


def _all_reduce_small(s):
    R, C = s.shape

    def body(s_ref, o_ref, buf, send_sems, recv_sems):
        x, y, c, _ = _place()
        me = 4 * x + 2 * y + c
        sends = []
        for k in range(1, N_DEV):
            fx, fy, fc = (k >> 2) & 1, (k >> 1) & 1, k & 1
            to = (x ^ fx, y ^ fy, c ^ fc)
            cp = pltpu.make_async_remote_copy(src_ref=s_ref, dst_ref=buf.at[me], send_sem=send_sems.at[k - 1],
                                              recv_sem=recv_sems.at[k - 1], device_id=to, device_id_type=MESH)
            cp.start()
            sends.append(cp)
        buf[me] = s_ref[...]
        for k in range(1, N_DEV):
            fx, fy, fc = (k >> 2) & 1, (k >> 1) & 1, k & 1
            frm = 4 * (x ^ fx) + 2 * (y ^ fy) + (c ^ fc)
            pltpu.make_async_remote_copy(src_ref=s_ref, dst_ref=buf.at[frm], send_sem=send_sems.at[k - 1],
                                         recv_sem=recv_sems.at[k - 1], device_id=(x, y, c),
                                         device_id_type=MESH).wait_recv()
        acc = buf[0]
        for d in range(1, N_DEV):
            acc = acc + buf[d]
        o_ref[...] = acc
        for cp in sends:
            cp.wait_send()

    vm = pl.BlockSpec(memory_space=pltpu.VMEM)
    return pl.pallas_call(
        body, name="all_reduce_small", in_specs=[vm], out_specs=vm,
        out_shape=jax.ShapeDtypeStruct((R, C), F32),
        scratch_shapes=[pltpu.VMEM((N_DEV, R, C), F32), pltpu.SemaphoreType.DMA((N_DEV - 1,)),
                        pltpu.SemaphoreType.DMA((N_DEV - 1,))],
    )(s)


def _padded(n):
    return -(-n // FLAT_UNIT) * FLAT_UNIT


def _pack_flat(pieces, dtype, row_block=FLAT_ROW_BLOCK):
    flat = []
    for p in pieces:
        p = p.reshape(-1).astype(dtype)
        flat.append(jnp.pad(p, (0, _padded(p.size) - p.size)))
    total = sum(p.size for p in flat)
    flat.append(jnp.zeros((-total) % (row_block * FLAT_COLS), dtype))
    return jnp.concatenate(flat).reshape(-1, FLAT_COLS)


def _unpack_flat(flat, shapes):
    lead = flat.shape[:-2]
    flat = flat.reshape(lead + (-1,))
    out, off = [], 0
    for shp in shapes:
        n = math.prod(shp)
        out.append(flat[..., off:off + n].reshape(lead + tuple(shp)))
        off += _padded(n)
    return out


def _shard_row_block(a):
    for rb in range(min(a, 512) // 16 * 16, 0, -16):
        if a % rb == 0:
            return rb
    return a


def _row_layout(shapes, n_layers):
    groups = {}
    for name, (a, b) in shapes.items():
        names, first, rows = groups.get(b, ((), {}, 0))
        rb = _shard_row_block(a)
        start = -(-rows // rb) * rb
        groups[b] = (names + (name,), {**first, name: start}, start + n_layers * a)
    return {b: (names, first, -(-rows // FLAT_ROW_BLOCK) * FLAT_ROW_BLOCK) for b, (names, first, rows) in groups.items()}


def _pack_rows(group, width, pieces, dtype):
    names, first, rows = group
    parts, at = [], 0
    for name in names:
        if first[name] > at:
            parts.append(jnp.zeros((first[name] - at, width), dtype))
        parts.append(pieces[name].astype(dtype))
        at = first[name] + pieces[name].shape[0]
    if rows > at:
        parts.append(jnp.zeros((rows - at, width), dtype))
    return jnp.concatenate(parts, axis=0)


def _slab(t, axis, j):
    if t.ndim == 3:
        return t[j]
    n = t.shape[axis - 1] // N_CHIPS
    return lax.slice_in_dim(t, j * n, (j + 1) * n, axis=axis - 1)


def _layer_fwd(h0, mem_n, wl, dims):
    n_sb, n_fx, n_mem, sbw, fxw, memw = dims
    n1, gate1, up1, a1 = _ffn_fwd_up(h0, wl["ffn1_pre_g"], wl["ffn1_w_gate"], wl["ffn1_w_up"])
    h1, f1 = _ffn_fwd_down(a1, wl["ffn1_w_down"], h0, wl["ffn1_post_g"])

    u, proj, fl, sg = _mix_fwd_in(h1, wl["mix_pre_g"], wl["w_in"], wl["w_gate"], wl["b_gate"], wl["b_forget"])
    c = _fox_cumsum(fl)
    S = h0.shape[0]
    tc = _attn_blocks("fox", S, S)[1]
    ct = c[:, :n_fx].T
    ccol, crow = ct.reshape(n_fx, S, 1), ct.reshape(n_fx, S // tc, tc)
    qkv_sb = [(proj, k * sbw) for k in range(3)]
    qkv_fx = [(proj, 3 * sbw + k * fxw) for k in range(3)]
    kv = _matmul(mem_n, wl["w_mem_kv"], out_dtype=BF16, name="mem_kv")
    qkv_mem = [(proj, 3 * sbw + 3 * fxw), (kv, 0), (kv, memw)]
    o_sb, tot_sb = _attn_fwd("sb", *qkv_sb, n_sb, HEAD_DIM)
    o_fx, lse_fx = _attn_fwd("fox", *qkv_fx, n_fx, HEAD_DIM, ccol, crow)
    o_mem, lse_mem = _attn_fwd("mem", *qkv_mem, n_mem, MEM_HEAD_DIM)
    h2, zmix, merged = _mix_fwd_out(o_sb, o_fx, o_mem, sg, wl["w_br_sb"], wl["w_br_fox"], wl["w_br_mem"],
                                    wl["w_out"], h1, wl["mix_post_g"])

    n2, gate2, up2, a2 = _ffn_fwd_up(h2, wl["ffn2_pre_g"], wl["ffn2_w_gate"], wl["ffn2_w_up"])
    h3, f2 = _ffn_fwd_down(a2, wl["ffn2_w_down"], h2, wl["ffn2_post_g"])
    saved = dict(h0=h0, n1=n1, gate1=gate1, up1=up1, a1=a1, f1=f1, h1=h1, u=u, fl=fl, sg=sg,
                 qkv_sb=qkv_sb, qkv_fx=qkv_fx, qkv_mem=qkv_mem, ccol=ccol, crow=crow, o_sb=o_sb, o_fx=o_fx, o_mem=o_mem,
                 tot_sb=tot_sb, lse_fx=lse_fx, lse_mem=lse_mem,
                 zmix=zmix, merged=merged, h2=h2, n2=n2, gate2=gate2, up2=up2, a2=a2, f2=f2)
    return h3, saved


def _ffn_bwd(dh, sv, wl, tag, h_in):
    n, gate, up, a, f = (sv[k + tag] for k in ("n", "gate", "up", "a", "f"))
    pre = "ffn" + tag
    df, dgate, dup, dg_post = _ffn_bwd_down(dh, f, wl[pre + "_post_g"], wl[pre + "_w_down"], gate, up)
    dh_in, dg_pre = _ffn_bwd_up(dgate, dup, wl[pre + "_w_gate"], wl[pre + "_w_up"], h_in, wl[pre + "_pre_g"], dh)
    grads = {pre + "_post_g": dg_post, pre + "_pre_g": dg_pre,
             pre + "_w_down": _matmul(a, df, ta=True, batch="a", name="dw_down"),
             pre + "_w_gate": _matmul(n, dgate, ta=True, batch="b", name="dw_gate"),
             pre + "_w_up": _matmul(n, dup, ta=True, batch="b", name="dw_up")}
    return dh_in, grads


def _layer_bwd(dh3, mem_n, wl, sv, dims):
    n_sb, n_fx, n_mem, sbw, fxw, memw = dims
    S = dh3.shape[0]
    dh2, grads = _ffn_bwd(dh3, sv, wl, "2", sv["h2"])

    (dz, db_sb, db_fx, db_mem, do_sb, do_fx, do_mem, dgp, db_gate, dg_post) = _mix_bwd_out(
        dh2, sv["zmix"], wl["mix_post_g"], wl["w_out"], sv["o_sb"], sv["o_fx"], sv["o_mem"],
        wl["w_br_sb"], wl["w_br_fox"], wl["w_br_mem"], sv["sg"])
    grads["mix_post_g"] = dg_post
    grads["b_gate"] = db_gate
    grads["w_out"] = _matmul(sv["merged"], dz, ta=True, name="dw_out")
    grads["w_br_sb"] = _matmul(sv["o_sb"], db_sb, ta=True, name="dw_br_sb")
    grads["w_br_fox"] = _matmul(sv["o_fx"], db_fx, ta=True, name="dw_br_fox")
    grads["w_br_mem"] = _matmul(sv["o_mem"], db_mem, ta=True, name="dw_br_mem")

    dq_sb, dk_sb, dv_sb = _attn_bwd("sb", *sv["qkv_sb"], sv["o_sb"], do_sb, n_sb, HEAD_DIM, lse=sv["tot_sb"])
    dq_fx, dk_fx, dv_fx, dcrow, dccol = _attn_bwd("fox", *sv["qkv_fx"], sv["o_fx"], do_fx, n_fx, HEAD_DIM,
                                                  sv["ccol"], sv["crow"], sv["lse_fx"])
    dq_mem, dk_mem, dv_mem = _attn_bwd("mem", *sv["qkv_mem"], sv["o_mem"], do_mem, n_mem, MEM_HEAD_DIM,
                                       lse=sv["lse_mem"])
    dkv = jnp.concatenate([dk_mem, dv_mem], axis=1)
    grads["w_mem_kv"] = _matmul(mem_n, dkv, ta=True, name="dw_mem_kv")
    dmem_n = _matmul(dkv, wl["w_mem_kv"], tb=True, out_dtype=F32, name="dmem_n")

    dc = jnp.pad((dcrow.reshape(n_fx, S) + dccol.reshape(n_fx, S)).T, ((0, 0), (0, LANE - n_fx)))
    dfl, db_forget = _fox_dlogit(dc, sv["fl"])
    grads["b_forget"] = db_forget
    dproj = jnp.concatenate([dq_sb, dk_sb, dv_sb, dq_fx, dk_fx, dv_fx, dq_mem, dfl], axis=1)
    dh1, dg_pre = _mix_bwd_in(dproj, dgp, wl["w_in"], wl["w_gate"], sv["h1"], wl["mix_pre_g"], dh2)
    grads["mix_pre_g"] = dg_pre
    grads["w_in"] = _matmul(sv["u"], dproj, ta=True, name="dw_in")
    grads["w_gate"] = _matmul(sv["u"], dgp, ta=True, name="dw_gate_mix")

    dh0, g1 = _ffn_bwd(dh1, sv, wl, "1", sv["h0"])
    grads.update(g1)
    return dh0, grads, dmem_n


def kernel(x, mem, ffn1_pre_g, ffn1_post_g, ffn1_w_gate, ffn1_w_up, ffn1_w_down, mix_pre_g, mix_post_g, w_in, b_forget, mem_norm_g, w_mem_kv, w_gate, b_gate, w_br_sb, w_br_fox, w_br_mem, w_out, ffn2_pre_g, ffn2_post_g, ffn2_w_gate, ffn2_w_up, ffn2_w_down, loss_target, m_ffn1_pre_g, m_ffn1_post_g, m_ffn1_w_gate, m_ffn1_w_up, m_ffn1_w_down, m_mix_pre_g, m_mix_post_g, m_w_in, m_b_forget, m_mem_norm_g, m_w_mem_kv, m_w_gate, m_b_gate, m_w_br_sb, m_w_br_fox, m_w_br_mem, m_w_out, m_ffn2_pre_g, m_ffn2_post_g, m_ffn2_w_gate, m_ffn2_w_up, m_ffn2_w_down, v_ffn1_pre_g, v_ffn1_post_g, v_ffn1_w_gate, v_ffn1_w_up, v_ffn1_w_down, v_mix_pre_g, v_mix_post_g, v_w_in, v_b_forget, v_mem_norm_g, v_w_mem_kv, v_w_gate, v_b_gate, v_w_br_sb, v_w_br_fox, v_w_br_mem, v_w_out, v_ffn2_pre_g, v_ffn2_post_g, v_ffn2_w_gate, v_ffn2_w_up, v_ffn2_w_down):
    args = dict(locals())
    w = {n: args[n] for n in WEIGHTS}
    m = {n: args["m_" + n] for n in WEIGHTS}
    v = {n: args["v_" + n] for n in WEIGHTS}
    L = w["ffn1_pre_g"].shape[0]
    Lh = L // 2
    D = x.shape[2]
    sbw, fxw, memw = w["w_br_sb"].shape[1], w["w_br_fox"].shape[1], w["w_br_mem"].shape[1]
    n_sb, n_fx, n_mem = sbw // HEAD_DIM, fxw // HEAD_DIM, memw // MEM_HEAD_DIM
    dims = (n_sb, n_fx, n_mem, sbw, fxw, memw)
    qkv_w = 3 * sbw + 3 * fxw
    c_idx = lax.axis_index("c")
    c_arr = c_idx.reshape(1).astype(jnp.int32)
    chip_arr = (2 * lax.axis_index("x") + lax.axis_index("y")).reshape(1).astype(jnp.int32)

    shard_shapes = {n: w[n].shape[1:] for n, _ in BIG}
    layout = _row_layout(shard_shapes, Lh)
    widths = list(layout)
    locs = [jnp.stack([_pack_rows(layout[b], b, {n: w[n][hf * Lh:(hf + 1) * Lh].reshape(-1, b) for n in layout[b][0]},
                                  BF16) for hf in range(2)]) for b in widths]
    gathered = dict(zip(widths, _gather_weights(locs, [_place_own(loc, chip_arr) for loc in locs])))

    def layer_weights(l):
        hf, li = divmod(l, Lh)
        wl = {}
        for n, axis in BIG:
            a, b = shard_shapes[n]
            r0 = layout[b][1][n] + li * a
            shards = gathered[b][:, hf, r0:r0 + a]
            if n.startswith("ffn"):
                wl[n] = shards
            else:
                wl[n] = (shards.transpose(1, 0, 2).reshape(a, N_CHIPS * b) if axis == 2 else
                         shards.reshape(N_CHIPS * a, b))
        wi = wl["w_in"]
        wl["w_in"] = jnp.concatenate([wi[:, :qkv_w], wi[:, qkv_w + n_fx:], wi[:, qkv_w:qkv_w + n_fx],
                                      jnp.zeros((D, LANE - n_fx), BF16)], axis=1)
        for n in SMALL:
            if n != "mem_norm_g":
                wl[n] = w[n][l][None, :]
        wl["b_forget"] = jnp.pad(wl["b_forget"], ((0, 0), (0, LANE - n_fx)))
        return wl

    g_mem = w["mem_norm_g"][None, :]

    mem_n = _mem_norm(mem[0], g_mem)
    h, wls, saved = x[0], [], []
    for l in range(L):
        wls.append(layer_weights(l))
        h, sv = _layer_fwd(h, mem_n, wls[l], dims)
        saved.append(sv)
    dh, loss_tile = _loss_head(h, loss_target[0])
    loss = lax.psum(loss_tile[0, 0], ("x", "y", "c"))
    gl, dmem_n = [None] * L, [None] * L
    for l in reversed(range(L)):
        dh, gl[l], dmem_n[l] = _layer_bwd(dh, mem_n, wls[l], saved[l], dims)
        gi = gl[l]["w_in"]
        gl[l]["w_in"] = jnp.concatenate([gi[:, :qkv_w], gi[:, qkv_w + memw:qkv_w + memw + n_fx],
                                         gi[:, qkv_w:qkv_w + memw]], axis=1)
    grad_x = dh
    g_mem_norm = _mem_norm_bwd(mem[0], g_mem, jnp.stack(dmem_n))

    axis_of = dict(BIG)
    partials = [jnp.stack([jnp.stack([
        _pack_rows(layout[b], b, {n: jnp.concatenate([_slab(gl[hf * Lh + li][n], axis_of[n], j) for li in range(Lh)])
                                  for n in layout[b][0]}, BF16)
        for j in range(N_CHIPS)]) for hf in range(2)]) for b in widths]
    pairs = [_pair_sum(g, sib, c_arr) for g, sib in zip(partials, _pair_exchange(partials))]
    mines = [_chip_sum(p, r, chip_arr) for p, r in zip(pairs, _chip_exchange(pairs))]
    theirs = _pair_swap(mines)

    grad, delta, new_m, new_v = {}, {}, {}, {}
    for n, _ in BIG:
        k = widths.index(shard_shapes[n][1])
        grad[n], delta[n], new_m[n], new_v[n] = _adamw_reduced(
            w[n], m[n], v[n], mines[k], theirs[k], c_arr, layout[widths[k]][1][n], name="adamw_" + n)

    small_local = {n: (g_mem_norm if n == "mem_norm_g" else
                       jnp.concatenate([gl[l][n][:, :n_fx] if n == "b_forget" else gl[l][n] for l in range(L)]))
                   for n in SMALL}
    small_shapes = [small_local[n].shape for n in SMALL]
    small_sum = _unpack_flat(_all_reduce_small(_pack_flat([small_local[n] for n in SMALL], F32, row_block=16)),
                             small_shapes)
    for n, t in zip(SMALL, small_sum):
        shp = w[n].shape
        two_d = (1, shp[0]) if len(shp) == 1 else shp
        grad[n] = t.reshape(shp)
        d_, m_, v_ = _adamw(w[n].reshape(two_d), t.reshape(two_d), m[n].reshape(two_d), v[n].reshape(two_d),
                            name="adamw_" + n)
        delta[n], new_m[n], new_v[n] = d_.reshape(shp), m_.reshape(shp), v_.reshape(shp)

    return (loss, grad_x[None], *[grad[n] for n in WEIGHTS], *[delta[n] for n in WEIGHTS],
            *[new_m[n] for n in WEIGHTS], *[new_v[n] for n in WEIGHTS])
```

```python
import math

import jax
import jax.numpy as jnp
from jax import lax
from jax.experimental import pallas as pl
from jax.experimental.pallas import tpu as pltpu

F32 = jnp.float32
BF16 = jnp.bfloat16
RMS_EPS = 1e-6
HEAD_DIM = 64
MEM_HEAD_DIM = 128
LANE = 128
V7X_VMEM_LIMIT_BYTES = 56 * 1024 * 1024
FLAT_COLS = 512
FLAT_UNIT = 16 * FLAT_COLS
FLAT_ROW_BLOCK = 512
N_CHIPS = 4
N_DEV = 8
NEG = float(jnp.finfo(jnp.float32).min)

ADAM_LR = 0.001
ADAM_B1 = 0.9
ADAM_B2 = 0.999
ADAM_EPS = 1e-08
ADAM_WD = 0.01
ADAM_STEP = 10

BIG = (("ffn1_w_gate", 2), ("ffn1_w_up", 2), ("ffn1_w_down", 1), ("w_in", 2), ("w_mem_kv", 1), ("w_gate", 2),
       ("w_br_sb", 2), ("w_br_fox", 2), ("w_br_mem", 2), ("w_out", 1),
       ("ffn2_w_gate", 2), ("ffn2_w_up", 2), ("ffn2_w_down", 1))
SMALL = ("ffn1_pre_g", "ffn1_post_g", "mix_pre_g", "mix_post_g", "b_forget", "mem_norm_g", "b_gate",
         "ffn2_pre_g", "ffn2_post_g")
WEIGHTS = ("ffn1_pre_g", "ffn1_post_g", "ffn1_w_gate", "ffn1_w_up", "ffn1_w_down", "mix_pre_g", "mix_post_g", "w_in",
           "b_forget", "mem_norm_g", "w_mem_kv", "w_gate", "b_gate", "w_br_sb", "w_br_fox", "w_br_mem", "w_out",
           "ffn2_pre_g", "ffn2_post_g", "ffn2_w_gate", "ffn2_w_up", "ffn2_w_down")


def _params(**kw):
    return pltpu.CompilerParams(vmem_limit_bytes=V7X_VMEM_LIMIT_BYTES, **kw)


def _dot(a, b):
    return jnp.dot(a, b, preferred_element_type=F32)


def _dot_nt(a, b):
    return lax.dot_general(a, b, (((1,), (1,)), ((), ())), preferred_element_type=F32)


def _dot_tn(a, b):
    return lax.dot_general(a, b, (((0,), (0,)), ((), ())), preferred_element_type=F32)


def _rms(t, g):
    return t * lax.rsqrt(jnp.mean(t * t, axis=-1, keepdims=True) + RMS_EPS) * g


def _pick(dim, pref):
    if dim <= pref:
        return dim
    for cand in range(pref - pref % LANE, 0, -LANE):
        if dim % cand == 0:
            return cand
    return dim


def _rows(bm, cols):
    return pl.BlockSpec((bm, cols), lambda i: (i, 0))


def _whole(shape):
    nd = len(shape)
    return pl.BlockSpec(shape, lambda i: (0,) * nd)


def _split3(x):
    hi = x.astype(BF16)
    r1 = x - hi.astype(F32)
    mid = r1.astype(BF16)
    lo = (r1 - mid.astype(F32)).astype(BF16)
    return hi, mid, lo


def _cumdot(x, tri):
    hi = x.astype(BF16)
    lo = (x - hi.astype(F32)).astype(BF16)
    return _dot(hi, tri) + _dot(lo, tri)


def _slabs(bm, cols):
    return pl.BlockSpec((N_CHIPS, bm, cols), lambda i: (0, i, 0))


def _ffn_fwd_up(h, g_pre, wg, wu):
    S, D = h.shape
    Fs = wg.shape[2]
    bm = _pick(S, 256)

    def body(h_ref, g_ref, wg_ref, wu_ref, n_ref, gate_ref, up_ref, a_ref):
        n = _rms(h_ref[...], g_ref[...]).astype(BF16)
        n_ref[...] = n
        for j in range(N_CHIPS):
            gate = _dot(n, wg_ref[j])
            up = _dot(n, wu_ref[j])
            gate_ref[j] = gate.astype(BF16)
            up_ref[j] = up.astype(BF16)
            a_ref[j] = (gate * jax.nn.sigmoid(gate) * up).astype(BF16)

    return pl.pallas_call(
        body, name="ffn_fwd_up", grid=(S // bm,),
        in_specs=[_rows(bm, D), _whole((1, D)), _whole((N_CHIPS, D, Fs)), _whole((N_CHIPS, D, Fs))],
        out_specs=[_rows(bm, D), _slabs(bm, Fs), _slabs(bm, Fs), _slabs(bm, Fs)],
        out_shape=[jax.ShapeDtypeStruct((S, D), BF16)] + [jax.ShapeDtypeStruct((N_CHIPS, S, Fs), BF16)] * 3,
        compiler_params=_params(),
    )(h, g_pre, wg, wu)


def _ffn_fwd_down(a, wd, h, g_post):
    _, S, Fs = a.shape
    D = wd.shape[2]
    bm = _pick(S, 256)

    def body(a_ref, wd_ref, h_ref, g_ref, hout_ref, f_ref):
        f = _dot(a_ref[0], wd_ref[0])
        for j in range(1, N_CHIPS):
            f = f + _dot(a_ref[j], wd_ref[j])
        f_ref[...] = f
        hout_ref[...] = h_ref[...] + 0.5 * _rms(f, g_ref[...])

    return pl.pallas_call(
        body, name="ffn_fwd_down", grid=(S // bm,),
        in_specs=[_slabs(bm, Fs), _whole((N_CHIPS, Fs, D)), _rows(bm, D), _whole((1, D))],
        out_specs=[_rows(bm, D), _rows(bm, D)],
        out_shape=[jax.ShapeDtypeStruct((S, D), F32)] * 2,
        compiler_params=_params(),
    )(a, wd, h, g_post)


def _ffn_bwd_down(dh, f, g_post, wd, gate, up):
    S, D = dh.shape
    Fs = wd.shape[1]
    bm = _pick(S, 256)

    def body(dh_ref, f_ref, g_ref, wd_ref, gate_ref, up_ref, df_ref, dgate_ref, dup_ref, dg_ref):
        _, vjp = jax.vjp(lambda t, g: 0.5 * _rms(t, g), f_ref[...], g_ref[...])
        df, dg = vjp(dh_ref[...])

        @pl.when(pl.program_id(0) == 0)
        def _():
            dg_ref[...] = jnp.zeros_like(dg_ref)

        dg_ref[...] += dg
        dfb = df.astype(BF16)
        df_ref[...] = dfb
        for j in range(N_CHIPS):
            da = _dot_nt(dfb, wd_ref[j])
            gt = gate_ref[j].astype(F32)
            sig = jax.nn.sigmoid(gt)
            silu = gt * sig
            dup_ref[j] = (da * silu).astype(BF16)
            dgate_ref[j] = (da * up_ref[j].astype(F32) * (sig + silu * (1.0 - sig))).astype(BF16)

    return pl.pallas_call(
        body, name="ffn_bwd_down", grid=(S // bm,),
        in_specs=[_rows(bm, D), _rows(bm, D), _whole((1, D)), _whole((N_CHIPS, Fs, D)), _slabs(bm, Fs),
                  _slabs(bm, Fs)],
        out_specs=[_rows(bm, D), _slabs(bm, Fs), _slabs(bm, Fs), _whole((1, D))],
        out_shape=[jax.ShapeDtypeStruct((S, D), BF16), jax.ShapeDtypeStruct((N_CHIPS, S, Fs), BF16),
                   jax.ShapeDtypeStruct((N_CHIPS, S, Fs), BF16), jax.ShapeDtypeStruct((1, D), F32)],
        compiler_params=_params(),
    )(dh, f, g_post, wd, gate, up)


def _ffn_bwd_up(dgate, dup, wg, wu, h_in, g_pre, dh):
    _, S, Fs = dgate.shape
    D = wg.shape[1]
    bm = _pick(S, 256)

    def body(dgate_ref, dup_ref, wg_ref, wu_ref, h_ref, g_ref, dh_ref, dhin_ref, dg_ref):
        dn = _dot_nt(dgate_ref[0], wg_ref[0]) + _dot_nt(dup_ref[0], wu_ref[0])
        for j in range(1, N_CHIPS):
            dn = dn + _dot_nt(dgate_ref[j], wg_ref[j]) + _dot_nt(dup_ref[j], wu_ref[j])
        _, vjp = jax.vjp(_rms, h_ref[...], g_ref[...])
        dhx, dg = vjp(dn)

        @pl.when(pl.program_id(0) == 0)
        def _():
            dg_ref[...] = jnp.zeros_like(dg_ref)

        dg_ref[...] += dg
        dhin_ref[...] = dh_ref[...] + dhx

    return pl.pallas_call(
        body, name="ffn_bwd_up", grid=(S // bm,),
        in_specs=[_slabs(bm, Fs), _slabs(bm, Fs), _whole((N_CHIPS, D, Fs)), _whole((N_CHIPS, D, Fs)), _rows(bm, D),
                  _whole((1, D)), _rows(bm, D)],
        out_specs=[_rows(bm, D), _whole((1, D))],
        out_shape=[jax.ShapeDtypeStruct((S, D), F32), jax.ShapeDtypeStruct((1, D), F32)],
        compiler_params=_params(),
    )(dgate, dup, wg, wu, h_in, g_pre, dh)


def _matmul(a, b, *, ta=False, tb=False, out_dtype=BF16, name, batch=None):
    n_batch = a.shape[0] if batch == "a" else b.shape[0] if batch == "b" else 1
    a_shape = a.shape[1:] if batch == "a" else a.shape
    b_shape = b.shape[1:] if batch == "b" else b.shape
    M, K = (a_shape[1], a_shape[0]) if ta else a_shape
    N = b_shape[0] if tb else b_shape[1]
    acc_budget = 12 * 1024 * 1024
    bm, bk = _pick(M, 1536), _pick(K, 512)
    while n_batch * N * bm * 4 > acc_budget and bm % (2 * LANE) == 0:
        bm //= 2
    bn = N if n_batch * N * bm * 4 <= acc_budget else _pick(N, 1536)
    nk = K // bk

    def body(a_ref, b_ref, o_ref, acc_ref):
        kk = pl.program_id(2)

        @pl.when(kk == 0)
        def _():
            acc_ref[...] = jnp.zeros_like(acc_ref)

        dims = (((0 if ta else 1,), (1 if tb else 0,)), ((), ()))
        if batch is None:
            acc_ref[...] += lax.dot_general(a_ref[...], b_ref[...], dims, preferred_element_type=F32)
        else:
            for g in range(n_batch):
                av = a_ref[g] if batch == "a" else a_ref[...]
                bv = b_ref[g] if batch == "b" else b_ref[...]
                acc_ref[g] += lax.dot_general(av, bv, dims, preferred_element_type=F32)

        @pl.when(kk == nk - 1)
        def _():
            o_ref[...] = acc_ref[...].astype(o_ref.dtype)

    def spec(block, index, batched):
        if batched:
            return pl.BlockSpec((n_batch,) + block, lambda i, j, k: (0,) + index(i, j, k))
        return pl.BlockSpec(block, index)

    a_spec = spec((bk, bm), lambda i, j, k: (k, i), batch == "a") if ta else \
        spec((bm, bk), lambda i, j, k: (i, k), batch == "a")
    b_spec = spec((bn, bk), lambda i, j, k: (j, k), batch == "b") if tb else \
        spec((bk, bn), lambda i, j, k: (k, j), batch == "b")
    lead = (n_batch,) if batch else ()
    return pl.pallas_call(
        body, name=name, grid=(M // bm, N // bn, nk),
        in_specs=[a_spec, b_spec],
        out_specs=spec((bm, bn), lambda i, j, k: (i, j), batch is not None),
        out_shape=jax.ShapeDtypeStruct(lead + (M, N), out_dtype),
        scratch_shapes=[pltpu.VMEM(lead + (bm, bn), F32)],
        compiler_params=_params(),
    )(a, b)


def _mix_fwd_in(h, g_pre, win, wgate, b_gate, b_forget):
    S, D = h.shape
    PW = win.shape[1] - LANE
    G = wgate.shape[1]
    bm = _pick(S, 256)

    def body(h_ref, g_ref, win_ref, wgate_ref, bg_ref, bf_ref, u_ref, proj_ref, fl_ref, sg_ref):
        u = _rms(h_ref[...], g_ref[...]).astype(BF16)
        u_ref[...] = u
        proj = _dot(u, win_ref[...])
        proj_ref[...] = proj[:, :PW].astype(BF16)
        fl_ref[...] = proj[:, PW:] + bf_ref[...]
        sg_ref[...] = jax.nn.sigmoid(_dot(u, wgate_ref[...]) + bg_ref[...]).astype(BF16)

    return pl.pallas_call(
        body, name="mix_fwd_in", grid=(S // bm,),
        in_specs=[_rows(bm, D), _whole((1, D)), _whole((D, PW + LANE)), _whole((D, G)), _whole((1, G)),
                  _whole((1, LANE))],
        out_specs=[_rows(bm, D), _rows(bm, PW), _rows(bm, LANE), _rows(bm, G)],
        out_shape=[jax.ShapeDtypeStruct((S, D), BF16), jax.ShapeDtypeStruct((S, PW), BF16),
                   jax.ShapeDtypeStruct((S, LANE), F32), jax.ShapeDtypeStruct((S, G), BF16)],
        compiler_params=_params(),
    )(h, g_pre, win, wgate, b_gate, b_forget)


def _mix_fwd_out(o_sb, o_fx, o_mem, sg, w_sb, w_fx, w_mem, w_out, h, g_post):
    S, D = h.shape
    bm = _pick(S, 256)
    widths = (o_sb.shape[1], o_fx.shape[1], o_mem.shape[1])

    def body(osb_ref, ofx_ref, omem_ref, sg_ref, wsb_ref, wfx_ref, wmem_ref, wout_ref, h_ref, g_ref,
             hout_ref, z_ref, merged_ref):
        s = sg_ref[...].astype(F32)
        merged = (s[:, :D] * _dot(osb_ref[...], wsb_ref[...]) + s[:, D:2 * D] * _dot(ofx_ref[...], wfx_ref[...])
                  + s[:, 2 * D:] * _dot(omem_ref[...], wmem_ref[...]))
        mb = merged.astype(BF16)
        merged_ref[...] = mb
        z = _dot(mb, wout_ref[...])
        z_ref[...] = z
        hout_ref[...] = h_ref[...] + _rms(z, g_ref[...])

    return pl.pallas_call(
        body, name="mix_fwd_out", grid=(S // bm,),
        in_specs=[_rows(bm, widths[0]), _rows(bm, widths[1]), _rows(bm, widths[2]), _rows(bm, 3 * D),
                  _whole((widths[0], D)), _whole((widths[1], D)), _whole((widths[2], D)), _whole((D, D)),
                  _rows(bm, D), _whole((1, D))],
        out_specs=[_rows(bm, D), _rows(bm, D), _rows(bm, D)],
        out_shape=[jax.ShapeDtypeStruct((S, D), F32), jax.ShapeDtypeStruct((S, D), F32),
                   jax.ShapeDtypeStruct((S, D), BF16)],
        compiler_params=_params(),
    )(o_sb, o_fx, o_mem, sg, w_sb, w_fx, w_mem, w_out, h, g_post)


def _mix_bwd_out(dh, z, g_post, w_out, o_sb, o_fx, o_mem, w_sb, w_fx, w_mem, sg):
    S, D = dh.shape
    bm = _pick(S, 256)
    widths = (o_sb.shape[1], o_fx.shape[1], o_mem.shape[1])

    def body(dh_ref, z_ref, g_ref, wout_ref, osb_ref, ofx_ref, omem_ref, wsb_ref, wfx_ref, wmem_ref, sg_ref,
             dz_ref, dbsb_ref, dbfx_ref, dbmem_ref, dosb_ref, dofx_ref, domem_ref, dgp_ref, dbg_ref, dg_ref):
        _, vjp = jax.vjp(_rms, z_ref[...], g_ref[...])
        dz, dg = vjp(dh_ref[...])

        @pl.when(pl.program_id(0) == 0)
        def _():
            dg_ref[...] = jnp.zeros_like(dg_ref)
            dbg_ref[...] = jnp.zeros_like(dbg_ref)

        dg_ref[...] += dg
        dzb = dz.astype(BF16)
        dz_ref[...] = dzb
        dmerged = _dot_nt(dzb, wout_ref[...])
        s = sg_ref[...].astype(F32)
        branches = ((osb_ref, wsb_ref, dbsb_ref, dosb_ref), (ofx_ref, wfx_ref, dbfx_ref, dofx_ref),
                    (omem_ref, wmem_ref, dbmem_ref, domem_ref))
        for k, (o_ref, w_ref, db_ref, do_ref) in enumerate(branches):
            gs = s[:, k * D:(k + 1) * D]
            dbb = (dmerged * gs).astype(BF16)
            db_ref[...] = dbb
            do_ref[...] = _dot_nt(dbb, w_ref[...]).astype(BF16)
            dgp = dmerged * _dot(o_ref[...], w_ref[...]) * gs * (1.0 - gs)
            dgp_ref[:, k * D:(k + 1) * D] = dgp.astype(BF16)
            dbg_ref[:, k * D:(k + 1) * D] += jnp.sum(dgp, axis=0, keepdims=True)

    return pl.pallas_call(
        body, name="mix_bwd_out", grid=(S // bm,),
        in_specs=[_rows(bm, D), _rows(bm, D), _whole((1, D)), _whole((D, D)),
                  _rows(bm, widths[0]), _rows(bm, widths[1]), _rows(bm, widths[2]),
                  _whole((widths[0], D)), _whole((widths[1], D)), _whole((widths[2], D)), _rows(bm, 3 * D)],
        out_specs=[_rows(bm, D)] * 4 + [_rows(bm, widths[0]), _rows(bm, widths[1]), _rows(bm, widths[2]),
                                        _rows(bm, 3 * D), _whole((1, 3 * D)), _whole((1, D))],
        out_shape=[jax.ShapeDtypeStruct((S, D), BF16)] * 4
        + [jax.ShapeDtypeStruct((S, w), BF16) for w in widths]
        + [jax.ShapeDtypeStruct((S, 3 * D), BF16), jax.ShapeDtypeStruct((1, 3 * D), F32),
           jax.ShapeDtypeStruct((1, D), F32)],
        compiler_params=_params(),
    )(dh, z, g_post, w_out, o_sb, o_fx, o_mem, w_sb, w_fx, w_mem, sg)


def _mix_bwd_in(dproj, dgp, win, wgate, h_in, g_pre, dh):
    S, PWL = dproj.shape
    G = dgp.shape[1]
    D = h_in.shape[1]
    bm = _pick(S, 256)

    def body(dproj_ref, dgp_ref, win_ref, wgate_ref, h_ref, g_ref, dh_ref, dhin_ref, dg_ref):
        du = _dot_nt(dproj_ref[...], win_ref[...]) + _dot_nt(dgp_ref[...], wgate_ref[...])
        _, vjp = jax.vjp(_rms, h_ref[...], g_ref[...])
        dhx, dg = vjp(du)

        @pl.when(pl.program_id(0) == 0)
        def _():
            dg_ref[...] = jnp.zeros_like(dg_ref)

        dg_ref[...] += dg
        dhin_ref[...] = dh_ref[...] + dhx

    return pl.pallas_call(
        body, name="mix_bwd_in", grid=(S // bm,),
        in_specs=[_rows(bm, PWL), _rows(bm, G), _whole((D, PWL)), _whole((D, G)), _rows(bm, D), _whole((1, D)),
                  _rows(bm, D)],
        out_specs=[_rows(bm, D), _whole((1, D))],
        out_shape=[jax.ShapeDtypeStruct((S, D), F32), jax.ShapeDtypeStruct((1, D), F32)],
        compiler_params=_params(),
    )(dproj, dgp, win, wgate, h_in, g_pre, dh)


def _log_sigmoid(x):
    return jnp.minimum(x, 0.0) - jnp.log(1.0 + jnp.exp(-jnp.abs(x)))


def _fox_cumsum(fl):
    S = fl.shape[0]
    rb = _pick(S, LANE)

    def body(fl_ref, c_ref, carry_ref):
        @pl.when(pl.program_id(0) == 0)
        def _():
            carry_ref[...] = jnp.zeros_like(carry_ref)

        r = lax.broadcasted_iota(jnp.int32, (rb, rb), 0)
        cidx = lax.broadcasted_iota(jnp.int32, (rb, rb), 1)
        tri = (cidx <= r).astype(BF16)
        hi, mid, lo = _split3(_log_sigmoid(fl_ref[...]))
        c = _dot(tri, hi) + _dot(tri, mid) + _dot(tri, lo) + carry_ref[...]
        c_ref[...] = c
        carry_ref[...] = c[rb - 1:rb, :]

    return pl.pallas_call(
        body, name="fox_cumsum", grid=(S // rb,),
        in_specs=[_rows(rb, LANE)], out_specs=_rows(rb, LANE),
        out_shape=jax.ShapeDtypeStruct((S, LANE), F32),
        scratch_shapes=[pltpu.VMEM((1, LANE), F32)],
        compiler_params=_params(),
    )(fl)


def _fox_dlogit(dc, fl):
    S = fl.shape[0]
    rb = _pick(S, LANE)
    nb = S // rb

    def body(dc_ref, fl_ref, dfl_ref, dbf_ref, carry_ref):
        @pl.when(pl.program_id(0) == 0)
        def _():
            carry_ref[...] = jnp.zeros_like(carry_ref)
            dbf_ref[...] = jnp.zeros_like(dbf_ref)

        r = lax.broadcasted_iota(jnp.int32, (rb, rb), 0)
        cidx = lax.broadcasted_iota(jnp.int32, (rb, rb), 1)
        tri = (cidx >= r).astype(BF16)
        hi, mid, lo = _split3(dc_ref[...])
        rc = _dot(tri, hi) + _dot(tri, mid) + _dot(tri, lo) + carry_ref[...]
        carry_ref[...] = rc[0:1, :]
        dfl = rc * jax.nn.sigmoid(-fl_ref[...])
        dfl_ref[...] = dfl.astype(BF16)
        dbf_ref[...] += jnp.sum(dfl, axis=0, keepdims=True)

    rev = pl.BlockSpec((rb, LANE), lambda i: (nb - 1 - i, 0))
    return pl.pallas_call(
        body, name="fox_dlogit", grid=(nb,),
        in_specs=[rev, rev], out_specs=[rev, _whole((1, LANE))],
        out_shape=[jax.ShapeDtypeStruct((S, LANE), BF16), jax.ShapeDtypeStruct((1, LANE), F32)],
        scratch_shapes=[pltpu.VMEM((1, LANE), F32)],
        compiler_params=_params(),
    )(dc, fl)


def _attn_blocks(kind, S, Sk, backward=False):
    tq = _pick(S, 1024 if backward else 2048)
    tc = LANE if kind == "sb" else _pick(Sk, 256)
    return tq, tc


def _is_power_of_two(x):
    return math.frexp(x)[0] == 0.5


def _sb_logs(z):
    ln = -jnp.maximum(z, 0.0) - jnp.log(1.0 + jnp.exp(-jnp.abs(z)))
    return ln + z, ln


def _head_lanes(pack, dh):
    lane = lax.broadcasted_iota(jnp.int32, (1, LANE), 1)
    return [(lane >= hh * dh) & (lane < (hh + 1) * dh) for hh in range(pack)]


def _by_head(sel, parts):
    out = parts[0]
    for hh in range(1, len(parts)):
        out = jnp.where(sel[hh], parts[hh], out)
    return out


def _only_head(sel, hh, x):
    return x if len(sel) == 1 else jnp.where(sel[hh], x, jnp.zeros_like(x))


def _tail(x, r0):
    return x if not r0 else x[r0:]


def _put_tail(x, tail, r0):
    return tail if not r0 else jnp.concatenate([x[:r0], tail], axis=0)


def _add_tail(x, tail, r0):
    return x + tail if not r0 else jnp.concatenate([x[:r0], x[r0:] + tail], axis=0)


def _q_cols(tq, first):
    return pl.BlockSpec((tq, LANE), lambda g, i: (i, first // LANE + g))


def _k_cols(rows, first):
    return pl.BlockSpec((rows, LANE), lambda g, i: (0, first // LANE + g))


def _attn_fwd(kind, q, k, v, n_heads, dh, ccol=None, crow=None):
    (qa, q0), (ka, k0), (va, v0) = q, k, v
    S, Sk = qa.shape[0], ka.shape[0]
    pack = LANE // dh
    tq, tc = _attn_blocks(kind, S, Sk)
    scale = dh ** -0.5
    fold = _is_power_of_two(scale)
    causal = kind != "mem"
    n_diag = tq // tc if causal else 0
    unroll = 2 if causal else 1
    assert n_diag % unroll == 0 and (Sk // tc) % unroll == 0

    def body(*refs):
        if kind == "fox":
            q_ref, k_ref, v_ref, cc_ref, cr_ref, o_ref, lse_ref = refs
        else:
            q_ref, k_ref, v_ref, o_ref, lse_ref = refs
        i = pl.program_id(1)
        n_full = (i * tq) // tc if causal else Sk // tc
        qpos = i * tq + lax.broadcasted_iota(jnp.int32, (tq, tc), 0)
        kio = lax.broadcasted_iota(jnp.int32, (tq, tc), 1)
        heads = range(pack)
        sel = _head_lanes(pack, dh)
        q2 = q_ref[...] * scale if fold else q_ref[...]
        qs = [_only_head(sel, hh, q2) for hh in heads]

        def kv(jc):
            off = pl.multiple_of(jc * tc, tc)
            return off, k_ref[pl.ds(off, tc), :], v_ref[pl.ds(off, tc), :]

        if kind == "sb":
            tri = (lax.broadcasted_iota(jnp.int32, (tc, tc), 0) > lax.broadcasted_iota(jnp.int32, (tc, tc), 1)
                   ).astype(BF16)

            def chunk(jc, r0, runs, acc):
                off, k2, v2 = kv(jc)
                new_runs, pv = [], []
                for hh in heads:
                    lb, ln = _sb_logs(_dot_nt(_tail(qs[hh], r0), k2))
                    if r0 is not None:
                        mask = (off + _tail(kio, r0)) < _tail(qpos, r0)
                        ln = jnp.where(mask, ln, 0.0)
                    w = jnp.exp(lb + _cumdot(ln, tri) + _tail(runs[hh], r0))
                    if r0 is not None:
                        w = jnp.where(mask, w, 0.0)
                    pv.append(_dot(w.astype(BF16), v2))
                    new_runs.append(_add_tail(runs[hh], jnp.sum(ln, axis=1, keepdims=True), r0))
                return tuple(new_runs), _add_tail(acc, _by_head(sel, pv), r0)

            state = (tuple(jnp.zeros((tq, 1), F32) for _ in heads), jnp.zeros((tq, LANE), F32))
            for d in range(n_diag - 1, -1, -1):
                state = chunk(n_full + d, d * tc, *state)

            def trip(t, st):
                for u in range(unroll):
                    st = chunk(n_full - 1 - unroll * t - u, None, *st)
                return st

            runs, acc = lax.fori_loop(0, n_full // unroll, trip, state)
            o_ref[...] = acc.astype(o_ref.dtype)
            for hh in heads:
                lse_ref[hh] = runs[hh]
        else:
            def chunk(jc, r0, ms, ls, acc):
                off, k2, v2 = kv(jc)
                new_ms, new_ls, alphas, pv = [], [], [], []
                for hh in heads:
                    z = _dot_nt(_tail(qs[hh], r0), k2)
                    if not fold:
                        z = z * scale
                    if kind == "fox":
                        z = z + _tail(cc_ref[hh], r0) - cr_ref[hh, pl.ds(jc, 1), :]
                    if r0 is not None:
                        z = jnp.where((off + _tail(kio, r0)) <= _tail(qpos, r0), z, NEG)
                    m_old, l_old = _tail(ms[hh], r0), _tail(ls[hh], r0)
                    m_new = jnp.maximum(m_old, jnp.max(z, axis=1, keepdims=True))
                    alpha = jnp.exp(m_old - m_new)
                    p = jnp.exp(z - m_new)
                    new_ms.append(_put_tail(ms[hh], m_new, r0))
                    new_ls.append(_put_tail(ls[hh], alpha * l_old + jnp.sum(p, axis=1, keepdims=True), r0))
                    alphas.append(alpha)
                    pv.append(_dot(p.astype(BF16), v2))
                acc_new = _by_head(sel, alphas) * _tail(acc, r0) + _by_head(sel, pv)
                return tuple(new_ms), tuple(new_ls), _put_tail(acc, acc_new, r0)

            state = (tuple(jnp.full((tq, 1), NEG, F32) for _ in heads), tuple(jnp.zeros((tq, 1), F32) for _ in heads),
                     jnp.zeros((tq, LANE), F32))

            def trip(t, st):
                for u in range(unroll):
                    st = chunk(unroll * t + u, None, *st)
                return st

            state = lax.fori_loop(0, n_full // unroll, trip, state)
            for d in range(n_diag):
                state = chunk(n_full + d, d * tc, *state)
            ms, ls, acc = state
            o_ref[...] = (acc / _by_head(sel, ls)).astype(o_ref.dtype)
            for hh in heads:
                lse_ref[hh] = ms[hh] + jnp.log(ls[hh])

    colspec = pl.BlockSpec((pack, tq, 1), lambda g, i: (g, i, 0))
    in_specs, args = [_q_cols(tq, q0), _k_cols(Sk, k0), _k_cols(Sk, v0)], [qa, ka, va]
    if kind == "fox":
        in_specs += [colspec, pl.BlockSpec((pack, Sk // tc, tc), lambda g, i: (g, 0, 0))]
        args += [ccol, crow]
    return pl.pallas_call(
        body, name="attn_fwd_" + kind, grid=(n_heads // pack, S // tq),
        in_specs=in_specs, out_specs=[_q_cols(tq, 0), colspec],
        out_shape=[jax.ShapeDtypeStruct((S, n_heads * dh), BF16), jax.ShapeDtypeStruct((n_heads, S, 1), F32)],
        compiler_params=_params(),
    )(*args)


def _attn_bwd(kind, q, k, v, o, do, n_heads, dh, ccol=None, crow=None, lse=None):
    (qa, q0), (ka, k0), (va, v0) = q, k, v
    S, Sk = qa.shape[0], ka.shape[0]
    pack = LANE // dh
    tq, tc = _attn_blocks(kind, S, Sk, backward=True)
    scale = dh ** -0.5
    fold = _is_power_of_two(scale)
    causal = kind != "mem"
    n_diag = tq // tc if causal else 0
    unroll = 4 if kind == "sb" else 1
    assert n_diag % unroll == 0 and (Sk // tc) % unroll == 0
    nq = S // tq

    def body(*refs):
        if kind == "fox":
            (q_ref, k_ref, v_ref, o_ref, do_ref, cc_ref, cr_ref, lse_ref,
             dq_ref, dk_ref, dv_ref, dc_ref, dcc_ref, dk_acc, dv_acc, dc_acc) = refs
        else:
            q_ref, k_ref, v_ref, o_ref, do_ref, lse_ref, dq_ref, dk_ref, dv_ref, dk_acc, dv_acc = refs
        i = pl.program_id(1)

        @pl.when(i == 0)
        def _():
            dk_acc[...] = jnp.zeros_like(dk_acc)
            dv_acc[...] = jnp.zeros_like(dv_acc)
            if kind == "fox":
                dc_acc[...] = jnp.zeros_like(dc_acc)

        n_full = (i * tq) // tc if causal else Sk // tc
        qpos = i * tq + lax.broadcasted_iota(jnp.int32, (tq, tc), 0)
        kio = lax.broadcasted_iota(jnp.int32, (tq, tc), 1)
        heads = range(pack)
        sel = _head_lanes(pack, dh)
        q2 = q_ref[...] * scale if fold else q_ref[...]
        do2 = do_ref[...]
        qs = [_only_head(sel, hh, q2) for hh in heads]
        dos = [_only_head(sel, hh, do2) for hh in heads]

        def kv(jc):
            off = pl.multiple_of(jc * tc, tc)
            return off, k_ref[pl.ds(off, tc), :], v_ref[pl.ds(off, tc), :]

        def accumulate(off, k2, dzb, wb, dq, r0):
            q2t, do2t = _tail(q2, r0), _tail(do2, r0)
            dk_acc[pl.ds(off, tc), :] += _by_head(sel, [_dot_tn(dzb[hh], q2t) for hh in heads])
            dv_acc[pl.ds(off, tc), :] += _by_head(sel, [_dot_tn(wb[hh], do2t) for hh in heads])
            return _add_tail(dq, _by_head(sel, [_dot(dzb[hh], k2) for hh in heads]), r0)

        if kind == "sb":
            r = lax.broadcasted_iota(jnp.int32, (tc, tc), 0)
            cidx = lax.broadcasted_iota(jnp.int32, (tc, tc), 1)
            tri_inc = (r <= cidx).astype(BF16)
            tri_exc = (r < cidx).astype(BF16)

            def chunk(jc, r0, pres, pres_e, dq):
                off, k2, v2 = kv(jc)
                new_pres, new_pres_e, dzb, wb = [], [], [], []
                for hh in heads:
                    lb, ln = _sb_logs(_dot_nt(_tail(qs[hh], r0), k2))
                    if r0 is not None:
                        mask = (off + _tail(kio, r0)) < _tail(qpos, r0)
                        ln = jnp.where(mask, ln, 0.0)
                    w = jnp.exp(lb + (_tail(lse_ref[hh], r0) - _tail(pres[hh], r0) - _cumdot(ln, tri_inc)))
                    if r0 is not None:
                        w = jnp.where(mask, w, 0.0)
                    e = w * _dot_nt(_tail(dos[hh], r0), v2)
                    beta = jnp.exp(lb)
                    dz = e * (1.0 - beta) - beta * (_tail(pres_e[hh], r0) + _dot(e.astype(BF16), tri_exc))
                    if r0 is not None:
                        dz = jnp.where(mask, dz, 0.0)
                    dzb.append(dz.astype(BF16))
                    wb.append(w.astype(BF16))
                    new_pres.append(_add_tail(pres[hh], jnp.sum(ln, axis=1, keepdims=True), r0))
                    new_pres_e.append(_add_tail(pres_e[hh], jnp.sum(e, axis=1, keepdims=True), r0))
                return tuple(new_pres), tuple(new_pres_e), accumulate(off, k2, dzb, wb, dq, r0)

            state = (tuple(jnp.zeros((tq, 1), F32) for _ in heads), tuple(jnp.zeros((tq, 1), F32) for _ in heads),
                     jnp.zeros((tq, LANE), F32))
        else:
            prod = o_ref[...].astype(F32) * do2.astype(F32)
            dsum = [jnp.sum(_only_head(sel, hh, prod), axis=1, keepdims=True) for hh in heads]

            def chunk(jc, r0, rowsums, dq):
                off, k2, v2 = kv(jc)
                new_rowsums, dsb, pb = [], [], []
                for hh in heads:
                    z = _dot_nt(_tail(qs[hh], r0), k2)
                    if not fold:
                        z = z * scale
                    if kind == "fox":
                        z = z + _tail(cc_ref[hh], r0) - cr_ref[hh, pl.ds(jc, 1), :]
                    if r0 is not None:
                        z = jnp.where((off + _tail(kio, r0)) <= _tail(qpos, r0), z, NEG)
                    p = jnp.exp(z - _tail(lse_ref[hh], r0))
                    ds = p * (_dot_nt(_tail(dos[hh], r0), v2) - _tail(dsum[hh], r0))
                    dsb.append(ds.astype(BF16))
                    pb.append(p.astype(BF16))
                    if kind == "fox":
                        dc_acc[hh, pl.ds(jc, 1), :] -= jnp.sum(ds, axis=0, keepdims=True)
                        new_rowsums.append(_add_tail(rowsums[hh], jnp.sum(ds, axis=1, keepdims=True), r0))
                    else:
                        new_rowsums.append(rowsums[hh])
                return tuple(new_rowsums), accumulate(off, k2, dsb, pb, dq, r0)

            state = (tuple(jnp.zeros((tq, 1), F32) for _ in heads), jnp.zeros((tq, LANE), F32))

        def trip(t, st):
            for u in range(unroll):
                st = chunk(unroll * t + u, None, *st)
            return st

        state = lax.fori_loop(0, n_full // unroll, trip, state)
        for d in range(n_diag):
            state = chunk(n_full + d, d * tc, *state)
        dq_ref[...] = (state[-1] * scale).astype(dq_ref.dtype)
        if kind == "fox":
            for hh in heads:
                dcc_ref[hh] = state[0][hh]

        @pl.when(i == nq - 1)
        def _():
            dk = dk_acc[...] if fold else dk_acc[...] * scale
            dk_ref[...] = dk.astype(dk_ref.dtype)
            dv_ref[...] = dv_acc[...].astype(dv_ref.dtype)
            if kind == "fox":
                dc_ref[...] = dc_acc[...]

    colspec = pl.BlockSpec((pack, tq, 1), lambda g, i: (g, i, 0))
    rowspec = pl.BlockSpec((pack, Sk // tc, tc), lambda g, i: (g, 0, 0))
    in_specs = [_q_cols(tq, q0), _k_cols(Sk, k0), _k_cols(Sk, v0), _q_cols(tq, 0), _q_cols(tq, 0)]
    args = [qa, ka, va, o, do]
    if kind == "fox":
        in_specs += [colspec, rowspec]
        args += [ccol, crow]
    in_specs += [colspec]
    args += [lse]
    width = n_heads * dh
    out_specs = [_q_cols(tq, 0), _k_cols(Sk, 0), _k_cols(Sk, 0)]
    out_shape = [jax.ShapeDtypeStruct((S, width), BF16), jax.ShapeDtypeStruct((Sk, width), BF16),
                 jax.ShapeDtypeStruct((Sk, width), BF16)]
    scratch = [pltpu.VMEM((Sk, LANE), F32), pltpu.VMEM((Sk, LANE), F32)]
    if kind == "fox":
        out_specs += [rowspec, colspec]
        out_shape += [jax.ShapeDtypeStruct((n_heads, Sk // tc, tc), F32), jax.ShapeDtypeStruct((n_heads, S, 1), F32)]
        scratch.append(pltpu.VMEM((pack, Sk // tc, tc), F32))
    return pl.pallas_call(
        body, name="attn_bwd_" + kind, grid=(n_heads // pack, nq),
        in_specs=in_specs, out_specs=out_specs, out_shape=out_shape, scratch_shapes=scratch,
        compiler_params=_params(),
    )(*args)


def _mem_norm(mem, g):
    M, D = mem.shape

    def body(mem_ref, g_ref, out_ref):
        out_ref[...] = _rms(mem_ref[...], g_ref[...]).astype(BF16)

    return pl.pallas_call(
        body, name="mem_norm", grid=(1,),
        in_specs=[_whole((M, D)), _whole((1, D))], out_specs=_whole((M, D)),
        out_shape=jax.ShapeDtypeStruct((M, D), BF16), compiler_params=_params(),
    )(mem, g)


def _mem_norm_bwd(mem, g, dmem_n):
    M, D = mem.shape
    L = dmem_n.shape[0]

    def body(mem_ref, g_ref, d_ref, dg_ref):
        d = d_ref[0]
        for l in range(1, L):
            d = d + d_ref[l]
        _, vjp = jax.vjp(_rms, mem_ref[...], g_ref[...])
        dg_ref[...] = vjp(d)[1]

    return pl.pallas_call(
        body, name="mem_norm_bwd", grid=(1,),
        in_specs=[_whole((M, D)), _whole((1, D)), _whole((L, M, D))], out_specs=_whole((1, D)),
        out_shape=jax.ShapeDtypeStruct((1, D), F32), compiler_params=_params(),
    )(mem, g, dmem_n)


def _loss_head(h, target):
    S, D = h.shape
    bm = _pick(S, 512)

    def body(h_ref, t_ref, dh_ref, loss_ref):
        err = h_ref[...] - t_ref[...]
        dh_ref[...] = err * (1.0 / D)

        @pl.when(pl.program_id(0) == 0)
        def _():
            loss_ref[...] = jnp.zeros_like(loss_ref)

        loss_ref[...] += 0.5 * jnp.sum(jnp.mean(err * err, axis=-1, keepdims=True), axis=0, keepdims=True)

    return pl.pallas_call(
        body, name="loss_head", grid=(S // bm,),
        in_specs=[_rows(bm, D), _rows(bm, D)], out_specs=[_rows(bm, D), _whole((8, LANE))],
        out_shape=[jax.ShapeDtypeStruct((S, D), F32), jax.ShapeDtypeStruct((8, LANE), F32)],
        compiler_params=_params(),
    )(h, target)


def _adamw(w, g, m, v, name):
    R, C = w.shape
    rb = R if R * C * 4 <= (1 << 20) else _pick(R, 256)
    if R % rb:
        rb = R
    c1 = 1.0 - ADAM_B1 ** ADAM_STEP
    c2 = 1.0 - ADAM_B2 ** ADAM_STEP

    def body(w_ref, g_ref, m_ref, v_ref, d_ref, mo_ref, vo_ref):
        gv = g_ref[...]
        mn = ADAM_B1 * m_ref[...] + (1.0 - ADAM_B1) * gv
        vn = ADAM_B2 * v_ref[...] + (1.0 - ADAM_B2) * (gv * gv)
        mo_ref[...] = mn
        vo_ref[...] = vn
        d_ref[...] = -ADAM_LR * ((mn / c1) / (jnp.sqrt(vn / c2) + ADAM_EPS) + ADAM_WD * w_ref[...])

    return pl.pallas_call(
        body, name=name, grid=(R // rb,),
        in_specs=[_rows(rb, C)] * 4, out_specs=[_rows(rb, C)] * 3,
        out_shape=[jax.ShapeDtypeStruct((R, C), F32)] * 3, compiler_params=_params(),
    )(w, g, m, v)


def _adamw_reduced(w, m, v, mine, theirs, c_idx, first_row, name):
    L, a, b = w.shape
    Lh = L // 2
    rb = _shard_row_block(a)
    nb = a // rb
    assert first_row % rb == 0
    c1 = 1.0 - ADAM_B1 ** ADAM_STEP
    c2 = 1.0 - ADAM_B2 ** ADAM_STEP

    def own(i, c_ref):
        return (i, 0)

    def reduced(i, c_ref):
        return (first_row // rb + ((i // nb) % Lh) * nb + i % nb, 0)

    def body(c_ref, w_ref, m_ref, v_ref, mine_ref, theirs_ref, g_ref, d_ref, mo_ref, vo_ref):
        half = (pl.program_id(0) // nb) // Lh
        gv = jnp.where(c_ref[0] == half, mine_ref[...], theirs_ref[...])
        g_ref[...] = gv
        mn = ADAM_B1 * m_ref[...] + (1.0 - ADAM_B1) * gv
        vn = ADAM_B2 * v_ref[...] + (1.0 - ADAM_B2) * (gv * gv)
        mo_ref[...] = mn
        vo_ref[...] = vn
        d_ref[...] = -ADAM_LR * ((mn / c1) / (jnp.sqrt(vn / c2) + ADAM_EPS) + ADAM_WD * w_ref[...])

    outs = pl.pallas_call(
        body, name=name,
        grid_spec=pltpu.PrefetchScalarGridSpec(
            num_scalar_prefetch=1, grid=(L * nb,),
            in_specs=[pl.BlockSpec((rb, b), own)] * 3 + [pl.BlockSpec((rb, b), reduced)] * 2,
            out_specs=[pl.BlockSpec((rb, b), own)] * 4),
        out_shape=[jax.ShapeDtypeStruct((L * a, b), F32)] * 4, compiler_params=_params(),
    )(c_idx, w.reshape(L * a, b), m.reshape(L * a, b), v.reshape(L * a, b), mine, theirs)
    return [t.reshape(L, a, b) for t in outs]


ANY = pl.BlockSpec(memory_space=pl.ANY)
MESH = pl.DeviceIdType.MESH


def _place():
    x, y, c = lax.axis_index("x"), lax.axis_index("y"), lax.axis_index("c")
    others = [(1 - x, y), (x, 1 - y), (1 - x, 1 - y)]
    return x, y, c, others


def _place_own(loc, chip_idx):
    _, R, C = loc.shape
    rb = _pick(R, 2 * FLAT_ROW_BLOCK)

    def body(chip_ref, loc_ref, out_ref):
        out_ref[...] = loc_ref[...]

    return pl.pallas_call(
        body, name="place_own",
        grid_spec=pltpu.PrefetchScalarGridSpec(
            num_scalar_prefetch=1, grid=(2, R // rb),
            in_specs=[pl.BlockSpec((None, rb, C), lambda hf, i, chip_ref: (hf, i, 0))],
            out_specs=pl.BlockSpec((None, None, rb, C), lambda hf, i, chip_ref: (chip_ref[0], hf, i, 0))),
        out_shape=jax.ShapeDtypeStruct((N_CHIPS, 2, R, C), loc.dtype), compiler_params=_params(),
    )(chip_idx, loc)


def _gather_weights(locs, owns):
    n = len(locs)

    def body(*refs):
        loc_refs, out_refs, (send_sems, recv_sems) = refs[:n], refs[2 * n:3 * n], refs[3 * n:]
        x, y, c, others = _place()
        me = 2 * x + y
        sibling = (x, y, 1 - c)

        def copy(a, k, src, dst, to):
            return pltpu.make_async_remote_copy(src_ref=src, dst_ref=dst, send_sem=send_sems.at[a, k],
                                                recv_sem=recv_sems.at[a, k], device_id=to, device_id_type=MESH)

        first = [copy(a, j, loc_refs[a].at[c], out_refs[a].at[me, c], (ox, oy, c))
                 for j, (ox, oy) in enumerate(others) for a in range(n)]
        for cp in first:
            cp.start()
        passed = []
        for j, (ox, oy) in enumerate(others):
            for a in range(n):
                landed = out_refs[a].at[2 * ox + oy, c]
                copy(a, j, loc_refs[a].at[c], landed, sibling).wait_recv()
                cp = copy(a, 3 + j, landed, landed, sibling)
                cp.start()
                passed.append(cp)
        for j, (ox, oy) in enumerate(others):
            for a in range(n):
                copy(a, 3 + j, loc_refs[a].at[c], out_refs[a].at[2 * ox + oy, 1 - c], sibling).wait_recv()
        for cp in first + passed:
            cp.wait_send()

    return pl.pallas_call(
        body, name="gather_weights", in_specs=[ANY] * (2 * n), out_specs=[ANY] * n,
        out_shape=[jax.ShapeDtypeStruct(own.shape, own.dtype) for own in owns],
        input_output_aliases={n + a: a for a in range(n)},
        scratch_shapes=[pltpu.SemaphoreType.DMA((n, 6)), pltpu.SemaphoreType.DMA((n, 6))],
    )(*locs, *owns)


def _pair_exchange(gs):
    n = len(gs)

    def body(*refs):
        g_refs, out_refs, (send_sems, recv_sems) = refs[:n], refs[n:2 * n], refs[2 * n:]
        x, y, c, _ = _place()
        copies = [pltpu.make_async_remote_copy(src_ref=g_refs[a].at[1 - c], dst_ref=out_refs[a],
                                               send_sem=send_sems.at[a], recv_sem=recv_sems.at[a],
                                               device_id=(x, y, 1 - c), device_id_type=MESH) for a in range(n)]
        for cp in copies:
            cp.start()
        for cp in copies:
            cp.wait()

    return pl.pallas_call(
        body, name="pair_exchange", in_specs=[ANY] * n, out_specs=[ANY] * n,
        out_shape=[jax.ShapeDtypeStruct(g.shape[1:], g.dtype) for g in gs],
        scratch_shapes=[pltpu.SemaphoreType.DMA((n,)), pltpu.SemaphoreType.DMA((n,))],
    )(*gs)


def _pair_sum(g, sib, c_idx):
    _, _, R, C = g.shape
    rb = _pick(R, 512)

    def body(c_ref, g_ref, s_ref, o_ref):
        o_ref[...] = (g_ref[...].astype(F32) + s_ref[...].astype(F32)).astype(o_ref.dtype)

    return pl.pallas_call(
        body, name="pair_sum",
        grid_spec=pltpu.PrefetchScalarGridSpec(
            num_scalar_prefetch=1, grid=(N_CHIPS, R // rb),
            in_specs=[pl.BlockSpec((None, None, rb, C), lambda j, i, c_ref: (c_ref[0], j, i, 0)),
                      pl.BlockSpec((None, rb, C), lambda j, i, c_ref: (j, i, 0))],
            out_specs=pl.BlockSpec((None, rb, C), lambda j, i, c_ref: (j, i, 0))),
        out_shape=jax.ShapeDtypeStruct((N_CHIPS, R, C), g.dtype), compiler_params=_params(),
    )(c_idx, g, sib)


def _chip_exchange(ps):
    n = len(ps)

    def body(*refs):
        p_refs, out_refs, (send_sems, recv_sems) = refs[:n], refs[n:2 * n], refs[2 * n:]
        x, y, c, others = _place()
        copies = []
        for j, (ox, oy) in enumerate(others):
            for a in range(n):
                cp = pltpu.make_async_remote_copy(src_ref=p_refs[a].at[2 * ox + oy], dst_ref=out_refs[a].at[j],
                                                  send_sem=send_sems.at[a, j], recv_sem=recv_sems.at[a, j],
                                                  device_id=(ox, oy, c), device_id_type=MESH)
                cp.start()
                copies.append(cp)
        for cp in copies:
            cp.wait()

    return pl.pallas_call(
        body, name="chip_exchange", in_specs=[ANY] * n, out_specs=[ANY] * n,
        out_shape=[jax.ShapeDtypeStruct((N_CHIPS - 1,) + p.shape[1:], p.dtype) for p in ps],
        scratch_shapes=[pltpu.SemaphoreType.DMA((n, 3)), pltpu.SemaphoreType.DMA((n, 3))],
    )(*ps)


def _chip_sum(p, r, chip_idx):
    _, R, C = r.shape
    rb = _pick(R, 512)

    def body(chip_ref, p_ref, r_ref, o_ref):
        acc = p_ref[...].astype(F32)
        for j in range(N_CHIPS - 1):
            acc = acc + r_ref[j].astype(F32)
        o_ref[...] = acc

    return pl.pallas_call(
        body, name="chip_sum",
        grid_spec=pltpu.PrefetchScalarGridSpec(
            num_scalar_prefetch=1, grid=(R // rb,),
            in_specs=[pl.BlockSpec((None, rb, C), lambda i, chip_ref: (chip_ref[0], i, 0)),
                      pl.BlockSpec((N_CHIPS - 1, rb, C), lambda i, chip_ref: (0, i, 0))],
            out_specs=pl.BlockSpec((rb, C), lambda i, chip_ref: (i, 0))),
        out_shape=jax.ShapeDtypeStruct((R, C), F32), compiler_params=_params(),
    )(chip_idx, p, r)


def _pair_swap(rhs):
    n = len(rhs)

    def body(*refs):
        rh_refs, out_refs, (send_sems, recv_sems) = refs[:n], refs[n:2 * n], refs[2 * n:]
        x, y, c, _ = _place()
        copies = [pltpu.make_async_remote_copy(src_ref=rh_refs[a], dst_ref=out_refs[a], send_sem=send_sems.at[a],
                                               recv_sem=recv_sems.at[a], device_id=(x, y, 1 - c),
                                               device_id_type=MESH) for a in range(n)]
        for cp in copies:
            cp.start()
        for cp in copies:
            cp.wait()

    return pl.pallas_call(
        body, name="pair_swap", in_specs=[ANY] * n, out_specs=[ANY] * n,
        out_shape=[jax.ShapeDtypeStruct(rh.shape, rh.dtype) for rh in rhs],
        scratch_shapes=[pltpu.SemaphoreType.DMA((n,)), pltpu.SemaphoreType.DMA((n,))],
    )(*rhs)


def _all_reduce_small(s):
    R, C = s.shape

    def body(s_ref, o_ref, buf, send_sems, recv_sems):
        x, y, c, _ = _place()
        me = 4 * x + 2 * y + c
        sends = []
        for k in range(1, N_DEV):
            fx, fy, fc = (k >> 2) & 1, (k >> 1) & 1, k & 1
            to = (x ^ fx, y ^ fy, c ^ fc)
            cp = pltpu.make_async_remote_copy(src_ref=s_ref, dst_ref=buf.at[me], send_sem=send_sems.at[k - 1],
                                              recv_sem=recv_sems.at[k - 1], device_id=to, device_id_type=MESH)
            cp.start()
            sends.append(cp)
        buf[me] = s_ref[...]
        for k in range(1, N_DEV):
            fx, fy, fc = (k >> 2) & 1, (k >> 1) & 1, k & 1
            frm = 4 * (x ^ fx) + 2 * (y ^ fy) + (c ^ fc)
            pltpu.make_async_remote_copy(src_ref=s_ref, dst_ref=buf.at[frm], send_sem=send_sems.at[k - 1],
                                         recv_sem=recv_sems.at[k - 1], device_id=(x, y, c),
                                         device_id_type=MESH).wait_recv()
        acc = buf[0]
        for d in range(1, N_DEV):
            acc = acc + buf[d]
        o_ref[...] = acc
        for cp in sends:
            cp.wait_send()

    vm = pl.BlockSpec(memory_space=pltpu.VMEM)
    return pl.pallas_call(
        body, name="all_reduce_small", in_specs=[vm], out_specs=vm,
        out_shape=jax.ShapeDtypeStruct((R, C), F32),
        scratch_shapes=[pltpu.VMEM((N_DEV, R, C), F32), pltpu.SemaphoreType.DMA((N_DEV - 1,)),
                        pltpu.SemaphoreType.DMA((N_DEV - 1,))],
    )(s)


def _padded(n):
    return -(-n // FLAT_UNIT) * FLAT_UNIT


def _pack_flat(pieces, dtype, row_block=FLAT_ROW_BLOCK):
    flat = []
    for p in pieces:
        p = p.reshape(-1).astype(dtype)
        flat.append(jnp.pad(p, (0, _padded(p.size) - p.size)))
    total = sum(p.size for p in flat)
    flat.append(jnp.zeros((-total) % (row_block * FLAT_COLS), dtype))
    return jnp.concatenate(flat).reshape(-1, FLAT_COLS)


def _unpack_flat(flat, shapes):
    lead = flat.shape[:-2]
    flat = flat.reshape(lead + (-1,))
    out, off = [], 0
    for shp in shapes:
        n = math.prod(shp)
        out.append(flat[..., off:off + n].reshape(lead + tuple(shp)))
        off += _padded(n)
    return out


def _shard_row_block(a):
    for rb in range(min(a, 512) // 16 * 16, 0, -16):
        if a % rb == 0:
            return rb
    return a


def _row_layout(shapes, n_layers):
    groups = {}
    for name, (a, b) in shapes.items():
        names, first, rows = groups.get(b, ((), {}, 0))
        rb = _shard_row_block(a)
        start = -(-rows // rb) * rb
        groups[b] = (names + (name,), {**first, name: start}, start + n_layers * a)
    return {b: (names, first, -(-rows // FLAT_ROW_BLOCK) * FLAT_ROW_BLOCK) for b, (names, first, rows) in groups.items()}


def _pack_rows(group, width, pieces, dtype):
    names, first, rows = group
    parts, at = [], 0
    for name in names:
        if first[name] > at:
            parts.append(jnp.zeros((first[name] - at, width), dtype))
        parts.append(pieces[name].astype(dtype))
        at = first[name] + pieces[name].shape[0]
    if rows > at:
        parts.append(jnp.zeros((rows - at, width), dtype))
    return jnp.concatenate(parts, axis=0)


def _slab(t, axis, j):
    if t.ndim == 3:
        return t[j]
    n = t.shape[axis - 1] // N_CHIPS
    return lax.slice_in_dim(t, j * n, (j + 1) * n, axis=axis - 1)


def _layer_fwd(h0, mem_n, wl, dims):
    n_sb, n_fx, n_mem, sbw, fxw, memw = dims
    n1, gate1, up1, a1 = _ffn_fwd_up(h0, wl["ffn1_pre_g"], wl["ffn1_w_gate"], wl["ffn1_w_up"])
    h1, f1 = _ffn_fwd_down(a1, wl["ffn1_w_down"], h0, wl["ffn1_post_g"])

    u, proj, fl, sg = _mix_fwd_in(h1, wl["mix_pre_g"], wl["w_in"], wl["w_gate"], wl["b_gate"], wl["b_forget"])
    c = _fox_cumsum(fl)
    S = h0.shape[0]
    tc = _attn_blocks("fox", S, S)[1]
    ct = c[:, :n_fx].T
    ccol, crow = ct.reshape(n_fx, S, 1), ct.reshape(n_fx, S // tc, tc)
    qkv_sb = [(proj, k * sbw) for k in range(3)]
    qkv_fx = [(proj, 3 * sbw + k * fxw) for k in range(3)]
    kv = _matmul(mem_n, wl["w_mem_kv"], out_dtype=BF16, name="mem_kv")
    qkv_mem = [(proj, 3 * sbw + 3 * fxw), (kv, 0), (kv, memw)]
    o_sb, tot_sb = _attn_fwd("sb", *qkv_sb, n_sb, HEAD_DIM)
    o_fx, lse_fx = _attn_fwd("fox", *qkv_fx, n_fx, HEAD_DIM, ccol, crow)
    o_mem, lse_mem = _attn_fwd("mem", *qkv_mem, n_mem, MEM_HEAD_DIM)
    h2, zmix, merged = _mix_fwd_out(o_sb, o_fx, o_mem, sg, wl["w_br_sb"], wl["w_br_fox"], wl["w_br_mem"],
                                    wl["w_out"], h1, wl["mix_post_g"])

    n2, gate2, up2, a2 = _ffn_fwd_up(h2, wl["ffn2_pre_g"], wl["ffn2_w_gate"], wl["ffn2_w_up"])
    h3, f2 = _ffn_fwd_down(a2, wl["ffn2_w_down"], h2, wl["ffn2_post_g"])
    saved = dict(h0=h0, n1=n1, gate1=gate1, up1=up1, a1=a1, f1=f1, h1=h1, u=u, fl=fl, sg=sg,
                 qkv_sb=qkv_sb, qkv_fx=qkv_fx, qkv_mem=qkv_mem, ccol=ccol, crow=crow, o_sb=o_sb, o_fx=o_fx, o_mem=o_mem,
                 tot_sb=tot_sb, lse_fx=lse_fx, lse_mem=lse_mem,
                 zmix=zmix, merged=merged, h2=h2, n2=n2, gate2=gate2, up2=up2, a2=a2, f2=f2)
    return h3, saved


def _ffn_bwd(dh, sv, wl, tag, h_in):
    n, gate, up, a, f = (sv[k + tag] for k in ("n", "gate", "up", "a", "f"))
    pre = "ffn" + tag
    df, dgate, dup, dg_post = _ffn_bwd_down(dh, f, wl[pre + "_post_g"], wl[pre + "_w_down"], gate, up)
    dh_in, dg_pre = _ffn_bwd_up(dgate, dup, wl[pre + "_w_gate"], wl[pre + "_w_up"], h_in, wl[pre + "_pre_g"], dh)
    grads = {pre + "_post_g": dg_post, pre + "_pre_g": dg_pre,
             pre + "_w_down": _matmul(a, df, ta=True, batch="a", name="dw_down"),
             pre + "_w_gate": _matmul(n, dgate, ta=True, batch="b", name="dw_gate"),
             pre + "_w_up": _matmul(n, dup, ta=True, batch="b", name="dw_up")}
    return dh_in, grads


def _layer_bwd(dh3, mem_n, wl, sv, dims):
    n_sb, n_fx, n_mem, sbw, fxw, memw = dims
    S = dh3.shape[0]
    dh2, grads = _ffn_bwd(dh3, sv, wl, "2", sv["h2"])

    (dz, db_sb, db_fx, db_mem, do_sb, do_fx, do_mem, dgp, db_gate, dg_post) = _mix_bwd_out(
        dh2, sv["zmix"], wl["mix_post_g"], wl["w_out"], sv["o_sb"], sv["o_fx"], sv["o_mem"],
        wl["w_br_sb"], wl["w_br_fox"], wl["w_br_mem"], sv["sg"])
    grads["mix_post_g"] = dg_post
    grads["b_gate"] = db_gate
    grads["w_out"] = _matmul(sv["merged"], dz, ta=True, name="dw_out")
    grads["w_br_sb"] = _matmul(sv["o_sb"], db_sb, ta=True, name="dw_br_sb")
    grads["w_br_fox"] = _matmul(sv["o_fx"], db_fx, ta=True, name="dw_br_fox")
    grads["w_br_mem"] = _matmul(sv["o_mem"], db_mem, ta=True, name="dw_br_mem")

    dq_sb, dk_sb, dv_sb = _attn_bwd("sb", *sv["qkv_sb"], sv["o_sb"], do_sb, n_sb, HEAD_DIM, lse=sv["tot_sb"])
    dq_fx, dk_fx, dv_fx, dcrow, dccol = _attn_bwd("fox", *sv["qkv_fx"], sv["o_fx"], do_fx, n_fx, HEAD_DIM,
                                                  sv["ccol"], sv["crow"], sv["lse_fx"])
    dq_mem, dk_mem, dv_mem = _attn_bwd("mem", *sv["qkv_mem"], sv["o_mem"], do_mem, n_mem, MEM_HEAD_DIM,
                                       lse=sv["lse_mem"])
    dkv = jnp.concatenate([dk_mem, dv_mem], axis=1)
    grads["w_mem_kv"] = _matmul(mem_n, dkv, ta=True, name="dw_mem_kv")
    dmem_n = _matmul(dkv, wl["w_mem_kv"], tb=True, out_dtype=F32, name="dmem_n")

    dc = jnp.pad((dcrow.reshape(n_fx, S) + dccol.reshape(n_fx, S)).T, ((0, 0), (0, LANE - n_fx)))
    dfl, db_forget = _fox_dlogit(dc, sv["fl"])
    grads["b_forget"] = db_forget
    dproj = jnp.concatenate([dq_sb, dk_sb, dv_sb, dq_fx, dk_fx, dv_fx, dq_mem, dfl], axis=1)
    dh1, dg_pre = _mix_bwd_in(dproj, dgp, wl["w_in"], wl["w_gate"], sv["h1"], wl["mix_pre_g"], dh2)
    grads["mix_pre_g"] = dg_pre
    grads["w_in"] = _matmul(sv["u"], dproj, ta=True, name="dw_in")
    grads["w_gate"] = _matmul(sv["u"], dgp, ta=True, name="dw_gate_mix")

    dh0, g1 = _ffn_bwd(dh1, sv, wl, "1", sv["h0"])
    grads.update(g1)
    return dh0, grads, dmem_n


def kernel(x, mem, ffn1_pre_g, ffn1_post_g, ffn1_w_gate, ffn1_w_up, ffn1_w_down, mix_pre_g, mix_post_g, w_in, b_forget, mem_norm_g, w_mem_kv, w_gate, b_gate, w_br_sb, w_br_fox, w_br_mem, w_out, ffn2_pre_g, ffn2_post_g, ffn2_w_gate, ffn2_w_up, ffn2_w_down, loss_target, m_ffn1_pre_g, m_ffn1_post_g, m_ffn1_w_gate, m_ffn1_w_up, m_ffn1_w_down, m_mix_pre_g, m_mix_post_g, m_w_in, m_b_forget, m_mem_norm_g, m_w_mem_kv, m_w_gate, m_b_gate, m_w_br_sb, m_w_br_fox, m_w_br_mem, m_w_out, m_ffn2_pre_g, m_ffn2_post_g, m_ffn2_w_gate, m_ffn2_w_up, m_ffn2_w_down, v_ffn1_pre_g, v_ffn1_post_g, v_ffn1_w_gate, v_ffn1_w_up, v_ffn1_w_down, v_mix_pre_g, v_mix_post_g, v_w_in, v_b_forget, v_mem_norm_g, v_w_mem_kv, v_w_gate, v_b_gate, v_w_br_sb, v_w_br_fox, v_w_br_mem, v_w_out, v_ffn2_pre_g, v_ffn2_post_g, v_ffn2_w_gate, v_ffn2_w_up, v_ffn2_w_down):
    args = dict(locals())
    w = {n: args[n] for n in WEIGHTS}
    m = {n: args["m_" + n] for n in WEIGHTS}
    v = {n: args["v_" + n] for n in WEIGHTS}
    L = w["ffn1_pre_g"].shape[0]
    Lh = L // 2
    D = x.shape[2]
    sbw, fxw, memw = w["w_br_sb"].shape[1], w["w_br_fox"].shape[1], w["w_br_mem"].shape[1]
    n_sb, n_fx, n_mem = sbw // HEAD_DIM, fxw // HEAD_DIM, memw // MEM_HEAD_DIM
    dims = (n_sb, n_fx, n_mem, sbw, fxw, memw)
    qkv_w = 3 * sbw + 3 * fxw
    c_idx = lax.axis_index("c")
    c_arr = c_idx.reshape(1).astype(jnp.int32)
    chip_arr = (2 * lax.axis_index("x") + lax.axis_index("y")).reshape(1).astype(jnp.int32)

    shard_shapes = {n: w[n].shape[1:] for n, _ in BIG}
    layout = _row_layout(shard_shapes, Lh)
    widths = list(layout)
    locs = [jnp.stack([_pack_rows(layout[b], b, {n: w[n][hf * Lh:(hf + 1) * Lh].reshape(-1, b) for n in layout[b][0]},
                                  BF16) for hf in range(2)]) for b in widths]
    gathered = dict(zip(widths, _gather_weights(locs, [_place_own(loc, chip_arr) for loc in locs])))

    def layer_weights(l):
        hf, li = divmod(l, Lh)
        wl = {}
        for n, axis in BIG:
            a, b = shard_shapes[n]
            r0 = layout[b][1][n] + li * a
            shards = gathered[b][:, hf, r0:r0 + a]
            if n.startswith("ffn"):
                wl[n] = shards
            else:
                wl[n] = (shards.transpose(1, 0, 2).reshape(a, N_CHIPS * b) if axis == 2 else
                         shards.reshape(N_CHIPS * a, b))
        wi = wl["w_in"]
        wl["w_in"] = jnp.concatenate([wi[:, :qkv_w], wi[:, qkv_w + n_fx:], wi[:, qkv_w:qkv_w + n_fx],
                                      jnp.zeros((D, LANE - n_fx), BF16)], axis=1)
        for n in SMALL:
            if n != "mem_norm_g":
                wl[n] = w[n][l][None, :]
        wl["b_forget"] = jnp.pad(wl["b_forget"], ((0, 0), (0, LANE - n_fx)))
        return wl

    g_mem = w["mem_norm_g"][None, :]

    mem_n = _mem_norm(mem[0], g_mem)
    h, wls, saved = x[0], [], []
    for l in range(L):
        wls.append(layer_weights(l))
        h, sv = _layer_fwd(h, mem_n, wls[l], dims)
        saved.append(sv)
    dh, loss_tile = _loss_head(h, loss_target[0])
    loss = lax.psum(loss_tile[0, 0], ("x", "y", "c"))
    gl, dmem_n = [None] * L, [None] * L
    for l in reversed(range(L)):
        dh, gl[l], dmem_n[l] = _layer_bwd(dh, mem_n, wls[l], saved[l], dims)
        gi = gl[l]["w_in"]
        gl[l]["w_in"] = jnp.concatenate([gi[:, :qkv_w], gi[:, qkv_w + memw:qkv_w + memw + n_fx],
                                         gi[:, qkv_w:qkv_w + memw]], axis=1)
    grad_x = dh
    g_mem_norm = _mem_norm_bwd(mem[0], g_mem, jnp.stack(dmem_n))

    axis_of = dict(BIG)
    partials = [jnp.stack([jnp.stack([
        _pack_rows(layout[b], b, {n: jnp.concatenate([_slab(gl[hf * Lh + li][n], axis_of[n], j) for li in range(Lh)])
                                  for n in layout[b][0]}, BF16)
        for j in range(N_CHIPS)]) for hf in range(2)]) for b in widths]
    pairs = [_pair_sum(g, sib, c_arr) for g, sib in zip(partials, _pair_exchange(partials))]
    mines = [_chip_sum(p, r, chip_arr) for p, r in zip(pairs, _chip_exchange(pairs))]
    theirs = _pair_swap(mines)

    grad, delta, new_m, new_v = {}, {}, {}, {}
    for n, _ in BIG:
        k = widths.index(shard_shapes[n][1])
        grad[n], delta[n], new_m[n], new_v[n] = _adamw_reduced(
            w[n], m[n], v[n], mines[k], theirs[k], c_arr, layout[widths[k]][1][n], name="adamw_" + n)

    small_local = {n: (g_mem_norm if n == "mem_norm_g" else
                       jnp.concatenate([gl[l][n][:, :n_fx] if n == "b_forget" else gl[l][n] for l in range(L)]))
                   for n in SMALL}
    small_shapes = [small_local[n].shape for n in SMALL]
    small_sum = _unpack_flat(_all_reduce_small(_pack_flat([small_local[n] for n in SMALL], F32, row_block=16)),
                             small_shapes)
    for n, t in zip(SMALL, small_sum):
        shp = w[n].shape
        two_d = (1, shp[0]) if len(shp) == 1 else shp
        grad[n] = t.reshape(shp)
        d_, m_, v_ = _adamw(w[n].reshape(two_d), t.reshape(two_d), m[n].reshape(two_d), v[n].reshape(two_d),
                            name="adamw_" + n)
        delta[n], new_m[n], new_v[n] = d_.reshape(shp), m_.reshape(shp), v_.reshape(shp)

    return (loss, grad_x[None], *[grad[n] for n in WEIGHTS], *[delta[n] for n in WEIGHTS],
            *[new_m[n] for n in WEIGHTS], *[new_v[n] for n in WEIGHTS])
```

```python
import math

import jax
import jax.numpy as jnp
from jax import lax
from jax.experimental import pallas as pl
from jax.experimental.pallas import tpu as pltpu

F32 = jnp.float32
BF16 = jnp.bfloat16
RMS_EPS = 1e-6
HEAD_DIM = 64
MEM_HEAD_DIM = 128
LANE = 128
V7X_VMEM_LIMIT_BYTES = 56 * 1024 * 1024
FLAT_COLS = 512
FLAT_UNIT = 16 * FLAT_COLS
FLAT_ROW_BLOCK = 512
N_CHIPS = 4
N_DEV = 8
NEG = float(jnp.finfo(jnp.float32).min)

ADAM_LR = 0.001
ADAM_B1 = 0.9
ADAM_B2 = 0.999
ADAM_EPS = 1e-08
ADAM_WD = 0.01
ADAM_STEP = 10

BIG = (("ffn1_w_gate", 2), ("ffn1_w_up", 2), ("ffn1_w_down", 1), ("w_in", 2), ("w_mem_kv", 1), ("w_gate", 2),
       ("w_br_sb", 2), ("w_br_fox", 2), ("w_br_mem", 2), ("w_out", 1),
       ("ffn2_w_gate", 2), ("ffn2_w_up", 2), ("ffn2_w_down", 1))
SMALL = ("ffn1_pre_g", "ffn1_post_g", "mix_pre_g", "mix_post_g", "b_forget", "mem_norm_g", "b_gate",
         "ffn2_pre_g", "ffn2_post_g")
WEIGHTS = ("ffn1_pre_g", "ffn1_post_g", "ffn1_w_gate", "ffn1_w_up", "ffn1_w_down", "mix_pre_g", "mix_post_g", "w_in",
           "b_forget", "mem_norm_g", "w_mem_kv", "w_gate", "b_gate", "w_br_sb", "w_br_fox", "w_br_mem", "w_out",
           "ffn2_pre_g", "ffn2_post_g", "ffn2_w_gate", "ffn2_w_up", "ffn2_w_down")


def _params(**kw):
    return pltpu.CompilerParams(vmem_limit_bytes=V7X_VMEM_LIMIT_BYTES, **kw)


def _dot(a, b):
    return jnp.dot(a, b, preferred_element_type=F32)


def _dot_nt(a, b):
    return lax.dot_general(a, b, (((1,), (1,)), ((), ())), preferred_element_type=F32)


def _dot_tn(a, b):
    return lax.dot_general(a, b, (((0,), (0,)), ((), ())), preferred_element_type=F32)


def _rms(t, g):
    return t * lax.rsqrt(jnp.mean(t * t, axis=-1, keepdims=True) + RMS_EPS) * g


def _pick(dim, pref):
    if dim <= pref:
        return dim
    for cand in range(pref - pref % LANE, 0, -LANE):
        if dim % cand == 0:
            return cand
    return dim


def _rows(bm, cols):
    return pl.BlockSpec((bm, cols), lambda i: (i, 0))


def _whole(shape):
    nd = len(shape)
    return pl.BlockSpec(shape, lambda i: (0,) * nd)


def _split3(x):
    hi = x.astype(BF16)
    r1 = x - hi.astype(F32)
    mid = r1.astype(BF16)
    lo = (r1 - mid.astype(F32)).astype(BF16)
    return hi, mid, lo


def _cumdot(x, tri):
    hi = x.astype(BF16)
    lo = (x - hi.astype(F32)).astype(BF16)
    return _dot(hi, tri) + _dot(lo, tri)


def _slabs(bm, cols):
    return pl.BlockSpec((N_CHIPS, bm, cols), lambda i: (0, i, 0))


def _ffn_fwd_up(h, g_pre, wg, wu):
    S, D = h.shape
    Fs = wg.shape[2]
    bm = _pick(S, 256)

    def body(h_ref, g_ref, wg_ref, wu_ref, n_ref, gate_ref, up_ref, a_ref):
        n = _rms(h_ref[...], g_ref[...]).astype(BF16)
        n_ref[...] = n
        for j in range(N_CHIPS):
            gate = _dot(n, wg_ref[j])
            up = _dot(n, wu_ref[j])
            gate_ref[j] = gate.astype(BF16)
            up_ref[j] = up.astype(BF16)
            a_ref[j] = (gate * jax.nn.sigmoid(gate) * up).astype(BF16)

    return pl.pallas_call(
        body, name="ffn_fwd_up", grid=(S // bm,),
        in_specs=[_rows(bm, D), _whole((1, D)), _whole((N_CHIPS, D, Fs)), _whole((N_CHIPS, D, Fs))],
        out_specs=[_rows(bm, D), _slabs(bm, Fs), _slabs(bm, Fs), _slabs(bm, Fs)],
        out_shape=[jax.ShapeDtypeStruct((S, D), BF16)] + [jax.ShapeDtypeStruct((N_CHIPS, S, Fs), BF16)] * 3,
        compiler_params=_params(),
    )(h, g_pre, wg, wu)


def _ffn_fwd_down(a, wd, h, g_post):
    _, S, Fs = a.shape
    D = wd.shape[2]
    bm = _pick(S, 256)

    def body(a_ref, wd_ref, h_ref, g_ref, hout_ref, f_ref):
        f = _dot(a_ref[0], wd_ref[0])
        for j in range(1, N_CHIPS):
            f = f + _dot(a_ref[j], wd_ref[j])
        f_ref[...] = f
        hout_ref[...] = h_ref[...] + 0.5 * _rms(f, g_ref[...])

    return pl.pallas_call(
        body, name="ffn_fwd_down", grid=(S // bm,),
        in_specs=[_slabs(bm, Fs), _whole((N_CHIPS, Fs, D)), _rows(bm, D), _whole((1, D))],
        out_specs=[_rows(bm, D), _rows(bm, D)],
        out_shape=[jax.ShapeDtypeStruct((S, D), F32)] * 2,
        compiler_params=_params(),
    )(a, wd, h, g_post)


def _ffn_bwd_down(dh, f, g_post, wd, gate, up):
    S, D = dh.shape
    Fs = wd.shape[1]
    bm = _pick(S, 256)

    def body(dh_ref, f_ref, g_ref, wd_ref, gate_ref, up_ref, df_ref, dgate_ref, dup_ref, dg_ref):
        _, vjp = jax.vjp(lambda t, g: 0.5 * _rms(t, g), f_ref[...], g_ref[...])
        df, dg = vjp(dh_ref[...])

        @pl.when(pl.program_id(0) == 0)
        def _():
            dg_ref[...] = jnp.zeros_like(dg_ref)

        dg_ref[...] += dg
        dfb = df.astype(BF16)
        df_ref[...] = dfb
        for j in range(N_CHIPS):
            da = _dot_nt(dfb, wd_ref[j])
            gt = gate_ref[j].astype(F32)
            sig = jax.nn.sigmoid(gt)
            silu = gt * sig
            dup_ref[j] = (da * silu).astype(BF16)
            dgate_ref[j] = (da * up_ref[j].astype(F32) * (sig + silu * (1.0 - sig))).astype(BF16)

    return pl.pallas_call(
        body, name="ffn_bwd_down", grid=(S // bm,),
        in_specs=[_rows(bm, D), _rows(bm, D), _whole((1, D)), _whole((N_CHIPS, Fs, D)), _slabs(bm, Fs),
                  _slabs(bm, Fs)],
        out_specs=[_rows(bm, D), _slabs(bm, Fs), _slabs(bm, Fs), _whole((1, D))],
        out_shape=[jax.ShapeDtypeStruct((S, D), BF16), jax.ShapeDtypeStruct((N_CHIPS, S, Fs), BF16),
                   jax.ShapeDtypeStruct((N_CHIPS, S, Fs), BF16), jax.ShapeDtypeStruct((1, D), F32)],
        compiler_params=_params(),
    )(dh, f, g_post, wd, gate, up)


def _ffn_bwd_up(dgate, dup, wg, wu, h_in, g_pre, dh):
    _, S, Fs = dgate.shape
    D = wg.shape[1]
    bm = _pick(S, 256)

    def body(dgate_ref, dup_ref, wg_ref, wu_ref, h_ref, g_ref, dh_ref, dhin_ref, dg_ref):
        dn = _dot_nt(dgate_ref[0], wg_ref[0]) + _dot_nt(dup_ref[0], wu_ref[0])
        for j in range(1, N_CHIPS):
            dn = dn + _dot_nt(dgate_ref[j], wg_ref[j]) + _dot_nt(dup_ref[j], wu_ref[j])
        _, vjp = jax.vjp(_rms, h_ref[...], g_ref[...])
        dhx, dg = vjp(dn)

        @pl.when(pl.program_id(0) == 0)
        def _():
            dg_ref[...] = jnp.zeros_like(dg_ref)

        dg_ref[...] += dg
        dhin_ref[...] = dh_ref[...] + dhx

    return pl.pallas_call(
        body, name="ffn_bwd_up", grid=(S // bm,),
        in_specs=[_slabs(bm, Fs), _slabs(bm, Fs), _whole((N_CHIPS, D, Fs)), _whole((N_CHIPS, D, Fs)), _rows(bm, D),
                  _whole((1, D)), _rows(bm, D)],
        out_specs=[_rows(bm, D), _whole((1, D))],
        out_shape=[jax.ShapeDtypeStruct((S, D), F32), jax.ShapeDtypeStruct((1, D), F32)],
        compiler_params=_params(),
    )(dgate, dup, wg, wu, h_in, g_pre, dh)


def _matmul(a, b, *, ta=False, tb=False, out_dtype=BF16, name, batch=None):
    n_batch = a.shape[0] if batch == "a" else b.shape[0] if batch == "b" else 1
    a_shape = a.shape[1:] if batch == "a" else a.shape
    b_shape = b.shape[1:] if batch == "b" else b.shape
    M, K = (a_shape[1], a_shape[0]) if ta else a_shape
    N = b_shape[0] if tb else b_shape[1]
    acc_budget = 12 * 1024 * 1024
    bm, bk = _pick(M, 1536), _pick(K, 512)
    while n_batch * N * bm * 4 > acc_budget and bm % (2 * LANE) == 0:
        bm //= 2
    bn = N if n_batch * N * bm * 4 <= acc_budget else _pick(N, 1536)
    nk = K // bk

    def body(a_ref, b_ref, o_ref, acc_ref):
        kk = pl.program_id(2)

        @pl.when(kk == 0)
        def _():
            acc_ref[...] = jnp.zeros_like(acc_ref)

        dims = (((0 if ta else 1,), (1 if tb else 0,)), ((), ()))
        if batch is None:
            acc_ref[...] += lax.dot_general(a_ref[...], b_ref[...], dims, preferred_element_type=F32)
        else:
            for g in range(n_batch):
                av = a_ref[g] if batch == "a" else a_ref[...]
                bv = b_ref[g] if batch == "b" else b_ref[...]
                acc_ref[g] += lax.dot_general(av, bv, dims, preferred_element_type=F32)

        @pl.when(kk == nk - 1)
        def _():
            o_ref[...] = acc_ref[...].astype(o_ref.dtype)

    def spec(block, index, batched):
        if batched:
            return pl.BlockSpec((n_batch,) + block, lambda i, j, k: (0,) + index(i, j, k))
        return pl.BlockSpec(block, index)

    a_spec = spec((bk, bm), lambda i, j, k: (k, i), batch == "a") if ta else \
        spec((bm, bk), lambda i, j, k: (i, k), batch == "a")
    b_spec = spec((bn, bk), lambda i, j, k: (j, k), batch == "b") if tb else \
        spec((bk, bn), lambda i, j, k: (k, j), batch == "b")
    lead = (n_batch,) if batch else ()
    return pl.pallas_call(
        body, name=name, grid=(M // bm, N // bn, nk),
        in_specs=[a_spec, b_spec],
        out_specs=spec((bm, bn), lambda i, j, k: (i, j), batch is not None),
        out_shape=jax.ShapeDtypeStruct(lead + (M, N), out_dtype),
        scratch_shapes=[pltpu.VMEM(lead + (bm, bn), F32)],
        compiler_params=_params(),
    )(a, b)


def _mix_fwd_in(h, g_pre, win, wgate, b_gate, b_forget):
    S, D = h.shape
    PW = win.shape[1] - LANE
    G = wgate.shape[1]
    bm = _pick(S, 256)

    def body(h_ref, g_ref, win_ref, wgate_ref, bg_ref, bf_ref, u_ref, proj_ref, fl_ref, sg_ref):
        u = _rms(h_ref[...], g_ref[...]).astype(BF16)
        u_ref[...] = u
        proj = _dot(u, win_ref[...])
        proj_ref[...] = proj[:, :PW].astype(BF16)
        fl_ref[...] = proj[:, PW:] + bf_ref[...]
        sg_ref[...] = jax.nn.sigmoid(_dot(u, wgate_ref[...]) + bg_ref[...]).astype(BF16)

    return pl.pallas_call(
        body, name="mix_fwd_in", grid=(S // bm,),
        in_specs=[_rows(bm, D), _whole((1, D)), _whole((D, PW + LANE)), _whole((D, G)), _whole((1, G)),
                  _whole((1, LANE))],
        out_specs=[_rows(bm, D), _rows(bm, PW), _rows(bm, LANE), _rows(bm, G)],
        out_shape=[jax.ShapeDtypeStruct((S, D), BF16), jax.ShapeDtypeStruct((S, PW), BF16),
                   jax.ShapeDtypeStruct((S, LANE), F32), jax.ShapeDtypeStruct((S, G), BF16)],
        compiler_params=_params(),
    )(h, g_pre, win, wgate, b_gate, b_forget)


def _mix_fwd_out(o_sb, o_fx, o_mem, sg, w_sb, w_fx, w_mem, w_out, h, g_post):
    S, D = h.shape
    bm = _pick(S, 256)
    widths = (o_sb.shape[1], o_fx.shape[1], o_mem.shape[1])

    def body(osb_ref, ofx_ref, omem_ref, sg_ref, wsb_ref, wfx_ref, wmem_ref, wout_ref, h_ref, g_ref,
             hout_ref, z_ref, merged_ref):
        s = sg_ref[...].astype(F32)
        merged = (s[:, :D] * _dot(osb_ref[...], wsb_ref[...]) + s[:, D:2 * D] * _dot(ofx_ref[...], wfx_ref[...])
                  + s[:, 2 * D:] * _dot(omem_ref[...], wmem_ref[...]))
        mb = merged.astype(BF16)
        merged_ref[...] = mb
        z = _dot(mb, wout_ref[...])
        z_ref[...] = z
        hout_ref[...] = h_ref[...] + _rms(z, g_ref[...])

    return pl.pallas_call(
        body, name="mix_fwd_out", grid=(S // bm,),
        in_specs=[_rows(bm, widths[0]), _rows(bm, widths[1]), _rows(bm, widths[2]), _rows(bm, 3 * D),
                  _whole((widths[0], D)), _whole((widths[1], D)), _whole((widths[2], D)), _whole((D, D)),
                  _rows(bm, D), _whole((1, D))],
        out_specs=[_rows(bm, D), _rows(bm, D), _rows(bm, D)],
        out_shape=[jax.ShapeDtypeStruct((S, D), F32), jax.ShapeDtypeStruct((S, D), F32),
                   jax.ShapeDtypeStruct((S, D), BF16)],
        compiler_params=_params(),
    )(o_sb, o_fx, o_mem, sg, w_sb, w_fx, w_mem, w_out, h, g_post)


def _mix_bwd_out(dh, z, g_post, w_out, o_sb, o_fx, o_mem, w_sb, w_fx, w_mem, sg):
    S, D = dh.shape
    bm = _pick(S, 256)
    widths = (o_sb.shape[1], o_fx.shape[1], o_mem.shape[1])

    def body(dh_ref, z_ref, g_ref, wout_ref, osb_ref, ofx_ref, omem_ref, wsb_ref, wfx_ref, wmem_ref, sg_ref,
             dz_ref, dbsb_ref, dbfx_ref, dbmem_ref, dosb_ref, dofx_ref, domem_ref, dgp_ref, dbg_ref, dg_ref):
        _, vjp = jax.vjp(_rms, z_ref[...], g_ref[...])
        dz, dg = vjp(dh_ref[...])

        @pl.when(pl.program_id(0) == 0)
        def _():
            dg_ref[...] = jnp.zeros_like(dg_ref)
            dbg_ref[...] = jnp.zeros_like(dbg_ref)

        dg_ref[...] += dg
        dzb = dz.astype(BF16)
        dz_ref[...] = dzb
        dmerged = _dot_nt(dzb, wout_ref[...])
        s = sg_ref[...].astype(F32)
        branches = ((osb_ref, wsb_ref, dbsb_ref, dosb_ref), (ofx_ref, wfx_ref, dbfx_ref, dofx_ref),
                    (omem_ref, wmem_ref, dbmem_ref, domem_ref))
        for k, (o_ref, w_ref, db_ref, do_ref) in enumerate(branches):
            gs = s[:, k * D:(k + 1) * D]
            dbb = (dmerged * gs).astype(BF16)
            db_ref[...] = dbb
            do_ref[...] = _dot_nt(dbb, w_ref[...]).astype(BF16)
            dgp = dmerged * _dot(o_ref[...], w_ref[...]) * gs * (1.0 - gs)
            dgp_ref[:, k * D:(k + 1) * D] = dgp.astype(BF16)
            dbg_ref[:, k * D:(k + 1) * D] += jnp.sum(dgp, axis=0, keepdims=True)

    return pl.pallas_call(
        body, name="mix_bwd_out", grid=(S // bm,),
        in_specs=[_rows(bm, D), _rows(bm, D), _whole((1, D)), _whole((D, D)),
                  _rows(bm, widths[0]), _rows(bm, widths[1]), _rows(bm, widths[2]),
                  _whole((widths[0], D)), _whole((widths[1], D)), _whole((widths[2], D)), _rows(bm, 3 * D)],
        out_specs=[_rows(bm, D)] * 4 + [_rows(bm, widths[0]), _rows(bm, widths[1]), _rows(bm, widths[2]),
                                        _rows(bm, 3 * D), _whole((1, 3 * D)), _whole((1, D))],
        out_shape=[jax.ShapeDtypeStruct((S, D), BF16)] * 4
        + [jax.ShapeDtypeStruct((S, w), BF16) for w in widths]
        + [jax.ShapeDtypeStruct((S, 3 * D), BF16), jax.ShapeDtypeStruct((1, 3 * D), F32),
           jax.ShapeDtypeStruct((1, D), F32)],
        compiler_params=_params(),
    )(dh, z, g_post, w_out, o_sb, o_fx, o_mem, w_sb, w_fx, w_mem, sg)


def _mix_bwd_in(dproj, dgp, win, wgate, h_in, g_pre, dh):
    S, PWL = dproj.shape
    G = dgp.shape[1]
    D = h_in.shape[1]
    bm = _pick(S, 256)

    def body(dproj_ref, dgp_ref, win_ref, wgate_ref, h_ref, g_ref, dh_ref, dhin_ref, dg_ref):
        du = _dot_nt(dproj_ref[...], win_ref[...]) + _dot_nt(dgp_ref[...], wgate_ref[...])
        _, vjp = jax.vjp(_rms, h_ref[...], g_ref[...])
        dhx, dg = vjp(du)

        @pl.when(pl.program_id(0) == 0)
        def _():
            dg_ref[...] = jnp.zeros_like(dg_ref)

        dg_ref[...] += dg
        dhin_ref[...] = dh_ref[...] + dhx

    return pl.pallas_call(
        body, name="mix_bwd_in", grid=(S // bm,),
        in_specs=[_rows(bm, PWL), _rows(bm, G), _whole((D, PWL)), _whole((D, G)), _rows(bm, D), _whole((1, D)),
                  _rows(bm, D)],
        out_specs=[_rows(bm, D), _whole((1, D))],
        out_shape=[jax.ShapeDtypeStruct((S, D), F32), jax.ShapeDtypeStruct((1, D), F32)],
        compiler_params=_params(),
    )(dproj, dgp, win, wgate, h_in, g_pre, dh)


def _log_sigmoid(x):
    return jnp.minimum(x, 0.0) - jnp.log(1.0 + jnp.exp(-jnp.abs(x)))


def _fox_cumsum(fl):
    S = fl.shape[0]
    rb = _pick(S, LANE)

    def body(fl_ref, c_ref, carry_ref):
        @pl.when(pl.program_id(0) == 0)
        def _():
            carry_ref[...] = jnp.zeros_like(carry_ref)

        r = lax.broadcasted_iota(jnp.int32, (rb, rb), 0)
        cidx = lax.broadcasted_iota(jnp.int32, (rb, rb), 1)
        tri = (cidx <= r).astype(BF16)
        hi, mid, lo = _split3(_log_sigmoid(fl_ref[...]))
        c = _dot(tri, hi) + _dot(tri, mid) + _dot(tri, lo) + carry_ref[...]
        c_ref[...] = c
        carry_ref[...] = c[rb - 1:rb, :]

    return pl.pallas_call(
        body, name="fox_cumsum", grid=(S // rb,),
        in_specs=[_rows(rb, LANE)], out_specs=_rows(rb, LANE),
        out_shape=jax.ShapeDtypeStruct((S, LANE), F32),
        scratch_shapes=[pltpu.VMEM((1, LANE), F32)],
        compiler_params=_params(),
    )(fl)


def _fox_dlogit(dc, fl):
    S = fl.shape[0]
    rb = _pick(S, LANE)
    nb = S // rb

    def body(dc_ref, fl_ref, dfl_ref, dbf_ref, carry_ref):
        @pl.when(pl.program_id(0) == 0)
        def _():
            carry_ref[...] = jnp.zeros_like(carry_ref)
            dbf_ref[...] = jnp.zeros_like(dbf_ref)

        r = lax.broadcasted_iota(jnp.int32, (rb, rb), 0)
        cidx = lax.broadcasted_iota(jnp.int32, (rb, rb), 1)
        tri = (cidx >= r).astype(BF16)
        hi, mid, lo = _split3(dc_ref[...])
        rc = _dot(tri, hi) + _dot(tri, mid) + _dot(tri, lo) + carry_ref[...]
        carry_ref[...] = rc[0:1, :]
        dfl = rc * jax.nn.sigmoid(-fl_ref[...])
        dfl_ref[...] = dfl.astype(BF16)
        dbf_ref[...] += jnp.sum(dfl, axis=0, keepdims=True)

    rev = pl.BlockSpec((rb, LANE), lambda i: (nb - 1 - i, 0))
    return pl.pallas_call(
        body, name="fox_dlogit", grid=(nb,),
        in_specs=[rev, rev], out_specs=[rev, _whole((1, LANE))],
        out_shape=[jax.ShapeDtypeStruct((S, LANE), BF16), jax.ShapeDtypeStruct((1, LANE), F32)],
        scratch_shapes=[pltpu.VMEM((1, LANE), F32)],
        compiler_params=_params(),
    )(dc, fl)


def _attn_blocks(kind, S, Sk, backward=False):
    tq = _pick(S, 1024 if backward else 2048)
    tc = LANE if kind == "sb" else _pick(Sk, 256)
    return tq, tc


def _is_power_of_two(x):
    return math.frexp(x)[0] == 0.5


def _sb_logs(z):
    ln = -jnp.maximum(z, 0.0) - jnp.log(1.0 + jnp.exp(-jnp.abs(z)))
    return ln + z, ln


def _head_lanes(pack, dh):
    lane = lax.broadcasted_iota(jnp.int32, (1, LANE), 1)
    return [(lane >= hh * dh) & (lane < (hh + 1) * dh) for hh in range(pack)]


def _by_head(sel, parts):
    out = parts[0]
    for hh in range(1, len(parts)):
        out = jnp.where(sel[hh], parts[hh], out)
    return out


def _only_head(sel, hh, x):
    return x if len(sel) == 1 else jnp.where(sel[hh], x, jnp.zeros_like(x))


ROW_PIECE = 1024


def _tail(x, r0):
    return x if not r0 else x[r0:]


def _put_tail(x, tail, r0):
    return tail if not r0 else jnp.concatenate([x[:r0], tail], axis=0)


def _add_tail(x, tail, r0):
    return x + tail if not r0 else jnp.concatenate([x[:r0], x[r0:] + tail], axis=0)


def _q_cols(tq, first):
    return pl.BlockSpec((tq, LANE), lambda g, i: (i, first // LANE + g))


def _k_cols(rows, first):
    return pl.BlockSpec((rows, LANE), lambda g, i: (0, first // LANE + g))


def _attn_fwd(kind, q, k, v, n_heads, dh, ccol=None, crow=None):
    (qa, q0), (ka, k0), (va, v0) = q, k, v
    S, Sk = qa.shape[0], ka.shape[0]
    pack = LANE // dh
    tq, tc = _attn_blocks(kind, S, Sk)
    scale = dh ** -0.5
    fold = _is_power_of_two(scale)
    causal = kind != "mem"
    n_diag = tq // tc if causal else 0
    unroll = 2 if causal else 1
    assert n_diag % unroll == 0 and (Sk // tc) % unroll == 0

    def body(*refs):
        if kind == "fox":
            q_ref, k_ref, v_ref, cc_ref, cr_ref, o_ref, lse_ref = refs
        else:
            q_ref, k_ref, v_ref, o_ref, lse_ref = refs
        i = pl.program_id(1)
        n_full = (i * tq) // tc if causal else Sk // tc
        qpos = i * tq + lax.broadcasted_iota(jnp.int32, (tq, tc), 0)
        kio = lax.broadcasted_iota(jnp.int32, (tq, tc), 1)
        heads = range(pack)
        sel = _head_lanes(pack, dh)
        q2 = q_ref[...] * scale if fold else q_ref[...]
        qs = [_only_head(sel, hh, q2) for hh in heads]

        def kv(jc):
            off = pl.multiple_of(jc * tc, tc)
            return off, k_ref[pl.ds(off, tc), :], v_ref[pl.ds(off, tc), :]

        if kind == "sb":
            tri = (lax.broadcasted_iota(jnp.int32, (tc, tc), 0) > lax.broadcasted_iota(jnp.int32, (tc, tc), 1)
                   ).astype(BF16)

            def chunk(jc, r0, runs, acc):
                off, k2, v2 = kv(jc)
                new_runs, pv = [], []
                for hh in heads:
                    lb, ln = _sb_logs(_dot_nt(_tail(qs[hh], r0), k2))
                    if r0 is not None:
                        mask = (off + _tail(kio, r0)) < _tail(qpos, r0)
                        ln = jnp.where(mask, ln, 0.0)
                    w = jnp.exp(lb + _cumdot(ln, tri) + _tail(runs[hh], r0))
                    if r0 is not None:
                        w = jnp.where(mask, w, 0.0)
                    pv.append(_dot(w.astype(BF16), v2))
                    new_runs.append(_add_tail(runs[hh], jnp.sum(ln, axis=1, keepdims=True), r0))
                return tuple(new_runs), _add_tail(acc, _by_head(sel, pv), r0)

            state = (tuple(jnp.zeros((tq, 1), F32) for _ in heads), jnp.zeros((tq, LANE), F32))
            for d in range(n_diag - 1, -1, -1):
                state = chunk(n_full + d, d * tc, *state)

            def trip(t, st):
                for u in range(unroll):
                    st = chunk(n_full - 1 - unroll * t - u, None, *st)
                return st

            runs, acc = lax.fori_loop(0, n_full // unroll, trip, state)
            o_ref[...] = acc.astype(o_ref.dtype)
            for hh in heads:
                lse_ref[hh] = runs[hh]
        else:
            pieces = [(a, a + ROW_PIECE) for a in range(0, tq, ROW_PIECE)] if tq % ROW_PIECE == 0 else [(0, tq)]

            def piece_chunk(a, b, off, jc, k2, v2, r0, ms, ls, acc):
                new_ms, new_ls, alphas, pv = [], [], [], []
                for hh in heads:
                    z = _dot_nt(_tail(qs[hh][a:b], r0), k2)
                    if not fold:
                        z = z * scale
                    if kind == "fox":
                        z = z + _tail(cc_ref[hh][a:b], r0) - cr_ref[hh, pl.ds(jc, 1), :]
                    if r0 is not None:
                        z = jnp.where((off + _tail(kio[a:b], r0)) <= _tail(qpos[a:b], r0), z, NEG)
                    m_old, l_old = _tail(ms[hh], r0), _tail(ls[hh], r0)
                    m_new = jnp.maximum(m_old, jnp.max(z, axis=1, keepdims=True))
                    alpha = jnp.exp(m_old - m_new)
                    p = jnp.exp(z - m_new)
                    new_ms.append(_put_tail(ms[hh], m_new, r0))
                    new_ls.append(_put_tail(ls[hh], alpha * l_old + jnp.sum(p, axis=1, keepdims=True), r0))
                    alphas.append(alpha)
                    pv.append(_dot(p.astype(BF16), v2))
                acc_new = _by_head(sel, alphas) * _tail(acc, r0) + _by_head(sel, pv)
                return tuple(new_ms), tuple(new_ls), _put_tail(acc, acc_new, r0)

            def chunk(jc, r0, state):
                off, k2, v2 = kv(jc)
                out = []
                for (a, b), st in zip(pieces, state):
                    if r0 is not None and r0 >= b:
                        out.append(st)
                    else:
                        out.append(piece_chunk(a, b, off, jc, k2, v2, None if r0 is None else max(r0 - a, 0), *st))
                return tuple(out)

            state = tuple((tuple(jnp.full((b - a, 1), NEG, F32) for _ in heads),
                           tuple(jnp.zeros((b - a, 1), F32) for _ in heads), jnp.zeros((b - a, LANE), F32))
                          for a, b in pieces)

            def trip(t, st):
                for u in range(unroll):
                    st = chunk(unroll * t + u, None, st)
                return st

            state = lax.fori_loop(0, n_full // unroll, trip, state)
            for d in range(n_diag):
                state = chunk(n_full + d, d * tc, state)
            for (a, b), (ms, ls, acc) in zip(pieces, state):
                o_ref[a:b, :] = (acc / _by_head(sel, ls)).astype(o_ref.dtype)
                for hh in heads:
                    lse_ref[hh, a:b, :] = ms[hh] + jnp.log(ls[hh])

    colspec = pl.BlockSpec((pack, tq, 1), lambda g, i: (g, i, 0))
    in_specs, args = [_q_cols(tq, q0), _k_cols(Sk, k0), _k_cols(Sk, v0)], [qa, ka, va]
    if kind == "fox":
        in_specs += [colspec, pl.BlockSpec((pack, Sk // tc, tc), lambda g, i: (g, 0, 0))]
        args += [ccol, crow]
    return pl.pallas_call(
        body, name="attn_fwd_" + kind, grid=(n_heads // pack, S // tq),
        in_specs=in_specs, out_specs=[_q_cols(tq, 0), colspec],
        out_shape=[jax.ShapeDtypeStruct((S, n_heads * dh), BF16), jax.ShapeDtypeStruct((n_heads, S, 1), F32)],
        compiler_params=_params(),
    )(*args)


def _attn_bwd(kind, q, k, v, o, do, n_heads, dh, ccol=None, crow=None, lse=None):
    (qa, q0), (ka, k0), (va, v0) = q, k, v
    S, Sk = qa.shape[0], ka.shape[0]
    pack = LANE // dh
    tq, tc = _attn_blocks(kind, S, Sk, backward=True)
    scale = dh ** -0.5
    fold = _is_power_of_two(scale)
    causal = kind != "mem"
    n_diag = tq // tc if causal else 0
    unroll = 4 if kind == "sb" else 1
    assert n_diag % unroll == 0 and (Sk // tc) % unroll == 0
    nq = S // tq

    def body(*refs):
        if kind == "fox":
            (q_ref, k_ref, v_ref, o_ref, do_ref, cc_ref, cr_ref, lse_ref,
             dq_ref, dk_ref, dv_ref, dc_ref, dcc_ref, dk_acc, dv_acc, dc_acc) = refs
        else:
            q_ref, k_ref, v_ref, o_ref, do_ref, lse_ref, dq_ref, dk_ref, dv_ref, dk_acc, dv_acc = refs
        i = pl.program_id(1)

        @pl.when(i == 0)
        def _():
            dk_acc[...] = jnp.zeros_like(dk_acc)
            dv_acc[...] = jnp.zeros_like(dv_acc)
            if kind == "fox":
                dc_acc[...] = jnp.zeros_like(dc_acc)

        n_full = (i * tq) // tc if causal else Sk // tc
        qpos = i * tq + lax.broadcasted_iota(jnp.int32, (tq, tc), 0)
        kio = lax.broadcasted_iota(jnp.int32, (tq, tc), 1)
        heads = range(pack)
        sel = _head_lanes(pack, dh)
        q2 = q_ref[...] * scale if fold else q_ref[...]
        do2 = do_ref[...]
        qs = [_only_head(sel, hh, q2) for hh in heads]
        dos = [_only_head(sel, hh, do2) for hh in heads]

        def kv(jc):
            off = pl.multiple_of(jc * tc, tc)
            return off, k_ref[pl.ds(off, tc), :], v_ref[pl.ds(off, tc), :]

        def accumulate(off, k2, dzb, wb, dq, r0):
            q2t, do2t = _tail(q2, r0), _tail(do2, r0)
            dk_acc[pl.ds(off, tc), :] += _by_head(sel, [_dot_tn(dzb[hh], q2t) for hh in heads])
            dv_acc[pl.ds(off, tc), :] += _by_head(sel, [_dot_tn(wb[hh], do2t) for hh in heads])
            return _add_tail(dq, _by_head(sel, [_dot(dzb[hh], k2) for hh in heads]), r0)

        if kind == "sb":
            r = lax.broadcasted_iota(jnp.int32, (tc, tc), 0)
            cidx = lax.broadcasted_iota(jnp.int32, (tc, tc), 1)
            tri_inc = (r <= cidx).astype(BF16)
            tri_exc = (r < cidx).astype(BF16)

            def chunk(jc, r0, pres, pres_e, dq):
                off, k2, v2 = kv(jc)
                new_pres, new_pres_e, dzb, wb = [], [], [], []
                for hh in heads:
                    lb, ln = _sb_logs(_dot_nt(_tail(qs[hh], r0), k2))
                    if r0 is not None:
                        mask = (off + _tail(kio, r0)) < _tail(qpos, r0)
                        ln = jnp.where(mask, ln, 0.0)
                    w = jnp.exp(lb + (_tail(lse_ref[hh], r0) - _tail(pres[hh], r0) - _cumdot(ln, tri_inc)))
                    if r0 is not None:
                        w = jnp.where(mask, w, 0.0)
                    e = w * _dot_nt(_tail(dos[hh], r0), v2)
                    beta = jnp.exp(lb)
                    dz = e * (1.0 - beta) - beta * (_tail(pres_e[hh], r0) + _dot(e.astype(BF16), tri_exc))
                    if r0 is not None:
                        dz = jnp.where(mask, dz, 0.0)
                    dzb.append(dz.astype(BF16))
                    wb.append(w.astype(BF16))
                    new_pres.append(_add_tail(pres[hh], jnp.sum(ln, axis=1, keepdims=True), r0))
                    new_pres_e.append(_add_tail(pres_e[hh], jnp.sum(e, axis=1, keepdims=True), r0))
                return tuple(new_pres), tuple(new_pres_e), accumulate(off, k2, dzb, wb, dq, r0)

            state = (tuple(jnp.zeros((tq, 1), F32) for _ in heads), tuple(jnp.zeros((tq, 1), F32) for _ in heads),
                     jnp.zeros((tq, LANE), F32))
        else:
            prod = o_ref[...].astype(F32) * do2.astype(F32)
            dsum = [jnp.sum(_only_head(sel, hh, prod), axis=1, keepdims=True) for hh in heads]

            def chunk(jc, r0, rowsums, dq):
                off, k2, v2 = kv(jc)
                new_rowsums, dsb, pb = [], [], []
                for hh in heads:
                    z = _dot_nt(_tail(qs[hh], r0), k2)
                    if not fold:
                        z = z * scale
                    if kind == "fox":
                        z = z + _tail(cc_ref[hh], r0) - cr_ref[hh, pl.ds(jc, 1), :]
                    if r0 is not None:
                        z = jnp.where((off + _tail(kio, r0)) <= _tail(qpos, r0), z, NEG)
                    p = jnp.exp(z - _tail(lse_ref[hh], r0))
                    ds = p * (_dot_nt(_tail(dos[hh], r0), v2) - _tail(dsum[hh], r0))
                    dsb.append(ds.astype(BF16))
                    pb.append(p.astype(BF16))
                    if kind == "fox":
                        dc_acc[hh, pl.ds(jc, 1), :] -= jnp.sum(ds, axis=0, keepdims=True)
                        new_rowsums.append(_add_tail(rowsums[hh], jnp.sum(ds, axis=1, keepdims=True), r0))
                    else:
                        new_rowsums.append(rowsums[hh])
                return tuple(new_rowsums), accumulate(off, k2, dsb, pb, dq, r0)

            state = (tuple(jnp.zeros((tq, 1), F32) for _ in heads), jnp.zeros((tq, LANE), F32))

        def trip(t, st):
            for u in range(unroll):
                st = chunk(unroll * t + u, None, *st)
            return st

        state = lax.fori_loop(0, n_full // unroll, trip, state)
        for d in range(n_diag):
            state = chunk(n_full + d, d * tc, *state)
        dq_ref[...] = (state[-1] * scale).astype(dq_ref.dtype)
        if kind == "fox":
            for hh in heads:
                dcc_ref[hh] = state[0][hh]

        @pl.when(i == nq - 1)
        def _():
            dk = dk_acc[...] if fold else dk_acc[...] * scale
            dk_ref[...] = dk.astype(dk_ref.dtype)
            dv_ref[...] = dv_acc[...].astype(dv_ref.dtype)
            if kind == "fox":
                dc_ref[...] = dc_acc[...]

    colspec = pl.BlockSpec((pack, tq, 1), lambda g, i: (g, i, 0))
    rowspec = pl.BlockSpec((pack, Sk // tc, tc), lambda g, i: (g, 0, 0))
    in_specs = [_q_cols(tq, q0), _k_cols(Sk, k0), _k_cols(Sk, v0), _q_cols(tq, 0), _q_cols(tq, 0)]
    args = [qa, ka, va, o, do]
    if kind == "fox":
        in_specs += [colspec, rowspec]
        args += [ccol, crow]
    in_specs += [colspec]
    args += [lse]
    width = n_heads * dh
    out_specs = [_q_cols(tq, 0), _k_cols(Sk, 0), _k_cols(Sk, 0)]
    out_shape = [jax.ShapeDtypeStruct((S, width), BF16), jax.ShapeDtypeStruct((Sk, width), BF16),
                 jax.ShapeDtypeStruct((Sk, width), BF16)]
    scratch = [pltpu.VMEM((Sk, LANE), F32), pltpu.VMEM((Sk, LANE), F32)]
    if kind == "fox":
        out_specs += [rowspec, colspec]
        out_shape += [jax.ShapeDtypeStruct((n_heads, Sk // tc, tc), F32), jax.ShapeDtypeStruct((n_heads, S, 1), F32)]
        scratch.append(pltpu.VMEM((pack, Sk // tc, tc), F32))
    return pl.pallas_call(
        body, name="attn_bwd_" + kind, grid=(n_heads // pack, nq),
        in_specs=in_specs, out_specs=out_specs, out_shape=out_shape, scratch_shapes=scratch,
        compiler_params=_params(),
    )(*args)


def _mem_norm(mem, g):
    M, D = mem.shape

    def body(mem_ref, g_ref, out_ref):
        out_ref[...] = _rms(mem_ref[...], g_ref[...]).astype(BF16)

    return pl.pallas_call(
        body, name="mem_norm", grid=(1,),
        in_specs=[_whole((M, D)), _whole((1, D))], out_specs=_whole((M, D)),
        out_shape=jax.ShapeDtypeStruct((M, D), BF16), compiler_params=_params(),
    )(mem, g)


def _mem_norm_bwd(mem, g, dmem_n):
    M, D = mem.shape
    L = dmem_n.shape[0]

    def body(mem_ref, g_ref, d_ref, dg_ref):
        d = d_ref[0]
        for l in range(1, L):
            d = d + d_ref[l]
        _, vjp = jax.vjp(_rms, mem_ref[...], g_ref[...])
        dg_ref[...] = vjp(d)[1]

    return pl.pallas_call(
        body, name="mem_norm_bwd", grid=(1,),
        in_specs=[_whole((M, D)), _whole((1, D)), _whole((L, M, D))], out_specs=_whole((1, D)),
        out_shape=jax.ShapeDtypeStruct((1, D), F32), compiler_params=_params(),
    )(mem, g, dmem_n)


def _loss_head(h, target):
    S, D = h.shape
    bm = _pick(S, 512)

    def body(h_ref, t_ref, dh_ref, loss_ref):
        err = h_ref[...] - t_ref[...]
        dh_ref[...] = err * (1.0 / D)

        @pl.when(pl.program_id(0) == 0)
        def _():
            loss_ref[...] = jnp.zeros_like(loss_ref)

        loss_ref[...] += 0.5 * jnp.sum(jnp.mean(err * err, axis=-1, keepdims=True), axis=0, keepdims=True)

    return pl.pallas_call(
        body, name="loss_head", grid=(S // bm,),
        in_specs=[_rows(bm, D), _rows(bm, D)], out_specs=[_rows(bm, D), _whole((8, LANE))],
        out_shape=[jax.ShapeDtypeStruct((S, D), F32), jax.ShapeDtypeStruct((8, LANE), F32)],
        compiler_params=_params(),
    )(h, target)


def _adamw(w, g, m, v, name):
    R, C = w.shape
    rb = R if R * C * 4 <= (1 << 20) else _pick(R, 256)
    if R % rb:
        rb = R
    c1 = 1.0 - ADAM_B1 ** ADAM_STEP
    c2 = 1.0 - ADAM_B2 ** ADAM_STEP

    def body(w_ref, g_ref, m_ref, v_ref, d_ref, mo_ref, vo_ref):
        gv = g_ref[...]
        mn = ADAM_B1 * m_ref[...] + (1.0 - ADAM_B1) * gv
        vn = ADAM_B2 * v_ref[...] + (1.0 - ADAM_B2) * (gv * gv)
        mo_ref[...] = mn
        vo_ref[...] = vn
        d_ref[...] = -ADAM_LR * ((mn / c1) / (jnp.sqrt(vn / c2) + ADAM_EPS) + ADAM_WD * w_ref[...])

    return pl.pallas_call(
        body, name=name, grid=(R // rb,),
        in_specs=[_rows(rb, C)] * 4, out_specs=[_rows(rb, C)] * 3,
        out_shape=[jax.ShapeDtypeStruct((R, C), F32)] * 3, compiler_params=_params(),
    )(w, g, m, v)


def _adamw_reduced(w, m, v, mine, theirs, c_idx, first_row, name):
    L, a, b = w.shape
    Lh = L // 2
    rb = _shard_row_block(a)
    nb = a // rb
    assert first_row % rb == 0
    c1 = 1.0 - ADAM_B1 ** ADAM_STEP
    c2 = 1.0 - ADAM_B2 ** ADAM_STEP

    def own(i, c_ref):
        return (i, 0)

    def reduced(i, c_ref):
        return (first_row // rb + ((i // nb) % Lh) * nb + i % nb, 0)

    def body(c_ref, w_ref, m_ref, v_ref, mine_ref, theirs_ref, g_ref, d_ref, mo_ref, vo_ref):
        half = (pl.program_id(0) // nb) // Lh
        gv = jnp.where(c_ref[0] == half, mine_ref[...], theirs_ref[...])
        g_ref[...] = gv
        mn = ADAM_B1 * m_ref[...] + (1.0 - ADAM_B1) * gv
        vn = ADAM_B2 * v_ref[...] + (1.0 - ADAM_B2) * (gv * gv)
        mo_ref[...] = mn
        vo_ref[...] = vn
        d_ref[...] = -ADAM_LR * ((mn / c1) / (jnp.sqrt(vn / c2) + ADAM_EPS) + ADAM_WD * w_ref[...])

    outs = pl.pallas_call(
        body, name=name,
        grid_spec=pltpu.PrefetchScalarGridSpec(
            num_scalar_prefetch=1, grid=(L * nb,),
            in_specs=[pl.BlockSpec((rb, b), own)] * 3 + [pl.BlockSpec((rb, b), reduced)] * 2,
            out_specs=[pl.BlockSpec((rb, b), own)] * 4),
        out_shape=[jax.ShapeDtypeStruct((L * a, b), F32)] * 4, compiler_params=_params(),
    )(c_idx, w.reshape(L * a, b), m.reshape(L * a, b), v.reshape(L * a, b), mine, theirs)
    return [t.reshape(L, a, b) for t in outs]


ANY = pl.BlockSpec(memory_space=pl.ANY)
MESH = pl.DeviceIdType.MESH


def _place():
    x, y, c = lax.axis_index("x"), lax.axis_index("y"), lax.axis_index("c")
    others = [(1 - x, y), (x, 1 - y), (1 - x, 1 - y)]
    return x, y, c, others


def _place_own(loc, chip_idx):
    _, R, C = loc.shape
    rb = _pick(R, 2 * FLAT_ROW_BLOCK)

    def body(chip_ref, loc_ref, out_ref):
        out_ref[...] = loc_ref[...]

    return pl.pallas_call(
        body, name="place_own",
        grid_spec=pltpu.PrefetchScalarGridSpec(
            num_scalar_prefetch=1, grid=(2, R // rb),
            in_specs=[pl.BlockSpec((None, rb, C), lambda hf, i, chip_ref: (hf, i, 0))],
            out_specs=pl.BlockSpec((None, None, rb, C), lambda hf, i, chip_ref: (chip_ref[0], hf, i, 0))),
        out_shape=jax.ShapeDtypeStruct((N_CHIPS, 2, R, C), loc.dtype), compiler_params=_params(),
    )(chip_idx, loc)


def _gather_weights(locs, owns):
    n = len(locs)

    def body(*refs):
        loc_refs, out_refs, (send_sems, recv_sems) = refs[:n], refs[2 * n:3 * n], refs[3 * n:]
        x, y, c, others = _place()
        me = 2 * x + y
        sibling = (x, y, 1 - c)

        def copy(a, k, src, dst, to):
            return pltpu.make_async_remote_copy(src_ref=src, dst_ref=dst, send_sem=send_sems.at[a, k],
                                                recv_sem=recv_sems.at[a, k], device_id=to, device_id_type=MESH)

        first = [copy(a, j, loc_refs[a].at[c], out_refs[a].at[me, c], (ox, oy, c))
                 for j, (ox, oy) in enumerate(others) for a in range(n)]
        for cp in first:
            cp.start()
        passed = []
        for j, (ox, oy) in enumerate(others):
            for a in range(n):
                landed = out_refs[a].at[2 * ox + oy, c]
                copy(a, j, loc_refs[a].at[c], landed, sibling).wait_recv()
                cp = copy(a, 3 + j, landed, landed, sibling)
                cp.start()
                passed.append(cp)
        for j, (ox, oy) in enumerate(others):
            for a in range(n):
                copy(a, 3 + j, loc_refs[a].at[c], out_refs[a].at[2 * ox + oy, 1 - c], sibling).wait_recv()
        for cp in first + passed:
            cp.wait_send()

    return pl.pallas_call(
        body, name="gather_weights", in_specs=[ANY] * (2 * n), out_specs=[ANY] * n,
        out_shape=[jax.ShapeDtypeStruct(own.shape, own.dtype) for own in owns],
        input_output_aliases={n + a: a for a in range(n)},
        scratch_shapes=[pltpu.SemaphoreType.DMA((n, 6)), pltpu.SemaphoreType.DMA((n, 6))],
    )(*locs, *owns)


def _pair_exchange(gs):
    n = len(gs)

    def body(*refs):
        g_refs, out_refs, (send_sems, recv_sems) = refs[:n], refs[n:2 * n], refs[2 * n:]
        x, y, c, _ = _place()
        copies = [pltpu.make_async_remote_copy(src_ref=g_refs[a].at[1 - c], dst_ref=out_refs[a],
                                               send_sem=send_sems.at[a], recv_sem=recv_sems.at[a],
                                               device_id=(x, y, 1 - c), device_id_type=MESH) for a in range(n)]
        for cp in copies:
            cp.start()
        for cp in copies:
            cp.wait()

    return pl.pallas_call(
        body, name="pair_exchange", in_specs=[ANY] * n, out_specs=[ANY] * n,
        out_shape=[jax.ShapeDtypeStruct(g.shape[1:], g.dtype) for g in gs],
        scratch_shapes=[pltpu.SemaphoreType.DMA((n,)), pltpu.SemaphoreType.DMA((n,))],
    )(*gs)


def _pair_sum(g, sib, c_idx):
    _, _, R, C = g.shape
    rb = _pick(R, 512)

    def body(c_ref, g_ref, s_ref, o_ref):
        o_ref[...] = (g_ref[...].astype(F32) + s_ref[...].astype(F32)).astype(o_ref.dtype)

    return pl.pallas_call(
        body, name="pair_sum",
        grid_spec=pltpu.PrefetchScalarGridSpec(
            num_scalar_prefetch=1, grid=(N_CHIPS, R // rb),
            in_specs=[pl.BlockSpec((None, None, rb, C), lambda j, i, c_ref: (c_ref[0], j, i, 0)),
                      pl.BlockSpec((None, rb, C), lambda j, i, c_ref: (j, i, 0))],
            out_specs=pl.BlockSpec((None, rb, C), lambda j, i, c_ref: (j, i, 0))),
        out_shape=jax.ShapeDtypeStruct((N_CHIPS, R, C), g.dtype), compiler_params=_params(),
    )(c_idx, g, sib)


def _chip_exchange(ps):
    n = len(ps)

    def body(*refs):
        p_refs, out_refs, (send_sems, recv_sems) = refs[:n], refs[n:2 * n], refs[2 * n:]
        x, y, c, others = _place()
        copies = []
        for j, (ox, oy) in enumerate(others):
            for a in range(n):
                cp = pltpu.make_async_remote_copy(src_ref=p_refs[a].at[2 * ox + oy], dst_ref=out_refs[a].at[j],
                                                  send_sem=send_sems.at[a, j], recv_sem=recv_sems.at[a, j],
                                                  device_id=(ox, oy, c), device_id_type=MESH)
                cp.start()
                copies.append(cp)
        for cp in copies:
            cp.wait()

    return pl.pallas_call(
        body, name="chip_exchange", in_specs=[ANY] * n, out_specs=[ANY] * n,
        out_shape=[jax.ShapeDtypeStruct((N_CHIPS - 1,) + p.shape[1:], p.dtype) for p in ps],
        scratch_shapes=[pltpu.SemaphoreType.DMA((n, 3)), pltpu.SemaphoreType.DMA((n, 3))],
    )(*ps)


def _chip_sum(p, r, chip_idx):
    _, R, C = r.shape
    rb = _pick(R, 512)

    def body(chip_ref, p_ref, r_ref, o_ref):
        acc = p_ref[...].astype(F32)
        for j in range(N_CHIPS - 1):
            acc = acc + r_ref[j].astype(F32)
        o_ref[...] = acc

    return pl.pallas_call(
        body, name="chip_sum",
        grid_spec=pltpu.PrefetchScalarGridSpec(
            num_scalar_prefetch=1, grid=(R // rb,),
            in_specs=[pl.BlockSpec((None, rb, C), lambda i, chip_ref: (chip_ref[0], i, 0)),
                      pl.BlockSpec((N_CHIPS - 1, rb, C), lambda i, chip_ref: (0, i, 0))],
            out_specs=pl.BlockSpec((rb, C), lambda i, chip_ref: (i, 0))),
        out_shape=jax.ShapeDtypeStruct((R, C), F32), compiler_params=_params(),
    )(chip_idx, p, r)


def _pair_swap(rhs):
    n = len(rhs)

    def body(*refs):
        rh_refs, out_refs, (send_sems, recv_sems) = refs[:n], refs[n:2 * n], refs[2 * n:]
        x, y, c, _ = _place()
        copies = [pltpu.make_async_remote_copy(src_ref=rh_refs[a], dst_ref=out_refs[a], send_sem=send_sems.at[a],
                                               recv_sem=recv_sems.at[a], device_id=(x, y, 1 - c),
                                               device_id_type=MESH) for a in range(n)]
        for cp in copies:
            cp.start()
        for cp in copies:
            cp.wait()

    return pl.pallas_call(
        body, name="pair_swap", in_specs=[ANY] * n, out_specs=[ANY] * n,
        out_shape=[jax.ShapeDtypeStruct(rh.shape, rh.dtype) for rh in rhs],
        scratch_shapes=[pltpu.SemaphoreType.DMA((n,)), pltpu.SemaphoreType.DMA((n,))],
    )(*rhs)


def _all_reduce_small(s):
    R, C = s.shape

    def body(s_ref, o_ref, buf, send_sems, recv_sems):
        x, y, c, _ = _place()
        me = 4 * x + 2 * y + c
        sends = []
        for k in range(1, N_DEV):
            fx, fy, fc = (k >> 2) & 1, (k >> 1) & 1, k & 1
            to = (x ^ fx, y ^ fy, c ^ fc)
            cp = pltpu.make_async_remote_copy(src_ref=s_ref, dst_ref=buf.at[me], send_sem=send_sems.at[k - 1],
                                              recv_sem=recv_sems.at[k - 1], device_id=to, device_id_type=MESH)
            cp.start()
            sends.append(cp)
        buf[me] = s_ref[...]
        for k in range(1, N_DEV):
            fx, fy, fc = (k >> 2) & 1, (k >> 1) & 1, k & 1
            frm = 4 * (x ^ fx) + 2 * (y ^ fy) + (c ^ fc)
            pltpu.make_async_remote_copy(src_ref=s_ref, dst_ref=buf.at[frm], send_sem=send_sems.at[k - 1],
                                         recv_sem=recv_sems.at[k - 1], device_id=(x, y, c),
                                         device_id_type=MESH).wait_recv()
        acc = buf[0]
        for d in range(1, N_DEV):
            acc = acc + buf[d]
        o_ref[...] = acc
        for cp in sends:
            cp.wait_send()

    vm = pl.BlockSpec(memory_space=pltpu.VMEM)
    return pl.pallas_call(
        body, name="all_reduce_small", in_specs=[vm], out_specs=vm,
        out_shape=jax.ShapeDtypeStruct((R, C), F32),
        scratch_shapes=[pltpu.VMEM((N_DEV, R, C), F32), pltpu.SemaphoreType.DMA((N_DEV - 1,)),
                        pltpu.SemaphoreType.DMA((N_DEV - 1,))],
    )(s)


def _padded(n):
    return -(-n // FLAT_UNIT) * FLAT_UNIT


def _pack_flat(pieces, dtype, row_block=FLAT_ROW_BLOCK):
    flat = []
    for p in pieces:
        p = p.reshape(-1).astype(dtype)
        flat.append(jnp.pad(p, (0, _padded(p.size) - p.size)))
    total = sum(p.size for p in flat)
    flat.append(jnp.zeros((-total) % (row_block * FLAT_COLS), dtype))
    return jnp.concatenate(flat).reshape(-1, FLAT_COLS)


def _unpack_flat(flat, shapes):
    lead = flat.shape[:-2]
    flat = flat.reshape(lead + (-1,))
    out, off = [], 0
    for shp in shapes:
        n = math.prod(shp)
        out.append(flat[..., off:off + n].reshape(lead + tuple(shp)))
        off += _padded(n)
    return out


def _shard_row_block(a):
    for rb in range(min(a, 512) // 16 * 16, 0, -16):
        if a % rb == 0:
            return rb
    return a


def _row_layout(shapes, n_layers):
    groups = {}
    for name, (a, b) in shapes.items():
        names, first, rows = groups.get(b, ((), {}, 0))
        rb = _shard_row_block(a)
        start = -(-rows // rb) * rb
        groups[b] = (names + (name,), {**first, name: start}, start + n_layers * a)
    return {b: (names, first, -(-rows // FLAT_ROW_BLOCK) * FLAT_ROW_BLOCK) for b, (names, first, rows) in groups.items()}


def _pack_rows(group, width, pieces, dtype):
    names, first, rows = group
    parts, at = [], 0
    for name in names:
        if first[name] > at:
            parts.append(jnp.zeros((first[name] - at, width), dtype))
        parts.append(pieces[name].astype(dtype))
        at = first[name] + pieces[name].shape[0]
    if rows > at:
        parts.append(jnp.zeros((rows - at, width), dtype))
    return jnp.concatenate(parts, axis=0)


def _slab(t, axis, j):
    if t.ndim == 3:
        return t[j]
    n = t.shape[axis - 1] // N_CHIPS
    return lax.slice_in_dim(t, j * n, (j + 1) * n, axis=axis - 1)


def _layer_fwd(h0, mem_n, wl, dims):
    n_sb, n_fx, n_mem, sbw, fxw, memw = dims
    n1, gate1, up1, a1 = _ffn_fwd_up(h0, wl["ffn1_pre_g"], wl["ffn1_w_gate"], wl["ffn1_w_up"])
    h1, f1 = _ffn_fwd_down(a1, wl["ffn1_w_down"], h0, wl["ffn1_post_g"])

    u, proj, fl, sg = _mix_fwd_in(h1, wl["mix_pre_g"], wl["w_in"], wl["w_gate"], wl["b_gate"], wl["b_forget"])
    c = _fox_cumsum(fl)
    S = h0.shape[0]
    tc = _attn_blocks("fox", S, S)[1]
    ct = c[:, :n_fx].T
    ccol, crow = ct.reshape(n_fx, S, 1), ct.reshape(n_fx, S // tc, tc)
    qkv_sb = [(proj, k * sbw) for k in range(3)]
    qkv_fx = [(proj, 3 * sbw + k * fxw) for k in range(3)]
    kv = _matmul(mem_n, wl["w_mem_kv"], out_dtype=BF16, name="mem_kv")
    qkv_mem = [(proj, 3 * sbw + 3 * fxw), (kv, 0), (kv, memw)]
    o_sb, tot_sb = _attn_fwd("sb", *qkv_sb, n_sb, HEAD_DIM)
    o_fx, lse_fx = _attn_fwd("fox", *qkv_fx, n_fx, HEAD_DIM, ccol, crow)
    o_mem, lse_mem = _attn_fwd("mem", *qkv_mem, n_mem, MEM_HEAD_DIM)
    h2, zmix, merged = _mix_fwd_out(o_sb, o_fx, o_mem, sg, wl["w_br_sb"], wl["w_br_fox"], wl["w_br_mem"],
                                    wl["w_out"], h1, wl["mix_post_g"])

    n2, gate2, up2, a2 = _ffn_fwd_up(h2, wl["ffn2_pre_g"], wl["ffn2_w_gate"], wl["ffn2_w_up"])
    h3, f2 = _ffn_fwd_down(a2, wl["ffn2_w_down"], h2, wl["ffn2_post_g"])
    saved = dict(h0=h0, n1=n1, gate1=gate1, up1=up1, a1=a1, f1=f1, h1=h1, u=u, fl=fl, sg=sg,
                 qkv_sb=qkv_sb, qkv_fx=qkv_fx, qkv_mem=qkv_mem, ccol=ccol, crow=crow, o_sb=o_sb, o_fx=o_fx, o_mem=o_mem,
                 tot_sb=tot_sb, lse_fx=lse_fx, lse_mem=lse_mem,
                 zmix=zmix, merged=merged, h2=h2, n2=n2, gate2=gate2, up2=up2, a2=a2, f2=f2)
    return h3, saved


def _ffn_bwd(dh, sv, wl, tag, h_in):
    n, gate, up, a, f = (sv[k + tag] for k in ("n", "gate", "up", "a", "f"))
    pre = "ffn" + tag
    df, dgate, dup, dg_post = _ffn_bwd_down(dh, f, wl[pre + "_post_g"], wl[pre + "_w_down"], gate, up)
    dh_in, dg_pre = _ffn_bwd_up(dgate, dup, wl[pre + "_w_gate"], wl[pre + "_w_up"], h_in, wl[pre + "_pre_g"], dh)
    grads = {pre + "_post_g": dg_post, pre + "_pre_g": dg_pre,
             pre + "_w_down": _matmul(a, df, ta=True, batch="a", name="dw_down"),
             pre + "_w_gate": _matmul(n, dgate, ta=True, batch="b", name="dw_gate"),
             pre + "_w_up": _matmul(n, dup, ta=True, batch="b", name="dw_up")}
    return dh_in, grads


def _layer_bwd(dh3, mem_n, wl, sv, dims):
    n_sb, n_fx, n_mem, sbw, fxw, memw = dims
    S = dh3.shape[0]
    dh2, grads = _ffn_bwd(dh3, sv, wl, "2", sv["h2"])

    (dz, db_sb, db_fx, db_mem, do_sb, do_fx, do_mem, dgp, db_gate, dg_post) = _mix_bwd_out(
        dh2, sv["zmix"], wl["mix_post_g"], wl["w_out"], sv["o_sb"], sv["o_fx"], sv["o_mem"],
        wl["w_br_sb"], wl["w_br_fox"], wl["w_br_mem"], sv["sg"])
    grads["mix_post_g"] = dg_post
    grads["b_gate"] = db_gate
    grads["w_out"] = _matmul(sv["merged"], dz, ta=True, name="dw_out")
    grads["w_br_sb"] = _matmul(sv["o_sb"], db_sb, ta=True, name="dw_br_sb")
    grads["w_br_fox"] = _matmul(sv["o_fx"], db_fx, ta=True, name="dw_br_fox")
    grads["w_br_mem"] = _matmul(sv["o_mem"], db_mem, ta=True, name="dw_br_mem")

    dq_sb, dk_sb, dv_sb = _attn_bwd("sb", *sv["qkv_sb"], sv["o_sb"], do_sb, n_sb, HEAD_DIM, lse=sv["tot_sb"])
    dq_fx, dk_fx, dv_fx, dcrow, dccol = _attn_bwd("fox", *sv["qkv_fx"], sv["o_fx"], do_fx, n_fx, HEAD_DIM,
                                                  sv["ccol"], sv["crow"], sv["lse_fx"])
    dq_mem, dk_mem, dv_mem = _attn_bwd("mem", *sv["qkv_mem"], sv["o_mem"], do_mem, n_mem, MEM_HEAD_DIM,
                                       lse=sv["lse_mem"])
    dkv = jnp.concatenate([dk_mem, dv_mem], axis=1)
    grads["w_mem_kv"] = _matmul(mem_n, dkv, ta=True, name="dw_mem_kv")
    dmem_n = _matmul(dkv, wl["w_mem_kv"], tb=True, out_dtype=F32, name="dmem_n")

    dc = jnp.pad((dcrow.reshape(n_fx, S) + dccol.reshape(n_fx, S)).T, ((0, 0), (0, LANE - n_fx)))
    dfl, db_forget = _fox_dlogit(dc, sv["fl"])
    grads["b_forget"] = db_forget
    dproj = jnp.concatenate([dq_sb, dk_sb, dv_sb, dq_fx, dk_fx, dv_fx, dq_mem, dfl], axis=1)
    dh1, dg_pre = _mix_bwd_in(dproj, dgp, wl["w_in"], wl["w_gate"], sv["h1"], wl["mix_pre_g"], dh2)
    grads["mix_pre_g"] = dg_pre
    grads["w_in"] = _matmul(sv["u"], dproj, ta=True, name="dw_in")
    grads["w_gate"] = _matmul(sv["u"], dgp, ta=True, name="dw_gate_mix")

    dh0, g1 = _ffn_bwd(dh1, sv, wl, "1", sv["h0"])
    grads.update(g1)
    return dh0, grads, dmem_n


def kernel(x, mem, ffn1_pre_g, ffn1_post_g, ffn1_w_gate, ffn1_w_up, ffn1_w_down, mix_pre_g, mix_post_g, w_in, b_forget, mem_norm_g, w_mem_kv, w_gate, b_gate, w_br_sb, w_br_fox, w_br_mem, w_out, ffn2_pre_g, ffn2_post_g, ffn2_w_gate, ffn2_w_up, ffn2_w_down, loss_target, m_ffn1_pre_g, m_ffn1_post_g, m_ffn1_w_gate, m_ffn1_w_up, m_ffn1_w_down, m_mix_pre_g, m_mix_post_g, m_w_in, m_b_forget, m_mem_norm_g, m_w_mem_kv, m_w_gate, m_b_gate, m_w_br_sb, m_w_br_fox, m_w_br_mem, m_w_out, m_ffn2_pre_g, m_ffn2_post_g, m_ffn2_w_gate, m_ffn2_w_up, m_ffn2_w_down, v_ffn1_pre_g, v_ffn1_post_g, v_ffn1_w_gate, v_ffn1_w_up, v_ffn1_w_down, v_mix_pre_g, v_mix_post_g, v_w_in, v_b_forget, v_mem_norm_g, v_w_mem_kv, v_w_gate, v_b_gate, v_w_br_sb, v_w_br_fox, v_w_br_mem, v_w_out, v_ffn2_pre_g, v_ffn2_post_g, v_ffn2_w_gate, v_ffn2_w_up, v_ffn2_w_down):
    args = dict(locals())
    w = {n: args[n] for n in WEIGHTS}
    m = {n: args["m_" + n] for n in WEIGHTS}
    v = {n: args["v_" + n] for n in WEIGHTS}
    L = w["ffn1_pre_g"].shape[0]
    Lh = L // 2
    D = x.shape[2]
    sbw, fxw, memw = w["w_br_sb"].shape[1], w["w_br_fox"].shape[1], w["w_br_mem"].shape[1]
    n_sb, n_fx, n_mem = sbw // HEAD_DIM, fxw // HEAD_DIM, memw // MEM_HEAD_DIM
    dims = (n_sb, n_fx, n_mem, sbw, fxw, memw)
    qkv_w = 3 * sbw + 3 * fxw
    c_idx = lax.axis_index("c")
    c_arr = c_idx.reshape(1).astype(jnp.int32)
    chip_arr = (2 * lax.axis_index("x") + lax.axis_index("y")).reshape(1).astype(jnp.int32)

    shard_shapes = {n: w[n].shape[1:] for n, _ in BIG}
    layout = _row_layout(shard_shapes, Lh)
    widths = list(layout)
    locs = [jnp.stack([_pack_rows(layout[b], b, {n: w[n][hf * Lh:(hf + 1) * Lh].reshape(-1, b) for n in layout[b][0]},
                                  BF16) for hf in range(2)]) for b in widths]
    gathered = dict(zip(widths, _gather_weights(locs, [_place_own(loc, chip_arr) for loc in locs])))

    def layer_weights(l):
        hf, li = divmod(l, Lh)
        wl = {}
        for n, axis in BIG:
            a, b = shard_shapes[n]
            r0 = layout[b][1][n] + li * a
            shards = gathered[b][:, hf, r0:r0 + a]
            if n.startswith("ffn"):
                wl[n] = shards
            else:
                wl[n] = (shards.transpose(1, 0, 2).reshape(a, N_CHIPS * b) if axis == 2 else
                         shards.reshape(N_CHIPS * a, b))
        wi = wl["w_in"]
        wl["w_in"] = jnp.concatenate([wi[:, :qkv_w], wi[:, qkv_w + n_fx:], wi[:, qkv_w:qkv_w + n_fx],
                                      jnp.zeros((D, LANE - n_fx), BF16)], axis=1)
        for n in SMALL:
            if n != "mem_norm_g":
                wl[n] = w[n][l][None, :]
        wl["b_forget"] = jnp.pad(wl["b_forget"], ((0, 0), (0, LANE - n_fx)))
        return wl

    g_mem = w["mem_norm_g"][None, :]

    mem_n = _mem_norm(mem[0], g_mem)
    h, wls, saved = x[0], [], []
    for l in range(L):
        wls.append(layer_weights(l))
        h, sv = _layer_fwd(h, mem_n, wls[l], dims)
        saved.append(sv)
    dh, loss_tile = _loss_head(h, loss_target[0])
    loss = lax.psum(loss_tile[0, 0], ("x", "y", "c"))
    gl, dmem_n = [None] * L, [None] * L
    for l in reversed(range(L)):
        dh, gl[l], dmem_n[l] = _layer_bwd(dh, mem_n, wls[l], saved[l], dims)
        gi = gl[l]["w_in"]
        gl[l]["w_in"] = jnp.concatenate([gi[:, :qkv_w], gi[:, qkv_w + memw:qkv_w + memw + n_fx],
                                         gi[:, qkv_w:qkv_w + memw]], axis=1)
    grad_x = dh
    g_mem_norm = _mem_norm_bwd(mem[0], g_mem, jnp.stack(dmem_n))

    axis_of = dict(BIG)
    partials = [jnp.stack([jnp.stack([
        _pack_rows(layout[b], b, {n: jnp.concatenate([_slab(gl[hf * Lh + li][n], axis_of[n], j) for li in range(Lh)])
                                  for n in layout[b][0]}, BF16)
        for j in range(N_CHIPS)]) for hf in range(2)]) for b in widths]
    pairs = [_pair_sum(g, sib, c_arr) for g, sib in zip(partials, _pair_exchange(partials))]
    mines = [_chip_sum(p, r, chip_arr) for p, r in zip(pairs, _chip_exchange(pairs))]
    theirs = _pair_swap(mines)

    grad, delta, new_m, new_v = {}, {}, {}, {}
    for n, _ in BIG:
        k = widths.index(shard_shapes[n][1])
        grad[n], delta[n], new_m[n], new_v[n] = _adamw_reduced(
            w[n], m[n], v[n], mines[k], theirs[k], c_arr, layout[widths[k]][1][n], name="adamw_" + n)

    small_local = {n: (g_mem_norm if n == "mem_norm_g" else
                       jnp.concatenate([gl[l][n][:, :n_fx] if n == "b_forget" else gl[l][n] for l in range(L)]))
                   for n in SMALL}
    small_shapes = [small_local[n].shape for n in SMALL]
    small_sum = _unpack_flat(_all_reduce_small(_pack_flat([small_local[n] for n in SMALL], F32, row_block=16)),
                             small_shapes)
    for n, t in zip(SMALL, small_sum):
        shp = w[n].shape
        two_d = (1, shp[0]) if len(shp) == 1 else shp
        grad[n] = t.reshape(shp)
        d_, m_, v_ = _adamw(w[n].reshape(two_d), t.reshape(two_d), m[n].reshape(two_d), v[n].reshape(two_d),
                            name="adamw_" + n)
        delta[n], new_m[n], new_v[n] = d_.reshape(shp), m_.reshape(shp), v_.reshape(shp)

    return (loss, grad_x[None], *[grad[n] for n in WEIGHTS], *[delta[n] for n in WEIGHTS],
            *[new_m[n] for n in WEIGHTS], *[new_v[n] for n in WEIGHTS])
```

```python
import math

import jax
import jax.numpy as jnp
from jax import lax
from jax.experimental import pallas as pl
from jax.experimental.pallas import tpu as pltpu

F32 = jnp.float32
BF16 = jnp.bfloat16
RMS_EPS = 1e-6
HEAD_DIM = 64
MEM_HEAD_DIM = 128
LANE = 128
V7X_VMEM_LIMIT_BYTES = 56 * 1024 * 1024
FLAT_COLS = 512
FLAT_UNIT = 16 * FLAT_COLS
FLAT_ROW_BLOCK = 512
N_CHIPS = 4
N_DEV = 8
NEG = float(jnp.finfo(jnp.float32).min)

ADAM_LR = 0.001
ADAM_B1 = 0.9
ADAM_B2 = 0.999
ADAM_EPS = 1e-08
ADAM_WD = 0.01
ADAM_STEP = 10

BIG = (("ffn1_w_gate", 2), ("ffn1_w_up", 2), ("ffn1_w_down", 1), ("w_in", 2), ("w_mem_kv", 1), ("w_gate", 2),
       ("w_br_sb", 2), ("w_br_fox", 2), ("w_br_mem", 2), ("w_out", 1),
       ("ffn2_w_gate", 2), ("ffn2_w_up", 2), ("ffn2_w_down", 1))
SMALL = ("ffn1_pre_g", "ffn1_post_g", "mix_pre_g", "mix_post_g", "b_forget", "mem_norm_g", "b_gate",
         "ffn2_pre_g", "ffn2_post_g")
WEIGHTS = ("ffn1_pre_g", "ffn1_post_g", "ffn1_w_gate", "ffn1_w_up", "ffn1_w_down", "mix_pre_g", "mix_post_g", "w_in",
           "b_forget", "mem_norm_g", "w_mem_kv", "w_gate", "b_gate", "w_br_sb", "w_br_fox", "w_br_mem", "w_out",
           "ffn2_pre_g", "ffn2_post_g", "ffn2_w_gate", "ffn2_w_up", "ffn2_w_down")


def _params(**kw):
    return pltpu.CompilerParams(vmem_limit_bytes=V7X_VMEM_LIMIT_BYTES, **kw)


def _dot(a, b):
    return jnp.dot(a, b, preferred_element_type=F32)


def _dot_nt(a, b):
    return lax.dot_general(a, b, (((1,), (1,)), ((), ())), preferred_element_type=F32)


def _dot_tn(a, b):
    return lax.dot_general(a, b, (((0,), (0,)), ((), ())), preferred_element_type=F32)


def _rms(t, g):
    return t * lax.rsqrt(jnp.mean(t * t, axis=-1, keepdims=True) + RMS_EPS) * g


def _pick(dim, pref):
    if dim <= pref:
        return dim
    for cand in range(pref - pref % LANE, 0, -LANE):
        if dim % cand == 0:
            return cand
    return dim


def _rows(bm, cols):
    return pl.BlockSpec((bm, cols), lambda i: (i, 0))


def _whole(shape):
    nd = len(shape)
    return pl.BlockSpec(shape, lambda i: (0,) * nd)


def _split3(x):
    hi = x.astype(BF16)
    r1 = x - hi.astype(F32)
    mid = r1.astype(BF16)
    lo = (r1 - mid.astype(F32)).astype(BF16)
    return hi, mid, lo


def _cumdot(x, tri):
    hi = x.astype(BF16)
    lo = (x - hi.astype(F32)).astype(BF16)
    return _dot(hi, tri) + _dot(lo, tri)


def _slabs(bm, cols):
    return pl.BlockSpec((N_CHIPS, bm, cols), lambda i: (0, i, 0))


def _ffn_fwd_up(h, g_pre, wg, wu):
    S, D = h.shape
    Fs = wg.shape[2]
    bm = _pick(S, 256)

    def body(h_ref, g_ref, wg_ref, wu_ref, n_ref, gate_ref, up_ref, a_ref):
        n = _rms(h_ref[...], g_ref[...]).astype(BF16)
        n_ref[...] = n
        for j in range(N_CHIPS):
            gate = _dot(n, wg_ref[j])
            up = _dot(n, wu_ref[j])
            gate_ref[j] = gate.astype(BF16)
            up_ref[j] = up.astype(BF16)
            a_ref[j] = (gate * jax.nn.sigmoid(gate) * up).astype(BF16)

    return pl.pallas_call(
        body, name="ffn_fwd_up", grid=(S // bm,),
        in_specs=[_rows(bm, D), _whole((1, D)), _whole((N_CHIPS, D, Fs)), _whole((N_CHIPS, D, Fs))],
        out_specs=[_rows(bm, D), _slabs(bm, Fs), _slabs(bm, Fs), _slabs(bm, Fs)],
        out_shape=[jax.ShapeDtypeStruct((S, D), BF16)] + [jax.ShapeDtypeStruct((N_CHIPS, S, Fs), BF16)] * 3,
        compiler_params=_params(),
    )(h, g_pre, wg, wu)


def _ffn_fwd_down(a, wd, h, g_post):
    _, S, Fs = a.shape
    D = wd.shape[2]
    bm = _pick(S, 256)

    def body(a_ref, wd_ref, h_ref, g_ref, hout_ref, f_ref):
        f = _dot(a_ref[0], wd_ref[0])
        for j in range(1, N_CHIPS):
            f = f + _dot(a_ref[j], wd_ref[j])
        f_ref[...] = f
        hout_ref[...] = h_ref[...] + 0.5 * _rms(f, g_ref[...])

    return pl.pallas_call(
        body, name="ffn_fwd_down", grid=(S // bm,),
        in_specs=[_slabs(bm, Fs), _whole((N_CHIPS, Fs, D)), _rows(bm, D), _whole((1, D))],
        out_specs=[_rows(bm, D), _rows(bm, D)],
        out_shape=[jax.ShapeDtypeStruct((S, D), F32)] * 2,
        compiler_params=_params(),
    )(a, wd, h, g_post)


def _ffn_bwd_down(dh, f, g_post, wd, gate, up):
    S, D = dh.shape
    Fs = wd.shape[1]
    bm = _pick(S, 256)

    def body(dh_ref, f_ref, g_ref, wd_ref, gate_ref, up_ref, df_ref, dgate_ref, dup_ref, dg_ref):
        _, vjp = jax.vjp(lambda t, g: 0.5 * _rms(t, g), f_ref[...], g_ref[...])
        df, dg = vjp(dh_ref[...])

        @pl.when(pl.program_id(0) == 0)
        def _():
            dg_ref[...] = jnp.zeros_like(dg_ref)

        dg_ref[...] += dg
        dfb = df.astype(BF16)
        df_ref[...] = dfb
        for j in range(N_CHIPS):
            da = _dot_nt(dfb, wd_ref[j])
            gt = gate_ref[j].astype(F32)
            sig = jax.nn.sigmoid(gt)
            silu = gt * sig
            dup_ref[j] = (da * silu).astype(BF16)
            dgate_ref[j] = (da * up_ref[j].astype(F32) * (sig + silu * (1.0 - sig))).astype(BF16)

    return pl.pallas_call(
        body, name="ffn_bwd_down", grid=(S // bm,),
        in_specs=[_rows(bm, D), _rows(bm, D), _whole((1, D)), _whole((N_CHIPS, Fs, D)), _slabs(bm, Fs),
                  _slabs(bm, Fs)],
        out_specs=[_rows(bm, D), _slabs(bm, Fs), _slabs(bm, Fs), _whole((1, D))],
        out_shape=[jax.ShapeDtypeStruct((S, D), BF16), jax.ShapeDtypeStruct((N_CHIPS, S, Fs), BF16),
                   jax.ShapeDtypeStruct((N_CHIPS, S, Fs), BF16), jax.ShapeDtypeStruct((1, D), F32)],
        compiler_params=_params(),
    )(dh, f, g_post, wd, gate, up)


def _ffn_bwd_up(dgate, dup, wg, wu, h_in, g_pre, dh):
    _, S, Fs = dgate.shape
    D = wg.shape[1]
    bm = _pick(S, 256)

    def body(dgate_ref, dup_ref, wg_ref, wu_ref, h_ref, g_ref, dh_ref, dhin_ref, dg_ref):
        dn = _dot_nt(dgate_ref[0], wg_ref[0]) + _dot_nt(dup_ref[0], wu_ref[0])
        for j in range(1, N_CHIPS):
            dn = dn + _dot_nt(dgate_ref[j], wg_ref[j]) + _dot_nt(dup_ref[j], wu_ref[j])
        _, vjp = jax.vjp(_rms, h_ref[...], g_ref[...])
        dhx, dg = vjp(dn)

        @pl.when(pl.program_id(0) == 0)
        def _():
            dg_ref[...] = jnp.zeros_like(dg_ref)

        dg_ref[...] += dg
        dhin_ref[...] = dh_ref[...] + dhx

    return pl.pallas_call(
        body, name="ffn_bwd_up", grid=(S // bm,),
        in_specs=[_slabs(bm, Fs), _slabs(bm, Fs), _whole((N_CHIPS, D, Fs)), _whole((N_CHIPS, D, Fs)), _rows(bm, D),
                  _whole((1, D)), _rows(bm, D)],
        out_specs=[_rows(bm, D), _whole((1, D))],
        out_shape=[jax.ShapeDtypeStruct((S, D), F32), jax.ShapeDtypeStruct((1, D), F32)],
        compiler_params=_params(),
    )(dgate, dup, wg, wu, h_in, g_pre, dh)


def _matmul(a, b, *, ta=False, tb=False, out_dtype=BF16, name, batch=None):
    n_batch = a.shape[0] if batch == "a" else b.shape[0] if batch == "b" else 1
    a_shape = a.shape[1:] if batch == "a" else a.shape
    b_shape = b.shape[1:] if batch == "b" else b.shape
    M, K = (a_shape[1], a_shape[0]) if ta else a_shape
    N = b_shape[0] if tb else b_shape[1]
    acc_budget = 12 * 1024 * 1024
    bm, bk = _pick(M, 1536), _pick(K, 1024)
    while n_batch * N * bm * 4 > acc_budget and bm % (2 * LANE) == 0:
        bm //= 2
    bn = N if n_batch * N * bm * 4 <= acc_budget else _pick(N, 1536)
    nk = K // bk

    def body(a_ref, b_ref, o_ref, acc_ref):
        kk = pl.program_id(2)

        @pl.when(kk == 0)
        def _():
            acc_ref[...] = jnp.zeros_like(acc_ref)

        dims = (((0 if ta else 1,), (1 if tb else 0,)), ((), ()))
        if batch is None:
            acc_ref[...] += lax.dot_general(a_ref[...], b_ref[...], dims, preferred_element_type=F32)
        else:
            for g in range(n_batch):
                av = a_ref[g] if batch == "a" else a_ref[...]
                bv = b_ref[g] if batch == "b" else b_ref[...]
                acc_ref[g] += lax.dot_general(av, bv, dims, preferred_element_type=F32)

        @pl.when(kk == nk - 1)
        def _():
            o_ref[...] = acc_ref[...].astype(o_ref.dtype)

    def spec(block, index, batched):
        if batched:
            return pl.BlockSpec((n_batch,) + block, lambda i, j, k: (0,) + index(i, j, k))
        return pl.BlockSpec(block, index)

    a_spec = spec((bk, bm), lambda i, j, k: (k, i), batch == "a") if ta else \
        spec((bm, bk), lambda i, j, k: (i, k), batch == "a")
    b_spec = spec((bn, bk), lambda i, j, k: (j, k), batch == "b") if tb else \
        spec((bk, bn), lambda i, j, k: (k, j), batch == "b")
    lead = (n_batch,) if batch else ()
    return pl.pallas_call(
        body, name=name, grid=(M // bm, N // bn, nk),
        in_specs=[a_spec, b_spec],
        out_specs=spec((bm, bn), lambda i, j, k: (i, j), batch is not None),
        out_shape=jax.ShapeDtypeStruct(lead + (M, N), out_dtype),
        scratch_shapes=[pltpu.VMEM(lead + (bm, bn), F32)],
        compiler_params=_params(),
    )(a, b)


def _mix_fwd_in(h, g_pre, win, wgate, b_gate, b_forget):
    S, D = h.shape
    PW = win.shape[1] - LANE
    G = wgate.shape[1]
    bm = _pick(S, 256)

    def body(h_ref, g_ref, win_ref, wgate_ref, bg_ref, bf_ref, u_ref, proj_ref, fl_ref, sg_ref):
        u = _rms(h_ref[...], g_ref[...]).astype(BF16)
        u_ref[...] = u
        proj = _dot(u, win_ref[...])
        proj_ref[...] = proj[:, :PW].astype(BF16)
        fl_ref[...] = proj[:, PW:] + bf_ref[...]
        sg_ref[...] = jax.nn.sigmoid(_dot(u, wgate_ref[...]) + bg_ref[...]).astype(BF16)

    return pl.pallas_call(
        body, name="mix_fwd_in", grid=(S // bm,),
        in_specs=[_rows(bm, D), _whole((1, D)), _whole((D, PW + LANE)), _whole((D, G)), _whole((1, G)),
                  _whole((1, LANE))],
        out_specs=[_rows(bm, D), _rows(bm, PW), _rows(bm, LANE), _rows(bm, G)],
        out_shape=[jax.ShapeDtypeStruct((S, D), BF16), jax.ShapeDtypeStruct((S, PW), BF16),
                   jax.ShapeDtypeStruct((S, LANE), F32), jax.ShapeDtypeStruct((S, G), BF16)],
        compiler_params=_params(),
    )(h, g_pre, win, wgate, b_gate, b_forget)


def _mix_fwd_out(o_sb, o_fx, o_mem, sg, w_sb, w_fx, w_mem, w_out, h, g_post):
    S, D = h.shape
    bm = _pick(S, 256)
    widths = (o_sb.shape[1], o_fx.shape[1], o_mem.shape[1])

    def body(osb_ref, ofx_ref, omem_ref, sg_ref, wsb_ref, wfx_ref, wmem_ref, wout_ref, h_ref, g_ref,
             hout_ref, z_ref, merged_ref):
        s = sg_ref[...].astype(F32)
        merged = (s[:, :D] * _dot(osb_ref[...], wsb_ref[...]) + s[:, D:2 * D] * _dot(ofx_ref[...], wfx_ref[...])
                  + s[:, 2 * D:] * _dot(omem_ref[...], wmem_ref[...]))
        mb = merged.astype(BF16)
        merged_ref[...] = mb
        z = _dot(mb, wout_ref[...])
        z_ref[...] = z
        hout_ref[...] = h_ref[...] + _rms(z, g_ref[...])

    return pl.pallas_call(
        body, name="mix_fwd_out", grid=(S // bm,),
        in_specs=[_rows(bm, widths[0]), _rows(bm, widths[1]), _rows(bm, widths[2]), _rows(bm, 3 * D),
                  _whole((widths[0], D)), _whole((widths[1], D)), _whole((widths[2], D)), _whole((D, D)),
                  _rows(bm, D), _whole((1, D))],
        out_specs=[_rows(bm, D), _rows(bm, D), _rows(bm, D)],
        out_shape=[jax.ShapeDtypeStruct((S, D), F32), jax.ShapeDtypeStruct((S, D), F32),
                   jax.ShapeDtypeStruct((S, D), BF16)],
        compiler_params=_params(),
    )(o_sb, o_fx, o_mem, sg, w_sb, w_fx, w_mem, w_out, h, g_post)


def _mix_bwd_out(dh, z, g_post, w_out, o_sb, o_fx, o_mem, w_sb, w_fx, w_mem, sg):
    S, D = dh.shape
    bm = _pick(S, 256)
    widths = (o_sb.shape[1], o_fx.shape[1], o_mem.shape[1])

    def body(dh_ref, z_ref, g_ref, wout_ref, osb_ref, ofx_ref, omem_ref, wsb_ref, wfx_ref, wmem_ref, sg_ref,
             dz_ref, dbsb_ref, dbfx_ref, dbmem_ref, dosb_ref, dofx_ref, domem_ref, dgp_ref, dbg_ref, dg_ref):
        _, vjp = jax.vjp(_rms, z_ref[...], g_ref[...])
        dz, dg = vjp(dh_ref[...])

        @pl.when(pl.program_id(0) == 0)
        def _():
            dg_ref[...] = jnp.zeros_like(dg_ref)
            dbg_ref[...] = jnp.zeros_like(dbg_ref)

        dg_ref[...] += dg
        dzb = dz.astype(BF16)
        dz_ref[...] = dzb
        dmerged = _dot_nt(dzb, wout_ref[...])
        s = sg_ref[...].astype(F32)
        branches = ((osb_ref, wsb_ref, dbsb_ref, dosb_ref), (ofx_ref, wfx_ref, dbfx_ref, dofx_ref),
                    (omem_ref, wmem_ref, dbmem_ref, domem_ref))
        for k, (o_ref, w_ref, db_ref, do_ref) in enumerate(branches):
            gs = s[:, k * D:(k + 1) * D]
            dbb = (dmerged * gs).astype(BF16)
            db_ref[...] = dbb
            do_ref[...] = _dot_nt(dbb, w_ref[...]).astype(BF16)
            dgp = dmerged * _dot(o_ref[...], w_ref[...]) * gs * (1.0 - gs)
            dgp_ref[:, k * D:(k + 1) * D] = dgp.astype(BF16)
            dbg_ref[:, k * D:(k + 1) * D] += jnp.sum(dgp, axis=0, keepdims=True)

    return pl.pallas_call(
        body, name="mix_bwd_out", grid=(S // bm,),
        in_specs=[_rows(bm, D), _rows(bm, D), _whole((1, D)), _whole((D, D)),
                  _rows(bm, widths[0]), _rows(bm, widths[1]), _rows(bm, widths[2]),
                  _whole((widths[0], D)), _whole((widths[1], D)), _whole((widths[2], D)), _rows(bm, 3 * D)],
        out_specs=[_rows(bm, D)] * 4 + [_rows(bm, widths[0]), _rows(bm, widths[1]), _rows(bm, widths[2]),
                                        _rows(bm, 3 * D), _whole((1, 3 * D)), _whole((1, D))],
        out_shape=[jax.ShapeDtypeStruct((S, D), BF16)] * 4
        + [jax.ShapeDtypeStruct((S, w), BF16) for w in widths]
        + [jax.ShapeDtypeStruct((S, 3 * D), BF16), jax.ShapeDtypeStruct((1, 3 * D), F32),
           jax.ShapeDtypeStruct((1, D), F32)],
        compiler_params=_params(),
    )(dh, z, g_post, w_out, o_sb, o_fx, o_mem, w_sb, w_fx, w_mem, sg)


def _mix_bwd_in(dproj, dgp, win, wgate, h_in, g_pre, dh):
    S, PWL = dproj.shape
    G = dgp.shape[1]
    D = h_in.shape[1]
    bm = _pick(S, 256)

    def body(dproj_ref, dgp_ref, win_ref, wgate_ref, h_ref, g_ref, dh_ref, dhin_ref, dg_ref):
        du = _dot_nt(dproj_ref[...], win_ref[...]) + _dot_nt(dgp_ref[...], wgate_ref[...])
        _, vjp = jax.vjp(_rms, h_ref[...], g_ref[...])
        dhx, dg = vjp(du)

        @pl.when(pl.program_id(0) == 0)
        def _():
            dg_ref[...] = jnp.zeros_like(dg_ref)

        dg_ref[...] += dg
        dhin_ref[...] = dh_ref[...] + dhx

    return pl.pallas_call(
        body, name="mix_bwd_in", grid=(S // bm,),
        in_specs=[_rows(bm, PWL), _rows(bm, G), _whole((D, PWL)), _whole((D, G)), _rows(bm, D), _whole((1, D)),
                  _rows(bm, D)],
        out_specs=[_rows(bm, D), _whole((1, D))],
        out_shape=[jax.ShapeDtypeStruct((S, D), F32), jax.ShapeDtypeStruct((1, D), F32)],
        compiler_params=_params(),
    )(dproj, dgp, win, wgate, h_in, g_pre, dh)


def _log_sigmoid(x):
    return jnp.minimum(x, 0.0) - jnp.log(1.0 + jnp.exp(-jnp.abs(x)))


def _fox_cumsum(fl):
    S = fl.shape[0]
    rb = _pick(S, LANE)

    def body(fl_ref, c_ref, carry_ref):
        @pl.when(pl.program_id(0) == 0)
        def _():
            carry_ref[...] = jnp.zeros_like(carry_ref)

        r = lax.broadcasted_iota(jnp.int32, (rb, rb), 0)
        cidx = lax.broadcasted_iota(jnp.int32, (rb, rb), 1)
        tri = (cidx <= r).astype(BF16)
        hi, mid, lo = _split3(_log_sigmoid(fl_ref[...]))
        c = _dot(tri, hi) + _dot(tri, mid) + _dot(tri, lo) + carry_ref[...]
        c_ref[...] = c
        carry_ref[...] = c[rb - 1:rb, :]

    return pl.pallas_call(
        body, name="fox_cumsum", grid=(S // rb,),
        in_specs=[_rows(rb, LANE)], out_specs=_rows(rb, LANE),
        out_shape=jax.ShapeDtypeStruct((S, LANE), F32),
        scratch_shapes=[pltpu.VMEM((1, LANE), F32)],
        compiler_params=_params(),
    )(fl)


def _fox_dlogit(dc, fl):
    S = fl.shape[0]
    rb = _pick(S, LANE)
    nb = S // rb

    def body(dc_ref, fl_ref, dfl_ref, dbf_ref, carry_ref):
        @pl.when(pl.program_id(0) == 0)
        def _():
            carry_ref[...] = jnp.zeros_like(carry_ref)
            dbf_ref[...] = jnp.zeros_like(dbf_ref)

        r = lax.broadcasted_iota(jnp.int32, (rb, rb), 0)
        cidx = lax.broadcasted_iota(jnp.int32, (rb, rb), 1)
        tri = (cidx >= r).astype(BF16)
        hi, mid, lo = _split3(dc_ref[...])
        rc = _dot(tri, hi) + _dot(tri, mid) + _dot(tri, lo) + carry_ref[...]
        carry_ref[...] = rc[0:1, :]
        dfl = rc * jax.nn.sigmoid(-fl_ref[...])
        dfl_ref[...] = dfl.astype(BF16)
        dbf_ref[...] += jnp.sum(dfl, axis=0, keepdims=True)

    rev = pl.BlockSpec((rb, LANE), lambda i: (nb - 1 - i, 0))
    return pl.pallas_call(
        body, name="fox_dlogit", grid=(nb,),
        in_specs=[rev, rev], out_specs=[rev, _whole((1, LANE))],
        out_shape=[jax.ShapeDtypeStruct((S, LANE), BF16), jax.ShapeDtypeStruct((1, LANE), F32)],
        scratch_shapes=[pltpu.VMEM((1, LANE), F32)],
        compiler_params=_params(),
    )(dc, fl)


def _attn_blocks(kind, S, Sk, backward=False):
    tq = _pick(S, 1024 if backward else 2048)
    tc = LANE if kind == "sb" else _pick(Sk, 256)
    return tq, tc


def _is_power_of_two(x):
    return math.frexp(x)[0] == 0.5


def _sb_logs(z):
    ln = -jnp.maximum(z, 0.0) - jnp.log(1.0 + jnp.exp(-jnp.abs(z)))
    return ln + z, ln


def _head_lanes(pack, dh):
    lane = lax.broadcasted_iota(jnp.int32, (1, LANE), 1)
    return [(lane >= hh * dh) & (lane < (hh + 1) * dh) for hh in range(pack)]


def _by_head(sel, parts):
    out = parts[0]
    for hh in range(1, len(parts)):
        out = jnp.where(sel[hh], parts[hh], out)
    return out


def _only_head(sel, hh, x):
    return x if len(sel) == 1 else jnp.where(sel[hh], x, jnp.zeros_like(x))


def _tail(x, r0):
    return x if not r0 else x[r0:]


def _put_tail(x, tail, r0):
    return tail if not r0 else jnp.concatenate([x[:r0], tail], axis=0)


def _add_tail(x, tail, r0):
    return x + tail if not r0 else jnp.concatenate([x[:r0], x[r0:] + tail], axis=0)


def _q_cols(tq, first):
    return pl.BlockSpec((tq, LANE), lambda g, i: (i, first // LANE + g))


def _k_cols(rows, first):
    return pl.BlockSpec((rows, LANE), lambda g, i: (0, first // LANE + g))


def _attn_fwd(kind, q, k, v, n_heads, dh, ccol=None, crow=None):
    (qa, q0), (ka, k0), (va, v0) = q, k, v
    S, Sk = qa.shape[0], ka.shape[0]
    pack = LANE // dh
    tq, tc = _attn_blocks(kind, S, Sk)
    scale = dh ** -0.5
    fold = _is_power_of_two(scale)
    causal = kind != "mem"
    n_diag = tq // tc if causal else 0
    unroll = 2 if causal else 1
    assert n_diag % unroll == 0 and (Sk // tc) % unroll == 0

    def body(*refs):
        if kind == "fox":
            q_ref, k_ref, v_ref, cc_ref, cr_ref, o_ref, lse_ref = refs
        else:
            q_ref, k_ref, v_ref, o_ref, lse_ref = refs
        i = pl.program_id(1)
        n_full = (i * tq) // tc if causal else Sk // tc
        qpos = i * tq + lax.broadcasted_iota(jnp.int32, (tq, tc), 0)
        kio = lax.broadcasted_iota(jnp.int32, (tq, tc), 1)
        heads = range(pack)
        sel = _head_lanes(pack, dh)
        q2 = q_ref[...] * scale if fold else q_ref[...]
        qs = [_only_head(sel, hh, q2) for hh in heads]

        def kv(jc):
            off = pl.multiple_of(jc * tc, tc)
            return off, k_ref[pl.ds(off, tc), :], v_ref[pl.ds(off, tc), :]

        if kind == "sb":
            tri = (lax.broadcasted_iota(jnp.int32, (tc, tc), 0) > lax.broadcasted_iota(jnp.int32, (tc, tc), 1)
                   ).astype(BF16)

            def chunk(jc, r0, runs, acc):
                off, k2, v2 = kv(jc)
                new_runs, pv = [], []
                for hh in heads:
                    lb, ln = _sb_logs(_dot_nt(_tail(qs[hh], r0), k2))
                    if r0 is not None:
                        mask = (off + _tail(kio, r0)) < _tail(qpos, r0)
                        ln = jnp.where(mask, ln, 0.0)
                    w = jnp.exp(lb + _cumdot(ln, tri) + _tail(runs[hh], r0))
                    if r0 is not None:
                        w = jnp.where(mask, w, 0.0)
                    pv.append(_dot(w.astype(BF16), v2))
                    new_runs.append(_add_tail(runs[hh], jnp.sum(ln, axis=1, keepdims=True), r0))
                return tuple(new_runs), _add_tail(acc, _by_head(sel, pv), r0)

            state = (tuple(jnp.zeros((tq, 1), F32) for _ in heads), jnp.zeros((tq, LANE), F32))
            for d in range(n_diag - 1, -1, -1):
                state = chunk(n_full + d, d * tc, *state)

            def trip(t, st):
                for u in range(unroll):
                    st = chunk(n_full - 1 - unroll * t - u, None, *st)
                return st

            runs, acc = lax.fori_loop(0, n_full // unroll, trip, state)
            o_ref[...] = acc.astype(o_ref.dtype)
            for hh in heads:
                lse_ref[hh] = runs[hh]
        else:
            def chunk(jc, r0, ms, ls, acc):
                off, k2, v2 = kv(jc)
                new_ms, new_ls, alphas, pv = [], [], [], []
                for hh in heads:
                    z = _dot_nt(_tail(qs[hh], r0), k2)
                    if not fold:
                        z = z * scale
                    if kind == "fox":
                        z = z + _tail(cc_ref[hh], r0) - cr_ref[hh, pl.ds(jc, 1), :]
                    if r0 is not None:
                        z = jnp.where((off + _tail(kio, r0)) <= _tail(qpos, r0), z, NEG)
                    m_old, l_old = _tail(ms[hh], r0), _tail(ls[hh], r0)
                    m_new = jnp.maximum(m_old, jnp.max(z, axis=1, keepdims=True))
                    alpha = jnp.exp(m_old - m_new)
                    p = jnp.exp(z - m_new)
                    new_ms.append(_put_tail(ms[hh], m_new, r0))
                    new_ls.append(_put_tail(ls[hh], alpha * l_old + jnp.sum(p, axis=1, keepdims=True), r0))
                    alphas.append(alpha)
                    pv.append(_dot(p.astype(BF16), v2))
                acc_new = _by_head(sel, alphas) * _tail(acc, r0) + _by_head(sel, pv)
                return tuple(new_ms), tuple(new_ls), _put_tail(acc, acc_new, r0)

            state = (tuple(jnp.full((tq, 1), NEG, F32) for _ in heads), tuple(jnp.zeros((tq, 1), F32) for _ in heads),
                     jnp.zeros((tq, LANE), F32))

            def trip(t, st):
                for u in range(unroll):
                    st = chunk(unroll * t + u, None, *st)
                return st

            state = lax.fori_loop(0, n_full // unroll, trip, state)
            for d in range(n_diag):
                state = chunk(n_full + d, d * tc, *state)
            ms, ls, acc = state
            o_ref[...] = (acc / _by_head(sel, ls)).astype(o_ref.dtype)
            for hh in heads:
                lse_ref[hh] = ms[hh] + jnp.log(ls[hh])

    colspec = pl.BlockSpec((pack, tq, 1), lambda g, i: (g, i, 0))
    in_specs, args = [_q_cols(tq, q0), _k_cols(Sk, k0), _k_cols(Sk, v0)], [qa, ka, va]
    if kind == "fox":
        in_specs += [colspec, pl.BlockSpec((pack, Sk // tc, tc), lambda g, i: (g, 0, 0))]
        args += [ccol, crow]
    return pl.pallas_call(
        body, name="attn_fwd_" + kind, grid=(n_heads // pack, S // tq),
        in_specs=in_specs, out_specs=[_q_cols(tq, 0), colspec],
        out_shape=[jax.ShapeDtypeStruct((S, n_heads * dh), BF16), jax.ShapeDtypeStruct((n_heads, S, 1), F32)],
        compiler_params=_params(),
    )(*args)


def _attn_bwd(kind, q, k, v, o, do, n_heads, dh, ccol=None, crow=None, lse=None):
    (qa, q0), (ka, k0), (va, v0) = q, k, v
    S, Sk = qa.shape[0], ka.shape[0]
    pack = LANE // dh
    tq, tc = _attn_blocks(kind, S, Sk, backward=True)
    scale = dh ** -0.5
    fold = _is_power_of_two(scale)
    causal = kind != "mem"
    n_diag = tq // tc if causal else 0
    unroll = 4 if kind == "sb" else 1
    assert n_diag % unroll == 0 and (Sk // tc) % unroll == 0
    nq = S // tq

    def body(*refs):
        if kind == "fox":
            (q_ref, k_ref, v_ref, o_ref, do_ref, cc_ref, cr_ref, lse_ref,
             dq_ref, dk_ref, dv_ref, dc_ref, dcc_ref, dk_acc, dv_acc, dc_acc) = refs
        else:
            q_ref, k_ref, v_ref, o_ref, do_ref, lse_ref, dq_ref, dk_ref, dv_ref, dk_acc, dv_acc = refs
        i = pl.program_id(1)

        @pl.when(i == 0)
        def _():
            dk_acc[...] = jnp.zeros_like(dk_acc)
            dv_acc[...] = jnp.zeros_like(dv_acc)
            if kind == "fox":
                dc_acc[...] = jnp.zeros_like(dc_acc)

        n_full = (i * tq) // tc if causal else Sk // tc
        qpos = i * tq + lax.broadcasted_iota(jnp.int32, (tq, tc), 0)
        kio = lax.broadcasted_iota(jnp.int32, (tq, tc), 1)
        heads = range(pack)
        sel = _head_lanes(pack, dh)
        q2 = q_ref[...] * scale if fold else q_ref[...]
        do2 = do_ref[...]
        qs = [_only_head(sel, hh, q2) for hh in heads]
        dos = [_only_head(sel, hh, do2) for hh in heads]

        def kv(jc):
            off = pl.multiple_of(jc * tc, tc)
            return off, k_ref[pl.ds(off, tc), :], v_ref[pl.ds(off, tc), :]

        def accumulate(off, k2, dzb, wb, dq, r0):
            q2t, do2t = _tail(q2, r0), _tail(do2, r0)
            dk_acc[pl.ds(off, tc), :] += _by_head(sel, [_dot_tn(dzb[hh], q2t) for hh in heads])
            dv_acc[pl.ds(off, tc), :] += _by_head(sel, [_dot_tn(wb[hh], do2t) for hh in heads])
            return _add_tail(dq, _by_head(sel, [_dot(dzb[hh], k2) for hh in heads]), r0)

        if kind == "sb":
            r = lax.broadcasted_iota(jnp.int32, (tc, tc), 0)
            cidx = lax.broadcasted_iota(jnp.int32, (tc, tc), 1)
            tri_inc = (r <= cidx).astype(BF16)
            tri_exc = (r < cidx).astype(BF16)

            def chunk(jc, r0, pres, pres_e, dq):
                off, k2, v2 = kv(jc)
                new_pres, new_pres_e, dzb, wb = [], [], [], []
                for hh in heads:
                    lb, ln = _sb_logs(_dot_nt(_tail(qs[hh], r0), k2))
                    if r0 is not None:
                        mask = (off + _tail(kio, r0)) < _tail(qpos, r0)
                        ln = jnp.where(mask, ln, 0.0)
                    w = jnp.exp(lb + (_tail(lse_ref[hh], r0) - _tail(pres[hh], r0) - _cumdot(ln, tri_inc)))
                    if r0 is not None:
                        w = jnp.where(mask, w, 0.0)
                    e = w * _dot_nt(_tail(dos[hh], r0), v2)
                    beta = jnp.exp(lb)
                    dz = e * (1.0 - beta) - beta * (_tail(pres_e[hh], r0) + _dot(e.astype(BF16), tri_exc))
                    if r0 is not None:
                        dz = jnp.where(mask, dz, 0.0)
                    dzb.append(dz.astype(BF16))
                    wb.append(w.astype(BF16))
                    new_pres.append(_add_tail(pres[hh], jnp.sum(ln, axis=1, keepdims=True), r0))
                    new_pres_e.append(_add_tail(pres_e[hh], jnp.sum(e, axis=1, keepdims=True), r0))
                return tuple(new_pres), tuple(new_pres_e), accumulate(off, k2, dzb, wb, dq, r0)

            state = (tuple(jnp.zeros((tq, 1), F32) for _ in heads), tuple(jnp.zeros((tq, 1), F32) for _ in heads),
                     jnp.zeros((tq, LANE), F32))
        else:
            prod = o_ref[...].astype(F32) * do2.astype(F32)
            dsum = [jnp.sum(_only_head(sel, hh, prod), axis=1, keepdims=True) for hh in heads]

            def chunk(jc, r0, rowsums, dq):
                off, k2, v2 = kv(jc)
                new_rowsums, dsb, pb = [], [], []
                for hh in heads:
                    z = _dot_nt(_tail(qs[hh], r0), k2)
                    if not fold:
                        z = z * scale
                    if kind == "fox":
                        z = z + _tail(cc_ref[hh], r0) - cr_ref[hh, pl.ds(jc, 1), :]
                    if r0 is not None:
                        z = jnp.where((off + _tail(kio, r0)) <= _tail(qpos, r0), z, NEG)
                    p = jnp.exp(z - _tail(lse_ref[hh], r0))
                    ds = p * (_dot_nt(_tail(dos[hh], r0), v2) - _tail(dsum[hh], r0))
                    dsb.append(ds.astype(BF16))
                    pb.append(p.astype(BF16))
                    if kind == "fox":
                        dc_acc[hh, pl.ds(jc, 1), :] -= jnp.sum(ds, axis=0, keepdims=True)
                        new_rowsums.append(_add_tail(rowsums[hh], jnp.sum(ds, axis=1, keepdims=True), r0))
                    else:
                        new_rowsums.append(rowsums[hh])
                return tuple(new_rowsums), accumulate(off, k2, dsb, pb, dq, r0)

            state = (tuple(jnp.zeros((tq, 1), F32) for _ in heads), jnp.zeros((tq, LANE), F32))

        def trip(t, st):
            for u in range(unroll):
                st = chunk(unroll * t + u, None, *st)
            return st

        state = lax.fori_loop(0, n_full // unroll, trip, state)
        for d in range(n_diag):
            state = chunk(n_full + d, d * tc, *state)
        dq_ref[...] = (state[-1] * scale).astype(dq_ref.dtype)
        if kind == "fox":
            for hh in heads:
                dcc_ref[hh] = state[0][hh]

        @pl.when(i == nq - 1)
        def _():
            dk = dk_acc[...] if fold else dk_acc[...] * scale
            dk_ref[...] = dk.astype(dk_ref.dtype)
            dv_ref[...] = dv_acc[...].astype(dv_ref.dtype)
            if kind == "fox":
                dc_ref[...] = dc_acc[...]

    colspec = pl.BlockSpec((pack, tq, 1), lambda g, i: (g, i, 0))
    rowspec = pl.BlockSpec((pack, Sk // tc, tc), lambda g, i: (g, 0, 0))
    in_specs = [_q_cols(tq, q0), _k_cols(Sk, k0), _k_cols(Sk, v0), _q_cols(tq, 0), _q_cols(tq, 0)]
    args = [qa, ka, va, o, do]
    if kind == "fox":
        in_specs += [colspec, rowspec]
        args += [ccol, crow]
    in_specs += [colspec]
    args += [lse]
    width = n_heads * dh
    out_specs = [_q_cols(tq, 0), _k_cols(Sk, 0), _k_cols(Sk, 0)]
    out_shape = [jax.ShapeDtypeStruct((S, width), BF16), jax.ShapeDtypeStruct((Sk, width), BF16),
                 jax.ShapeDtypeStruct((Sk, width), BF16)]
    scratch = [pltpu.VMEM((Sk, LANE), F32), pltpu.VMEM((Sk, LANE), F32)]
    if kind == "fox":
        out_specs += [rowspec, colspec]
        out_shape += [jax.ShapeDtypeStruct((n_heads, Sk // tc, tc), F32), jax.ShapeDtypeStruct((n_heads, S, 1), F32)]
        scratch.append(pltpu.VMEM((pack, Sk // tc, tc), F32))
    return pl.pallas_call(
        body, name="attn_bwd_" + kind, grid=(n_heads // pack, nq),
        in_specs=in_specs, out_specs=out_specs, out_shape=out_shape, scratch_shapes=scratch,
        compiler_params=_params(),
    )(*args)


def _mem_norm(mem, g):
    M, D = mem.shape

    def body(mem_ref, g_ref, out_ref):
        out_ref[...] = _rms(mem_ref[...], g_ref[...]).astype(BF16)

    return pl.pallas_call(
        body, name="mem_norm", grid=(1,),
        in_specs=[_whole((M, D)), _whole((1, D))], out_specs=_whole((M, D)),
        out_shape=jax.ShapeDtypeStruct((M, D), BF16), compiler_params=_params(),
    )(mem, g)


def _mem_norm_bwd(mem, g, dmem_n):
    M, D = mem.shape
    L = dmem_n.shape[0]

    def body(mem_ref, g_ref, d_ref, dg_ref):
        d = d_ref[0]
        for l in range(1, L):
            d = d + d_ref[l]
        _, vjp = jax.vjp(_rms, mem_ref[...], g_ref[...])
        dg_ref[...] = vjp(d)[1]

    return pl.pallas_call(
        body, name="mem_norm_bwd", grid=(1,),
        in_specs=[_whole((M, D)), _whole((1, D)), _whole((L, M, D))], out_specs=_whole((1, D)),
        out_shape=jax.ShapeDtypeStruct((1, D), F32), compiler_params=_params(),
    )(mem, g, dmem_n)


def _loss_head(h, target):
    S, D = h.shape
    bm = _pick(S, 512)

    def body(h_ref, t_ref, dh_ref, loss_ref):
        err = h_ref[...] - t_ref[...]
        dh_ref[...] = err * (1.0 / D)

        @pl.when(pl.program_id(0) == 0)
        def _():
            loss_ref[...] = jnp.zeros_like(loss_ref)

        loss_ref[...] += 0.5 * jnp.sum(jnp.mean(err * err, axis=-1, keepdims=True), axis=0, keepdims=True)

    return pl.pallas_call(
        body, name="loss_head", grid=(S // bm,),
        in_specs=[_rows(bm, D), _rows(bm, D)], out_specs=[_rows(bm, D), _whole((8, LANE))],
        out_shape=[jax.ShapeDtypeStruct((S, D), F32), jax.ShapeDtypeStruct((8, LANE), F32)],
        compiler_params=_params(),
    )(h, target)


def _adamw(w, g, m, v, name):
    R, C = w.shape
    rb = R if R * C * 4 <= (1 << 20) else _pick(R, 256)
    if R % rb:
        rb = R
    c1 = 1.0 - ADAM_B1 ** ADAM_STEP
    c2 = 1.0 - ADAM_B2 ** ADAM_STEP

    def body(w_ref, g_ref, m_ref, v_ref, d_ref, mo_ref, vo_ref):
        gv = g_ref[...]
        mn = ADAM_B1 * m_ref[...] + (1.0 - ADAM_B1) * gv
        vn = ADAM_B2 * v_ref[...] + (1.0 - ADAM_B2) * (gv * gv)
        mo_ref[...] = mn
        vo_ref[...] = vn
        d_ref[...] = -ADAM_LR * ((mn / c1) / (jnp.sqrt(vn / c2) + ADAM_EPS) + ADAM_WD * w_ref[...])

    return pl.pallas_call(
        body, name=name, grid=(R // rb,),
        in_specs=[_rows(rb, C)] * 4, out_specs=[_rows(rb, C)] * 3,
        out_shape=[jax.ShapeDtypeStruct((R, C), F32)] * 3, compiler_params=_params(),
    )(w, g, m, v)


def _adamw_reduced(w, m, v, mine, theirs, c_idx, first_row, name):
    L, a, b = w.shape
    Lh = L // 2
    rb = _shard_row_block(a)
    nb = a // rb
    assert first_row % rb == 0
    c1 = 1.0 - ADAM_B1 ** ADAM_STEP
    c2 = 1.0 - ADAM_B2 ** ADAM_STEP

    def own(i, c_ref):
        return (i, 0)

    def reduced(i, c_ref):
        return (first_row // rb + ((i // nb) % Lh) * nb + i % nb, 0)

    def body(c_ref, w_ref, m_ref, v_ref, mine_ref, theirs_ref, g_ref, d_ref, mo_ref, vo_ref):
        half = (pl.program_id(0) // nb) // Lh
        gv = jnp.where(c_ref[0] == half, mine_ref[...], theirs_ref[...])
        g_ref[...] = gv
        mn = ADAM_B1 * m_ref[...] + (1.0 - ADAM_B1) * gv
        vn = ADAM_B2 * v_ref[...] + (1.0 - ADAM_B2) * (gv * gv)
        mo_ref[...] = mn
        vo_ref[...] = vn
        d_ref[...] = -ADAM_LR * ((mn / c1) / (jnp.sqrt(vn / c2) + ADAM_EPS) + ADAM_WD * w_ref[...])

    outs = pl.pallas_call(
        body, name=name,
        grid_spec=pltpu.PrefetchScalarGridSpec(
            num_scalar_prefetch=1, grid=(L * nb,),
            in_specs=[pl.BlockSpec((rb, b), own)] * 3 + [pl.BlockSpec((rb, b), reduced)] * 2,
            out_specs=[pl.BlockSpec((rb, b), own)] * 4),
        out_shape=[jax.ShapeDtypeStruct((L * a, b), F32)] * 4, compiler_params=_params(),
    )(c_idx, w.reshape(L * a, b), m.reshape(L * a, b), v.reshape(L * a, b), mine, theirs)
    return [t.reshape(L, a, b) for t in outs]


ANY = pl.BlockSpec(memory_space=pl.ANY)
MESH = pl.DeviceIdType.MESH


def _place():
    x, y, c = lax.axis_index("x"), lax.axis_index("y"), lax.axis_index("c")
    others = [(1 - x, y), (x, 1 - y), (1 - x, 1 - y)]
    return x, y, c, others


def _place_own(loc, chip_idx):
    _, R, C = loc.shape
    rb = _pick(R, 2 * FLAT_ROW_BLOCK)

    def body(chip_ref, loc_ref, out_ref):
        out_ref[...] = loc_ref[...]

    return pl.pallas_call(
        body, name="place_own",
        grid_spec=pltpu.PrefetchScalarGridSpec(
            num_scalar_prefetch=1, grid=(2, R // rb),
            in_specs=[pl.BlockSpec((None, rb, C), lambda hf, i, chip_ref: (hf, i, 0))],
            out_specs=pl.BlockSpec((None, None, rb, C), lambda hf, i, chip_ref: (chip_ref[0], hf, i, 0))),
        out_shape=jax.ShapeDtypeStruct((N_CHIPS, 2, R, C), loc.dtype), compiler_params=_params(),
    )(chip_idx, loc)


def _gather_weights(locs, owns):
    n = len(locs)

    def body(*refs):
        loc_refs, out_refs, (send_sems, recv_sems) = refs[:n], refs[2 * n:3 * n], refs[3 * n:]
        x, y, c, others = _place()
        me = 2 * x + y
        sibling = (x, y, 1 - c)

        def copy(a, k, src, dst, to):
            return pltpu.make_async_remote_copy(src_ref=src, dst_ref=dst, send_sem=send_sems.at[a, k],
                                                recv_sem=recv_sems.at[a, k], device_id=to, device_id_type=MESH)

        first = [copy(a, j, loc_refs[a].at[c], out_refs[a].at[me, c], (ox, oy, c))
                 for j, (ox, oy) in enumerate(others) for a in range(n)]
        for cp in first:
            cp.start()
        passed = []
        for j, (ox, oy) in enumerate(others):
            for a in range(n):
                landed = out_refs[a].at[2 * ox + oy, c]
                copy(a, j, loc_refs[a].at[c], landed, sibling).wait_recv()
                cp = copy(a, 3 + j, landed, landed, sibling)
                cp.start()
                passed.append(cp)
        for j, (ox, oy) in enumerate(others):
            for a in range(n):
                copy(a, 3 + j, loc_refs[a].at[c], out_refs[a].at[2 * ox + oy, 1 - c], sibling).wait_recv()
        for cp in first + passed:
            cp.wait_send()

    return pl.pallas_call(
        body, name="gather_weights", in_specs=[ANY] * (2 * n), out_specs=[ANY] * n,
        out_shape=[jax.ShapeDtypeStruct(own.shape, own.dtype) for own in owns],
        input_output_aliases={n + a: a for a in range(n)},
        scratch_shapes=[pltpu.SemaphoreType.DMA((n, 6)), pltpu.SemaphoreType.DMA((n, 6))],
    )(*locs, *owns)


def _pair_exchange(gs):
    n = len(gs)

    def body(*refs):
        g_refs, out_refs, (send_sems, recv_sems) = refs[:n], refs[n:2 * n], refs[2 * n:]
        x, y, c, _ = _place()
        copies = [pltpu.make_async_remote_copy(src_ref=g_refs[a].at[1 - c], dst_ref=out_refs[a],
                                               send_sem=send_sems.at[a], recv_sem=recv_sems.at[a],
                                               device_id=(x, y, 1 - c), device_id_type=MESH) for a in range(n)]
        for cp in copies:
            cp.start()
        for cp in copies:
            cp.wait()

    return pl.pallas_call(
        body, name="pair_exchange", in_specs=[ANY] * n, out_specs=[ANY] * n,
        out_shape=[jax.ShapeDtypeStruct(g.shape[1:], g.dtype) for g in gs],
        scratch_shapes=[pltpu.SemaphoreType.DMA((n,)), pltpu.SemaphoreType.DMA((n,))],
    )(*gs)


def _pair_sum(g, sib, c_idx):
    _, _, R, C = g.shape
    rb = _pick(R, 512)

    def body(c_ref, g_ref, s_ref, o_ref):
        o_ref[...] = (g_ref[...].astype(F32) + s_ref[...].astype(F32)).astype(o_ref.dtype)

    return pl.pallas_call(
        body, name="pair_sum",
        grid_spec=pltpu.PrefetchScalarGridSpec(
            num_scalar_prefetch=1, grid=(N_CHIPS, R // rb),
            in_specs=[pl.BlockSpec((None, None, rb, C), lambda j, i, c_ref: (c_ref[0], j, i, 0)),
                      pl.BlockSpec((None, rb, C), lambda j, i, c_ref: (j, i, 0))],
            out_specs=pl.BlockSpec((None, rb, C), lambda j, i, c_ref: (j, i, 0))),
        out_shape=jax.ShapeDtypeStruct((N_CHIPS, R, C), g.dtype), compiler_params=_params(),
    )(c_idx, g, sib)


def _chip_exchange(ps):
    n = len(ps)

    def body(*refs):
        p_refs, out_refs, (send_sems, recv_sems) = refs[:n], refs[n:2 * n], refs[2 * n:]
        x, y, c, others = _place()
        copies = []
        for j, (ox, oy) in enumerate(others):
            for a in range(n):
                cp = pltpu.make_async_remote_copy(src_ref=p_refs[a].at[2 * ox + oy], dst_ref=out_refs[a].at[j],
                                                  send_sem=send_sems.at[a, j], recv_sem=recv_sems.at[a, j],
                                                  device_id=(ox, oy, c), device_id_type=MESH)
                cp.start()
                copies.append(cp)
        for cp in copies:
            cp.wait()

    return pl.pallas_call(
        body, name="chip_exchange", in_specs=[ANY] * n, out_specs=[ANY] * n,
        out_shape=[jax.ShapeDtypeStruct((N_CHIPS - 1,) + p.shape[1:], p.dtype) for p in ps],
        scratch_shapes=[pltpu.SemaphoreType.DMA((n, 3)), pltpu.SemaphoreType.DMA((n, 3))],
    )(*ps)


def _chip_sum(p, r, chip_idx):
    _, R, C = r.shape
    rb = _pick(R, 512)

    def body(chip_ref, p_ref, r_ref, o_ref):
        acc = p_ref[...].astype(F32)
        for j in range(N_CHIPS - 1):
            acc = acc + r_ref[j].astype(F32)
        o_ref[...] = acc

    return pl.pallas_call(
        body, name="chip_sum",
        grid_spec=pltpu.PrefetchScalarGridSpec(
            num_scalar_prefetch=1, grid=(R // rb,),
            in_specs=[pl.BlockSpec((None, rb, C), lambda i, chip_ref: (chip_ref[0], i, 0)),
                      pl.BlockSpec((N_CHIPS - 1, rb, C), lambda i, chip_ref: (0, i, 0))],
            out_specs=pl.BlockSpec((rb, C), lambda i, chip_ref: (i, 0))),
        out_shape=jax.ShapeDtypeStruct((R, C), F32), compiler_params=_params(),
    )(chip_idx, p, r)


def _pair_swap(rhs):
    n = len(rhs)

    def body(*refs):
        rh_refs, out_refs, (send_sems, recv_sems) = refs[:n], refs[n:2 * n], refs[2 * n:]
        x, y, c, _ = _place()
        copies = [pltpu.make_async_remote_copy(src_ref=rh_refs[a], dst_ref=out_refs[a], send_sem=send_sems.at[a],
                                               recv_sem=recv_sems.at[a], device_id=(x, y, 1 - c),
                                               device_id_type=MESH) for a in range(n)]
        for cp in copies:
            cp.start()
        for cp in copies:
            cp.wait()

    return pl.pallas_call(
        body, name="pair_swap", in_specs=[ANY] * n, out_specs=[ANY] * n,
        out_shape=[jax.ShapeDtypeStruct(rh.shape, rh.dtype) for rh in rhs],
        scratch_shapes=[pltpu.SemaphoreType.DMA((n,)), pltpu.SemaphoreType.DMA((n,))],
    )(*rhs)


def _all_reduce_small(s):
    R, C = s.shape

    def body(s_ref, o_ref, buf, send_sems, recv_sems):
        x, y, c, _ = _place()
        me = 4 * x + 2 * y + c
        sends = []
        for k in range(1, N_DEV):
            fx, fy, fc = (k >> 2) & 1, (k >> 1) & 1, k & 1
            to = (x ^ fx, y ^ fy, c ^ fc)
            cp = pltpu.make_async_remote_copy(src_ref=s_ref, dst_ref=buf.at[me], send_sem=send_sems.at[k - 1],
                                              recv_sem=recv_sems.at[k - 1], device_id=to, device_id_type=MESH)
            cp.start()
            sends.append(cp)
        buf[me] = s_ref[...]
        for k in range(1, N_DEV):
            fx, fy, fc = (k >> 2) & 1, (k >> 1) & 1, k & 1
            frm = 4 * (x ^ fx) + 2 * (y ^ fy) + (c ^ fc)
            pltpu.make_async_remote_copy(src_ref=s_ref, dst_ref=buf.at[frm], send_sem=send_sems.at[k - 1],
                                         recv_sem=recv_sems.at[k - 1], device_id=(x, y, c),
                                         device_id_type=MESH).wait_recv()
        acc = buf[0]
        for d in range(1, N_DEV):
            acc = acc + buf[d]
        o_ref[...] = acc
        for cp in sends:
            cp.wait_send()

    vm = pl.BlockSpec(memory_space=pltpu.VMEM)
    return pl.pallas_call(
        body, name="all_reduce_small", in_specs=[vm], out_specs=vm,
        out_shape=jax.ShapeDtypeStruct((R, C), F32),
        scratch_shapes=[pltpu.VMEM((N_DEV, R, C), F32), pltpu.SemaphoreType.DMA((N_DEV - 1,)),
                        pltpu.SemaphoreType.DMA((N_DEV - 1,))],
    )(s)


def _padded(n):
    return -(-n // FLAT_UNIT) * FLAT_UNIT


def _pack_flat(pieces, dtype, row_block=FLAT_ROW_BLOCK):
    flat = []
    for p in pieces:
        p = p.reshape(-1).astype(dtype)
        flat.append(jnp.pad(p, (0, _padded(p.size) - p.size)))
    total = sum(p.size for p in flat)
    flat.append(jnp.zeros((-total) % (row_block * FLAT_COLS), dtype))
    return jnp.concatenate(flat).reshape(-1, FLAT_COLS)


def _unpack_flat(flat, shapes):
    lead = flat.shape[:-2]
    flat = flat.reshape(lead + (-1,))
    out, off = [], 0
    for shp in shapes:
        n = math.prod(shp)
        out.append(flat[..., off:off + n].reshape(lead + tuple(shp)))
        off += _padded(n)
    return out


def _shard_row_block(a):
    for rb in range(min(a, 512) // 16 * 16, 0, -16):
        if a % rb == 0:
            return rb
    return a


def _row_layout(shapes, n_layers):
    groups = {}
    for name, (a, b) in shapes.items():
        names, first, rows = groups.get(b, ((), {}, 0))
        rb = _shard_row_block(a)
        start = -(-rows // rb) * rb
        groups[b] = (names + (name,), {**first, name: start}, start + n_layers * a)
    return {b: (names, first, -(-rows // FLAT_ROW_BLOCK) * FLAT_ROW_BLOCK) for b, (names, first, rows) in groups.items()}


def _pack_rows(group, width, pieces, dtype):
    names, first, rows = group
    parts, at = [], 0
    for name in names:
        if first[name] > at:
            parts.append(jnp.zeros((first[name] - at, width), dtype))
        parts.append(pieces[name].astype(dtype))
        at = first[name] + pieces[name].shape[0]
    if rows > at:
        parts.append(jnp.zeros((rows - at, width), dtype))
    return jnp.concatenate(parts, axis=0)


def _slab(t, axis, j):
    if t.ndim == 3:
        return t[j]
    n = t.shape[axis - 1] // N_CHIPS
    return lax.slice_in_dim(t, j * n, (j + 1) * n, axis=axis - 1)


def _layer_fwd(h0, mem_n, wl, dims):
    n_sb, n_fx, n_mem, sbw, fxw, memw = dims
    n1, gate1, up1, a1 = _ffn_fwd_up(h0, wl["ffn1_pre_g"], wl["ffn1_w_gate"], wl["ffn1_w_up"])
    h1, f1 = _ffn_fwd_down(a1, wl["ffn1_w_down"], h0, wl["ffn1_post_g"])

    u, proj, fl, sg = _mix_fwd_in(h1, wl["mix_pre_g"], wl["w_in"], wl["w_gate"], wl["b_gate"], wl["b_forget"])
    c = _fox_cumsum(fl)
    S = h0.shape[0]
    tc = _attn_blocks("fox", S, S)[1]
    ct = c[:, :n_fx].T
    ccol, crow = ct.reshape(n_fx, S, 1), ct.reshape(n_fx, S // tc, tc)
    qkv_sb = [(proj, k * sbw) for k in range(3)]
    qkv_fx = [(proj, 3 * sbw + k * fxw) for k in range(3)]
    kv = _matmul(mem_n, wl["w_mem_kv"], out_dtype=BF16, name="mem_kv")
    qkv_mem = [(proj, 3 * sbw + 3 * fxw), (kv, 0), (kv, memw)]
    o_sb, tot_sb = _attn_fwd("sb", *qkv_sb, n_sb, HEAD_DIM)
    o_fx, lse_fx = _attn_fwd("fox", *qkv_fx, n_fx, HEAD_DIM, ccol, crow)
    o_mem, lse_mem = _attn_fwd("mem", *qkv_mem, n_mem, MEM_HEAD_DIM)
    h2, zmix, merged = _mix_fwd_out(o_sb, o_fx, o_mem, sg, wl["w_br_sb"], wl["w_br_fox"], wl["w_br_mem"],
                                    wl["w_out"], h1, wl["mix_post_g"])

    n2, gate2, up2, a2 = _ffn_fwd_up(h2, wl["ffn2_pre_g"], wl["ffn2_w_gate"], wl["ffn2_w_up"])
    h3, f2 = _ffn_fwd_down(a2, wl["ffn2_w_down"], h2, wl["ffn2_post_g"])
    saved = dict(h0=h0, n1=n1, gate1=gate1, up1=up1, a1=a1, f1=f1, h1=h1, u=u, fl=fl, sg=sg,
                 qkv_sb=qkv_sb, qkv_fx=qkv_fx, qkv_mem=qkv_mem, ccol=ccol, crow=crow, o_sb=o_sb, o_fx=o_fx, o_mem=o_mem,
                 tot_sb=tot_sb, lse_fx=lse_fx, lse_mem=lse_mem,
                 zmix=zmix, merged=merged, h2=h2, n2=n2, gate2=gate2, up2=up2, a2=a2, f2=f2)
    return h3, saved


def _ffn_bwd(dh, sv, wl, tag, h_in):
    n, gate, up, a, f = (sv[k + tag] for k in ("n", "gate", "up", "a", "f"))
    pre = "ffn" + tag
    df, dgate, dup, dg_post = _ffn_bwd_down(dh, f, wl[pre + "_post_g"], wl[pre + "_w_down"], gate, up)
    dh_in, dg_pre = _ffn_bwd_up(dgate, dup, wl[pre + "_w_gate"], wl[pre + "_w_up"], h_in, wl[pre + "_pre_g"], dh)
    grads = {pre + "_post_g": dg_post, pre + "_pre_g": dg_pre,
             pre + "_w_down": _matmul(a, df, ta=True, batch="a", name="dw_down"),
             pre + "_w_gate": _matmul(n, dgate, ta=True, batch="b", name="dw_gate"),
             pre + "_w_up": _matmul(n, dup, ta=True, batch="b", name="dw_up")}
    return dh_in, grads


def _layer_bwd(dh3, mem_n, wl, sv, dims):
    n_sb, n_fx, n_mem, sbw, fxw, memw = dims
    S = dh3.shape[0]
    dh2, grads = _ffn_bwd(dh3, sv, wl, "2", sv["h2"])

    (dz, db_sb, db_fx, db_mem, do_sb, do_fx, do_mem, dgp, db_gate, dg_post) = _mix_bwd_out(
        dh2, sv["zmix"], wl["mix_post_g"], wl["w_out"], sv["o_sb"], sv["o_fx"], sv["o_mem"],
        wl["w_br_sb"], wl["w_br_fox"], wl["w_br_mem"], sv["sg"])
    grads["mix_post_g"] = dg_post
    grads["b_gate"] = db_gate
    grads["w_out"] = _matmul(sv["merged"], dz, ta=True, name="dw_out")
    grads["w_br_sb"] = _matmul(sv["o_sb"], db_sb, ta=True, name="dw_br_sb")
    grads["w_br_fox"] = _matmul(sv["o_fx"], db_fx, ta=True, name="dw_br_fox")
    grads["w_br_mem"] = _matmul(sv["o_mem"], db_mem, ta=True, name="dw_br_mem")

    dq_sb, dk_sb, dv_sb = _attn_bwd("sb", *sv["qkv_sb"], sv["o_sb"], do_sb, n_sb, HEAD_DIM, lse=sv["tot_sb"])
    dq_fx, dk_fx, dv_fx, dcrow, dccol = _attn_bwd("fox", *sv["qkv_fx"], sv["o_fx"], do_fx, n_fx, HEAD_DIM,
                                                  sv["ccol"], sv["crow"], sv["lse_fx"])
    dq_mem, dk_mem, dv_mem = _attn_bwd("mem", *sv["qkv_mem"], sv["o_mem"], do_mem, n_mem, MEM_HEAD_DIM,
                                       lse=sv["lse_mem"])
    dkv = jnp.concatenate([dk_mem, dv_mem], axis=1)
    grads["w_mem_kv"] = _matmul(mem_n, dkv, ta=True, name="dw_mem_kv")
    dmem_n = _matmul(dkv, wl["w_mem_kv"], tb=True, out_dtype=F32, name="dmem_n")

    dc = jnp.pad((dcrow.reshape(n_fx, S) + dccol.reshape(n_fx, S)).T, ((0, 0), (0, LANE - n_fx)))
    dfl, db_forget = _fox_dlogit(dc, sv["fl"])
    grads["b_forget"] = db_forget
    dproj = jnp.concatenate([dq_sb, dk_sb, dv_sb, dq_fx, dk_fx, dv_fx, dq_mem, dfl], axis=1)
    dh1, dg_pre = _mix_bwd_in(dproj, dgp, wl["w_in"], wl["w_gate"], sv["h1"], wl["mix_pre_g"], dh2)
    grads["mix_pre_g"] = dg_pre
    grads["w_in"] = _matmul(sv["u"], dproj, ta=True, name="dw_in")
    grads["w_gate"] = _matmul(sv["u"], dgp, ta=True, name="dw_gate_mix")

    dh0, g1 = _ffn_bwd(dh1, sv, wl, "1", sv["h0"])
    grads.update(g1)
    return dh0, grads, dmem_n


def kernel(x, mem, ffn1_pre_g, ffn1_post_g, ffn1_w_gate, ffn1_w_up, ffn1_w_down, mix_pre_g, mix_post_g, w_in, b_forget, mem_norm_g, w_mem_kv, w_gate, b_gate, w_br_sb, w_br_fox, w_br_mem, w_out, ffn2_pre_g, ffn2_post_g, ffn2_w_gate, ffn2_w_up, ffn2_w_down, loss_target, m_ffn1_pre_g, m_ffn1_post_g, m_ffn1_w_gate, m_ffn1_w_up, m_ffn1_w_down, m_mix_pre_g, m_mix_post_g, m_w_in, m_b_forget, m_mem_norm_g, m_w_mem_kv, m_w_gate, m_b_gate, m_w_br_sb, m_w_br_fox, m_w_br_mem, m_w_out, m_ffn2_pre_g, m_ffn2_post_g, m_ffn2_w_gate, m_ffn2_w_up, m_ffn2_w_down, v_ffn1_pre_g, v_ffn1_post_g, v_ffn1_w_gate, v_ffn1_w_up, v_ffn1_w_down, v_mix_pre_g, v_mix_post_g, v_w_in, v_b_forget, v_mem_norm_g, v_w_mem_kv, v_w_gate, v_b_gate, v_w_br_sb, v_w_br_fox, v_w_br_mem, v_w_out, v_ffn2_pre_g, v_ffn2_post_g, v_ffn2_w_gate, v_ffn2_w_up, v_ffn2_w_down):
    args = dict(locals())
    w = {n: args[n] for n in WEIGHTS}
    m = {n: args["m_" + n] for n in WEIGHTS}
    v = {n: args["v_" + n] for n in WEIGHTS}
    L = w["ffn1_pre_g"].shape[0]
    Lh = L // 2
    D = x.shape[2]
    sbw, fxw, memw = w["w_br_sb"].shape[1], w["w_br_fox"].shape[1], w["w_br_mem"].shape[1]
    n_sb, n_fx, n_mem = sbw // HEAD_DIM, fxw // HEAD_DIM, memw // MEM_HEAD_DIM
    dims = (n_sb, n_fx, n_mem, sbw, fxw, memw)
    qkv_w = 3 * sbw + 3 * fxw
    c_idx = lax.axis_index("c")
    c_arr = c_idx.reshape(1).astype(jnp.int32)
    chip_arr = (2 * lax.axis_index("x") + lax.axis_index("y")).reshape(1).astype(jnp.int32)

    shard_shapes = {n: w[n].shape[1:] for n, _ in BIG}
    layout = _row_layout(shard_shapes, Lh)
    widths = list(layout)
    locs = [jnp.stack([_pack_rows(layout[b], b, {n: w[n][hf * Lh:(hf + 1) * Lh].reshape(-1, b) for n in layout[b][0]},
                                  BF16) for hf in range(2)]) for b in widths]
    gathered = dict(zip(widths, _gather_weights(locs, [_place_own(loc, chip_arr) for loc in locs])))

    def layer_weights(l):
        hf, li = divmod(l, Lh)
        wl = {}
        for n, axis in BIG:
            a, b = shard_shapes[n]
            r0 = layout[b][1][n] + li * a
            shards = gathered[b][:, hf, r0:r0 + a]
            if n.startswith("ffn"):
                wl[n] = shards
            else:
                wl[n] = (shards.transpose(1, 0, 2).reshape(a, N_CHIPS * b) if axis == 2 else
                         shards.reshape(N_CHIPS * a, b))
        wi = wl["w_in"]
        wl["w_in"] = jnp.concatenate([wi[:, :qkv_w], wi[:, qkv_w + n_fx:], wi[:, qkv_w:qkv_w + n_fx],
                                      jnp.zeros((D, LANE - n_fx), BF16)], axis=1)
        for n in SMALL:
            if n != "mem_norm_g":
                wl[n] = w[n][l][None, :]
        wl["b_forget"] = jnp.pad(wl["b_forget"], ((0, 0), (0, LANE - n_fx)))
        return wl

    g_mem = w["mem_norm_g"][None, :]

    mem_n = _mem_norm(mem[0], g_mem)
    h, wls, saved = x[0], [], []
    for l in range(L):
        wls.append(layer_weights(l))
        h, sv = _layer_fwd(h, mem_n, wls[l], dims)
        saved.append(sv)
    dh, loss_tile = _loss_head(h, loss_target[0])
    loss = lax.psum(loss_tile[0, 0], ("x", "y", "c"))
    gl, dmem_n = [None] * L, [None] * L
    for l in reversed(range(L)):
        dh, gl[l], dmem_n[l] = _layer_bwd(dh, mem_n, wls[l], saved[l], dims)
        gi = gl[l]["w_in"]
        gl[l]["w_in"] = jnp.concatenate([gi[:, :qkv_w], gi[:, qkv_w + memw:qkv_w + memw + n_fx],
                                         gi[:, qkv_w:qkv_w + memw]], axis=1)
    grad_x = dh
    g_mem_norm = _mem_norm_bwd(mem[0], g_mem, jnp.stack(dmem_n))

    axis_of = dict(BIG)
    partials = [jnp.stack([jnp.stack([
        _pack_rows(layout[b], b, {n: jnp.concatenate([_slab(gl[hf * Lh + li][n], axis_of[n], j) for li in range(Lh)])
                                  for n in layout[b][0]}, BF16)
        for j in range(N_CHIPS)]) for hf in range(2)]) for b in widths]
    pairs = [_pair_sum(g, sib, c_arr) for g, sib in zip(partials, _pair_exchange(partials))]
    mines = [_chip_sum(p, r, chip_arr) for p, r in zip(pairs, _chip_exchange(pairs))]
    theirs = _pair_swap(mines)

    grad, delta, new_m, new_v = {}, {}, {}, {}
    for n, _ in BIG:
        k = widths.index(shard_shapes[n][1])
        grad[n], delta[n], new_m[n], new_v[n] = _adamw_reduced(
            w[n], m[n], v[n], mines[k], theirs[k], c_arr, layout[widths[k]][1][n], name="adamw_" + n)

    small_local = {n: (g_mem_norm if n == "mem_norm_g" else
                       jnp.concatenate([gl[l][n][:, :n_fx] if n == "b_forget" else gl[l][n] for l in range(L)]))
                   for n in SMALL}
    small_shapes = [small_local[n].shape for n in SMALL]
    small_sum = _unpack_flat(_all_reduce_small(_pack_flat([small_local[n] for n in SMALL], F32, row_block=16)),
                             small_shapes)
    for n, t in zip(SMALL, small_sum):
        shp = w[n].shape
        two_d = (1, shp[0]) if len(shp) == 1 else shp
        grad[n] = t.reshape(shp)
        d_, m_, v_ = _adamw(w[n].reshape(two_d), t.reshape(two_d), m[n].reshape(two_d), v[n].reshape(two_d),
                            name="adamw_" + n)
        delta[n], new_m[n], new_v[n] = d_.reshape(shp), m_.reshape(shp), v_.reshape(shp)

    return (loss, grad_x[None], *[grad[n] for n in WEIGHTS], *[delta[n] for n in WEIGHTS],
            *[new_m[n] for n in WEIGHTS], *[new_v[n] for n in WEIGHTS])
```
